```python
import math
import jax, jax.numpy as jnp
from jax import lax
import numpy as np

D_MODEL = 1024
BATCH = 8
SEQ = 8192
DEPTH = 1

D_FF = 2816
D_CONV = D_MODEL
CONV_WIDTH = 31
HEAD_DIM = 64
N_HEADS = D_MODEL // HEAD_DIM
N_KV_HEADS = 4
GROUP = N_HEADS // N_KV_HEADS
WINDOW = 128
ROPE_THETA = 10000.0
EPS = 1e-6
LN_EPS = 1e-5
NEG_INF = -1e30

SPLITS = (D_CONV, D_CONV, N_HEADS * HEAD_DIM, N_KV_HEADS * HEAD_DIM, N_KV_HEADS * HEAD_DIM, D_MODEL, D_MODEL)
D_IN = sum(SPLITS)

kernel_name = "hybrid_macaron_conv_swa_gated_block"


def rmsnorm(x, g):
    xf = x.astype(jnp.float32)
    y = xf * lax.rsqrt(jnp.mean(xf * xf, axis=-1, keepdims=True) + EPS)
    return (y * g.astype(jnp.float32)).astype(x.dtype)


def layernorm(x, g, b):
    xf = x.astype(jnp.float32)
    mu = jnp.mean(xf, axis=-1, keepdims=True)
    var = jnp.mean(jnp.square(xf - mu), axis=-1, keepdims=True)
    y = (xf - mu) * lax.rsqrt(var + LN_EPS)
    return (y * g.astype(jnp.float32) + b.astype(jnp.float32)).astype(x.dtype)


def swiglu(x, w_gate, w_up, w_down):
    return (jax.nn.silu(x @ w_gate) * (x @ w_up)) @ w_down


def rope(x, positions):
    half = HEAD_DIM // 2
    inv_freq = ROPE_THETA ** (-jnp.arange(half, dtype=jnp.float32) / half)
    ang = positions.astype(jnp.float32)[..., None] * inv_freq
    cos = jnp.cos(ang)[:, :, None, :]
    sin = jnp.sin(ang)[:, :, None, :]
    xf = x.astype(jnp.float32)
    x1, x2 = xf[..., :half], xf[..., half:]
    out = jnp.concatenate([x1 * cos - x2 * sin, x2 * cos + x1 * sin], axis=-1)
    return out.astype(x.dtype)


def causal_depthwise_conv(u, w, b):
    out = lax.conv_general_dilated(
        u, w[:, None, :].astype(u.dtype), window_strides=(1,),
        padding=((CONV_WIDTH - 1, 0),),
        dimension_numbers=("NWC", "WIO", "NWC"),
        feature_group_count=u.shape[-1])
    return out + b.astype(u.dtype)


def conformer_conv_branch(glu_a, glu_b, dw_w, dw_b, ln_g, ln_b, w_proj):
    u = glu_a * jax.nn.sigmoid(glu_b)
    u = causal_depthwise_conv(u, dw_w, dw_b)
    u = jax.nn.silu(layernorm(u, ln_g, ln_b))
    return u @ w_proj


def band(t):
    B, S = t.shape[:2]
    nb = S // WINDOW
    tb = t.reshape(B, nb, WINDOW, N_KV_HEADS, HEAD_DIM)
    prev = jnp.pad(tb[:, :-1], ((0, 0), (1, 0), (0, 0), (0, 0), (0, 0)))
    return jnp.concatenate([prev, tb], axis=2)


def sliding_window_gqa_sinks(q, k, v, sinks):
    B, S = q.shape[:2]
    nb = S // WINDOW
    qb = q.reshape(B, nb, WINDOW, N_KV_HEADS, GROUP, HEAD_DIM)
    kb, vb = band(k), band(v)
    scores = jnp.einsum("bnqkgd,bnskd->bnkgqs", qb, kb).astype(jnp.float32) * (HEAD_DIM ** -0.5)
    qi = jnp.arange(WINDOW)[:, None]
    sj = jnp.arange(2 * WINDOW)[None, :] - WINDOW
    rel = qi - sj
    allowed = (rel >= 0) & (rel < WINDOW)
    blk = jnp.arange(nb)[:, None, None]
    allowed = allowed[None] & ((blk > 0) | (sj[None] >= 0))
    scores = jnp.where(allowed[None, :, None, None], scores, NEG_INF)
    sink = sinks.astype(jnp.float32).reshape(N_KV_HEADS, GROUP)[None, None, :, :, None, None]
    m = jnp.maximum(jnp.max(scores, axis=-1, keepdims=True), sink)
    p = jnp.exp(scores - m)
    probs = p / (jnp.sum(p, axis=-1, keepdims=True) + jnp.exp(sink - m))
    out = jnp.einsum("bnkgqs,bnskd->bnqkgd", probs.astype(v.dtype), vb)
    return out.reshape(B, S, N_HEADS * HEAD_DIM)


def _fwd_setup_inputs(seed: int = 0) -> dict:
    key = jax.random.key(seed)
    ks = jax.random.split(key, 24)
    f32 = jnp.float32

    def w(k, shape, fan_in):
        return jax.random.normal(k, shape, f32) * (fan_in ** -0.5)

    def gain(k, n):
        return 1.0 + 0.02 * jax.random.normal(k, (DEPTH, n), f32)

    def small(k, shape):
        return 0.02 * jax.random.normal(k, shape, f32)

    x = jax.random.normal(ks[0], (BATCH, SEQ, D_MODEL), f32)
    positions = jnp.broadcast_to(jnp.arange(SEQ, dtype=jnp.int32)[None, :], (BATCH, SEQ))
    return {
        "x": x,
        "positions": positions,
        "ffn1_norm": gain(ks[1], D_MODEL),
        "ffn1_w_gate": w(ks[2], (DEPTH, D_MODEL, D_FF), D_MODEL),
        "ffn1_w_up": w(ks[3], (DEPTH, D_MODEL, D_FF), D_MODEL),
        "ffn1_w_down": w(ks[4], (DEPTH, D_FF, D_MODEL), D_FF),
        "mix_norm": gain(ks[5], D_MODEL),
        "w_in": w(ks[6], (DEPTH, D_MODEL, D_IN), D_MODEL),
        "conv_dw_w": w(ks[7], (DEPTH, CONV_WIDTH, D_CONV), CONV_WIDTH),
        "conv_dw_b": small(ks[8], (DEPTH, D_CONV)),
        "conv_ln_g": gain(ks[9], D_CONV),
        "conv_ln_b": small(ks[10], (DEPTH, D_CONV)),
        "conv_w_proj": w(ks[11], (DEPTH, D_CONV, D_MODEL), D_CONV),
        "attn_sinks": 0.5 * jax.random.normal(ks[12], (DEPTH, N_HEADS), f32),
        "attn_w_o": w(ks[13], (DEPTH, N_HEADS * HEAD_DIM, D_MODEL), N_HEADS * HEAD_DIM),
        "gate_b": small(ks[14], (DEPTH, 2 * D_MODEL)),
        "w_out": w(ks[15], (DEPTH, D_MODEL, D_MODEL), D_MODEL),
        "ffn2_norm": gain(ks[16], D_MODEL),
        "ffn2_w_gate": w(ks[17], (DEPTH, D_MODEL, D_FF), D_MODEL),
        "ffn2_w_up": w(ks[18], (DEPTH, D_MODEL, D_FF), D_MODEL),
        "ffn2_w_down": w(ks[19], (DEPTH, D_FF, D_MODEL), D_FF),
        "final_norm": 1.0 + 0.02 * jax.random.normal(ks[20], (D_MODEL,), f32),
    }


def _fwd_reference(x, positions, ffn1_norm, ffn1_w_gate, ffn1_w_up, ffn1_w_down, mix_norm, w_in,
              conv_dw_w, conv_dw_b, conv_ln_g, conv_ln_b, conv_w_proj, attn_sinks, attn_w_o,
              gate_b, w_out, ffn2_norm, ffn2_w_gate, ffn2_w_up, ffn2_w_down, final_norm):
    B, S, _ = x.shape
    bounds = np.cumsum(SPLITS)[:-1].tolist()
    for l in range(DEPTH):
        x = x + 0.5 * swiglu(rmsnorm(x, ffn1_norm[l]), ffn1_w_gate[l], ffn1_w_up[l], ffn1_w_down[l])

        h = rmsnorm(x, mix_norm[l])
        proj = h @ w_in[l]
        glu_a, glu_b, q, k, v, g_conv, g_attn = jnp.split(proj, bounds, axis=-1)

        conv_out = conformer_conv_branch(glu_a, glu_b, conv_dw_w[l], conv_dw_b[l],
                                         conv_ln_g[l], conv_ln_b[l], conv_w_proj[l])

        q = rope(q.reshape(B, S, N_HEADS, HEAD_DIM), positions)
        k = rope(k.reshape(B, S, N_KV_HEADS, HEAD_DIM), positions)
        v = v.reshape(B, S, N_KV_HEADS, HEAD_DIM)
        attn_out = sliding_window_gqa_sinks(q, k, v, attn_sinks[l]) @ attn_w_o[l]

        gb_conv, gb_attn = jnp.split(gate_b[l], 2)
        merged = (jax.nn.sigmoid(g_conv + gb_conv) * conv_out
                  + jax.nn.sigmoid(g_attn + gb_attn) * attn_out)
        x = x + merged @ w_out[l]

        x = x + 0.5 * swiglu(rmsnorm(x, ffn2_norm[l]), ffn2_w_gate[l], ffn2_w_up[l], ffn2_w_down[l])
    return rmsnorm(x, final_norm)


import jax as _jax
import jax.numpy as _jnp

TWIN_FORMAT = 'train_step'
FWD_PARAMS = ['x', 'positions', 'ffn1_norm', 'ffn1_w_gate', 'ffn1_w_up', 'ffn1_w_down', 'mix_norm', 'w_in', 'conv_dw_w', 'conv_dw_b', 'conv_ln_g', 'conv_ln_b', 'conv_w_proj', 'attn_sinks', 'attn_w_o', 'gate_b', 'w_out', 'ffn2_norm', 'ffn2_w_gate', 'ffn2_w_up', 'ffn2_w_down', 'final_norm']
TWIN_WEIGHTS = ['ffn1_norm', 'ffn1_w_gate', 'ffn1_w_up', 'ffn1_w_down', 'mix_norm', 'w_in', 'conv_dw_w', 'conv_dw_b', 'conv_ln_g', 'conv_ln_b', 'conv_w_proj', 'attn_sinks', 'attn_w_o', 'gate_b', 'w_out', 'ffn2_norm', 'ffn2_w_gate', 'ffn2_w_up', 'ffn2_w_down', 'final_norm']
TWIN_DIFF_INPUT = 'x'
TWIN_INPUTS = ['x', 'positions', 'ffn1_norm', 'ffn1_w_gate', 'ffn1_w_up', 'ffn1_w_down', 'mix_norm', 'w_in', 'conv_dw_w', 'conv_dw_b', 'conv_ln_g', 'conv_ln_b', 'conv_w_proj', 'attn_sinks', 'attn_w_o', 'gate_b', 'w_out', 'ffn2_norm', 'ffn2_w_gate', 'ffn2_w_up', 'ffn2_w_down', 'final_norm', 'loss_target', 'm_ffn1_norm', 'm_ffn1_w_gate', 'm_ffn1_w_up', 'm_ffn1_w_down', 'm_mix_norm', 'm_w_in', 'm_conv_dw_w', 'm_conv_dw_b', 'm_conv_ln_g', 'm_conv_ln_b', 'm_conv_w_proj', 'm_attn_sinks', 'm_attn_w_o', 'm_gate_b', 'm_w_out', 'm_ffn2_norm', 'm_ffn2_w_gate', 'm_ffn2_w_up', 'm_ffn2_w_down', 'm_final_norm', 'v_ffn1_norm', 'v_ffn1_w_gate', 'v_ffn1_w_up', 'v_ffn1_w_down', 'v_mix_norm', 'v_w_in', 'v_conv_dw_w', 'v_conv_dw_b', 'v_conv_ln_g', 'v_conv_ln_b', 'v_conv_w_proj', 'v_attn_sinks', 'v_attn_w_o', 'v_gate_b', 'v_w_out', 'v_ffn2_norm', 'v_ffn2_w_gate', 'v_ffn2_w_up', 'v_ffn2_w_down', 'v_final_norm']
TWIN_OUTPUTS = ['loss', 'grad_x', 'grad_ffn1_norm', 'grad_ffn1_w_gate', 'grad_ffn1_w_up', 'grad_ffn1_w_down', 'grad_mix_norm', 'grad_w_in', 'grad_conv_dw_w', 'grad_conv_dw_b', 'grad_conv_ln_g', 'grad_conv_ln_b', 'grad_conv_w_proj', 'grad_attn_sinks', 'grad_attn_w_o', 'grad_gate_b', 'grad_w_out', 'grad_ffn2_norm', 'grad_ffn2_w_gate', 'grad_ffn2_w_up', 'grad_ffn2_w_down', 'grad_final_norm', 'delta_ffn1_norm', 'delta_ffn1_w_gate', 'delta_ffn1_w_up', 'delta_ffn1_w_down', 'delta_mix_norm', 'delta_w_in', 'delta_conv_dw_w', 'delta_conv_dw_b', 'delta_conv_ln_g', 'delta_conv_ln_b', 'delta_conv_w_proj', 'delta_attn_sinks', 'delta_attn_w_o', 'delta_gate_b', 'delta_w_out', 'delta_ffn2_norm', 'delta_ffn2_w_gate', 'delta_ffn2_w_up', 'delta_ffn2_w_down', 'delta_final_norm', 'new_m_ffn1_norm', 'new_m_ffn1_w_gate', 'new_m_ffn1_w_up', 'new_m_ffn1_w_down', 'new_m_mix_norm', 'new_m_w_in', 'new_m_conv_dw_w', 'new_m_conv_dw_b', 'new_m_conv_ln_g', 'new_m_conv_ln_b', 'new_m_conv_w_proj', 'new_m_attn_sinks', 'new_m_attn_w_o', 'new_m_gate_b', 'new_m_w_out', 'new_m_ffn2_norm', 'new_m_ffn2_w_gate', 'new_m_ffn2_w_up', 'new_m_ffn2_w_down', 'new_m_final_norm', 'new_v_ffn1_norm', 'new_v_ffn1_w_gate', 'new_v_ffn1_w_up', 'new_v_ffn1_w_down', 'new_v_mix_norm', 'new_v_w_in', 'new_v_conv_dw_w', 'new_v_conv_dw_b', 'new_v_conv_ln_g', 'new_v_conv_ln_b', 'new_v_conv_w_proj', 'new_v_attn_sinks', 'new_v_attn_w_o', 'new_v_gate_b', 'new_v_w_out', 'new_v_ffn2_norm', 'new_v_ffn2_w_gate', 'new_v_ffn2_w_up', 'new_v_ffn2_w_down', 'new_v_final_norm']
TWIN_LEAF_KINDS = {'loss': 'loss', 'grad_x': 'grad_x', 'grad_ffn1_norm': 'grad_w', 'grad_ffn1_w_gate': 'grad_w', 'grad_ffn1_w_up': 'grad_w', 'grad_ffn1_w_down': 'grad_w', 'grad_mix_norm': 'grad_w', 'grad_w_in': 'grad_w', 'grad_conv_dw_w': 'grad_w', 'grad_conv_dw_b': 'grad_w', 'grad_conv_ln_g': 'grad_w', 'grad_conv_ln_b': 'grad_w', 'grad_conv_w_proj': 'grad_w', 'grad_attn_sinks': 'grad_w', 'grad_attn_w_o': 'grad_w', 'grad_gate_b': 'grad_w', 'grad_w_out': 'grad_w', 'grad_ffn2_norm': 'grad_w', 'grad_ffn2_w_gate': 'grad_w', 'grad_ffn2_w_up': 'grad_w', 'grad_ffn2_w_down': 'grad_w', 'grad_final_norm': 'grad_w', 'delta_ffn1_norm': 'delta_w', 'delta_ffn1_w_gate': 'delta_w', 'delta_ffn1_w_up': 'delta_w', 'delta_ffn1_w_down': 'delta_w', 'delta_mix_norm': 'delta_w', 'delta_w_in': 'delta_w', 'delta_conv_dw_w': 'delta_w', 'delta_conv_dw_b': 'delta_w', 'delta_conv_ln_g': 'delta_w', 'delta_conv_ln_b': 'delta_w', 'delta_conv_w_proj': 'delta_w', 'delta_attn_sinks': 'delta_w', 'delta_attn_w_o': 'delta_w', 'delta_gate_b': 'delta_w', 'delta_w_out': 'delta_w', 'delta_ffn2_norm': 'delta_w', 'delta_ffn2_w_gate': 'delta_w', 'delta_ffn2_w_up': 'delta_w', 'delta_ffn2_w_down': 'delta_w', 'delta_final_norm': 'delta_w', 'new_m_ffn1_norm': 'new_m', 'new_m_ffn1_w_gate': 'new_m', 'new_m_ffn1_w_up': 'new_m', 'new_m_ffn1_w_down': 'new_m', 'new_m_mix_norm': 'new_m', 'new_m_w_in': 'new_m', 'new_m_conv_dw_w': 'new_m', 'new_m_conv_dw_b': 'new_m', 'new_m_conv_ln_g': 'new_m', 'new_m_conv_ln_b': 'new_m', 'new_m_conv_w_proj': 'new_m', 'new_m_attn_sinks': 'new_m', 'new_m_attn_w_o': 'new_m', 'new_m_gate_b': 'new_m', 'new_m_w_out': 'new_m', 'new_m_ffn2_norm': 'new_m', 'new_m_ffn2_w_gate': 'new_m', 'new_m_ffn2_w_up': 'new_m', 'new_m_ffn2_w_down': 'new_m', 'new_m_final_norm': 'new_m', 'new_v_ffn1_norm': 'new_v', 'new_v_ffn1_w_gate': 'new_v', 'new_v_ffn1_w_up': 'new_v', 'new_v_ffn1_w_down': 'new_v', 'new_v_mix_norm': 'new_v', 'new_v_w_in': 'new_v', 'new_v_conv_dw_w': 'new_v', 'new_v_conv_dw_b': 'new_v', 'new_v_conv_ln_g': 'new_v', 'new_v_conv_ln_b': 'new_v', 'new_v_conv_w_proj': 'new_v', 'new_v_attn_sinks': 'new_v', 'new_v_attn_w_o': 'new_v', 'new_v_gate_b': 'new_v', 'new_v_w_out': 'new_v', 'new_v_ffn2_norm': 'new_v', 'new_v_ffn2_w_gate': 'new_v', 'new_v_ffn2_w_up': 'new_v', 'new_v_ffn2_w_down': 'new_v', 'new_v_final_norm': 'new_v'}


def _forward(args):
    return _fwd_reference(*[args[k] for k in FWD_PARAMS])


def _output_shape():
    def fwd():
        inp = _fwd_setup_inputs(0)
        return _fwd_reference(*[inp[k] for k in FWD_PARAMS])
    out = _jax.eval_shape(fwd)
    return out.shape, out.dtype

N_MICROBATCH = 1
ADAM_LR = 0.001
ADAM_B1 = 0.9
ADAM_B2 = 0.999
ADAM_EPS = 1e-08
ADAM_WD = 0.01
ADAM_STEP = 10
PER_EXAMPLE_BATCH_AXIS = {'x': 0, 'positions': 0, 'loss_target': 0}
SHARED_INPUTS = []
_WEIGHT_DTYPES = {'ffn1_norm': _jnp.float32, 'ffn1_w_gate': _jnp.float32, 'ffn1_w_up': _jnp.float32, 'ffn1_w_down': _jnp.float32, 'mix_norm': _jnp.float32, 'w_in': _jnp.float32, 'conv_dw_w': _jnp.float32, 'conv_dw_b': _jnp.float32, 'conv_ln_g': _jnp.float32, 'conv_ln_b': _jnp.float32, 'conv_w_proj': _jnp.float32, 'attn_sinks': _jnp.float32, 'attn_w_o': _jnp.float32, 'gate_b': _jnp.float32, 'w_out': _jnp.float32, 'ffn2_norm': _jnp.float32, 'ffn2_w_gate': _jnp.float32, 'ffn2_w_up': _jnp.float32, 'ffn2_w_down': _jnp.float32, 'final_norm': _jnp.float32}
MOMENT_SCALE = {'ffn1_norm': 1.145225e-01, 'ffn1_w_gate': 4.736373e-02, 'ffn1_w_up': 4.586480e-02, 'ffn1_w_down': 7.598116e-02, 'mix_norm': 1.018180e-01, 'w_in': 4.268219e-02, 'conv_dw_w': 8.010716e-02, 'conv_dw_b': 1.630255e-01, 'conv_ln_g': 9.441915e-02, 'conv_ln_b': 8.360346e-02, 'conv_w_proj': 7.855340e-02, 'attn_sinks': 2.049088e-02, 'attn_w_o': 2.631029e-02, 'gate_b': 2.400084e-02, 'w_out': 8.129508e-02, 'ffn2_norm': 9.341081e-02, 'ffn2_w_gate': 4.123535e-02, 'ffn2_w_up': 3.995016e-02, 'ffn2_w_down': 6.628318e-02, 'final_norm': 6.405446e+01}


def _to_microbatches(a, axis):
    t = _jnp.moveaxis(a, axis, 0)
    t = t.reshape((N_MICROBATCH, t.shape[0] // N_MICROBATCH) + t.shape[1:])
    return _jnp.moveaxis(t, 1, axis + 1)


def setup_inputs(seed: int = 0) -> dict:
    inp = _fwd_setup_inputs(seed)
    key = _jax.random.fold_in(_jax.random.key(seed), 7919)
    shape, _ = _output_shape()
    out = dict(inp)
    out["loss_target"] = _jax.random.normal(_jax.random.fold_in(key, 0), shape, _jnp.float32)
    for i, name in enumerate(TWIN_WEIGHTS):
        w = inp[name].astype(_jnp.float32)
        if MOMENT_SCALE is None:
            s = _jnp.sqrt(_jnp.mean(_jnp.square(w)) + 1e-30)
        else:
            s = MOMENT_SCALE[name]
        km, kv = _jax.random.split(_jax.random.fold_in(key, i + 1))
        out[name] = w
        out["m_" + name] = s * _jax.random.normal(km, w.shape, _jnp.float32)
        out["v_" + name] = (s * s) * _jax.random.uniform(kv, w.shape, _jnp.float32, 0.5, 1.5)
    if N_MICROBATCH > 1:
        for name, axis in PER_EXAMPLE_BATCH_AXIS.items():
            out[name] = _to_microbatches(out[name], axis)
    return {'x': out['x'], 'positions': out['positions'], 'ffn1_norm': out['ffn1_norm'], 'ffn1_w_gate': out['ffn1_w_gate'], 'ffn1_w_up': out['ffn1_w_up'], 'ffn1_w_down': out['ffn1_w_down'], 'mix_norm': out['mix_norm'], 'w_in': out['w_in'], 'conv_dw_w': out['conv_dw_w'], 'conv_dw_b': out['conv_dw_b'], 'conv_ln_g': out['conv_ln_g'], 'conv_ln_b': out['conv_ln_b'], 'conv_w_proj': out['conv_w_proj'], 'attn_sinks': out['attn_sinks'], 'attn_w_o': out['attn_w_o'], 'gate_b': out['gate_b'], 'w_out': out['w_out'], 'ffn2_norm': out['ffn2_norm'], 'ffn2_w_gate': out['ffn2_w_gate'], 'ffn2_w_up': out['ffn2_w_up'], 'ffn2_w_down': out['ffn2_w_down'], 'final_norm': out['final_norm'], 'loss_target': out['loss_target'], 'm_ffn1_norm': out['m_ffn1_norm'], 'm_ffn1_w_gate': out['m_ffn1_w_gate'], 'm_ffn1_w_up': out['m_ffn1_w_up'], 'm_ffn1_w_down': out['m_ffn1_w_down'], 'm_mix_norm': out['m_mix_norm'], 'm_w_in': out['m_w_in'], 'm_conv_dw_w': out['m_conv_dw_w'], 'm_conv_dw_b': out['m_conv_dw_b'], 'm_conv_ln_g': out['m_conv_ln_g'], 'm_conv_ln_b': out['m_conv_ln_b'], 'm_conv_w_proj': out['m_conv_w_proj'], 'm_attn_sinks': out['m_attn_sinks'], 'm_attn_w_o': out['m_attn_w_o'], 'm_gate_b': out['m_gate_b'], 'm_w_out': out['m_w_out'], 'm_ffn2_norm': out['m_ffn2_norm'], 'm_ffn2_w_gate': out['m_ffn2_w_gate'], 'm_ffn2_w_up': out['m_ffn2_w_up'], 'm_ffn2_w_down': out['m_ffn2_w_down'], 'm_final_norm': out['m_final_norm'], 'v_ffn1_norm': out['v_ffn1_norm'], 'v_ffn1_w_gate': out['v_ffn1_w_gate'], 'v_ffn1_w_up': out['v_ffn1_w_up'], 'v_ffn1_w_down': out['v_ffn1_w_down'], 'v_mix_norm': out['v_mix_norm'], 'v_w_in': out['v_w_in'], 'v_conv_dw_w': out['v_conv_dw_w'], 'v_conv_dw_b': out['v_conv_dw_b'], 'v_conv_ln_g': out['v_conv_ln_g'], 'v_conv_ln_b': out['v_conv_ln_b'], 'v_conv_w_proj': out['v_conv_w_proj'], 'v_attn_sinks': out['v_attn_sinks'], 'v_attn_w_o': out['v_attn_w_o'], 'v_gate_b': out['v_gate_b'], 'v_w_out': out['v_w_out'], 'v_ffn2_norm': out['v_ffn2_norm'], 'v_ffn2_w_gate': out['v_ffn2_w_gate'], 'v_ffn2_w_up': out['v_ffn2_w_up'], 'v_ffn2_w_down': out['v_ffn2_w_down'], 'v_final_norm': out['v_final_norm']}


def _loss(weights, diff, rest, loss_target):
    with _jax.named_scope("forward"):
        args = {**rest, TWIN_DIFF_INPUT: diff, **{k: w.astype(_WEIGHT_DTYPES[k]) for k, w in weights.items()}}
        y = _forward(args)
    with _jax.named_scope("loss_head"):
        err = _jnp.square(y.astype(_jnp.float32) - loss_target)
        return 0.5 * _jnp.sum(_jnp.mean(err, axis=-1)) if err.ndim else 0.5 * err


def _adamw(w, g, m, v):
    m = ADAM_B1 * m + (1.0 - ADAM_B1) * g
    v = ADAM_B2 * v + (1.0 - ADAM_B2) * _jnp.square(g)
    m_hat = m / (1.0 - ADAM_B1 ** ADAM_STEP)
    v_hat = v / (1.0 - ADAM_B2 ** ADAM_STEP)
    delta = -ADAM_LR * (m_hat / (_jnp.sqrt(v_hat) + ADAM_EPS) + ADAM_WD * w)
    return delta, m, v


def reference(x, positions, ffn1_norm, ffn1_w_gate, ffn1_w_up, ffn1_w_down, mix_norm, w_in, conv_dw_w, conv_dw_b, conv_ln_g, conv_ln_b, conv_w_proj, attn_sinks, attn_w_o, gate_b, w_out, ffn2_norm, ffn2_w_gate, ffn2_w_up, ffn2_w_down, final_norm, loss_target, m_ffn1_norm, m_ffn1_w_gate, m_ffn1_w_up, m_ffn1_w_down, m_mix_norm, m_w_in, m_conv_dw_w, m_conv_dw_b, m_conv_ln_g, m_conv_ln_b, m_conv_w_proj, m_attn_sinks, m_attn_w_o, m_gate_b, m_w_out, m_ffn2_norm, m_ffn2_w_gate, m_ffn2_w_up, m_ffn2_w_down, m_final_norm, v_ffn1_norm, v_ffn1_w_gate, v_ffn1_w_up, v_ffn1_w_down, v_mix_norm, v_w_in, v_conv_dw_w, v_conv_dw_b, v_conv_ln_g, v_conv_ln_b, v_conv_w_proj, v_attn_sinks, v_attn_w_o, v_gate_b, v_w_out, v_ffn2_norm, v_ffn2_w_gate, v_ffn2_w_up, v_ffn2_w_down, v_final_norm):
    given = dict(x=x, positions=positions, ffn1_norm=ffn1_norm, ffn1_w_gate=ffn1_w_gate, ffn1_w_up=ffn1_w_up, ffn1_w_down=ffn1_w_down, mix_norm=mix_norm, w_in=w_in, conv_dw_w=conv_dw_w, conv_dw_b=conv_dw_b, conv_ln_g=conv_ln_g, conv_ln_b=conv_ln_b, conv_w_proj=conv_w_proj, attn_sinks=attn_sinks, attn_w_o=attn_w_o, gate_b=gate_b, w_out=w_out, ffn2_norm=ffn2_norm, ffn2_w_gate=ffn2_w_gate, ffn2_w_up=ffn2_w_up, ffn2_w_down=ffn2_w_down, final_norm=final_norm, loss_target=loss_target, m_ffn1_norm=m_ffn1_norm, m_ffn1_w_gate=m_ffn1_w_gate, m_ffn1_w_up=m_ffn1_w_up, m_ffn1_w_down=m_ffn1_w_down, m_mix_norm=m_mix_norm, m_w_in=m_w_in, m_conv_dw_w=m_conv_dw_w, m_conv_dw_b=m_conv_dw_b, m_conv_ln_g=m_conv_ln_g, m_conv_ln_b=m_conv_ln_b, m_conv_w_proj=m_conv_w_proj, m_attn_sinks=m_attn_sinks, m_attn_w_o=m_attn_w_o, m_gate_b=m_gate_b, m_w_out=m_w_out, m_ffn2_norm=m_ffn2_norm, m_ffn2_w_gate=m_ffn2_w_gate, m_ffn2_w_up=m_ffn2_w_up, m_ffn2_w_down=m_ffn2_w_down, m_final_norm=m_final_norm, v_ffn1_norm=v_ffn1_norm, v_ffn1_w_gate=v_ffn1_w_gate, v_ffn1_w_up=v_ffn1_w_up, v_ffn1_w_down=v_ffn1_w_down, v_mix_norm=v_mix_norm, v_w_in=v_w_in, v_conv_dw_w=v_conv_dw_w, v_conv_dw_b=v_conv_dw_b, v_conv_ln_g=v_conv_ln_g, v_conv_ln_b=v_conv_ln_b, v_conv_w_proj=v_conv_w_proj, v_attn_sinks=v_attn_sinks, v_attn_w_o=v_attn_w_o, v_gate_b=v_gate_b, v_w_out=v_w_out, v_ffn2_norm=v_ffn2_norm, v_ffn2_w_gate=v_ffn2_w_gate, v_ffn2_w_up=v_ffn2_w_up, v_ffn2_w_down=v_ffn2_w_down, v_final_norm=v_final_norm)
    weights = {n: given[n] for n in TWIN_WEIGHTS}
    shared = {n: given[n] for n in SHARED_INPUTS}
    per_example = {n: given[n] for n in ['x', 'positions']}
    grad_fn = _jax.value_and_grad(_loss, argnums=(0, 1))

    def one_microbatch(ex, loss_target):
        ex = dict(ex)
        diff = ex.pop(TWIN_DIFF_INPUT)
        return grad_fn(weights, diff, {**shared, **ex}, loss_target)

    if N_MICROBATCH == 1:
        loss, (grad_w, grad_x) = one_microbatch(per_example, given["loss_target"])
    else:
        def body(carry, xs):
            loss_sum, grad_sum = carry
            l_k, (gw_k, gx_k) = one_microbatch(xs[0], xs[1])
            with _jax.named_scope("update"):
                return (loss_sum + l_k, _jax.tree.map(_jnp.add, grad_sum, gw_k)), gx_k

        init = (_jnp.zeros((), _jnp.float32), _jax.tree.map(_jnp.zeros_like, weights))
        (loss, grad_w), grad_x = _jax.lax.scan(body, init, (per_example, given["loss_target"]))
    with _jax.named_scope("update"):
        delta_w, new_m, new_v = {}, {}, {}
        for n in TWIN_WEIGHTS:
            delta_w[n], new_m[n], new_v[n] = _adamw(weights[n], grad_w[n], given["m_" + n], given["v_" + n])
    return (loss, grad_x, *[grad_w[n] for n in TWIN_WEIGHTS], *[delta_w[n] for n in TWIN_WEIGHTS],
            *[new_m[n] for n in TWIN_WEIGHTS], *[new_v[n] for n in TWIN_WEIGHTS])
```

```python
import functools

import jax
import jax.numpy as jnp
from jax import lax
from jax.experimental import pallas as pl
from jax.experimental.pallas import tpu as pltpu

F32 = jnp.float32
BF16 = jnp.bfloat16
MESH = pl.DeviceIdType.MESH

HEAD_DIM = 64
WINDOW = 128
CONV_WIDTH = 31
CONV_HALO = 32
ROPE_THETA = 10000.0
EPS = 1e-6
LN_EPS = 1e-5
NEG_INF = -1e30
N_CHIPS = 4
N_DEV = 8

ADAM_LR = 0.001
ADAM_B1 = 0.9
ADAM_B2 = 0.999
ADAM_EPS = 1e-08
ADAM_WD = 0.01
ADAM_STEP = 10

TM_FFN = 512
TM_MM = 512
TM_ROW = 256
TK_TN = 512
TR_ELT = 256
VMEM_LIMIT = 56 * 1024 * 1024

NT_DIMS = (((1,), (1,)), ((), ()))
TN_DIMS = (((0,), (0,)), ((), ()))


def _row_tile(rows, cap):
    for t in range(min(cap, rows), 15, -1):
        if rows % t == 0 and t % 16 == 0:
            return t
    return rows


def _params(sem):
    return pltpu.CompilerParams(dimension_semantics=sem, vmem_limit_bytes=VMEM_LIMIT)


def _dot(a, b):
    return jnp.dot(a, b, preferred_element_type=F32)


def _dot_nt(a, b):
    return lax.dot_general(a, b, NT_DIMS, preferred_element_type=F32)


def _dot_tn(a, b):
    return lax.dot_general(a, b, TN_DIMS, preferred_element_type=F32)


def _sigmoid(x):
    return jax.nn.sigmoid(x)


def _rms_scale(xv):
    return lax.rsqrt(jnp.mean(xv * xv, axis=-1, keepdims=True) + EPS)


def _rms_bwd(xv, nw, dh):
    r = _rms_scale(xv)
    dn = dh * nw
    dx = r * dn - xv * (r * r * r) * jnp.mean(dn * xv, axis=-1, keepdims=True)
    dnw = jnp.sum(dh * (xv * r), axis=0, keepdims=True)
    return dx, dnw


def _silu_grad(z, s):
    return s * (1.0 + z * (1.0 - s))


def ffn_fwd(x, nw, wg, wu, wd, name):
    T, D = x.shape
    NP, _, Fs = wg.shape
    tm = min(TM_FFN, T)

    def body(x_ref, nw_ref, wg_ref, wu_ref, wd_ref, xo_ref, h_ref, g_ref, u_ref, acc_ref):
        j = pl.program_id(1)

        @pl.when(j == 0)
        def _():
            xv = x_ref[...]
            h_ref[...] = (xv * _rms_scale(xv) * nw_ref[...]).astype(BF16)
            acc_ref[...] = jnp.zeros_like(acc_ref)

        h = h_ref[...]
        g = _dot(h, wg_ref[...])
        u = _dot(h, wu_ref[...])
        a = (g * _sigmoid(g)) * u
        g_ref[...] = g.astype(BF16)
        u_ref[...] = u.astype(BF16)
        acc_ref[...] += _dot(a.astype(BF16), wd_ref[...])

        @pl.when(j == NP - 1)
        def _():
            xo_ref[...] = x_ref[...] + 0.5 * acc_ref[...]

    return pl.pallas_call(
        body, name=name, grid=(T // tm, NP),
        in_specs=[pl.BlockSpec((tm, D), lambda i, j: (i, 0)),
                  pl.BlockSpec((1, D), lambda i, j: (0, 0)),
                  pl.BlockSpec((None, D, Fs), lambda i, j: (j, 0, 0)),
                  pl.BlockSpec((None, D, Fs), lambda i, j: (j, 0, 0)),
                  pl.BlockSpec((None, Fs, D), lambda i, j: (j, 0, 0))],
        out_specs=[pl.BlockSpec((tm, D), lambda i, j: (i, 0)),
                   pl.BlockSpec((tm, D), lambda i, j: (i, 0)),
                   pl.BlockSpec((None, tm, Fs), lambda i, j: (j, i, 0)),
                   pl.BlockSpec((None, tm, Fs), lambda i, j: (j, i, 0))],
        out_shape=[jax.ShapeDtypeStruct((T, D), F32), jax.ShapeDtypeStruct((T, D), BF16),
                   jax.ShapeDtypeStruct((NP, T, Fs), BF16), jax.ShapeDtypeStruct((NP, T, Fs), BF16)],
        scratch_shapes=[pltpu.VMEM((tm, D), F32)],
        compiler_params=_params(("parallel", "arbitrary")),
    )(x, nw, wg, wu, wd)


def ffn_bwd_x(x, nw, g, u, wg, wu, wd, dout, name):
    T, D = x.shape
    NP, _, Fs = wg.shape
    tm = min(TM_FFN, T)

    def body(x_ref, nw_ref, g_ref, u_ref, wg_ref, wu_ref, wd_ref, do_ref,
             dx_ref, dg_ref, du_ref, dnw_ref, dh_ref, dob_ref):
        i = pl.program_id(0)
        j = pl.program_id(1)

        @pl.when((i == 0) & (j == 0))
        def _():
            dnw_ref[...] = jnp.zeros_like(dnw_ref)

        @pl.when(j == 0)
        def _():
            dh_ref[...] = jnp.zeros_like(dh_ref)
            dob_ref[...] = (0.5 * do_ref[...]).astype(BF16)

        da = _dot_nt(dob_ref[...], wd_ref[...])
        gf = g_ref[...].astype(F32)
        uf = u_ref[...].astype(F32)
        s = _sigmoid(gf)
        dg = (da * uf * _silu_grad(gf, s)).astype(BF16)
        du = (da * (gf * s)).astype(BF16)
        dg_ref[...] = dg
        du_ref[...] = du
        dh_ref[...] += _dot_nt(dg, wg_ref[...]) + _dot_nt(du, wu_ref[...])

        @pl.when(j == NP - 1)
        def _():
            dxn, dnw = _rms_bwd(x_ref[...], nw_ref[...], dh_ref[...])
            dx_ref[...] = do_ref[...] + dxn
            dnw_ref[...] += dnw

    return pl.pallas_call(
        body, name=name, grid=(T // tm, NP),
        in_specs=[pl.BlockSpec((tm, D), lambda i, j: (i, 0)),
                  pl.BlockSpec((1, D), lambda i, j: (0, 0)),
                  pl.BlockSpec((None, tm, Fs), lambda i, j: (j, i, 0)),
                  pl.BlockSpec((None, tm, Fs), lambda i, j: (j, i, 0)),
                  pl.BlockSpec((None, D, Fs), lambda i, j: (j, 0, 0)),
                  pl.BlockSpec((None, D, Fs), lambda i, j: (j, 0, 0)),
                  pl.BlockSpec((None, Fs, D), lambda i, j: (j, 0, 0)),
                  pl.BlockSpec((tm, D), lambda i, j: (i, 0))],
        out_specs=[pl.BlockSpec((tm, D), lambda i, j: (i, 0)),
                   pl.BlockSpec((None, tm, Fs), lambda i, j: (j, i, 0)),
                   pl.BlockSpec((None, tm, Fs), lambda i, j: (j, i, 0)),
                   pl.BlockSpec((1, D), lambda i, j: (0, 0))],
        out_shape=[jax.ShapeDtypeStruct((T, D), F32),
                   jax.ShapeDtypeStruct((NP, T, Fs), BF16), jax.ShapeDtypeStruct((NP, T, Fs), BF16),
                   jax.ShapeDtypeStruct((1, D), F32)],
        scratch_shapes=[pltpu.VMEM((tm, D), F32), pltpu.VMEM((tm, D), BF16)],
        compiler_params=_params(("arbitrary", "arbitrary")),
    )(x, nw, g, u, wg, wu, wd, dout)


def ffn_bwd_w(h, g, u, dg, du, dout, name):
    T, D = h.shape
    NP, _, Fs = g.shape
    tk = min(TK_TN, T)

    def body(h_ref, g_ref, u_ref, dg_ref, du_ref, do_ref, dwg_ref, dwu_ref, dwd_ref):
        t = pl.program_id(1)

        @pl.when(t == 0)
        def _():
            dwg_ref[...] = jnp.zeros_like(dwg_ref)
            dwu_ref[...] = jnp.zeros_like(dwu_ref)
            dwd_ref[...] = jnp.zeros_like(dwd_ref)

        hb = h_ref[...]
        dwg_ref[...] += _dot_tn(hb, dg_ref[...])
        dwu_ref[...] += _dot_tn(hb, du_ref[...])
        gf = g_ref[...].astype(F32)
        a = ((gf * _sigmoid(gf)) * u_ref[...].astype(F32)).astype(BF16)
        dwd_ref[...] += _dot_tn(a, (0.5 * do_ref[...]).astype(BF16))

    piece = pl.BlockSpec((None, tk, Fs), lambda j, t: (j, t, 0))
    return pl.pallas_call(
        body, name=name, grid=(NP, T // tk),
        in_specs=[pl.BlockSpec((tk, D), lambda j, t: (t, 0)), piece, piece, piece, piece,
                  pl.BlockSpec((tk, D), lambda j, t: (t, 0))],
        out_specs=[pl.BlockSpec((None, D, Fs), lambda j, t: (j, 0, 0)),
                   pl.BlockSpec((None, D, Fs), lambda j, t: (j, 0, 0)),
                   pl.BlockSpec((None, Fs, D), lambda j, t: (j, 0, 0))],
        out_shape=[jax.ShapeDtypeStruct((NP, D, Fs), F32), jax.ShapeDtypeStruct((NP, D, Fs), F32),
                   jax.ShapeDtypeStruct((NP, Fs, D), F32)],
        compiler_params=_params(("parallel", "arbitrary")),
    )(h, g, u, dg, du, dout)


def rmsnorm_fwd(x, nw, name):
    T, D = x.shape
    tm = min(TM_MM, T)

    def body(x_ref, nw_ref, h_ref):
        xv = x_ref[...]
        h_ref[...] = (xv * _rms_scale(xv) * nw_ref[...]).astype(BF16)

    return pl.pallas_call(
        body, name=name, grid=(T // tm,),
        in_specs=[pl.BlockSpec((tm, D), lambda i: (i, 0)), pl.BlockSpec((1, D), lambda i: (0, 0))],
        out_specs=pl.BlockSpec((tm, D), lambda i: (i, 0)),
        out_shape=jax.ShapeDtypeStruct((T, D), BF16),
        compiler_params=_params(("parallel",)),
    )(x, nw)


def matmul_nn(a, w, name):
    T, K = a.shape
    N = w.shape[1]
    tm = min(TM_MM, T)

    def body(a_ref, w_ref, o_ref):
        o_ref[...] = _dot(a_ref[...], w_ref[...])

    return pl.pallas_call(
        body, name=name, grid=(T // tm,),
        in_specs=[pl.BlockSpec((tm, K), lambda i: (i, 0)), pl.BlockSpec((K, N), lambda i: (0, 0))],
        out_specs=pl.BlockSpec((tm, N), lambda i: (i, 0)),
        out_shape=jax.ShapeDtypeStruct((T, N), F32),
        compiler_params=_params(("parallel",)),
    )(a, w)


def matmul_tn(lhs, rhs, name):
    T, K = lhs.shape
    N = rhs.shape[1]
    tk = min(TK_TN, T)

    def body(l_ref, r_ref, o_ref):
        @pl.when(pl.program_id(0) == 0)
        def _():
            o_ref[...] = jnp.zeros_like(o_ref)

        o_ref[...] += _dot_tn(l_ref[...].astype(BF16), r_ref[...].astype(BF16))

    return pl.pallas_call(
        body, name=name, grid=(T // tk,),
        in_specs=[pl.BlockSpec((tk, K), lambda t: (t, 0)), pl.BlockSpec((tk, N), lambda t: (t, 0))],
        out_specs=pl.BlockSpec((K, N), lambda t: (0, 0)),
        out_shape=jax.ShapeDtypeStruct((K, N), F32),
        compiler_params=_params(("arbitrary",)),
    )(lhs, rhs)


def mix_in_bwd(dps, ws, x, nw, dres, name):
    T, D = x.shape
    tm = min(TM_ROW, T)
    n = len(dps)

    def body(*refs):
        dp_refs, w_refs = refs[:n], refs[n:2 * n]
        x_ref, nw_ref, dr_ref, dx_ref, dnw_ref = refs[2 * n:]

        @pl.when(pl.program_id(0) == 0)
        def _():
            dnw_ref[...] = jnp.zeros_like(dnw_ref)

        dh = _dot_nt(dp_refs[0][...], w_refs[0][...])
        for k in range(1, n):
            dh += _dot_nt(dp_refs[k][...], w_refs[k][...])
        dxn, dnw = _rms_bwd(x_ref[...], nw_ref[...], dh)
        dx_ref[...] = dr_ref[...] + dxn
        dnw_ref[...] += dnw

    in_specs = [pl.BlockSpec((tm, dp.shape[1]), lambda i: (i, 0)) for dp in dps]
    in_specs += [pl.BlockSpec(w.shape, lambda i: (0, 0)) for w in ws]
    in_specs += [pl.BlockSpec((tm, D), lambda i: (i, 0)), pl.BlockSpec((1, D), lambda i: (0, 0)),
                 pl.BlockSpec((tm, D), lambda i: (i, 0))]
    return pl.pallas_call(
        body, name=name, grid=(T // tm,), in_specs=in_specs,
        out_specs=[pl.BlockSpec((tm, D), lambda i: (i, 0)), pl.BlockSpec((1, D), lambda i: (0, 0))],
        out_shape=[jax.ShapeDtypeStruct((T, D), F32), jax.ShapeDtypeStruct((1, D), F32)],
        compiler_params=_params(("arbitrary",)),
    )(*dps, *ws, x, nw, dres)


def _layernorm_stats(c1):
    mu = jnp.mean(c1, axis=-1, keepdims=True)
    xc = c1 - mu
    rstd = lax.rsqrt(jnp.mean(xc * xc, axis=-1, keepdims=True) + LN_EPS)
    return xc * rstd, rstd


def conv_fwd(p_glu, dw_w, dw_b, ln_g, ln_b, name):
    T, D2 = p_glu.shape
    D = D2 // 2
    tm = min(TM_ROW, T)
    hb = tm // CONV_HALO

    def body(a_ref, b_ref, ah_ref, bh_ref, w_ref, wb_ref, g_ref, be_ref, c1_ref, c3_ref, e_ref):
        i = pl.program_id(0)
        halo = ah_ref[...] * _sigmoid(bh_ref[...])
        e_ref[pl.ds(0, CONV_HALO), :] = jnp.where(i > 0, halo, 0.0)
        e_ref[pl.ds(CONV_HALO, tm), :] = a_ref[...] * _sigmoid(b_ref[...])
        off = CONV_HALO - (CONV_WIDTH - 1)
        acc = jnp.zeros((tm, D), F32) + wb_ref[...]
        for k in range(CONV_WIDTH):
            acc += w_ref[pl.ds(k, 1), :] * e_ref[pl.ds(off + k, tm), :]
        c1_ref[...] = acc
        xhat, _ = _layernorm_stats(acc)
        c2 = xhat * g_ref[...] + be_ref[...]
        c3_ref[...] = (c2 * _sigmoid(c2)).astype(BF16)

    row = pl.BlockSpec((1, D), lambda i: (0, 0))
    return pl.pallas_call(
        body, name=name, grid=(T // tm,),
        in_specs=[pl.BlockSpec((tm, D), lambda i: (i, 0)), pl.BlockSpec((tm, D), lambda i: (i, 1)),
                  pl.BlockSpec((CONV_HALO, D), lambda i: (jnp.maximum(i * hb - 1, 0), 0)),
                  pl.BlockSpec((CONV_HALO, D), lambda i: (jnp.maximum(i * hb - 1, 0), 1)),
                  pl.BlockSpec((CONV_HALO, D), lambda i: (0, 0)), row, row, row],
        out_specs=[pl.BlockSpec((tm, D), lambda i: (i, 0)), pl.BlockSpec((tm, D), lambda i: (i, 0))],
        out_shape=[jax.ShapeDtypeStruct((T, D), F32), jax.ShapeDtypeStruct((T, D), BF16)],
        scratch_shapes=[pltpu.VMEM((tm + CONV_HALO, D), F32)],
        compiler_params=_params(("parallel",)),
    )(p_glu, p_glu, p_glu, p_glu, dw_w, dw_b, ln_g, ln_b)


def conv_bwd(p_glu, dc1, dw_w, name):
    T, D2 = p_glu.shape
    D = D2 // 2
    tm = min(TM_ROW, T)
    hb = tm // CONV_HALO
    last = T // CONV_HALO - 1
    nblk = T // tm

    def body(a_ref, b_ref, ah_ref, bh_ref, d_ref, dn_ref, w_ref, dp_ref, dw_ref, e_ref, f_ref):
        i = pl.program_id(0)

        @pl.when(i == 0)
        def _():
            dw_ref[...] = jnp.zeros_like(dw_ref)

        a = a_ref[...]
        sb = _sigmoid(b_ref[...])
        halo = ah_ref[...] * _sigmoid(bh_ref[...])
        e_ref[pl.ds(0, CONV_HALO), :] = jnp.where(i > 0, halo, 0.0)
        e_ref[pl.ds(CONV_HALO, tm), :] = a * sb
        d = d_ref[...]
        f_ref[pl.ds(0, tm), :] = d
        f_ref[pl.ds(tm, CONV_HALO), :] = jnp.where(i < nblk - 1, dn_ref[...], 0.0)
        off = CONV_HALO - (CONV_WIDTH - 1)
        dc0 = jnp.zeros((tm, D), F32)
        for k in range(CONV_WIDTH):
            dw_ref[pl.ds(k, 1), :] += jnp.sum(d * e_ref[pl.ds(off + k, tm), :], axis=0, keepdims=True)
            dc0 += w_ref[pl.ds(k, 1), :] * f_ref[pl.ds(CONV_WIDTH - 1 - k, tm), :]
        dp_ref[:, pl.ds(0, D)] = (dc0 * sb).astype(BF16)
        dp_ref[:, pl.ds(D, D)] = (dc0 * a * sb * (1.0 - sb)).astype(BF16)

    return pl.pallas_call(
        body, name=name, grid=(nblk,),
        in_specs=[pl.BlockSpec((tm, D), lambda i: (i, 0)), pl.BlockSpec((tm, D), lambda i: (i, 1)),
                  pl.BlockSpec((CONV_HALO, D), lambda i: (jnp.maximum(i * hb - 1, 0), 0)),
                  pl.BlockSpec((CONV_HALO, D), lambda i: (jnp.maximum(i * hb - 1, 0), 1)),
                  pl.BlockSpec((tm, D), lambda i: (i, 0)),
                  pl.BlockSpec((CONV_HALO, D), lambda i: (jnp.minimum((i + 1) * hb, last), 0)),
                  pl.BlockSpec((CONV_HALO, D), lambda i: (0, 0))],
        out_specs=[pl.BlockSpec((tm, D2), lambda i: (i, 0)), pl.BlockSpec((CONV_HALO, D), lambda i: (0, 0))],
        out_shape=[jax.ShapeDtypeStruct((T, D2), BF16), jax.ShapeDtypeStruct((CONV_HALO, D), F32)],
        scratch_shapes=[pltpu.VMEM((tm + CONV_HALO, D), F32), pltpu.VMEM((tm + CONV_HALO, D), F32)],
        compiler_params=_params(("arbitrary",)),
    )(p_glu, p_glu, p_glu, p_glu, dc1, dc1, dw_w)


def _rot_half(x):
    lane = lax.broadcasted_iota(jnp.int32, x.shape, 1)
    first = (lane % HEAD_DIM) < HEAD_DIM // 2
    return jnp.where(first, pltpu.roll(x, 128 - HEAD_DIM // 2, 1), pltpu.roll(x, HEAD_DIM // 2, 1))


def _rope_chunks(x, cs, sn, sign):
    outs = []
    for c in range(x.shape[1] // 128):
        xc = x[:, c * 128:(c + 1) * 128]
        outs.append(xc * cs + sign * (_rot_half(xc) * sn))
    return outs[0] if len(outs) == 1 else jnp.concatenate(outs, axis=1)


def rope_fwd(p_qkv, cs, sn, D, name):
    T, W = p_qkv.shape
    KV = (W - D) // 2
    tm = min(TM_ROW, T)
    kb = D // KV

    def body(q_ref, k_ref, v_ref, cs_ref, sn_ref, qo_ref, ko_ref, vo_ref):
        cs_v, sn_v = cs_ref[...], sn_ref[...]
        qo_ref[...] = _rope_chunks(q_ref[...], cs_v, sn_v, 1.0).astype(BF16)
        ko_ref[...] = _rope_chunks(k_ref[...], cs_v, sn_v, 1.0).astype(BF16)
        vo_ref[...] = v_ref[...].astype(BF16)

    tab = pl.BlockSpec((tm, 128), lambda i: (i, 0))
    return pl.pallas_call(
        body, name=name, grid=(T // tm,),
        in_specs=[pl.BlockSpec((tm, D), lambda i: (i, 0)), pl.BlockSpec((tm, KV), lambda i: (i, kb)),
                  pl.BlockSpec((tm, KV), lambda i: (i, kb + 1)), tab, tab],
        out_specs=[pl.BlockSpec((tm, D), lambda i: (i, 0)), pl.BlockSpec((tm, KV), lambda i: (i, 0)),
                   pl.BlockSpec((tm, KV), lambda i: (i, 0))],
        out_shape=[jax.ShapeDtypeStruct((T, D), BF16), jax.ShapeDtypeStruct((T, KV), BF16),
                   jax.ShapeDtypeStruct((T, KV), BF16)],
        compiler_params=_params(("parallel",)),
    )(p_qkv, p_qkv, p_qkv, cs, sn)


def rope_bwd(dq, dk, dv, cs, sn, name):
    T, D = dq.shape
    KV = dk.shape[1]
    tm = min(TM_ROW, T)

    def body(dq_ref, dk_ref, dv_ref, cs_ref, sn_ref, o_ref):
        cs_v, sn_v = cs_ref[...], sn_ref[...]
        o_ref[:, pl.ds(0, D)] = _rope_chunks(dq_ref[...], cs_v, sn_v, -1.0).astype(BF16)
        o_ref[:, pl.ds(D, KV)] = _rope_chunks(dk_ref[...], cs_v, sn_v, -1.0).astype(BF16)
        o_ref[:, pl.ds(D + KV, KV)] = dv_ref[...].astype(BF16)

    tab = pl.BlockSpec((tm, 128), lambda i: (i, 0))
    return pl.pallas_call(
        body, name=name, grid=(T // tm,),
        in_specs=[pl.BlockSpec((tm, D), lambda i: (i, 0)), pl.BlockSpec((tm, KV), lambda i: (i, 0)),
                  pl.BlockSpec((tm, KV), lambda i: (i, 0)), tab, tab],
        out_specs=pl.BlockSpec((tm, D + 2 * KV), lambda i: (i, 0)),
        out_shape=jax.ShapeDtypeStruct((T, D + 2 * KV), BF16),
        compiler_params=_params(("parallel",)),
    )(dq, dk, dv, cs, sn)


def _lane_lo():
    return lax.broadcasted_iota(jnp.int32, (1, 128), 1) < HEAD_DIM


def _band_mask(i):
    qi = lax.broadcasted_iota(jnp.int32, (WINDOW, 2 * WINDOW), 0)
    cj = lax.broadcasted_iota(jnp.int32, (WINDOW, 2 * WINDOW), 1)
    rel = qi - cj + WINDOW
    return (rel >= 0) & (rel < WINDOW) & ((i > 0) | (cj >= WINDOW))


def _kv_lo_hi(x2, g):
    pair, half = divmod(g, 2)
    lo = _lane_lo()
    xg = x2[:, pair * 128:(pair + 1) * 128].astype(F32)
    xg = jnp.where(lo if half == 0 else ~lo, xg, 0.0)
    sw = pltpu.roll(xg, HEAD_DIM, 1)
    x_lo, x_hi = (xg, sw) if half == 0 else (sw, xg)
    return x_lo.astype(BF16), x_hi.astype(BF16)


def _softmax_sink(s, allowed, sink):
    s = jnp.where(allowed, s * (HEAD_DIM ** -0.5), NEG_INF)
    m = jnp.maximum(jnp.max(s, axis=-1, keepdims=True), sink)
    p = jnp.exp(s - m)
    es = jnp.exp(sink - m)
    inv = 1.0 / (jnp.sum(p, axis=-1, keepdims=True) + es)
    return p * inv, es * inv


def attn_fwd(qr, kr, vb, sinks, name):
    T, D = qr.shape
    KV = kr.shape[1]
    n_kv = KV // HEAD_DIM
    group = (D // HEAD_DIM) // n_kv
    nb = T // WINDOW

    def body(sink_ref, q_ref, kp_ref, kc_ref, vp_ref, vc_ref, o_ref):
        i = pl.program_id(0)
        allowed = _band_mask(i)
        k2 = jnp.concatenate([kp_ref[...], kc_ref[...]], axis=0)
        v2 = jnp.concatenate([vp_ref[...], vc_ref[...]], axis=0)
        for g in range(n_kv):
            k_lo, k_hi = _kv_lo_hi(k2, g)
            v_lo, v_hi = _kv_lo_hi(v2, g)
            for pp in range(group // 2):
                h0 = g * group + 2 * pp
                cols = pl.ds((h0 // 2) * 128, 128)
                qp = q_ref[:, cols]
                pe, _ = _softmax_sink(_dot_nt(qp, k_lo), allowed, sink_ref[0, h0])
                po, _ = _softmax_sink(_dot_nt(qp, k_hi), allowed, sink_ref[0, h0 + 1])
                o = _dot(pe.astype(BF16), v_lo) + _dot(po.astype(BF16), v_hi)
                o_ref[:, cols] = o.astype(BF16)

    prev = lambda i: (jnp.maximum(i - 1, 0), 0)
    cur = lambda i: (i, 0)
    return pl.pallas_call(
        body, name=name, grid=(nb,),
        in_specs=[pl.BlockSpec(memory_space=pltpu.SMEM),
                  pl.BlockSpec((WINDOW, D), cur),
                  pl.BlockSpec((WINDOW, KV), prev), pl.BlockSpec((WINDOW, KV), cur),
                  pl.BlockSpec((WINDOW, KV), prev), pl.BlockSpec((WINDOW, KV), cur)],
        out_specs=pl.BlockSpec((WINDOW, D), cur),
        out_shape=jax.ShapeDtypeStruct((T, D), BF16),
        compiler_params=_params(("parallel",)),
    )(sinks, qr, kr, kr, vb, vb)


def attn_bwd(qr, kr, vb, o, do, sinks, name):
    T, D = qr.shape
    KV = kr.shape[1]
    n_heads = D // HEAD_DIM
    n_kv = KV // HEAD_DIM
    group = n_heads // n_kv
    nb = T // WINDOW
    scale = HEAD_DIM ** -0.5

    def body(sink_ref, q_ref, kp_ref, kc_ref, vp_ref, vc_ref, o_ref, do_ref,
             dq_ref, dk_ref, dv_ref, ds_ref, ck_ref, cv_ref):
        i = pl.program_id(0)
        lo = _lane_lo()

        @pl.when(i == 0)
        def _():
            ck_ref[...] = jnp.zeros_like(ck_ref)
            cv_ref[...] = jnp.zeros_like(cv_ref)
            ds_ref[...] = jnp.zeros_like(ds_ref)

        @pl.when(i < nb)
        def _():
            allowed = _band_mask(i)
            k2 = jnp.concatenate([kp_ref[...], kc_ref[...]], axis=0)
            v2 = jnp.concatenate([vp_ref[...], vc_ref[...]], axis=0)
            lane = lax.broadcasted_iota(jnp.int32, (1, 128), 1)
            dsink = jnp.zeros((1, 128), F32)
            dk_pairs = [jnp.zeros((2 * WINDOW, 128), F32) for _ in range(KV // 128)]
            dv_pairs = [jnp.zeros((2 * WINDOW, 128), F32) for _ in range(KV // 128)]
            for g in range(n_kv):
                k_lo, k_hi = _kv_lo_hi(k2, g)
                v_lo, v_hi = _kv_lo_hi(v2, g)
                dkg = jnp.zeros((2 * WINDOW, 128), F32)
                dvg = jnp.zeros((2 * WINDOW, 128), F32)
                for pp in range(group // 2):
                    h0 = g * group + 2 * pp
                    cols = pl.ds((h0 // 2) * 128, 128)
                    qp = q_ref[:, cols]
                    dop = do_ref[:, cols]
                    dd = dop.astype(F32) * o_ref[:, cols].astype(F32)
                    dq = jnp.zeros((WINDOW, 128), F32)
                    for h, k_h, v_h, sel in ((h0, k_lo, v_lo, lo), (h0 + 1, k_hi, v_hi, ~lo)):
                        p, ps = _softmax_sink(_dot_nt(qp, k_h), allowed, sink_ref[0, h])
                        delta = jnp.sum(jnp.where(sel, dd, 0.0), axis=-1, keepdims=True)
                        dp = _dot_nt(dop, v_h)
                        dsc = (p * (dp - delta)).astype(BF16)
                        dsink += jnp.where(lane == h, jnp.sum(-ps * delta), 0.0)
                        dq += _dot(dsc, k_h)
                        dkg += jnp.where(sel, _dot_tn(dsc, qp), 0.0)
                        dvg += jnp.where(sel, _dot_tn(p.astype(BF16), dop), 0.0)
                    dq_ref[:, cols] = dq * scale
                pair, half = divmod(g, 2)
                keep = lo if half == 0 else ~lo
                dk_pairs[pair] += jnp.where(keep, dkg + pltpu.roll(dkg, HEAD_DIM, 1), 0.0) * scale
                dv_pairs[pair] += jnp.where(keep, dvg + pltpu.roll(dvg, HEAD_DIM, 1), 0.0)
            dk2 = dk_pairs[0] if len(dk_pairs) == 1 else jnp.concatenate(dk_pairs, axis=1)
            dv2 = dv_pairs[0] if len(dv_pairs) == 1 else jnp.concatenate(dv_pairs, axis=1)
            dk_ref[...] = ck_ref[...] + dk2[:WINDOW]
            dv_ref[...] = cv_ref[...] + dv2[:WINDOW]
            ck_ref[...] = dk2[WINDOW:]
            cv_ref[...] = dv2[WINDOW:]
            ds_ref[pl.ds(0, 1), :] += dsink

        @pl.when(i == nb)
        def _():
            dk_ref[...] = ck_ref[...]
            dv_ref[...] = cv_ref[...]

    prev = lambda i: (jnp.maximum(i - 1, 0), 0)
    cur = lambda i: (jnp.minimum(i, nb - 1), 0)
    prevc = lambda i: (jnp.maximum(jnp.minimum(i, nb - 1) - 1, 0), 0)
    return pl.pallas_call(
        body, name=name, grid=(nb + 1,),
        in_specs=[pl.BlockSpec(memory_space=pltpu.SMEM),
                  pl.BlockSpec((WINDOW, D), cur),
                  pl.BlockSpec((WINDOW, KV), prevc), pl.BlockSpec((WINDOW, KV), cur),
                  pl.BlockSpec((WINDOW, KV), prevc), pl.BlockSpec((WINDOW, KV), cur),
                  pl.BlockSpec((WINDOW, D), cur), pl.BlockSpec((WINDOW, D), cur)],
        out_specs=[pl.BlockSpec((WINDOW, D), cur), pl.BlockSpec((WINDOW, KV), prev),
                   pl.BlockSpec((WINDOW, KV), prev), pl.BlockSpec((8, 128), lambda i: (0, 0))],
        out_shape=[jax.ShapeDtypeStruct((T, D), F32), jax.ShapeDtypeStruct((T, KV), F32),
                   jax.ShapeDtypeStruct((T, KV), F32), jax.ShapeDtypeStruct((8, 128), F32)],
        scratch_shapes=[pltpu.VMEM((WINDOW, KV), F32), pltpu.VMEM((WINDOW, KV), F32)],
        compiler_params=_params(("arbitrary",)),
    )(sinks, qr, kr, kr, vb, vb, o, do)


def merge_fwd(x, c3, o, p_gate, gate_b, w_proj, w_o, w_out, name):
    T, D = x.shape
    tm = min(TM_ROW, T)

    def body(x_ref, c3_ref, o_ref, gc_ref, ga_ref, bc_ref, ba_ref, wp_ref, wo_ref, wout_ref,
             xo_ref, co_ref, ao_ref, mg_ref):
        conv_out = _dot(c3_ref[...], wp_ref[...])
        attn_out = _dot(o_ref[...], wo_ref[...])
        merged = (_sigmoid(gc_ref[...] + bc_ref[...]) * conv_out
                  + _sigmoid(ga_ref[...] + ba_ref[...]) * attn_out).astype(BF16)
        co_ref[...] = conv_out.astype(BF16)
        ao_ref[...] = attn_out.astype(BF16)
        mg_ref[...] = merged
        xo_ref[...] = x_ref[...] + _dot(merged, wout_ref[...])

    blk = lambda j: pl.BlockSpec((tm, D), lambda i: (i, j))
    row = lambda j: pl.BlockSpec((1, D), lambda i: (0, j))
    mat = pl.BlockSpec((D, D), lambda i: (0, 0))
    return pl.pallas_call(
        body, name=name, grid=(T // tm,),
        in_specs=[blk(0), blk(0), blk(0), blk(0), blk(1), row(0), row(1), mat, mat, mat],
        out_specs=[blk(0), blk(0), blk(0), blk(0)],
        out_shape=[jax.ShapeDtypeStruct((T, D), F32)] + [jax.ShapeDtypeStruct((T, D), BF16)] * 3,
        compiler_params=_params(("parallel",)),
    )(x, c3, o, p_gate, p_gate, gate_b, gate_b, w_proj, w_o, w_out)


def merge_bwd(dx, p_gate, gate_b, conv_out, attn_out, c1, ln_g, ln_b, w_proj, w_o, w_out, name):
    T, D = dx.shape
    tm = min(TM_ROW, T)

    def body(dx_ref, gc_ref, ga_ref, bc_ref, ba_ref, co_ref, ao_ref, c1_ref, g_ref, be_ref,
             wp_ref, wo_ref, wout_ref, dgt_ref, dco_ref, dao_ref, do_ref, dc1_ref, sm_ref):
        @pl.when(pl.program_id(0) == 0)
        def _():
            sm_ref[...] = jnp.zeros_like(sm_ref)

        dm = _dot_nt(dx_ref[...].astype(BF16), wout_ref[...])
        sc = _sigmoid(gc_ref[...] + bc_ref[...])
        sa = _sigmoid(ga_ref[...] + ba_ref[...])
        dco = (dm * sc).astype(BF16)
        dao = (dm * sa).astype(BF16)
        dgc = dm * co_ref[...].astype(F32) * sc * (1.0 - sc)
        dga = dm * ao_ref[...].astype(F32) * sa * (1.0 - sa)
        dgt_ref[:, pl.ds(0, D)] = dgc.astype(BF16)
        dgt_ref[:, pl.ds(D, D)] = dga.astype(BF16)
        dco_ref[...] = dco
        dao_ref[...] = dao
        do_ref[...] = _dot_nt(dao, wo_ref[...]).astype(BF16)
        dc3 = _dot_nt(dco, wp_ref[...])
        xhat, rstd = _layernorm_stats(c1_ref[...])
        c2 = xhat * g_ref[...] + be_ref[...]
        dc2 = dc3 * _silu_grad(c2, _sigmoid(c2))
        dxh = dc2 * g_ref[...]
        dc1 = rstd * (dxh - jnp.mean(dxh, axis=-1, keepdims=True)
                      - xhat * jnp.mean(dxh * xhat, axis=-1, keepdims=True))
        dc1_ref[...] = dc1
        colsum = lambda v: jnp.sum(v, axis=0, keepdims=True)
        for r, (left, right) in enumerate(((dgc, dga), (dc2 * xhat, dc2), (dc1, None))):
            sm_ref[pl.ds(r, 1), pl.ds(0, D)] += colsum(left)
            if right is not None:
                sm_ref[pl.ds(r, 1), pl.ds(D, D)] += colsum(right)

    blk = lambda j: pl.BlockSpec((tm, D), lambda i: (i, j))
    row = lambda j: pl.BlockSpec((1, D), lambda i: (0, j))
    mat = pl.BlockSpec((D, D), lambda i: (0, 0))
    return pl.pallas_call(
        body, name=name, grid=(T // tm,),
        in_specs=[blk(0), blk(0), blk(1), row(0), row(1), blk(0), blk(0), blk(0), row(0), row(0), mat, mat, mat],
        out_specs=[pl.BlockSpec((tm, 2 * D), lambda i: (i, 0)), blk(0), blk(0), blk(0), blk(0),
                   pl.BlockSpec((8, 2 * D), lambda i: (0, 0))],
        out_shape=[jax.ShapeDtypeStruct((T, 2 * D), BF16)] + [jax.ShapeDtypeStruct((T, D), BF16)] * 3
                  + [jax.ShapeDtypeStruct((T, D), F32), jax.ShapeDtypeStruct((8, 2 * D), F32)],
        compiler_params=_params(("arbitrary",)),
    )(dx, p_gate, p_gate, gate_b, gate_b, conv_out, attn_out, c1, ln_g, ln_b, w_proj, w_o, w_out)


def loss_head(x, nw, target, name):
    T, D = x.shape
    tm = min(TM_ROW, T)

    def body(x_ref, nw_ref, t_ref, dx_ref, sm_ref):
        @pl.when(pl.program_id(0) == 0)
        def _():
            sm_ref[...] = jnp.zeros_like(sm_ref)

        xv = x_ref[...]
        err = xv * _rms_scale(xv) * nw_ref[...] - t_ref[...]
        loss = 0.5 * jnp.sum(jnp.mean(err * err, axis=-1, keepdims=True))
        dxn, dnw = _rms_bwd(xv, nw_ref[...], err * (1.0 / D))
        dx_ref[...] = dxn
        sm_ref[pl.ds(0, 1), :] += dnw
        sm_ref[pl.ds(1, 1), :] += jnp.zeros((1, D), F32) + loss

    return pl.pallas_call(
        body, name=name, grid=(T // tm,),
        in_specs=[pl.BlockSpec((tm, D), lambda i: (i, 0)), pl.BlockSpec((1, D), lambda i: (0, 0)),
                  pl.BlockSpec((tm, D), lambda i: (i, 0))],
        out_specs=[pl.BlockSpec((tm, D), lambda i: (i, 0)), pl.BlockSpec((8, D), lambda i: (0, 0))],
        out_shape=[jax.ShapeDtypeStruct((T, D), F32), jax.ShapeDtypeStruct((8, D), F32)],
        compiler_params=_params(("arbitrary",)),
    )(x, nw, target)


def adamw(w, g, m, v, name):
    R, C = w.shape
    tr = _row_tile(R, TR_ELT)

    def body(w_ref, g_ref, m_ref, v_ref, d_ref, mo_ref, vo_ref):
        gv = g_ref[...]
        mn = ADAM_B1 * m_ref[...] + (1.0 - ADAM_B1) * gv
        vn = ADAM_B2 * v_ref[...] + (1.0 - ADAM_B2) * (gv * gv)
        m_hat = mn / (1.0 - ADAM_B1 ** ADAM_STEP)
        v_hat = vn / (1.0 - ADAM_B2 ** ADAM_STEP)
        d_ref[...] = -ADAM_LR * (m_hat / (jnp.sqrt(v_hat) + ADAM_EPS) + ADAM_WD * w_ref[...])
        mo_ref[...] = mn
        vo_ref[...] = vn

    spec = pl.BlockSpec((tr, C), lambda i: (i, 0))
    return pl.pallas_call(
        body, name=name, grid=(R // tr,), in_specs=[spec] * 4, out_specs=[spec] * 3,
        out_shape=[jax.ShapeDtypeStruct((R, C), F32)] * 3,
        compiler_params=_params(("parallel",)),
    )(w, g, m, v)


HBM_SPEC = pl.BlockSpec(memory_space=pl.ANY)


def _place():
    return lax.axis_index("x"), lax.axis_index("y"), lax.axis_index("c")


def gather_weights(shards, small):
    n, ns = len(shards), len(small)

    def body(*refs):
        src = refs[:n + ns]
        dst = refs[n + ns:2 * (n + ns)]
        ici_send, ici_recv, d2d_send, d2d_recv, loc_sem = refs[2 * (n + ns):]
        x, y, c = _place()
        me = 2 * x + y
        local, sends, fwds = [], [], []
        for k in range(n + ns):
            cp = pltpu.make_async_copy(src[k], dst[k].at[me], loc_sem.at[k])
            cp.start()
            local.append(cp)
        for k in range(n + ns):
            half = src[k].shape[0] // 2
            for j in (1, 2, 3):
                to = (x ^ (j >> 1), y ^ (j & 1), c)
                if k < n:
                    rows = pl.ds(c * half, half)
                    s_ref, d_ref = src[k].at[rows], dst[k].at[me, rows]
                else:
                    s_ref, d_ref = src[k], dst[k].at[me]
                cp = pltpu.make_async_remote_copy(s_ref, d_ref, ici_send.at[3 * k + j - 1], ici_recv.at[3 * k + j - 1],
                                                  device_id=to, device_id_type=MESH)
                cp.start()
                sends.append(cp)
        for k in range(n + ns):
            half = src[k].shape[0] // 2
            for j in (1, 2, 3):
                frm = 2 * (x ^ (j >> 1)) + (y ^ (j & 1))
                if k < n:
                    rows = pl.ds(c * half, half)
                    got = dst[k].at[frm, rows]
                    pltpu.make_async_remote_copy(got, got, ici_send.at[3 * k + j - 1], ici_recv.at[3 * k + j - 1],
                                                 device_id=(x, y, c), device_id_type=MESH).wait_recv()
                    cp = pltpu.make_async_remote_copy(got, got, d2d_send.at[3 * k + j - 1], d2d_recv.at[3 * k + j - 1],
                                                      device_id=(x, y, 1 - c), device_id_type=MESH)
                    cp.start()
                    fwds.append(cp)
                else:
                    got = dst[k].at[frm]
                    pltpu.make_async_remote_copy(got, got, ici_send.at[3 * k + j - 1], ici_recv.at[3 * k + j - 1],
                                                 device_id=(x, y, c), device_id_type=MESH).wait_recv()
        for k in range(n):
            half = src[k].shape[0] // 2
            for j in (1, 2, 3):
                frm = 2 * (x ^ (j >> 1)) + (y ^ (j & 1))
                got = dst[k].at[frm, pl.ds((1 - c) * half, half)]
                pltpu.make_async_remote_copy(got, got, d2d_send.at[3 * k + j - 1], d2d_recv.at[3 * k + j - 1],
                                             device_id=(x, y, c), device_id_type=MESH).wait_recv()
        for cp in sends + fwds:
            cp.wait_send()
        for cp in local:
            cp.wait()

    arrays = list(shards) + list(small)
    return pl.pallas_call(
        body, name="gather_weights",
        in_specs=[HBM_SPEC] * (n + ns), out_specs=[HBM_SPEC] * (n + ns),
        out_shape=[jax.ShapeDtypeStruct((N_CHIPS,) + a.shape, a.dtype) for a in arrays],
        scratch_shapes=[pltpu.SemaphoreType.DMA((3 * (n + ns),)), pltpu.SemaphoreType.DMA((3 * (n + ns),)),
                        pltpu.SemaphoreType.DMA((3 * n,)), pltpu.SemaphoreType.DMA((3 * n,)),
                        pltpu.SemaphoreType.DMA((n + ns,))],
    )(*arrays)


def allreduce_small(block):
    R, C = block.shape

    def body(x_ref, out_ref, all_ref, send_sems, recv_sems, local_sem):
        x, y, c = _place()
        me, sibling = (x, y, c), (x, y, 1 - c)
        chips = [(1 - x, y), (x, 1 - y), (1 - x, 1 - y)]

        def slot(px, py, pc):
            return all_ref.at[4 * px + 2 * py + pc]

        def copy(k, block_of, to, src=None):
            return pltpu.make_async_remote_copy(
                src_ref=slot(*block_of) if src is None else src, dst_ref=slot(*block_of),
                send_sem=send_sems.at[k], recv_sem=recv_sems.at[k], device_id=to, device_id_type=MESH)

        mine = pltpu.make_async_copy(x_ref, slot(*me), local_sem)
        mine.start()
        first = [copy(0, me, sibling, src=x_ref)]
        first += [copy(1 + j, me, (*chip, c), src=x_ref) for j, chip in enumerate(chips)]
        for cp in first:
            cp.start()
        passed = [copy(4 + j, (*chip, c), sibling) for j, chip in enumerate(chips)]
        for j, chip in enumerate(chips):
            copy(1 + j, (*chip, c), me).wait_recv()
            passed[j].start()
        copy(0, sibling, me).wait_recv()
        for j, chip in enumerate(chips):
            copy(4 + j, (*chip, 1 - c), me).wait_recv()
        for cp in first + passed:
            cp.wait_send()
        mine.wait()
        total = all_ref[0]
        for d in range(1, N_DEV):
            total = total + all_ref[d]
        out_ref[...] = total

    return pl.pallas_call(
        body, name="allreduce_small",
        in_specs=[pl.BlockSpec(memory_space=pltpu.VMEM)], out_specs=pl.BlockSpec(memory_space=pltpu.VMEM),
        out_shape=jax.ShapeDtypeStruct((R, C), F32),
        scratch_shapes=[pltpu.VMEM((N_DEV, R, C), F32), pltpu.SemaphoreType.DMA((7,)),
                        pltpu.SemaphoreType.DMA((7,)), pltpu.SemaphoreType.DMA],
        compiler_params=pltpu.CompilerParams(vmem_limit_bytes=VMEM_LIMIT),
    )(block)


def rs_exchange_siblings(grads):
    n = len(grads)

    def body(*refs):
        src, dst = refs[:n], refs[n:2 * n]
        send_sems, recv_sems = refs[2 * n:]
        x, y, c = _place()
        copies = []
        for k in range(n):
            half = src[k].shape[1] // 2
            cp = pltpu.make_async_remote_copy(src[k].at[:, pl.ds((1 - c) * half, half)], dst[k],
                                              send_sems.at[k], recv_sems.at[k],
                                              device_id=(x, y, 1 - c), device_id_type=MESH)
            cp.start()
            copies.append(cp)
        for cp in copies:
            cp.wait()

    return pl.pallas_call(
        body, name="rs_exchange_siblings",
        in_specs=[HBM_SPEC] * n, out_specs=[HBM_SPEC] * n,
        out_shape=[jax.ShapeDtypeStruct((N_CHIPS, g.shape[1] // 2, g.shape[2]), F32) for g in grads],
        scratch_shapes=[pltpu.SemaphoreType.DMA((n,)), pltpu.SemaphoreType.DMA((n,))],
    )(*grads)


def rs_chip_sum(place, grad, sib, name):
    NP, R, C = grad.shape
    half = R // 2
    tr = _row_tile(half, TR_ELT)
    nr = half // tr

    def body(pc_ref, g_ref, s_ref, wire_ref, own_ref):
        q = pl.program_id(1)
        total = g_ref[...] + s_ref[...]
        wire_ref[...] = total.astype(BF16)

        @pl.when(q == pc_ref[0])
        def _():
            own_ref[...] = total

    return pl.pallas_call(
        body, name=name,
        grid_spec=pltpu.PrefetchScalarGridSpec(
            num_scalar_prefetch=1, grid=(nr, NP),
            in_specs=[pl.BlockSpec((None, tr, C), lambda r, q, pc: (q, pc[1] * nr + r, 0)),
                      pl.BlockSpec((None, tr, C), lambda r, q, pc: (q, r, 0))],
            out_specs=[pl.BlockSpec((None, tr, C), lambda r, q, pc: (q, r, 0)),
                       pl.BlockSpec((tr, C), lambda r, q, pc: (r, 0))]),
        out_shape=[jax.ShapeDtypeStruct((NP, half, C), BF16), jax.ShapeDtypeStruct((half, C), F32)],
        compiler_params=_params(("arbitrary", "arbitrary")),
    )(place, grad, sib)


def rs_exchange_chips(wires):
    n = len(wires)

    def body(*refs):
        src, dst = refs[:n], refs[n:2 * n]
        send_sems, recv_sems, loc_sem = refs[2 * n:]
        x, y, c = _place()
        me = 2 * x + y
        copies, local = [], []
        for k in range(n):
            cp = pltpu.make_async_copy(src[k].at[me], dst[k].at[me], loc_sem.at[k])
            cp.start()
            local.append(cp)
            for j in (1, 2, 3):
                qx, qy = x ^ (j >> 1), y ^ (j & 1)
                cp = pltpu.make_async_remote_copy(src[k].at[2 * qx + qy], dst[k].at[me],
                                                  send_sems.at[3 * k + j - 1], recv_sems.at[3 * k + j - 1],
                                                  device_id=(qx, qy, c), device_id_type=MESH)
                cp.start()
                copies.append(cp)
        for cp in copies:
            cp.wait()
        for cp in local:
            cp.wait()

    return pl.pallas_call(
        body, name="rs_exchange_chips",
        in_specs=[HBM_SPEC] * n, out_specs=[HBM_SPEC] * n,
        out_shape=[jax.ShapeDtypeStruct(w.shape, BF16) for w in wires],
        scratch_shapes=[pltpu.SemaphoreType.DMA((3 * n,)), pltpu.SemaphoreType.DMA((3 * n,)),
                        pltpu.SemaphoreType.DMA((n,))],
    )(*wires)


def rs_final_sum(place, own, got, name):
    NP, half, C = got.shape
    tr = _row_tile(half, TR_ELT)

    def body(pc_ref, own_ref, got_ref, out_ref):
        total = own_ref[...]
        for q in range(NP):
            total = total + jnp.where(q == pc_ref[0], 0.0, got_ref[q].astype(F32))
        out_ref[...] = total

    return pl.pallas_call(
        body, name=name,
        grid_spec=pltpu.PrefetchScalarGridSpec(
            num_scalar_prefetch=1, grid=(half // tr,),
            in_specs=[pl.BlockSpec((tr, C), lambda r, pc: (r, 0)),
                      pl.BlockSpec((NP, tr, C), lambda r, pc: (0, r, 0))],
            out_specs=pl.BlockSpec((tr, C), lambda r, pc: (r, 0))),
        out_shape=jax.ShapeDtypeStruct((half, C), F32),
        compiler_params=_params(("arbitrary",)),
    )(place, own, got)


def rs_share_siblings(totals):
    n = len(totals)

    def body(*refs):
        src, dst = refs[:n], refs[n:2 * n]
        send_sems, recv_sems, loc_sem = refs[2 * n:]
        x, y, c = _place()
        copies, local = [], []
        for k in range(n):
            half = src[k].shape[0]
            rows = dst[k].at[pl.ds(c * half, half)]
            cp = pltpu.make_async_copy(src[k], rows, loc_sem.at[k])
            cp.start()
            local.append(cp)
            cp = pltpu.make_async_remote_copy(src[k], rows, send_sems.at[k], recv_sems.at[k],
                                              device_id=(x, y, 1 - c), device_id_type=MESH)
            cp.start()
            copies.append(cp)
        for k, cp in enumerate(copies):
            cp.wait_send()
            half = src[k].shape[0]
            got = dst[k].at[pl.ds((1 - c) * half, half)]
            pltpu.make_async_remote_copy(got, got, send_sems.at[k], recv_sems.at[k],
                                         device_id=(x, y, c), device_id_type=MESH).wait_recv()
        for cp in local:
            cp.wait()

    return pl.pallas_call(
        body, name="rs_share_siblings",
        in_specs=[HBM_SPEC] * n, out_specs=[HBM_SPEC] * n,
        out_shape=[jax.ShapeDtypeStruct((2 * t.shape[0], t.shape[1]), F32) for t in totals],
        scratch_shapes=[pltpu.SemaphoreType.DMA((n,)), pltpu.SemaphoreType.DMA((n,)),
                        pltpu.SemaphoreType.DMA((n,))],
    )(*totals)


def reduce_scatter(grads):
    x, y, c = _place()
    place = jnp.stack([2 * x + y, c]).astype(jnp.int32)
    sibs = rs_exchange_siblings(grads)
    wires, owns = [], []
    for k, (g, s) in enumerate(zip(grads, sibs)):
        w, o = rs_chip_sum(place, g, s, f"rs_chip_sum_{k}")
        wires.append(w)
        owns.append(o)
    gots = rs_exchange_chips(wires)
    totals = [rs_final_sum(place, o, g, f"rs_final_sum_{k}") for k, (o, g) in enumerate(zip(owns, gots))]
    return rs_share_siblings(totals)


def _rope_tables(positions):
    half = HEAD_DIM // 2
    inv_freq = ROPE_THETA ** (-jnp.arange(half, dtype=F32) / half)
    ang = positions.astype(F32)[:, None] * inv_freq
    cos, sin = jnp.cos(ang), jnp.sin(ang)
    return jnp.tile(cos, (1, 4)), jnp.concatenate([-sin, sin, -sin, sin], axis=1)


def _pieces_from_cols(full):
    D, W = full.shape
    return full.reshape(D, N_CHIPS, W // N_CHIPS).transpose(1, 0, 2)


def kernel(x, positions, ffn1_norm, ffn1_w_gate, ffn1_w_up, ffn1_w_down, mix_norm, w_in, conv_dw_w, conv_dw_b, conv_ln_g, conv_ln_b, conv_w_proj, attn_sinks, attn_w_o, gate_b, w_out, ffn2_norm, ffn2_w_gate, ffn2_w_up, ffn2_w_down, final_norm, loss_target, m_ffn1_norm, m_ffn1_w_gate, m_ffn1_w_up, m_ffn1_w_down, m_mix_norm, m_w_in, m_conv_dw_w, m_conv_dw_b, m_conv_ln_g, m_conv_ln_b, m_conv_w_proj, m_attn_sinks, m_attn_w_o, m_gate_b, m_w_out, m_ffn2_norm, m_ffn2_w_gate, m_ffn2_w_up, m_ffn2_w_down, m_final_norm, v_ffn1_norm, v_ffn1_w_gate, v_ffn1_w_up, v_ffn1_w_down, v_mix_norm, v_w_in, v_conv_dw_w, v_conv_dw_b, v_conv_ln_g, v_conv_ln_b, v_conv_w_proj, v_attn_sinks, v_attn_w_o, v_gate_b, v_w_out, v_ffn2_norm, v_ffn2_w_gate, v_ffn2_w_up, v_ffn2_w_down, v_final_norm):
    weights = dict(ffn1_norm=ffn1_norm, ffn1_w_gate=ffn1_w_gate, ffn1_w_up=ffn1_w_up, ffn1_w_down=ffn1_w_down,
                   mix_norm=mix_norm, w_in=w_in, conv_dw_w=conv_dw_w, conv_dw_b=conv_dw_b, conv_ln_g=conv_ln_g,
                   conv_ln_b=conv_ln_b, conv_w_proj=conv_w_proj, attn_sinks=attn_sinks, attn_w_o=attn_w_o,
                   gate_b=gate_b, w_out=w_out, ffn2_norm=ffn2_norm, ffn2_w_gate=ffn2_w_gate, ffn2_w_up=ffn2_w_up,
                   ffn2_w_down=ffn2_w_down, final_norm=final_norm)
    m_in = dict(ffn1_norm=m_ffn1_norm, ffn1_w_gate=m_ffn1_w_gate, ffn1_w_up=m_ffn1_w_up, ffn1_w_down=m_ffn1_w_down,
                mix_norm=m_mix_norm, w_in=m_w_in, conv_dw_w=m_conv_dw_w, conv_dw_b=m_conv_dw_b,
                conv_ln_g=m_conv_ln_g, conv_ln_b=m_conv_ln_b, conv_w_proj=m_conv_w_proj, attn_sinks=m_attn_sinks,
                attn_w_o=m_attn_w_o, gate_b=m_gate_b, w_out=m_w_out, ffn2_norm=m_ffn2_norm,
                ffn2_w_gate=m_ffn2_w_gate, ffn2_w_up=m_ffn2_w_up, ffn2_w_down=m_ffn2_w_down, final_norm=m_final_norm)
    v_in = dict(ffn1_norm=v_ffn1_norm, ffn1_w_gate=v_ffn1_w_gate, ffn1_w_up=v_ffn1_w_up, ffn1_w_down=v_ffn1_w_down,
                mix_norm=v_mix_norm, w_in=v_w_in, conv_dw_w=v_conv_dw_w, conv_dw_b=v_conv_dw_b,
                conv_ln_g=v_conv_ln_g, conv_ln_b=v_conv_ln_b, conv_w_proj=v_conv_w_proj, attn_sinks=v_attn_sinks,
                attn_w_o=v_attn_w_o, gate_b=v_gate_b, w_out=v_w_out, ffn2_norm=v_ffn2_norm,
                ffn2_w_gate=v_ffn2_w_gate, ffn2_w_up=v_ffn2_w_up, ffn2_w_down=v_ffn2_w_down, final_norm=v_final_norm)
    names = list(weights)
    big = ["ffn1_w_gate", "ffn1_w_up", "ffn1_w_down", "w_in", "conv_w_proj", "attn_w_o", "w_out",
           "ffn2_w_gate", "ffn2_w_up", "ffn2_w_down"]

    xs = x[0]
    T, D = xs.shape
    KV = (w_in.shape[2] * N_CHIPS - 5 * D) // 2
    n_heads = D // HEAD_DIM
    my_chip = 2 * lax.axis_index("x") + lax.axis_index("y")

    gathered = gather_weights([weights[k][0].astype(BF16) for k in big], [conv_dw_w[0]])
    full = dict(zip(big + ["conv_dw_w"], gathered))
    wg1, wu1, wd1 = full["ffn1_w_gate"], full["ffn1_w_up"], full["ffn1_w_down"]
    wg2, wu2, wd2 = full["ffn2_w_gate"], full["ffn2_w_up"], full["ffn2_w_down"]
    w_in_full = full["w_in"].transpose(1, 0, 2).reshape(D, -1)
    w_glu, w_qkv, w_gate = w_in_full[:, :2 * D], w_in_full[:, 2 * D:3 * D + 2 * KV], w_in_full[:, 3 * D + 2 * KV:]
    w_proj = full["conv_w_proj"].reshape(D, D)
    w_o = full["attn_w_o"].reshape(D, D)
    w_out_f = full["w_out"].reshape(D, D)
    dw_w = full["conv_dw_w"].transpose(1, 0, 2).reshape(CONV_WIDTH, D)
    dw_w = jnp.concatenate([dw_w, jnp.zeros((CONV_HALO - CONV_WIDTH, D), F32)], axis=0)
    cs, sn = _rope_tables(positions[0])
    fn_row = final_norm.reshape(1, D)

    x1, h1, g1, u1 = ffn_fwd(xs, ffn1_norm, wg1, wu1, wd1, "ffn1_fwd")
    h2 = rmsnorm_fwd(x1, mix_norm, "mix_norm_fwd")
    p_glu = matmul_nn(h2, w_glu, "mix_in_glu")
    p_qkv = matmul_nn(h2, w_qkv, "mix_in_qkv")
    p_gate = matmul_nn(h2, w_gate, "mix_in_gate")
    c1, c3 = conv_fwd(p_glu, dw_w, conv_dw_b, conv_ln_g, conv_ln_b, "conv_fwd")
    qr, kr, vb = rope_fwd(p_qkv, cs, sn, D, "rope_fwd")
    o = attn_fwd(qr, kr, vb, attn_sinks, "attn_fwd")
    x2, conv_out, attn_out, merged = merge_fwd(x1, c3, o, p_gate, gate_b, w_proj, w_o, w_out_f, "merge_fwd")
    x3, h3, g2, u2 = ffn_fwd(x2, ffn2_norm, wg2, wu2, wd2, "ffn2_fwd")

    dx3, head_sums = loss_head(x3, fn_row, loss_target[0], "loss_head")
    dx2, dg2, du2, d_ffn2_norm = ffn_bwd_x(x2, ffn2_norm, g2, u2, wg2, wu2, wd2, dx3, "ffn2_bwd_x")
    dwg2, dwu2, dwd2 = ffn_bwd_w(h3, g2, u2, dg2, du2, dx3, "ffn2_bwd_w")
    d_gates, d_conv_out, d_attn_out, d_o, dc1, merge_sums = merge_bwd(
        dx2, p_gate, gate_b, conv_out, attn_out, c1, conv_ln_g, conv_ln_b, w_proj, w_o, w_out_f, "merge_bwd")
    d_w_out = matmul_tn(merged, dx2, "d_w_out")
    d_w_proj = matmul_tn(c3, d_conv_out, "d_conv_w_proj")
    d_w_o = matmul_tn(o, d_attn_out, "d_attn_w_o")
    d_glu, d_dw_w = conv_bwd(p_glu, dc1, dw_w, "conv_bwd")
    dq, dk, dv, d_sinks = attn_bwd(qr, kr, vb, o, d_o, attn_sinks, "attn_bwd")
    d_qkv = rope_bwd(dq, dk, dv, cs, sn, "rope_bwd")
    dx1, d_mix_norm = mix_in_bwd([d_glu, d_qkv, d_gates], [w_glu, w_qkv, w_gate], x1, mix_norm, dx2, "mix_in_bwd")
    d_w_in = jnp.concatenate([matmul_tn(h2, d_glu, "d_w_in_glu"), matmul_tn(h2, d_qkv, "d_w_in_qkv"),
                              matmul_tn(h2, d_gates, "d_w_in_gate")], axis=1)
    dx0, dg1, du1, d_ffn1_norm = ffn_bwd_x(xs, ffn1_norm, g1, u1, wg1, wu1, wd1, dx1, "ffn1_bwd_x")
    dwg1, dwu1, dwd1 = ffn_bwd_w(h1, g1, u1, dg1, du1, dx1, "ffn1_bwd_w")

    pad_row = lambda v: jnp.pad(v, ((0, 0), (0, D - v.shape[1])))
    small_rows = jnp.concatenate([
        d_ffn1_norm, d_mix_norm, merge_sums[2:3, :D], merge_sums[1:2, :D], merge_sums[1:2, D:],
        pad_row(d_sinks[0:1, :n_heads]), merge_sums[0:1, :D], merge_sums[0:1, D:], d_ffn2_norm,
        head_sums[0:1], head_sums[1:2], jnp.zeros((5, D), F32), d_dw_w], axis=0)
    small = allreduce_small(small_rows)
    loss = small[10, 0]
    grads = {"ffn1_norm": small[0:1], "mix_norm": small[1:2], "conv_dw_b": small[2:3], "conv_ln_g": small[3:4],
             "conv_ln_b": small[4:5], "attn_sinks": small[5:6, :n_heads],
             "gate_b": jnp.concatenate([small[6:7], small[7:8]], axis=1), "ffn2_norm": small[8:9],
             "final_norm": small[9:10]}
    dwc = D // N_CHIPS
    grads["conv_dw_w"] = lax.dynamic_slice(small[16:16 + CONV_WIDTH], (0, my_chip * dwc), (CONV_WIDTH, dwc))

    rs_in = [dwg1, dwu1, dwd1, _pieces_from_cols(d_w_in), d_w_proj.reshape(N_CHIPS, dwc, D),
             d_w_o.reshape(N_CHIPS, dwc, D), d_w_out.reshape(N_CHIPS, dwc, D), dwg2, dwu2, dwd2]
    grads.update(zip(big, reduce_scatter(rs_in)))

    deltas, new_m, new_v = {}, {}, {}
    for k in names:
        shape = weights[k].shape
        g2d = grads[k].reshape(-1, shape[-1])
        grads[k] = g2d.reshape(shape)
        d, mn, vn = adamw(weights[k].reshape(g2d.shape), g2d, m_in[k].reshape(g2d.shape),
                          v_in[k].reshape(g2d.shape), f"adamw_{k}")
        deltas[k], new_m[k], new_v[k] = d.reshape(shape), mn.reshape(shape), vn.reshape(shape)

    return (loss, dx0[None], *[grads[k] for k in names], *[deltas[k] for k in names],
            *[new_m[k] for k in names], *[new_v[k] for k in names])
```

```python
import functools

import jax
import jax.numpy as jnp
from jax import lax
from jax.experimental import pallas as pl
from jax.experimental.pallas import tpu as pltpu

F32 = jnp.float32
BF16 = jnp.bfloat16
MESH = pl.DeviceIdType.MESH

HEAD_DIM = 64
WINDOW = 128
CONV_WIDTH = 31
CONV_HALO = 32
ROPE_THETA = 10000.0
EPS = 1e-6
LN_EPS = 1e-5
NEG_INF = -1e30
N_CHIPS = 4
N_DEV = 8

ADAM_LR = 0.001
ADAM_B1 = 0.9
ADAM_B2 = 0.999
ADAM_EPS = 1e-08
ADAM_WD = 0.01
ADAM_STEP = 10

TM_FFN = 512
TM_MM = 512
TM_ROW = 256
TK_TN = 512
TR_ELT = 256
VMEM_LIMIT = 56 * 1024 * 1024

NT_DIMS = (((1,), (1,)), ((), ()))
TN_DIMS = (((0,), (0,)), ((), ()))


def _row_tile(rows, cap):
    for t in range(min(cap, rows), 15, -1):
        if rows % t == 0 and t % 16 == 0:
            return t
    return rows


def _params(sem):
    return pltpu.CompilerParams(dimension_semantics=sem, vmem_limit_bytes=VMEM_LIMIT)


def _dot(a, b):
    return jnp.dot(a, b, preferred_element_type=F32)


def _dot_nt(a, b):
    return lax.dot_general(a, b, NT_DIMS, preferred_element_type=F32)


def _dot_tn(a, b):
    return lax.dot_general(a, b, TN_DIMS, preferred_element_type=F32)


def _sigmoid(x):
    return jax.nn.sigmoid(x)


def _rms_scale(xv):
    return lax.rsqrt(jnp.mean(xv * xv, axis=-1, keepdims=True) + EPS)


def _rms_bwd(xv, nw, dh):
    r = _rms_scale(xv)
    dn = dh * nw
    dx = r * dn - xv * (r * r * r) * jnp.mean(dn * xv, axis=-1, keepdims=True)
    dnw = jnp.sum(dh * (xv * r), axis=0, keepdims=True)
    return dx, dnw


def _silu_grad(z, s):
    return s * (1.0 + z * (1.0 - s))


def ffn_fwd(x, nw, wg, wu, wd, name):
    T, D = x.shape
    NP, _, Fs = wg.shape
    tm = min(TM_FFN, T)

    def body(x_ref, nw_ref, wg_ref, wu_ref, wd_ref, xo_ref, h_ref, g_ref, u_ref, acc_ref):
        j = pl.program_id(1)

        @pl.when(j == 0)
        def _():
            xv = x_ref[...]
            h_ref[...] = (xv * _rms_scale(xv) * nw_ref[...]).astype(BF16)
            acc_ref[...] = jnp.zeros_like(acc_ref)

        h = h_ref[...]
        g = _dot(h, wg_ref[...])
        u = _dot(h, wu_ref[...])
        a = (g * _sigmoid(g)) * u
        g_ref[...] = g.astype(BF16)
        u_ref[...] = u.astype(BF16)
        acc_ref[...] += _dot(a.astype(BF16), wd_ref[...])

        @pl.when(j == NP - 1)
        def _():
            xo_ref[...] = x_ref[...] + 0.5 * acc_ref[...]

    return pl.pallas_call(
        body, name=name, grid=(T // tm, NP),
        in_specs=[pl.BlockSpec((tm, D), lambda i, j: (i, 0)),
                  pl.BlockSpec((1, D), lambda i, j: (0, 0)),
                  pl.BlockSpec((None, D, Fs), lambda i, j: (j, 0, 0)),
                  pl.BlockSpec((None, D, Fs), lambda i, j: (j, 0, 0)),
                  pl.BlockSpec((None, Fs, D), lambda i, j: (j, 0, 0))],
        out_specs=[pl.BlockSpec((tm, D), lambda i, j: (i, 0)),
                   pl.BlockSpec((tm, D), lambda i, j: (i, 0)),
                   pl.BlockSpec((None, tm, Fs), lambda i, j: (j, i, 0)),
                   pl.BlockSpec((None, tm, Fs), lambda i, j: (j, i, 0))],
        out_shape=[jax.ShapeDtypeStruct((T, D), F32), jax.ShapeDtypeStruct((T, D), BF16),
                   jax.ShapeDtypeStruct((NP, T, Fs), BF16), jax.ShapeDtypeStruct((NP, T, Fs), BF16)],
        scratch_shapes=[pltpu.VMEM((tm, D), F32)],
        compiler_params=_params(("parallel", "arbitrary")),
    )(x, nw, wg, wu, wd)


def ffn_bwd_x(x, nw, g, u, wg, wu, wd, dout, name):
    T, D = x.shape
    NP, _, Fs = wg.shape
    tm = min(TM_FFN, T)

    def body(x_ref, nw_ref, g_ref, u_ref, wg_ref, wu_ref, wd_ref, do_ref,
             dx_ref, dg_ref, du_ref, dnw_ref, dh_ref, dob_ref):
        i = pl.program_id(0)
        j = pl.program_id(1)

        @pl.when((i == 0) & (j == 0))
        def _():
            dnw_ref[...] = jnp.zeros_like(dnw_ref)

        @pl.when(j == 0)
        def _():
            dh_ref[...] = jnp.zeros_like(dh_ref)
            dob_ref[...] = (0.5 * do_ref[...]).astype(BF16)

        da = _dot_nt(dob_ref[...], wd_ref[...])
        gf = g_ref[...].astype(F32)
        uf = u_ref[...].astype(F32)
        s = _sigmoid(gf)
        dg = (da * uf * _silu_grad(gf, s)).astype(BF16)
        du = (da * (gf * s)).astype(BF16)
        dg_ref[...] = dg
        du_ref[...] = du
        dh_ref[...] += _dot_nt(dg, wg_ref[...]) + _dot_nt(du, wu_ref[...])

        @pl.when(j == NP - 1)
        def _():
            dxn, dnw = _rms_bwd(x_ref[...], nw_ref[...], dh_ref[...])
            dx_ref[...] = do_ref[...] + dxn
            dnw_ref[...] += dnw

    return pl.pallas_call(
        body, name=name, grid=(T // tm, NP),
        in_specs=[pl.BlockSpec((tm, D), lambda i, j: (i, 0)),
                  pl.BlockSpec((1, D), lambda i, j: (0, 0)),
                  pl.BlockSpec((None, tm, Fs), lambda i, j: (j, i, 0)),
                  pl.BlockSpec((None, tm, Fs), lambda i, j: (j, i, 0)),
                  pl.BlockSpec((None, D, Fs), lambda i, j: (j, 0, 0)),
                  pl.BlockSpec((None, D, Fs), lambda i, j: (j, 0, 0)),
                  pl.BlockSpec((None, Fs, D), lambda i, j: (j, 0, 0)),
                  pl.BlockSpec((tm, D), lambda i, j: (i, 0))],
        out_specs=[pl.BlockSpec((tm, D), lambda i, j: (i, 0)),
                   pl.BlockSpec((None, tm, Fs), lambda i, j: (j, i, 0)),
                   pl.BlockSpec((None, tm, Fs), lambda i, j: (j, i, 0)),
                   pl.BlockSpec((1, D), lambda i, j: (0, 0))],
        out_shape=[jax.ShapeDtypeStruct((T, D), F32),
                   jax.ShapeDtypeStruct((NP, T, Fs), BF16), jax.ShapeDtypeStruct((NP, T, Fs), BF16),
                   jax.ShapeDtypeStruct((1, D), F32)],
        scratch_shapes=[pltpu.VMEM((tm, D), F32), pltpu.VMEM((tm, D), BF16)],
        compiler_params=_params(("arbitrary", "arbitrary")),
    )(x, nw, g, u, wg, wu, wd, dout)


def ffn_bwd_w(h, g, u, dg, du, dout, name):
    T, D = h.shape
    NP, _, Fs = g.shape
    tk = min(TK_TN, T)

    def body(h_ref, g_ref, u_ref, dg_ref, du_ref, do_ref, dwg_ref, dwu_ref, dwd_ref):
        t = pl.program_id(1)

        @pl.when(t == 0)
        def _():
            dwg_ref[...] = jnp.zeros_like(dwg_ref)
            dwu_ref[...] = jnp.zeros_like(dwu_ref)
            dwd_ref[...] = jnp.zeros_like(dwd_ref)

        hb = h_ref[...]
        dwg_ref[...] += _dot_tn(hb, dg_ref[...])
        dwu_ref[...] += _dot_tn(hb, du_ref[...])
        gf = g_ref[...].astype(F32)
        a = ((gf * _sigmoid(gf)) * u_ref[...].astype(F32)).astype(BF16)
        dwd_ref[...] += _dot_tn(a, (0.5 * do_ref[...]).astype(BF16))

    piece = pl.BlockSpec((None, tk, Fs), lambda j, t: (j, t, 0))
    return pl.pallas_call(
        body, name=name, grid=(NP, T // tk),
        in_specs=[pl.BlockSpec((tk, D), lambda j, t: (t, 0)), piece, piece, piece, piece,
                  pl.BlockSpec((tk, D), lambda j, t: (t, 0))],
        out_specs=[pl.BlockSpec((None, D, Fs), lambda j, t: (j, 0, 0)),
                   pl.BlockSpec((None, D, Fs), lambda j, t: (j, 0, 0)),
                   pl.BlockSpec((None, Fs, D), lambda j, t: (j, 0, 0))],
        out_shape=[jax.ShapeDtypeStruct((NP, D, Fs), F32), jax.ShapeDtypeStruct((NP, D, Fs), F32),
                   jax.ShapeDtypeStruct((NP, Fs, D), F32)],
        compiler_params=_params(("parallel", "arbitrary")),
    )(h, g, u, dg, du, dout)


def rmsnorm_fwd(x, nw, name):
    T, D = x.shape
    tm = min(TM_MM, T)

    def body(x_ref, nw_ref, h_ref):
        xv = x_ref[...]
        h_ref[...] = (xv * _rms_scale(xv) * nw_ref[...]).astype(BF16)

    return pl.pallas_call(
        body, name=name, grid=(T // tm,),
        in_specs=[pl.BlockSpec((tm, D), lambda i: (i, 0)), pl.BlockSpec((1, D), lambda i: (0, 0))],
        out_specs=pl.BlockSpec((tm, D), lambda i: (i, 0)),
        out_shape=jax.ShapeDtypeStruct((T, D), BF16),
        compiler_params=_params(("parallel",)),
    )(x, nw)


def matmul_nn(a, w, name):
    T, K = a.shape
    N = w.shape[1]
    tm = min(TM_MM, T)

    def body(a_ref, w_ref, o_ref):
        o_ref[...] = _dot(a_ref[...], w_ref[...])

    return pl.pallas_call(
        body, name=name, grid=(T // tm,),
        in_specs=[pl.BlockSpec((tm, K), lambda i: (i, 0)), pl.BlockSpec((K, N), lambda i: (0, 0))],
        out_specs=pl.BlockSpec((tm, N), lambda i: (i, 0)),
        out_shape=jax.ShapeDtypeStruct((T, N), F32),
        compiler_params=_params(("parallel",)),
    )(a, w)


def matmul_tn(lhs, rhs, name):
    T, K = lhs.shape
    N = rhs.shape[1]
    tk = min(TK_TN, T)

    def body(l_ref, r_ref, o_ref):
        @pl.when(pl.program_id(0) == 0)
        def _():
            o_ref[...] = jnp.zeros_like(o_ref)

        o_ref[...] += _dot_tn(l_ref[...].astype(BF16), r_ref[...].astype(BF16))

    return pl.pallas_call(
        body, name=name, grid=(T // tk,),
        in_specs=[pl.BlockSpec((tk, K), lambda t: (t, 0)), pl.BlockSpec((tk, N), lambda t: (t, 0))],
        out_specs=pl.BlockSpec((K, N), lambda t: (0, 0)),
        out_shape=jax.ShapeDtypeStruct((K, N), F32),
        compiler_params=_params(("arbitrary",)),
    )(lhs, rhs)


def mix_in_bwd(dps, ws, x, nw, dres, name):
    T, D = x.shape
    tm = min(TM_ROW, T)
    n = len(dps)

    def body(*refs):
        dp_refs, w_refs = refs[:n], refs[n:2 * n]
        x_ref, nw_ref, dr_ref, dx_ref, dnw_ref = refs[2 * n:]

        @pl.when(pl.program_id(0) == 0)
        def _():
            dnw_ref[...] = jnp.zeros_like(dnw_ref)

        dh = _dot_nt(dp_refs[0][...], w_refs[0][...])
        for k in range(1, n):
            dh += _dot_nt(dp_refs[k][...], w_refs[k][...])
        dxn, dnw = _rms_bwd(x_ref[...], nw_ref[...], dh)
        dx_ref[...] = dr_ref[...] + dxn
        dnw_ref[...] += dnw

    in_specs = [pl.BlockSpec((tm, dp.shape[1]), lambda i: (i, 0)) for dp in dps]
    in_specs += [pl.BlockSpec(w.shape, lambda i: (0, 0)) for w in ws]
    in_specs += [pl.BlockSpec((tm, D), lambda i: (i, 0)), pl.BlockSpec((1, D), lambda i: (0, 0)),
                 pl.BlockSpec((tm, D), lambda i: (i, 0))]
    return pl.pallas_call(
        body, name=name, grid=(T // tm,), in_specs=in_specs,
        out_specs=[pl.BlockSpec((tm, D), lambda i: (i, 0)), pl.BlockSpec((1, D), lambda i: (0, 0))],
        out_shape=[jax.ShapeDtypeStruct((T, D), F32), jax.ShapeDtypeStruct((1, D), F32)],
        compiler_params=_params(("arbitrary",)),
    )(*dps, *ws, x, nw, dres)


def _layernorm_stats(c1):
    mu = jnp.mean(c1, axis=-1, keepdims=True)
    xc = c1 - mu
    rstd = lax.rsqrt(jnp.mean(xc * xc, axis=-1, keepdims=True) + LN_EPS)
    return xc * rstd, rstd


def _shifted_copies(src_ref, dst_ref):
    rows = dst_ref.shape[1]
    for b in range(1, 8):
        dst_ref[b - 1] = src_ref[pl.ds(b, rows), :]


def _shifted_rows(src_ref, shifted_ref, start, rows, cols):
    a8, b = divmod(start, 8)
    if b == 0:
        return src_ref[pl.ds(8 * a8, rows), cols]
    return shifted_ref[b - 1, pl.ds(8 * a8, rows), cols]


def conv_fwd(p_glu, dw_w, dw_b, ln_g, ln_b, name):
    T, D2 = p_glu.shape
    D = D2 // 2
    tm = min(TM_ROW, T)
    hb = tm // CONV_HALO

    def body(a_ref, b_ref, ah_ref, bh_ref, w_ref, wb_ref, g_ref, be_ref, c1_ref, c3_ref, e_ref, es_ref):
        i = pl.program_id(0)
        halo = ah_ref[...] * _sigmoid(bh_ref[...])
        e_ref[pl.ds(0, CONV_HALO), :] = jnp.where(i > 0, halo, 0.0)
        e_ref[pl.ds(CONV_HALO, tm), :] = a_ref[...] * _sigmoid(b_ref[...])
        _shifted_copies(e_ref, es_ref)
        off = CONV_HALO - (CONV_WIDTH - 1)

        def strip(s, carry):
            cols = pl.ds(pl.multiple_of(s * 128, 128), 128)
            acc = jnp.zeros((tm, 128), F32) + wb_ref[:, cols]
            for k in range(CONV_WIDTH):
                acc += w_ref[pl.ds(k, 1), cols] * _shifted_rows(e_ref, es_ref, off + k, tm, cols)
            c1_ref[:, cols] = acc
            return carry

        lax.fori_loop(0, D // 128, strip, 0)
        xhat, _ = _layernorm_stats(c1_ref[...])
        c2 = xhat * g_ref[...] + be_ref[...]
        c3_ref[...] = (c2 * _sigmoid(c2)).astype(BF16)

    row = pl.BlockSpec((1, D), lambda i: (0, 0))
    return pl.pallas_call(
        body, name=name, grid=(T // tm,),
        in_specs=[pl.BlockSpec((tm, D), lambda i: (i, 0)), pl.BlockSpec((tm, D), lambda i: (i, 1)),
                  pl.BlockSpec((CONV_HALO, D), lambda i: (jnp.maximum(i * hb - 1, 0), 0)),
                  pl.BlockSpec((CONV_HALO, D), lambda i: (jnp.maximum(i * hb - 1, 0), 1)),
                  pl.BlockSpec((CONV_HALO, D), lambda i: (0, 0)), row, row, row],
        out_specs=[pl.BlockSpec((tm, D), lambda i: (i, 0)), pl.BlockSpec((tm, D), lambda i: (i, 0))],
        out_shape=[jax.ShapeDtypeStruct((T, D), F32), jax.ShapeDtypeStruct((T, D), BF16)],
        scratch_shapes=[pltpu.VMEM((tm + CONV_HALO, D), F32), pltpu.VMEM((7, tm + CONV_HALO - 8, D), F32)],
        compiler_params=_params(("parallel",)),
    )(p_glu, p_glu, p_glu, p_glu, dw_w, dw_b, ln_g, ln_b)


def conv_bwd(p_glu, dc1, dw_w, name):
    T, D2 = p_glu.shape
    D = D2 // 2
    tm = min(TM_ROW, T)
    hb = tm // CONV_HALO
    last = T // CONV_HALO - 1
    nblk = T // tm

    def body(a_ref, b_ref, ah_ref, bh_ref, d_ref, dn_ref, w_ref, dp_ref, dw_ref, e_ref, f_ref, es_ref, fs_ref):
        i = pl.program_id(0)

        @pl.when(i == 0)
        def _():
            dw_ref[...] = jnp.zeros_like(dw_ref)

        halo = ah_ref[...] * _sigmoid(bh_ref[...])
        e_ref[pl.ds(0, CONV_HALO), :] = jnp.where(i > 0, halo, 0.0)
        e_ref[pl.ds(CONV_HALO, tm), :] = a_ref[...] * _sigmoid(b_ref[...])
        f_ref[pl.ds(0, tm), :] = d_ref[...]
        f_ref[pl.ds(tm, CONV_HALO), :] = jnp.where(i < nblk - 1, dn_ref[...], 0.0)
        _shifted_copies(e_ref, es_ref)
        _shifted_copies(f_ref, fs_ref)
        off = CONV_HALO - (CONV_WIDTH - 1)

        def strip(s, carry):
            cols = pl.ds(pl.multiple_of(s * 128, 128), 128)
            d = d_ref[:, cols]
            dc0 = jnp.zeros((tm, 128), F32)
            for k in range(CONV_WIDTH):
                dw_ref[pl.ds(k, 1), cols] += jnp.sum(d * _shifted_rows(e_ref, es_ref, off + k, tm, cols),
                                                     axis=0, keepdims=True)
                dc0 += w_ref[pl.ds(k, 1), cols] * _shifted_rows(f_ref, fs_ref, CONV_WIDTH - 1 - k, tm, cols)
            a = a_ref[:, cols]
            sb = _sigmoid(b_ref[:, cols])
            dp_ref[:, cols] = (dc0 * sb).astype(BF16)
            dp_ref[:, pl.ds(pl.multiple_of(D + s * 128, 128), 128)] = (dc0 * a * sb * (1.0 - sb)).astype(BF16)
            return carry

        lax.fori_loop(0, D // 128, strip, 0)

    return pl.pallas_call(
        body, name=name, grid=(nblk,),
        in_specs=[pl.BlockSpec((tm, D), lambda i: (i, 0)), pl.BlockSpec((tm, D), lambda i: (i, 1)),
                  pl.BlockSpec((CONV_HALO, D), lambda i: (jnp.maximum(i * hb - 1, 0), 0)),
                  pl.BlockSpec((CONV_HALO, D), lambda i: (jnp.maximum(i * hb - 1, 0), 1)),
                  pl.BlockSpec((tm, D), lambda i: (i, 0)),
                  pl.BlockSpec((CONV_HALO, D), lambda i: (jnp.minimum((i + 1) * hb, last), 0)),
                  pl.BlockSpec((CONV_HALO, D), lambda i: (0, 0))],
        out_specs=[pl.BlockSpec((tm, D2), lambda i: (i, 0)), pl.BlockSpec((CONV_HALO, D), lambda i: (0, 0))],
        out_shape=[jax.ShapeDtypeStruct((T, D2), BF16), jax.ShapeDtypeStruct((CONV_HALO, D), F32)],
        scratch_shapes=[pltpu.VMEM((tm + CONV_HALO, D), F32), pltpu.VMEM((tm + CONV_HALO, D), F32),
                        pltpu.VMEM((7, tm + CONV_HALO - 8, D), F32), pltpu.VMEM((7, tm + CONV_HALO - 8, D), F32)],
        compiler_params=_params(("arbitrary",)),
    )(p_glu, p_glu, p_glu, p_glu, dc1, dc1, dw_w)


def _rot_half(x):
    lane = lax.broadcasted_iota(jnp.int32, x.shape, 1)
    first = (lane % HEAD_DIM) < HEAD_DIM // 2
    return jnp.where(first, pltpu.roll(x, 128 - HEAD_DIM // 2, 1), pltpu.roll(x, HEAD_DIM // 2, 1))


def _rope_chunks(x, cs, sn, sign):
    outs = []
    for c in range(x.shape[1] // 128):
        xc = x[:, c * 128:(c + 1) * 128]
        outs.append(xc * cs + sign * (_rot_half(xc) * sn))
    return outs[0] if len(outs) == 1 else jnp.concatenate(outs, axis=1)


def rope_fwd(p_qkv, cs, sn, D, name):
    T, W = p_qkv.shape
    KV = (W - D) // 2
    tm = min(TM_ROW, T)
    kb = D // KV

    def body(q_ref, k_ref, v_ref, cs_ref, sn_ref, qo_ref, ko_ref, vo_ref):
        cs_v, sn_v = cs_ref[...], sn_ref[...]
        qo_ref[...] = _rope_chunks(q_ref[...], cs_v, sn_v, 1.0).astype(BF16)
        ko_ref[...] = _rope_chunks(k_ref[...], cs_v, sn_v, 1.0).astype(BF16)
        vo_ref[...] = v_ref[...].astype(BF16)

    tab = pl.BlockSpec((tm, 128), lambda i: (i, 0))
    return pl.pallas_call(
        body, name=name, grid=(T // tm,),
        in_specs=[pl.BlockSpec((tm, D), lambda i: (i, 0)), pl.BlockSpec((tm, KV), lambda i: (i, kb)),
                  pl.BlockSpec((tm, KV), lambda i: (i, kb + 1)), tab, tab],
        out_specs=[pl.BlockSpec((tm, D), lambda i: (i, 0)), pl.BlockSpec((tm, KV), lambda i: (i, 0)),
                   pl.BlockSpec((tm, KV), lambda i: (i, 0))],
        out_shape=[jax.ShapeDtypeStruct((T, D), BF16), jax.ShapeDtypeStruct((T, KV), BF16),
                   jax.ShapeDtypeStruct((T, KV), BF16)],
        compiler_params=_params(("parallel",)),
    )(p_qkv, p_qkv, p_qkv, cs, sn)


def rope_bwd(dq, dk, dv, cs, sn, name):
    T, D = dq.shape
    KV = dk.shape[1]
    tm = min(TM_ROW, T)

    def body(dq_ref, dk_ref, dv_ref, cs_ref, sn_ref, o_ref):
        cs_v, sn_v = cs_ref[...], sn_ref[...]
        o_ref[:, pl.ds(0, D)] = _rope_chunks(dq_ref[...], cs_v, sn_v, -1.0).astype(BF16)
        o_ref[:, pl.ds(D, KV)] = _rope_chunks(dk_ref[...], cs_v, sn_v, -1.0).astype(BF16)
        o_ref[:, pl.ds(D + KV, KV)] = dv_ref[...].astype(BF16)

    tab = pl.BlockSpec((tm, 128), lambda i: (i, 0))
    return pl.pallas_call(
        body, name=name, grid=(T // tm,),
        in_specs=[pl.BlockSpec((tm, D), lambda i: (i, 0)), pl.BlockSpec((tm, KV), lambda i: (i, 0)),
                  pl.BlockSpec((tm, KV), lambda i: (i, 0)), tab, tab],
        out_specs=pl.BlockSpec((tm, D + 2 * KV), lambda i: (i, 0)),
        out_shape=jax.ShapeDtypeStruct((T, D + 2 * KV), BF16),
        compiler_params=_params(("parallel",)),
    )(dq, dk, dv, cs, sn)


def _lane_lo():
    return lax.broadcasted_iota(jnp.int32, (1, 128), 1) < HEAD_DIM


def _band_mask(i, reps):
    shape = (reps * WINDOW, 2 * WINDOW)
    qi = lax.broadcasted_iota(jnp.int32, shape, 0) % WINDOW
    cj = lax.broadcasted_iota(jnp.int32, shape, 1)
    rel = qi - cj + WINDOW
    return (rel >= 0) & (rel < WINDOW) & ((i > 0) | (cj >= WINDOW))


def _stack_pairs(ref, first, n):
    parts = [ref[:, pl.ds((first + p) * 128, 128)] for p in range(n)]
    return parts[0] if n == 1 else jnp.concatenate(parts, axis=0)


def _pair_rows(n):
    return lax.broadcasted_iota(jnp.int32, (n * WINDOW, 1), 0) // WINDOW


def _per_pair_column(values, n):
    rows = _pair_rows(n)
    col = jnp.zeros((n * WINDOW, 1), F32) + values[0]
    for p in range(1, n):
        col = jnp.where(rows == p, values[p], col)
    return col


def _kv_lo_hi(x2, g):
    pair, half = divmod(g, 2)
    lo = _lane_lo()
    xg = x2[:, pair * 128:(pair + 1) * 128].astype(F32)
    xg = jnp.where(lo if half == 0 else ~lo, xg, 0.0)
    sw = pltpu.roll(xg, HEAD_DIM, 1)
    x_lo, x_hi = (xg, sw) if half == 0 else (sw, xg)
    return x_lo.astype(BF16), x_hi.astype(BF16)


def _softmax_sink(s, allowed, sink):
    s = jnp.where(allowed, s * (HEAD_DIM ** -0.5), NEG_INF)
    m = jnp.maximum(jnp.max(s, axis=-1, keepdims=True), sink)
    p = jnp.exp(s - m)
    es = jnp.exp(sink - m)
    inv = 1.0 / (jnp.sum(p, axis=-1, keepdims=True) + es)
    return p * inv, es * inv


def attn_fwd(qr, kr, vb, sinks, name):
    T, D = qr.shape
    KV = kr.shape[1]
    n_kv = KV // HEAD_DIM
    group = (D // HEAD_DIM) // n_kv
    nb = T // WINDOW

    npair = group // 2

    def body(sink_ref, q_ref, kp_ref, kc_ref, vp_ref, vc_ref, o_ref):
        i = pl.program_id(0)
        allowed = _band_mask(i, npair)
        k2 = jnp.concatenate([kp_ref[...], kc_ref[...]], axis=0)
        v2 = jnp.concatenate([vp_ref[...], vc_ref[...]], axis=0)
        outs = [None] * (D // 128)
        for g in range(n_kv):
            k_lo, k_hi = _kv_lo_hi(k2, g)
            v_lo, v_hi = _kv_lo_hi(v2, g)
            first = (g * group) // 2
            q = _stack_pairs(q_ref, first, npair)
            sink_e = _per_pair_column([sink_ref[0, g * group + 2 * p] for p in range(npair)], npair)
            sink_o = _per_pair_column([sink_ref[0, g * group + 2 * p + 1] for p in range(npair)], npair)
            pe, _ = _softmax_sink(_dot_nt(q, k_lo), allowed, sink_e)
            po, _ = _softmax_sink(_dot_nt(q, k_hi), allowed, sink_o)
            o = _dot(pe.astype(BF16), v_lo) + _dot(po.astype(BF16), v_hi)
            for p in range(npair):
                outs[first + p] = o[p * WINDOW:(p + 1) * WINDOW]
        o_ref[...] = jnp.concatenate(outs, axis=1).astype(BF16)

    prev = lambda i: (jnp.maximum(i - 1, 0), 0)
    cur = lambda i: (i, 0)
    return pl.pallas_call(
        body, name=name, grid=(nb,),
        in_specs=[pl.BlockSpec(memory_space=pltpu.SMEM),
                  pl.BlockSpec((WINDOW, D), cur),
                  pl.BlockSpec((WINDOW, KV), prev), pl.BlockSpec((WINDOW, KV), cur),
                  pl.BlockSpec((WINDOW, KV), prev), pl.BlockSpec((WINDOW, KV), cur)],
        out_specs=pl.BlockSpec((WINDOW, D), cur),
        out_shape=jax.ShapeDtypeStruct((T, D), BF16),
        compiler_params=_params(("parallel",)),
    )(sinks, qr, kr, kr, vb, vb)


def attn_bwd(qr, kr, vb, o, do, sinks, name):
    T, D = qr.shape
    KV = kr.shape[1]
    n_heads = D // HEAD_DIM
    n_kv = KV // HEAD_DIM
    group = n_heads // n_kv
    nb = T // WINDOW
    npair = group // 2
    scale = HEAD_DIM ** -0.5

    def body(sink_ref, q_ref, kp_ref, kc_ref, vp_ref, vc_ref, o_ref, do_ref,
             dq_ref, dk_ref, dv_ref, ds_ref, ck_ref, cv_ref):
        i = pl.program_id(0)
        lo = _lane_lo()

        @pl.when(i == 0)
        def _():
            ck_ref[...] = jnp.zeros_like(ck_ref)
            cv_ref[...] = jnp.zeros_like(cv_ref)
            ds_ref[...] = jnp.zeros_like(ds_ref)

        @pl.when(i < nb)
        def _():
            allowed = _band_mask(i, npair)
            rows = _pair_rows(npair)
            k2 = jnp.concatenate([kp_ref[...], kc_ref[...]], axis=0)
            v2 = jnp.concatenate([vp_ref[...], vc_ref[...]], axis=0)
            lane = lax.broadcasted_iota(jnp.int32, (1, 128), 1)
            dsink = jnp.zeros((1, 128), F32)
            dq_out = [None] * (D // 128)
            dk_pairs = [jnp.zeros((2 * WINDOW, 128), F32) for _ in range(KV // 128)]
            dv_pairs = [jnp.zeros((2 * WINDOW, 128), F32) for _ in range(KV // 128)]
            for g in range(n_kv):
                k_lo, k_hi = _kv_lo_hi(k2, g)
                v_lo, v_hi = _kv_lo_hi(v2, g)
                first = (g * group) // 2
                q = _stack_pairs(q_ref, first, npair)
                dop = _stack_pairs(do_ref, first, npair)
                dd = dop.astype(F32) * _stack_pairs(o_ref, first, npair).astype(F32)
                dq = jnp.zeros((npair * WINDOW, 128), F32)
                dkg = jnp.zeros((2 * WINDOW, 128), F32)
                dvg = jnp.zeros((2 * WINDOW, 128), F32)
                for parity, k_h, v_h, sel in ((0, k_lo, v_lo, lo), (1, k_hi, v_hi, ~lo)):
                    heads = [g * group + 2 * p + parity for p in range(npair)]
                    sink = _per_pair_column([sink_ref[0, h] for h in heads], npair)
                    p_, ps = _softmax_sink(_dot_nt(q, k_h), allowed, sink)
                    delta = jnp.sum(jnp.where(sel, dd, 0.0), axis=-1, keepdims=True)
                    dsc = (p_ * (_dot_nt(dop, v_h) - delta)).astype(BF16)
                    sd = -ps * delta
                    for p, h in enumerate(heads):
                        dsink += jnp.where(lane == h, jnp.sum(jnp.where(rows == p, sd, 0.0)), 0.0)
                    dq += _dot(dsc, k_h)
                    dkg += jnp.where(sel, _dot_tn(dsc, q), 0.0)
                    dvg += jnp.where(sel, _dot_tn(p_.astype(BF16), dop), 0.0)
                for p in range(npair):
                    dq_out[first + p] = dq[p * WINDOW:(p + 1) * WINDOW]
                pair, half = divmod(g, 2)
                keep = lo if half == 0 else ~lo
                dk_pairs[pair] += jnp.where(keep, dkg + pltpu.roll(dkg, HEAD_DIM, 1), 0.0) * scale
                dv_pairs[pair] += jnp.where(keep, dvg + pltpu.roll(dvg, HEAD_DIM, 1), 0.0)
            dq_ref[...] = jnp.concatenate(dq_out, axis=1) * scale
            dk2 = dk_pairs[0] if len(dk_pairs) == 1 else jnp.concatenate(dk_pairs, axis=1)
            dv2 = dv_pairs[0] if len(dv_pairs) == 1 else jnp.concatenate(dv_pairs, axis=1)
            dk_ref[...] = ck_ref[...] + dk2[:WINDOW]
            dv_ref[...] = cv_ref[...] + dv2[:WINDOW]
            ck_ref[...] = dk2[WINDOW:]
            cv_ref[...] = dv2[WINDOW:]
            ds_ref[pl.ds(0, 1), :] += dsink

        @pl.when(i == nb)
        def _():
            dk_ref[...] = ck_ref[...]
            dv_ref[...] = cv_ref[...]

    prev = lambda i: (jnp.maximum(i - 1, 0), 0)
    cur = lambda i: (jnp.minimum(i, nb - 1), 0)
    prevc = lambda i: (jnp.maximum(jnp.minimum(i, nb - 1) - 1, 0), 0)
    return pl.pallas_call(
        body, name=name, grid=(nb + 1,),
        in_specs=[pl.BlockSpec(memory_space=pltpu.SMEM),
                  pl.BlockSpec((WINDOW, D), cur),
                  pl.BlockSpec((WINDOW, KV), prevc), pl.BlockSpec((WINDOW, KV), cur),
                  pl.BlockSpec((WINDOW, KV), prevc), pl.BlockSpec((WINDOW, KV), cur),
                  pl.BlockSpec((WINDOW, D), cur), pl.BlockSpec((WINDOW, D), cur)],
        out_specs=[pl.BlockSpec((WINDOW, D), cur), pl.BlockSpec((WINDOW, KV), prev),
                   pl.BlockSpec((WINDOW, KV), prev), pl.BlockSpec((8, 128), lambda i: (0, 0))],
        out_shape=[jax.ShapeDtypeStruct((T, D), F32), jax.ShapeDtypeStruct((T, KV), F32),
                   jax.ShapeDtypeStruct((T, KV), F32), jax.ShapeDtypeStruct((8, 128), F32)],
        scratch_shapes=[pltpu.VMEM((WINDOW, KV), F32), pltpu.VMEM((WINDOW, KV), F32)],
        compiler_params=_params(("arbitrary",)),
    )(sinks, qr, kr, kr, vb, vb, o, do)


def merge_fwd(x, c3, o, p_gate, gate_b, w_proj, w_o, w_out, name):
    T, D = x.shape
    tm = min(TM_ROW, T)

    def body(x_ref, c3_ref, o_ref, gc_ref, ga_ref, bc_ref, ba_ref, wp_ref, wo_ref, wout_ref,
             xo_ref, co_ref, ao_ref, mg_ref):
        conv_out = _dot(c3_ref[...], wp_ref[...])
        attn_out = _dot(o_ref[...], wo_ref[...])
        merged = (_sigmoid(gc_ref[...] + bc_ref[...]) * conv_out
                  + _sigmoid(ga_ref[...] + ba_ref[...]) * attn_out).astype(BF16)
        co_ref[...] = conv_out.astype(BF16)
        ao_ref[...] = attn_out.astype(BF16)
        mg_ref[...] = merged
        xo_ref[...] = x_ref[...] + _dot(merged, wout_ref[...])

    blk = lambda j: pl.BlockSpec((tm, D), lambda i: (i, j))
    row = lambda j: pl.BlockSpec((1, D), lambda i: (0, j))
    mat = pl.BlockSpec((D, D), lambda i: (0, 0))
    return pl.pallas_call(
        body, name=name, grid=(T // tm,),
        in_specs=[blk(0), blk(0), blk(0), blk(0), blk(1), row(0), row(1), mat, mat, mat],
        out_specs=[blk(0), blk(0), blk(0), blk(0)],
        out_shape=[jax.ShapeDtypeStruct((T, D), F32)] + [jax.ShapeDtypeStruct((T, D), BF16)] * 3,
        compiler_params=_params(("parallel",)),
    )(x, c3, o, p_gate, p_gate, gate_b, gate_b, w_proj, w_o, w_out)


def merge_bwd(dx, p_gate, gate_b, conv_out, attn_out, c1, ln_g, ln_b, w_proj, w_o, w_out, name):
    T, D = dx.shape
    tm = min(TM_ROW, T)

    def body(dx_ref, gc_ref, ga_ref, bc_ref, ba_ref, co_ref, ao_ref, c1_ref, g_ref, be_ref,
             wp_ref, wo_ref, wout_ref, dgt_ref, dco_ref, dao_ref, do_ref, dc1_ref, sm_ref):
        @pl.when(pl.program_id(0) == 0)
        def _():
            sm_ref[...] = jnp.zeros_like(sm_ref)

        dm = _dot_nt(dx_ref[...].astype(BF16), wout_ref[...])
        sc = _sigmoid(gc_ref[...] + bc_ref[...])
        sa = _sigmoid(ga_ref[...] + ba_ref[...])
        dco = (dm * sc).astype(BF16)
        dao = (dm * sa).astype(BF16)
        dgc = dm * co_ref[...].astype(F32) * sc * (1.0 - sc)
        dga = dm * ao_ref[...].astype(F32) * sa * (1.0 - sa)
        dgt_ref[:, pl.ds(0, D)] = dgc.astype(BF16)
        dgt_ref[:, pl.ds(D, D)] = dga.astype(BF16)
        dco_ref[...] = dco
        dao_ref[...] = dao
        do_ref[...] = _dot_nt(dao, wo_ref[...]).astype(BF16)
        dc3 = _dot_nt(dco, wp_ref[...])
        xhat, rstd = _layernorm_stats(c1_ref[...])
        c2 = xhat * g_ref[...] + be_ref[...]
        dc2 = dc3 * _silu_grad(c2, _sigmoid(c2))
        dxh = dc2 * g_ref[...]
        dc1 = rstd * (dxh - jnp.mean(dxh, axis=-1, keepdims=True)
                      - xhat * jnp.mean(dxh * xhat, axis=-1, keepdims=True))
        dc1_ref[...] = dc1
        colsum = lambda v: jnp.sum(v, axis=0, keepdims=True)
        for r, (left, right) in enumerate(((dgc, dga), (dc2 * xhat, dc2), (dc1, None))):
            sm_ref[pl.ds(r, 1), pl.ds(0, D)] += colsum(left)
            if right is not None:
                sm_ref[pl.ds(r, 1), pl.ds(D, D)] += colsum(right)

    blk = lambda j: pl.BlockSpec((tm, D), lambda i: (i, j))
    row = lambda j: pl.BlockSpec((1, D), lambda i: (0, j))
    mat = pl.BlockSpec((D, D), lambda i: (0, 0))
    return pl.pallas_call(
        body, name=name, grid=(T // tm,),
        in_specs=[blk(0), blk(0), blk(1), row(0), row(1), blk(0), blk(0), blk(0), row(0), row(0), mat, mat, mat],
        out_specs=[pl.BlockSpec((tm, 2 * D), lambda i: (i, 0)), blk(0), blk(0), blk(0), blk(0),
                   pl.BlockSpec((8, 2 * D), lambda i: (0, 0))],
        out_shape=[jax.ShapeDtypeStruct((T, 2 * D), BF16)] + [jax.ShapeDtypeStruct((T, D), BF16)] * 3
                  + [jax.ShapeDtypeStruct((T, D), F32), jax.ShapeDtypeStruct((8, 2 * D), F32)],
        compiler_params=_params(("arbitrary",)),
    )(dx, p_gate, p_gate, gate_b, gate_b, conv_out, attn_out, c1, ln_g, ln_b, w_proj, w_o, w_out)


def loss_head(x, nw, target, name):
    T, D = x.shape
    tm = min(TM_ROW, T)

    def body(x_ref, nw_ref, t_ref, dx_ref, sm_ref):
        @pl.when(pl.program_id(0) == 0)
        def _():
            sm_ref[...] = jnp.zeros_like(sm_ref)

        xv = x_ref[...]
        err = xv * _rms_scale(xv) * nw_ref[...] - t_ref[...]
        loss = 0.5 * jnp.sum(jnp.mean(err * err, axis=-1, keepdims=True))
        dxn, dnw = _rms_bwd(xv, nw_ref[...], err * (1.0 / D))
        dx_ref[...] = dxn
        sm_ref[pl.ds(0, 1), :] += dnw
        sm_ref[pl.ds(1, 1), :] += jnp.zeros((1, D), F32) + loss

    return pl.pallas_call(
        body, name=name, grid=(T // tm,),
        in_specs=[pl.BlockSpec((tm, D), lambda i: (i, 0)), pl.BlockSpec((1, D), lambda i: (0, 0)),
                  pl.BlockSpec((tm, D), lambda i: (i, 0))],
        out_specs=[pl.BlockSpec((tm, D), lambda i: (i, 0)), pl.BlockSpec((8, D), lambda i: (0, 0))],
        out_shape=[jax.ShapeDtypeStruct((T, D), F32), jax.ShapeDtypeStruct((8, D), F32)],
        compiler_params=_params(("arbitrary",)),
    )(x, nw, target)


def adamw(w, g, m, v, name):
    R, C = w.shape
    tr = _row_tile(R, TR_ELT)

    def body(w_ref, g_ref, m_ref, v_ref, d_ref, mo_ref, vo_ref):
        gv = g_ref[...]
        mn = ADAM_B1 * m_ref[...] + (1.0 - ADAM_B1) * gv
        vn = ADAM_B2 * v_ref[...] + (1.0 - ADAM_B2) * (gv * gv)
        m_hat = mn / (1.0 - ADAM_B1 ** ADAM_STEP)
        v_hat = vn / (1.0 - ADAM_B2 ** ADAM_STEP)
        d_ref[...] = -ADAM_LR * (m_hat / (jnp.sqrt(v_hat) + ADAM_EPS) + ADAM_WD * w_ref[...])
        mo_ref[...] = mn
        vo_ref[...] = vn

    spec = pl.BlockSpec((tr, C), lambda i: (i, 0))
    return pl.pallas_call(
        body, name=name, grid=(R // tr,), in_specs=[spec] * 4, out_specs=[spec] * 3,
        out_shape=[jax.ShapeDtypeStruct((R, C), F32)] * 3,
        compiler_params=_params(("parallel",)),
    )(w, g, m, v)


HBM_SPEC = pl.BlockSpec(memory_space=pl.ANY)


def _place():
    return lax.axis_index("x"), lax.axis_index("y"), lax.axis_index("c")


def place_shard(place, w, dtype, name):
    R, C = w.shape
    tr = _row_tile(R, TR_ELT)

    def body(pc_ref, w_ref, o_ref):
        o_ref[...] = w_ref[...].astype(dtype)

    return pl.pallas_call(
        body, name=name,
        grid_spec=pltpu.PrefetchScalarGridSpec(
            num_scalar_prefetch=1, grid=(R // tr,),
            in_specs=[pl.BlockSpec((tr, C), lambda r, pc: (r, 0))],
            out_specs=pl.BlockSpec((None, tr, C), lambda r, pc: (pc[0], r, 0))),
        out_shape=jax.ShapeDtypeStruct((N_CHIPS, R, C), dtype),
        compiler_params=_params(("arbitrary",)),
    )(place, w)


def gather_weights(shards, small):
    n, ns = len(shards), len(small)

    def body(*refs):
        dst = refs[n + ns:2 * (n + ns)]
        ici_send, ici_recv, d2d_send, d2d_recv = refs[2 * (n + ns):]
        x, y, c = _place()
        me = 2 * x + y
        sends, fwds = [], []
        for k in range(n + ns):
            half = dst[k].shape[1] // 2
            for j in (1, 2, 3):
                to = (x ^ (j >> 1), y ^ (j & 1), c)
                mine = dst[k].at[me, pl.ds(c * half, half)] if k < n else dst[k].at[me]
                cp = pltpu.make_async_remote_copy(mine, mine, ici_send.at[3 * k + j - 1], ici_recv.at[3 * k + j - 1],
                                                  device_id=to, device_id_type=MESH)
                cp.start()
                sends.append(cp)
        for k in range(n + ns):
            half = dst[k].shape[1] // 2
            for j in (1, 2, 3):
                frm = 2 * (x ^ (j >> 1)) + (y ^ (j & 1))
                if k < n:
                    rows = pl.ds(c * half, half)
                    got = dst[k].at[frm, rows]
                    pltpu.make_async_remote_copy(got, got, ici_send.at[3 * k + j - 1], ici_recv.at[3 * k + j - 1],
                                                 device_id=(x, y, c), device_id_type=MESH).wait_recv()
                    cp = pltpu.make_async_remote_copy(got, got, d2d_send.at[3 * k + j - 1], d2d_recv.at[3 * k + j - 1],
                                                      device_id=(x, y, 1 - c), device_id_type=MESH)
                    cp.start()
                    fwds.append(cp)
                else:
                    got = dst[k].at[frm]
                    pltpu.make_async_remote_copy(got, got, ici_send.at[3 * k + j - 1], ici_recv.at[3 * k + j - 1],
                                                 device_id=(x, y, c), device_id_type=MESH).wait_recv()
        for k in range(n):
            half = dst[k].shape[1] // 2
            for j in (1, 2, 3):
                frm = 2 * (x ^ (j >> 1)) + (y ^ (j & 1))
                got = dst[k].at[frm, pl.ds((1 - c) * half, half)]
                pltpu.make_async_remote_copy(got, got, d2d_send.at[3 * k + j - 1], d2d_recv.at[3 * k + j - 1],
                                             device_id=(x, y, c), device_id_type=MESH).wait_recv()
        for cp in sends + fwds:
            cp.wait_send()

    arrays = list(shards) + list(small)
    return pl.pallas_call(
        body, name="gather_weights",
        in_specs=[HBM_SPEC] * (n + ns), out_specs=[HBM_SPEC] * (n + ns),
        out_shape=[jax.ShapeDtypeStruct(a.shape, a.dtype) for a in arrays],
        input_output_aliases={k: k for k in range(n + ns)},
        scratch_shapes=[pltpu.SemaphoreType.DMA((3 * (n + ns),)), pltpu.SemaphoreType.DMA((3 * (n + ns),)),
                        pltpu.SemaphoreType.DMA((3 * n,)), pltpu.SemaphoreType.DMA((3 * n,))],
    )(*arrays)


def allreduce_small(block):
    R, C = block.shape

    def body(x_ref, out_ref, all_ref, send_sems, recv_sems, local_sem):
        x, y, c = _place()
        me, sibling = (x, y, c), (x, y, 1 - c)
        chips = [(1 - x, y), (x, 1 - y), (1 - x, 1 - y)]

        def slot(px, py, pc):
            return all_ref.at[4 * px + 2 * py + pc]

        def copy(k, block_of, to, src=None):
            return pltpu.make_async_remote_copy(
                src_ref=slot(*block_of) if src is None else src, dst_ref=slot(*block_of),
                send_sem=send_sems.at[k], recv_sem=recv_sems.at[k], device_id=to, device_id_type=MESH)

        mine = pltpu.make_async_copy(x_ref, slot(*me), local_sem)
        mine.start()
        first = [copy(0, me, sibling, src=x_ref)]
        first += [copy(1 + j, me, (*chip, c), src=x_ref) for j, chip in enumerate(chips)]
        for cp in first:
            cp.start()
        passed = [copy(4 + j, (*chip, c), sibling) for j, chip in enumerate(chips)]
        for j, chip in enumerate(chips):
            copy(1 + j, (*chip, c), me).wait_recv()
            passed[j].start()
        copy(0, sibling, me).wait_recv()
        for j, chip in enumerate(chips):
            copy(4 + j, (*chip, 1 - c), me).wait_recv()
        for cp in first + passed:
            cp.wait_send()
        mine.wait()
        total = all_ref[0]
        for d in range(1, N_DEV):
            total = total + all_ref[d]
        out_ref[...] = total

    return pl.pallas_call(
        body, name="allreduce_small",
        in_specs=[pl.BlockSpec(memory_space=pltpu.VMEM)], out_specs=pl.BlockSpec(memory_space=pltpu.VMEM),
        out_shape=jax.ShapeDtypeStruct((R, C), F32),
        scratch_shapes=[pltpu.VMEM((N_DEV, R, C), F32), pltpu.SemaphoreType.DMA((7,)),
                        pltpu.SemaphoreType.DMA((7,)), pltpu.SemaphoreType.DMA],
        compiler_params=pltpu.CompilerParams(vmem_limit_bytes=VMEM_LIMIT),
    )(block)


def rs_exchange_siblings(grads):
    n = len(grads)

    def body(*refs):
        src, dst = refs[:n], refs[n:2 * n]
        send_sems, recv_sems = refs[2 * n:]
        x, y, c = _place()
        copies = []
        for k in range(n):
            half = src[k].shape[1] // 2
            cp = pltpu.make_async_remote_copy(src[k].at[:, pl.ds((1 - c) * half, half)], dst[k],
                                              send_sems.at[k], recv_sems.at[k],
                                              device_id=(x, y, 1 - c), device_id_type=MESH)
            cp.start()
            copies.append(cp)
        for cp in copies:
            cp.wait()

    return pl.pallas_call(
        body, name="rs_exchange_siblings",
        in_specs=[HBM_SPEC] * n, out_specs=[HBM_SPEC] * n,
        out_shape=[jax.ShapeDtypeStruct((N_CHIPS, g.shape[1] // 2, g.shape[2]), F32) for g in grads],
        scratch_shapes=[pltpu.SemaphoreType.DMA((n,)), pltpu.SemaphoreType.DMA((n,))],
    )(*grads)


def rs_chip_sum(place, grad, sib, name):
    NP, R, C = grad.shape
    half = R // 2
    tr = _row_tile(half, TR_ELT)
    nr = half // tr

    def body(pc_ref, g_ref, s_ref, wire_ref, own_ref):
        q = pl.program_id(1)
        total = g_ref[...] + s_ref[...]
        wire_ref[...] = total.astype(BF16)

        @pl.when(q == pc_ref[0])
        def _():
            own_ref[...] = total

    return pl.pallas_call(
        body, name=name,
        grid_spec=pltpu.PrefetchScalarGridSpec(
            num_scalar_prefetch=1, grid=(nr, NP),
            in_specs=[pl.BlockSpec((None, tr, C), lambda r, q, pc: (q, pc[1] * nr + r, 0)),
                      pl.BlockSpec((None, tr, C), lambda r, q, pc: (q, r, 0))],
            out_specs=[pl.BlockSpec((None, tr, C), lambda r, q, pc: (q, r, 0)),
                       pl.BlockSpec((tr, C), lambda r, q, pc: (r, 0))]),
        out_shape=[jax.ShapeDtypeStruct((NP, half, C), BF16), jax.ShapeDtypeStruct((half, C), F32)],
        compiler_params=_params(("arbitrary", "arbitrary")),
    )(place, grad, sib)


def rs_exchange_chips(wires):
    n = len(wires)

    def body(*refs):
        src, dst = refs[:n], refs[n:2 * n]
        send_sems, recv_sems = refs[2 * n:]
        x, y, c = _place()
        me = 2 * x + y
        copies = []
        for k in range(n):
            for j in (1, 2, 3):
                qx, qy = x ^ (j >> 1), y ^ (j & 1)
                cp = pltpu.make_async_remote_copy(src[k].at[2 * qx + qy], dst[k].at[me],
                                                  send_sems.at[3 * k + j - 1], recv_sems.at[3 * k + j - 1],
                                                  device_id=(qx, qy, c), device_id_type=MESH)
                cp.start()
                copies.append(cp)
        for cp in copies:
            cp.wait()

    return pl.pallas_call(
        body, name="rs_exchange_chips",
        in_specs=[HBM_SPEC] * n, out_specs=[HBM_SPEC] * n,
        out_shape=[jax.ShapeDtypeStruct(w.shape, BF16) for w in wires],
        scratch_shapes=[pltpu.SemaphoreType.DMA((3 * n,)), pltpu.SemaphoreType.DMA((3 * n,))],
    )(*wires)


def rs_final_sum(place, own, got, name):
    NP, half, C = got.shape
    tr = _row_tile(half, TR_ELT)
    nr = half // tr

    def body(pc_ref, own_ref, g1_ref, g2_ref, g3_ref, out_ref):
        out_ref[...] = ((own_ref[...] + g1_ref[...].astype(F32)) + g2_ref[...].astype(F32)) + g3_ref[...].astype(F32)

    slot = lambda j: pl.BlockSpec((None, tr, C), lambda r, pc: (pc[0] ^ j, r, 0))
    return pl.pallas_call(
        body, name=name,
        grid_spec=pltpu.PrefetchScalarGridSpec(
            num_scalar_prefetch=1, grid=(nr,),
            in_specs=[pl.BlockSpec((tr, C), lambda r, pc: (r, 0)), slot(1), slot(2), slot(3)],
            out_specs=pl.BlockSpec((tr, C), lambda r, pc: (pc[1] * nr + r, 0))),
        out_shape=jax.ShapeDtypeStruct((2 * half, C), F32),
        compiler_params=_params(("arbitrary",)),
    )(place, own, got, got, got)


def rs_share_siblings(totals):
    n = len(totals)

    def body(*refs):
        dst = refs[n:2 * n]
        send_sems, recv_sems = refs[2 * n:]
        x, y, c = _place()
        copies = []
        for k in range(n):
            half = dst[k].shape[0] // 2
            rows = dst[k].at[pl.ds(c * half, half)]
            cp = pltpu.make_async_remote_copy(rows, rows, send_sems.at[k], recv_sems.at[k],
                                              device_id=(x, y, 1 - c), device_id_type=MESH)
            cp.start()
            copies.append(cp)
        for k, cp in enumerate(copies):
            cp.wait_send()
            half = dst[k].shape[0] // 2
            got = dst[k].at[pl.ds((1 - c) * half, half)]
            pltpu.make_async_remote_copy(got, got, send_sems.at[k], recv_sems.at[k],
                                         device_id=(x, y, c), device_id_type=MESH).wait_recv()

    return pl.pallas_call(
        body, name="rs_share_siblings",
        in_specs=[HBM_SPEC] * n, out_specs=[HBM_SPEC] * n,
        out_shape=[jax.ShapeDtypeStruct(t.shape, F32) for t in totals],
        input_output_aliases={k: k for k in range(n)},
        scratch_shapes=[pltpu.SemaphoreType.DMA((n,)), pltpu.SemaphoreType.DMA((n,))],
    )(*totals)


def reduce_scatter(place, grads):
    sibs = rs_exchange_siblings(grads)
    wires, owns = [], []
    for k, (g, s) in enumerate(zip(grads, sibs)):
        w, o = rs_chip_sum(place, g, s, f"rs_chip_sum_{k}")
        wires.append(w)
        owns.append(o)
    gots = rs_exchange_chips(wires)
    totals = [rs_final_sum(place, o, g, f"rs_final_sum_{k}") for k, (o, g) in enumerate(zip(owns, gots))]
    return rs_share_siblings(totals)


def _rope_tables(positions):
    half = HEAD_DIM // 2
    inv_freq = ROPE_THETA ** (-jnp.arange(half, dtype=F32) / half)
    ang = positions.astype(F32)[:, None] * inv_freq
    cos, sin = jnp.cos(ang), jnp.sin(ang)
    return jnp.tile(cos, (1, 4)), jnp.concatenate([-sin, sin, -sin, sin], axis=1)


def _pieces_from_cols(full):
    D, W = full.shape
    return full.reshape(D, N_CHIPS, W // N_CHIPS).transpose(1, 0, 2)


def kernel(x, positions, ffn1_norm, ffn1_w_gate, ffn1_w_up, ffn1_w_down, mix_norm, w_in, conv_dw_w, conv_dw_b, conv_ln_g, conv_ln_b, conv_w_proj, attn_sinks, attn_w_o, gate_b, w_out, ffn2_norm, ffn2_w_gate, ffn2_w_up, ffn2_w_down, final_norm, loss_target, m_ffn1_norm, m_ffn1_w_gate, m_ffn1_w_up, m_ffn1_w_down, m_mix_norm, m_w_in, m_conv_dw_w, m_conv_dw_b, m_conv_ln_g, m_conv_ln_b, m_conv_w_proj, m_attn_sinks, m_attn_w_o, m_gate_b, m_w_out, m_ffn2_norm, m_ffn2_w_gate, m_ffn2_w_up, m_ffn2_w_down, m_final_norm, v_ffn1_norm, v_ffn1_w_gate, v_ffn1_w_up, v_ffn1_w_down, v_mix_norm, v_w_in, v_conv_dw_w, v_conv_dw_b, v_conv_ln_g, v_conv_ln_b, v_conv_w_proj, v_attn_sinks, v_attn_w_o, v_gate_b, v_w_out, v_ffn2_norm, v_ffn2_w_gate, v_ffn2_w_up, v_ffn2_w_down, v_final_norm):
    weights = dict(ffn1_norm=ffn1_norm, ffn1_w_gate=ffn1_w_gate, ffn1_w_up=ffn1_w_up, ffn1_w_down=ffn1_w_down,
                   mix_norm=mix_norm, w_in=w_in, conv_dw_w=conv_dw_w, conv_dw_b=conv_dw_b, conv_ln_g=conv_ln_g,
                   conv_ln_b=conv_ln_b, conv_w_proj=conv_w_proj, attn_sinks=attn_sinks, attn_w_o=attn_w_o,
                   gate_b=gate_b, w_out=w_out, ffn2_norm=ffn2_norm, ffn2_w_gate=ffn2_w_gate, ffn2_w_up=ffn2_w_up,
                   ffn2_w_down=ffn2_w_down, final_norm=final_norm)
    m_in = dict(ffn1_norm=m_ffn1_norm, ffn1_w_gate=m_ffn1_w_gate, ffn1_w_up=m_ffn1_w_up, ffn1_w_down=m_ffn1_w_down,
                mix_norm=m_mix_norm, w_in=m_w_in, conv_dw_w=m_conv_dw_w, conv_dw_b=m_conv_dw_b,
                conv_ln_g=m_conv_ln_g, conv_ln_b=m_conv_ln_b, conv_w_proj=m_conv_w_proj, attn_sinks=m_attn_sinks,
                attn_w_o=m_attn_w_o, gate_b=m_gate_b, w_out=m_w_out, ffn2_norm=m_ffn2_norm,
                ffn2_w_gate=m_ffn2_w_gate, ffn2_w_up=m_ffn2_w_up, ffn2_w_down=m_ffn2_w_down, final_norm=m_final_norm)
    v_in = dict(ffn1_norm=v_ffn1_norm, ffn1_w_gate=v_ffn1_w_gate, ffn1_w_up=v_ffn1_w_up, ffn1_w_down=v_ffn1_w_down,
                mix_norm=v_mix_norm, w_in=v_w_in, conv_dw_w=v_conv_dw_w, conv_dw_b=v_conv_dw_b,
                conv_ln_g=v_conv_ln_g, conv_ln_b=v_conv_ln_b, conv_w_proj=v_conv_w_proj, attn_sinks=v_attn_sinks,
                attn_w_o=v_attn_w_o, gate_b=v_gate_b, w_out=v_w_out, ffn2_norm=v_ffn2_norm,
                ffn2_w_gate=v_ffn2_w_gate, ffn2_w_up=v_ffn2_w_up, ffn2_w_down=v_ffn2_w_down, final_norm=v_final_norm)
    names = list(weights)
    big = ["ffn1_w_gate", "ffn1_w_up", "ffn1_w_down", "w_in", "conv_w_proj", "attn_w_o", "w_out",
           "ffn2_w_gate", "ffn2_w_up", "ffn2_w_down"]

    xs = x[0]
    T, D = xs.shape
    KV = (w_in.shape[2] * N_CHIPS - 5 * D) // 2
    n_heads = D // HEAD_DIM
    my_chip = 2 * lax.axis_index("x") + lax.axis_index("y")
    place = jnp.stack([my_chip, lax.axis_index("c")]).astype(jnp.int32)

    gathered = gather_weights([place_shard(place, weights[k][0], BF16, f"place_{k}") for k in big],
                              [place_shard(place, conv_dw_w[0], F32, "place_conv_dw_w")])
    full = dict(zip(big + ["conv_dw_w"], gathered))
    wg1, wu1, wd1 = full["ffn1_w_gate"], full["ffn1_w_up"], full["ffn1_w_down"]
    wg2, wu2, wd2 = full["ffn2_w_gate"], full["ffn2_w_up"], full["ffn2_w_down"]
    w_in_full = full["w_in"].transpose(1, 0, 2).reshape(D, -1)
    w_glu, w_qkv, w_gate = w_in_full[:, :2 * D], w_in_full[:, 2 * D:3 * D + 2 * KV], w_in_full[:, 3 * D + 2 * KV:]
    w_proj = full["conv_w_proj"].reshape(D, D)
    w_o = full["attn_w_o"].reshape(D, D)
    w_out_f = full["w_out"].reshape(D, D)
    dw_w = full["conv_dw_w"].transpose(1, 0, 2).reshape(CONV_WIDTH, D)
    dw_w = jnp.concatenate([dw_w, jnp.zeros((CONV_HALO - CONV_WIDTH, D), F32)], axis=0)
    cs, sn = _rope_tables(positions[0])
    fn_row = final_norm.reshape(1, D)

    x1, h1, g1, u1 = ffn_fwd(xs, ffn1_norm, wg1, wu1, wd1, "ffn1_fwd")
    h2 = rmsnorm_fwd(x1, mix_norm, "mix_norm_fwd")
    p_glu = matmul_nn(h2, w_glu, "mix_in_glu")
    p_qkv = matmul_nn(h2, w_qkv, "mix_in_qkv")
    p_gate = matmul_nn(h2, w_gate, "mix_in_gate")
    c1, c3 = conv_fwd(p_glu, dw_w, conv_dw_b, conv_ln_g, conv_ln_b, "conv_fwd")
    qr, kr, vb = rope_fwd(p_qkv, cs, sn, D, "rope_fwd")
    o = attn_fwd(qr, kr, vb, attn_sinks, "attn_fwd")
    x2, conv_out, attn_out, merged = merge_fwd(x1, c3, o, p_gate, gate_b, w_proj, w_o, w_out_f, "merge_fwd")
    x3, h3, g2, u2 = ffn_fwd(x2, ffn2_norm, wg2, wu2, wd2, "ffn2_fwd")

    dx3, head_sums = loss_head(x3, fn_row, loss_target[0], "loss_head")
    dx2, dg2, du2, d_ffn2_norm = ffn_bwd_x(x2, ffn2_norm, g2, u2, wg2, wu2, wd2, dx3, "ffn2_bwd_x")
    dwg2, dwu2, dwd2 = ffn_bwd_w(h3, g2, u2, dg2, du2, dx3, "ffn2_bwd_w")
    d_gates, d_conv_out, d_attn_out, d_o, dc1, merge_sums = merge_bwd(
        dx2, p_gate, gate_b, conv_out, attn_out, c1, conv_ln_g, conv_ln_b, w_proj, w_o, w_out_f, "merge_bwd")
    d_w_out = matmul_tn(merged, dx2, "d_w_out")
    d_w_proj = matmul_tn(c3, d_conv_out, "d_conv_w_proj")
    d_w_o = matmul_tn(o, d_attn_out, "d_attn_w_o")
    d_glu, d_dw_w = conv_bwd(p_glu, dc1, dw_w, "conv_bwd")
    dq, dk, dv, d_sinks = attn_bwd(qr, kr, vb, o, d_o, attn_sinks, "attn_bwd")
    d_qkv = rope_bwd(dq, dk, dv, cs, sn, "rope_bwd")
    dx1, d_mix_norm = mix_in_bwd([d_glu, d_qkv, d_gates], [w_glu, w_qkv, w_gate], x1, mix_norm, dx2, "mix_in_bwd")
    d_w_in = jnp.concatenate([matmul_tn(h2, d_glu, "d_w_in_glu"), matmul_tn(h2, d_qkv, "d_w_in_qkv"),
                              matmul_tn(h2, d_gates, "d_w_in_gate")], axis=1)
    dx0, dg1, du1, d_ffn1_norm = ffn_bwd_x(xs, ffn1_norm, g1, u1, wg1, wu1, wd1, dx1, "ffn1_bwd_x")
    dwg1, dwu1, dwd1 = ffn_bwd_w(h1, g1, u1, dg1, du1, dx1, "ffn1_bwd_w")

    pad_row = lambda v: jnp.pad(v, ((0, 0), (0, D - v.shape[1])))
    small_rows = jnp.concatenate([
        d_ffn1_norm, d_mix_norm, merge_sums[2:3, :D], merge_sums[1:2, :D], merge_sums[1:2, D:],
        pad_row(d_sinks[0:1, :n_heads]), merge_sums[0:1, :D], merge_sums[0:1, D:], d_ffn2_norm,
        head_sums[0:1], head_sums[1:2], jnp.zeros((5, D), F32), d_dw_w], axis=0)
    small = allreduce_small(small_rows)
    loss = small[10, 0]
    grads = {"ffn1_norm": small[0:1], "mix_norm": small[1:2], "conv_dw_b": small[2:3], "conv_ln_g": small[3:4],
             "conv_ln_b": small[4:5], "attn_sinks": small[5:6, :n_heads],
             "gate_b": jnp.concatenate([small[6:7], small[7:8]], axis=1), "ffn2_norm": small[8:9],
             "final_norm": small[9:10]}
    dwc = D // N_CHIPS
    grads["conv_dw_w"] = lax.dynamic_slice(small[16:16 + CONV_WIDTH], (0, my_chip * dwc), (CONV_WIDTH, dwc))

    rs_in = [dwg1, dwu1, dwd1, _pieces_from_cols(d_w_in), d_w_proj.reshape(N_CHIPS, dwc, D),
             d_w_o.reshape(N_CHIPS, dwc, D), d_w_out.reshape(N_CHIPS, dwc, D), dwg2, dwu2, dwd2]
    grads.update(zip(big, reduce_scatter(place, rs_in)))

    deltas, new_m, new_v = {}, {}, {}
    for k in names:
        shape = weights[k].shape
        g2d = grads[k].reshape(-1, shape[-1])
        grads[k] = g2d.reshape(shape)
        d, mn, vn = adamw(weights[k].reshape(g2d.shape), g2d, m_in[k].reshape(g2d.shape),
                          v_in[k].reshape(g2d.shape), f"adamw_{k}")
        deltas[k], new_m[k], new_v[k] = d.reshape(shape), mn.reshape(shape), vn.reshape(shape)

    return (loss, dx0[None], *[grads[k] for k in names], *[deltas[k] for k in names],
            *[new_m[k] for k in names], *[new_v[k] for k in names])
```

```python
import functools

import jax
import jax.numpy as jnp
from jax import lax
from jax.experimental import pallas as pl
from jax.experimental.pallas import tpu as pltpu

F32 = jnp.float32
BF16 = jnp.bfloat16
MESH = pl.DeviceIdType.MESH

HEAD_DIM = 64
WINDOW = 128
CONV_WIDTH = 31
CONV_HALO = 32
ROPE_THETA = 10000.0
EPS = 1e-6
LN_EPS = 1e-5
NEG_INF = -1e30
N_CHIPS = 4
N_DEV = 8

ADAM_LR = 0.001
ADAM_B1 = 0.9
ADAM_B2 = 0.999
ADAM_EPS = 1e-08
ADAM_WD = 0.01
ADAM_STEP = 10

TM_FFN = 512
TM_MM = 512
TM_ROW = 256
TK_TN = 512
TR_ELT = 256
VMEM_LIMIT = 56 * 1024 * 1024

NT_DIMS = (((1,), (1,)), ((), ()))
TN_DIMS = (((0,), (0,)), ((), ()))


def _row_tile(rows, cap):
    for t in range(min(cap, rows), 15, -1):
        if rows % t == 0 and t % 16 == 0:
            return t
    return rows


def _params(sem):
    return pltpu.CompilerParams(dimension_semantics=sem, vmem_limit_bytes=VMEM_LIMIT)


def _dot(a, b):
    return jnp.dot(a, b, preferred_element_type=F32)


def _dot_nt(a, b):
    return lax.dot_general(a, b, NT_DIMS, preferred_element_type=F32)


def _dot_tn(a, b):
    return lax.dot_general(a, b, TN_DIMS, preferred_element_type=F32)


def _sigmoid(x):
    return jax.nn.sigmoid(x)


def _rms_scale(xv):
    return lax.rsqrt(jnp.mean(xv * xv, axis=-1, keepdims=True) + EPS)


def _rms_bwd(xv, nw, dh):
    r = _rms_scale(xv)
    dn = dh * nw
    dx = r * dn - xv * (r * r * r) * jnp.mean(dn * xv, axis=-1, keepdims=True)
    dnw = jnp.sum(dh * (xv * r), axis=0, keepdims=True)
    return dx, dnw


def _silu_grad(z, s):
    return s * (1.0 + z * (1.0 - s))


HBM_SPEC = pl.BlockSpec(memory_space=pl.ANY)


def _call_hosting(body, side, *, grid, in_specs, out_specs, out_shape, scratch_shapes, operands, name):
    params = _params(("arbitrary",) * len(grid))
    if side is None:
        return pl.pallas_call(body, name=name, grid=grid, in_specs=in_specs, out_specs=out_specs, out_shape=out_shape,
                              scratch_shapes=scratch_shapes, compiler_params=params)(*operands)
    n_in, n_out, n_scr = len(in_specs), len(out_shape), len(scratch_shapes)
    s_in, s_out = len(side["inputs"]), len(side["out_shapes"])

    def at_step(end):
        hit = pl.program_id(0) == (grid[0] - 1 if end else 0)
        for a in range(1, len(grid)):
            hit &= pl.program_id(a) == (grid[a] - 1 if end else 0)
        return hit

    def hosted(*refs):
        b = n_in + s_in
        c = b + n_out
        d = c + s_out
        e = d + n_scr
        src, dst, sems = refs[n_in:b], refs[c:d], refs[e:]

        @pl.when(at_step(False))
        def _():
            side["start"](src, dst, sems)

        body(*refs[:n_in], *refs[b:c], *refs[d:e])

        @pl.when(at_step(True))
        def _():
            side["finish"](src, dst, sems)

    return pl.pallas_call(
        hosted, name=name, grid=grid, in_specs=list(in_specs) + [HBM_SPEC] * s_in,
        out_specs=list(out_specs) + [HBM_SPEC] * s_out, out_shape=list(out_shape) + list(side["out_shapes"]),
        scratch_shapes=list(scratch_shapes) + list(side["sems"]),
        input_output_aliases={n_in + a: n_out + b for a, b in side["aliases"].items()},
        compiler_params=params)(*operands, *side["inputs"])


def ffn_fwd(x, nw, wg, wu, wd, name, side=None):
    T, D = x.shape
    NP, _, Fs = wg.shape
    tm = min(TM_FFN, T)

    def body(x_ref, nw_ref, wg_ref, wu_ref, wd_ref, xo_ref, h_ref, g_ref, u_ref, acc_ref):
        j = pl.program_id(1)

        @pl.when(j == 0)
        def _():
            xv = x_ref[...]
            h_ref[...] = (xv * _rms_scale(xv) * nw_ref[...]).astype(BF16)
            acc_ref[...] = jnp.zeros_like(acc_ref)

        h = h_ref[...]
        g = _dot(h, wg_ref[...])
        u = _dot(h, wu_ref[...])
        a = (g * _sigmoid(g)) * u
        g_ref[...] = g.astype(BF16)
        u_ref[...] = u.astype(BF16)
        acc_ref[...] += _dot(a.astype(BF16), wd_ref[...])

        @pl.when(j == NP - 1)
        def _():
            xo_ref[...] = x_ref[...] + 0.5 * acc_ref[...]

    return _call_hosting(
        body, side, name=name, grid=(T // tm, NP),
        in_specs=[pl.BlockSpec((tm, D), lambda i, j: (i, 0)),
                  pl.BlockSpec((1, D), lambda i, j: (0, 0)),
                  pl.BlockSpec((None, D, Fs), lambda i, j: (j, 0, 0)),
                  pl.BlockSpec((None, D, Fs), lambda i, j: (j, 0, 0)),
                  pl.BlockSpec((None, Fs, D), lambda i, j: (j, 0, 0))],
        out_specs=[pl.BlockSpec((tm, D), lambda i, j: (i, 0)),
                   pl.BlockSpec((tm, D), lambda i, j: (i, 0)),
                   pl.BlockSpec((None, tm, Fs), lambda i, j: (j, i, 0)),
                   pl.BlockSpec((None, tm, Fs), lambda i, j: (j, i, 0))],
        out_shape=[jax.ShapeDtypeStruct((T, D), F32), jax.ShapeDtypeStruct((T, D), BF16),
                   jax.ShapeDtypeStruct((NP, T, Fs), BF16), jax.ShapeDtypeStruct((NP, T, Fs), BF16)],
        scratch_shapes=[pltpu.VMEM((tm, D), F32)],
        operands=(x, nw, wg, wu, wd))


def ffn_bwd_x(x, nw, g, u, wg, wu, wd, dout, name, side=None):
    T, D = x.shape
    NP, _, Fs = wg.shape
    tm = min(TM_FFN, T)

    def body(x_ref, nw_ref, g_ref, u_ref, wg_ref, wu_ref, wd_ref, do_ref,
             dx_ref, dg_ref, du_ref, dnw_ref, dh_ref, dob_ref):
        i = pl.program_id(0)
        j = pl.program_id(1)

        @pl.when((i == 0) & (j == 0))
        def _():
            dnw_ref[...] = jnp.zeros_like(dnw_ref)

        @pl.when(j == 0)
        def _():
            dh_ref[...] = jnp.zeros_like(dh_ref)
            dob_ref[...] = (0.5 * do_ref[...]).astype(BF16)

        da = _dot_nt(dob_ref[...], wd_ref[...])
        gf = g_ref[...].astype(F32)
        uf = u_ref[...].astype(F32)
        s = _sigmoid(gf)
        dg = (da * uf * _silu_grad(gf, s)).astype(BF16)
        du = (da * (gf * s)).astype(BF16)
        dg_ref[...] = dg
        du_ref[...] = du
        dh_ref[...] += _dot_nt(dg, wg_ref[...]) + _dot_nt(du, wu_ref[...])

        @pl.when(j == NP - 1)
        def _():
            dxn, dnw = _rms_bwd(x_ref[...], nw_ref[...], dh_ref[...])
            dx_ref[...] = do_ref[...] + dxn
            dnw_ref[...] += dnw

    return _call_hosting(
        body, side, name=name, grid=(T // tm, NP),
        in_specs=[pl.BlockSpec((tm, D), lambda i, j: (i, 0)),
                  pl.BlockSpec((1, D), lambda i, j: (0, 0)),
                  pl.BlockSpec((None, tm, Fs), lambda i, j: (j, i, 0)),
                  pl.BlockSpec((None, tm, Fs), lambda i, j: (j, i, 0)),
                  pl.BlockSpec((None, D, Fs), lambda i, j: (j, 0, 0)),
                  pl.BlockSpec((None, D, Fs), lambda i, j: (j, 0, 0)),
                  pl.BlockSpec((None, Fs, D), lambda i, j: (j, 0, 0)),
                  pl.BlockSpec((tm, D), lambda i, j: (i, 0))],
        out_specs=[pl.BlockSpec((tm, D), lambda i, j: (i, 0)),
                   pl.BlockSpec((None, tm, Fs), lambda i, j: (j, i, 0)),
                   pl.BlockSpec((None, tm, Fs), lambda i, j: (j, i, 0)),
                   pl.BlockSpec((1, D), lambda i, j: (0, 0))],
        out_shape=[jax.ShapeDtypeStruct((T, D), F32),
                   jax.ShapeDtypeStruct((NP, T, Fs), BF16), jax.ShapeDtypeStruct((NP, T, Fs), BF16),
                   jax.ShapeDtypeStruct((1, D), F32)],
        scratch_shapes=[pltpu.VMEM((tm, D), F32), pltpu.VMEM((tm, D), BF16)],
        operands=(x, nw, g, u, wg, wu, wd, dout))


def ffn_bwd_w(h, g, u, dg, du, dout, name):
    T, D = h.shape
    NP, _, Fs = g.shape
    tk = min(TK_TN, T)

    def body(h_ref, g_ref, u_ref, dg_ref, du_ref, do_ref, dwg_ref, dwu_ref, dwd_ref):
        t = pl.program_id(1)

        @pl.when(t == 0)
        def _():
            dwg_ref[...] = jnp.zeros_like(dwg_ref)
            dwu_ref[...] = jnp.zeros_like(dwu_ref)
            dwd_ref[...] = jnp.zeros_like(dwd_ref)

        hb = h_ref[...]
        dwg_ref[...] += _dot_tn(hb, dg_ref[...])
        dwu_ref[...] += _dot_tn(hb, du_ref[...])
        gf = g_ref[...].astype(F32)
        a = ((gf * _sigmoid(gf)) * u_ref[...].astype(F32)).astype(BF16)
        dwd_ref[...] += _dot_tn(a, (0.5 * do_ref[...]).astype(BF16))

    piece = pl.BlockSpec((None, tk, Fs), lambda j, t: (j, t, 0))
    return pl.pallas_call(
        body, name=name, grid=(NP, T // tk),
        in_specs=[pl.BlockSpec((tk, D), lambda j, t: (t, 0)), piece, piece, piece, piece,
                  pl.BlockSpec((tk, D), lambda j, t: (t, 0))],
        out_specs=[pl.BlockSpec((None, D, Fs), lambda j, t: (j, 0, 0)),
                   pl.BlockSpec((None, D, Fs), lambda j, t: (j, 0, 0)),
                   pl.BlockSpec((None, Fs, D), lambda j, t: (j, 0, 0))],
        out_shape=[jax.ShapeDtypeStruct((NP, D, Fs), F32), jax.ShapeDtypeStruct((NP, D, Fs), F32),
                   jax.ShapeDtypeStruct((NP, Fs, D), F32)],
        compiler_params=_params(("parallel", "arbitrary")),
    )(h, g, u, dg, du, dout)


def rmsnorm_fwd(x, nw, name):
    T, D = x.shape
    tm = min(TM_MM, T)

    def body(x_ref, nw_ref, h_ref):
        xv = x_ref[...]
        h_ref[...] = (xv * _rms_scale(xv) * nw_ref[...]).astype(BF16)

    return pl.pallas_call(
        body, name=name, grid=(T // tm,),
        in_specs=[pl.BlockSpec((tm, D), lambda i: (i, 0)), pl.BlockSpec((1, D), lambda i: (0, 0))],
        out_specs=pl.BlockSpec((tm, D), lambda i: (i, 0)),
        out_shape=jax.ShapeDtypeStruct((T, D), BF16),
        compiler_params=_params(("parallel",)),
    )(x, nw)


def matmul_nn(a, w, name):
    T, K = a.shape
    N = w.shape[1]
    tm = min(TM_MM, T)

    def body(a_ref, w_ref, o_ref):
        o_ref[...] = _dot(a_ref[...], w_ref[...])

    return pl.pallas_call(
        body, name=name, grid=(T // tm,),
        in_specs=[pl.BlockSpec((tm, K), lambda i: (i, 0)), pl.BlockSpec((K, N), lambda i: (0, 0))],
        out_specs=pl.BlockSpec((tm, N), lambda i: (i, 0)),
        out_shape=jax.ShapeDtypeStruct((T, N), F32),
        compiler_params=_params(("parallel",)),
    )(a, w)


def matmul_tn(lhs, rhs, name):
    T, K = lhs.shape
    N = rhs.shape[1]
    tk = min(TK_TN, T)

    def body(l_ref, r_ref, o_ref):
        @pl.when(pl.program_id(0) == 0)
        def _():
            o_ref[...] = jnp.zeros_like(o_ref)

        o_ref[...] += _dot_tn(l_ref[...].astype(BF16), r_ref[...].astype(BF16))

    return pl.pallas_call(
        body, name=name, grid=(T // tk,),
        in_specs=[pl.BlockSpec((tk, K), lambda t: (t, 0)), pl.BlockSpec((tk, N), lambda t: (t, 0))],
        out_specs=pl.BlockSpec((K, N), lambda t: (0, 0)),
        out_shape=jax.ShapeDtypeStruct((K, N), F32),
        compiler_params=_params(("arbitrary",)),
    )(lhs, rhs)


def mix_in_bwd(dps, ws, x, nw, dres, name):
    T, D = x.shape
    tm = min(TM_ROW, T)
    n = len(dps)

    def body(*refs):
        dp_refs, w_refs = refs[:n], refs[n:2 * n]
        x_ref, nw_ref, dr_ref, dx_ref, dnw_ref = refs[2 * n:]

        @pl.when(pl.program_id(0) == 0)
        def _():
            dnw_ref[...] = jnp.zeros_like(dnw_ref)

        dh = _dot_nt(dp_refs[0][...], w_refs[0][...])
        for k in range(1, n):
            dh += _dot_nt(dp_refs[k][...], w_refs[k][...])
        dxn, dnw = _rms_bwd(x_ref[...], nw_ref[...], dh)
        dx_ref[...] = dr_ref[...] + dxn
        dnw_ref[...] += dnw

    in_specs = [pl.BlockSpec((tm, dp.shape[1]), lambda i: (i, 0)) for dp in dps]
    in_specs += [pl.BlockSpec(w.shape, lambda i: (0, 0)) for w in ws]
    in_specs += [pl.BlockSpec((tm, D), lambda i: (i, 0)), pl.BlockSpec((1, D), lambda i: (0, 0)),
                 pl.BlockSpec((tm, D), lambda i: (i, 0))]
    return pl.pallas_call(
        body, name=name, grid=(T // tm,), in_specs=in_specs,
        out_specs=[pl.BlockSpec((tm, D), lambda i: (i, 0)), pl.BlockSpec((1, D), lambda i: (0, 0))],
        out_shape=[jax.ShapeDtypeStruct((T, D), F32), jax.ShapeDtypeStruct((1, D), F32)],
        compiler_params=_params(("arbitrary",)),
    )(*dps, *ws, x, nw, dres)


def _layernorm_stats(c1):
    mu = jnp.mean(c1, axis=-1, keepdims=True)
    xc = c1 - mu
    rstd = lax.rsqrt(jnp.mean(xc * xc, axis=-1, keepdims=True) + LN_EPS)
    return xc * rstd, rstd


def _shifted_copies(src_ref, dst_ref):
    rows = dst_ref.shape[1]
    for b in range(1, 8):
        dst_ref[b - 1] = src_ref[pl.ds(b, rows), :]


def _shifted_rows(src_ref, shifted_ref, start, rows, cols):
    a8, b = divmod(start, 8)
    if b == 0:
        return src_ref[pl.ds(8 * a8, rows), cols]
    return shifted_ref[b - 1, pl.ds(8 * a8, rows), cols]


def conv_fwd(p_glu, dw_w, dw_b, ln_g, ln_b, name):
    T, D2 = p_glu.shape
    D = D2 // 2
    tm = min(TM_ROW, T)
    hb = tm // CONV_HALO

    def body(a_ref, b_ref, ah_ref, bh_ref, w_ref, wb_ref, g_ref, be_ref, c1_ref, c3_ref, e_ref, es_ref):
        i = pl.program_id(0)
        halo = ah_ref[...] * _sigmoid(bh_ref[...])
        e_ref[pl.ds(0, CONV_HALO), :] = jnp.where(i > 0, halo, 0.0)
        e_ref[pl.ds(CONV_HALO, tm), :] = a_ref[...] * _sigmoid(b_ref[...])
        _shifted_copies(e_ref, es_ref)
        off = CONV_HALO - (CONV_WIDTH - 1)

        def strip(s, carry):
            cols = pl.ds(pl.multiple_of(s * 128, 128), 128)
            acc = jnp.zeros((tm, 128), F32) + wb_ref[:, cols]
            for k in range(CONV_WIDTH):
                acc += w_ref[pl.ds(k, 1), cols] * _shifted_rows(e_ref, es_ref, off + k, tm, cols)
            c1_ref[:, cols] = acc
            return carry

        lax.fori_loop(0, D // 128, strip, 0)
        xhat, _ = _layernorm_stats(c1_ref[...])
        c2 = xhat * g_ref[...] + be_ref[...]
        c3_ref[...] = (c2 * _sigmoid(c2)).astype(BF16)

    row = pl.BlockSpec((1, D), lambda i: (0, 0))
    return pl.pallas_call(
        body, name=name, grid=(T // tm,),
        in_specs=[pl.BlockSpec((tm, D), lambda i: (i, 0)), pl.BlockSpec((tm, D), lambda i: (i, 1)),
                  pl.BlockSpec((CONV_HALO, D), lambda i: (jnp.maximum(i * hb - 1, 0), 0)),
                  pl.BlockSpec((CONV_HALO, D), lambda i: (jnp.maximum(i * hb - 1, 0), 1)),
                  pl.BlockSpec((CONV_HALO, D), lambda i: (0, 0)), row, row, row],
        out_specs=[pl.BlockSpec((tm, D), lambda i: (i, 0)), pl.BlockSpec((tm, D), lambda i: (i, 0))],
        out_shape=[jax.ShapeDtypeStruct((T, D), F32), jax.ShapeDtypeStruct((T, D), BF16)],
        scratch_shapes=[pltpu.VMEM((tm + CONV_HALO, D), F32), pltpu.VMEM((7, tm + CONV_HALO - 8, D), F32)],
        compiler_params=_params(("parallel",)),
    )(p_glu, p_glu, p_glu, p_glu, dw_w, dw_b, ln_g, ln_b)


def conv_bwd(p_glu, dc1, dw_w, name):
    T, D2 = p_glu.shape
    D = D2 // 2
    tm = min(TM_ROW, T)
    hb = tm // CONV_HALO
    last = T // CONV_HALO - 1
    nblk = T // tm

    def body(a_ref, b_ref, ah_ref, bh_ref, d_ref, dn_ref, w_ref, dp_ref, dw_ref, e_ref, f_ref, es_ref, fs_ref):
        i = pl.program_id(0)

        @pl.when(i == 0)
        def _():
            dw_ref[...] = jnp.zeros_like(dw_ref)

        halo = ah_ref[...] * _sigmoid(bh_ref[...])
        e_ref[pl.ds(0, CONV_HALO), :] = jnp.where(i > 0, halo, 0.0)
        e_ref[pl.ds(CONV_HALO, tm), :] = a_ref[...] * _sigmoid(b_ref[...])
        f_ref[pl.ds(0, tm), :] = d_ref[...]
        f_ref[pl.ds(tm, CONV_HALO), :] = jnp.where(i < nblk - 1, dn_ref[...], 0.0)
        _shifted_copies(e_ref, es_ref)
        _shifted_copies(f_ref, fs_ref)
        off = CONV_HALO - (CONV_WIDTH - 1)

        def strip(s, carry):
            cols = pl.ds(pl.multiple_of(s * 128, 128), 128)
            d = d_ref[:, cols]
            dc0 = jnp.zeros((tm, 128), F32)
            for k in range(CONV_WIDTH):
                dw_ref[pl.ds(k, 1), cols] += jnp.sum(d * _shifted_rows(e_ref, es_ref, off + k, tm, cols),
                                                     axis=0, keepdims=True)
                dc0 += w_ref[pl.ds(k, 1), cols] * _shifted_rows(f_ref, fs_ref, CONV_WIDTH - 1 - k, tm, cols)
            a = a_ref[:, cols]
            sb = _sigmoid(b_ref[:, cols])
            dp_ref[:, cols] = (dc0 * sb).astype(BF16)
            dp_ref[:, pl.ds(pl.multiple_of(D + s * 128, 128), 128)] = (dc0 * a * sb * (1.0 - sb)).astype(BF16)
            return carry

        lax.fori_loop(0, D // 128, strip, 0)

    return pl.pallas_call(
        body, name=name, grid=(nblk,),
        in_specs=[pl.BlockSpec((tm, D), lambda i: (i, 0)), pl.BlockSpec((tm, D), lambda i: (i, 1)),
                  pl.BlockSpec((CONV_HALO, D), lambda i: (jnp.maximum(i * hb - 1, 0), 0)),
                  pl.BlockSpec((CONV_HALO, D), lambda i: (jnp.maximum(i * hb - 1, 0), 1)),
                  pl.BlockSpec((tm, D), lambda i: (i, 0)),
                  pl.BlockSpec((CONV_HALO, D), lambda i: (jnp.minimum((i + 1) * hb, last), 0)),
                  pl.BlockSpec((CONV_HALO, D), lambda i: (0, 0))],
        out_specs=[pl.BlockSpec((tm, D2), lambda i: (i, 0)), pl.BlockSpec((CONV_HALO, D), lambda i: (0, 0))],
        out_shape=[jax.ShapeDtypeStruct((T, D2), BF16), jax.ShapeDtypeStruct((CONV_HALO, D), F32)],
        scratch_shapes=[pltpu.VMEM((tm + CONV_HALO, D), F32), pltpu.VMEM((tm + CONV_HALO, D), F32),
                        pltpu.VMEM((7, tm + CONV_HALO - 8, D), F32), pltpu.VMEM((7, tm + CONV_HALO - 8, D), F32)],
        compiler_params=_params(("arbitrary",)),
    )(p_glu, p_glu, p_glu, p_glu, dc1, dc1, dw_w)


def _rot_half(x):
    lane = lax.broadcasted_iota(jnp.int32, x.shape, 1)
    first = (lane % HEAD_DIM) < HEAD_DIM // 2
    return jnp.where(first, pltpu.roll(x, 128 - HEAD_DIM // 2, 1), pltpu.roll(x, HEAD_DIM // 2, 1))


def _rope_chunks(x, cs, sn, sign):
    outs = []
    for c in range(x.shape[1] // 128):
        xc = x[:, c * 128:(c + 1) * 128]
        outs.append(xc * cs + sign * (_rot_half(xc) * sn))
    return outs[0] if len(outs) == 1 else jnp.concatenate(outs, axis=1)


def rope_fwd(p_qkv, cs, sn, D, name):
    T, W = p_qkv.shape
    KV = (W - D) // 2
    tm = min(TM_ROW, T)
    kb = D // KV

    def body(q_ref, k_ref, v_ref, cs_ref, sn_ref, qo_ref, ko_ref, vo_ref):
        cs_v, sn_v = cs_ref[...], sn_ref[...]
        qo_ref[...] = _rope_chunks(q_ref[...], cs_v, sn_v, 1.0).astype(BF16)
        ko_ref[...] = _rope_chunks(k_ref[...], cs_v, sn_v, 1.0).astype(BF16)
        vo_ref[...] = v_ref[...].astype(BF16)

    tab = pl.BlockSpec((tm, 128), lambda i: (i, 0))
    return pl.pallas_call(
        body, name=name, grid=(T // tm,),
        in_specs=[pl.BlockSpec((tm, D), lambda i: (i, 0)), pl.BlockSpec((tm, KV), lambda i: (i, kb)),
                  pl.BlockSpec((tm, KV), lambda i: (i, kb + 1)), tab, tab],
        out_specs=[pl.BlockSpec((tm, D), lambda i: (i, 0)), pl.BlockSpec((tm, KV), lambda i: (i, 0)),
                   pl.BlockSpec((tm, KV), lambda i: (i, 0))],
        out_shape=[jax.ShapeDtypeStruct((T, D), BF16), jax.ShapeDtypeStruct((T, KV), BF16),
                   jax.ShapeDtypeStruct((T, KV), BF16)],
        compiler_params=_params(("parallel",)),
    )(p_qkv, p_qkv, p_qkv, cs, sn)


def rope_bwd(dq, dk, dv, cs, sn, name):
    T, D = dq.shape
    KV = dk.shape[1]
    tm = min(TM_ROW, T)

    def body(dq_ref, dk_ref, dv_ref, cs_ref, sn_ref, o_ref):
        cs_v, sn_v = cs_ref[...], sn_ref[...]
        o_ref[:, pl.ds(0, D)] = _rope_chunks(dq_ref[...], cs_v, sn_v, -1.0).astype(BF16)
        o_ref[:, pl.ds(D, KV)] = _rope_chunks(dk_ref[...], cs_v, sn_v, -1.0).astype(BF16)
        o_ref[:, pl.ds(D + KV, KV)] = dv_ref[...].astype(BF16)

    tab = pl.BlockSpec((tm, 128), lambda i: (i, 0))
    return pl.pallas_call(
        body, name=name, grid=(T // tm,),
        in_specs=[pl.BlockSpec((tm, D), lambda i: (i, 0)), pl.BlockSpec((tm, KV), lambda i: (i, 0)),
                  pl.BlockSpec((tm, KV), lambda i: (i, 0)), tab, tab],
        out_specs=pl.BlockSpec((tm, D + 2 * KV), lambda i: (i, 0)),
        out_shape=jax.ShapeDtypeStruct((T, D + 2 * KV), BF16),
        compiler_params=_params(("parallel",)),
    )(dq, dk, dv, cs, sn)


def _lane_lo():
    return lax.broadcasted_iota(jnp.int32, (1, 128), 1) < HEAD_DIM


def _band_mask(i, reps):
    shape = (reps * WINDOW, 2 * WINDOW)
    qi = lax.broadcasted_iota(jnp.int32, shape, 0) % WINDOW
    cj = lax.broadcasted_iota(jnp.int32, shape, 1)
    rel = qi - cj + WINDOW
    return (rel >= 0) & (rel < WINDOW) & ((i > 0) | (cj >= WINDOW))


def _stack_pairs(ref, first, n):
    parts = [ref[:, pl.ds((first + p) * 128, 128)] for p in range(n)]
    return parts[0] if n == 1 else jnp.concatenate(parts, axis=0)


def _pair_rows(n):
    return lax.broadcasted_iota(jnp.int32, (n * WINDOW, 1), 0) // WINDOW


def _per_pair_column(values, n):
    rows = _pair_rows(n)
    col = jnp.zeros((n * WINDOW, 1), F32) + values[0]
    for p in range(1, n):
        col = jnp.where(rows == p, values[p], col)
    return col


def _kv_lo_hi(x2, g):
    pair, half = divmod(g, 2)
    lo = _lane_lo()
    xg = x2[:, pair * 128:(pair + 1) * 128].astype(F32)
    xg = jnp.where(lo if half == 0 else ~lo, xg, 0.0)
    sw = pltpu.roll(xg, HEAD_DIM, 1)
    x_lo, x_hi = (xg, sw) if half == 0 else (sw, xg)
    return x_lo.astype(BF16), x_hi.astype(BF16)


def _softmax_sink(s, allowed, sink):
    s = jnp.where(allowed, s * (HEAD_DIM ** -0.5), NEG_INF)
    m = jnp.maximum(jnp.max(s, axis=-1, keepdims=True), sink)
    p = jnp.exp(s - m)
    es = jnp.exp(sink - m)
    inv = 1.0 / (jnp.sum(p, axis=-1, keepdims=True) + es)
    return p * inv, es * inv


def attn_fwd(qr, kr, vb, sinks, name):
    T, D = qr.shape
    KV = kr.shape[1]
    n_kv = KV // HEAD_DIM
    group = (D // HEAD_DIM) // n_kv
    nb = T // WINDOW

    npair = group // 2

    def body(sink_ref, q_ref, kp_ref, kc_ref, vp_ref, vc_ref, o_ref):
        i = pl.program_id(0)
        allowed = _band_mask(i, npair)
        k2 = jnp.concatenate([kp_ref[...], kc_ref[...]], axis=0)
        v2 = jnp.concatenate([vp_ref[...], vc_ref[...]], axis=0)
        outs = [None] * (D // 128)
        for g in range(n_kv):
            k_lo, k_hi = _kv_lo_hi(k2, g)
            v_lo, v_hi = _kv_lo_hi(v2, g)
            first = (g * group) // 2
            q = _stack_pairs(q_ref, first, npair)
            sink_e = _per_pair_column([sink_ref[0, g * group + 2 * p] for p in range(npair)], npair)
            sink_o = _per_pair_column([sink_ref[0, g * group + 2 * p + 1] for p in range(npair)], npair)
            pe, _ = _softmax_sink(_dot_nt(q, k_lo), allowed, sink_e)
            po, _ = _softmax_sink(_dot_nt(q, k_hi), allowed, sink_o)
            o = _dot(pe.astype(BF16), v_lo) + _dot(po.astype(BF16), v_hi)
            for p in range(npair):
                outs[first + p] = o[p * WINDOW:(p + 1) * WINDOW]
        o_ref[...] = jnp.concatenate(outs, axis=1).astype(BF16)

    prev = lambda i: (jnp.maximum(i - 1, 0), 0)
    cur = lambda i: (i, 0)
    return pl.pallas_call(
        body, name=name, grid=(nb,),
        in_specs=[pl.BlockSpec(memory_space=pltpu.SMEM),
                  pl.BlockSpec((WINDOW, D), cur),
                  pl.BlockSpec((WINDOW, KV), prev), pl.BlockSpec((WINDOW, KV), cur),
                  pl.BlockSpec((WINDOW, KV), prev), pl.BlockSpec((WINDOW, KV), cur)],
        out_specs=pl.BlockSpec((WINDOW, D), cur),
        out_shape=jax.ShapeDtypeStruct((T, D), BF16),
        compiler_params=_params(("parallel",)),
    )(sinks, qr, kr, kr, vb, vb)


def attn_bwd(qr, kr, vb, o, do, sinks, name):
    T, D = qr.shape
    KV = kr.shape[1]
    n_heads = D // HEAD_DIM
    n_kv = KV // HEAD_DIM
    group = n_heads // n_kv
    nb = T // WINDOW
    npair = group // 2
    scale = HEAD_DIM ** -0.5

    def body(sink_ref, q_ref, kp_ref, kc_ref, vp_ref, vc_ref, o_ref, do_ref,
             dq_ref, dk_ref, dv_ref, ds_ref, ck_ref, cv_ref):
        i = pl.program_id(0)
        lo = _lane_lo()

        @pl.when(i == 0)
        def _():
            ck_ref[...] = jnp.zeros_like(ck_ref)
            cv_ref[...] = jnp.zeros_like(cv_ref)
            ds_ref[...] = jnp.zeros_like(ds_ref)

        @pl.when(i < nb)
        def _():
            allowed = _band_mask(i, npair)
            rows = _pair_rows(npair)
            k2 = jnp.concatenate([kp_ref[...], kc_ref[...]], axis=0)
            v2 = jnp.concatenate([vp_ref[...], vc_ref[...]], axis=0)
            lane = lax.broadcasted_iota(jnp.int32, (1, 128), 1)
            dsink = jnp.zeros((1, 128), F32)
            dq_out = [None] * (D // 128)
            dk_pairs = [jnp.zeros((2 * WINDOW, 128), F32) for _ in range(KV // 128)]
            dv_pairs = [jnp.zeros((2 * WINDOW, 128), F32) for _ in range(KV // 128)]
            for g in range(n_kv):
                k_lo, k_hi = _kv_lo_hi(k2, g)
                v_lo, v_hi = _kv_lo_hi(v2, g)
                first = (g * group) // 2
                q = _stack_pairs(q_ref, first, npair)
                dop = _stack_pairs(do_ref, first, npair)
                dd = dop.astype(F32) * _stack_pairs(o_ref, first, npair).astype(F32)
                dq = jnp.zeros((npair * WINDOW, 128), F32)
                dkg = jnp.zeros((2 * WINDOW, 128), F32)
                dvg = jnp.zeros((2 * WINDOW, 128), F32)
                for parity, k_h, v_h, sel in ((0, k_lo, v_lo, lo), (1, k_hi, v_hi, ~lo)):
                    heads = [g * group + 2 * p + parity for p in range(npair)]
                    sink = _per_pair_column([sink_ref[0, h] for h in heads], npair)
                    p_, ps = _softmax_sink(_dot_nt(q, k_h), allowed, sink)
                    delta = jnp.sum(jnp.where(sel, dd, 0.0), axis=-1, keepdims=True)
                    dsc = (p_ * (_dot_nt(dop, v_h) - delta)).astype(BF16)
                    sd = -ps * delta
                    for p, h in enumerate(heads):
                        dsink += jnp.where(lane == h, jnp.sum(jnp.where(rows == p, sd, 0.0)), 0.0)
                    dq += _dot(dsc, k_h)
                    dkg += jnp.where(sel, _dot_tn(dsc, q), 0.0)
                    dvg += jnp.where(sel, _dot_tn(p_.astype(BF16), dop), 0.0)
                for p in range(npair):
                    dq_out[first + p] = dq[p * WINDOW:(p + 1) * WINDOW]
                pair, half = divmod(g, 2)
                keep = lo if half == 0 else ~lo
                dk_pairs[pair] += jnp.where(keep, dkg + pltpu.roll(dkg, HEAD_DIM, 1), 0.0) * scale
                dv_pairs[pair] += jnp.where(keep, dvg + pltpu.roll(dvg, HEAD_DIM, 1), 0.0)
            dq_ref[...] = jnp.concatenate(dq_out, axis=1) * scale
            dk2 = dk_pairs[0] if len(dk_pairs) == 1 else jnp.concatenate(dk_pairs, axis=1)
            dv2 = dv_pairs[0] if len(dv_pairs) == 1 else jnp.concatenate(dv_pairs, axis=1)
            dk_ref[...] = ck_ref[...] + dk2[:WINDOW]
            dv_ref[...] = cv_ref[...] + dv2[:WINDOW]
            ck_ref[...] = dk2[WINDOW:]
            cv_ref[...] = dv2[WINDOW:]
            ds_ref[pl.ds(0, 1), :] += dsink

        @pl.when(i == nb)
        def _():
            dk_ref[...] = ck_ref[...]
            dv_ref[...] = cv_ref[...]

    prev = lambda i: (jnp.maximum(i - 1, 0), 0)
    cur = lambda i: (jnp.minimum(i, nb - 1), 0)
    prevc = lambda i: (jnp.maximum(jnp.minimum(i, nb - 1) - 1, 0), 0)
    return pl.pallas_call(
        body, name=name, grid=(nb + 1,),
        in_specs=[pl.BlockSpec(memory_space=pltpu.SMEM),
                  pl.BlockSpec((WINDOW, D), cur),
                  pl.BlockSpec((WINDOW, KV), prevc), pl.BlockSpec((WINDOW, KV), cur),
                  pl.BlockSpec((WINDOW, KV), prevc), pl.BlockSpec((WINDOW, KV), cur),
                  pl.BlockSpec((WINDOW, D), cur), pl.BlockSpec((WINDOW, D), cur)],
        out_specs=[pl.BlockSpec((WINDOW, D), cur), pl.BlockSpec((WINDOW, KV), prev),
                   pl.BlockSpec((WINDOW, KV), prev), pl.BlockSpec((8, 128), lambda i: (0, 0))],
        out_shape=[jax.ShapeDtypeStruct((T, D), F32), jax.ShapeDtypeStruct((T, KV), F32),
                   jax.ShapeDtypeStruct((T, KV), F32), jax.ShapeDtypeStruct((8, 128), F32)],
        scratch_shapes=[pltpu.VMEM((WINDOW, KV), F32), pltpu.VMEM((WINDOW, KV), F32)],
        compiler_params=_params(("arbitrary",)),
    )(sinks, qr, kr, kr, vb, vb, o, do)


def merge_fwd(x, c3, o, p_gate, gate_b, w_proj, w_o, w_out, name):
    T, D = x.shape
    tm = min(TM_ROW, T)

    def body(x_ref, c3_ref, o_ref, gc_ref, ga_ref, bc_ref, ba_ref, wp_ref, wo_ref, wout_ref,
             xo_ref, co_ref, ao_ref, mg_ref):
        conv_out = _dot(c3_ref[...], wp_ref[...])
        attn_out = _dot(o_ref[...], wo_ref[...])
        merged = (_sigmoid(gc_ref[...] + bc_ref[...]) * conv_out
                  + _sigmoid(ga_ref[...] + ba_ref[...]) * attn_out).astype(BF16)
        co_ref[...] = conv_out.astype(BF16)
        ao_ref[...] = attn_out.astype(BF16)
        mg_ref[...] = merged
        xo_ref[...] = x_ref[...] + _dot(merged, wout_ref[...])

    blk = lambda j: pl.BlockSpec((tm, D), lambda i: (i, j))
    row = lambda j: pl.BlockSpec((1, D), lambda i: (0, j))
    mat = pl.BlockSpec((D, D), lambda i: (0, 0))
    return pl.pallas_call(
        body, name=name, grid=(T // tm,),
        in_specs=[blk(0), blk(0), blk(0), blk(0), blk(1), row(0), row(1), mat, mat, mat],
        out_specs=[blk(0), blk(0), blk(0), blk(0)],
        out_shape=[jax.ShapeDtypeStruct((T, D), F32)] + [jax.ShapeDtypeStruct((T, D), BF16)] * 3,
        compiler_params=_params(("parallel",)),
    )(x, c3, o, p_gate, p_gate, gate_b, gate_b, w_proj, w_o, w_out)


def merge_bwd(dx, p_gate, gate_b, conv_out, attn_out, c1, ln_g, ln_b, w_proj, w_o, w_out, name):
    T, D = dx.shape
    tm = min(TM_ROW, T)

    def body(dx_ref, gc_ref, ga_ref, bc_ref, ba_ref, co_ref, ao_ref, c1_ref, g_ref, be_ref,
             wp_ref, wo_ref, wout_ref, dgt_ref, dco_ref, dao_ref, do_ref, dc1_ref, sm_ref):
        @pl.when(pl.program_id(0) == 0)
        def _():
            sm_ref[...] = jnp.zeros_like(sm_ref)

        dm = _dot_nt(dx_ref[...].astype(BF16), wout_ref[...])
        sc = _sigmoid(gc_ref[...] + bc_ref[...])
        sa = _sigmoid(ga_ref[...] + ba_ref[...])
        dco = (dm * sc).astype(BF16)
        dao = (dm * sa).astype(BF16)
        dgc = dm * co_ref[...].astype(F32) * sc * (1.0 - sc)
        dga = dm * ao_ref[...].astype(F32) * sa * (1.0 - sa)
        dgt_ref[:, pl.ds(0, D)] = dgc.astype(BF16)
        dgt_ref[:, pl.ds(D, D)] = dga.astype(BF16)
        dco_ref[...] = dco
        dao_ref[...] = dao
        do_ref[...] = _dot_nt(dao, wo_ref[...]).astype(BF16)
        dc3 = _dot_nt(dco, wp_ref[...])
        xhat, rstd = _layernorm_stats(c1_ref[...])
        c2 = xhat * g_ref[...] + be_ref[...]
        dc2 = dc3 * _silu_grad(c2, _sigmoid(c2))
        dxh = dc2 * g_ref[...]
        dc1 = rstd * (dxh - jnp.mean(dxh, axis=-1, keepdims=True)
                      - xhat * jnp.mean(dxh * xhat, axis=-1, keepdims=True))
        dc1_ref[...] = dc1
        colsum = lambda v: jnp.sum(v, axis=0, keepdims=True)
        for r, (left, right) in enumerate(((dgc, dga), (dc2 * xhat, dc2), (dc1, None))):
            sm_ref[pl.ds(r, 1), pl.ds(0, D)] += colsum(left)
            if right is not None:
                sm_ref[pl.ds(r, 1), pl.ds(D, D)] += colsum(right)

    blk = lambda j: pl.BlockSpec((tm, D), lambda i: (i, j))
    row = lambda j: pl.BlockSpec((1, D), lambda i: (0, j))
    mat = pl.BlockSpec((D, D), lambda i: (0, 0))
    return pl.pallas_call(
        body, name=name, grid=(T // tm,),
        in_specs=[blk(0), blk(0), blk(1), row(0), row(1), blk(0), blk(0), blk(0), row(0), row(0), mat, mat, mat],
        out_specs=[pl.BlockSpec((tm, 2 * D), lambda i: (i, 0)), blk(0), blk(0), blk(0), blk(0),
                   pl.BlockSpec((8, 2 * D), lambda i: (0, 0))],
        out_shape=[jax.ShapeDtypeStruct((T, 2 * D), BF16)] + [jax.ShapeDtypeStruct((T, D), BF16)] * 3
                  + [jax.ShapeDtypeStruct((T, D), F32), jax.ShapeDtypeStruct((8, 2 * D), F32)],
        compiler_params=_params(("arbitrary",)),
    )(dx, p_gate, p_gate, gate_b, gate_b, conv_out, attn_out, c1, ln_g, ln_b, w_proj, w_o, w_out)


def loss_head(x, nw, target, name):
    T, D = x.shape
    tm = min(TM_ROW, T)

    def body(x_ref, nw_ref, t_ref, dx_ref, sm_ref):
        @pl.when(pl.program_id(0) == 0)
        def _():
            sm_ref[...] = jnp.zeros_like(sm_ref)

        xv = x_ref[...]
        err = xv * _rms_scale(xv) * nw_ref[...] - t_ref[...]
        loss = 0.5 * jnp.sum(jnp.mean(err * err, axis=-1, keepdims=True))
        dxn, dnw = _rms_bwd(xv, nw_ref[...], err * (1.0 / D))
        dx_ref[...] = dxn
        sm_ref[pl.ds(0, 1), :] += dnw
        sm_ref[pl.ds(1, 1), :] += jnp.zeros((1, D), F32) + loss

    return pl.pallas_call(
        body, name=name, grid=(T // tm,),
        in_specs=[pl.BlockSpec((tm, D), lambda i: (i, 0)), pl.BlockSpec((1, D), lambda i: (0, 0)),
                  pl.BlockSpec((tm, D), lambda i: (i, 0))],
        out_specs=[pl.BlockSpec((tm, D), lambda i: (i, 0)), pl.BlockSpec((8, D), lambda i: (0, 0))],
        out_shape=[jax.ShapeDtypeStruct((T, D), F32), jax.ShapeDtypeStruct((8, D), F32)],
        compiler_params=_params(("arbitrary",)),
    )(x, nw, target)


def adamw(w, g, m, v, name):
    R, C = w.shape
    tr = _row_tile(R, TR_ELT)

    def body(w_ref, g_ref, m_ref, v_ref, d_ref, mo_ref, vo_ref):
        gv = g_ref[...]
        mn = ADAM_B1 * m_ref[...] + (1.0 - ADAM_B1) * gv
        vn = ADAM_B2 * v_ref[...] + (1.0 - ADAM_B2) * (gv * gv)
        m_hat = mn / (1.0 - ADAM_B1 ** ADAM_STEP)
        v_hat = vn / (1.0 - ADAM_B2 ** ADAM_STEP)
        d_ref[...] = -ADAM_LR * (m_hat / (jnp.sqrt(v_hat) + ADAM_EPS) + ADAM_WD * w_ref[...])
        mo_ref[...] = mn
        vo_ref[...] = vn

    spec = pl.BlockSpec((tr, C), lambda i: (i, 0))
    return pl.pallas_call(
        body, name=name, grid=(R // tr,), in_specs=[spec] * 4, out_specs=[spec] * 3,
        out_shape=[jax.ShapeDtypeStruct((R, C), F32)] * 3,
        compiler_params=_params(("parallel",)),
    )(w, g, m, v)


def _place():
    return lax.axis_index("x"), lax.axis_index("y"), lax.axis_index("c")


def place_shard(place, w, dtype, name):
    R, C = w.shape
    tr = _row_tile(R, TR_ELT)

    def body(pc_ref, w_ref, o_ref):
        o_ref[...] = w_ref[...].astype(dtype)

    return pl.pallas_call(
        body, name=name,
        grid_spec=pltpu.PrefetchScalarGridSpec(
            num_scalar_prefetch=1, grid=(R // tr,),
            in_specs=[pl.BlockSpec((tr, C), lambda r, pc: (r, 0))],
            out_specs=pl.BlockSpec((None, tr, C), lambda r, pc: (pc[0], r, 0))),
        out_shape=jax.ShapeDtypeStruct((N_CHIPS, R, C), dtype),
        compiler_params=_params(("arbitrary",)),
    )(place, w)


def gather_side(shards, small):
    n, ns = len(shards), len(small)

    def ici_copy(dst, sems, k, j, x, y, c, sending):
        px, py = x ^ (j >> 1), y ^ (j & 1)
        slot = 2 * x + y if sending else 2 * px + py
        half = dst[k].shape[1] // 2
        part = dst[k].at[slot, pl.ds(c * half, half)] if k < n else dst[k].at[slot]
        return pltpu.make_async_remote_copy(part, part, sems[0].at[3 * k + j - 1], sems[1].at[3 * k + j - 1],
                                            device_id=(px, py, c), device_id_type=MESH)

    def d2d_copy(dst, sems, k, j, x, y, c, sending):
        half = dst[k].shape[1] // 2
        part = dst[k].at[2 * (x ^ (j >> 1)) + (y ^ (j & 1)), pl.ds((c if sending else 1 - c) * half, half)]
        return pltpu.make_async_remote_copy(part, part, sems[2].at[3 * k + j - 1], sems[3].at[3 * k + j - 1],
                                            device_id=(x, y, 1 - c), device_id_type=MESH)

    def start(src, dst, sems):
        x, y, c = _place()
        for k in range(n + ns):
            for j in (1, 2, 3):
                ici_copy(dst, sems, k, j, x, y, c, True).start()

    def finish(src, dst, sems):
        x, y, c = _place()
        for k in range(n + ns):
            for j in (1, 2, 3):
                ici_copy(dst, sems, k, j, x, y, c, False).wait_recv()
                if k < n:
                    d2d_copy(dst, sems, k, j, x, y, c, True).start()
        for k in range(n):
            for j in (1, 2, 3):
                d2d_copy(dst, sems, k, j, x, y, c, False).wait_recv()
        for k in range(n + ns):
            for j in (1, 2, 3):
                ici_copy(dst, sems, k, j, x, y, c, True).wait_send()
                if k < n:
                    d2d_copy(dst, sems, k, j, x, y, c, True).wait_send()

    arrays = list(shards) + list(small)
    return dict(inputs=arrays, out_shapes=[jax.ShapeDtypeStruct(a.shape, a.dtype) for a in arrays],
                aliases={k: k for k in range(n + ns)},
                sems=[pltpu.SemaphoreType.DMA((3 * (n + ns),)), pltpu.SemaphoreType.DMA((3 * (n + ns),)),
                      pltpu.SemaphoreType.DMA((3 * n,)), pltpu.SemaphoreType.DMA((3 * n,))],
                start=start, finish=finish)


def run_side(side, name):
    n_in, n_out = len(side["inputs"]), len(side["out_shapes"])

    def body(*refs):
        src, dst, sems = refs[:n_in], refs[n_in:n_in + n_out], refs[n_in + n_out:]
        side["start"](src, dst, sems)
        side["finish"](src, dst, sems)

    return pl.pallas_call(
        body, name=name, in_specs=[HBM_SPEC] * n_in, out_specs=[HBM_SPEC] * n_out,
        out_shape=side["out_shapes"], input_output_aliases=side["aliases"], scratch_shapes=side["sems"],
    )(*side["inputs"])


def allreduce_small(block):
    R, C = block.shape

    def body(x_ref, out_ref, all_ref, send_sems, recv_sems, local_sem):
        x, y, c = _place()
        me, sibling = (x, y, c), (x, y, 1 - c)
        chips = [(1 - x, y), (x, 1 - y), (1 - x, 1 - y)]

        def slot(px, py, pc):
            return all_ref.at[4 * px + 2 * py + pc]

        def copy(k, block_of, to, src=None):
            return pltpu.make_async_remote_copy(
                src_ref=slot(*block_of) if src is None else src, dst_ref=slot(*block_of),
                send_sem=send_sems.at[k], recv_sem=recv_sems.at[k], device_id=to, device_id_type=MESH)

        mine = pltpu.make_async_copy(x_ref, slot(*me), local_sem)
        mine.start()
        first = [copy(0, me, sibling, src=x_ref)]
        first += [copy(1 + j, me, (*chip, c), src=x_ref) for j, chip in enumerate(chips)]
        for cp in first:
            cp.start()
        passed = [copy(4 + j, (*chip, c), sibling) for j, chip in enumerate(chips)]
        for j, chip in enumerate(chips):
            copy(1 + j, (*chip, c), me).wait_recv()
            passed[j].start()
        copy(0, sibling, me).wait_recv()
        for j, chip in enumerate(chips):
            copy(4 + j, (*chip, 1 - c), me).wait_recv()
        for cp in first + passed:
            cp.wait_send()
        mine.wait()
        total = all_ref[0]
        for d in range(1, N_DEV):
            total = total + all_ref[d]
        out_ref[...] = total

    return pl.pallas_call(
        body, name="allreduce_small",
        in_specs=[pl.BlockSpec(memory_space=pltpu.VMEM)], out_specs=pl.BlockSpec(memory_space=pltpu.VMEM),
        out_shape=jax.ShapeDtypeStruct((R, C), F32),
        scratch_shapes=[pltpu.VMEM((N_DEV, R, C), F32), pltpu.SemaphoreType.DMA((7,)),
                        pltpu.SemaphoreType.DMA((7,)), pltpu.SemaphoreType.DMA],
        compiler_params=pltpu.CompilerParams(vmem_limit_bytes=VMEM_LIMIT),
    )(block)


def rs_exchange_siblings(grads, name):
    n = len(grads)

    def body(*refs):
        src, dst = refs[:n], refs[n:2 * n]
        send_sems, recv_sems = refs[2 * n:]
        x, y, c = _place()
        copies = []
        for k in range(n):
            half = src[k].shape[1] // 2
            cp = pltpu.make_async_remote_copy(src[k].at[:, pl.ds((1 - c) * half, half)], dst[k],
                                              send_sems.at[k], recv_sems.at[k],
                                              device_id=(x, y, 1 - c), device_id_type=MESH)
            cp.start()
            copies.append(cp)
        for cp in copies:
            cp.wait()

    return pl.pallas_call(
        body, name=name,
        in_specs=[HBM_SPEC] * n, out_specs=[HBM_SPEC] * n,
        out_shape=[jax.ShapeDtypeStruct((N_CHIPS, g.shape[1] // 2, g.shape[2]), F32) for g in grads],
        scratch_shapes=[pltpu.SemaphoreType.DMA((n,)), pltpu.SemaphoreType.DMA((n,))],
    )(*grads)


def rs_chip_sum(place, grad, sib, name):
    NP, R, C = grad.shape
    half = R // 2
    tr = _row_tile(half, TR_ELT)
    nr = half // tr

    def body(pc_ref, g_ref, s_ref, wire_ref, own_ref):
        q = pl.program_id(1)
        total = g_ref[...] + s_ref[...]
        wire_ref[...] = total.astype(BF16)

        @pl.when(q == pc_ref[0])
        def _():
            own_ref[...] = total

    return pl.pallas_call(
        body, name=name,
        grid_spec=pltpu.PrefetchScalarGridSpec(
            num_scalar_prefetch=1, grid=(nr, NP),
            in_specs=[pl.BlockSpec((None, tr, C), lambda r, q, pc: (q, pc[1] * nr + r, 0)),
                      pl.BlockSpec((None, tr, C), lambda r, q, pc: (q, r, 0))],
            out_specs=[pl.BlockSpec((None, tr, C), lambda r, q, pc: (q, r, 0)),
                       pl.BlockSpec((tr, C), lambda r, q, pc: (r, 0))]),
        out_shape=[jax.ShapeDtypeStruct((NP, half, C), BF16), jax.ShapeDtypeStruct((half, C), F32)],
        compiler_params=_params(("arbitrary", "arbitrary")),
    )(place, grad, sib)


def exchange_chips_side(wires):
    n = len(wires)

    def copies(src, dst, sems):
        x, y, c = _place()
        for k in range(n):
            for j in (1, 2, 3):
                qx, qy = x ^ (j >> 1), y ^ (j & 1)
                yield pltpu.make_async_remote_copy(src[k].at[2 * qx + qy], dst[k].at[2 * x + y],
                                                   sems[0].at[3 * k + j - 1], sems[1].at[3 * k + j - 1],
                                                   device_id=(qx, qy, c), device_id_type=MESH)

    def start(src, dst, sems):
        for cp in copies(src, dst, sems):
            cp.start()

    def finish(src, dst, sems):
        for cp in copies(src, dst, sems):
            cp.wait()

    return dict(inputs=list(wires), out_shapes=[jax.ShapeDtypeStruct(w.shape, BF16) for w in wires], aliases={},
                sems=[pltpu.SemaphoreType.DMA((3 * n,)), pltpu.SemaphoreType.DMA((3 * n,))],
                start=start, finish=finish)


def rs_final_sum(place, own, got, name):
    NP, half, C = got.shape
    tr = _row_tile(half, TR_ELT)
    nr = half // tr

    def body(pc_ref, own_ref, g1_ref, g2_ref, g3_ref, out_ref):
        out_ref[...] = ((own_ref[...] + g1_ref[...].astype(F32)) + g2_ref[...].astype(F32)) + g3_ref[...].astype(F32)

    slot = lambda j: pl.BlockSpec((None, tr, C), lambda r, pc: (pc[0] ^ j, r, 0))
    return pl.pallas_call(
        body, name=name,
        grid_spec=pltpu.PrefetchScalarGridSpec(
            num_scalar_prefetch=1, grid=(nr,),
            in_specs=[pl.BlockSpec((tr, C), lambda r, pc: (r, 0)), slot(1), slot(2), slot(3)],
            out_specs=pl.BlockSpec((tr, C), lambda r, pc: (pc[1] * nr + r, 0))),
        out_shape=jax.ShapeDtypeStruct((2 * half, C), F32),
        compiler_params=_params(("arbitrary",)),
    )(place, own, got, got, got)


def rs_share_siblings(totals):
    n = len(totals)

    def body(*refs):
        dst = refs[n:2 * n]
        send_sems, recv_sems = refs[2 * n:]
        x, y, c = _place()
        copies = []
        for k in range(n):
            half = dst[k].shape[0] // 2
            rows = dst[k].at[pl.ds(c * half, half)]
            cp = pltpu.make_async_remote_copy(rows, rows, send_sems.at[k], recv_sems.at[k],
                                              device_id=(x, y, 1 - c), device_id_type=MESH)
            cp.start()
            copies.append(cp)
        for k, cp in enumerate(copies):
            cp.wait_send()
            half = dst[k].shape[0] // 2
            got = dst[k].at[pl.ds((1 - c) * half, half)]
            pltpu.make_async_remote_copy(got, got, send_sems.at[k], recv_sems.at[k],
                                         device_id=(x, y, c), device_id_type=MESH).wait_recv()

    return pl.pallas_call(
        body, name="rs_share_siblings",
        in_specs=[HBM_SPEC] * n, out_specs=[HBM_SPEC] * n,
        out_shape=[jax.ShapeDtypeStruct(t.shape, F32) for t in totals],
        input_output_aliases={k: k for k in range(n)},
        scratch_shapes=[pltpu.SemaphoreType.DMA((n,)), pltpu.SemaphoreType.DMA((n,))],
    )(*totals)


def rs_to_wires(place, grads, tag):
    sibs = rs_exchange_siblings(grads, f"rs_exchange_siblings_{tag}")
    wires, owns = [], []
    for k, (g, s) in enumerate(zip(grads, sibs)):
        w, o = rs_chip_sum(place, g, s, f"rs_chip_sum_{tag}{k}")
        wires.append(w)
        owns.append(o)
    return wires, owns


def rs_finish(place, owns, gots):
    totals = [rs_final_sum(place, o, g, f"rs_final_sum_{k}") for k, (o, g) in enumerate(zip(owns, gots))]
    return rs_share_siblings(totals)


def _rope_tables(positions):
    half = HEAD_DIM // 2
    inv_freq = ROPE_THETA ** (-jnp.arange(half, dtype=F32) / half)
    ang = positions.astype(F32)[:, None] * inv_freq
    cos, sin = jnp.cos(ang), jnp.sin(ang)
    return jnp.tile(cos, (1, 4)), jnp.concatenate([-sin, sin, -sin, sin], axis=1)


def _pieces_from_cols(full):
    D, W = full.shape
    return full.reshape(D, N_CHIPS, W // N_CHIPS).transpose(1, 0, 2)


def kernel(x, positions, ffn1_norm, ffn1_w_gate, ffn1_w_up, ffn1_w_down, mix_norm, w_in, conv_dw_w, conv_dw_b, conv_ln_g, conv_ln_b, conv_w_proj, attn_sinks, attn_w_o, gate_b, w_out, ffn2_norm, ffn2_w_gate, ffn2_w_up, ffn2_w_down, final_norm, loss_target, m_ffn1_norm, m_ffn1_w_gate, m_ffn1_w_up, m_ffn1_w_down, m_mix_norm, m_w_in, m_conv_dw_w, m_conv_dw_b, m_conv_ln_g, m_conv_ln_b, m_conv_w_proj, m_attn_sinks, m_attn_w_o, m_gate_b, m_w_out, m_ffn2_norm, m_ffn2_w_gate, m_ffn2_w_up, m_ffn2_w_down, m_final_norm, v_ffn1_norm, v_ffn1_w_gate, v_ffn1_w_up, v_ffn1_w_down, v_mix_norm, v_w_in, v_conv_dw_w, v_conv_dw_b, v_conv_ln_g, v_conv_ln_b, v_conv_w_proj, v_attn_sinks, v_attn_w_o, v_gate_b, v_w_out, v_ffn2_norm, v_ffn2_w_gate, v_ffn2_w_up, v_ffn2_w_down, v_final_norm):
    weights = dict(ffn1_norm=ffn1_norm, ffn1_w_gate=ffn1_w_gate, ffn1_w_up=ffn1_w_up, ffn1_w_down=ffn1_w_down,
                   mix_norm=mix_norm, w_in=w_in, conv_dw_w=conv_dw_w, conv_dw_b=conv_dw_b, conv_ln_g=conv_ln_g,
                   conv_ln_b=conv_ln_b, conv_w_proj=conv_w_proj, attn_sinks=attn_sinks, attn_w_o=attn_w_o,
                   gate_b=gate_b, w_out=w_out, ffn2_norm=ffn2_norm, ffn2_w_gate=ffn2_w_gate, ffn2_w_up=ffn2_w_up,
                   ffn2_w_down=ffn2_w_down, final_norm=final_norm)
    m_in = dict(ffn1_norm=m_ffn1_norm, ffn1_w_gate=m_ffn1_w_gate, ffn1_w_up=m_ffn1_w_up, ffn1_w_down=m_ffn1_w_down,
                mix_norm=m_mix_norm, w_in=m_w_in, conv_dw_w=m_conv_dw_w, conv_dw_b=m_conv_dw_b,
                conv_ln_g=m_conv_ln_g, conv_ln_b=m_conv_ln_b, conv_w_proj=m_conv_w_proj, attn_sinks=m_attn_sinks,
                attn_w_o=m_attn_w_o, gate_b=m_gate_b, w_out=m_w_out, ffn2_norm=m_ffn2_norm,
                ffn2_w_gate=m_ffn2_w_gate, ffn2_w_up=m_ffn2_w_up, ffn2_w_down=m_ffn2_w_down, final_norm=m_final_norm)
    v_in = dict(ffn1_norm=v_ffn1_norm, ffn1_w_gate=v_ffn1_w_gate, ffn1_w_up=v_ffn1_w_up, ffn1_w_down=v_ffn1_w_down,
                mix_norm=v_mix_norm, w_in=v_w_in, conv_dw_w=v_conv_dw_w, conv_dw_b=v_conv_dw_b,
                conv_ln_g=v_conv_ln_g, conv_ln_b=v_conv_ln_b, conv_w_proj=v_conv_w_proj, attn_sinks=v_attn_sinks,
                attn_w_o=v_attn_w_o, gate_b=v_gate_b, w_out=v_w_out, ffn2_norm=v_ffn2_norm,
                ffn2_w_gate=v_ffn2_w_gate, ffn2_w_up=v_ffn2_w_up, ffn2_w_down=v_ffn2_w_down, final_norm=v_final_norm)
    names = list(weights)
    big = ["ffn1_w_gate", "ffn1_w_up", "ffn1_w_down", "w_in", "conv_w_proj", "attn_w_o", "w_out",
           "ffn2_w_gate", "ffn2_w_up", "ffn2_w_down"]

    xs = x[0]
    T, D = xs.shape
    KV = (w_in.shape[2] * N_CHIPS - 5 * D) // 2
    n_heads = D // HEAD_DIM
    my_chip = 2 * lax.axis_index("x") + lax.axis_index("y")
    place = jnp.stack([my_chip, lax.axis_index("c")]).astype(jnp.int32)

    placed = {k: place_shard(place, weights[k][0], BF16, f"place_{k}") for k in big}
    placed_dw = place_shard(place, conv_dw_w[0], F32, "place_conv_dw_w")
    first, later = big[:3], big[3:]
    wg1, wu1, wd1 = run_side(gather_side([placed[k] for k in first], []), "gather_ffn1")
    x1, h1, g1, u1, *gathered = ffn_fwd(x[0], ffn1_norm, wg1, wu1, wd1, "ffn1_fwd",
                                        side=gather_side([placed[k] for k in later], [placed_dw]))
    full = dict(zip(later + ["conv_dw_w"], gathered))
    wg2, wu2, wd2 = full["ffn2_w_gate"], full["ffn2_w_up"], full["ffn2_w_down"]
    w_in_full = full["w_in"].transpose(1, 0, 2).reshape(D, -1)
    w_glu, w_qkv, w_gate = w_in_full[:, :2 * D], w_in_full[:, 2 * D:3 * D + 2 * KV], w_in_full[:, 3 * D + 2 * KV:]
    w_proj = full["conv_w_proj"].reshape(D, D)
    w_o = full["attn_w_o"].reshape(D, D)
    w_out_f = full["w_out"].reshape(D, D)
    dw_w = full["conv_dw_w"].transpose(1, 0, 2).reshape(CONV_WIDTH, D)
    dw_w = jnp.concatenate([dw_w, jnp.zeros((CONV_HALO - CONV_WIDTH, D), F32)], axis=0)
    cs, sn = _rope_tables(positions[0])
    fn_row = final_norm.reshape(1, D)

    h2 = rmsnorm_fwd(x1, mix_norm, "mix_norm_fwd")
    p_glu = matmul_nn(h2, w_glu, "mix_in_glu")
    p_qkv = matmul_nn(h2, w_qkv, "mix_in_qkv")
    p_gate = matmul_nn(h2, w_gate, "mix_in_gate")
    c1, c3 = conv_fwd(p_glu, dw_w, conv_dw_b, conv_ln_g, conv_ln_b, "conv_fwd")
    qr, kr, vb = rope_fwd(p_qkv, cs, sn, D, "rope_fwd")
    o = attn_fwd(qr, kr, vb, attn_sinks, "attn_fwd")
    x2, conv_out, attn_out, merged = merge_fwd(x1, c3, o, p_gate, gate_b, w_proj, w_o, w_out_f, "merge_fwd")
    x3, h3, g2, u2 = ffn_fwd(x2, ffn2_norm, wg2, wu2, wd2, "ffn2_fwd")

    dx3, head_sums = loss_head(x3, fn_row, loss_target[0], "loss_head")
    dx2, dg2, du2, d_ffn2_norm = ffn_bwd_x(x2, ffn2_norm, g2, u2, wg2, wu2, wd2, dx3, "ffn2_bwd_x")
    dwg2, dwu2, dwd2 = ffn_bwd_w(h3, g2, u2, dg2, du2, dx3, "ffn2_bwd_w")
    d_gates, d_conv_out, d_attn_out, d_o, dc1, merge_sums = merge_bwd(
        dx2, p_gate, gate_b, conv_out, attn_out, c1, conv_ln_g, conv_ln_b, w_proj, w_o, w_out_f, "merge_bwd")
    d_w_out = matmul_tn(merged, dx2, "d_w_out")
    d_w_proj = matmul_tn(c3, d_conv_out, "d_conv_w_proj")
    d_w_o = matmul_tn(o, d_attn_out, "d_attn_w_o")
    d_glu, d_dw_w = conv_bwd(p_glu, dc1, dw_w, "conv_bwd")
    dq, dk, dv, d_sinks = attn_bwd(qr, kr, vb, o, d_o, attn_sinks, "attn_bwd")
    d_qkv = rope_bwd(dq, dk, dv, cs, sn, "rope_bwd")
    dx1, d_mix_norm = mix_in_bwd([d_glu, d_qkv, d_gates], [w_glu, w_qkv, w_gate], x1, mix_norm, dx2, "mix_in_bwd")
    d_w_in = jnp.concatenate([matmul_tn(h2, d_glu, "d_w_in_glu"), matmul_tn(h2, d_qkv, "d_w_in_qkv"),
                              matmul_tn(h2, d_gates, "d_w_in_gate")], axis=1)
    dwc = D // N_CHIPS
    early = [dwg2, dwu2, dwd2, _pieces_from_cols(d_w_in), d_w_proj.reshape(N_CHIPS, dwc, D),
             d_w_o.reshape(N_CHIPS, dwc, D), d_w_out.reshape(N_CHIPS, dwc, D)]
    wires_e, owns_e = rs_to_wires(place, early, "early")
    dx0, dg1, du1, d_ffn1_norm, *gots_e = ffn_bwd_x(xs, ffn1_norm, g1, u1, wg1, wu1, wd1, dx1, "ffn1_bwd_x",
                                                    side=exchange_chips_side(wires_e))
    dwg1, dwu1, dwd1 = ffn_bwd_w(h1, g1, u1, dg1, du1, dx1, "ffn1_bwd_w")
    wires_l, owns_l = rs_to_wires(place, [dwg1, dwu1, dwd1], "late")
    gots_l = run_side(exchange_chips_side(wires_l), "rs_exchange_chips_late")
    reduced = rs_finish(place, owns_e + owns_l, list(gots_e) + list(gots_l))

    pad_row = lambda v: jnp.pad(v, ((0, 0), (0, D - v.shape[1])))
    small_rows = jnp.concatenate([
        d_ffn1_norm, d_mix_norm, merge_sums[2:3, :D], merge_sums[1:2, :D], merge_sums[1:2, D:],
        pad_row(d_sinks[0:1, :n_heads]), merge_sums[0:1, :D], merge_sums[0:1, D:], d_ffn2_norm,
        head_sums[0:1], head_sums[1:2], jnp.zeros((5, D), F32), d_dw_w], axis=0)
    small = allreduce_small(small_rows)
    loss = small[10, 0]
    grads = {"ffn1_norm": small[0:1], "mix_norm": small[1:2], "conv_dw_b": small[2:3], "conv_ln_g": small[3:4],
             "conv_ln_b": small[4:5], "attn_sinks": small[5:6, :n_heads],
             "gate_b": jnp.concatenate([small[6:7], small[7:8]], axis=1), "ffn2_norm": small[8:9],
             "final_norm": small[9:10]}
    grads["conv_dw_w"] = lax.dynamic_slice(small[16:16 + CONV_WIDTH], (0, my_chip * dwc), (CONV_WIDTH, dwc))
    grads.update(zip(["ffn2_w_gate", "ffn2_w_up", "ffn2_w_down", "w_in", "conv_w_proj", "attn_w_o", "w_out",
                      "ffn1_w_gate", "ffn1_w_up", "ffn1_w_down"], reduced))

    deltas, new_m, new_v = {}, {}, {}
    for k in names:
        shape = weights[k].shape
        g2d = grads[k].reshape(-1, shape[-1])
        grads[k] = g2d.reshape(shape)
        d, mn, vn = adamw(weights[k].reshape(g2d.shape), g2d, m_in[k].reshape(g2d.shape),
                          v_in[k].reshape(g2d.shape), f"adamw_{k}")
        deltas[k], new_m[k], new_v[k] = d.reshape(shape), mn.reshape(shape), vn.reshape(shape)

    return (loss, dx0[None], *[grads[k] for k in names], *[deltas[k] for k in names],
            *[new_m[k] for k in names], *[new_v[k] for k in names])
```

```python
import functools

import jax
import jax.numpy as jnp
from jax import lax
from jax.experimental import pallas as pl
from jax.experimental.pallas import tpu as pltpu

F32 = jnp.float32
BF16 = jnp.bfloat16
MESH = pl.DeviceIdType.MESH

HEAD_DIM = 64
WINDOW = 128
CONV_WIDTH = 31
CONV_HALO = 32
ROPE_THETA = 10000.0
EPS = 1e-6
LN_EPS = 1e-5
NEG_INF = -1e30
N_CHIPS = 4
N_DEV = 8

ADAM_LR = 0.001
ADAM_B1 = 0.9
ADAM_B2 = 0.999
ADAM_EPS = 1e-08
ADAM_WD = 0.01
ADAM_STEP = 10

TM_FFN = 512
TM_FFN_FWD = 1024
TM_MM = 512
TM_ROW = 256
TK_TN = 512
TR_ELT = 256
VMEM_LIMIT = 56 * 1024 * 1024

NT_DIMS = (((1,), (1,)), ((), ()))
TN_DIMS = (((0,), (0,)), ((), ()))


def _row_tile(rows, cap):
    for t in range(min(cap, rows), 15, -1):
        if rows % t == 0 and t % 16 == 0:
            return t
    return rows


def _params(sem):
    return pltpu.CompilerParams(dimension_semantics=sem, vmem_limit_bytes=VMEM_LIMIT)


def _dot(a, b):
    return jnp.dot(a, b, preferred_element_type=F32)


def _dot_nt(a, b):
    return lax.dot_general(a, b, NT_DIMS, preferred_element_type=F32)


def _dot_tn(a, b):
    return lax.dot_general(a, b, TN_DIMS, preferred_element_type=F32)


def _split_rows(dot, a, b):
    m = a.shape[0] // 2
    return jnp.concatenate([dot(a[:m], b), dot(a[m:], b)], axis=0)


def _sigmoid(x):
    return jax.nn.sigmoid(x)


def _rms_scale(xv):
    return lax.rsqrt(jnp.mean(xv * xv, axis=-1, keepdims=True) + EPS)


def _rms_bwd(xv, nw, dh):
    r = _rms_scale(xv)
    dn = dh * nw
    dx = r * dn - xv * (r * r * r) * jnp.mean(dn * xv, axis=-1, keepdims=True)
    dnw = jnp.sum(dh * (xv * r), axis=0, keepdims=True)
    return dx, dnw


def _silu_grad(z, s):
    return s * (1.0 + z * (1.0 - s))


HBM_SPEC = pl.BlockSpec(memory_space=pl.ANY)


def _call_hosting(body, side, *, grid, in_specs, out_specs, out_shape, scratch_shapes, operands, name):
    params = _params(("arbitrary",) * len(grid))
    if side is None:
        return pl.pallas_call(body, name=name, grid=grid, in_specs=in_specs, out_specs=out_specs, out_shape=out_shape,
                              scratch_shapes=scratch_shapes, compiler_params=params)(*operands)
    n_in, n_out, n_scr = len(in_specs), len(out_shape), len(scratch_shapes)
    s_in, s_out = len(side["inputs"]), len(side["out_shapes"])

    def at_step(end):
        hit = pl.program_id(0) == (grid[0] - 1 if end else 0)
        for a in range(1, len(grid)):
            hit &= pl.program_id(a) == (grid[a] - 1 if end else 0)
        return hit

    def hosted(*refs):
        b = n_in + s_in
        c = b + n_out
        d = c + s_out
        e = d + n_scr
        src, dst, sems = refs[n_in:b], refs[c:d], refs[e:]

        @pl.when(at_step(False))
        def _():
            side["start"](src, dst, sems)

        body(*refs[:n_in], *refs[b:c], *refs[d:e])

        @pl.when(at_step(True))
        def _():
            side["finish"](src, dst, sems)

    return pl.pallas_call(
        hosted, name=name, grid=grid, in_specs=list(in_specs) + [HBM_SPEC] * s_in,
        out_specs=list(out_specs) + [HBM_SPEC] * s_out, out_shape=list(out_shape) + list(side["out_shapes"]),
        scratch_shapes=list(scratch_shapes) + list(side["sems"]),
        input_output_aliases={n_in + a: n_out + b for a, b in side["aliases"].items()},
        compiler_params=params)(*operands, *side["inputs"])


def ffn_fwd(x, nw, wg, wu, wd, name, side=None):
    T, D = x.shape
    NP, _, Fs = wg.shape
    tm = min(TM_FFN_FWD, T)

    def body(x_ref, nw_ref, wg_ref, wu_ref, wd_ref, xo_ref, h_ref, g_ref, u_ref, acc_ref):
        j = pl.program_id(1)

        @pl.when(j == 0)
        def _():
            xv = x_ref[...]
            h_ref[...] = (xv * _rms_scale(xv) * nw_ref[...]).astype(BF16)
            acc_ref[...] = jnp.zeros_like(acc_ref)

        h = h_ref[...]
        g = _dot(h, wg_ref[...])
        u = _dot(h, wu_ref[...])
        a = (g * _sigmoid(g)) * u
        g_ref[...] = g.astype(BF16)
        u_ref[...] = u.astype(BF16)
        acc_ref[...] += _dot(a.astype(BF16), wd_ref[...])

        @pl.when(j == NP - 1)
        def _():
            xo_ref[...] = x_ref[...] + 0.5 * acc_ref[...]

    return _call_hosting(
        body, side, name=name, grid=(T // tm, NP),
        in_specs=[pl.BlockSpec((tm, D), lambda i, j: (i, 0)),
                  pl.BlockSpec((1, D), lambda i, j: (0, 0)),
                  pl.BlockSpec((None, D, Fs), lambda i, j: (j, 0, 0)),
                  pl.BlockSpec((None, D, Fs), lambda i, j: (j, 0, 0)),
                  pl.BlockSpec((None, Fs, D), lambda i, j: (j, 0, 0))],
        out_specs=[pl.BlockSpec((tm, D), lambda i, j: (i, 0)),
                   pl.BlockSpec((tm, D), lambda i, j: (i, 0)),
                   pl.BlockSpec((None, tm, Fs), lambda i, j: (j, i, 0)),
                   pl.BlockSpec((None, tm, Fs), lambda i, j: (j, i, 0))],
        out_shape=[jax.ShapeDtypeStruct((T, D), F32), jax.ShapeDtypeStruct((T, D), BF16),
                   jax.ShapeDtypeStruct((NP, T, Fs), BF16), jax.ShapeDtypeStruct((NP, T, Fs), BF16)],
        scratch_shapes=[pltpu.VMEM((tm, D), F32)],
        operands=(x, nw, wg, wu, wd))


def ffn_bwd_x(x, nw, g, u, wg, wu, wd, dout, name, side=None):
    T, D = x.shape
    NP, _, Fs = wg.shape
    tm = min(TM_FFN, T)

    def body(x_ref, nw_ref, g_ref, u_ref, wg_ref, wu_ref, wd_ref, do_ref,
             dx_ref, dg_ref, du_ref, dnw_ref, dh_ref, dob_ref):
        i = pl.program_id(0)
        j = pl.program_id(1)

        @pl.when((i == 0) & (j == 0))
        def _():
            dnw_ref[...] = jnp.zeros_like(dnw_ref)

        @pl.when(j == 0)
        def _():
            dh_ref[...] = jnp.zeros_like(dh_ref)
            dob_ref[...] = (0.5 * do_ref[...]).astype(BF16)

        da = _split_rows(_dot_nt, dob_ref[...], wd_ref[...])
        gf = g_ref[...].astype(F32)
        uf = u_ref[...].astype(F32)
        s = _sigmoid(gf)
        dg = (da * uf * _silu_grad(gf, s)).astype(BF16)
        du = (da * (gf * s)).astype(BF16)
        dg_ref[...] = dg
        du_ref[...] = du
        dh_ref[...] += _dot_nt(dg, wg_ref[...]) + _dot_nt(du, wu_ref[...])

        @pl.when(j == NP - 1)
        def _():
            dxn, dnw = _rms_bwd(x_ref[...], nw_ref[...], dh_ref[...])
            dx_ref[...] = do_ref[...] + dxn
            dnw_ref[...] += dnw

    return _call_hosting(
        body, side, name=name, grid=(T // tm, NP),
        in_specs=[pl.BlockSpec((tm, D), lambda i, j: (i, 0)),
                  pl.BlockSpec((1, D), lambda i, j: (0, 0)),
                  pl.BlockSpec((None, tm, Fs), lambda i, j: (j, i, 0)),
                  pl.BlockSpec((None, tm, Fs), lambda i, j: (j, i, 0)),
                  pl.BlockSpec((None, D, Fs), lambda i, j: (j, 0, 0)),
                  pl.BlockSpec((None, D, Fs), lambda i, j: (j, 0, 0)),
                  pl.BlockSpec((None, Fs, D), lambda i, j: (j, 0, 0)),
                  pl.BlockSpec((tm, D), lambda i, j: (i, 0))],
        out_specs=[pl.BlockSpec((tm, D), lambda i, j: (i, 0)),
                   pl.BlockSpec((None, tm, Fs), lambda i, j: (j, i, 0)),
                   pl.BlockSpec((None, tm, Fs), lambda i, j: (j, i, 0)),
                   pl.BlockSpec((1, D), lambda i, j: (0, 0))],
        out_shape=[jax.ShapeDtypeStruct((T, D), F32),
                   jax.ShapeDtypeStruct((NP, T, Fs), BF16), jax.ShapeDtypeStruct((NP, T, Fs), BF16),
                   jax.ShapeDtypeStruct((1, D), F32)],
        scratch_shapes=[pltpu.VMEM((tm, D), F32), pltpu.VMEM((tm, D), BF16)],
        operands=(x, nw, g, u, wg, wu, wd, dout))


def ffn_bwd_w(h, g, u, dg, du, dout, name):
    T, D = h.shape
    NP, _, Fs = g.shape
    tk = min(TK_TN, T)

    def body(h_ref, g_ref, u_ref, dg_ref, du_ref, do_ref, dwg_ref, dwu_ref, dwd_ref):
        t = pl.program_id(1)

        @pl.when(t == 0)
        def _():
            dwg_ref[...] = jnp.zeros_like(dwg_ref)
            dwu_ref[...] = jnp.zeros_like(dwu_ref)
            dwd_ref[...] = jnp.zeros_like(dwd_ref)

        hb = h_ref[...]
        dwg_ref[...] += _dot_tn(hb, dg_ref[...])
        dwu_ref[...] += _dot_tn(hb, du_ref[...])
        gf = g_ref[...].astype(F32)
        a = ((gf * _sigmoid(gf)) * u_ref[...].astype(F32)).astype(BF16)
        dwd_ref[...] += _dot_tn(a, (0.5 * do_ref[...]).astype(BF16))

    piece = pl.BlockSpec((None, tk, Fs), lambda j, t: (j, t, 0))
    return pl.pallas_call(
        body, name=name, grid=(NP, T // tk),
        in_specs=[pl.BlockSpec((tk, D), lambda j, t: (t, 0)), piece, piece, piece, piece,
                  pl.BlockSpec((tk, D), lambda j, t: (t, 0))],
        out_specs=[pl.BlockSpec((None, D, Fs), lambda j, t: (j, 0, 0)),
                   pl.BlockSpec((None, D, Fs), lambda j, t: (j, 0, 0)),
                   pl.BlockSpec((None, Fs, D), lambda j, t: (j, 0, 0))],
        out_shape=[jax.ShapeDtypeStruct((NP, D, Fs), F32), jax.ShapeDtypeStruct((NP, D, Fs), F32),
                   jax.ShapeDtypeStruct((NP, Fs, D), F32)],
        compiler_params=_params(("parallel", "arbitrary")),
    )(h, g, u, dg, du, dout)


def rmsnorm_fwd(x, nw, name):
    T, D = x.shape
    tm = min(TM_MM, T)

    def body(x_ref, nw_ref, h_ref):
        xv = x_ref[...]
        h_ref[...] = (xv * _rms_scale(xv) * nw_ref[...]).astype(BF16)

    return pl.pallas_call(
        body, name=name, grid=(T // tm,),
        in_specs=[pl.BlockSpec((tm, D), lambda i: (i, 0)), pl.BlockSpec((1, D), lambda i: (0, 0))],
        out_specs=pl.BlockSpec((tm, D), lambda i: (i, 0)),
        out_shape=jax.ShapeDtypeStruct((T, D), BF16),
        compiler_params=_params(("parallel",)),
    )(x, nw)


def matmul_nn(a, w, name):
    T, K = a.shape
    N = w.shape[1]
    tm = min(TM_MM, T)

    def body(a_ref, w_ref, o_ref):
        o_ref[...] = _dot(a_ref[...], w_ref[...])

    return pl.pallas_call(
        body, name=name, grid=(T // tm,),
        in_specs=[pl.BlockSpec((tm, K), lambda i: (i, 0)), pl.BlockSpec((K, N), lambda i: (0, 0))],
        out_specs=pl.BlockSpec((tm, N), lambda i: (i, 0)),
        out_shape=jax.ShapeDtypeStruct((T, N), F32),
        compiler_params=_params(("parallel",)),
    )(a, w)


def matmul_tn(lhs, rhs, name):
    T, K = lhs.shape
    N = rhs.shape[1]
    tk = min(TK_TN, T)

    def body(l_ref, r_ref, o_ref):
        @pl.when(pl.program_id(0) == 0)
        def _():
            o_ref[...] = jnp.zeros_like(o_ref)

        o_ref[...] += _dot_tn(l_ref[...].astype(BF16), r_ref[...].astype(BF16))

    return pl.pallas_call(
        body, name=name, grid=(T // tk,),
        in_specs=[pl.BlockSpec((tk, K), lambda t: (t, 0)), pl.BlockSpec((tk, N), lambda t: (t, 0))],
        out_specs=pl.BlockSpec((K, N), lambda t: (0, 0)),
        out_shape=jax.ShapeDtypeStruct((K, N), F32),
        compiler_params=_params(("arbitrary",)),
    )(lhs, rhs)


def mix_in_bwd(dps, ws, x, nw, dres, name):
    T, D = x.shape
    tm = min(TM_ROW, T)
    n = len(dps)

    def body(*refs):
        dp_refs, w_refs = refs[:n], refs[n:2 * n]
        x_ref, nw_ref, dr_ref, dx_ref, dnw_ref = refs[2 * n:]

        @pl.when(pl.program_id(0) == 0)
        def _():
            dnw_ref[...] = jnp.zeros_like(dnw_ref)

        dh = _dot_nt(dp_refs[0][...], w_refs[0][...])
        for k in range(1, n):
            dh += _dot_nt(dp_refs[k][...], w_refs[k][...])
        dxn, dnw = _rms_bwd(x_ref[...], nw_ref[...], dh)
        dx_ref[...] = dr_ref[...] + dxn
        dnw_ref[...] += dnw

    in_specs = [pl.BlockSpec((tm, dp.shape[1]), lambda i: (i, 0)) for dp in dps]
    in_specs += [pl.BlockSpec(w.shape, lambda i: (0, 0)) for w in ws]
    in_specs += [pl.BlockSpec((tm, D), lambda i: (i, 0)), pl.BlockSpec((1, D), lambda i: (0, 0)),
                 pl.BlockSpec((tm, D), lambda i: (i, 0))]
    return pl.pallas_call(
        body, name=name, grid=(T // tm,), in_specs=in_specs,
        out_specs=[pl.BlockSpec((tm, D), lambda i: (i, 0)), pl.BlockSpec((1, D), lambda i: (0, 0))],
        out_shape=[jax.ShapeDtypeStruct((T, D), F32), jax.ShapeDtypeStruct((1, D), F32)],
        compiler_params=_params(("arbitrary",)),
    )(*dps, *ws, x, nw, dres)


def _layernorm_stats(c1):
    mu = jnp.mean(c1, axis=-1, keepdims=True)
    xc = c1 - mu
    rstd = lax.rsqrt(jnp.mean(xc * xc, axis=-1, keepdims=True) + LN_EPS)
    return xc * rstd, rstd


def _shifted_copies(src_ref, dst_ref):
    rows = dst_ref.shape[1]
    for b in range(1, 8):
        dst_ref[b - 1] = src_ref[pl.ds(b, rows), :]


def _shifted_rows(src_ref, shifted_ref, start, rows, cols):
    a8, b = divmod(start, 8)
    if b == 0:
        return src_ref[pl.ds(8 * a8, rows), cols]
    return shifted_ref[b - 1, pl.ds(8 * a8, rows), cols]


def conv_fwd(p_glu, dw_w, dw_b, ln_g, ln_b, name):
    T, D2 = p_glu.shape
    D = D2 // 2
    tm = min(TM_ROW, T)
    hb = tm // CONV_HALO

    def body(a_ref, b_ref, ah_ref, bh_ref, w_ref, wb_ref, g_ref, be_ref, c1_ref, c3_ref, e_ref, es_ref):
        i = pl.program_id(0)
        halo = ah_ref[...] * _sigmoid(bh_ref[...])
        e_ref[pl.ds(0, CONV_HALO), :] = jnp.where(i > 0, halo, 0.0)
        e_ref[pl.ds(CONV_HALO, tm), :] = a_ref[...] * _sigmoid(b_ref[...])
        _shifted_copies(e_ref, es_ref)
        off = CONV_HALO - (CONV_WIDTH - 1)

        def strip(s, carry):
            cols = pl.ds(pl.multiple_of(s * 128, 128), 128)
            acc = jnp.zeros((tm, 128), F32) + wb_ref[:, cols]
            for k in range(CONV_WIDTH):
                acc += w_ref[pl.ds(k, 1), cols] * _shifted_rows(e_ref, es_ref, off + k, tm, cols)
            c1_ref[:, cols] = acc
            return carry

        lax.fori_loop(0, D // 128, strip, 0)
        xhat, _ = _layernorm_stats(c1_ref[...])
        c2 = xhat * g_ref[...] + be_ref[...]
        c3_ref[...] = (c2 * _sigmoid(c2)).astype(BF16)

    row = pl.BlockSpec((1, D), lambda i: (0, 0))
    return pl.pallas_call(
        body, name=name, grid=(T // tm,),
        in_specs=[pl.BlockSpec((tm, D), lambda i: (i, 0)), pl.BlockSpec((tm, D), lambda i: (i, 1)),
                  pl.BlockSpec((CONV_HALO, D), lambda i: (jnp.maximum(i * hb - 1, 0), 0)),
                  pl.BlockSpec((CONV_HALO, D), lambda i: (jnp.maximum(i * hb - 1, 0), 1)),
                  pl.BlockSpec((CONV_HALO, D), lambda i: (0, 0)), row, row, row],
        out_specs=[pl.BlockSpec((tm, D), lambda i: (i, 0)), pl.BlockSpec((tm, D), lambda i: (i, 0))],
        out_shape=[jax.ShapeDtypeStruct((T, D), F32), jax.ShapeDtypeStruct((T, D), BF16)],
        scratch_shapes=[pltpu.VMEM((tm + CONV_HALO, D), F32), pltpu.VMEM((7, tm + CONV_HALO - 8, D), F32)],
        compiler_params=_params(("parallel",)),
    )(p_glu, p_glu, p_glu, p_glu, dw_w, dw_b, ln_g, ln_b)


def conv_bwd(p_glu, dc1, dw_w, name):
    T, D2 = p_glu.shape
    D = D2 // 2
    tm = min(TM_ROW, T)
    hb = tm // CONV_HALO
    last = T // CONV_HALO - 1
    nblk = T // tm

    def body(a_ref, b_ref, ah_ref, bh_ref, d_ref, dn_ref, w_ref, dp_ref, dw_ref, e_ref, f_ref, es_ref, fs_ref):
        i = pl.program_id(0)

        @pl.when(i == 0)
        def _():
            dw_ref[...] = jnp.zeros_like(dw_ref)

        halo = ah_ref[...] * _sigmoid(bh_ref[...])
        e_ref[pl.ds(0, CONV_HALO), :] = jnp.where(i > 0, halo, 0.0)
        e_ref[pl.ds(CONV_HALO, tm), :] = a_ref[...] * _sigmoid(b_ref[...])
        f_ref[pl.ds(0, tm), :] = d_ref[...]
        f_ref[pl.ds(tm, CONV_HALO), :] = jnp.where(i < nblk - 1, dn_ref[...], 0.0)
        _shifted_copies(e_ref, es_ref)
        _shifted_copies(f_ref, fs_ref)
        off = CONV_HALO - (CONV_WIDTH - 1)

        def strip(s, carry):
            cols = pl.ds(pl.multiple_of(s * 128, 128), 128)
            d = d_ref[:, cols]
            dc0 = jnp.zeros((tm, 128), F32)
            for k in range(CONV_WIDTH):
                dw_ref[pl.ds(k, 1), cols] += jnp.sum(d * _shifted_rows(e_ref, es_ref, off + k, tm, cols),
                                                     axis=0, keepdims=True)
                dc0 += w_ref[pl.ds(k, 1), cols] * _shifted_rows(f_ref, fs_ref, CONV_WIDTH - 1 - k, tm, cols)
            a = a_ref[:, cols]
            sb = _sigmoid(b_ref[:, cols])
            dp_ref[:, cols] = (dc0 * sb).astype(BF16)
            dp_ref[:, pl.ds(pl.multiple_of(D + s * 128, 128), 128)] = (dc0 * a * sb * (1.0 - sb)).astype(BF16)
            return carry

        lax.fori_loop(0, D // 128, strip, 0)

    return pl.pallas_call(
        body, name=name, grid=(nblk,),
        in_specs=[pl.BlockSpec((tm, D), lambda i: (i, 0)), pl.BlockSpec((tm, D), lambda i: (i, 1)),
                  pl.BlockSpec((CONV_HALO, D), lambda i: (jnp.maximum(i * hb - 1, 0), 0)),
                  pl.BlockSpec((CONV_HALO, D), lambda i: (jnp.maximum(i * hb - 1, 0), 1)),
                  pl.BlockSpec((tm, D), lambda i: (i, 0)),
                  pl.BlockSpec((CONV_HALO, D), lambda i: (jnp.minimum((i + 1) * hb, last), 0)),
                  pl.BlockSpec((CONV_HALO, D), lambda i: (0, 0))],
        out_specs=[pl.BlockSpec((tm, D2), lambda i: (i, 0)), pl.BlockSpec((CONV_HALO, D), lambda i: (0, 0))],
        out_shape=[jax.ShapeDtypeStruct((T, D2), BF16), jax.ShapeDtypeStruct((CONV_HALO, D), F32)],
        scratch_shapes=[pltpu.VMEM((tm + CONV_HALO, D), F32), pltpu.VMEM((tm + CONV_HALO, D), F32),
                        pltpu.VMEM((7, tm + CONV_HALO - 8, D), F32), pltpu.VMEM((7, tm + CONV_HALO - 8, D), F32)],
        compiler_params=_params(("arbitrary",)),
    )(p_glu, p_glu, p_glu, p_glu, dc1, dc1, dw_w)


def _rot_half(x):
    lane = lax.broadcasted_iota(jnp.int32, x.shape, 1)
    first = (lane % HEAD_DIM) < HEAD_DIM // 2
    return jnp.where(first, pltpu.roll(x, 128 - HEAD_DIM // 2, 1), pltpu.roll(x, HEAD_DIM // 2, 1))


def _rope_chunks(x, cs, sn, sign):
    outs = []
    for c in range(x.shape[1] // 128):
        xc = x[:, c * 128:(c + 1) * 128]
        outs.append(xc * cs + sign * (_rot_half(xc) * sn))
    return outs[0] if len(outs) == 1 else jnp.concatenate(outs, axis=1)


def rope_fwd(p_qkv, cs, sn, D, name):
    T, W = p_qkv.shape
    KV = (W - D) // 2
    tm = min(TM_ROW, T)
    kb = D // KV

    def body(q_ref, k_ref, v_ref, cs_ref, sn_ref, qo_ref, ko_ref, vo_ref):
        cs_v, sn_v = cs_ref[...], sn_ref[...]
        qo_ref[...] = _rope_chunks(q_ref[...], cs_v, sn_v, 1.0).astype(BF16)
        ko_ref[...] = _rope_chunks(k_ref[...], cs_v, sn_v, 1.0).astype(BF16)
        vo_ref[...] = v_ref[...].astype(BF16)

    tab = pl.BlockSpec((tm, 128), lambda i: (i, 0))
    return pl.pallas_call(
        body, name=name, grid=(T // tm,),
        in_specs=[pl.BlockSpec((tm, D), lambda i: (i, 0)), pl.BlockSpec((tm, KV), lambda i: (i, kb)),
                  pl.BlockSpec((tm, KV), lambda i: (i, kb + 1)), tab, tab],
        out_specs=[pl.BlockSpec((tm, D), lambda i: (i, 0)), pl.BlockSpec((tm, KV), lambda i: (i, 0)),
                   pl.BlockSpec((tm, KV), lambda i: (i, 0))],
        out_shape=[jax.ShapeDtypeStruct((T, D), BF16), jax.ShapeDtypeStruct((T, KV), BF16),
                   jax.ShapeDtypeStruct((T, KV), BF16)],
        compiler_params=_params(("parallel",)),
    )(p_qkv, p_qkv, p_qkv, cs, sn)


def rope_bwd(dq, dk, dv, cs, sn, name):
    T, D = dq.shape
    KV = dk.shape[1]
    tm = min(TM_ROW, T)

    def body(dq_ref, dk_ref, dv_ref, cs_ref, sn_ref, o_ref):
        cs_v, sn_v = cs_ref[...], sn_ref[...]
        o_ref[:, pl.ds(0, D)] = _rope_chunks(dq_ref[...], cs_v, sn_v, -1.0).astype(BF16)
        o_ref[:, pl.ds(D, KV)] = _rope_chunks(dk_ref[...], cs_v, sn_v, -1.0).astype(BF16)
        o_ref[:, pl.ds(D + KV, KV)] = dv_ref[...].astype(BF16)

    tab = pl.BlockSpec((tm, 128), lambda i: (i, 0))
    return pl.pallas_call(
        body, name=name, grid=(T // tm,),
        in_specs=[pl.BlockSpec((tm, D), lambda i: (i, 0)), pl.BlockSpec((tm, KV), lambda i: (i, 0)),
                  pl.BlockSpec((tm, KV), lambda i: (i, 0)), tab, tab],
        out_specs=pl.BlockSpec((tm, D + 2 * KV), lambda i: (i, 0)),
        out_shape=jax.ShapeDtypeStruct((T, D + 2 * KV), BF16),
        compiler_params=_params(("parallel",)),
    )(dq, dk, dv, cs, sn)


def _lane_lo():
    return lax.broadcasted_iota(jnp.int32, (1, 128), 1) < HEAD_DIM


def _band_mask(i, reps):
    shape = (reps * WINDOW, 2 * WINDOW)
    qi = lax.broadcasted_iota(jnp.int32, shape, 0) % WINDOW
    cj = lax.broadcasted_iota(jnp.int32, shape, 1)
    rel = qi - cj + WINDOW
    return (rel >= 0) & (rel < WINDOW) & ((i > 0) | (cj >= WINDOW))


def _stack_pairs(ref, first, n):
    parts = [ref[:, pl.ds((first + p) * 128, 128)] for p in range(n)]
    return parts[0] if n == 1 else jnp.concatenate(parts, axis=0)


def _pair_rows(n):
    return lax.broadcasted_iota(jnp.int32, (n * WINDOW, 1), 0) // WINDOW


def _per_pair_column(values, n):
    rows = _pair_rows(n)
    col = jnp.zeros((n * WINDOW, 1), F32) + values[0]
    for p in range(1, n):
        col = jnp.where(rows == p, values[p], col)
    return col


def _kv_lo_hi(x2, g):
    pair, half = divmod(g, 2)
    lo = _lane_lo()
    xg = x2[:, pair * 128:(pair + 1) * 128].astype(F32)
    xg = jnp.where(lo if half == 0 else ~lo, xg, 0.0)
    sw = pltpu.roll(xg, HEAD_DIM, 1)
    x_lo, x_hi = (xg, sw) if half == 0 else (sw, xg)
    return x_lo.astype(BF16), x_hi.astype(BF16)


def _softmax_sink(s, allowed, sink):
    s = jnp.where(allowed, s * (HEAD_DIM ** -0.5), NEG_INF)
    m = jnp.maximum(jnp.max(s, axis=-1, keepdims=True), sink)
    p = jnp.exp(s - m)
    es = jnp.exp(sink - m)
    inv = 1.0 / (jnp.sum(p, axis=-1, keepdims=True) + es)
    return p * inv, es * inv


def attn_fwd(qr, kr, vb, sinks, name):
    T, D = qr.shape
    KV = kr.shape[1]
    n_kv = KV // HEAD_DIM
    group = (D // HEAD_DIM) // n_kv
    nb = T // WINDOW

    npair = group // 2

    def body(sink_ref, q_ref, kp_ref, kc_ref, vp_ref, vc_ref, o_ref):
        i = pl.program_id(0)
        allowed = _band_mask(i, npair)
        k2 = jnp.concatenate([kp_ref[...], kc_ref[...]], axis=0)
        v2 = jnp.concatenate([vp_ref[...], vc_ref[...]], axis=0)
        outs = [None] * (D // 128)
        for g in range(n_kv):
            k_lo, k_hi = _kv_lo_hi(k2, g)
            v_lo, v_hi = _kv_lo_hi(v2, g)
            first = (g * group) // 2
            q = _stack_pairs(q_ref, first, npair)
            sink_e = _per_pair_column([sink_ref[0, g * group + 2 * p] for p in range(npair)], npair)
            sink_o = _per_pair_column([sink_ref[0, g * group + 2 * p + 1] for p in range(npair)], npair)
            pe, _ = _softmax_sink(_dot_nt(q, k_lo), allowed, sink_e)
            po, _ = _softmax_sink(_dot_nt(q, k_hi), allowed, sink_o)
            o = _dot(pe.astype(BF16), v_lo) + _dot(po.astype(BF16), v_hi)
            for p in range(npair):
                outs[first + p] = o[p * WINDOW:(p + 1) * WINDOW]
        o_ref[...] = jnp.concatenate(outs, axis=1).astype(BF16)

    prev = lambda i: (jnp.maximum(i - 1, 0), 0)
    cur = lambda i: (i, 0)
    return pl.pallas_call(
        body, name=name, grid=(nb,),
        in_specs=[pl.BlockSpec(memory_space=pltpu.SMEM),
                  pl.BlockSpec((WINDOW, D), cur),
                  pl.BlockSpec((WINDOW, KV), prev), pl.BlockSpec((WINDOW, KV), cur),
                  pl.BlockSpec((WINDOW, KV), prev), pl.BlockSpec((WINDOW, KV), cur)],
        out_specs=pl.BlockSpec((WINDOW, D), cur),
        out_shape=jax.ShapeDtypeStruct((T, D), BF16),
        compiler_params=_params(("parallel",)),
    )(sinks, qr, kr, kr, vb, vb)


def attn_bwd(qr, kr, vb, o, do, sinks, name):
    T, D = qr.shape
    KV = kr.shape[1]
    n_heads = D // HEAD_DIM
    n_kv = KV // HEAD_DIM
    group = n_heads // n_kv
    nb = T // WINDOW
    npair = group // 2
    scale = HEAD_DIM ** -0.5

    def body(sink_ref, q_ref, kp_ref, kc_ref, vp_ref, vc_ref, o_ref, do_ref,
             dq_ref, dk_ref, dv_ref, ds_ref, ck_ref, cv_ref):
        i = pl.program_id(0)
        lo = _lane_lo()

        @pl.when(i == 0)
        def _():
            ck_ref[...] = jnp.zeros_like(ck_ref)
            cv_ref[...] = jnp.zeros_like(cv_ref)
            ds_ref[...] = jnp.zeros_like(ds_ref)

        @pl.when(i < nb)
        def _():
            allowed = _band_mask(i, npair)
            rows = _pair_rows(npair)
            k2 = jnp.concatenate([kp_ref[...], kc_ref[...]], axis=0)
            v2 = jnp.concatenate([vp_ref[...], vc_ref[...]], axis=0)
            lane = lax.broadcasted_iota(jnp.int32, (1, 128), 1)
            dsink = jnp.zeros((1, 128), F32)
            dq_out = [None] * (D // 128)
            dk_pairs = [jnp.zeros((2 * WINDOW, 128), F32) for _ in range(KV // 128)]
            dv_pairs = [jnp.zeros((2 * WINDOW, 128), F32) for _ in range(KV // 128)]
            for g in range(n_kv):
                k_lo, k_hi = _kv_lo_hi(k2, g)
                v_lo, v_hi = _kv_lo_hi(v2, g)
                first = (g * group) // 2
                q = _stack_pairs(q_ref, first, npair)
                dop = _stack_pairs(do_ref, first, npair)
                dd = dop.astype(F32) * _stack_pairs(o_ref, first, npair).astype(F32)
                dq = jnp.zeros((npair * WINDOW, 128), F32)
                dkg = jnp.zeros((2 * WINDOW, 128), F32)
                dvg = jnp.zeros((2 * WINDOW, 128), F32)
                for parity, k_h, v_h, sel in ((0, k_lo, v_lo, lo), (1, k_hi, v_hi, ~lo)):
                    heads = [g * group + 2 * p + parity for p in range(npair)]
                    sink = _per_pair_column([sink_ref[0, h] for h in heads], npair)
                    p_, ps = _softmax_sink(_dot_nt(q, k_h), allowed, sink)
                    delta = jnp.sum(jnp.where(sel, dd, 0.0), axis=-1, keepdims=True)
                    dsc = (p_ * (_dot_nt(dop, v_h) - delta)).astype(BF16)
                    sd = -ps * delta
                    for p, h in enumerate(heads):
                        dsink += jnp.where(lane == h, jnp.sum(jnp.where(rows == p, sd, 0.0)), 0.0)
                    dq += _dot(dsc, k_h)
                    dkg += jnp.where(sel, _dot_tn(dsc, q), 0.0)
                    dvg += jnp.where(sel, _dot_tn(p_.astype(BF16), dop), 0.0)
                for p in range(npair):
                    dq_out[first + p] = dq[p * WINDOW:(p + 1) * WINDOW]
                pair, half = divmod(g, 2)
                keep = lo if half == 0 else ~lo
                dk_pairs[pair] += jnp.where(keep, dkg + pltpu.roll(dkg, HEAD_DIM, 1), 0.0) * scale
                dv_pairs[pair] += jnp.where(keep, dvg + pltpu.roll(dvg, HEAD_DIM, 1), 0.0)
            dq_ref[...] = jnp.concatenate(dq_out, axis=1) * scale
            dk2 = dk_pairs[0] if len(dk_pairs) == 1 else jnp.concatenate(dk_pairs, axis=1)
            dv2 = dv_pairs[0] if len(dv_pairs) == 1 else jnp.concatenate(dv_pairs, axis=1)
            dk_ref[...] = ck_ref[...] + dk2[:WINDOW]
            dv_ref[...] = cv_ref[...] + dv2[:WINDOW]
            ck_ref[...] = dk2[WINDOW:]
            cv_ref[...] = dv2[WINDOW:]
            ds_ref[pl.ds(0, 1), :] += dsink

        @pl.when(i == nb)
        def _():
            dk_ref[...] = ck_ref[...]
            dv_ref[...] = cv_ref[...]

    prev = lambda i: (jnp.maximum(i - 1, 0), 0)
    cur = lambda i: (jnp.minimum(i, nb - 1), 0)
    prevc = lambda i: (jnp.maximum(jnp.minimum(i, nb - 1) - 1, 0), 0)
    return pl.pallas_call(
        body, name=name, grid=(nb + 1,),
        in_specs=[pl.BlockSpec(memory_space=pltpu.SMEM),
                  pl.BlockSpec((WINDOW, D), cur),
                  pl.BlockSpec((WINDOW, KV), prevc), pl.BlockSpec((WINDOW, KV), cur),
                  pl.BlockSpec((WINDOW, KV), prevc), pl.BlockSpec((WINDOW, KV), cur),
                  pl.BlockSpec((WINDOW, D), cur), pl.BlockSpec((WINDOW, D), cur)],
        out_specs=[pl.BlockSpec((WINDOW, D), cur), pl.BlockSpec((WINDOW, KV), prev),
                   pl.BlockSpec((WINDOW, KV), prev), pl.BlockSpec((8, 128), lambda i: (0, 0))],
        out_shape=[jax.ShapeDtypeStruct((T, D), F32), jax.ShapeDtypeStruct((T, KV), F32),
                   jax.ShapeDtypeStruct((T, KV), F32), jax.ShapeDtypeStruct((8, 128), F32)],
        scratch_shapes=[pltpu.VMEM((WINDOW, KV), F32), pltpu.VMEM((WINDOW, KV), F32)],
        compiler_params=_params(("arbitrary",)),
    )(sinks, qr, kr, kr, vb, vb, o, do)


def merge_fwd(x, c3, o, p_gate, gate_b, w_proj, w_o, w_out, name):
    T, D = x.shape
    tm = min(TM_ROW, T)

    def body(x_ref, c3_ref, o_ref, gc_ref, ga_ref, bc_ref, ba_ref, wp_ref, wo_ref, wout_ref,
             xo_ref, co_ref, ao_ref, mg_ref):
        conv_out = _dot(c3_ref[...], wp_ref[...])
        attn_out = _dot(o_ref[...], wo_ref[...])
        merged = (_sigmoid(gc_ref[...] + bc_ref[...]) * conv_out
                  + _sigmoid(ga_ref[...] + ba_ref[...]) * attn_out).astype(BF16)
        co_ref[...] = conv_out.astype(BF16)
        ao_ref[...] = attn_out.astype(BF16)
        mg_ref[...] = merged
        xo_ref[...] = x_ref[...] + _dot(merged, wout_ref[...])

    blk = lambda j: pl.BlockSpec((tm, D), lambda i: (i, j))
    row = lambda j: pl.BlockSpec((1, D), lambda i: (0, j))
    mat = pl.BlockSpec((D, D), lambda i: (0, 0))
    return pl.pallas_call(
        body, name=name, grid=(T // tm,),
        in_specs=[blk(0), blk(0), blk(0), blk(0), blk(1), row(0), row(1), mat, mat, mat],
        out_specs=[blk(0), blk(0), blk(0), blk(0)],
        out_shape=[jax.ShapeDtypeStruct((T, D), F32)] + [jax.ShapeDtypeStruct((T, D), BF16)] * 3,
        compiler_params=_params(("parallel",)),
    )(x, c3, o, p_gate, p_gate, gate_b, gate_b, w_proj, w_o, w_out)


def merge_bwd(dx, p_gate, gate_b, conv_out, attn_out, c1, ln_g, ln_b, w_proj, w_o, w_out, name):
    T, D = dx.shape
    tm = min(TM_ROW, T)

    def body(dx_ref, gc_ref, ga_ref, bc_ref, ba_ref, co_ref, ao_ref, c1_ref, g_ref, be_ref,
             wp_ref, wo_ref, wout_ref, dgt_ref, dco_ref, dao_ref, do_ref, dc1_ref, sm_ref):
        @pl.when(pl.program_id(0) == 0)
        def _():
            sm_ref[...] = jnp.zeros_like(sm_ref)

        dm = _dot_nt(dx_ref[...].astype(BF16), wout_ref[...])
        sc = _sigmoid(gc_ref[...] + bc_ref[...])
        sa = _sigmoid(ga_ref[...] + ba_ref[...])
        dco = (dm * sc).astype(BF16)
        dao = (dm * sa).astype(BF16)
        dgc = dm * co_ref[...].astype(F32) * sc * (1.0 - sc)
        dga = dm * ao_ref[...].astype(F32) * sa * (1.0 - sa)
        dgt_ref[:, pl.ds(0, D)] = dgc.astype(BF16)
        dgt_ref[:, pl.ds(D, D)] = dga.astype(BF16)
        dco_ref[...] = dco
        dao_ref[...] = dao
        do_ref[...] = _dot_nt(dao, wo_ref[...]).astype(BF16)
        dc3 = _dot_nt(dco, wp_ref[...])
        xhat, rstd = _layernorm_stats(c1_ref[...])
        c2 = xhat * g_ref[...] + be_ref[...]
        dc2 = dc3 * _silu_grad(c2, _sigmoid(c2))
        dxh = dc2 * g_ref[...]
        dc1 = rstd * (dxh - jnp.mean(dxh, axis=-1, keepdims=True)
                      - xhat * jnp.mean(dxh * xhat, axis=-1, keepdims=True))
        dc1_ref[...] = dc1
        colsum = lambda v: jnp.sum(v, axis=0, keepdims=True)
        for r, (left, right) in enumerate(((dgc, dga), (dc2 * xhat, dc2), (dc1, None))):
            sm_ref[pl.ds(r, 1), pl.ds(0, D)] += colsum(left)
            if right is not None:
                sm_ref[pl.ds(r, 1), pl.ds(D, D)] += colsum(right)

    blk = lambda j: pl.BlockSpec((tm, D), lambda i: (i, j))
    row = lambda j: pl.BlockSpec((1, D), lambda i: (0, j))
    mat = pl.BlockSpec((D, D), lambda i: (0, 0))
    return pl.pallas_call(
        body, name=name, grid=(T // tm,),
        in_specs=[blk(0), blk(0), blk(1), row(0), row(1), blk(0), blk(0), blk(0), row(0), row(0), mat, mat, mat],
        out_specs=[pl.BlockSpec((tm, 2 * D), lambda i: (i, 0)), blk(0), blk(0), blk(0), blk(0),
                   pl.BlockSpec((8, 2 * D), lambda i: (0, 0))],
        out_shape=[jax.ShapeDtypeStruct((T, 2 * D), BF16)] + [jax.ShapeDtypeStruct((T, D), BF16)] * 3
                  + [jax.ShapeDtypeStruct((T, D), F32), jax.ShapeDtypeStruct((8, 2 * D), F32)],
        compiler_params=_params(("arbitrary",)),
    )(dx, p_gate, p_gate, gate_b, gate_b, conv_out, attn_out, c1, ln_g, ln_b, w_proj, w_o, w_out)


def loss_head(x, nw, target, name):
    T, D = x.shape
    tm = min(TM_ROW, T)

    def body(x_ref, nw_ref, t_ref, dx_ref, sm_ref):
        @pl.when(pl.program_id(0) == 0)
        def _():
            sm_ref[...] = jnp.zeros_like(sm_ref)

        xv = x_ref[...]
        err = xv * _rms_scale(xv) * nw_ref[...] - t_ref[...]
        loss = 0.5 * jnp.sum(jnp.mean(err * err, axis=-1, keepdims=True))
        dxn, dnw = _rms_bwd(xv, nw_ref[...], err * (1.0 / D))
        dx_ref[...] = dxn
        sm_ref[pl.ds(0, 1), :] += dnw
        sm_ref[pl.ds(1, 1), :] += jnp.zeros((1, D), F32) + loss

    return pl.pallas_call(
        body, name=name, grid=(T // tm,),
        in_specs=[pl.BlockSpec((tm, D), lambda i: (i, 0)), pl.BlockSpec((1, D), lambda i: (0, 0)),
                  pl.BlockSpec((tm, D), lambda i: (i, 0))],
        out_specs=[pl.BlockSpec((tm, D), lambda i: (i, 0)), pl.BlockSpec((8, D), lambda i: (0, 0))],
        out_shape=[jax.ShapeDtypeStruct((T, D), F32), jax.ShapeDtypeStruct((8, D), F32)],
        compiler_params=_params(("arbitrary",)),
    )(x, nw, target)


def adamw(w, g, m, v, name):
    R, C = w.shape
    tr = _row_tile(R, TR_ELT)

    def body(w_ref, g_ref, m_ref, v_ref, d_ref, mo_ref, vo_ref):
        gv = g_ref[...]
        mn = ADAM_B1 * m_ref[...] + (1.0 - ADAM_B1) * gv
        vn = ADAM_B2 * v_ref[...] + (1.0 - ADAM_B2) * (gv * gv)
        m_hat = mn / (1.0 - ADAM_B1 ** ADAM_STEP)
        v_hat = vn / (1.0 - ADAM_B2 ** ADAM_STEP)
        d_ref[...] = -ADAM_LR * (m_hat / (jnp.sqrt(v_hat) + ADAM_EPS) + ADAM_WD * w_ref[...])
        mo_ref[...] = mn
        vo_ref[...] = vn

    spec = pl.BlockSpec((tr, C), lambda i: (i, 0))
    return pl.pallas_call(
        body, name=name, grid=(R // tr,), in_specs=[spec] * 4, out_specs=[spec] * 3,
        out_shape=[jax.ShapeDtypeStruct((R, C), F32)] * 3,
        compiler_params=_params(("parallel",)),
    )(w, g, m, v)


def _place():
    return lax.axis_index("x"), lax.axis_index("y"), lax.axis_index("c")


def place_shard(place, w, dtype, name):
    R, C = w.shape
    tr = _row_tile(R, TR_ELT)

    def body(pc_ref, w_ref, o_ref):
        o_ref[...] = w_ref[...].astype(dtype)

    return pl.pallas_call(
        body, name=name,
        grid_spec=pltpu.PrefetchScalarGridSpec(
            num_scalar_prefetch=1, grid=(R // tr,),
            in_specs=[pl.BlockSpec((tr, C), lambda r, pc: (r, 0))],
            out_specs=pl.BlockSpec((None, tr, C), lambda r, pc: (pc[0], r, 0))),
        out_shape=jax.ShapeDtypeStruct((N_CHIPS, R, C), dtype),
        compiler_params=_params(("arbitrary",)),
    )(place, w)


def gather_side(shards, small):
    n, ns = len(shards), len(small)

    def ici_copy(dst, sems, k, j, x, y, c, sending):
        px, py = x ^ (j >> 1), y ^ (j & 1)
        slot = 2 * x + y if sending else 2 * px + py
        half = dst[k].shape[1] // 2
        part = dst[k].at[slot, pl.ds(c * half, half)] if k < n else dst[k].at[slot]
        return pltpu.make_async_remote_copy(part, part, sems[0].at[3 * k + j - 1], sems[1].at[3 * k + j - 1],
                                            device_id=(px, py, c), device_id_type=MESH)

    def d2d_copy(dst, sems, k, j, x, y, c, sending):
        half = dst[k].shape[1] // 2
        part = dst[k].at[2 * (x ^ (j >> 1)) + (y ^ (j & 1)), pl.ds((c if sending else 1 - c) * half, half)]
        return pltpu.make_async_remote_copy(part, part, sems[2].at[3 * k + j - 1], sems[3].at[3 * k + j - 1],
                                            device_id=(x, y, 1 - c), device_id_type=MESH)

    def start(src, dst, sems):
        x, y, c = _place()
        for k in range(n + ns):
            for j in (1, 2, 3):
                ici_copy(dst, sems, k, j, x, y, c, True).start()

    def finish(src, dst, sems):
        x, y, c = _place()
        for k in range(n + ns):
            for j in (1, 2, 3):
                ici_copy(dst, sems, k, j, x, y, c, False).wait_recv()
                if k < n:
                    d2d_copy(dst, sems, k, j, x, y, c, True).start()
        for k in range(n):
            for j in (1, 2, 3):
                d2d_copy(dst, sems, k, j, x, y, c, False).wait_recv()
        for k in range(n + ns):
            for j in (1, 2, 3):
                ici_copy(dst, sems, k, j, x, y, c, True).wait_send()
                if k < n:
                    d2d_copy(dst, sems, k, j, x, y, c, True).wait_send()

    arrays = list(shards) + list(small)
    return dict(inputs=arrays, out_shapes=[jax.ShapeDtypeStruct(a.shape, a.dtype) for a in arrays],
                aliases={k: k for k in range(n + ns)},
                sems=[pltpu.SemaphoreType.DMA((3 * (n + ns),)), pltpu.SemaphoreType.DMA((3 * (n + ns),)),
                      pltpu.SemaphoreType.DMA((3 * n,)), pltpu.SemaphoreType.DMA((3 * n,))],
                start=start, finish=finish)


def run_side(side, name):
    n_in, n_out = len(side["inputs"]), len(side["out_shapes"])

    def body(*refs):
        src, dst, sems = refs[:n_in], refs[n_in:n_in + n_out], refs[n_in + n_out:]
        side["start"](src, dst, sems)
        side["finish"](src, dst, sems)

    return pl.pallas_call(
        body, name=name, in_specs=[HBM_SPEC] * n_in, out_specs=[HBM_SPEC] * n_out,
        out_shape=side["out_shapes"], input_output_aliases=side["aliases"], scratch_shapes=side["sems"],
    )(*side["inputs"])


def allreduce_small(block):
    R, C = block.shape

    def body(x_ref, out_ref, all_ref, send_sems, recv_sems, local_sem):
        x, y, c = _place()
        me, sibling = (x, y, c), (x, y, 1 - c)
        chips = [(1 - x, y), (x, 1 - y), (1 - x, 1 - y)]

        def slot(px, py, pc):
            return all_ref.at[4 * px + 2 * py + pc]

        def copy(k, block_of, to, src=None):
            return pltpu.make_async_remote_copy(
                src_ref=slot(*block_of) if src is None else src, dst_ref=slot(*block_of),
                send_sem=send_sems.at[k], recv_sem=recv_sems.at[k], device_id=to, device_id_type=MESH)

        mine = pltpu.make_async_copy(x_ref, slot(*me), local_sem)
        mine.start()
        first = [copy(0, me, sibling, src=x_ref)]
        first += [copy(1 + j, me, (*chip, c), src=x_ref) for j, chip in enumerate(chips)]
        for cp in first:
            cp.start()
        passed = [copy(4 + j, (*chip, c), sibling) for j, chip in enumerate(chips)]
        for j, chip in enumerate(chips):
            copy(1 + j, (*chip, c), me).wait_recv()
            passed[j].start()
        copy(0, sibling, me).wait_recv()
        for j, chip in enumerate(chips):
            copy(4 + j, (*chip, 1 - c), me).wait_recv()
        for cp in first + passed:
            cp.wait_send()
        mine.wait()
        total = all_ref[0]
        for d in range(1, N_DEV):
            total = total + all_ref[d]
        out_ref[...] = total

    return pl.pallas_call(
        body, name="allreduce_small",
        in_specs=[pl.BlockSpec(memory_space=pltpu.VMEM)], out_specs=pl.BlockSpec(memory_space=pltpu.VMEM),
        out_shape=jax.ShapeDtypeStruct((R, C), F32),
        scratch_shapes=[pltpu.VMEM((N_DEV, R, C), F32), pltpu.SemaphoreType.DMA((7,)),
                        pltpu.SemaphoreType.DMA((7,)), pltpu.SemaphoreType.DMA],
        compiler_params=pltpu.CompilerParams(vmem_limit_bytes=VMEM_LIMIT),
    )(block)


def rs_exchange_siblings(grads, name):
    n = len(grads)

    def body(*refs):
        src, dst = refs[:n], refs[n:2 * n]
        send_sems, recv_sems = refs[2 * n:]
        x, y, c = _place()
        copies = []
        for k in range(n):
            half = src[k].shape[1] // 2
            cp = pltpu.make_async_remote_copy(src[k].at[:, pl.ds((1 - c) * half, half)], dst[k],
                                              send_sems.at[k], recv_sems.at[k],
                                              device_id=(x, y, 1 - c), device_id_type=MESH)
            cp.start()
            copies.append(cp)
        for cp in copies:
            cp.wait()

    return pl.pallas_call(
        body, name=name,
        in_specs=[HBM_SPEC] * n, out_specs=[HBM_SPEC] * n,
        out_shape=[jax.ShapeDtypeStruct((N_CHIPS, g.shape[1] // 2, g.shape[2]), F32) for g in grads],
        scratch_shapes=[pltpu.SemaphoreType.DMA((n,)), pltpu.SemaphoreType.DMA((n,))],
    )(*grads)


def rs_chip_sum(place, grad, sib, name):
    NP, R, C = grad.shape
    half = R // 2
    tr = _row_tile(half, TR_ELT)
    nr = half // tr

    def body(pc_ref, g_ref, s_ref, wire_ref, own_ref):
        q = pl.program_id(1)
        total = g_ref[...] + s_ref[...]
        wire_ref[...] = total.astype(BF16)

        @pl.when(q == pc_ref[0])
        def _():
            own_ref[...] = total

    return pl.pallas_call(
        body, name=name,
        grid_spec=pltpu.PrefetchScalarGridSpec(
            num_scalar_prefetch=1, grid=(nr, NP),
            in_specs=[pl.BlockSpec((None, tr, C), lambda r, q, pc: (q, pc[1] * nr + r, 0)),
                      pl.BlockSpec((None, tr, C), lambda r, q, pc: (q, r, 0))],
            out_specs=[pl.BlockSpec((None, tr, C), lambda r, q, pc: (q, r, 0)),
                       pl.BlockSpec((tr, C), lambda r, q, pc: (r, 0))]),
        out_shape=[jax.ShapeDtypeStruct((NP, half, C), BF16), jax.ShapeDtypeStruct((half, C), F32)],
        compiler_params=_params(("arbitrary", "arbitrary")),
    )(place, grad, sib)


def exchange_chips_side(wires):
    n = len(wires)

    def copies(src, dst, sems):
        x, y, c = _place()
        for k in range(n):
            for j in (1, 2, 3):
                qx, qy = x ^ (j >> 1), y ^ (j & 1)
                yield pltpu.make_async_remote_copy(src[k].at[2 * qx + qy], dst[k].at[2 * x + y],
                                                   sems[0].at[3 * k + j - 1], sems[1].at[3 * k + j - 1],
                                                   device_id=(qx, qy, c), device_id_type=MESH)

    def start(src, dst, sems):
        for cp in copies(src, dst, sems):
            cp.start()

    def finish(src, dst, sems):
        for cp in copies(src, dst, sems):
            cp.wait()

    return dict(inputs=list(wires), out_shapes=[jax.ShapeDtypeStruct(w.shape, BF16) for w in wires], aliases={},
                sems=[pltpu.SemaphoreType.DMA((3 * n,)), pltpu.SemaphoreType.DMA((3 * n,))],
                start=start, finish=finish)


def rs_final_sum(place, own, got, name):
    NP, half, C = got.shape
    tr = _row_tile(half, TR_ELT)
    nr = half // tr

    def body(pc_ref, own_ref, g1_ref, g2_ref, g3_ref, out_ref):
        out_ref[...] = ((own_ref[...] + g1_ref[...].astype(F32)) + g2_ref[...].astype(F32)) + g3_ref[...].astype(F32)

    slot = lambda j: pl.BlockSpec((None, tr, C), lambda r, pc: (pc[0] ^ j, r, 0))
    return pl.pallas_call(
        body, name=name,
        grid_spec=pltpu.PrefetchScalarGridSpec(
            num_scalar_prefetch=1, grid=(nr,),
            in_specs=[pl.BlockSpec((tr, C), lambda r, pc: (r, 0)), slot(1), slot(2), slot(3)],
            out_specs=pl.BlockSpec((tr, C), lambda r, pc: (pc[1] * nr + r, 0))),
        out_shape=jax.ShapeDtypeStruct((2 * half, C), F32),
        compiler_params=_params(("arbitrary",)),
    )(place, own, got, got, got)


def rs_share_siblings(totals):
    n = len(totals)

    def body(*refs):
        dst = refs[n:2 * n]
        send_sems, recv_sems = refs[2 * n:]
        x, y, c = _place()
        copies = []
        for k in range(n):
            half = dst[k].shape[0] // 2
            rows = dst[k].at[pl.ds(c * half, half)]
            cp = pltpu.make_async_remote_copy(rows, rows, send_sems.at[k], recv_sems.at[k],
                                              device_id=(x, y, 1 - c), device_id_type=MESH)
            cp.start()
            copies.append(cp)
        for k, cp in enumerate(copies):
            cp.wait_send()
            half = dst[k].shape[0] // 2
            got = dst[k].at[pl.ds((1 - c) * half, half)]
            pltpu.make_async_remote_copy(got, got, send_sems.at[k], recv_sems.at[k],
                                         device_id=(x, y, c), device_id_type=MESH).wait_recv()

    return pl.pallas_call(
        body, name="rs_share_siblings",
        in_specs=[HBM_SPEC] * n, out_specs=[HBM_SPEC] * n,
        out_shape=[jax.ShapeDtypeStruct(t.shape, F32) for t in totals],
        input_output_aliases={k: k for k in range(n)},
        scratch_shapes=[pltpu.SemaphoreType.DMA((n,)), pltpu.SemaphoreType.DMA((n,))],
    )(*totals)


def rs_to_wires(place, grads, tag):
    sibs = rs_exchange_siblings(grads, f"rs_exchange_siblings_{tag}")
    wires, owns = [], []
    for k, (g, s) in enumerate(zip(grads, sibs)):
        w, o = rs_chip_sum(place, g, s, f"rs_chip_sum_{tag}{k}")
        wires.append(w)
        owns.append(o)
    return wires, owns


def rs_finish(place, owns, gots):
    totals = [rs_final_sum(place, o, g, f"rs_final_sum_{k}") for k, (o, g) in enumerate(zip(owns, gots))]
    return rs_share_siblings(totals)


def _rope_tables(positions):
    half = HEAD_DIM // 2
    inv_freq = ROPE_THETA ** (-jnp.arange(half, dtype=F32) / half)
    ang = positions.astype(F32)[:, None] * inv_freq
    cos, sin = jnp.cos(ang), jnp.sin(ang)
    return jnp.tile(cos, (1, 4)), jnp.concatenate([-sin, sin, -sin, sin], axis=1)


def _pieces_from_cols(full):
    D, W = full.shape
    return full.reshape(D, N_CHIPS, W // N_CHIPS).transpose(1, 0, 2)


def kernel(x, positions, ffn1_norm, ffn1_w_gate, ffn1_w_up, ffn1_w_down, mix_norm, w_in, conv_dw_w, conv_dw_b, conv_ln_g, conv_ln_b, conv_w_proj, attn_sinks, attn_w_o, gate_b, w_out, ffn2_norm, ffn2_w_gate, ffn2_w_up, ffn2_w_down, final_norm, loss_target, m_ffn1_norm, m_ffn1_w_gate, m_ffn1_w_up, m_ffn1_w_down, m_mix_norm, m_w_in, m_conv_dw_w, m_conv_dw_b, m_conv_ln_g, m_conv_ln_b, m_conv_w_proj, m_attn_sinks, m_attn_w_o, m_gate_b, m_w_out, m_ffn2_norm, m_ffn2_w_gate, m_ffn2_w_up, m_ffn2_w_down, m_final_norm, v_ffn1_norm, v_ffn1_w_gate, v_ffn1_w_up, v_ffn1_w_down, v_mix_norm, v_w_in, v_conv_dw_w, v_conv_dw_b, v_conv_ln_g, v_conv_ln_b, v_conv_w_proj, v_attn_sinks, v_attn_w_o, v_gate_b, v_w_out, v_ffn2_norm, v_ffn2_w_gate, v_ffn2_w_up, v_ffn2_w_down, v_final_norm):
    weights = dict(ffn1_norm=ffn1_norm, ffn1_w_gate=ffn1_w_gate, ffn1_w_up=ffn1_w_up, ffn1_w_down=ffn1_w_down,
                   mix_norm=mix_norm, w_in=w_in, conv_dw_w=conv_dw_w, conv_dw_b=conv_dw_b, conv_ln_g=conv_ln_g,
                   conv_ln_b=conv_ln_b, conv_w_proj=conv_w_proj, attn_sinks=attn_sinks, attn_w_o=attn_w_o,
                   gate_b=gate_b, w_out=w_out, ffn2_norm=ffn2_norm, ffn2_w_gate=ffn2_w_gate, ffn2_w_up=ffn2_w_up,
                   ffn2_w_down=ffn2_w_down, final_norm=final_norm)
    m_in = dict(ffn1_norm=m_ffn1_norm, ffn1_w_gate=m_ffn1_w_gate, ffn1_w_up=m_ffn1_w_up, ffn1_w_down=m_ffn1_w_down,
                mix_norm=m_mix_norm, w_in=m_w_in, conv_dw_w=m_conv_dw_w, conv_dw_b=m_conv_dw_b,
                conv_ln_g=m_conv_ln_g, conv_ln_b=m_conv_ln_b, conv_w_proj=m_conv_w_proj, attn_sinks=m_attn_sinks,
                attn_w_o=m_attn_w_o, gate_b=m_gate_b, w_out=m_w_out, ffn2_norm=m_ffn2_norm,
                ffn2_w_gate=m_ffn2_w_gate, ffn2_w_up=m_ffn2_w_up, ffn2_w_down=m_ffn2_w_down, final_norm=m_final_norm)
    v_in = dict(ffn1_norm=v_ffn1_norm, ffn1_w_gate=v_ffn1_w_gate, ffn1_w_up=v_ffn1_w_up, ffn1_w_down=v_ffn1_w_down,
                mix_norm=v_mix_norm, w_in=v_w_in, conv_dw_w=v_conv_dw_w, conv_dw_b=v_conv_dw_b,
                conv_ln_g=v_conv_ln_g, conv_ln_b=v_conv_ln_b, conv_w_proj=v_conv_w_proj, attn_sinks=v_attn_sinks,
                attn_w_o=v_attn_w_o, gate_b=v_gate_b, w_out=v_w_out, ffn2_norm=v_ffn2_norm,
                ffn2_w_gate=v_ffn2_w_gate, ffn2_w_up=v_ffn2_w_up, ffn2_w_down=v_ffn2_w_down, final_norm=v_final_norm)
    names = list(weights)
    big = ["ffn1_w_gate", "ffn1_w_up", "ffn1_w_down", "w_in", "conv_w_proj", "attn_w_o", "w_out",
           "ffn2_w_gate", "ffn2_w_up", "ffn2_w_down"]

    xs = x[0]
    T, D = xs.shape
    KV = (w_in.shape[2] * N_CHIPS - 5 * D) // 2
    n_heads = D // HEAD_DIM
    my_chip = 2 * lax.axis_index("x") + lax.axis_index("y")
    place = jnp.stack([my_chip, lax.axis_index("c")]).astype(jnp.int32)

    placed = {k: place_shard(place, weights[k][0], BF16, f"place_{k}") for k in big}
    placed_dw = place_shard(place, conv_dw_w[0], F32, "place_conv_dw_w")
    first, later = big[:3], big[3:]
    wg1, wu1, wd1 = run_side(gather_side([placed[k] for k in first], []), "gather_ffn1")
    x1, h1, g1, u1, *gathered = ffn_fwd(x[0], ffn1_norm, wg1, wu1, wd1, "ffn1_fwd",
                                        side=gather_side([placed[k] for k in later], [placed_dw]))
    full = dict(zip(later + ["conv_dw_w"], gathered))
    wg2, wu2, wd2 = full["ffn2_w_gate"], full["ffn2_w_up"], full["ffn2_w_down"]
    w_in_full = full["w_in"].transpose(1, 0, 2).reshape(D, -1)
    w_glu, w_qkv, w_gate = w_in_full[:, :2 * D], w_in_full[:, 2 * D:3 * D + 2 * KV], w_in_full[:, 3 * D + 2 * KV:]
    w_proj = full["conv_w_proj"].reshape(D, D)
    w_o = full["attn_w_o"].reshape(D, D)
    w_out_f = full["w_out"].reshape(D, D)
    dw_w = full["conv_dw_w"].transpose(1, 0, 2).reshape(CONV_WIDTH, D)
    dw_w = jnp.concatenate([dw_w, jnp.zeros((CONV_HALO - CONV_WIDTH, D), F32)], axis=0)
    cs, sn = _rope_tables(positions[0])
    fn_row = final_norm.reshape(1, D)

    h2 = rmsnorm_fwd(x1, mix_norm, "mix_norm_fwd")
    p_glu = matmul_nn(h2, w_glu, "mix_in_glu")
    p_qkv = matmul_nn(h2, w_qkv, "mix_in_qkv")
    p_gate = matmul_nn(h2, w_gate, "mix_in_gate")
    c1, c3 = conv_fwd(p_glu, dw_w, conv_dw_b, conv_ln_g, conv_ln_b, "conv_fwd")
    qr, kr, vb = rope_fwd(p_qkv, cs, sn, D, "rope_fwd")
    o = attn_fwd(qr, kr, vb, attn_sinks, "attn_fwd")
    x2, conv_out, attn_out, merged = merge_fwd(x1, c3, o, p_gate, gate_b, w_proj, w_o, w_out_f, "merge_fwd")
    x3, h3, g2, u2 = ffn_fwd(x2, ffn2_norm, wg2, wu2, wd2, "ffn2_fwd")

    dx3, head_sums = loss_head(x3, fn_row, loss_target[0], "loss_head")
    dx2, dg2, du2, d_ffn2_norm = ffn_bwd_x(x2, ffn2_norm, g2, u2, wg2, wu2, wd2, dx3, "ffn2_bwd_x")
    dwg2, dwu2, dwd2 = ffn_bwd_w(h3, g2, u2, dg2, du2, dx3, "ffn2_bwd_w")
    d_gates, d_conv_out, d_attn_out, d_o, dc1, merge_sums = merge_bwd(
        dx2, p_gate, gate_b, conv_out, attn_out, c1, conv_ln_g, conv_ln_b, w_proj, w_o, w_out_f, "merge_bwd")
    d_w_out = matmul_tn(merged, dx2, "d_w_out")
    d_w_proj = matmul_tn(c3, d_conv_out, "d_conv_w_proj")
    d_w_o = matmul_tn(o, d_attn_out, "d_attn_w_o")
    d_glu, d_dw_w = conv_bwd(p_glu, dc1, dw_w, "conv_bwd")
    dq, dk, dv, d_sinks = attn_bwd(qr, kr, vb, o, d_o, attn_sinks, "attn_bwd")
    d_qkv = rope_bwd(dq, dk, dv, cs, sn, "rope_bwd")
    dx1, d_mix_norm = mix_in_bwd([d_glu, d_qkv, d_gates], [w_glu, w_qkv, w_gate], x1, mix_norm, dx2, "mix_in_bwd")
    d_w_in = jnp.concatenate([matmul_tn(h2, d_glu, "d_w_in_glu"), matmul_tn(h2, d_qkv, "d_w_in_qkv"),
                              matmul_tn(h2, d_gates, "d_w_in_gate")], axis=1)
    dwc = D // N_CHIPS
    early = [dwg2, dwu2, dwd2, _pieces_from_cols(d_w_in), d_w_proj.reshape(N_CHIPS, dwc, D),
             d_w_o.reshape(N_CHIPS, dwc, D), d_w_out.reshape(N_CHIPS, dwc, D)]
    wires_e, owns_e = rs_to_wires(place, early, "early")
    dx0, dg1, du1, d_ffn1_norm, *gots_e = ffn_bwd_x(xs, ffn1_norm, g1, u1, wg1, wu1, wd1, dx1, "ffn1_bwd_x",
                                                    side=exchange_chips_side(wires_e))
    dwg1, dwu1, dwd1 = ffn_bwd_w(h1, g1, u1, dg1, du1, dx1, "ffn1_bwd_w")
    wires_l, owns_l = rs_to_wires(place, [dwg1, dwu1, dwd1], "late")
    gots_l = run_side(exchange_chips_side(wires_l), "rs_exchange_chips_late")
    reduced = rs_finish(place, owns_e + owns_l, list(gots_e) + list(gots_l))

    pad_row = lambda v: jnp.pad(v, ((0, 0), (0, D - v.shape[1])))
    small_rows = jnp.concatenate([
        d_ffn1_norm, d_mix_norm, merge_sums[2:3, :D], merge_sums[1:2, :D], merge_sums[1:2, D:],
        pad_row(d_sinks[0:1, :n_heads]), merge_sums[0:1, :D], merge_sums[0:1, D:], d_ffn2_norm,
        head_sums[0:1], head_sums[1:2], jnp.zeros((5, D), F32), d_dw_w], axis=0)
    small = allreduce_small(small_rows)
    loss = small[10, 0]
    grads = {"ffn1_norm": small[0:1], "mix_norm": small[1:2], "conv_dw_b": small[2:3], "conv_ln_g": small[3:4],
             "conv_ln_b": small[4:5], "attn_sinks": small[5:6, :n_heads],
             "gate_b": jnp.concatenate([small[6:7], small[7:8]], axis=1), "ffn2_norm": small[8:9],
             "final_norm": small[9:10]}
    grads["conv_dw_w"] = lax.dynamic_slice(small[16:16 + CONV_WIDTH], (0, my_chip * dwc), (CONV_WIDTH, dwc))
    grads.update(zip(["ffn2_w_gate", "ffn2_w_up", "ffn2_w_down", "w_in", "conv_w_proj", "attn_w_o", "w_out",
                      "ffn1_w_gate", "ffn1_w_up", "ffn1_w_down"], reduced))

    deltas, new_m, new_v = {}, {}, {}
    for k in names:
        shape = weights[k].shape
        g2d = grads[k].reshape(-1, shape[-1])
        grads[k] = g2d.reshape(shape)
        d, mn, vn = adamw(weights[k].reshape(g2d.shape), g2d, m_in[k].reshape(g2d.shape),
                          v_in[k].reshape(g2d.shape), f"adamw_{k}")
        deltas[k], new_m[k], new_v[k] = d.reshape(shape), mn.reshape(shape), vn.reshape(shape)

    return (loss, dx0[None], *[grads[k] for k in names], *[deltas[k] for k in names],
            *[new_m[k] for k in names], *[new_v[k] for k in names])
```

```python
import functools

import jax
import jax.numpy as jnp
from jax import lax
from jax.experimental import pallas as pl
from jax.experimental.pallas import tpu as pltpu

F32 = jnp.float32
BF16 = jnp.bfloat16
MESH = pl.DeviceIdType.MESH

HEAD_DIM = 64
WINDOW = 128
CONV_WIDTH = 31
CONV_HALO = 32
ROPE_THETA = 10000.0
EPS = 1e-6
LN_EPS = 1e-5
NEG_INF = -1e30
N_CHIPS = 4
N_DEV = 8

ADAM_LR = 0.001
ADAM_B1 = 0.9
ADAM_B2 = 0.999
ADAM_EPS = 1e-08
ADAM_WD = 0.01
ADAM_STEP = 10

TM_FFN = 512
TM_FFN_FWD = 1024
TM_ROW = 256
TK_TN = 1024
TR_ELT = 256
VMEM_LIMIT = 56 * 1024 * 1024

NT_DIMS = (((1,), (1,)), ((), ()))
TN_DIMS = (((0,), (0,)), ((), ()))


def _row_tile(rows, cap):
    for t in range(min(cap, rows), 15, -1):
        if rows % t == 0 and t % 16 == 0:
            return t
    return rows


def _params(sem):
    return pltpu.CompilerParams(dimension_semantics=sem, vmem_limit_bytes=VMEM_LIMIT)


def _dot(a, b):
    return jnp.dot(a, b, preferred_element_type=F32)


def _dot_nt(a, b):
    return lax.dot_general(a, b, NT_DIMS, preferred_element_type=F32)


def _dot_tn(a, b):
    return lax.dot_general(a, b, TN_DIMS, preferred_element_type=F32)


def _split_rows(dot, a, b):
    m = a.shape[0] // 2
    return jnp.concatenate([dot(a[:m], b), dot(a[m:], b)], axis=0)


def _sigmoid(x):
    return jax.nn.sigmoid(x)


def _rms_scale(xv):
    return lax.rsqrt(jnp.mean(xv * xv, axis=-1, keepdims=True) + EPS)


def _rms_bwd(xv, nw, dh):
    r = _rms_scale(xv)
    dn = dh * nw
    dx = r * dn - xv * (r * r * r) * jnp.mean(dn * xv, axis=-1, keepdims=True)
    dnw = jnp.sum(dh * (xv * r), axis=0, keepdims=True)
    return dx, dnw


def _silu_grad(z, s):
    return s * (1.0 + z * (1.0 - s))


HBM_SPEC = pl.BlockSpec(memory_space=pl.ANY)


def _call_hosting(body, side, *, grid, in_specs, out_specs, out_shape, scratch_shapes, operands, name):
    params = _params(("arbitrary",) * len(grid))
    if side is None:
        return pl.pallas_call(body, name=name, grid=grid, in_specs=in_specs, out_specs=out_specs, out_shape=out_shape,
                              scratch_shapes=scratch_shapes, compiler_params=params)(*operands)
    n_in, n_out, n_scr = len(in_specs), len(out_shape), len(scratch_shapes)
    s_in, s_out = len(side["inputs"]), len(side["out_shapes"])

    def at_step(end):
        hit = pl.program_id(0) == (grid[0] - 1 if end else 0)
        for a in range(1, len(grid)):
            hit &= pl.program_id(a) == (grid[a] - 1 if end else 0)
        return hit

    def hosted(*refs):
        b = n_in + s_in
        c = b + n_out
        d = c + s_out
        e = d + n_scr
        src, dst, sems = refs[n_in:b], refs[c:d], refs[e:]

        @pl.when(at_step(False))
        def _():
            side["start"](src, dst, sems)

        body(*refs[:n_in], *refs[b:c], *refs[d:e])

        @pl.when(at_step(True))
        def _():
            side["finish"](src, dst, sems)

    return pl.pallas_call(
        hosted, name=name, grid=grid, in_specs=list(in_specs) + [HBM_SPEC] * s_in,
        out_specs=list(out_specs) + [HBM_SPEC] * s_out, out_shape=list(out_shape) + list(side["out_shapes"]),
        scratch_shapes=list(scratch_shapes) + list(side["sems"]),
        input_output_aliases={n_in + a: n_out + b for a, b in side["aliases"].items()},
        compiler_params=params)(*operands, *side["inputs"])


def ffn_fwd(x, nw, wg, wu, wd, name, side=None):
    T, D = x.shape
    NP, _, Fs = wg.shape
    tm = min(TM_FFN_FWD, T)

    def body(x_ref, nw_ref, wg_ref, wu_ref, wd_ref, xo_ref, h_ref, g_ref, u_ref, acc_ref):
        j = pl.program_id(1)

        @pl.when(j == 0)
        def _():
            xv = x_ref[...]
            h_ref[...] = (xv * _rms_scale(xv) * nw_ref[...]).astype(BF16)
            acc_ref[...] = jnp.zeros_like(acc_ref)

        h = h_ref[...]
        g = _dot(h, wg_ref[...])
        u = _dot(h, wu_ref[...])
        a = (g * _sigmoid(g)) * u
        g_ref[...] = g.astype(BF16)
        u_ref[...] = u.astype(BF16)
        acc_ref[...] += _dot(a.astype(BF16), wd_ref[...])

        @pl.when(j == NP - 1)
        def _():
            xo_ref[...] = x_ref[...] + 0.5 * acc_ref[...]

    return _call_hosting(
        body, side, name=name, grid=(T // tm, NP),
        in_specs=[pl.BlockSpec((tm, D), lambda i, j: (i, 0)),
                  pl.BlockSpec((1, D), lambda i, j: (0, 0)),
                  pl.BlockSpec((None, D, Fs), lambda i, j: (j, 0, 0)),
                  pl.BlockSpec((None, D, Fs), lambda i, j: (j, 0, 0)),
                  pl.BlockSpec((None, Fs, D), lambda i, j: (j, 0, 0))],
        out_specs=[pl.BlockSpec((tm, D), lambda i, j: (i, 0)),
                   pl.BlockSpec((tm, D), lambda i, j: (i, 0)),
                   pl.BlockSpec((None, tm, Fs), lambda i, j: (j, i, 0)),
                   pl.BlockSpec((None, tm, Fs), lambda i, j: (j, i, 0))],
        out_shape=[jax.ShapeDtypeStruct((T, D), F32), jax.ShapeDtypeStruct((T, D), BF16),
                   jax.ShapeDtypeStruct((NP, T, Fs), BF16), jax.ShapeDtypeStruct((NP, T, Fs), BF16)],
        scratch_shapes=[pltpu.VMEM((tm, D), F32)],
        operands=(x, nw, wg, wu, wd))


def ffn_bwd_x(x, nw, g, u, wg, wu, wd, dout, name, side=None):
    T, D = x.shape
    NP, _, Fs = wg.shape
    tm = min(TM_FFN, T)

    def body(x_ref, nw_ref, g_ref, u_ref, wg_ref, wu_ref, wd_ref, do_ref,
             dx_ref, dg_ref, du_ref, a_ref, dnw_ref, dh_ref, dob_ref):
        i = pl.program_id(0)
        j = pl.program_id(1)

        @pl.when((i == 0) & (j == 0))
        def _():
            dnw_ref[...] = jnp.zeros_like(dnw_ref)

        @pl.when(j == 0)
        def _():
            dh_ref[...] = jnp.zeros_like(dh_ref)
            dob_ref[...] = (0.5 * do_ref[...]).astype(BF16)

        da = _split_rows(_dot_nt, dob_ref[...], wd_ref[...])
        gf = g_ref[...].astype(F32)
        uf = u_ref[...].astype(F32)
        s = _sigmoid(gf)
        act = gf * s
        dg = (da * uf * _silu_grad(gf, s)).astype(BF16)
        du = (da * act).astype(BF16)
        dg_ref[...] = dg
        du_ref[...] = du
        a_ref[...] = (act * uf).astype(BF16)
        dh_ref[...] += _dot_nt(dg, wg_ref[...]) + _dot_nt(du, wu_ref[...])

        @pl.when(j == NP - 1)
        def _():
            dxn, dnw = _rms_bwd(x_ref[...], nw_ref[...], dh_ref[...])
            dx_ref[...] = do_ref[...] + dxn
            dnw_ref[...] += dnw

    return _call_hosting(
        body, side, name=name, grid=(T // tm, NP),
        in_specs=[pl.BlockSpec((tm, D), lambda i, j: (i, 0)),
                  pl.BlockSpec((1, D), lambda i, j: (0, 0)),
                  pl.BlockSpec((None, tm, Fs), lambda i, j: (j, i, 0)),
                  pl.BlockSpec((None, tm, Fs), lambda i, j: (j, i, 0)),
                  pl.BlockSpec((None, D, Fs), lambda i, j: (j, 0, 0)),
                  pl.BlockSpec((None, D, Fs), lambda i, j: (j, 0, 0)),
                  pl.BlockSpec((None, Fs, D), lambda i, j: (j, 0, 0)),
                  pl.BlockSpec((tm, D), lambda i, j: (i, 0))],
        out_specs=[pl.BlockSpec((tm, D), lambda i, j: (i, 0)),
                   pl.BlockSpec((None, tm, Fs), lambda i, j: (j, i, 0)),
                   pl.BlockSpec((None, tm, Fs), lambda i, j: (j, i, 0)),
                   pl.BlockSpec((None, tm, Fs), lambda i, j: (j, i, 0)),
                   pl.BlockSpec((1, D), lambda i, j: (0, 0))],
        out_shape=[jax.ShapeDtypeStruct((T, D), F32)] + [jax.ShapeDtypeStruct((NP, T, Fs), BF16)] * 3
                  + [jax.ShapeDtypeStruct((1, D), F32)],
        scratch_shapes=[pltpu.VMEM((tm, D), F32), pltpu.VMEM((tm, D), BF16)],
        operands=(x, nw, g, u, wg, wu, wd, dout))


def ffn_bwd_w(h, a, dg, du, dout, name):
    T, D = h.shape
    NP, _, Fs = a.shape
    tk = min(TK_TN, T)

    def body(h_ref, a_ref, dg_ref, du_ref, do_ref, dwg_ref, dwu_ref, dwd_ref):
        t = pl.program_id(1)

        @pl.when(t == 0)
        def _():
            dwg_ref[...] = jnp.zeros_like(dwg_ref)
            dwu_ref[...] = jnp.zeros_like(dwu_ref)
            dwd_ref[...] = jnp.zeros_like(dwd_ref)

        hb = h_ref[...]
        dwg_ref[...] += _dot_tn(hb, dg_ref[...])
        dwu_ref[...] += _dot_tn(hb, du_ref[...])
        dwd_ref[...] += _dot_tn(a_ref[...], (0.5 * do_ref[...]).astype(BF16))

    piece = pl.BlockSpec((None, tk, Fs), lambda j, t: (j, t, 0))
    return pl.pallas_call(
        body, name=name, grid=(NP, T // tk),
        in_specs=[pl.BlockSpec((tk, D), lambda j, t: (t, 0)), piece, piece, piece,
                  pl.BlockSpec((tk, D), lambda j, t: (t, 0))],
        out_specs=[pl.BlockSpec((None, D, Fs), lambda j, t: (j, 0, 0)),
                   pl.BlockSpec((None, D, Fs), lambda j, t: (j, 0, 0)),
                   pl.BlockSpec((None, Fs, D), lambda j, t: (j, 0, 0))],
        out_shape=[jax.ShapeDtypeStruct((NP, D, Fs), F32), jax.ShapeDtypeStruct((NP, D, Fs), F32),
                   jax.ShapeDtypeStruct((NP, Fs, D), F32)],
        compiler_params=_params(("parallel", "arbitrary")),
    )(h, a, dg, du, dout)


def mix_in_fwd(x, nw, w_glu, w_qkv, w_gate, cs, sn, name):
    T, D = x.shape
    KV = (w_qkv.shape[1] - D) // 2
    tm = min(TM_ROW, T)

    def body(x_ref, nw_ref, wa_ref, wq_ref, wg_ref, cs_ref, sn_ref, h_ref, pa_ref, pg_ref, q_ref, k_ref, v_ref):
        xv = x_ref[...]
        h = (xv * _rms_scale(xv) * nw_ref[...]).astype(BF16)
        h_ref[...] = h
        pa_ref[...] = _dot(h, wa_ref[...])
        pg_ref[...] = _dot(h, wg_ref[...])
        qkv = _dot(h, wq_ref[...])
        cs_v, sn_v = cs_ref[...], sn_ref[...]
        q_ref[...] = _rope_chunks(qkv[:, :D], cs_v, sn_v, 1.0).astype(BF16)
        k_ref[...] = _rope_chunks(qkv[:, D:D + KV], cs_v, sn_v, 1.0).astype(BF16)
        v_ref[...] = qkv[:, D + KV:].astype(BF16)

    rows = lambda w: pl.BlockSpec((tm, w), lambda i: (i, 0))
    whole = lambda a: pl.BlockSpec(a.shape, lambda i: (0, 0))
    return pl.pallas_call(
        body, name=name, grid=(T // tm,),
        in_specs=[rows(D), whole(nw), whole(w_glu), whole(w_qkv), whole(w_gate), rows(128), rows(128)],
        out_specs=[rows(D), rows(2 * D), rows(2 * D), rows(D), rows(KV), rows(KV)],
        out_shape=[jax.ShapeDtypeStruct((T, D), BF16), jax.ShapeDtypeStruct((T, 2 * D), F32),
                   jax.ShapeDtypeStruct((T, 2 * D), F32), jax.ShapeDtypeStruct((T, D), BF16),
                   jax.ShapeDtypeStruct((T, KV), BF16), jax.ShapeDtypeStruct((T, KV), BF16)],
        compiler_params=_params(("parallel",)),
    )(x, nw, w_glu, w_qkv, w_gate, cs, sn)


def matmul_tn(lhs, rhs, name):
    T, K = lhs.shape
    N = rhs.shape[1]
    tk = min(TK_TN, T)

    def body(l_ref, r_ref, o_ref):
        @pl.when(pl.program_id(0) == 0)
        def _():
            o_ref[...] = jnp.zeros_like(o_ref)

        o_ref[...] += _dot_tn(l_ref[...].astype(BF16), r_ref[...].astype(BF16))

    return pl.pallas_call(
        body, name=name, grid=(T // tk,),
        in_specs=[pl.BlockSpec((tk, K), lambda t: (t, 0)), pl.BlockSpec((tk, N), lambda t: (t, 0))],
        out_specs=pl.BlockSpec((K, N), lambda t: (0, 0)),
        out_shape=jax.ShapeDtypeStruct((K, N), F32),
        compiler_params=_params(("arbitrary",)),
    )(lhs, rhs)


def mix_in_bwd(dps, ws, x, nw, dres, name):
    T, D = x.shape
    tm = min(TM_ROW, T)
    n = len(dps)

    def body(*refs):
        dp_refs, w_refs = refs[:n], refs[n:2 * n]
        x_ref, nw_ref, dr_ref, dx_ref, dnw_ref = refs[2 * n:]

        @pl.when(pl.program_id(0) == 0)
        def _():
            dnw_ref[...] = jnp.zeros_like(dnw_ref)

        dh = _dot_nt(dp_refs[0][...], w_refs[0][...])
        for k in range(1, n):
            dh += _dot_nt(dp_refs[k][...], w_refs[k][...])
        dxn, dnw = _rms_bwd(x_ref[...], nw_ref[...], dh)
        dx_ref[...] = dr_ref[...] + dxn
        dnw_ref[...] += dnw

    in_specs = [pl.BlockSpec((tm, dp.shape[1]), lambda i: (i, 0)) for dp in dps]
    in_specs += [pl.BlockSpec(w.shape, lambda i: (0, 0)) for w in ws]
    in_specs += [pl.BlockSpec((tm, D), lambda i: (i, 0)), pl.BlockSpec((1, D), lambda i: (0, 0)),
                 pl.BlockSpec((tm, D), lambda i: (i, 0))]
    return pl.pallas_call(
        body, name=name, grid=(T // tm,), in_specs=in_specs,
        out_specs=[pl.BlockSpec((tm, D), lambda i: (i, 0)), pl.BlockSpec((1, D), lambda i: (0, 0))],
        out_shape=[jax.ShapeDtypeStruct((T, D), F32), jax.ShapeDtypeStruct((1, D), F32)],
        compiler_params=_params(("arbitrary",)),
    )(*dps, *ws, x, nw, dres)


def _layernorm_stats(c1):
    mu = jnp.mean(c1, axis=-1, keepdims=True)
    xc = c1 - mu
    rstd = lax.rsqrt(jnp.mean(xc * xc, axis=-1, keepdims=True) + LN_EPS)
    return xc * rstd, rstd


def _shifted_copies(src_ref, dst_ref):
    rows = dst_ref.shape[1]
    for b in range(1, 8):
        dst_ref[b - 1] = src_ref[pl.ds(b, rows), :]


def _shifted_rows(src_ref, shifted_ref, start, rows, cols):
    a8, b = divmod(start, 8)
    if b == 0:
        return src_ref[pl.ds(8 * a8, rows), cols]
    return shifted_ref[b - 1, pl.ds(8 * a8, rows), cols]


def conv_fwd(p_glu, dw_w, dw_b, ln_g, ln_b, name):
    T, D2 = p_glu.shape
    D = D2 // 2
    tm = min(TM_ROW, T)
    hb = tm // CONV_HALO

    def body(a_ref, b_ref, ah_ref, bh_ref, w_ref, wb_ref, g_ref, be_ref, c1_ref, c3_ref, e_ref, es_ref):
        i = pl.program_id(0)
        halo = ah_ref[...] * _sigmoid(bh_ref[...])
        e_ref[pl.ds(0, CONV_HALO), :] = jnp.where(i > 0, halo, 0.0)
        e_ref[pl.ds(CONV_HALO, tm), :] = a_ref[...] * _sigmoid(b_ref[...])
        _shifted_copies(e_ref, es_ref)
        off = CONV_HALO - (CONV_WIDTH - 1)

        def strip(s, carry):
            cols = pl.ds(pl.multiple_of(s * 128, 128), 128)
            acc = jnp.zeros((tm, 128), F32) + wb_ref[:, cols]
            for k in range(CONV_WIDTH):
                acc += w_ref[pl.ds(k, 1), cols] * _shifted_rows(e_ref, es_ref, off + k, tm, cols)
            c1_ref[:, cols] = acc
            return carry

        lax.fori_loop(0, D // 128, strip, 0)
        xhat, _ = _layernorm_stats(c1_ref[...])
        c2 = xhat * g_ref[...] + be_ref[...]
        c3_ref[...] = (c2 * _sigmoid(c2)).astype(BF16)

    row = pl.BlockSpec((1, D), lambda i: (0, 0))
    return pl.pallas_call(
        body, name=name, grid=(T // tm,),
        in_specs=[pl.BlockSpec((tm, D), lambda i: (i, 0)), pl.BlockSpec((tm, D), lambda i: (i, 1)),
                  pl.BlockSpec((CONV_HALO, D), lambda i: (jnp.maximum(i * hb - 1, 0), 0)),
                  pl.BlockSpec((CONV_HALO, D), lambda i: (jnp.maximum(i * hb - 1, 0), 1)),
                  pl.BlockSpec((CONV_HALO, D), lambda i: (0, 0)), row, row, row],
        out_specs=[pl.BlockSpec((tm, D), lambda i: (i, 0)), pl.BlockSpec((tm, D), lambda i: (i, 0))],
        out_shape=[jax.ShapeDtypeStruct((T, D), F32), jax.ShapeDtypeStruct((T, D), BF16)],
        scratch_shapes=[pltpu.VMEM((tm + CONV_HALO, D), F32), pltpu.VMEM((7, tm + CONV_HALO - 8, D), F32)],
        compiler_params=_params(("parallel",)),
    )(p_glu, p_glu, p_glu, p_glu, dw_w, dw_b, ln_g, ln_b)


def conv_bwd(p_glu, dc1, dw_w, name):
    T, D2 = p_glu.shape
    D = D2 // 2
    tm = min(TM_ROW, T)
    hb = tm // CONV_HALO
    last = T // CONV_HALO - 1
    nblk = T // tm

    def body(a_ref, b_ref, ah_ref, bh_ref, d_ref, dn_ref, w_ref, dp_ref, dw_ref, e_ref, f_ref, es_ref, fs_ref):
        i = pl.program_id(0)

        @pl.when(i == 0)
        def _():
            dw_ref[...] = jnp.zeros_like(dw_ref)

        halo = ah_ref[...] * _sigmoid(bh_ref[...])
        e_ref[pl.ds(0, CONV_HALO), :] = jnp.where(i > 0, halo, 0.0)
        e_ref[pl.ds(CONV_HALO, tm), :] = a_ref[...] * _sigmoid(b_ref[...])
        f_ref[pl.ds(0, tm), :] = d_ref[...]
        f_ref[pl.ds(tm, CONV_HALO), :] = jnp.where(i < nblk - 1, dn_ref[...], 0.0)
        _shifted_copies(e_ref, es_ref)
        _shifted_copies(f_ref, fs_ref)
        off = CONV_HALO - (CONV_WIDTH - 1)

        def strip(s, carry):
            cols = pl.ds(pl.multiple_of(s * 128, 128), 128)
            d = d_ref[:, cols]
            dc0 = jnp.zeros((tm, 128), F32)
            for k in range(CONV_WIDTH):
                dw_ref[pl.ds(k, 1), cols] += jnp.sum(d * _shifted_rows(e_ref, es_ref, off + k, tm, cols),
                                                     axis=0, keepdims=True)
                dc0 += w_ref[pl.ds(k, 1), cols] * _shifted_rows(f_ref, fs_ref, CONV_WIDTH - 1 - k, tm, cols)
            a = a_ref[:, cols]
            sb = _sigmoid(b_ref[:, cols])
            dp_ref[:, cols] = (dc0 * sb).astype(BF16)
            dp_ref[:, pl.ds(pl.multiple_of(D + s * 128, 128), 128)] = (dc0 * a * sb * (1.0 - sb)).astype(BF16)
            return carry

        lax.fori_loop(0, D // 128, strip, 0)

    return pl.pallas_call(
        body, name=name, grid=(nblk,),
        in_specs=[pl.BlockSpec((tm, D), lambda i: (i, 0)), pl.BlockSpec((tm, D), lambda i: (i, 1)),
                  pl.BlockSpec((CONV_HALO, D), lambda i: (jnp.maximum(i * hb - 1, 0), 0)),
                  pl.BlockSpec((CONV_HALO, D), lambda i: (jnp.maximum(i * hb - 1, 0), 1)),
                  pl.BlockSpec((tm, D), lambda i: (i, 0)),
                  pl.BlockSpec((CONV_HALO, D), lambda i: (jnp.minimum((i + 1) * hb, last), 0)),
                  pl.BlockSpec((CONV_HALO, D), lambda i: (0, 0))],
        out_specs=[pl.BlockSpec((tm, D2), lambda i: (i, 0)), pl.BlockSpec((CONV_HALO, D), lambda i: (0, 0))],
        out_shape=[jax.ShapeDtypeStruct((T, D2), BF16), jax.ShapeDtypeStruct((CONV_HALO, D), F32)],
        scratch_shapes=[pltpu.VMEM((tm + CONV_HALO, D), F32), pltpu.VMEM((tm + CONV_HALO, D), F32),
                        pltpu.VMEM((7, tm + CONV_HALO - 8, D), F32), pltpu.VMEM((7, tm + CONV_HALO - 8, D), F32)],
        compiler_params=_params(("arbitrary",)),
    )(p_glu, p_glu, p_glu, p_glu, dc1, dc1, dw_w)


def _rot_half(x):
    lane = lax.broadcasted_iota(jnp.int32, x.shape, 1)
    first = (lane % HEAD_DIM) < HEAD_DIM // 2
    return jnp.where(first, pltpu.roll(x, 128 - HEAD_DIM // 2, 1), pltpu.roll(x, HEAD_DIM // 2, 1))


def _rope_chunks(x, cs, sn, sign):
    outs = []
    for c in range(x.shape[1] // 128):
        xc = x[:, c * 128:(c + 1) * 128]
        outs.append(xc * cs + sign * (_rot_half(xc) * sn))
    return outs[0] if len(outs) == 1 else jnp.concatenate(outs, axis=1)


def rope_bwd(dq, dk, dv, cs, sn, name):
    T, D = dq.shape
    KV = dk.shape[1]
    tm = min(TM_ROW, T)

    def body(dq_ref, dk_ref, dv_ref, cs_ref, sn_ref, o_ref):
        cs_v, sn_v = cs_ref[...], sn_ref[...]
        o_ref[:, pl.ds(0, D)] = _rope_chunks(dq_ref[...], cs_v, sn_v, -1.0).astype(BF16)
        o_ref[:, pl.ds(D, KV)] = _rope_chunks(dk_ref[...], cs_v, sn_v, -1.0).astype(BF16)
        o_ref[:, pl.ds(D + KV, KV)] = dv_ref[...].astype(BF16)

    tab = pl.BlockSpec((tm, 128), lambda i: (i, 0))
    return pl.pallas_call(
        body, name=name, grid=(T // tm,),
        in_specs=[pl.BlockSpec((tm, D), lambda i: (i, 0)), pl.BlockSpec((tm, KV), lambda i: (i, 0)),
                  pl.BlockSpec((tm, KV), lambda i: (i, 0)), tab, tab],
        out_specs=pl.BlockSpec((tm, D + 2 * KV), lambda i: (i, 0)),
        out_shape=jax.ShapeDtypeStruct((T, D + 2 * KV), BF16),
        compiler_params=_params(("parallel",)),
    )(dq, dk, dv, cs, sn)


def _lane_lo():
    return lax.broadcasted_iota(jnp.int32, (1, 128), 1) < HEAD_DIM


def _band_mask(i, reps):
    shape = (reps * WINDOW, 2 * WINDOW)
    qi = lax.broadcasted_iota(jnp.int32, shape, 0) % WINDOW
    cj = lax.broadcasted_iota(jnp.int32, shape, 1)
    rel = qi - cj + WINDOW
    return (rel >= 0) & (rel < WINDOW) & ((i > 0) | (cj >= WINDOW))


def _stack_pairs(ref, first, n):
    parts = [ref[:, pl.ds((first + p) * 128, 128)] for p in range(n)]
    return parts[0] if n == 1 else jnp.concatenate(parts, axis=0)


def _pair_rows(n):
    return lax.broadcasted_iota(jnp.int32, (n * WINDOW, 1), 0) // WINDOW


def _per_pair_column(values, n):
    rows = _pair_rows(n)
    col = jnp.zeros((n * WINDOW, 1), F32) + values[0]
    for p in range(1, n):
        col = jnp.where(rows == p, values[p], col)
    return col


def _kv_lo_hi(x2, g):
    pair, half = divmod(g, 2)
    lo = _lane_lo()
    xg = x2[:, pair * 128:(pair + 1) * 128].astype(F32)
    xg = jnp.where(lo if half == 0 else ~lo, xg, 0.0)
    sw = pltpu.roll(xg, HEAD_DIM, 1)
    x_lo, x_hi = (xg, sw) if half == 0 else (sw, xg)
    return x_lo.astype(BF16), x_hi.astype(BF16)


def _softmax_sink(s, allowed, sink):
    s = jnp.where(allowed, s * (HEAD_DIM ** -0.5), NEG_INF)
    m = jnp.maximum(jnp.max(s, axis=-1, keepdims=True), sink)
    p = jnp.exp(s - m)
    es = jnp.exp(sink - m)
    inv = 1.0 / (jnp.sum(p, axis=-1, keepdims=True) + es)
    return p * inv, es * inv


def attn_fwd(qr, kr, vb, sinks, name):
    T, D = qr.shape
    KV = kr.shape[1]
    n_kv = KV // HEAD_DIM
    group = (D // HEAD_DIM) // n_kv
    nb = T // WINDOW

    npair = group // 2

    def body(sink_ref, q_ref, kp_ref, kc_ref, vp_ref, vc_ref, o_ref):
        i = pl.program_id(0)
        allowed = _band_mask(i, npair)
        k2 = jnp.concatenate([kp_ref[...], kc_ref[...]], axis=0)
        v2 = jnp.concatenate([vp_ref[...], vc_ref[...]], axis=0)
        outs = [None] * (D // 128)
        for g in range(n_kv):
            k_lo, k_hi = _kv_lo_hi(k2, g)
            v_lo, v_hi = _kv_lo_hi(v2, g)
            first = (g * group) // 2
            q = _stack_pairs(q_ref, first, npair)
            sink_e = _per_pair_column([sink_ref[0, g * group + 2 * p] for p in range(npair)], npair)
            sink_o = _per_pair_column([sink_ref[0, g * group + 2 * p + 1] for p in range(npair)], npair)
            pe, _ = _softmax_sink(_dot_nt(q, k_lo), allowed, sink_e)
            po, _ = _softmax_sink(_dot_nt(q, k_hi), allowed, sink_o)
            o = _dot(pe.astype(BF16), v_lo) + _dot(po.astype(BF16), v_hi)
            for p in range(npair):
                outs[first + p] = o[p * WINDOW:(p + 1) * WINDOW]
        o_ref[...] = jnp.concatenate(outs, axis=1).astype(BF16)

    prev = lambda i: (jnp.maximum(i - 1, 0), 0)
    cur = lambda i: (i, 0)
    return pl.pallas_call(
        body, name=name, grid=(nb,),
        in_specs=[pl.BlockSpec(memory_space=pltpu.SMEM),
                  pl.BlockSpec((WINDOW, D), cur),
                  pl.BlockSpec((WINDOW, KV), prev), pl.BlockSpec((WINDOW, KV), cur),
                  pl.BlockSpec((WINDOW, KV), prev), pl.BlockSpec((WINDOW, KV), cur)],
        out_specs=pl.BlockSpec((WINDOW, D), cur),
        out_shape=jax.ShapeDtypeStruct((T, D), BF16),
        compiler_params=_params(("parallel",)),
    )(sinks, qr, kr, kr, vb, vb)


def attn_bwd(qr, kr, vb, o, do, sinks, name):
    T, D = qr.shape
    KV = kr.shape[1]
    n_heads = D // HEAD_DIM
    n_kv = KV // HEAD_DIM
    group = n_heads // n_kv
    nb = T // WINDOW
    npair = group // 2
    scale = HEAD_DIM ** -0.5

    def body(sink_ref, q_ref, kp_ref, kc_ref, vp_ref, vc_ref, o_ref, do_ref,
             dq_ref, dk_ref, dv_ref, ds_ref, ck_ref, cv_ref):
        i = pl.program_id(0)
        lo = _lane_lo()

        @pl.when(i == 0)
        def _():
            ck_ref[...] = jnp.zeros_like(ck_ref)
            cv_ref[...] = jnp.zeros_like(cv_ref)
            ds_ref[...] = jnp.zeros_like(ds_ref)

        @pl.when(i < nb)
        def _():
            allowed = _band_mask(i, npair)
            rows = _pair_rows(npair)
            k2 = jnp.concatenate([kp_ref[...], kc_ref[...]], axis=0)
            v2 = jnp.concatenate([vp_ref[...], vc_ref[...]], axis=0)
            lane = lax.broadcasted_iota(jnp.int32, (1, 128), 1)
            dsink = jnp.zeros((1, 128), F32)
            dq_out = [None] * (D // 128)
            dk_pairs = [jnp.zeros((2 * WINDOW, 128), F32) for _ in range(KV // 128)]
            dv_pairs = [jnp.zeros((2 * WINDOW, 128), F32) for _ in range(KV // 128)]
            for g in range(n_kv):
                k_lo, k_hi = _kv_lo_hi(k2, g)
                v_lo, v_hi = _kv_lo_hi(v2, g)
                first = (g * group) // 2
                q = _stack_pairs(q_ref, first, npair)
                dop = _stack_pairs(do_ref, first, npair)
                dd = dop.astype(F32) * _stack_pairs(o_ref, first, npair).astype(F32)
                dq = jnp.zeros((npair * WINDOW, 128), F32)
                dkg = jnp.zeros((2 * WINDOW, 128), F32)
                dvg = jnp.zeros((2 * WINDOW, 128), F32)
                for parity, k_h, v_h, sel in ((0, k_lo, v_lo, lo), (1, k_hi, v_hi, ~lo)):
                    heads = [g * group + 2 * p + parity for p in range(npair)]
                    sink = _per_pair_column([sink_ref[0, h] for h in heads], npair)
                    p_, ps = _softmax_sink(_dot_nt(q, k_h), allowed, sink)
                    delta = jnp.sum(jnp.where(sel, dd, 0.0), axis=-1, keepdims=True)
                    dsc = (p_ * (_dot_nt(dop, v_h) - delta)).astype(BF16)
                    sd = -ps * delta
                    for p, h in enumerate(heads):
                        dsink += jnp.where(lane == h, jnp.sum(jnp.where(rows == p, sd, 0.0)), 0.0)
                    dq += _dot(dsc, k_h)
                    dkg += jnp.where(sel, _dot_tn(dsc, q), 0.0)
                    dvg += jnp.where(sel, _dot_tn(p_.astype(BF16), dop), 0.0)
                for p in range(npair):
                    dq_out[first + p] = dq[p * WINDOW:(p + 1) * WINDOW]
                pair, half = divmod(g, 2)
                keep = lo if half == 0 else ~lo
                dk_pairs[pair] += jnp.where(keep, dkg + pltpu.roll(dkg, HEAD_DIM, 1), 0.0) * scale
                dv_pairs[pair] += jnp.where(keep, dvg + pltpu.roll(dvg, HEAD_DIM, 1), 0.0)
            dq_ref[...] = jnp.concatenate(dq_out, axis=1) * scale
            dk2 = dk_pairs[0] if len(dk_pairs) == 1 else jnp.concatenate(dk_pairs, axis=1)
            dv2 = dv_pairs[0] if len(dv_pairs) == 1 else jnp.concatenate(dv_pairs, axis=1)
            dk_ref[...] = ck_ref[...] + dk2[:WINDOW]
            dv_ref[...] = cv_ref[...] + dv2[:WINDOW]
            ck_ref[...] = dk2[WINDOW:]
            cv_ref[...] = dv2[WINDOW:]
            ds_ref[pl.ds(0, 1), :] += dsink

        @pl.when(i == nb)
        def _():
            dk_ref[...] = ck_ref[...]
            dv_ref[...] = cv_ref[...]

    prev = lambda i: (jnp.maximum(i - 1, 0), 0)
    cur = lambda i: (jnp.minimum(i, nb - 1), 0)
    prevc = lambda i: (jnp.maximum(jnp.minimum(i, nb - 1) - 1, 0), 0)
    return pl.pallas_call(
        body, name=name, grid=(nb + 1,),
        in_specs=[pl.BlockSpec(memory_space=pltpu.SMEM),
                  pl.BlockSpec((WINDOW, D), cur),
                  pl.BlockSpec((WINDOW, KV), prevc), pl.BlockSpec((WINDOW, KV), cur),
                  pl.BlockSpec((WINDOW, KV), prevc), pl.BlockSpec((WINDOW, KV), cur),
                  pl.BlockSpec((WINDOW, D), cur), pl.BlockSpec((WINDOW, D), cur)],
        out_specs=[pl.BlockSpec((WINDOW, D), cur), pl.BlockSpec((WINDOW, KV), prev),
                   pl.BlockSpec((WINDOW, KV), prev), pl.BlockSpec((8, 128), lambda i: (0, 0))],
        out_shape=[jax.ShapeDtypeStruct((T, D), F32), jax.ShapeDtypeStruct((T, KV), F32),
                   jax.ShapeDtypeStruct((T, KV), F32), jax.ShapeDtypeStruct((8, 128), F32)],
        scratch_shapes=[pltpu.VMEM((WINDOW, KV), F32), pltpu.VMEM((WINDOW, KV), F32)],
        compiler_params=_params(("arbitrary",)),
    )(sinks, qr, kr, kr, vb, vb, o, do)


def merge_fwd(x, c3, o, p_gate, gate_b, w_proj, w_o, w_out, name):
    T, D = x.shape
    tm = min(TM_ROW, T)

    def body(x_ref, c3_ref, o_ref, gc_ref, ga_ref, bc_ref, ba_ref, wp_ref, wo_ref, wout_ref,
             xo_ref, co_ref, ao_ref, mg_ref):
        conv_out = _dot(c3_ref[...], wp_ref[...])
        attn_out = _dot(o_ref[...], wo_ref[...])
        merged = (_sigmoid(gc_ref[...] + bc_ref[...]) * conv_out
                  + _sigmoid(ga_ref[...] + ba_ref[...]) * attn_out).astype(BF16)
        co_ref[...] = conv_out.astype(BF16)
        ao_ref[...] = attn_out.astype(BF16)
        mg_ref[...] = merged
        xo_ref[...] = x_ref[...] + _dot(merged, wout_ref[...])

    blk = lambda j: pl.BlockSpec((tm, D), lambda i: (i, j))
    row = lambda j: pl.BlockSpec((1, D), lambda i: (0, j))
    mat = pl.BlockSpec((D, D), lambda i: (0, 0))
    return pl.pallas_call(
        body, name=name, grid=(T // tm,),
        in_specs=[blk(0), blk(0), blk(0), blk(0), blk(1), row(0), row(1), mat, mat, mat],
        out_specs=[blk(0), blk(0), blk(0), blk(0)],
        out_shape=[jax.ShapeDtypeStruct((T, D), F32)] + [jax.ShapeDtypeStruct((T, D), BF16)] * 3,
        compiler_params=_params(("parallel",)),
    )(x, c3, o, p_gate, p_gate, gate_b, gate_b, w_proj, w_o, w_out)


def merge_bwd(dx, p_gate, gate_b, conv_out, attn_out, c1, ln_g, ln_b, w_proj, w_o, w_out, name):
    T, D = dx.shape
    tm = min(TM_ROW, T)

    def body(dx_ref, gc_ref, ga_ref, bc_ref, ba_ref, co_ref, ao_ref, c1_ref, g_ref, be_ref,
             wp_ref, wo_ref, wout_ref, dgt_ref, dco_ref, dao_ref, do_ref, dc1_ref, sm_ref):
        @pl.when(pl.program_id(0) == 0)
        def _():
            sm_ref[...] = jnp.zeros_like(sm_ref)

        dm = _dot_nt(dx_ref[...].astype(BF16), wout_ref[...])
        sc = _sigmoid(gc_ref[...] + bc_ref[...])
        sa = _sigmoid(ga_ref[...] + ba_ref[...])
        dco = (dm * sc).astype(BF16)
        dao = (dm * sa).astype(BF16)
        dgc = dm * co_ref[...].astype(F32) * sc * (1.0 - sc)
        dga = dm * ao_ref[...].astype(F32) * sa * (1.0 - sa)
        dgt_ref[:, pl.ds(0, D)] = dgc.astype(BF16)
        dgt_ref[:, pl.ds(D, D)] = dga.astype(BF16)
        dco_ref[...] = dco
        dao_ref[...] = dao
        do_ref[...] = _dot_nt(dao, wo_ref[...]).astype(BF16)
        dc3 = _dot_nt(dco, wp_ref[...])
        xhat, rstd = _layernorm_stats(c1_ref[...])
        c2 = xhat * g_ref[...] + be_ref[...]
        dc2 = dc3 * _silu_grad(c2, _sigmoid(c2))
        dxh = dc2 * g_ref[...]
        dc1 = rstd * (dxh - jnp.mean(dxh, axis=-1, keepdims=True)
                      - xhat * jnp.mean(dxh * xhat, axis=-1, keepdims=True))
        dc1_ref[...] = dc1
        colsum = lambda v: jnp.sum(v, axis=0, keepdims=True)
        for r, (left, right) in enumerate(((dgc, dga), (dc2 * xhat, dc2), (dc1, None))):
            sm_ref[pl.ds(r, 1), pl.ds(0, D)] += colsum(left)
            if right is not None:
                sm_ref[pl.ds(r, 1), pl.ds(D, D)] += colsum(right)

    blk = lambda j: pl.BlockSpec((tm, D), lambda i: (i, j))
    row = lambda j: pl.BlockSpec((1, D), lambda i: (0, j))
    mat = pl.BlockSpec((D, D), lambda i: (0, 0))
    return pl.pallas_call(
        body, name=name, grid=(T // tm,),
        in_specs=[blk(0), blk(0), blk(1), row(0), row(1), blk(0), blk(0), blk(0), row(0), row(0), mat, mat, mat],
        out_specs=[pl.BlockSpec((tm, 2 * D), lambda i: (i, 0)), blk(0), blk(0), blk(0), blk(0),
                   pl.BlockSpec((8, 2 * D), lambda i: (0, 0))],
        out_shape=[jax.ShapeDtypeStruct((T, 2 * D), BF16)] + [jax.ShapeDtypeStruct((T, D), BF16)] * 3
                  + [jax.ShapeDtypeStruct((T, D), F32), jax.ShapeDtypeStruct((8, 2 * D), F32)],
        compiler_params=_params(("arbitrary",)),
    )(dx, p_gate, p_gate, gate_b, gate_b, conv_out, attn_out, c1, ln_g, ln_b, w_proj, w_o, w_out)


def loss_head(x, nw, target, name):
    T, D = x.shape
    tm = min(TM_ROW, T)

    def body(x_ref, nw_ref, t_ref, dx_ref, sm_ref):
        @pl.when(pl.program_id(0) == 0)
        def _():
            sm_ref[...] = jnp.zeros_like(sm_ref)

        xv = x_ref[...]
        err = xv * _rms_scale(xv) * nw_ref[...] - t_ref[...]
        loss = 0.5 * jnp.sum(jnp.mean(err * err, axis=-1, keepdims=True))
        dxn, dnw = _rms_bwd(xv, nw_ref[...], err * (1.0 / D))
        dx_ref[...] = dxn
        sm_ref[pl.ds(0, 1), :] += dnw
        sm_ref[pl.ds(1, 1), :] += jnp.zeros((1, D), F32) + loss

    return pl.pallas_call(
        body, name=name, grid=(T // tm,),
        in_specs=[pl.BlockSpec((tm, D), lambda i: (i, 0)), pl.BlockSpec((1, D), lambda i: (0, 0)),
                  pl.BlockSpec((tm, D), lambda i: (i, 0))],
        out_specs=[pl.BlockSpec((tm, D), lambda i: (i, 0)), pl.BlockSpec((8, D), lambda i: (0, 0))],
        out_shape=[jax.ShapeDtypeStruct((T, D), F32), jax.ShapeDtypeStruct((8, D), F32)],
        compiler_params=_params(("arbitrary",)),
    )(x, nw, target)


def adamw(w, g, m, v, name):
    R, C = w.shape
    tr = _row_tile(R, TR_ELT)

    def body(w_ref, g_ref, m_ref, v_ref, d_ref, mo_ref, vo_ref):
        gv = g_ref[...]
        mn = ADAM_B1 * m_ref[...] + (1.0 - ADAM_B1) * gv
        vn = ADAM_B2 * v_ref[...] + (1.0 - ADAM_B2) * (gv * gv)
        m_hat = mn / (1.0 - ADAM_B1 ** ADAM_STEP)
        v_hat = vn / (1.0 - ADAM_B2 ** ADAM_STEP)
        d_ref[...] = -ADAM_LR * (m_hat / (jnp.sqrt(v_hat) + ADAM_EPS) + ADAM_WD * w_ref[...])
        mo_ref[...] = mn
        vo_ref[...] = vn

    spec = pl.BlockSpec((tr, C), lambda i: (i, 0))
    return pl.pallas_call(
        body, name=name, grid=(R // tr,), in_specs=[spec] * 4, out_specs=[spec] * 3,
        out_shape=[jax.ShapeDtypeStruct((R, C), F32)] * 3,
        compiler_params=_params(("parallel",)),
    )(w, g, m, v)


def _place():
    return lax.axis_index("x"), lax.axis_index("y"), lax.axis_index("c")


def place_shard(place, w, dtype, name):
    R, C = w.shape
    tr = _row_tile(R, TR_ELT)

    def body(pc_ref, w_ref, o_ref):
        o_ref[...] = w_ref[...].astype(dtype)

    return pl.pallas_call(
        body, name=name,
        grid_spec=pltpu.PrefetchScalarGridSpec(
            num_scalar_prefetch=1, grid=(R // tr,),
            in_specs=[pl.BlockSpec((tr, C), lambda r, pc: (r, 0))],
            out_specs=pl.BlockSpec((None, tr, C), lambda r, pc: (pc[0], r, 0))),
        out_shape=jax.ShapeDtypeStruct((N_CHIPS, R, C), dtype),
        compiler_params=_params(("arbitrary",)),
    )(place, w)


def gather_side(shards, small):
    n, ns = len(shards), len(small)

    def ici_copy(dst, sems, k, j, x, y, c, sending):
        px, py = x ^ (j >> 1), y ^ (j & 1)
        slot = 2 * x + y if sending else 2 * px + py
        half = dst[k].shape[1] // 2
        part = dst[k].at[slot, pl.ds(c * half, half)] if k < n else dst[k].at[slot]
        return pltpu.make_async_remote_copy(part, part, sems[0].at[3 * k + j - 1], sems[1].at[3 * k + j - 1],
                                            device_id=(px, py, c), device_id_type=MESH)

    def d2d_copy(dst, sems, k, j, x, y, c, sending):
        half = dst[k].shape[1] // 2
        part = dst[k].at[2 * (x ^ (j >> 1)) + (y ^ (j & 1)), pl.ds((c if sending else 1 - c) * half, half)]
        return pltpu.make_async_remote_copy(part, part, sems[2].at[3 * k + j - 1], sems[3].at[3 * k + j - 1],
                                            device_id=(x, y, 1 - c), device_id_type=MESH)

    def start(src, dst, sems):
        x, y, c = _place()
        for k in range(n + ns):
            for j in (1, 2, 3):
                ici_copy(dst, sems, k, j, x, y, c, True).start()

    def finish(src, dst, sems):
        x, y, c = _place()
        for k in range(n + ns):
            for j in (1, 2, 3):
                ici_copy(dst, sems, k, j, x, y, c, False).wait_recv()
                if k < n:
                    d2d_copy(dst, sems, k, j, x, y, c, True).start()
        for k in range(n):
            for j in (1, 2, 3):
                d2d_copy(dst, sems, k, j, x, y, c, False).wait_recv()
        for k in range(n + ns):
            for j in (1, 2, 3):
                ici_copy(dst, sems, k, j, x, y, c, True).wait_send()
                if k < n:
                    d2d_copy(dst, sems, k, j, x, y, c, True).wait_send()

    arrays = list(shards) + list(small)
    return dict(inputs=arrays, out_shapes=[jax.ShapeDtypeStruct(a.shape, a.dtype) for a in arrays],
                aliases={k: k for k in range(n + ns)},
                sems=[pltpu.SemaphoreType.DMA((3 * (n + ns),)), pltpu.SemaphoreType.DMA((3 * (n + ns),)),
                      pltpu.SemaphoreType.DMA((3 * n,)), pltpu.SemaphoreType.DMA((3 * n,))],
                start=start, finish=finish)


def run_side(side, name):
    n_in, n_out = len(side["inputs"]), len(side["out_shapes"])

    def body(*refs):
        src, dst, sems = refs[:n_in], refs[n_in:n_in + n_out], refs[n_in + n_out:]
        side["start"](src, dst, sems)
        side["finish"](src, dst, sems)

    return pl.pallas_call(
        body, name=name, in_specs=[HBM_SPEC] * n_in, out_specs=[HBM_SPEC] * n_out,
        out_shape=side["out_shapes"], input_output_aliases=side["aliases"], scratch_shapes=side["sems"],
    )(*side["inputs"])


def allreduce_small(block):
    R, C = block.shape

    def body(x_ref, out_ref, all_ref, send_sems, recv_sems, local_sem):
        x, y, c = _place()
        me, sibling = (x, y, c), (x, y, 1 - c)
        chips = [(1 - x, y), (x, 1 - y), (1 - x, 1 - y)]

        def slot(px, py, pc):
            return all_ref.at[4 * px + 2 * py + pc]

        def copy(k, block_of, to, src=None):
            return pltpu.make_async_remote_copy(
                src_ref=slot(*block_of) if src is None else src, dst_ref=slot(*block_of),
                send_sem=send_sems.at[k], recv_sem=recv_sems.at[k], device_id=to, device_id_type=MESH)

        mine = pltpu.make_async_copy(x_ref, slot(*me), local_sem)
        mine.start()
        first = [copy(0, me, sibling, src=x_ref)]
        first += [copy(1 + j, me, (*chip, c), src=x_ref) for j, chip in enumerate(chips)]
        for cp in first:
            cp.start()
        passed = [copy(4 + j, (*chip, c), sibling) for j, chip in enumerate(chips)]
        for j, chip in enumerate(chips):
            copy(1 + j, (*chip, c), me).wait_recv()
            passed[j].start()
        copy(0, sibling, me).wait_recv()
        for j, chip in enumerate(chips):
            copy(4 + j, (*chip, 1 - c), me).wait_recv()
        for cp in first + passed:
            cp.wait_send()
        mine.wait()
        total = all_ref[0]
        for d in range(1, N_DEV):
            total = total + all_ref[d]
        out_ref[...] = total

    return pl.pallas_call(
        body, name="allreduce_small",
        in_specs=[pl.BlockSpec(memory_space=pltpu.VMEM)], out_specs=pl.BlockSpec(memory_space=pltpu.VMEM),
        out_shape=jax.ShapeDtypeStruct((R, C), F32),
        scratch_shapes=[pltpu.VMEM((N_DEV, R, C), F32), pltpu.SemaphoreType.DMA((7,)),
                        pltpu.SemaphoreType.DMA((7,)), pltpu.SemaphoreType.DMA],
        compiler_params=pltpu.CompilerParams(vmem_limit_bytes=VMEM_LIMIT),
    )(block)


def rs_exchange_siblings(grads, name):
    n = len(grads)

    def body(*refs):
        src, dst = refs[:n], refs[n:2 * n]
        send_sems, recv_sems = refs[2 * n:]
        x, y, c = _place()
        copies = []
        for k in range(n):
            half = src[k].shape[1] // 2
            cp = pltpu.make_async_remote_copy(src[k].at[:, pl.ds((1 - c) * half, half)], dst[k],
                                              send_sems.at[k], recv_sems.at[k],
                                              device_id=(x, y, 1 - c), device_id_type=MESH)
            cp.start()
            copies.append(cp)
        for cp in copies:
            cp.wait()

    return pl.pallas_call(
        body, name=name,
        in_specs=[HBM_SPEC] * n, out_specs=[HBM_SPEC] * n,
        out_shape=[jax.ShapeDtypeStruct((N_CHIPS, g.shape[1] // 2, g.shape[2]), F32) for g in grads],
        scratch_shapes=[pltpu.SemaphoreType.DMA((n,)), pltpu.SemaphoreType.DMA((n,))],
    )(*grads)


def rs_chip_sum(place, grad, sib, name):
    NP, R, C = grad.shape
    half = R // 2
    tr = _row_tile(half, TR_ELT)
    nr = half // tr

    def body(pc_ref, g_ref, s_ref, wire_ref, own_ref):
        q = pl.program_id(1)
        total = g_ref[...] + s_ref[...]
        wire_ref[...] = total.astype(BF16)

        @pl.when(q == pc_ref[0])
        def _():
            own_ref[...] = total

    return pl.pallas_call(
        body, name=name,
        grid_spec=pltpu.PrefetchScalarGridSpec(
            num_scalar_prefetch=1, grid=(nr, NP),
            in_specs=[pl.BlockSpec((None, tr, C), lambda r, q, pc: (q, pc[1] * nr + r, 0)),
                      pl.BlockSpec((None, tr, C), lambda r, q, pc: (q, r, 0))],
            out_specs=[pl.BlockSpec((None, tr, C), lambda r, q, pc: (q, r, 0)),
                       pl.BlockSpec((tr, C), lambda r, q, pc: (r, 0))]),
        out_shape=[jax.ShapeDtypeStruct((NP, half, C), BF16), jax.ShapeDtypeStruct((half, C), F32)],
        compiler_params=_params(("arbitrary", "arbitrary")),
    )(place, grad, sib)


def exchange_chips_side(wires):
    n = len(wires)

    def copies(src, dst, sems):
        x, y, c = _place()
        for k in range(n):
            for j in (1, 2, 3):
                qx, qy = x ^ (j >> 1), y ^ (j & 1)
                yield pltpu.make_async_remote_copy(src[k].at[2 * qx + qy], dst[k].at[2 * x + y],
                                                   sems[0].at[3 * k + j - 1], sems[1].at[3 * k + j - 1],
                                                   device_id=(qx, qy, c), device_id_type=MESH)

    def start(src, dst, sems):
        for cp in copies(src, dst, sems):
            cp.start()

    def finish(src, dst, sems):
        for cp in copies(src, dst, sems):
            cp.wait()

    return dict(inputs=list(wires), out_shapes=[jax.ShapeDtypeStruct(w.shape, BF16) for w in wires], aliases={},
                sems=[pltpu.SemaphoreType.DMA((3 * n,)), pltpu.SemaphoreType.DMA((3 * n,))],
                start=start, finish=finish)


def rs_final_sum(place, own, got, name):
    NP, half, C = got.shape
    tr = _row_tile(half, TR_ELT)
    nr = half // tr

    def body(pc_ref, own_ref, g1_ref, g2_ref, g3_ref, out_ref):
        out_ref[...] = ((own_ref[...] + g1_ref[...].astype(F32)) + g2_ref[...].astype(F32)) + g3_ref[...].astype(F32)

    slot = lambda j: pl.BlockSpec((None, tr, C), lambda r, pc: (pc[0] ^ j, r, 0))
    return pl.pallas_call(
        body, name=name,
        grid_spec=pltpu.PrefetchScalarGridSpec(
            num_scalar_prefetch=1, grid=(nr,),
            in_specs=[pl.BlockSpec((tr, C), lambda r, pc: (r, 0)), slot(1), slot(2), slot(3)],
            out_specs=pl.BlockSpec((tr, C), lambda r, pc: (pc[1] * nr + r, 0))),
        out_shape=jax.ShapeDtypeStruct((2 * half, C), F32),
        compiler_params=_params(("arbitrary",)),
    )(place, own, got, got, got)


def rs_share_siblings(totals):
    n = len(totals)

    def body(*refs):
        dst = refs[n:2 * n]
        send_sems, recv_sems = refs[2 * n:]
        x, y, c = _place()
        copies = []
        for k in range(n):
            half = dst[k].shape[0] // 2
            rows = dst[k].at[pl.ds(c * half, half)]
            cp = pltpu.make_async_remote_copy(rows, rows, send_sems.at[k], recv_sems.at[k],
                                              device_id=(x, y, 1 - c), device_id_type=MESH)
            cp.start()
            copies.append(cp)
        for k, cp in enumerate(copies):
            cp.wait_send()
            half = dst[k].shape[0] // 2
            got = dst[k].at[pl.ds((1 - c) * half, half)]
            pltpu.make_async_remote_copy(got, got, send_sems.at[k], recv_sems.at[k],
                                         device_id=(x, y, c), device_id_type=MESH).wait_recv()

    return pl.pallas_call(
        body, name="rs_share_siblings",
        in_specs=[HBM_SPEC] * n, out_specs=[HBM_SPEC] * n,
        out_shape=[jax.ShapeDtypeStruct(t.shape, F32) for t in totals],
        input_output_aliases={k: k for k in range(n)},
        scratch_shapes=[pltpu.SemaphoreType.DMA((n,)), pltpu.SemaphoreType.DMA((n,))],
    )(*totals)


def rs_to_wires(place, grads, tag):
    sibs = rs_exchange_siblings(grads, f"rs_exchange_siblings_{tag}")
    wires, owns = [], []
    for k, (g, s) in enumerate(zip(grads, sibs)):
        w, o = rs_chip_sum(place, g, s, f"rs_chip_sum_{tag}{k}")
        wires.append(w)
        owns.append(o)
    return wires, owns


def rs_finish(place, owns, gots):
    totals = [rs_final_sum(place, o, g, f"rs_final_sum_{k}") for k, (o, g) in enumerate(zip(owns, gots))]
    return rs_share_siblings(totals)


def _rope_tables(positions):
    half = HEAD_DIM // 2
    inv_freq = ROPE_THETA ** (-jnp.arange(half, dtype=F32) / half)
    ang = positions.astype(F32)[:, None] * inv_freq
    cos, sin = jnp.cos(ang), jnp.sin(ang)
    return jnp.tile(cos, (1, 4)), jnp.concatenate([-sin, sin, -sin, sin], axis=1)


def _pieces_from_cols(full):
    D, W = full.shape
    return full.reshape(D, N_CHIPS, W // N_CHIPS).transpose(1, 0, 2)


def kernel(x, positions, ffn1_norm, ffn1_w_gate, ffn1_w_up, ffn1_w_down, mix_norm, w_in, conv_dw_w, conv_dw_b, conv_ln_g, conv_ln_b, conv_w_proj, attn_sinks, attn_w_o, gate_b, w_out, ffn2_norm, ffn2_w_gate, ffn2_w_up, ffn2_w_down, final_norm, loss_target, m_ffn1_norm, m_ffn1_w_gate, m_ffn1_w_up, m_ffn1_w_down, m_mix_norm, m_w_in, m_conv_dw_w, m_conv_dw_b, m_conv_ln_g, m_conv_ln_b, m_conv_w_proj, m_attn_sinks, m_attn_w_o, m_gate_b, m_w_out, m_ffn2_norm, m_ffn2_w_gate, m_ffn2_w_up, m_ffn2_w_down, m_final_norm, v_ffn1_norm, v_ffn1_w_gate, v_ffn1_w_up, v_ffn1_w_down, v_mix_norm, v_w_in, v_conv_dw_w, v_conv_dw_b, v_conv_ln_g, v_conv_ln_b, v_conv_w_proj, v_attn_sinks, v_attn_w_o, v_gate_b, v_w_out, v_ffn2_norm, v_ffn2_w_gate, v_ffn2_w_up, v_ffn2_w_down, v_final_norm):
    weights = dict(ffn1_norm=ffn1_norm, ffn1_w_gate=ffn1_w_gate, ffn1_w_up=ffn1_w_up, ffn1_w_down=ffn1_w_down,
                   mix_norm=mix_norm, w_in=w_in, conv_dw_w=conv_dw_w, conv_dw_b=conv_dw_b, conv_ln_g=conv_ln_g,
                   conv_ln_b=conv_ln_b, conv_w_proj=conv_w_proj, attn_sinks=attn_sinks, attn_w_o=attn_w_o,
                   gate_b=gate_b, w_out=w_out, ffn2_norm=ffn2_norm, ffn2_w_gate=ffn2_w_gate, ffn2_w_up=ffn2_w_up,
                   ffn2_w_down=ffn2_w_down, final_norm=final_norm)
    m_in = dict(ffn1_norm=m_ffn1_norm, ffn1_w_gate=m_ffn1_w_gate, ffn1_w_up=m_ffn1_w_up, ffn1_w_down=m_ffn1_w_down,
                mix_norm=m_mix_norm, w_in=m_w_in, conv_dw_w=m_conv_dw_w, conv_dw_b=m_conv_dw_b,
                conv_ln_g=m_conv_ln_g, conv_ln_b=m_conv_ln_b, conv_w_proj=m_conv_w_proj, attn_sinks=m_attn_sinks,
                attn_w_o=m_attn_w_o, gate_b=m_gate_b, w_out=m_w_out, ffn2_norm=m_ffn2_norm,
                ffn2_w_gate=m_ffn2_w_gate, ffn2_w_up=m_ffn2_w_up, ffn2_w_down=m_ffn2_w_down, final_norm=m_final_norm)
    v_in = dict(ffn1_norm=v_ffn1_norm, ffn1_w_gate=v_ffn1_w_gate, ffn1_w_up=v_ffn1_w_up, ffn1_w_down=v_ffn1_w_down,
                mix_norm=v_mix_norm, w_in=v_w_in, conv_dw_w=v_conv_dw_w, conv_dw_b=v_conv_dw_b,
                conv_ln_g=v_conv_ln_g, conv_ln_b=v_conv_ln_b, conv_w_proj=v_conv_w_proj, attn_sinks=v_attn_sinks,
                attn_w_o=v_attn_w_o, gate_b=v_gate_b, w_out=v_w_out, ffn2_norm=v_ffn2_norm,
                ffn2_w_gate=v_ffn2_w_gate, ffn2_w_up=v_ffn2_w_up, ffn2_w_down=v_ffn2_w_down, final_norm=v_final_norm)
    names = list(weights)
    big = ["ffn1_w_gate", "ffn1_w_up", "ffn1_w_down", "w_in", "conv_w_proj", "attn_w_o", "w_out",
           "ffn2_w_gate", "ffn2_w_up", "ffn2_w_down"]

    xs = x[0]
    T, D = xs.shape
    KV = (w_in.shape[2] * N_CHIPS - 5 * D) // 2
    n_heads = D // HEAD_DIM
    my_chip = 2 * lax.axis_index("x") + lax.axis_index("y")
    place = jnp.stack([my_chip, lax.axis_index("c")]).astype(jnp.int32)

    placed = {k: place_shard(place, weights[k][0], BF16, f"place_{k}") for k in big}
    placed_dw = place_shard(place, conv_dw_w[0], F32, "place_conv_dw_w")
    first, later = big[:3], big[3:]
    wg1, wu1, wd1 = run_side(gather_side([placed[k] for k in first], []), "gather_ffn1")
    x1, h1, g1, u1, *gathered = ffn_fwd(x[0], ffn1_norm, wg1, wu1, wd1, "ffn1_fwd",
                                        side=gather_side([placed[k] for k in later], [placed_dw]))
    full = dict(zip(later + ["conv_dw_w"], gathered))
    wg2, wu2, wd2 = full["ffn2_w_gate"], full["ffn2_w_up"], full["ffn2_w_down"]
    w_in_full = full["w_in"].transpose(1, 0, 2).reshape(D, -1)
    w_glu, w_qkv, w_gate = w_in_full[:, :2 * D], w_in_full[:, 2 * D:3 * D + 2 * KV], w_in_full[:, 3 * D + 2 * KV:]
    w_proj = full["conv_w_proj"].reshape(D, D)
    w_o = full["attn_w_o"].reshape(D, D)
    w_out_f = full["w_out"].reshape(D, D)
    dw_w = full["conv_dw_w"].transpose(1, 0, 2).reshape(CONV_WIDTH, D)
    dw_w = jnp.concatenate([dw_w, jnp.zeros((CONV_HALO - CONV_WIDTH, D), F32)], axis=0)
    cs, sn = _rope_tables(positions[0])
    fn_row = final_norm.reshape(1, D)

    h2, p_glu, p_gate, qr, kr, vb = mix_in_fwd(x1, mix_norm, w_glu, w_qkv, w_gate, cs, sn, "mix_in_fwd")
    c1, c3 = conv_fwd(p_glu, dw_w, conv_dw_b, conv_ln_g, conv_ln_b, "conv_fwd")
    o = attn_fwd(qr, kr, vb, attn_sinks, "attn_fwd")
    x2, conv_out, attn_out, merged = merge_fwd(x1, c3, o, p_gate, gate_b, w_proj, w_o, w_out_f, "merge_fwd")
    x3, h3, g2, u2 = ffn_fwd(x2, ffn2_norm, wg2, wu2, wd2, "ffn2_fwd")

    dx3, head_sums = loss_head(x3, fn_row, loss_target[0], "loss_head")
    dx2, dg2, du2, a2, d_ffn2_norm = ffn_bwd_x(x2, ffn2_norm, g2, u2, wg2, wu2, wd2, dx3, "ffn2_bwd_x")
    dwg2, dwu2, dwd2 = ffn_bwd_w(h3, a2, dg2, du2, dx3, "ffn2_bwd_w")
    d_gates, d_conv_out, d_attn_out, d_o, dc1, merge_sums = merge_bwd(
        dx2, p_gate, gate_b, conv_out, attn_out, c1, conv_ln_g, conv_ln_b, w_proj, w_o, w_out_f, "merge_bwd")
    d_w_out = matmul_tn(merged, dx2, "d_w_out")
    d_w_proj = matmul_tn(c3, d_conv_out, "d_conv_w_proj")
    d_w_o = matmul_tn(o, d_attn_out, "d_attn_w_o")
    d_glu, d_dw_w = conv_bwd(p_glu, dc1, dw_w, "conv_bwd")
    dq, dk, dv, d_sinks = attn_bwd(qr, kr, vb, o, d_o, attn_sinks, "attn_bwd")
    d_qkv = rope_bwd(dq, dk, dv, cs, sn, "rope_bwd")
    dx1, d_mix_norm = mix_in_bwd([d_glu, d_qkv, d_gates], [w_glu, w_qkv, w_gate], x1, mix_norm, dx2, "mix_in_bwd")
    d_w_in = jnp.concatenate([matmul_tn(h2, d_glu, "d_w_in_glu"), matmul_tn(h2, d_qkv, "d_w_in_qkv"),
                              matmul_tn(h2, d_gates, "d_w_in_gate")], axis=1)
    dwc = D // N_CHIPS
    early = [dwg2, dwu2, dwd2, _pieces_from_cols(d_w_in), d_w_proj.reshape(N_CHIPS, dwc, D),
             d_w_o.reshape(N_CHIPS, dwc, D), d_w_out.reshape(N_CHIPS, dwc, D)]
    wires_e, owns_e = rs_to_wires(place, early, "early")
    dx0, dg1, du1, a1, d_ffn1_norm, *gots_e = ffn_bwd_x(xs, ffn1_norm, g1, u1, wg1, wu1, wd1, dx1, "ffn1_bwd_x",
                                                        side=exchange_chips_side(wires_e))
    dwg1, dwu1, dwd1 = ffn_bwd_w(h1, a1, dg1, du1, dx1, "ffn1_bwd_w")
    wires_l, owns_l = rs_to_wires(place, [dwg1, dwu1, dwd1], "late")
    gots_l = run_side(exchange_chips_side(wires_l), "rs_exchange_chips_late")
    reduced = rs_finish(place, owns_e + owns_l, list(gots_e) + list(gots_l))

    pad_row = lambda v: jnp.pad(v, ((0, 0), (0, D - v.shape[1])))
    small_rows = jnp.concatenate([
        d_ffn1_norm, d_mix_norm, merge_sums[2:3, :D], merge_sums[1:2, :D], merge_sums[1:2, D:],
        pad_row(d_sinks[0:1, :n_heads]), merge_sums[0:1, :D], merge_sums[0:1, D:], d_ffn2_norm,
        head_sums[0:1], head_sums[1:2], jnp.zeros((5, D), F32), d_dw_w], axis=0)
    small = allreduce_small(small_rows)
    loss = small[10, 0]
    grads = {"ffn1_norm": small[0:1], "mix_norm": small[1:2], "conv_dw_b": small[2:3], "conv_ln_g": small[3:4],
             "conv_ln_b": small[4:5], "attn_sinks": small[5:6, :n_heads],
             "gate_b": jnp.concatenate([small[6:7], small[7:8]], axis=1), "ffn2_norm": small[8:9],
             "final_norm": small[9:10]}
    grads["conv_dw_w"] = lax.dynamic_slice(small[16:16 + CONV_WIDTH], (0, my_chip * dwc), (CONV_WIDTH, dwc))
    grads.update(zip(["ffn2_w_gate", "ffn2_w_up", "ffn2_w_down", "w_in", "conv_w_proj", "attn_w_o", "w_out",
                      "ffn1_w_gate", "ffn1_w_up", "ffn1_w_down"], reduced))

    deltas, new_m, new_v = {}, {}, {}
    for k in names:
        shape = weights[k].shape
        g2d = grads[k].reshape(-1, shape[-1])
        grads[k] = g2d.reshape(shape)
        d, mn, vn = adamw(weights[k].reshape(g2d.shape), g2d, m_in[k].reshape(g2d.shape),
                          v_in[k].reshape(g2d.shape), f"adamw_{k}")
        deltas[k], new_m[k], new_v[k] = d.reshape(shape), mn.reshape(shape), vn.reshape(shape)

    return (loss, dx0[None], *[grads[k] for k in names], *[deltas[k] for k in names],
            *[new_m[k] for k in names], *[new_v[k] for k in names])
```

```python
import functools

import jax
import jax.numpy as jnp
from jax import lax
from jax.experimental import pallas as pl
from jax.experimental.pallas import tpu as pltpu

F32 = jnp.float32
BF16 = jnp.bfloat16
MESH = pl.DeviceIdType.MESH

HEAD_DIM = 64
WINDOW = 128
CONV_WIDTH = 31
CONV_HALO = 32
ROPE_THETA = 10000.0
EPS = 1e-6
LN_EPS = 1e-5
NEG_INF = -1e30
N_CHIPS = 4
N_DEV = 8

ADAM_LR = 0.001
ADAM_B1 = 0.9
ADAM_B2 = 0.999
ADAM_EPS = 1e-08
ADAM_WD = 0.01
ADAM_STEP = 10

TM_FFN = 512
TM_FFN_FWD = 1024
TM_ROW = 256
TK_TN = 1024
TR_ELT = 256
VMEM_LIMIT = 56 * 1024 * 1024

NT_DIMS = (((1,), (1,)), ((), ()))
TN_DIMS = (((0,), (0,)), ((), ()))


def _row_tile(rows, cap):
    for t in range(min(cap, rows), 15, -1):
        if rows % t == 0 and t % 16 == 0:
            return t
    return rows


def _params(sem):
    return pltpu.CompilerParams(dimension_semantics=sem, vmem_limit_bytes=VMEM_LIMIT)


def _dot(a, b):
    return jnp.dot(a, b, preferred_element_type=F32)


def _dot_nt(a, b):
    return lax.dot_general(a, b, NT_DIMS, preferred_element_type=F32)


def _dot_tn(a, b):
    return lax.dot_general(a, b, TN_DIMS, preferred_element_type=F32)


def _split_rows(dot, a, b):
    m = a.shape[0] // 2
    return jnp.concatenate([dot(a[:m], b), dot(a[m:], b)], axis=0)


def _sigmoid(x):
    return jax.nn.sigmoid(x)


def _rms_scale(xv):
    return lax.rsqrt(jnp.mean(xv * xv, axis=-1, keepdims=True) + EPS)


def _rms_bwd(xv, nw, dh):
    r = _rms_scale(xv)
    dn = dh * nw
    dx = r * dn - xv * (r * r * r) * jnp.mean(dn * xv, axis=-1, keepdims=True)
    dnw = jnp.sum(dh * (xv * r), axis=0, keepdims=True)
    return dx, dnw


def _silu_grad(z, s):
    return s * (1.0 + z * (1.0 - s))


HBM_SPEC = pl.BlockSpec(memory_space=pl.ANY)


def _call_hosting(body, side, *, grid, in_specs, out_specs, out_shape, scratch_shapes, operands, name):
    params = _params(("arbitrary",) * len(grid))
    if side is None:
        return pl.pallas_call(body, name=name, grid=grid, in_specs=in_specs, out_specs=out_specs, out_shape=out_shape,
                              scratch_shapes=scratch_shapes, compiler_params=params)(*operands)
    n_in, n_out, n_scr = len(in_specs), len(out_shape), len(scratch_shapes)
    s_in, s_out = len(side["inputs"]), len(side["out_shapes"])

    def at_step(end):
        hit = pl.program_id(0) == (grid[0] - 1 if end else 0)
        for a in range(1, len(grid)):
            hit &= pl.program_id(a) == (grid[a] - 1 if end else 0)
        return hit

    def hosted(*refs):
        b = n_in + s_in
        c = b + n_out
        d = c + s_out
        e = d + n_scr
        src, dst, sems = refs[n_in:b], refs[c:d], refs[e:]

        @pl.when(at_step(False))
        def _():
            side["start"](src, dst, sems)

        body(*refs[:n_in], *refs[b:c], *refs[d:e])

        @pl.when(at_step(True))
        def _():
            side["finish"](src, dst, sems)

    return pl.pallas_call(
        hosted, name=name, grid=grid, in_specs=list(in_specs) + [HBM_SPEC] * s_in,
        out_specs=list(out_specs) + [HBM_SPEC] * s_out, out_shape=list(out_shape) + list(side["out_shapes"]),
        scratch_shapes=list(scratch_shapes) + list(side["sems"]),
        input_output_aliases={n_in + a: n_out + b for a, b in side["aliases"].items()},
        compiler_params=params)(*operands, *side["inputs"])


def ffn_fwd(x, nw, wg, wu, wd, name, side=None):
    T, D = x.shape
    NP, _, Fs = wg.shape
    tm = min(TM_FFN_FWD, T)

    def body(x_ref, nw_ref, wg_ref, wu_ref, wd_ref, xo_ref, h_ref, g_ref, u_ref, acc_ref):
        j = pl.program_id(1)

        @pl.when(j == 0)
        def _():
            xv = x_ref[...]
            h_ref[...] = (xv * _rms_scale(xv) * nw_ref[...]).astype(BF16)
            acc_ref[...] = jnp.zeros_like(acc_ref)

        h = h_ref[...]
        g = _dot(h, wg_ref[...])
        u = _dot(h, wu_ref[...])
        a = (g * _sigmoid(g)) * u
        g_ref[...] = g.astype(BF16)
        u_ref[...] = u.astype(BF16)
        acc_ref[...] += _dot(a.astype(BF16), wd_ref[...])

        @pl.when(j == NP - 1)
        def _():
            xo_ref[...] = x_ref[...] + 0.5 * acc_ref[...]

    return _call_hosting(
        body, side, name=name, grid=(T // tm, NP),
        in_specs=[pl.BlockSpec((tm, D), lambda i, j: (i, 0)),
                  pl.BlockSpec((1, D), lambda i, j: (0, 0)),
                  pl.BlockSpec((None, D, Fs), lambda i, j: (j, 0, 0)),
                  pl.BlockSpec((None, D, Fs), lambda i, j: (j, 0, 0)),
                  pl.BlockSpec((None, Fs, D), lambda i, j: (j, 0, 0))],
        out_specs=[pl.BlockSpec((tm, D), lambda i, j: (i, 0)),
                   pl.BlockSpec((tm, D), lambda i, j: (i, 0)),
                   pl.BlockSpec((None, tm, Fs), lambda i, j: (j, i, 0)),
                   pl.BlockSpec((None, tm, Fs), lambda i, j: (j, i, 0))],
        out_shape=[jax.ShapeDtypeStruct((T, D), F32), jax.ShapeDtypeStruct((T, D), BF16),
                   jax.ShapeDtypeStruct((NP, T, Fs), BF16), jax.ShapeDtypeStruct((NP, T, Fs), BF16)],
        scratch_shapes=[pltpu.VMEM((tm, D), F32)],
        operands=(x, nw, wg, wu, wd))


def ffn_bwd_x(x, nw, g, u, wg, wu, wd, dout, name, side=None):
    T, D = x.shape
    NP, _, Fs = wg.shape
    tm = min(TM_FFN, T)

    def body(x_ref, nw_ref, g_ref, u_ref, wg_ref, wu_ref, wd_ref, do_ref,
             dx_ref, dg_ref, du_ref, a_ref, dnw_ref, dh_ref, dob_ref):
        i = pl.program_id(0)
        j = pl.program_id(1)

        @pl.when((i == 0) & (j == 0))
        def _():
            dnw_ref[...] = jnp.zeros_like(dnw_ref)

        @pl.when(j == 0)
        def _():
            dh_ref[...] = jnp.zeros_like(dh_ref)
            dob_ref[...] = (0.5 * do_ref[...]).astype(BF16)

        da = _split_rows(_dot_nt, dob_ref[...], wd_ref[...])
        gf = g_ref[...].astype(F32)
        uf = u_ref[...].astype(F32)
        s = _sigmoid(gf)
        act = gf * s
        dg = (da * uf * _silu_grad(gf, s)).astype(BF16)
        du = (da * act).astype(BF16)
        dg_ref[...] = dg
        du_ref[...] = du
        a_ref[...] = (act * uf).astype(BF16)
        dh_ref[...] += _dot_nt(dg, wg_ref[...]) + _dot_nt(du, wu_ref[...])

        @pl.when(j == NP - 1)
        def _():
            dxn, dnw = _rms_bwd(x_ref[...], nw_ref[...], dh_ref[...])
            dx_ref[...] = do_ref[...] + dxn
            dnw_ref[...] += dnw

    return _call_hosting(
        body, side, name=name, grid=(T // tm, NP),
        in_specs=[pl.BlockSpec((tm, D), lambda i, j: (i, 0)),
                  pl.BlockSpec((1, D), lambda i, j: (0, 0)),
                  pl.BlockSpec((None, tm, Fs), lambda i, j: (j, i, 0)),
                  pl.BlockSpec((None, tm, Fs), lambda i, j: (j, i, 0)),
                  pl.BlockSpec((None, D, Fs), lambda i, j: (j, 0, 0)),
                  pl.BlockSpec((None, D, Fs), lambda i, j: (j, 0, 0)),
                  pl.BlockSpec((None, Fs, D), lambda i, j: (j, 0, 0)),
                  pl.BlockSpec((tm, D), lambda i, j: (i, 0))],
        out_specs=[pl.BlockSpec((tm, D), lambda i, j: (i, 0)),
                   pl.BlockSpec((None, tm, Fs), lambda i, j: (j, i, 0)),
                   pl.BlockSpec((None, tm, Fs), lambda i, j: (j, i, 0)),
                   pl.BlockSpec((None, tm, Fs), lambda i, j: (j, i, 0)),
                   pl.BlockSpec((1, D), lambda i, j: (0, 0))],
        out_shape=[jax.ShapeDtypeStruct((T, D), F32)] + [jax.ShapeDtypeStruct((NP, T, Fs), BF16)] * 3
                  + [jax.ShapeDtypeStruct((1, D), F32)],
        scratch_shapes=[pltpu.VMEM((tm, D), F32), pltpu.VMEM((tm, D), BF16)],
        operands=(x, nw, g, u, wg, wu, wd, dout))


def ffn_bwd_w(h, a, dg, du, dout, name):
    T, D = h.shape
    NP, _, Fs = a.shape
    tk = min(TK_TN, T)

    def body(h_ref, a_ref, dg_ref, du_ref, do_ref, dwg_ref, dwu_ref, dwd_ref):
        t = pl.program_id(1)

        @pl.when(t == 0)
        def _():
            dwg_ref[...] = jnp.zeros_like(dwg_ref)
            dwu_ref[...] = jnp.zeros_like(dwu_ref)
            dwd_ref[...] = jnp.zeros_like(dwd_ref)

        hb = h_ref[...]
        dwg_ref[...] += _dot_tn(hb, dg_ref[...])
        dwu_ref[...] += _dot_tn(hb, du_ref[...])
        dwd_ref[...] += _dot_tn(a_ref[...], (0.5 * do_ref[...]).astype(BF16))

    piece = pl.BlockSpec((None, tk, Fs), lambda j, t: (j, t, 0))
    return pl.pallas_call(
        body, name=name, grid=(NP, T // tk),
        in_specs=[pl.BlockSpec((tk, D), lambda j, t: (t, 0)), piece, piece, piece,
                  pl.BlockSpec((tk, D), lambda j, t: (t, 0))],
        out_specs=[pl.BlockSpec((None, D, Fs), lambda j, t: (j, 0, 0)),
                   pl.BlockSpec((None, D, Fs), lambda j, t: (j, 0, 0)),
                   pl.BlockSpec((None, Fs, D), lambda j, t: (j, 0, 0))],
        out_shape=[jax.ShapeDtypeStruct((NP, D, Fs), F32), jax.ShapeDtypeStruct((NP, D, Fs), F32),
                   jax.ShapeDtypeStruct((NP, Fs, D), F32)],
        compiler_params=_params(("parallel", "arbitrary")),
    )(h, a, dg, du, dout)


def mix_in_fwd(x, nw, w_glu, w_qkv, w_gate, cs, sn, name):
    T, D = x.shape
    KV = (w_qkv.shape[1] - D) // 2
    tm = min(TM_ROW, T)

    def body(x_ref, nw_ref, wa_ref, wq_ref, wg_ref, cs_ref, sn_ref, h_ref, pa_ref, pg_ref, q_ref, k_ref, v_ref):
        xv = x_ref[...]
        h = (xv * _rms_scale(xv) * nw_ref[...]).astype(BF16)
        h_ref[...] = h
        pa_ref[...] = _dot(h, wa_ref[...])
        pg_ref[...] = _dot(h, wg_ref[...])
        qkv = _dot(h, wq_ref[...])
        cs_v, sn_v = cs_ref[...], sn_ref[...]
        q_ref[...] = _rope_chunks(qkv[:, :D], cs_v, sn_v, 1.0).astype(BF16)
        k_ref[...] = _rope_chunks(qkv[:, D:D + KV], cs_v, sn_v, 1.0).astype(BF16)
        v_ref[...] = qkv[:, D + KV:].astype(BF16)

    rows = lambda w: pl.BlockSpec((tm, w), lambda i: (i, 0))
    whole = lambda a: pl.BlockSpec(a.shape, lambda i: (0, 0))
    return pl.pallas_call(
        body, name=name, grid=(T // tm,),
        in_specs=[rows(D), whole(nw), whole(w_glu), whole(w_qkv), whole(w_gate), rows(128), rows(128)],
        out_specs=[rows(D), rows(2 * D), rows(2 * D), rows(D), rows(KV), rows(KV)],
        out_shape=[jax.ShapeDtypeStruct((T, D), BF16), jax.ShapeDtypeStruct((T, 2 * D), F32),
                   jax.ShapeDtypeStruct((T, 2 * D), F32), jax.ShapeDtypeStruct((T, D), BF16),
                   jax.ShapeDtypeStruct((T, KV), BF16), jax.ShapeDtypeStruct((T, KV), BF16)],
        compiler_params=_params(("parallel",)),
    )(x, nw, w_glu, w_qkv, w_gate, cs, sn)


def matmul_tn(lhs, rhs, name):
    T, K = lhs.shape
    N = rhs.shape[1]
    tk = min(TK_TN, T)

    def body(l_ref, r_ref, o_ref):
        @pl.when(pl.program_id(0) == 0)
        def _():
            o_ref[...] = jnp.zeros_like(o_ref)

        o_ref[...] += _dot_tn(l_ref[...].astype(BF16), r_ref[...].astype(BF16))

    return pl.pallas_call(
        body, name=name, grid=(T // tk,),
        in_specs=[pl.BlockSpec((tk, K), lambda t: (t, 0)), pl.BlockSpec((tk, N), lambda t: (t, 0))],
        out_specs=pl.BlockSpec((K, N), lambda t: (0, 0)),
        out_shape=jax.ShapeDtypeStruct((K, N), F32),
        compiler_params=_params(("arbitrary",)),
    )(lhs, rhs)


def mix_in_bwd(dps, ws, x, nw, dres, name):
    T, D = x.shape
    tm = min(TM_ROW, T)
    n = len(dps)

    def body(*refs):
        dp_refs, w_refs = refs[:n], refs[n:2 * n]
        x_ref, nw_ref, dr_ref, dx_ref, dnw_ref = refs[2 * n:]

        @pl.when(pl.program_id(0) == 0)
        def _():
            dnw_ref[...] = jnp.zeros_like(dnw_ref)

        dh = _dot_nt(dp_refs[0][...], w_refs[0][...])
        for k in range(1, n):
            dh += _dot_nt(dp_refs[k][...], w_refs[k][...])
        dxn, dnw = _rms_bwd(x_ref[...], nw_ref[...], dh)
        dx_ref[...] = dr_ref[...] + dxn
        dnw_ref[...] += dnw

    in_specs = [pl.BlockSpec((tm, dp.shape[1]), lambda i: (i, 0)) for dp in dps]
    in_specs += [pl.BlockSpec(w.shape, lambda i: (0, 0)) for w in ws]
    in_specs += [pl.BlockSpec((tm, D), lambda i: (i, 0)), pl.BlockSpec((1, D), lambda i: (0, 0)),
                 pl.BlockSpec((tm, D), lambda i: (i, 0))]
    return pl.pallas_call(
        body, name=name, grid=(T // tm,), in_specs=in_specs,
        out_specs=[pl.BlockSpec((tm, D), lambda i: (i, 0)), pl.BlockSpec((1, D), lambda i: (0, 0))],
        out_shape=[jax.ShapeDtypeStruct((T, D), F32), jax.ShapeDtypeStruct((1, D), F32)],
        compiler_params=_params(("arbitrary",)),
    )(*dps, *ws, x, nw, dres)


def _layernorm_stats(c1):
    mu = jnp.mean(c1, axis=-1, keepdims=True)
    xc = c1 - mu
    rstd = lax.rsqrt(jnp.mean(xc * xc, axis=-1, keepdims=True) + LN_EPS)
    return xc * rstd, rstd


def _shifted_copies(src_ref, dst_ref):
    rows = dst_ref.shape[1]
    for b in range(1, 8):
        dst_ref[b - 1] = src_ref[pl.ds(b, rows), :]


def _shifted_rows(src_ref, shifted_ref, start, rows, cols):
    a8, b = divmod(start, 8)
    if b == 0:
        return src_ref[pl.ds(8 * a8, rows), cols]
    return shifted_ref[b - 1, pl.ds(8 * a8, rows), cols]


def conv_fwd(p_glu, dw_w, dw_b, ln_g, ln_b, name):
    T, D2 = p_glu.shape
    D = D2 // 2
    tm = min(TM_ROW, T)
    hb = tm // CONV_HALO

    def body(a_ref, b_ref, ah_ref, bh_ref, w_ref, wb_ref, g_ref, be_ref, c1_ref, c3_ref, e_ref, es_ref):
        i = pl.program_id(0)
        halo = ah_ref[...] * _sigmoid(bh_ref[...])
        e_ref[pl.ds(0, CONV_HALO), :] = jnp.where(i > 0, halo, 0.0)
        e_ref[pl.ds(CONV_HALO, tm), :] = a_ref[...] * _sigmoid(b_ref[...])
        _shifted_copies(e_ref, es_ref)
        off = CONV_HALO - (CONV_WIDTH - 1)

        def strip(s, carry):
            cols = pl.ds(pl.multiple_of(s * 128, 128), 128)
            acc = jnp.zeros((tm, 128), F32) + wb_ref[:, cols]
            for k in range(CONV_WIDTH):
                acc += w_ref[pl.ds(k, 1), cols] * _shifted_rows(e_ref, es_ref, off + k, tm, cols)
            c1_ref[:, cols] = acc
            return carry

        lax.fori_loop(0, D // 128, strip, 0)
        xhat, _ = _layernorm_stats(c1_ref[...])
        c2 = xhat * g_ref[...] + be_ref[...]
        c3_ref[...] = (c2 * _sigmoid(c2)).astype(BF16)

    row = pl.BlockSpec((1, D), lambda i: (0, 0))
    return pl.pallas_call(
        body, name=name, grid=(T // tm,),
        in_specs=[pl.BlockSpec((tm, D), lambda i: (i, 0)), pl.BlockSpec((tm, D), lambda i: (i, 1)),
                  pl.BlockSpec((CONV_HALO, D), lambda i: (jnp.maximum(i * hb - 1, 0), 0)),
                  pl.BlockSpec((CONV_HALO, D), lambda i: (jnp.maximum(i * hb - 1, 0), 1)),
                  pl.BlockSpec((CONV_HALO, D), lambda i: (0, 0)), row, row, row],
        out_specs=[pl.BlockSpec((tm, D), lambda i: (i, 0)), pl.BlockSpec((tm, D), lambda i: (i, 0))],
        out_shape=[jax.ShapeDtypeStruct((T, D), F32), jax.ShapeDtypeStruct((T, D), BF16)],
        scratch_shapes=[pltpu.VMEM((tm + CONV_HALO, D), F32), pltpu.VMEM((7, tm + CONV_HALO - 8, D), F32)],
        compiler_params=_params(("parallel",)),
    )(p_glu, p_glu, p_glu, p_glu, dw_w, dw_b, ln_g, ln_b)


def conv_bwd(p_glu, dc1, dw_w, name, side=None):
    T, D2 = p_glu.shape
    D = D2 // 2
    tm = min(TM_ROW, T)
    hb = tm // CONV_HALO
    last = T // CONV_HALO - 1
    nblk = T // tm

    def body(a_ref, b_ref, ah_ref, bh_ref, d_ref, dn_ref, w_ref, dp_ref, dw_ref, e_ref, f_ref, es_ref, fs_ref):
        i = pl.program_id(0)

        @pl.when(i == 0)
        def _():
            dw_ref[...] = jnp.zeros_like(dw_ref)

        halo = ah_ref[...] * _sigmoid(bh_ref[...])
        e_ref[pl.ds(0, CONV_HALO), :] = jnp.where(i > 0, halo, 0.0)
        e_ref[pl.ds(CONV_HALO, tm), :] = a_ref[...] * _sigmoid(b_ref[...])
        f_ref[pl.ds(0, tm), :] = d_ref[...]
        f_ref[pl.ds(tm, CONV_HALO), :] = jnp.where(i < nblk - 1, dn_ref[...], 0.0)
        _shifted_copies(e_ref, es_ref)
        _shifted_copies(f_ref, fs_ref)
        off = CONV_HALO - (CONV_WIDTH - 1)

        def strip(s, carry):
            cols = pl.ds(pl.multiple_of(s * 128, 128), 128)
            d = d_ref[:, cols]
            dc0 = jnp.zeros((tm, 128), F32)
            for k in range(CONV_WIDTH):
                dw_ref[pl.ds(k, 1), cols] += jnp.sum(d * _shifted_rows(e_ref, es_ref, off + k, tm, cols),
                                                     axis=0, keepdims=True)
                dc0 += w_ref[pl.ds(k, 1), cols] * _shifted_rows(f_ref, fs_ref, CONV_WIDTH - 1 - k, tm, cols)
            a = a_ref[:, cols]
            sb = _sigmoid(b_ref[:, cols])
            dp_ref[:, cols] = (dc0 * sb).astype(BF16)
            dp_ref[:, pl.ds(pl.multiple_of(D + s * 128, 128), 128)] = (dc0 * a * sb * (1.0 - sb)).astype(BF16)
            return carry

        lax.fori_loop(0, D // 128, strip, 0)

    return _call_hosting(
        body, side, name=name, grid=(nblk,),
        in_specs=[pl.BlockSpec((tm, D), lambda i: (i, 0)), pl.BlockSpec((tm, D), lambda i: (i, 1)),
                  pl.BlockSpec((CONV_HALO, D), lambda i: (jnp.maximum(i * hb - 1, 0), 0)),
                  pl.BlockSpec((CONV_HALO, D), lambda i: (jnp.maximum(i * hb - 1, 0), 1)),
                  pl.BlockSpec((tm, D), lambda i: (i, 0)),
                  pl.BlockSpec((CONV_HALO, D), lambda i: (jnp.minimum((i + 1) * hb, last), 0)),
                  pl.BlockSpec((CONV_HALO, D), lambda i: (0, 0))],
        out_specs=[pl.BlockSpec((tm, D2), lambda i: (i, 0)), pl.BlockSpec((CONV_HALO, D), lambda i: (0, 0))],
        out_shape=[jax.ShapeDtypeStruct((T, D2), BF16), jax.ShapeDtypeStruct((CONV_HALO, D), F32)],
        scratch_shapes=[pltpu.VMEM((tm + CONV_HALO, D), F32), pltpu.VMEM((tm + CONV_HALO, D), F32),
                        pltpu.VMEM((7, tm + CONV_HALO - 8, D), F32), pltpu.VMEM((7, tm + CONV_HALO - 8, D), F32)],
        operands=(p_glu, p_glu, p_glu, p_glu, dc1, dc1, dw_w))


def _rot_half(x):
    lane = lax.broadcasted_iota(jnp.int32, x.shape, 1)
    first = (lane % HEAD_DIM) < HEAD_DIM // 2
    return jnp.where(first, pltpu.roll(x, 128 - HEAD_DIM // 2, 1), pltpu.roll(x, HEAD_DIM // 2, 1))


def _rope_chunks(x, cs, sn, sign):
    outs = []
    for c in range(x.shape[1] // 128):
        xc = x[:, c * 128:(c + 1) * 128]
        outs.append(xc * cs + sign * (_rot_half(xc) * sn))
    return outs[0] if len(outs) == 1 else jnp.concatenate(outs, axis=1)


def rope_bwd(dq, dk, dv, cs, sn, name):
    T, D = dq.shape
    KV = dk.shape[1]
    tm = min(TM_ROW, T)

    def body(dq_ref, dk_ref, dv_ref, cs_ref, sn_ref, o_ref):
        cs_v, sn_v = cs_ref[...], sn_ref[...]
        o_ref[:, pl.ds(0, D)] = _rope_chunks(dq_ref[...], cs_v, sn_v, -1.0).astype(BF16)
        o_ref[:, pl.ds(D, KV)] = _rope_chunks(dk_ref[...], cs_v, sn_v, -1.0).astype(BF16)
        o_ref[:, pl.ds(D + KV, KV)] = dv_ref[...].astype(BF16)

    tab = pl.BlockSpec((tm, 128), lambda i: (i, 0))
    return pl.pallas_call(
        body, name=name, grid=(T // tm,),
        in_specs=[pl.BlockSpec((tm, D), lambda i: (i, 0)), pl.BlockSpec((tm, KV), lambda i: (i, 0)),
                  pl.BlockSpec((tm, KV), lambda i: (i, 0)), tab, tab],
        out_specs=pl.BlockSpec((tm, D + 2 * KV), lambda i: (i, 0)),
        out_shape=jax.ShapeDtypeStruct((T, D + 2 * KV), BF16),
        compiler_params=_params(("parallel",)),
    )(dq, dk, dv, cs, sn)


def _lane_lo():
    return lax.broadcasted_iota(jnp.int32, (1, 128), 1) < HEAD_DIM


def _band_mask(i, reps):
    shape = (reps * WINDOW, 2 * WINDOW)
    qi = lax.broadcasted_iota(jnp.int32, shape, 0) % WINDOW
    cj = lax.broadcasted_iota(jnp.int32, shape, 1)
    rel = qi - cj + WINDOW
    return (rel >= 0) & (rel < WINDOW) & ((i > 0) | (cj >= WINDOW))


def _stack_pairs(ref, first, n):
    parts = [ref[:, pl.ds((first + p) * 128, 128)] for p in range(n)]
    return parts[0] if n == 1 else jnp.concatenate(parts, axis=0)


def _pair_rows(n):
    return lax.broadcasted_iota(jnp.int32, (n * WINDOW, 1), 0) // WINDOW


def _per_pair_column(values, n):
    rows = _pair_rows(n)
    col = jnp.zeros((n * WINDOW, 1), F32) + values[0]
    for p in range(1, n):
        col = jnp.where(rows == p, values[p], col)
    return col


def _kv_lo_hi(x2, g):
    pair, half = divmod(g, 2)
    lo = _lane_lo()
    xg = x2[:, pair * 128:(pair + 1) * 128].astype(F32)
    xg = jnp.where(lo if half == 0 else ~lo, xg, 0.0)
    sw = pltpu.roll(xg, HEAD_DIM, 1)
    x_lo, x_hi = (xg, sw) if half == 0 else (sw, xg)
    return x_lo.astype(BF16), x_hi.astype(BF16)


def _softmax_sink(s, allowed, sink):
    s = jnp.where(allowed, s * (HEAD_DIM ** -0.5), NEG_INF)
    m = jnp.maximum(jnp.max(s, axis=-1, keepdims=True), sink)
    p = jnp.exp(s - m)
    es = jnp.exp(sink - m)
    inv = 1.0 / (jnp.sum(p, axis=-1, keepdims=True) + es)
    return p * inv, es * inv


def attn_fwd(qr, kr, vb, sinks, name):
    T, D = qr.shape
    KV = kr.shape[1]
    n_kv = KV // HEAD_DIM
    group = (D // HEAD_DIM) // n_kv
    nb = T // WINDOW

    npair = group // 2

    def body(sink_ref, q_ref, kp_ref, kc_ref, vp_ref, vc_ref, o_ref):
        i = pl.program_id(0)
        allowed = _band_mask(i, npair)
        k2 = jnp.concatenate([kp_ref[...], kc_ref[...]], axis=0)
        v2 = jnp.concatenate([vp_ref[...], vc_ref[...]], axis=0)
        outs = [None] * (D // 128)
        for g in range(n_kv):
            k_lo, k_hi = _kv_lo_hi(k2, g)
            v_lo, v_hi = _kv_lo_hi(v2, g)
            first = (g * group) // 2
            q = _stack_pairs(q_ref, first, npair)
            sink_e = _per_pair_column([sink_ref[0, g * group + 2 * p] for p in range(npair)], npair)
            sink_o = _per_pair_column([sink_ref[0, g * group + 2 * p + 1] for p in range(npair)], npair)
            pe, _ = _softmax_sink(_dot_nt(q, k_lo), allowed, sink_e)
            po, _ = _softmax_sink(_dot_nt(q, k_hi), allowed, sink_o)
            o = _dot(pe.astype(BF16), v_lo) + _dot(po.astype(BF16), v_hi)
            for p in range(npair):
                outs[first + p] = o[p * WINDOW:(p + 1) * WINDOW]
        o_ref[...] = jnp.concatenate(outs, axis=1).astype(BF16)

    prev = lambda i: (jnp.maximum(i - 1, 0), 0)
    cur = lambda i: (i, 0)
    return pl.pallas_call(
        body, name=name, grid=(nb,),
        in_specs=[pl.BlockSpec(memory_space=pltpu.SMEM),
                  pl.BlockSpec((WINDOW, D), cur),
                  pl.BlockSpec((WINDOW, KV), prev), pl.BlockSpec((WINDOW, KV), cur),
                  pl.BlockSpec((WINDOW, KV), prev), pl.BlockSpec((WINDOW, KV), cur)],
        out_specs=pl.BlockSpec((WINDOW, D), cur),
        out_shape=jax.ShapeDtypeStruct((T, D), BF16),
        compiler_params=_params(("parallel",)),
    )(sinks, qr, kr, kr, vb, vb)


def attn_bwd(qr, kr, vb, o, do, sinks, name):
    T, D = qr.shape
    KV = kr.shape[1]
    n_heads = D // HEAD_DIM
    n_kv = KV // HEAD_DIM
    group = n_heads // n_kv
    nb = T // WINDOW
    npair = group // 2
    scale = HEAD_DIM ** -0.5

    def body(sink_ref, q_ref, kp_ref, kc_ref, vp_ref, vc_ref, o_ref, do_ref,
             dq_ref, dk_ref, dv_ref, ds_ref, ck_ref, cv_ref):
        i = pl.program_id(0)
        lo = _lane_lo()

        @pl.when(i == 0)
        def _():
            ck_ref[...] = jnp.zeros_like(ck_ref)
            cv_ref[...] = jnp.zeros_like(cv_ref)
            ds_ref[...] = jnp.zeros_like(ds_ref)

        @pl.when(i < nb)
        def _():
            allowed = _band_mask(i, npair)
            rows = _pair_rows(npair)
            k2 = jnp.concatenate([kp_ref[...], kc_ref[...]], axis=0)
            v2 = jnp.concatenate([vp_ref[...], vc_ref[...]], axis=0)
            lane = lax.broadcasted_iota(jnp.int32, (1, 128), 1)
            dsink = jnp.zeros((1, 128), F32)
            dq_out = [None] * (D // 128)
            dk_pairs = [jnp.zeros((2 * WINDOW, 128), F32) for _ in range(KV // 128)]
            dv_pairs = [jnp.zeros((2 * WINDOW, 128), F32) for _ in range(KV // 128)]
            for g in range(n_kv):
                k_lo, k_hi = _kv_lo_hi(k2, g)
                v_lo, v_hi = _kv_lo_hi(v2, g)
                first = (g * group) // 2
                q = _stack_pairs(q_ref, first, npair)
                dop = _stack_pairs(do_ref, first, npair)
                dd = dop.astype(F32) * _stack_pairs(o_ref, first, npair).astype(F32)
                dq = jnp.zeros((npair * WINDOW, 128), F32)
                dkg = jnp.zeros((2 * WINDOW, 128), F32)
                dvg = jnp.zeros((2 * WINDOW, 128), F32)
                for parity, k_h, v_h, sel in ((0, k_lo, v_lo, lo), (1, k_hi, v_hi, ~lo)):
                    heads = [g * group + 2 * p + parity for p in range(npair)]
                    sink = _per_pair_column([sink_ref[0, h] for h in heads], npair)
                    p_, ps = _softmax_sink(_dot_nt(q, k_h), allowed, sink)
                    delta = jnp.sum(jnp.where(sel, dd, 0.0), axis=-1, keepdims=True)
                    dsc = (p_ * (_dot_nt(dop, v_h) - delta)).astype(BF16)
                    sd = -ps * delta
                    for p, h in enumerate(heads):
                        dsink += jnp.where(lane == h, jnp.sum(jnp.where(rows == p, sd, 0.0)), 0.0)
                    dq += _dot(dsc, k_h)
                    dkg += jnp.where(sel, _dot_tn(dsc, q), 0.0)
                    dvg += jnp.where(sel, _dot_tn(p_.astype(BF16), dop), 0.0)
                for p in range(npair):
                    dq_out[first + p] = dq[p * WINDOW:(p + 1) * WINDOW]
                pair, half = divmod(g, 2)
                keep = lo if half == 0 else ~lo
                dk_pairs[pair] += jnp.where(keep, dkg + pltpu.roll(dkg, HEAD_DIM, 1), 0.0) * scale
                dv_pairs[pair] += jnp.where(keep, dvg + pltpu.roll(dvg, HEAD_DIM, 1), 0.0)
            dq_ref[...] = jnp.concatenate(dq_out, axis=1) * scale
            dk2 = dk_pairs[0] if len(dk_pairs) == 1 else jnp.concatenate(dk_pairs, axis=1)
            dv2 = dv_pairs[0] if len(dv_pairs) == 1 else jnp.concatenate(dv_pairs, axis=1)
            dk_ref[...] = ck_ref[...] + dk2[:WINDOW]
            dv_ref[...] = cv_ref[...] + dv2[:WINDOW]
            ck_ref[...] = dk2[WINDOW:]
            cv_ref[...] = dv2[WINDOW:]
            ds_ref[pl.ds(0, 1), :] += dsink

        @pl.when(i == nb)
        def _():
            dk_ref[...] = ck_ref[...]
            dv_ref[...] = cv_ref[...]

    prev = lambda i: (jnp.maximum(i - 1, 0), 0)
    cur = lambda i: (jnp.minimum(i, nb - 1), 0)
    prevc = lambda i: (jnp.maximum(jnp.minimum(i, nb - 1) - 1, 0), 0)
    return pl.pallas_call(
        body, name=name, grid=(nb + 1,),
        in_specs=[pl.BlockSpec(memory_space=pltpu.SMEM),
                  pl.BlockSpec((WINDOW, D), cur),
                  pl.BlockSpec((WINDOW, KV), prevc), pl.BlockSpec((WINDOW, KV), cur),
                  pl.BlockSpec((WINDOW, KV), prevc), pl.BlockSpec((WINDOW, KV), cur),
                  pl.BlockSpec((WINDOW, D), cur), pl.BlockSpec((WINDOW, D), cur)],
        out_specs=[pl.BlockSpec((WINDOW, D), cur), pl.BlockSpec((WINDOW, KV), prev),
                   pl.BlockSpec((WINDOW, KV), prev), pl.BlockSpec((8, 128), lambda i: (0, 0))],
        out_shape=[jax.ShapeDtypeStruct((T, D), F32), jax.ShapeDtypeStruct((T, KV), F32),
                   jax.ShapeDtypeStruct((T, KV), F32), jax.ShapeDtypeStruct((8, 128), F32)],
        scratch_shapes=[pltpu.VMEM((WINDOW, KV), F32), pltpu.VMEM((WINDOW, KV), F32)],
        compiler_params=_params(("arbitrary",)),
    )(sinks, qr, kr, kr, vb, vb, o, do)


def merge_fwd(x, c3, o, p_gate, gate_b, w_proj, w_o, w_out, name):
    T, D = x.shape
    tm = min(TM_ROW, T)

    def body(x_ref, c3_ref, o_ref, gc_ref, ga_ref, bc_ref, ba_ref, wp_ref, wo_ref, wout_ref,
             xo_ref, co_ref, ao_ref, mg_ref):
        conv_out = _dot(c3_ref[...], wp_ref[...])
        attn_out = _dot(o_ref[...], wo_ref[...])
        merged = (_sigmoid(gc_ref[...] + bc_ref[...]) * conv_out
                  + _sigmoid(ga_ref[...] + ba_ref[...]) * attn_out).astype(BF16)
        co_ref[...] = conv_out.astype(BF16)
        ao_ref[...] = attn_out.astype(BF16)
        mg_ref[...] = merged
        xo_ref[...] = x_ref[...] + _dot(merged, wout_ref[...])

    blk = lambda j: pl.BlockSpec((tm, D), lambda i: (i, j))
    row = lambda j: pl.BlockSpec((1, D), lambda i: (0, j))
    mat = pl.BlockSpec((D, D), lambda i: (0, 0))
    return pl.pallas_call(
        body, name=name, grid=(T // tm,),
        in_specs=[blk(0), blk(0), blk(0), blk(0), blk(1), row(0), row(1), mat, mat, mat],
        out_specs=[blk(0), blk(0), blk(0), blk(0)],
        out_shape=[jax.ShapeDtypeStruct((T, D), F32)] + [jax.ShapeDtypeStruct((T, D), BF16)] * 3,
        compiler_params=_params(("parallel",)),
    )(x, c3, o, p_gate, p_gate, gate_b, gate_b, w_proj, w_o, w_out)


def merge_bwd(dx, p_gate, gate_b, conv_out, attn_out, c1, ln_g, ln_b, w_proj, w_o, w_out, name, side=None):
    T, D = dx.shape
    tm = min(TM_ROW, T)

    def body(dx_ref, gc_ref, ga_ref, bc_ref, ba_ref, co_ref, ao_ref, c1_ref, g_ref, be_ref,
             wp_ref, wo_ref, wout_ref, dgt_ref, dco_ref, dao_ref, do_ref, dc1_ref, sm_ref):
        @pl.when(pl.program_id(0) == 0)
        def _():
            sm_ref[...] = jnp.zeros_like(sm_ref)

        dm = _dot_nt(dx_ref[...].astype(BF16), wout_ref[...])
        sc = _sigmoid(gc_ref[...] + bc_ref[...])
        sa = _sigmoid(ga_ref[...] + ba_ref[...])
        dco = (dm * sc).astype(BF16)
        dao = (dm * sa).astype(BF16)
        dgc = dm * co_ref[...].astype(F32) * sc * (1.0 - sc)
        dga = dm * ao_ref[...].astype(F32) * sa * (1.0 - sa)
        dgt_ref[:, pl.ds(0, D)] = dgc.astype(BF16)
        dgt_ref[:, pl.ds(D, D)] = dga.astype(BF16)
        dco_ref[...] = dco
        dao_ref[...] = dao
        do_ref[...] = _dot_nt(dao, wo_ref[...]).astype(BF16)
        dc3 = _dot_nt(dco, wp_ref[...])
        xhat, rstd = _layernorm_stats(c1_ref[...])
        c2 = xhat * g_ref[...] + be_ref[...]
        dc2 = dc3 * _silu_grad(c2, _sigmoid(c2))
        dxh = dc2 * g_ref[...]
        dc1 = rstd * (dxh - jnp.mean(dxh, axis=-1, keepdims=True)
                      - xhat * jnp.mean(dxh * xhat, axis=-1, keepdims=True))
        dc1_ref[...] = dc1
        colsum = lambda v: jnp.sum(v, axis=0, keepdims=True)
        for r, (left, right) in enumerate(((dgc, dga), (dc2 * xhat, dc2), (dc1, None))):
            sm_ref[pl.ds(r, 1), pl.ds(0, D)] += colsum(left)
            if right is not None:
                sm_ref[pl.ds(r, 1), pl.ds(D, D)] += colsum(right)

    blk = lambda j: pl.BlockSpec((tm, D), lambda i: (i, j))
    row = lambda j: pl.BlockSpec((1, D), lambda i: (0, j))
    mat = pl.BlockSpec((D, D), lambda i: (0, 0))
    return _call_hosting(
        body, side, name=name, grid=(T // tm,),
        in_specs=[blk(0), blk(0), blk(1), row(0), row(1), blk(0), blk(0), blk(0), row(0), row(0), mat, mat, mat],
        out_specs=[pl.BlockSpec((tm, 2 * D), lambda i: (i, 0)), blk(0), blk(0), blk(0), blk(0),
                   pl.BlockSpec((8, 2 * D), lambda i: (0, 0))],
        out_shape=[jax.ShapeDtypeStruct((T, 2 * D), BF16)] + [jax.ShapeDtypeStruct((T, D), BF16)] * 3
                  + [jax.ShapeDtypeStruct((T, D), F32), jax.ShapeDtypeStruct((8, 2 * D), F32)],
        scratch_shapes=[],
        operands=(dx, p_gate, p_gate, gate_b, gate_b, conv_out, attn_out, c1, ln_g, ln_b, w_proj, w_o, w_out))


def loss_head(x, nw, target, name):
    T, D = x.shape
    tm = min(TM_ROW, T)

    def body(x_ref, nw_ref, t_ref, dx_ref, sm_ref):
        @pl.when(pl.program_id(0) == 0)
        def _():
            sm_ref[...] = jnp.zeros_like(sm_ref)

        xv = x_ref[...]
        err = xv * _rms_scale(xv) * nw_ref[...] - t_ref[...]
        loss = 0.5 * jnp.sum(jnp.mean(err * err, axis=-1, keepdims=True))
        dxn, dnw = _rms_bwd(xv, nw_ref[...], err * (1.0 / D))
        dx_ref[...] = dxn
        sm_ref[pl.ds(0, 1), :] += dnw
        sm_ref[pl.ds(1, 1), :] += jnp.zeros((1, D), F32) + loss

    return pl.pallas_call(
        body, name=name, grid=(T // tm,),
        in_specs=[pl.BlockSpec((tm, D), lambda i: (i, 0)), pl.BlockSpec((1, D), lambda i: (0, 0)),
                  pl.BlockSpec((tm, D), lambda i: (i, 0))],
        out_specs=[pl.BlockSpec((tm, D), lambda i: (i, 0)), pl.BlockSpec((8, D), lambda i: (0, 0))],
        out_shape=[jax.ShapeDtypeStruct((T, D), F32), jax.ShapeDtypeStruct((8, D), F32)],
        compiler_params=_params(("arbitrary",)),
    )(x, nw, target)


def adamw(w, g, m, v, name):
    R, C = w.shape
    tr = _row_tile(R, TR_ELT)

    def body(w_ref, g_ref, m_ref, v_ref, d_ref, mo_ref, vo_ref):
        gv = g_ref[...]
        mn = ADAM_B1 * m_ref[...] + (1.0 - ADAM_B1) * gv
        vn = ADAM_B2 * v_ref[...] + (1.0 - ADAM_B2) * (gv * gv)
        m_hat = mn / (1.0 - ADAM_B1 ** ADAM_STEP)
        v_hat = vn / (1.0 - ADAM_B2 ** ADAM_STEP)
        d_ref[...] = -ADAM_LR * (m_hat / (jnp.sqrt(v_hat) + ADAM_EPS) + ADAM_WD * w_ref[...])
        mo_ref[...] = mn
        vo_ref[...] = vn

    spec = pl.BlockSpec((tr, C), lambda i: (i, 0))
    return pl.pallas_call(
        body, name=name, grid=(R // tr,), in_specs=[spec] * 4, out_specs=[spec] * 3,
        out_shape=[jax.ShapeDtypeStruct((R, C), F32)] * 3,
        compiler_params=_params(("parallel",)),
    )(w, g, m, v)


def _place():
    return lax.axis_index("x"), lax.axis_index("y"), lax.axis_index("c")


def place_shard(place, w, dtype, name):
    R, C = w.shape
    tr = _row_tile(R, TR_ELT)

    def body(pc_ref, w_ref, o_ref):
        o_ref[...] = w_ref[...].astype(dtype)

    return pl.pallas_call(
        body, name=name,
        grid_spec=pltpu.PrefetchScalarGridSpec(
            num_scalar_prefetch=1, grid=(R // tr,),
            in_specs=[pl.BlockSpec((tr, C), lambda r, pc: (r, 0))],
            out_specs=pl.BlockSpec((None, tr, C), lambda r, pc: (pc[0], r, 0))),
        out_shape=jax.ShapeDtypeStruct((N_CHIPS, R, C), dtype),
        compiler_params=_params(("arbitrary",)),
    )(place, w)


def gather_side(shards, small):
    n, ns = len(shards), len(small)

    def ici_copy(dst, sems, k, j, x, y, c, sending):
        px, py = x ^ (j >> 1), y ^ (j & 1)
        slot = 2 * x + y if sending else 2 * px + py
        half = dst[k].shape[1] // 2
        part = dst[k].at[slot, pl.ds(c * half, half)] if k < n else dst[k].at[slot]
        return pltpu.make_async_remote_copy(part, part, sems[0].at[3 * k + j - 1], sems[1].at[3 * k + j - 1],
                                            device_id=(px, py, c), device_id_type=MESH)

    def d2d_copy(dst, sems, k, j, x, y, c, sending):
        half = dst[k].shape[1] // 2
        part = dst[k].at[2 * (x ^ (j >> 1)) + (y ^ (j & 1)), pl.ds((c if sending else 1 - c) * half, half)]
        return pltpu.make_async_remote_copy(part, part, sems[2].at[3 * k + j - 1], sems[3].at[3 * k + j - 1],
                                            device_id=(x, y, 1 - c), device_id_type=MESH)

    def start(src, dst, sems):
        x, y, c = _place()
        for k in range(n + ns):
            for j in (1, 2, 3):
                ici_copy(dst, sems, k, j, x, y, c, True).start()

    def finish(src, dst, sems):
        x, y, c = _place()
        for k in range(n + ns):
            for j in (1, 2, 3):
                ici_copy(dst, sems, k, j, x, y, c, False).wait_recv()
                if k < n:
                    d2d_copy(dst, sems, k, j, x, y, c, True).start()
        for k in range(n):
            for j in (1, 2, 3):
                d2d_copy(dst, sems, k, j, x, y, c, False).wait_recv()
        for k in range(n + ns):
            for j in (1, 2, 3):
                ici_copy(dst, sems, k, j, x, y, c, True).wait_send()
                if k < n:
                    d2d_copy(dst, sems, k, j, x, y, c, True).wait_send()

    arrays = list(shards) + list(small)
    return dict(inputs=arrays, out_shapes=[jax.ShapeDtypeStruct(a.shape, a.dtype) for a in arrays],
                aliases={k: k for k in range(n + ns)},
                sems=[pltpu.SemaphoreType.DMA((3 * (n + ns),)), pltpu.SemaphoreType.DMA((3 * (n + ns),)),
                      pltpu.SemaphoreType.DMA((3 * n,)), pltpu.SemaphoreType.DMA((3 * n,))],
                start=start, finish=finish)


def run_side(side, name):
    n_in, n_out = len(side["inputs"]), len(side["out_shapes"])

    def body(*refs):
        src, dst, sems = refs[:n_in], refs[n_in:n_in + n_out], refs[n_in + n_out:]
        side["start"](src, dst, sems)
        side["finish"](src, dst, sems)

    return pl.pallas_call(
        body, name=name, in_specs=[HBM_SPEC] * n_in, out_specs=[HBM_SPEC] * n_out,
        out_shape=side["out_shapes"], input_output_aliases=side["aliases"], scratch_shapes=side["sems"],
    )(*side["inputs"])


def allreduce_small(block):
    R, C = block.shape

    def body(x_ref, out_ref, all_ref, send_sems, recv_sems, local_sem):
        x, y, c = _place()
        me, sibling = (x, y, c), (x, y, 1 - c)
        chips = [(1 - x, y), (x, 1 - y), (1 - x, 1 - y)]

        def slot(px, py, pc):
            return all_ref.at[4 * px + 2 * py + pc]

        def copy(k, block_of, to, src=None):
            return pltpu.make_async_remote_copy(
                src_ref=slot(*block_of) if src is None else src, dst_ref=slot(*block_of),
                send_sem=send_sems.at[k], recv_sem=recv_sems.at[k], device_id=to, device_id_type=MESH)

        mine = pltpu.make_async_copy(x_ref, slot(*me), local_sem)
        mine.start()
        first = [copy(0, me, sibling, src=x_ref)]
        first += [copy(1 + j, me, (*chip, c), src=x_ref) for j, chip in enumerate(chips)]
        for cp in first:
            cp.start()
        passed = [copy(4 + j, (*chip, c), sibling) for j, chip in enumerate(chips)]
        for j, chip in enumerate(chips):
            copy(1 + j, (*chip, c), me).wait_recv()
            passed[j].start()
        copy(0, sibling, me).wait_recv()
        for j, chip in enumerate(chips):
            copy(4 + j, (*chip, 1 - c), me).wait_recv()
        for cp in first + passed:
            cp.wait_send()
        mine.wait()
        total = all_ref[0]
        for d in range(1, N_DEV):
            total = total + all_ref[d]
        out_ref[...] = total

    return pl.pallas_call(
        body, name="allreduce_small",
        in_specs=[pl.BlockSpec(memory_space=pltpu.VMEM)], out_specs=pl.BlockSpec(memory_space=pltpu.VMEM),
        out_shape=jax.ShapeDtypeStruct((R, C), F32),
        scratch_shapes=[pltpu.VMEM((N_DEV, R, C), F32), pltpu.SemaphoreType.DMA((7,)),
                        pltpu.SemaphoreType.DMA((7,)), pltpu.SemaphoreType.DMA],
        compiler_params=pltpu.CompilerParams(vmem_limit_bytes=VMEM_LIMIT),
    )(block)


def exchange_siblings_side(grads):
    n = len(grads)

    def copies(src, dst, sems):
        x, y, c = _place()
        for k in range(n):
            half = src[k].shape[1] // 2
            yield pltpu.make_async_remote_copy(src[k].at[:, pl.ds((1 - c) * half, half)], dst[k],
                                               sems[0].at[k], sems[1].at[k],
                                               device_id=(x, y, 1 - c), device_id_type=MESH)

    def start(src, dst, sems):
        for cp in copies(src, dst, sems):
            cp.start()

    def finish(src, dst, sems):
        for cp in copies(src, dst, sems):
            cp.wait()

    return dict(inputs=list(grads), aliases={},
                out_shapes=[jax.ShapeDtypeStruct((N_CHIPS, g.shape[1] // 2, g.shape[2]), F32) for g in grads],
                sems=[pltpu.SemaphoreType.DMA((n,)), pltpu.SemaphoreType.DMA((n,))], start=start, finish=finish)


def rs_chip_sum(place, grad, sib, name):
    NP, R, C = grad.shape
    half = R // 2
    tr = _row_tile(half, TR_ELT)
    nr = half // tr

    def body(pc_ref, g_ref, s_ref, wire_ref, own_ref):
        q = pl.program_id(1)
        total = g_ref[...] + s_ref[...]
        wire_ref[...] = total.astype(BF16)

        @pl.when(q == pc_ref[0])
        def _():
            own_ref[...] = total

    return pl.pallas_call(
        body, name=name,
        grid_spec=pltpu.PrefetchScalarGridSpec(
            num_scalar_prefetch=1, grid=(nr, NP),
            in_specs=[pl.BlockSpec((None, tr, C), lambda r, q, pc: (q, pc[1] * nr + r, 0)),
                      pl.BlockSpec((None, tr, C), lambda r, q, pc: (q, r, 0))],
            out_specs=[pl.BlockSpec((None, tr, C), lambda r, q, pc: (q, r, 0)),
                       pl.BlockSpec((tr, C), lambda r, q, pc: (r, 0))]),
        out_shape=[jax.ShapeDtypeStruct((NP, half, C), BF16), jax.ShapeDtypeStruct((half, C), F32)],
        compiler_params=_params(("arbitrary", "arbitrary")),
    )(place, grad, sib)


def exchange_chips_side(wires):
    n = len(wires)

    def copies(src, dst, sems):
        x, y, c = _place()
        for k in range(n):
            for j in (1, 2, 3):
                qx, qy = x ^ (j >> 1), y ^ (j & 1)
                yield pltpu.make_async_remote_copy(src[k].at[2 * qx + qy], dst[k].at[2 * x + y],
                                                   sems[0].at[3 * k + j - 1], sems[1].at[3 * k + j - 1],
                                                   device_id=(qx, qy, c), device_id_type=MESH)

    def start(src, dst, sems):
        for cp in copies(src, dst, sems):
            cp.start()

    def finish(src, dst, sems):
        for cp in copies(src, dst, sems):
            cp.wait()

    return dict(inputs=list(wires), out_shapes=[jax.ShapeDtypeStruct(w.shape, BF16) for w in wires], aliases={},
                sems=[pltpu.SemaphoreType.DMA((3 * n,)), pltpu.SemaphoreType.DMA((3 * n,))],
                start=start, finish=finish)


def rs_final_sum(place, own, got, name):
    NP, half, C = got.shape
    tr = _row_tile(half, TR_ELT)
    nr = half // tr

    def body(pc_ref, own_ref, g1_ref, g2_ref, g3_ref, out_ref):
        out_ref[...] = ((own_ref[...] + g1_ref[...].astype(F32)) + g2_ref[...].astype(F32)) + g3_ref[...].astype(F32)

    slot = lambda j: pl.BlockSpec((None, tr, C), lambda r, pc: (pc[0] ^ j, r, 0))
    return pl.pallas_call(
        body, name=name,
        grid_spec=pltpu.PrefetchScalarGridSpec(
            num_scalar_prefetch=1, grid=(nr,),
            in_specs=[pl.BlockSpec((tr, C), lambda r, pc: (r, 0)), slot(1), slot(2), slot(3)],
            out_specs=pl.BlockSpec((tr, C), lambda r, pc: (pc[1] * nr + r, 0))),
        out_shape=jax.ShapeDtypeStruct((2 * half, C), F32),
        compiler_params=_params(("arbitrary",)),
    )(place, own, got, got, got)


def rs_share_siblings(totals):
    n = len(totals)

    def body(*refs):
        dst = refs[n:2 * n]
        send_sems, recv_sems = refs[2 * n:]
        x, y, c = _place()
        copies = []
        for k in range(n):
            half = dst[k].shape[0] // 2
            rows = dst[k].at[pl.ds(c * half, half)]
            cp = pltpu.make_async_remote_copy(rows, rows, send_sems.at[k], recv_sems.at[k],
                                              device_id=(x, y, 1 - c), device_id_type=MESH)
            cp.start()
            copies.append(cp)
        for k, cp in enumerate(copies):
            cp.wait_send()
            half = dst[k].shape[0] // 2
            got = dst[k].at[pl.ds((1 - c) * half, half)]
            pltpu.make_async_remote_copy(got, got, send_sems.at[k], recv_sems.at[k],
                                         device_id=(x, y, c), device_id_type=MESH).wait_recv()

    return pl.pallas_call(
        body, name="rs_share_siblings",
        in_specs=[HBM_SPEC] * n, out_specs=[HBM_SPEC] * n,
        out_shape=[jax.ShapeDtypeStruct(t.shape, F32) for t in totals],
        input_output_aliases={k: k for k in range(n)},
        scratch_shapes=[pltpu.SemaphoreType.DMA((n,)), pltpu.SemaphoreType.DMA((n,))],
    )(*totals)


def rs_to_wires(place, grads, tag, sibs=None):
    if sibs is None:
        sibs = run_side(exchange_siblings_side(grads), f"rs_exchange_siblings_{tag}")
    wires, owns = [], []
    for k, (g, s) in enumerate(zip(grads, sibs)):
        w, o = rs_chip_sum(place, g, s, f"rs_chip_sum_{tag}{k}")
        wires.append(w)
        owns.append(o)
    return wires, owns


def rs_finish(place, owns, gots):
    totals = [rs_final_sum(place, o, g, f"rs_final_sum_{k}") for k, (o, g) in enumerate(zip(owns, gots))]
    return rs_share_siblings(totals)


def _rope_tables(positions):
    half = HEAD_DIM // 2
    inv_freq = ROPE_THETA ** (-jnp.arange(half, dtype=F32) / half)
    ang = positions.astype(F32)[:, None] * inv_freq
    cos, sin = jnp.cos(ang), jnp.sin(ang)
    return jnp.tile(cos, (1, 4)), jnp.concatenate([-sin, sin, -sin, sin], axis=1)


def _cols_from_pieces(pieces, start, stop):
    C = pieces.shape[2]
    parts = []
    for q in range(N_CHIPS):
        lo, hi = max(start, q * C), min(stop, (q + 1) * C)
        if lo < hi:
            parts.append(pieces[q][:, lo - q * C:hi - q * C])
    return parts[0] if len(parts) == 1 else jnp.concatenate(parts, axis=1)


def _pieces_from_groups(groups):
    C = sum(g.shape[1] for g in groups) // N_CHIPS
    pieces = []
    for q in range(N_CHIPS):
        parts, off = [], 0
        for g in groups:
            lo, hi = max(q * C, off), min((q + 1) * C, off + g.shape[1])
            if lo < hi:
                parts.append(g[:, lo - off:hi - off])
            off += g.shape[1]
        pieces.append(parts[0] if len(parts) == 1 else jnp.concatenate(parts, axis=1))
    return jnp.stack(pieces)


def kernel(x, positions, ffn1_norm, ffn1_w_gate, ffn1_w_up, ffn1_w_down, mix_norm, w_in, conv_dw_w, conv_dw_b, conv_ln_g, conv_ln_b, conv_w_proj, attn_sinks, attn_w_o, gate_b, w_out, ffn2_norm, ffn2_w_gate, ffn2_w_up, ffn2_w_down, final_norm, loss_target, m_ffn1_norm, m_ffn1_w_gate, m_ffn1_w_up, m_ffn1_w_down, m_mix_norm, m_w_in, m_conv_dw_w, m_conv_dw_b, m_conv_ln_g, m_conv_ln_b, m_conv_w_proj, m_attn_sinks, m_attn_w_o, m_gate_b, m_w_out, m_ffn2_norm, m_ffn2_w_gate, m_ffn2_w_up, m_ffn2_w_down, m_final_norm, v_ffn1_norm, v_ffn1_w_gate, v_ffn1_w_up, v_ffn1_w_down, v_mix_norm, v_w_in, v_conv_dw_w, v_conv_dw_b, v_conv_ln_g, v_conv_ln_b, v_conv_w_proj, v_attn_sinks, v_attn_w_o, v_gate_b, v_w_out, v_ffn2_norm, v_ffn2_w_gate, v_ffn2_w_up, v_ffn2_w_down, v_final_norm):
    weights = dict(ffn1_norm=ffn1_norm, ffn1_w_gate=ffn1_w_gate, ffn1_w_up=ffn1_w_up, ffn1_w_down=ffn1_w_down,
                   mix_norm=mix_norm, w_in=w_in, conv_dw_w=conv_dw_w, conv_dw_b=conv_dw_b, conv_ln_g=conv_ln_g,
                   conv_ln_b=conv_ln_b, conv_w_proj=conv_w_proj, attn_sinks=attn_sinks, attn_w_o=attn_w_o,
                   gate_b=gate_b, w_out=w_out, ffn2_norm=ffn2_norm, ffn2_w_gate=ffn2_w_gate, ffn2_w_up=ffn2_w_up,
                   ffn2_w_down=ffn2_w_down, final_norm=final_norm)
    m_in = dict(ffn1_norm=m_ffn1_norm, ffn1_w_gate=m_ffn1_w_gate, ffn1_w_up=m_ffn1_w_up, ffn1_w_down=m_ffn1_w_down,
                mix_norm=m_mix_norm, w_in=m_w_in, conv_dw_w=m_conv_dw_w, conv_dw_b=m_conv_dw_b,
                conv_ln_g=m_conv_ln_g, conv_ln_b=m_conv_ln_b, conv_w_proj=m_conv_w_proj, attn_sinks=m_attn_sinks,
                attn_w_o=m_attn_w_o, gate_b=m_gate_b, w_out=m_w_out, ffn2_norm=m_ffn2_norm,
                ffn2_w_gate=m_ffn2_w_gate, ffn2_w_up=m_ffn2_w_up, ffn2_w_down=m_ffn2_w_down, final_norm=m_final_norm)
    v_in = dict(ffn1_norm=v_ffn1_norm, ffn1_w_gate=v_ffn1_w_gate, ffn1_w_up=v_ffn1_w_up, ffn1_w_down=v_ffn1_w_down,
                mix_norm=v_mix_norm, w_in=v_w_in, conv_dw_w=v_conv_dw_w, conv_dw_b=v_conv_dw_b,
                conv_ln_g=v_conv_ln_g, conv_ln_b=v_conv_ln_b, conv_w_proj=v_conv_w_proj, attn_sinks=v_attn_sinks,
                attn_w_o=v_attn_w_o, gate_b=v_gate_b, w_out=v_w_out, ffn2_norm=v_ffn2_norm,
                ffn2_w_gate=v_ffn2_w_gate, ffn2_w_up=v_ffn2_w_up, ffn2_w_down=v_ffn2_w_down, final_norm=v_final_norm)
    names = list(weights)
    big = ["ffn1_w_gate", "ffn1_w_up", "ffn1_w_down", "w_in", "conv_w_proj", "attn_w_o", "w_out",
           "ffn2_w_gate", "ffn2_w_up", "ffn2_w_down"]

    xs = x[0]
    T, D = xs.shape
    KV = (w_in.shape[2] * N_CHIPS - 5 * D) // 2
    n_heads = D // HEAD_DIM
    my_chip = 2 * lax.axis_index("x") + lax.axis_index("y")
    place = jnp.stack([my_chip, lax.axis_index("c")]).astype(jnp.int32)

    placed = {k: place_shard(place, weights[k][0], BF16, f"place_{k}") for k in big}
    placed_dw = place_shard(place, conv_dw_w[0], F32, "place_conv_dw_w")
    first, later = big[:3], big[3:]
    wg1, wu1, wd1 = run_side(gather_side([placed[k] for k in first], []), "gather_ffn1")
    x1, h1, g1, u1, *gathered = ffn_fwd(x[0], ffn1_norm, wg1, wu1, wd1, "ffn1_fwd",
                                        side=gather_side([placed[k] for k in later], [placed_dw]))
    full = dict(zip(later + ["conv_dw_w"], gathered))
    wg2, wu2, wd2 = full["ffn2_w_gate"], full["ffn2_w_up"], full["ffn2_w_down"]
    w_glu = _cols_from_pieces(full["w_in"], 0, 2 * D)
    w_qkv = _cols_from_pieces(full["w_in"], 2 * D, 3 * D + 2 * KV)
    w_gate = _cols_from_pieces(full["w_in"], 3 * D + 2 * KV, 5 * D + 2 * KV)
    w_proj = full["conv_w_proj"].reshape(D, D)
    w_o = full["attn_w_o"].reshape(D, D)
    w_out_f = full["w_out"].reshape(D, D)
    dw_w = full["conv_dw_w"].transpose(1, 0, 2).reshape(CONV_WIDTH, D)
    dw_w = jnp.concatenate([dw_w, jnp.zeros((CONV_HALO - CONV_WIDTH, D), F32)], axis=0)
    cs, sn = _rope_tables(positions[0])
    fn_row = final_norm.reshape(1, D)

    h2, p_glu, p_gate, qr, kr, vb = mix_in_fwd(x1, mix_norm, w_glu, w_qkv, w_gate, cs, sn, "mix_in_fwd")
    c1, c3 = conv_fwd(p_glu, dw_w, conv_dw_b, conv_ln_g, conv_ln_b, "conv_fwd")
    o = attn_fwd(qr, kr, vb, attn_sinks, "attn_fwd")
    x2, conv_out, attn_out, merged = merge_fwd(x1, c3, o, p_gate, gate_b, w_proj, w_o, w_out_f, "merge_fwd")
    x3, h3, g2, u2 = ffn_fwd(x2, ffn2_norm, wg2, wu2, wd2, "ffn2_fwd")

    dx3, head_sums = loss_head(x3, fn_row, loss_target[0], "loss_head")
    dx2, dg2, du2, a2, d_ffn2_norm = ffn_bwd_x(x2, ffn2_norm, g2, u2, wg2, wu2, wd2, dx3, "ffn2_bwd_x")
    dwg2, dwu2, dwd2 = ffn_bwd_w(h3, a2, dg2, du2, dx3, "ffn2_bwd_w")
    ffn2_grads = [dwg2, dwu2, dwd2]
    d_gates, d_conv_out, d_attn_out, d_o, dc1, merge_sums, *sibs_f2 = merge_bwd(
        dx2, p_gate, gate_b, conv_out, attn_out, c1, conv_ln_g, conv_ln_b, w_proj, w_o, w_out_f, "merge_bwd",
        side=exchange_siblings_side(ffn2_grads))
    d_w_out = matmul_tn(merged, dx2, "d_w_out")
    d_w_proj = matmul_tn(c3, d_conv_out, "d_conv_w_proj")
    d_w_o = matmul_tn(o, d_attn_out, "d_attn_w_o")
    wires_f2, owns_f2 = rs_to_wires(place, ffn2_grads, "ffn2", sibs=sibs_f2)
    d_glu, d_dw_w, *gots_f2 = conv_bwd(p_glu, dc1, dw_w, "conv_bwd", side=exchange_chips_side(wires_f2))
    dq, dk, dv, d_sinks = attn_bwd(qr, kr, vb, o, d_o, attn_sinks, "attn_bwd")
    d_qkv = rope_bwd(dq, dk, dv, cs, sn, "rope_bwd")
    dx1, d_mix_norm = mix_in_bwd([d_glu, d_qkv, d_gates], [w_glu, w_qkv, w_gate], x1, mix_norm, dx2, "mix_in_bwd")
    d_w_in = _pieces_from_groups([matmul_tn(h2, d_glu, "d_w_in_glu"), matmul_tn(h2, d_qkv, "d_w_in_qkv"),
                                  matmul_tn(h2, d_gates, "d_w_in_gate")])
    dwc = D // N_CHIPS
    mixer_grads = [d_w_in, d_w_proj.reshape(N_CHIPS, dwc, D), d_w_o.reshape(N_CHIPS, dwc, D),
                   d_w_out.reshape(N_CHIPS, dwc, D)]
    wires_m, owns_m = rs_to_wires(place, mixer_grads, "mixer")
    dx0, dg1, du1, a1, d_ffn1_norm, *gots_m = ffn_bwd_x(xs, ffn1_norm, g1, u1, wg1, wu1, wd1, dx1, "ffn1_bwd_x",
                                                        side=exchange_chips_side(wires_m))
    dwg1, dwu1, dwd1 = ffn_bwd_w(h1, a1, dg1, du1, dx1, "ffn1_bwd_w")
    wires_l, owns_l = rs_to_wires(place, [dwg1, dwu1, dwd1], "ffn1")
    gots_l = run_side(exchange_chips_side(wires_l), "rs_exchange_chips_ffn1")
    reduced = rs_finish(place, owns_f2 + owns_m + owns_l, list(gots_f2) + list(gots_m) + list(gots_l))

    pad_row = lambda v: jnp.pad(v, ((0, 0), (0, D - v.shape[1])))
    small_rows = jnp.concatenate([
        d_ffn1_norm, d_mix_norm, merge_sums[2:3, :D], merge_sums[1:2, :D], merge_sums[1:2, D:],
        pad_row(d_sinks[0:1, :n_heads]), merge_sums[0:1, :D], merge_sums[0:1, D:], d_ffn2_norm,
        head_sums[0:1], head_sums[1:2], jnp.zeros((5, D), F32), d_dw_w], axis=0)
    small = allreduce_small(small_rows)
    loss = small[10, 0]
    grads = {"ffn1_norm": small[0:1], "mix_norm": small[1:2], "conv_dw_b": small[2:3], "conv_ln_g": small[3:4],
             "conv_ln_b": small[4:5], "attn_sinks": small[5:6, :n_heads],
             "gate_b": jnp.concatenate([small[6:7], small[7:8]], axis=1), "ffn2_norm": small[8:9],
             "final_norm": small[9:10]}
    grads["conv_dw_w"] = lax.dynamic_slice(small[16:16 + CONV_WIDTH], (0, my_chip * dwc), (CONV_WIDTH, dwc))
    grads.update(zip(["ffn2_w_gate", "ffn2_w_up", "ffn2_w_down", "w_in", "conv_w_proj", "attn_w_o", "w_out",
                      "ffn1_w_gate", "ffn1_w_up", "ffn1_w_down"], reduced))

    deltas, new_m, new_v = {}, {}, {}
    for k in names:
        shape = weights[k].shape
        g2d = grads[k].reshape(-1, shape[-1])
        grads[k] = g2d.reshape(shape)
        d, mn, vn = adamw(weights[k].reshape(g2d.shape), g2d, m_in[k].reshape(g2d.shape),
                          v_in[k].reshape(g2d.shape), f"adamw_{k}")
        deltas[k], new_m[k], new_v[k] = d.reshape(shape), mn.reshape(shape), vn.reshape(shape)

    return (loss, dx0[None], *[grads[k] for k in names], *[deltas[k] for k in names],
            *[new_m[k] for k in names], *[new_v[k] for k in names])
```

```python
import functools

import jax
import jax.numpy as jnp
from jax import lax
from jax.experimental import pallas as pl
from jax.experimental.pallas import tpu as pltpu

F32 = jnp.float32
BF16 = jnp.bfloat16
MESH = pl.DeviceIdType.MESH

HEAD_DIM = 64
WINDOW = 128
CONV_WIDTH = 31
CONV_HALO = 32
ROPE_THETA = 10000.0
EPS = 1e-6
LN_EPS = 1e-5
NEG_INF = -1e30
N_CHIPS = 4
N_DEV = 8

ADAM_LR = 0.001
ADAM_B1 = 0.9
ADAM_B2 = 0.999
ADAM_EPS = 1e-08
ADAM_WD = 0.01
ADAM_STEP = 10

TM_FFN = 512
TM_FFN_FWD = 1024
TM_ROW = 256
TK_TN = 1024
TR_ELT = 256
VMEM_LIMIT = 56 * 1024 * 1024

NT_DIMS = (((1,), (1,)), ((), ()))
TN_DIMS = (((0,), (0,)), ((), ()))


def _row_tile(rows, cap):
    for t in range(min(cap, rows), 15, -1):
        if rows % t == 0 and t % 16 == 0:
            return t
    return rows


def _params(sem):
    return pltpu.CompilerParams(dimension_semantics=sem, vmem_limit_bytes=VMEM_LIMIT)


def _dot(a, b):
    return jnp.dot(a, b, preferred_element_type=F32)


def _dot_nt(a, b):
    return lax.dot_general(a, b, NT_DIMS, preferred_element_type=F32)


def _dot_tn(a, b):
    return lax.dot_general(a, b, TN_DIMS, preferred_element_type=F32)


def _split_rows(dot, a, b):
    m = a.shape[0] // 2
    return jnp.concatenate([dot(a[:m], b), dot(a[m:], b)], axis=0)


def _sigmoid(x):
    return jax.nn.sigmoid(x)


def _rms_scale(xv):
    return lax.rsqrt(jnp.mean(xv * xv, axis=-1, keepdims=True) + EPS)


def _rms_bwd(xv, nw, dh):
    r = _rms_scale(xv)
    dn = dh * nw
    dx = r * dn - xv * (r * r * r) * jnp.mean(dn * xv, axis=-1, keepdims=True)
    dnw = jnp.sum(dh * (xv * r), axis=0, keepdims=True)
    return dx, dnw


def _silu_grad(z, s):
    return s * (1.0 + z * (1.0 - s))


HBM_SPEC = pl.BlockSpec(memory_space=pl.ANY)


def _call_hosting(body, side, *, grid, in_specs, out_specs, out_shape, scratch_shapes, operands, name, aliases=None):
    params = _params(("arbitrary",) * len(grid))
    aliases = dict(aliases or {})
    if side is None:
        return pl.pallas_call(body, name=name, grid=grid, in_specs=in_specs, out_specs=out_specs, out_shape=out_shape,
                              scratch_shapes=scratch_shapes, input_output_aliases=aliases,
                              compiler_params=params)(*operands)
    n_in, n_out, n_scr = len(in_specs), len(out_shape), len(scratch_shapes)
    s_in, s_out = len(side["inputs"]), len(side["out_shapes"])

    def at_step(end):
        hit = pl.program_id(0) == (grid[0] - 1 if end else 0)
        for a in range(1, len(grid)):
            hit &= pl.program_id(a) == (grid[a] - 1 if end else 0)
        return hit

    def hosted(*refs):
        b = n_in + s_in
        c = b + n_out
        d = c + s_out
        e = d + n_scr
        src, dst, sems = refs[n_in:b], refs[c:d], refs[e:]

        @pl.when(at_step(False))
        def _():
            side["start"](src, dst, sems)

        body(*refs[:n_in], *refs[b:c], *refs[d:e])

        @pl.when(at_step(True))
        def _():
            side["finish"](src, dst, sems)

    return pl.pallas_call(
        hosted, name=name, grid=grid, in_specs=list(in_specs) + [HBM_SPEC] * s_in,
        out_specs=list(out_specs) + [HBM_SPEC] * s_out, out_shape=list(out_shape) + list(side["out_shapes"]),
        scratch_shapes=list(scratch_shapes) + list(side["sems"]),
        input_output_aliases={**aliases, **{n_in + a: n_out + b for a, b in side["aliases"].items()}},
        compiler_params=params)(*operands, *side["inputs"])


def ffn_fwd(x, nw, wg, wu, wd, name, side=None):
    T, D = x.shape
    NP, _, Fs = wg.shape
    tm = min(TM_FFN_FWD, T)

    def body(x_ref, nw_ref, wg_ref, wu_ref, wd_ref, xo_ref, h_ref, g_ref, u_ref, acc_ref):
        j = pl.program_id(1)

        @pl.when(j == 0)
        def _():
            xv = x_ref[...]
            h_ref[...] = (xv * _rms_scale(xv) * nw_ref[...]).astype(BF16)
            acc_ref[...] = jnp.zeros_like(acc_ref)

        h = h_ref[...]
        g = _dot(h, wg_ref[...])
        u = _dot(h, wu_ref[...])
        a = (g * _sigmoid(g)) * u
        g_ref[...] = g.astype(BF16)
        u_ref[...] = u.astype(BF16)
        acc_ref[...] += _dot(a.astype(BF16), wd_ref[...])

        @pl.when(j == NP - 1)
        def _():
            xo_ref[...] = x_ref[...] + 0.5 * acc_ref[...]

    return _call_hosting(
        body, side, name=name, grid=(T // tm, NP),
        in_specs=[pl.BlockSpec((tm, D), lambda i, j: (i, 0)),
                  pl.BlockSpec((1, D), lambda i, j: (0, 0)),
                  pl.BlockSpec((None, D, Fs), lambda i, j: (j, 0, 0)),
                  pl.BlockSpec((None, D, Fs), lambda i, j: (j, 0, 0)),
                  pl.BlockSpec((None, Fs, D), lambda i, j: (j, 0, 0))],
        out_specs=[pl.BlockSpec((tm, D), lambda i, j: (i, 0)),
                   pl.BlockSpec((tm, D), lambda i, j: (i, 0)),
                   pl.BlockSpec((None, tm, Fs), lambda i, j: (j, i, 0)),
                   pl.BlockSpec((None, tm, Fs), lambda i, j: (j, i, 0))],
        out_shape=[jax.ShapeDtypeStruct((T, D), F32), jax.ShapeDtypeStruct((T, D), BF16),
                   jax.ShapeDtypeStruct((NP, T, Fs), BF16), jax.ShapeDtypeStruct((NP, T, Fs), BF16)],
        scratch_shapes=[pltpu.VMEM((tm, D), F32)],
        operands=(x, nw, wg, wu, wd))


def _ffn_bwd_piece(j, h, g, u, wg, wu, wd, dout, dh_in, dws_in, name, side):
    T, D = h.shape
    NP, _, Fs = wg.shape
    tm = min(TM_FFN, T)
    n_in = 7 + (dh_in is not None) + (3 if dws_in else 0)

    def body(*refs):
        h_ref, g_ref, u_ref, wg_ref, wu_ref, wd_ref, do_ref = refs[:7]
        dhin_ref = refs[7] if dh_in is not None else None
        dh_ref, dwg_ref, dwu_ref, dwd_ref = refs[n_in:]

        @pl.when(pl.program_id(0) == 0)
        def _():
            dwg_ref[...] = jnp.zeros_like(dwg_ref)
            dwu_ref[...] = jnp.zeros_like(dwu_ref)
            dwd_ref[...] = jnp.zeros_like(dwd_ref)

        dob = (0.5 * do_ref[...]).astype(BF16)
        da = _split_rows(_dot_nt, dob, wd_ref[...])
        gf = g_ref[...].astype(F32)
        uf = u_ref[...].astype(F32)
        s = _sigmoid(gf)
        act = gf * s
        dg = (da * uf * _silu_grad(gf, s)).astype(BF16)
        du = (da * act).astype(BF16)
        a = (act * uf).astype(BF16)
        dh = _dot_nt(dg, wg_ref[...]) + _dot_nt(du, wu_ref[...])
        dh_ref[...] = dh if dhin_ref is None else dhin_ref[...] + dh
        hb = h_ref[...]
        dwg_ref[...] += _dot_tn(hb, dg)
        dwu_ref[...] += _dot_tn(hb, du)
        dwd_ref[...] += _dot_tn(a, dob)

    rows = pl.BlockSpec((tm, D), lambda i: (i, 0))
    piece = pl.BlockSpec((None, tm, Fs), lambda i: (j, i, 0))
    once = pl.Buffered(1)
    slot = lambda r, c: pl.BlockSpec((None, r, c), lambda i: (j, 0, 0), pipeline_mode=once)
    in_specs = [rows, piece, piece, slot(D, Fs), slot(D, Fs), slot(Fs, D), rows]
    operands = [h, g, u, wg, wu, wd, dout]
    aliases = {}
    if dh_in is not None:
        in_specs.append(rows)
        operands.append(dh_in)
    if dws_in:
        aliases = {len(operands) + k: 1 + k for k in range(3)}
        in_specs += [HBM_SPEC] * 3
        operands += list(dws_in)
    return _call_hosting(
        body, side, name=name, grid=(T // tm,), in_specs=in_specs,
        out_specs=[rows, slot(D, Fs), slot(D, Fs), slot(Fs, D)],
        out_shape=[jax.ShapeDtypeStruct((T, D), F32), jax.ShapeDtypeStruct((NP, D, Fs), F32),
                   jax.ShapeDtypeStruct((NP, D, Fs), F32), jax.ShapeDtypeStruct((NP, Fs, D), F32)],
        scratch_shapes=[], aliases=aliases, operands=tuple(operands))


def ffn_bwd(h, g, u, wg, wu, wd, dout, name, side=None):
    dh, dws, extra = None, None, []
    for j in range(wg.shape[0]):
        dh, *rest = _ffn_bwd_piece(j, h, g, u, wg, wu, wd, dout, dh, dws, f"{name}_{j}", side if j == 0 else None)
        dws, extra = rest[:3], (rest[3:] if j == 0 else extra)
    return (dh, *dws, *extra)


def rmsnorm_bwd(x, nw, dh, dres, name):
    T, D = x.shape
    tm = min(TM_ROW, T)

    def body(x_ref, nw_ref, dh_ref, dr_ref, dx_ref, dnw_ref):
        @pl.when(pl.program_id(0) == 0)
        def _():
            dnw_ref[...] = jnp.zeros_like(dnw_ref)

        dxn, dnw = _rms_bwd(x_ref[...], nw_ref[...], dh_ref[...])
        dx_ref[...] = dr_ref[...] + dxn
        dnw_ref[...] += dnw

    rows = pl.BlockSpec((tm, D), lambda i: (i, 0))
    return pl.pallas_call(
        body, name=name, grid=(T // tm,),
        in_specs=[rows, pl.BlockSpec((1, D), lambda i: (0, 0)), rows, rows],
        out_specs=[rows, pl.BlockSpec((1, D), lambda i: (0, 0))],
        out_shape=[jax.ShapeDtypeStruct((T, D), F32), jax.ShapeDtypeStruct((1, D), F32)],
        compiler_params=_params(("arbitrary",)),
    )(x, nw, dh, dres)


def ffn_bwd_x(x, nw, g, u, wg, wu, wd, dout, name, side=None):
    T, D = x.shape
    NP, _, Fs = wg.shape
    tm = min(TM_FFN, T)

    def body(x_ref, nw_ref, g_ref, u_ref, wg_ref, wu_ref, wd_ref, do_ref,
             dx_ref, dg_ref, du_ref, a_ref, dnw_ref, dh_ref, dob_ref):
        i = pl.program_id(0)
        j = pl.program_id(1)

        @pl.when((i == 0) & (j == 0))
        def _():
            dnw_ref[...] = jnp.zeros_like(dnw_ref)

        @pl.when(j == 0)
        def _():
            dh_ref[...] = jnp.zeros_like(dh_ref)
            dob_ref[...] = (0.5 * do_ref[...]).astype(BF16)

        da = _split_rows(_dot_nt, dob_ref[...], wd_ref[...])
        gf = g_ref[...].astype(F32)
        uf = u_ref[...].astype(F32)
        s = _sigmoid(gf)
        act = gf * s
        dg = (da * uf * _silu_grad(gf, s)).astype(BF16)
        du = (da * act).astype(BF16)
        dg_ref[...] = dg
        du_ref[...] = du
        a_ref[...] = (act * uf).astype(BF16)
        dh_ref[...] += _dot_nt(dg, wg_ref[...]) + _dot_nt(du, wu_ref[...])

        @pl.when(j == NP - 1)
        def _():
            dxn, dnw = _rms_bwd(x_ref[...], nw_ref[...], dh_ref[...])
            dx_ref[...] = do_ref[...] + dxn
            dnw_ref[...] += dnw

    return _call_hosting(
        body, side, name=name, grid=(T // tm, NP),
        in_specs=[pl.BlockSpec((tm, D), lambda i, j: (i, 0)),
                  pl.BlockSpec((1, D), lambda i, j: (0, 0)),
                  pl.BlockSpec((None, tm, Fs), lambda i, j: (j, i, 0)),
                  pl.BlockSpec((None, tm, Fs), lambda i, j: (j, i, 0)),
                  pl.BlockSpec((None, D, Fs), lambda i, j: (j, 0, 0)),
                  pl.BlockSpec((None, D, Fs), lambda i, j: (j, 0, 0)),
                  pl.BlockSpec((None, Fs, D), lambda i, j: (j, 0, 0)),
                  pl.BlockSpec((tm, D), lambda i, j: (i, 0))],
        out_specs=[pl.BlockSpec((tm, D), lambda i, j: (i, 0)),
                   pl.BlockSpec((None, tm, Fs), lambda i, j: (j, i, 0)),
                   pl.BlockSpec((None, tm, Fs), lambda i, j: (j, i, 0)),
                   pl.BlockSpec((None, tm, Fs), lambda i, j: (j, i, 0)),
                   pl.BlockSpec((1, D), lambda i, j: (0, 0))],
        out_shape=[jax.ShapeDtypeStruct((T, D), F32)] + [jax.ShapeDtypeStruct((NP, T, Fs), BF16)] * 3
                  + [jax.ShapeDtypeStruct((1, D), F32)],
        scratch_shapes=[pltpu.VMEM((tm, D), F32), pltpu.VMEM((tm, D), BF16)],
        operands=(x, nw, g, u, wg, wu, wd, dout))


def ffn_bwd_w(h, a, dg, du, dout, name):
    T, D = h.shape
    NP, _, Fs = a.shape
    tk = min(TK_TN, T)

    def body(h_ref, a_ref, dg_ref, du_ref, do_ref, dwg_ref, dwu_ref, dwd_ref):
        t = pl.program_id(1)

        @pl.when(t == 0)
        def _():
            dwg_ref[...] = jnp.zeros_like(dwg_ref)
            dwu_ref[...] = jnp.zeros_like(dwu_ref)
            dwd_ref[...] = jnp.zeros_like(dwd_ref)

        hb = h_ref[...]
        dwg_ref[...] += _dot_tn(hb, dg_ref[...])
        dwu_ref[...] += _dot_tn(hb, du_ref[...])
        dwd_ref[...] += _dot_tn(a_ref[...], (0.5 * do_ref[...]).astype(BF16))

    piece = pl.BlockSpec((None, tk, Fs), lambda j, t: (j, t, 0))
    return pl.pallas_call(
        body, name=name, grid=(NP, T // tk),
        in_specs=[pl.BlockSpec((tk, D), lambda j, t: (t, 0)), piece, piece, piece,
                  pl.BlockSpec((tk, D), lambda j, t: (t, 0))],
        out_specs=[pl.BlockSpec((None, D, Fs), lambda j, t: (j, 0, 0)),
                   pl.BlockSpec((None, D, Fs), lambda j, t: (j, 0, 0)),
                   pl.BlockSpec((None, Fs, D), lambda j, t: (j, 0, 0))],
        out_shape=[jax.ShapeDtypeStruct((NP, D, Fs), F32), jax.ShapeDtypeStruct((NP, D, Fs), F32),
                   jax.ShapeDtypeStruct((NP, Fs, D), F32)],
        compiler_params=_params(("parallel", "arbitrary")),
    )(h, a, dg, du, dout)


def mix_in_fwd(x, nw, w_glu, w_qkv, w_gate, cs, sn, name):
    T, D = x.shape
    KV = (w_qkv.shape[1] - D) // 2
    tm = min(TM_ROW, T)

    def body(x_ref, nw_ref, wa_ref, wq_ref, wg_ref, cs_ref, sn_ref, h_ref, pa_ref, pg_ref, q_ref, k_ref, v_ref):
        xv = x_ref[...]
        h = (xv * _rms_scale(xv) * nw_ref[...]).astype(BF16)
        h_ref[...] = h
        pa_ref[...] = _dot(h, wa_ref[...])
        pg_ref[...] = _dot(h, wg_ref[...])
        qkv = _dot(h, wq_ref[...])
        cs_v, sn_v = cs_ref[...], sn_ref[...]
        q_ref[...] = _rope_chunks(qkv[:, :D], cs_v, sn_v, 1.0).astype(BF16)
        k_ref[...] = _rope_chunks(qkv[:, D:D + KV], cs_v, sn_v, 1.0).astype(BF16)
        v_ref[...] = qkv[:, D + KV:].astype(BF16)

    rows = lambda w: pl.BlockSpec((tm, w), lambda i: (i, 0))
    whole = lambda a: pl.BlockSpec(a.shape, lambda i: (0, 0))
    return pl.pallas_call(
        body, name=name, grid=(T // tm,),
        in_specs=[rows(D), whole(nw), whole(w_glu), whole(w_qkv), whole(w_gate), rows(128), rows(128)],
        out_specs=[rows(D), rows(2 * D), rows(2 * D), rows(D), rows(KV), rows(KV)],
        out_shape=[jax.ShapeDtypeStruct((T, D), BF16), jax.ShapeDtypeStruct((T, 2 * D), F32),
                   jax.ShapeDtypeStruct((T, 2 * D), F32), jax.ShapeDtypeStruct((T, D), BF16),
                   jax.ShapeDtypeStruct((T, KV), BF16), jax.ShapeDtypeStruct((T, KV), BF16)],
        compiler_params=_params(("parallel",)),
    )(x, nw, w_glu, w_qkv, w_gate, cs, sn)


def matmul_tn(lhs, rhs, name):
    T, K = lhs.shape
    N = rhs.shape[1]
    tk = min(TK_TN, T)

    def body(l_ref, r_ref, o_ref):
        @pl.when(pl.program_id(0) == 0)
        def _():
            o_ref[...] = jnp.zeros_like(o_ref)

        o_ref[...] += _dot_tn(l_ref[...].astype(BF16), r_ref[...].astype(BF16))

    return pl.pallas_call(
        body, name=name, grid=(T // tk,),
        in_specs=[pl.BlockSpec((tk, K), lambda t: (t, 0)), pl.BlockSpec((tk, N), lambda t: (t, 0))],
        out_specs=pl.BlockSpec((K, N), lambda t: (0, 0)),
        out_shape=jax.ShapeDtypeStruct((K, N), F32),
        compiler_params=_params(("arbitrary",)),
    )(lhs, rhs)


def mix_in_bwd(dps, ws, x, nw, dres, name):
    T, D = x.shape
    tm = min(TM_ROW, T)
    n = len(dps)

    def body(*refs):
        dp_refs, w_refs = refs[:n], refs[n:2 * n]
        x_ref, nw_ref, dr_ref, dx_ref, dnw_ref = refs[2 * n:]

        @pl.when(pl.program_id(0) == 0)
        def _():
            dnw_ref[...] = jnp.zeros_like(dnw_ref)

        dh = _dot_nt(dp_refs[0][...], w_refs[0][...])
        for k in range(1, n):
            dh += _dot_nt(dp_refs[k][...], w_refs[k][...])
        dxn, dnw = _rms_bwd(x_ref[...], nw_ref[...], dh)
        dx_ref[...] = dr_ref[...] + dxn
        dnw_ref[...] += dnw

    in_specs = [pl.BlockSpec((tm, dp.shape[1]), lambda i: (i, 0)) for dp in dps]
    in_specs += [pl.BlockSpec(w.shape, lambda i: (0, 0)) for w in ws]
    in_specs += [pl.BlockSpec((tm, D), lambda i: (i, 0)), pl.BlockSpec((1, D), lambda i: (0, 0)),
                 pl.BlockSpec((tm, D), lambda i: (i, 0))]
    return pl.pallas_call(
        body, name=name, grid=(T // tm,), in_specs=in_specs,
        out_specs=[pl.BlockSpec((tm, D), lambda i: (i, 0)), pl.BlockSpec((1, D), lambda i: (0, 0))],
        out_shape=[jax.ShapeDtypeStruct((T, D), F32), jax.ShapeDtypeStruct((1, D), F32)],
        compiler_params=_params(("arbitrary",)),
    )(*dps, *ws, x, nw, dres)


def _layernorm_stats(c1):
    mu = jnp.mean(c1, axis=-1, keepdims=True)
    xc = c1 - mu
    rstd = lax.rsqrt(jnp.mean(xc * xc, axis=-1, keepdims=True) + LN_EPS)
    return xc * rstd, rstd


def _shifted_copies(src_ref, dst_ref):
    rows = dst_ref.shape[1]
    for b in range(1, 8):
        dst_ref[b - 1] = src_ref[pl.ds(b, rows), :]


def _shifted_rows(src_ref, shifted_ref, start, rows, cols):
    a8, b = divmod(start, 8)
    if b == 0:
        return src_ref[pl.ds(8 * a8, rows), cols]
    return shifted_ref[b - 1, pl.ds(8 * a8, rows), cols]


def conv_fwd(p_glu, dw_w, dw_b, ln_g, ln_b, name):
    T, D2 = p_glu.shape
    D = D2 // 2
    tm = min(TM_ROW, T)
    hb = tm // CONV_HALO

    def body(a_ref, b_ref, ah_ref, bh_ref, w_ref, wb_ref, g_ref, be_ref, c1_ref, c3_ref, e_ref, es_ref):
        i = pl.program_id(0)
        halo = ah_ref[...] * _sigmoid(bh_ref[...])
        e_ref[pl.ds(0, CONV_HALO), :] = jnp.where(i > 0, halo, 0.0)
        e_ref[pl.ds(CONV_HALO, tm), :] = a_ref[...] * _sigmoid(b_ref[...])
        _shifted_copies(e_ref, es_ref)
        off = CONV_HALO - (CONV_WIDTH - 1)

        def strip(s, carry):
            cols = pl.ds(pl.multiple_of(s * 128, 128), 128)
            acc = jnp.zeros((tm, 128), F32) + wb_ref[:, cols]
            for k in range(CONV_WIDTH):
                acc += w_ref[pl.ds(k, 1), cols] * _shifted_rows(e_ref, es_ref, off + k, tm, cols)
            c1_ref[:, cols] = acc
            return carry

        lax.fori_loop(0, D // 128, strip, 0)
        xhat, _ = _layernorm_stats(c1_ref[...])
        c2 = xhat * g_ref[...] + be_ref[...]
        c3_ref[...] = (c2 * _sigmoid(c2)).astype(BF16)

    row = pl.BlockSpec((1, D), lambda i: (0, 0))
    return pl.pallas_call(
        body, name=name, grid=(T // tm,),
        in_specs=[pl.BlockSpec((tm, D), lambda i: (i, 0)), pl.BlockSpec((tm, D), lambda i: (i, 1)),
                  pl.BlockSpec((CONV_HALO, D), lambda i: (jnp.maximum(i * hb - 1, 0), 0)),
                  pl.BlockSpec((CONV_HALO, D), lambda i: (jnp.maximum(i * hb - 1, 0), 1)),
                  pl.BlockSpec((CONV_HALO, D), lambda i: (0, 0)), row, row, row],
        out_specs=[pl.BlockSpec((tm, D), lambda i: (i, 0)), pl.BlockSpec((tm, D), lambda i: (i, 0))],
        out_shape=[jax.ShapeDtypeStruct((T, D), F32), jax.ShapeDtypeStruct((T, D), BF16)],
        scratch_shapes=[pltpu.VMEM((tm + CONV_HALO, D), F32), pltpu.VMEM((7, tm + CONV_HALO - 8, D), F32)],
        compiler_params=_params(("parallel",)),
    )(p_glu, p_glu, p_glu, p_glu, dw_w, dw_b, ln_g, ln_b)


def conv_bwd(p_glu, dc1, dw_w, name, side=None):
    T, D2 = p_glu.shape
    D = D2 // 2
    tm = min(TM_ROW, T)
    hb = tm // CONV_HALO
    last = T // CONV_HALO - 1
    nblk = T // tm

    def body(a_ref, b_ref, ah_ref, bh_ref, d_ref, dn_ref, w_ref, dp_ref, dw_ref, e_ref, f_ref, es_ref, fs_ref):
        i = pl.program_id(0)

        @pl.when(i == 0)
        def _():
            dw_ref[...] = jnp.zeros_like(dw_ref)

        halo = ah_ref[...] * _sigmoid(bh_ref[...])
        e_ref[pl.ds(0, CONV_HALO), :] = jnp.where(i > 0, halo, 0.0)
        e_ref[pl.ds(CONV_HALO, tm), :] = a_ref[...] * _sigmoid(b_ref[...])
        f_ref[pl.ds(0, tm), :] = d_ref[...]
        f_ref[pl.ds(tm, CONV_HALO), :] = jnp.where(i < nblk - 1, dn_ref[...], 0.0)
        _shifted_copies(e_ref, es_ref)
        _shifted_copies(f_ref, fs_ref)
        off = CONV_HALO - (CONV_WIDTH - 1)

        def strip(s, carry):
            cols = pl.ds(pl.multiple_of(s * 128, 128), 128)
            d = d_ref[:, cols]
            dc0 = jnp.zeros((tm, 128), F32)
            for k in range(CONV_WIDTH):
                dw_ref[pl.ds(k, 1), cols] += jnp.sum(d * _shifted_rows(e_ref, es_ref, off + k, tm, cols),
                                                     axis=0, keepdims=True)
                dc0 += w_ref[pl.ds(k, 1), cols] * _shifted_rows(f_ref, fs_ref, CONV_WIDTH - 1 - k, tm, cols)
            a = a_ref[:, cols]
            sb = _sigmoid(b_ref[:, cols])
            dp_ref[:, cols] = (dc0 * sb).astype(BF16)
            dp_ref[:, pl.ds(pl.multiple_of(D + s * 128, 128), 128)] = (dc0 * a * sb * (1.0 - sb)).astype(BF16)
            return carry

        lax.fori_loop(0, D // 128, strip, 0)

    return _call_hosting(
        body, side, name=name, grid=(nblk,),
        in_specs=[pl.BlockSpec((tm, D), lambda i: (i, 0)), pl.BlockSpec((tm, D), lambda i: (i, 1)),
                  pl.BlockSpec((CONV_HALO, D), lambda i: (jnp.maximum(i * hb - 1, 0), 0)),
                  pl.BlockSpec((CONV_HALO, D), lambda i: (jnp.maximum(i * hb - 1, 0), 1)),
                  pl.BlockSpec((tm, D), lambda i: (i, 0)),
                  pl.BlockSpec((CONV_HALO, D), lambda i: (jnp.minimum((i + 1) * hb, last), 0)),
                  pl.BlockSpec((CONV_HALO, D), lambda i: (0, 0))],
        out_specs=[pl.BlockSpec((tm, D2), lambda i: (i, 0)), pl.BlockSpec((CONV_HALO, D), lambda i: (0, 0))],
        out_shape=[jax.ShapeDtypeStruct((T, D2), BF16), jax.ShapeDtypeStruct((CONV_HALO, D), F32)],
        scratch_shapes=[pltpu.VMEM((tm + CONV_HALO, D), F32), pltpu.VMEM((tm + CONV_HALO, D), F32),
                        pltpu.VMEM((7, tm + CONV_HALO - 8, D), F32), pltpu.VMEM((7, tm + CONV_HALO - 8, D), F32)],
        operands=(p_glu, p_glu, p_glu, p_glu, dc1, dc1, dw_w))


def _rot_half(x):
    lane = lax.broadcasted_iota(jnp.int32, x.shape, 1)
    first = (lane % HEAD_DIM) < HEAD_DIM // 2
    return jnp.where(first, pltpu.roll(x, 128 - HEAD_DIM // 2, 1), pltpu.roll(x, HEAD_DIM // 2, 1))


def _rope_chunks(x, cs, sn, sign):
    outs = []
    for c in range(x.shape[1] // 128):
        xc = x[:, c * 128:(c + 1) * 128]
        outs.append(xc * cs + sign * (_rot_half(xc) * sn))
    return outs[0] if len(outs) == 1 else jnp.concatenate(outs, axis=1)


def rope_bwd(dq, dk, dv, cs, sn, name):
    T, D = dq.shape
    KV = dk.shape[1]
    tm = min(TM_ROW, T)

    def body(dq_ref, dk_ref, dv_ref, cs_ref, sn_ref, o_ref):
        cs_v, sn_v = cs_ref[...], sn_ref[...]
        o_ref[:, pl.ds(0, D)] = _rope_chunks(dq_ref[...], cs_v, sn_v, -1.0).astype(BF16)
        o_ref[:, pl.ds(D, KV)] = _rope_chunks(dk_ref[...], cs_v, sn_v, -1.0).astype(BF16)
        o_ref[:, pl.ds(D + KV, KV)] = dv_ref[...].astype(BF16)

    tab = pl.BlockSpec((tm, 128), lambda i: (i, 0))
    return pl.pallas_call(
        body, name=name, grid=(T // tm,),
        in_specs=[pl.BlockSpec((tm, D), lambda i: (i, 0)), pl.BlockSpec((tm, KV), lambda i: (i, 0)),
                  pl.BlockSpec((tm, KV), lambda i: (i, 0)), tab, tab],
        out_specs=pl.BlockSpec((tm, D + 2 * KV), lambda i: (i, 0)),
        out_shape=jax.ShapeDtypeStruct((T, D + 2 * KV), BF16),
        compiler_params=_params(("parallel",)),
    )(dq, dk, dv, cs, sn)


def _lane_lo():
    return lax.broadcasted_iota(jnp.int32, (1, 128), 1) < HEAD_DIM


def _band_mask(i, reps):
    shape = (reps * WINDOW, 2 * WINDOW)
    qi = lax.broadcasted_iota(jnp.int32, shape, 0) % WINDOW
    cj = lax.broadcasted_iota(jnp.int32, shape, 1)
    rel = qi - cj + WINDOW
    return (rel >= 0) & (rel < WINDOW) & ((i > 0) | (cj >= WINDOW))


def _stack_pairs(ref, first, n):
    parts = [ref[:, pl.ds((first + p) * 128, 128)] for p in range(n)]
    return parts[0] if n == 1 else jnp.concatenate(parts, axis=0)


def _pair_rows(n):
    return lax.broadcasted_iota(jnp.int32, (n * WINDOW, 1), 0) // WINDOW


def _per_pair_column(values, n):
    rows = _pair_rows(n)
    col = jnp.zeros((n * WINDOW, 1), F32) + values[0]
    for p in range(1, n):
        col = jnp.where(rows == p, values[p], col)
    return col


def _kv_lo_hi(x2, g):
    pair, half = divmod(g, 2)
    lo = _lane_lo()
    xg = x2[:, pair * 128:(pair + 1) * 128].astype(F32)
    xg = jnp.where(lo if half == 0 else ~lo, xg, 0.0)
    sw = pltpu.roll(xg, HEAD_DIM, 1)
    x_lo, x_hi = (xg, sw) if half == 0 else (sw, xg)
    return x_lo.astype(BF16), x_hi.astype(BF16)


def _softmax_sink(s, allowed, sink):
    s = jnp.where(allowed, s * (HEAD_DIM ** -0.5), NEG_INF)
    m = jnp.maximum(jnp.max(s, axis=-1, keepdims=True), sink)
    p = jnp.exp(s - m)
    es = jnp.exp(sink - m)
    inv = 1.0 / (jnp.sum(p, axis=-1, keepdims=True) + es)
    return p * inv, es * inv


def attn_fwd(qr, kr, vb, sinks, name):
    T, D = qr.shape
    KV = kr.shape[1]
    n_kv = KV // HEAD_DIM
    group = (D // HEAD_DIM) // n_kv
    nb = T // WINDOW

    npair = group // 2

    def body(sink_ref, q_ref, kp_ref, kc_ref, vp_ref, vc_ref, o_ref):
        i = pl.program_id(0)
        allowed = _band_mask(i, npair)
        k2 = jnp.concatenate([kp_ref[...], kc_ref[...]], axis=0)
        v2 = jnp.concatenate([vp_ref[...], vc_ref[...]], axis=0)
        outs = [None] * (D // 128)
        for g in range(n_kv):
            k_lo, k_hi = _kv_lo_hi(k2, g)
            v_lo, v_hi = _kv_lo_hi(v2, g)
            first = (g * group) // 2
            q = _stack_pairs(q_ref, first, npair)
            sink_e = _per_pair_column([sink_ref[0, g * group + 2 * p] for p in range(npair)], npair)
            sink_o = _per_pair_column([sink_ref[0, g * group + 2 * p + 1] for p in range(npair)], npair)
            pe, _ = _softmax_sink(_dot_nt(q, k_lo), allowed, sink_e)
            po, _ = _softmax_sink(_dot_nt(q, k_hi), allowed, sink_o)
            o = _dot(pe.astype(BF16), v_lo) + _dot(po.astype(BF16), v_hi)
            for p in range(npair):
                outs[first + p] = o[p * WINDOW:(p + 1) * WINDOW]
        o_ref[...] = jnp.concatenate(outs, axis=1).astype(BF16)

    prev = lambda i: (jnp.maximum(i - 1, 0), 0)
    cur = lambda i: (i, 0)
    return pl.pallas_call(
        body, name=name, grid=(nb,),
        in_specs=[pl.BlockSpec(memory_space=pltpu.SMEM),
                  pl.BlockSpec((WINDOW, D), cur),
                  pl.BlockSpec((WINDOW, KV), prev), pl.BlockSpec((WINDOW, KV), cur),
                  pl.BlockSpec((WINDOW, KV), prev), pl.BlockSpec((WINDOW, KV), cur)],
        out_specs=pl.BlockSpec((WINDOW, D), cur),
        out_shape=jax.ShapeDtypeStruct((T, D), BF16),
        compiler_params=_params(("parallel",)),
    )(sinks, qr, kr, kr, vb, vb)


def attn_bwd(qr, kr, vb, o, do, sinks, name):
    T, D = qr.shape
    KV = kr.shape[1]
    n_heads = D // HEAD_DIM
    n_kv = KV // HEAD_DIM
    group = n_heads // n_kv
    nb = T // WINDOW
    npair = group // 2
    scale = HEAD_DIM ** -0.5

    def body(sink_ref, q_ref, kp_ref, kc_ref, vp_ref, vc_ref, o_ref, do_ref,
             dq_ref, dk_ref, dv_ref, ds_ref, ck_ref, cv_ref):
        i = pl.program_id(0)
        lo = _lane_lo()

        @pl.when(i == 0)
        def _():
            ck_ref[...] = jnp.zeros_like(ck_ref)
            cv_ref[...] = jnp.zeros_like(cv_ref)
            ds_ref[...] = jnp.zeros_like(ds_ref)

        @pl.when(i < nb)
        def _():
            allowed = _band_mask(i, npair)
            rows = _pair_rows(npair)
            k2 = jnp.concatenate([kp_ref[...], kc_ref[...]], axis=0)
            v2 = jnp.concatenate([vp_ref[...], vc_ref[...]], axis=0)
            lane = lax.broadcasted_iota(jnp.int32, (1, 128), 1)
            dsink = jnp.zeros((1, 128), F32)
            dq_out = [None] * (D // 128)
            dk_pairs = [jnp.zeros((2 * WINDOW, 128), F32) for _ in range(KV // 128)]
            dv_pairs = [jnp.zeros((2 * WINDOW, 128), F32) for _ in range(KV // 128)]
            for g in range(n_kv):
                k_lo, k_hi = _kv_lo_hi(k2, g)
                v_lo, v_hi = _kv_lo_hi(v2, g)
                first = (g * group) // 2
                q = _stack_pairs(q_ref, first, npair)
                dop = _stack_pairs(do_ref, first, npair)
                dd = dop.astype(F32) * _stack_pairs(o_ref, first, npair).astype(F32)
                dq = jnp.zeros((npair * WINDOW, 128), F32)
                dkg = jnp.zeros((2 * WINDOW, 128), F32)
                dvg = jnp.zeros((2 * WINDOW, 128), F32)
                for parity, k_h, v_h, sel in ((0, k_lo, v_lo, lo), (1, k_hi, v_hi, ~lo)):
                    heads = [g * group + 2 * p + parity for p in range(npair)]
                    sink = _per_pair_column([sink_ref[0, h] for h in heads], npair)
                    p_, ps = _softmax_sink(_dot_nt(q, k_h), allowed, sink)
                    delta = jnp.sum(jnp.where(sel, dd, 0.0), axis=-1, keepdims=True)
                    dsc = (p_ * (_dot_nt(dop, v_h) - delta)).astype(BF16)
                    sd = -ps * delta
                    for p, h in enumerate(heads):
                        dsink += jnp.where(lane == h, jnp.sum(jnp.where(rows == p, sd, 0.0)), 0.0)
                    dq += _dot(dsc, k_h)
                    dkg += jnp.where(sel, _dot_tn(dsc, q), 0.0)
                    dvg += jnp.where(sel, _dot_tn(p_.astype(BF16), dop), 0.0)
                for p in range(npair):
                    dq_out[first + p] = dq[p * WINDOW:(p + 1) * WINDOW]
                pair, half = divmod(g, 2)
                keep = lo if half == 0 else ~lo
                dk_pairs[pair] += jnp.where(keep, dkg + pltpu.roll(dkg, HEAD_DIM, 1), 0.0) * scale
                dv_pairs[pair] += jnp.where(keep, dvg + pltpu.roll(dvg, HEAD_DIM, 1), 0.0)
            dq_ref[...] = jnp.concatenate(dq_out, axis=1) * scale
            dk2 = dk_pairs[0] if len(dk_pairs) == 1 else jnp.concatenate(dk_pairs, axis=1)
            dv2 = dv_pairs[0] if len(dv_pairs) == 1 else jnp.concatenate(dv_pairs, axis=1)
            dk_ref[...] = ck_ref[...] + dk2[:WINDOW]
            dv_ref[...] = cv_ref[...] + dv2[:WINDOW]
            ck_ref[...] = dk2[WINDOW:]
            cv_ref[...] = dv2[WINDOW:]
            ds_ref[pl.ds(0, 1), :] += dsink

        @pl.when(i == nb)
        def _():
            dk_ref[...] = ck_ref[...]
            dv_ref[...] = cv_ref[...]

    prev = lambda i: (jnp.maximum(i - 1, 0), 0)
    cur = lambda i: (jnp.minimum(i, nb - 1), 0)
    prevc = lambda i: (jnp.maximum(jnp.minimum(i, nb - 1) - 1, 0), 0)
    return pl.pallas_call(
        body, name=name, grid=(nb + 1,),
        in_specs=[pl.BlockSpec(memory_space=pltpu.SMEM),
                  pl.BlockSpec((WINDOW, D), cur),
                  pl.BlockSpec((WINDOW, KV), prevc), pl.BlockSpec((WINDOW, KV), cur),
                  pl.BlockSpec((WINDOW, KV), prevc), pl.BlockSpec((WINDOW, KV), cur),
                  pl.BlockSpec((WINDOW, D), cur), pl.BlockSpec((WINDOW, D), cur)],
        out_specs=[pl.BlockSpec((WINDOW, D), cur), pl.BlockSpec((WINDOW, KV), prev),
                   pl.BlockSpec((WINDOW, KV), prev), pl.BlockSpec((8, 128), lambda i: (0, 0))],
        out_shape=[jax.ShapeDtypeStruct((T, D), F32), jax.ShapeDtypeStruct((T, KV), F32),
                   jax.ShapeDtypeStruct((T, KV), F32), jax.ShapeDtypeStruct((8, 128), F32)],
        scratch_shapes=[pltpu.VMEM((WINDOW, KV), F32), pltpu.VMEM((WINDOW, KV), F32)],
        compiler_params=_params(("arbitrary",)),
    )(sinks, qr, kr, kr, vb, vb, o, do)


def merge_fwd(x, c3, o, p_gate, gate_b, w_proj, w_o, w_out, name):
    T, D = x.shape
    tm = min(TM_ROW, T)

    def body(x_ref, c3_ref, o_ref, gc_ref, ga_ref, bc_ref, ba_ref, wp_ref, wo_ref, wout_ref,
             xo_ref, co_ref, ao_ref, mg_ref):
        conv_out = _dot(c3_ref[...], wp_ref[...])
        attn_out = _dot(o_ref[...], wo_ref[...])
        merged = (_sigmoid(gc_ref[...] + bc_ref[...]) * conv_out
                  + _sigmoid(ga_ref[...] + ba_ref[...]) * attn_out).astype(BF16)
        co_ref[...] = conv_out.astype(BF16)
        ao_ref[...] = attn_out.astype(BF16)
        mg_ref[...] = merged
        xo_ref[...] = x_ref[...] + _dot(merged, wout_ref[...])

    blk = lambda j: pl.BlockSpec((tm, D), lambda i: (i, j))
    row = lambda j: pl.BlockSpec((1, D), lambda i: (0, j))
    mat = pl.BlockSpec((D, D), lambda i: (0, 0))
    return pl.pallas_call(
        body, name=name, grid=(T // tm,),
        in_specs=[blk(0), blk(0), blk(0), blk(0), blk(1), row(0), row(1), mat, mat, mat],
        out_specs=[blk(0), blk(0), blk(0), blk(0)],
        out_shape=[jax.ShapeDtypeStruct((T, D), F32)] + [jax.ShapeDtypeStruct((T, D), BF16)] * 3,
        compiler_params=_params(("parallel",)),
    )(x, c3, o, p_gate, p_gate, gate_b, gate_b, w_proj, w_o, w_out)


def merge_bwd(dx, p_gate, gate_b, conv_out, attn_out, c1, ln_g, ln_b, w_proj, w_o, w_out, name, side=None):
    T, D = dx.shape
    tm = min(TM_ROW, T)

    def body(dx_ref, gc_ref, ga_ref, bc_ref, ba_ref, co_ref, ao_ref, c1_ref, g_ref, be_ref,
             wp_ref, wo_ref, wout_ref, dgt_ref, dco_ref, dao_ref, do_ref, dc1_ref, sm_ref):
        @pl.when(pl.program_id(0) == 0)
        def _():
            sm_ref[...] = jnp.zeros_like(sm_ref)

        dm = _dot_nt(dx_ref[...].astype(BF16), wout_ref[...])
        sc = _sigmoid(gc_ref[...] + bc_ref[...])
        sa = _sigmoid(ga_ref[...] + ba_ref[...])
        dco = (dm * sc).astype(BF16)
        dao = (dm * sa).astype(BF16)
        dgc = dm * co_ref[...].astype(F32) * sc * (1.0 - sc)
        dga = dm * ao_ref[...].astype(F32) * sa * (1.0 - sa)
        dgt_ref[:, pl.ds(0, D)] = dgc.astype(BF16)
        dgt_ref[:, pl.ds(D, D)] = dga.astype(BF16)
        dco_ref[...] = dco
        dao_ref[...] = dao
        do_ref[...] = _dot_nt(dao, wo_ref[...]).astype(BF16)
        dc3 = _dot_nt(dco, wp_ref[...])
        xhat, rstd = _layernorm_stats(c1_ref[...])
        c2 = xhat * g_ref[...] + be_ref[...]
        dc2 = dc3 * _silu_grad(c2, _sigmoid(c2))
        dxh = dc2 * g_ref[...]
        dc1 = rstd * (dxh - jnp.mean(dxh, axis=-1, keepdims=True)
                      - xhat * jnp.mean(dxh * xhat, axis=-1, keepdims=True))
        dc1_ref[...] = dc1
        colsum = lambda v: jnp.sum(v, axis=0, keepdims=True)
        for r, (left, right) in enumerate(((dgc, dga), (dc2 * xhat, dc2), (dc1, None))):
            sm_ref[pl.ds(r, 1), pl.ds(0, D)] += colsum(left)
            if right is not None:
                sm_ref[pl.ds(r, 1), pl.ds(D, D)] += colsum(right)

    blk = lambda j: pl.BlockSpec((tm, D), lambda i: (i, j))
    row = lambda j: pl.BlockSpec((1, D), lambda i: (0, j))
    mat = pl.BlockSpec((D, D), lambda i: (0, 0))
    return _call_hosting(
        body, side, name=name, grid=(T // tm,),
        in_specs=[blk(0), blk(0), blk(1), row(0), row(1), blk(0), blk(0), blk(0), row(0), row(0), mat, mat, mat],
        out_specs=[pl.BlockSpec((tm, 2 * D), lambda i: (i, 0)), blk(0), blk(0), blk(0), blk(0),
                   pl.BlockSpec((8, 2 * D), lambda i: (0, 0))],
        out_shape=[jax.ShapeDtypeStruct((T, 2 * D), BF16)] + [jax.ShapeDtypeStruct((T, D), BF16)] * 3
                  + [jax.ShapeDtypeStruct((T, D), F32), jax.ShapeDtypeStruct((8, 2 * D), F32)],
        scratch_shapes=[],
        operands=(dx, p_gate, p_gate, gate_b, gate_b, conv_out, attn_out, c1, ln_g, ln_b, w_proj, w_o, w_out))


def loss_head(x, nw, target, name):
    T, D = x.shape
    tm = min(TM_ROW, T)

    def body(x_ref, nw_ref, t_ref, dx_ref, sm_ref):
        @pl.when(pl.program_id(0) == 0)
        def _():
            sm_ref[...] = jnp.zeros_like(sm_ref)

        xv = x_ref[...]
        err = xv * _rms_scale(xv) * nw_ref[...] - t_ref[...]
        loss = 0.5 * jnp.sum(jnp.mean(err * err, axis=-1, keepdims=True))
        dxn, dnw = _rms_bwd(xv, nw_ref[...], err * (1.0 / D))
        dx_ref[...] = dxn
        sm_ref[pl.ds(0, 1), :] += dnw
        sm_ref[pl.ds(1, 1), :] += jnp.zeros((1, D), F32) + loss

    return pl.pallas_call(
        body, name=name, grid=(T // tm,),
        in_specs=[pl.BlockSpec((tm, D), lambda i: (i, 0)), pl.BlockSpec((1, D), lambda i: (0, 0)),
                  pl.BlockSpec((tm, D), lambda i: (i, 0))],
        out_specs=[pl.BlockSpec((tm, D), lambda i: (i, 0)), pl.BlockSpec((8, D), lambda i: (0, 0))],
        out_shape=[jax.ShapeDtypeStruct((T, D), F32), jax.ShapeDtypeStruct((8, D), F32)],
        compiler_params=_params(("arbitrary",)),
    )(x, nw, target)


def adamw(w, g, m, v, name):
    R, C = w.shape
    tr = _row_tile(R, TR_ELT)

    def body(w_ref, g_ref, m_ref, v_ref, d_ref, mo_ref, vo_ref):
        gv = g_ref[...]
        mn = ADAM_B1 * m_ref[...] + (1.0 - ADAM_B1) * gv
        vn = ADAM_B2 * v_ref[...] + (1.0 - ADAM_B2) * (gv * gv)
        m_hat = mn / (1.0 - ADAM_B1 ** ADAM_STEP)
        v_hat = vn / (1.0 - ADAM_B2 ** ADAM_STEP)
        d_ref[...] = -ADAM_LR * (m_hat / (jnp.sqrt(v_hat) + ADAM_EPS) + ADAM_WD * w_ref[...])
        mo_ref[...] = mn
        vo_ref[...] = vn

    spec = pl.BlockSpec((tr, C), lambda i: (i, 0))
    return pl.pallas_call(
        body, name=name, grid=(R // tr,), in_specs=[spec] * 4, out_specs=[spec] * 3,
        out_shape=[jax.ShapeDtypeStruct((R, C), F32)] * 3,
        compiler_params=_params(("parallel",)),
    )(w, g, m, v)


def _place():
    return lax.axis_index("x"), lax.axis_index("y"), lax.axis_index("c")


def place_shard(place, w, dtype, name):
    R, C = w.shape
    tr = _row_tile(R, TR_ELT)

    def body(pc_ref, w_ref, o_ref):
        o_ref[...] = w_ref[...].astype(dtype)

    return pl.pallas_call(
        body, name=name,
        grid_spec=pltpu.PrefetchScalarGridSpec(
            num_scalar_prefetch=1, grid=(R // tr,),
            in_specs=[pl.BlockSpec((tr, C), lambda r, pc: (r, 0))],
            out_specs=pl.BlockSpec((None, tr, C), lambda r, pc: (pc[0], r, 0))),
        out_shape=jax.ShapeDtypeStruct((N_CHIPS, R, C), dtype),
        compiler_params=_params(("arbitrary",)),
    )(place, w)


def gather_side(shards, small):
    n, ns = len(shards), len(small)

    def ici_copy(dst, sems, k, j, x, y, c, sending):
        px, py = x ^ (j >> 1), y ^ (j & 1)
        slot = 2 * x + y if sending else 2 * px + py
        half = dst[k].shape[1] // 2
        part = dst[k].at[slot, pl.ds(c * half, half)] if k < n else dst[k].at[slot]
        return pltpu.make_async_remote_copy(part, part, sems[0].at[3 * k + j - 1], sems[1].at[3 * k + j - 1],
                                            device_id=(px, py, c), device_id_type=MESH)

    def d2d_copy(dst, sems, k, j, x, y, c, sending):
        half = dst[k].shape[1] // 2
        part = dst[k].at[2 * (x ^ (j >> 1)) + (y ^ (j & 1)), pl.ds((c if sending else 1 - c) * half, half)]
        return pltpu.make_async_remote_copy(part, part, sems[2].at[3 * k + j - 1], sems[3].at[3 * k + j - 1],
                                            device_id=(x, y, 1 - c), device_id_type=MESH)

    def start(src, dst, sems):
        x, y, c = _place()
        for k in range(n + ns):
            for j in (1, 2, 3):
                ici_copy(dst, sems, k, j, x, y, c, True).start()

    def finish(src, dst, sems):
        x, y, c = _place()
        for k in range(n + ns):
            for j in (1, 2, 3):
                ici_copy(dst, sems, k, j, x, y, c, False).wait_recv()
                if k < n:
                    d2d_copy(dst, sems, k, j, x, y, c, True).start()
        for k in range(n):
            for j in (1, 2, 3):
                d2d_copy(dst, sems, k, j, x, y, c, False).wait_recv()
        for k in range(n + ns):
            for j in (1, 2, 3):
                ici_copy(dst, sems, k, j, x, y, c, True).wait_send()
                if k < n:
                    d2d_copy(dst, sems, k, j, x, y, c, True).wait_send()

    arrays = list(shards) + list(small)
    return dict(inputs=arrays, out_shapes=[jax.ShapeDtypeStruct(a.shape, a.dtype) for a in arrays],
                aliases={k: k for k in range(n + ns)},
                sems=[pltpu.SemaphoreType.DMA((3 * (n + ns),)), pltpu.SemaphoreType.DMA((3 * (n + ns),)),
                      pltpu.SemaphoreType.DMA((3 * n,)), pltpu.SemaphoreType.DMA((3 * n,))],
                start=start, finish=finish)


def run_side(side, name):
    n_in, n_out = len(side["inputs"]), len(side["out_shapes"])

    def body(*refs):
        src, dst, sems = refs[:n_in], refs[n_in:n_in + n_out], refs[n_in + n_out:]
        side["start"](src, dst, sems)
        side["finish"](src, dst, sems)

    return pl.pallas_call(
        body, name=name, in_specs=[HBM_SPEC] * n_in, out_specs=[HBM_SPEC] * n_out,
        out_shape=side["out_shapes"], input_output_aliases=side["aliases"], scratch_shapes=side["sems"],
    )(*side["inputs"])


def allreduce_small(block):
    R, C = block.shape

    def body(x_ref, out_ref, all_ref, send_sems, recv_sems, local_sem):
        x, y, c = _place()
        me, sibling = (x, y, c), (x, y, 1 - c)
        chips = [(1 - x, y), (x, 1 - y), (1 - x, 1 - y)]

        def slot(px, py, pc):
            return all_ref.at[4 * px + 2 * py + pc]

        def copy(k, block_of, to, src=None):
            return pltpu.make_async_remote_copy(
                src_ref=slot(*block_of) if src is None else src, dst_ref=slot(*block_of),
                send_sem=send_sems.at[k], recv_sem=recv_sems.at[k], device_id=to, device_id_type=MESH)

        mine = pltpu.make_async_copy(x_ref, slot(*me), local_sem)
        mine.start()
        first = [copy(0, me, sibling, src=x_ref)]
        first += [copy(1 + j, me, (*chip, c), src=x_ref) for j, chip in enumerate(chips)]
        for cp in first:
            cp.start()
        passed = [copy(4 + j, (*chip, c), sibling) for j, chip in enumerate(chips)]
        for j, chip in enumerate(chips):
            copy(1 + j, (*chip, c), me).wait_recv()
            passed[j].start()
        copy(0, sibling, me).wait_recv()
        for j, chip in enumerate(chips):
            copy(4 + j, (*chip, 1 - c), me).wait_recv()
        for cp in first + passed:
            cp.wait_send()
        mine.wait()
        total = all_ref[0]
        for d in range(1, N_DEV):
            total = total + all_ref[d]
        out_ref[...] = total

    return pl.pallas_call(
        body, name="allreduce_small",
        in_specs=[pl.BlockSpec(memory_space=pltpu.VMEM)], out_specs=pl.BlockSpec(memory_space=pltpu.VMEM),
        out_shape=jax.ShapeDtypeStruct((R, C), F32),
        scratch_shapes=[pltpu.VMEM((N_DEV, R, C), F32), pltpu.SemaphoreType.DMA((7,)),
                        pltpu.SemaphoreType.DMA((7,)), pltpu.SemaphoreType.DMA],
        compiler_params=pltpu.CompilerParams(vmem_limit_bytes=VMEM_LIMIT),
    )(block)


def exchange_siblings_side(grads):
    n = len(grads)

    def copies(src, dst, sems):
        x, y, c = _place()
        for k in range(n):
            half = src[k].shape[1] // 2
            yield pltpu.make_async_remote_copy(src[k].at[:, pl.ds((1 - c) * half, half)], dst[k],
                                               sems[0].at[k], sems[1].at[k],
                                               device_id=(x, y, 1 - c), device_id_type=MESH)

    def start(src, dst, sems):
        for cp in copies(src, dst, sems):
            cp.start()

    def finish(src, dst, sems):
        for cp in copies(src, dst, sems):
            cp.wait()

    return dict(inputs=list(grads), aliases={},
                out_shapes=[jax.ShapeDtypeStruct((N_CHIPS, g.shape[1] // 2, g.shape[2]), F32) for g in grads],
                sems=[pltpu.SemaphoreType.DMA((n,)), pltpu.SemaphoreType.DMA((n,))], start=start, finish=finish)


def rs_chip_sum(place, grad, sib, name):
    NP, R, C = grad.shape
    half = R // 2
    tr = _row_tile(half, TR_ELT)
    nr = half // tr

    def body(pc_ref, g_ref, s_ref, wire_ref, own_ref):
        q = pl.program_id(1)
        total = g_ref[...] + s_ref[...]
        wire_ref[...] = total.astype(BF16)

        @pl.when(q == pc_ref[0])
        def _():
            own_ref[...] = total

    return pl.pallas_call(
        body, name=name,
        grid_spec=pltpu.PrefetchScalarGridSpec(
            num_scalar_prefetch=1, grid=(nr, NP),
            in_specs=[pl.BlockSpec((None, tr, C), lambda r, q, pc: (q, pc[1] * nr + r, 0)),
                      pl.BlockSpec((None, tr, C), lambda r, q, pc: (q, r, 0))],
            out_specs=[pl.BlockSpec((None, tr, C), lambda r, q, pc: (q, r, 0)),
                       pl.BlockSpec((tr, C), lambda r, q, pc: (r, 0))]),
        out_shape=[jax.ShapeDtypeStruct((NP, half, C), BF16), jax.ShapeDtypeStruct((half, C), F32)],
        compiler_params=_params(("arbitrary", "arbitrary")),
    )(place, grad, sib)


def exchange_chips_side(wires):
    n = len(wires)

    def copies(src, dst, sems):
        x, y, c = _place()
        for k in range(n):
            for j in (1, 2, 3):
                qx, qy = x ^ (j >> 1), y ^ (j & 1)
                yield pltpu.make_async_remote_copy(src[k].at[2 * qx + qy], dst[k].at[2 * x + y],
                                                   sems[0].at[3 * k + j - 1], sems[1].at[3 * k + j - 1],
                                                   device_id=(qx, qy, c), device_id_type=MESH)

    def start(src, dst, sems):
        for cp in copies(src, dst, sems):
            cp.start()

    def finish(src, dst, sems):
        for cp in copies(src, dst, sems):
            cp.wait()

    return dict(inputs=list(wires), out_shapes=[jax.ShapeDtypeStruct(w.shape, BF16) for w in wires], aliases={},
                sems=[pltpu.SemaphoreType.DMA((3 * n,)), pltpu.SemaphoreType.DMA((3 * n,))],
                start=start, finish=finish)


def rs_final_sum(place, own, got, name):
    NP, half, C = got.shape
    tr = _row_tile(half, TR_ELT)
    nr = half // tr

    def body(pc_ref, own_ref, g1_ref, g2_ref, g3_ref, out_ref):
        out_ref[...] = ((own_ref[...] + g1_ref[...].astype(F32)) + g2_ref[...].astype(F32)) + g3_ref[...].astype(F32)

    slot = lambda j: pl.BlockSpec((None, tr, C), lambda r, pc: (pc[0] ^ j, r, 0))
    return pl.pallas_call(
        body, name=name,
        grid_spec=pltpu.PrefetchScalarGridSpec(
            num_scalar_prefetch=1, grid=(nr,),
            in_specs=[pl.BlockSpec((tr, C), lambda r, pc: (r, 0)), slot(1), slot(2), slot(3)],
            out_specs=pl.BlockSpec((tr, C), lambda r, pc: (pc[1] * nr + r, 0))),
        out_shape=jax.ShapeDtypeStruct((2 * half, C), F32),
        compiler_params=_params(("arbitrary",)),
    )(place, own, got, got, got)


def rs_share_siblings(totals):
    n = len(totals)

    def body(*refs):
        dst = refs[n:2 * n]
        send_sems, recv_sems = refs[2 * n:]
        x, y, c = _place()
        copies = []
        for k in range(n):
            half = dst[k].shape[0] // 2
            rows = dst[k].at[pl.ds(c * half, half)]
            cp = pltpu.make_async_remote_copy(rows, rows, send_sems.at[k], recv_sems.at[k],
                                              device_id=(x, y, 1 - c), device_id_type=MESH)
            cp.start()
            copies.append(cp)
        for k, cp in enumerate(copies):
            cp.wait_send()
            half = dst[k].shape[0] // 2
            got = dst[k].at[pl.ds((1 - c) * half, half)]
            pltpu.make_async_remote_copy(got, got, send_sems.at[k], recv_sems.at[k],
                                         device_id=(x, y, c), device_id_type=MESH).wait_recv()

    return pl.pallas_call(
        body, name="rs_share_siblings",
        in_specs=[HBM_SPEC] * n, out_specs=[HBM_SPEC] * n,
        out_shape=[jax.ShapeDtypeStruct(t.shape, F32) for t in totals],
        input_output_aliases={k: k for k in range(n)},
        scratch_shapes=[pltpu.SemaphoreType.DMA((n,)), pltpu.SemaphoreType.DMA((n,))],
    )(*totals)


def rs_to_wires(place, grads, tag, sibs=None):
    if sibs is None:
        sibs = run_side(exchange_siblings_side(grads), f"rs_exchange_siblings_{tag}")
    wires, owns = [], []
    for k, (g, s) in enumerate(zip(grads, sibs)):
        w, o = rs_chip_sum(place, g, s, f"rs_chip_sum_{tag}{k}")
        wires.append(w)
        owns.append(o)
    return wires, owns


def rs_finish(place, owns, gots):
    totals = [rs_final_sum(place, o, g, f"rs_final_sum_{k}") for k, (o, g) in enumerate(zip(owns, gots))]
    return rs_share_siblings(totals)


def _rope_tables(positions):
    half = HEAD_DIM // 2
    inv_freq = ROPE_THETA ** (-jnp.arange(half, dtype=F32) / half)
    ang = positions.astype(F32)[:, None] * inv_freq
    cos, sin = jnp.cos(ang), jnp.sin(ang)
    return jnp.tile(cos, (1, 4)), jnp.concatenate([-sin, sin, -sin, sin], axis=1)


def _cols_from_pieces(pieces, start, stop):
    C = pieces.shape[2]
    parts = []
    for q in range(N_CHIPS):
        lo, hi = max(start, q * C), min(stop, (q + 1) * C)
        if lo < hi:
            parts.append(pieces[q][:, lo - q * C:hi - q * C])
    return parts[0] if len(parts) == 1 else jnp.concatenate(parts, axis=1)


def _pieces_from_groups(groups):
    C = sum(g.shape[1] for g in groups) // N_CHIPS
    pieces = []
    for q in range(N_CHIPS):
        parts, off = [], 0
        for g in groups:
            lo, hi = max(q * C, off), min((q + 1) * C, off + g.shape[1])
            if lo < hi:
                parts.append(g[:, lo - off:hi - off])
            off += g.shape[1]
        pieces.append(parts[0] if len(parts) == 1 else jnp.concatenate(parts, axis=1))
    return jnp.stack(pieces)


def kernel(x, positions, ffn1_norm, ffn1_w_gate, ffn1_w_up, ffn1_w_down, mix_norm, w_in, conv_dw_w, conv_dw_b, conv_ln_g, conv_ln_b, conv_w_proj, attn_sinks, attn_w_o, gate_b, w_out, ffn2_norm, ffn2_w_gate, ffn2_w_up, ffn2_w_down, final_norm, loss_target, m_ffn1_norm, m_ffn1_w_gate, m_ffn1_w_up, m_ffn1_w_down, m_mix_norm, m_w_in, m_conv_dw_w, m_conv_dw_b, m_conv_ln_g, m_conv_ln_b, m_conv_w_proj, m_attn_sinks, m_attn_w_o, m_gate_b, m_w_out, m_ffn2_norm, m_ffn2_w_gate, m_ffn2_w_up, m_ffn2_w_down, m_final_norm, v_ffn1_norm, v_ffn1_w_gate, v_ffn1_w_up, v_ffn1_w_down, v_mix_norm, v_w_in, v_conv_dw_w, v_conv_dw_b, v_conv_ln_g, v_conv_ln_b, v_conv_w_proj, v_attn_sinks, v_attn_w_o, v_gate_b, v_w_out, v_ffn2_norm, v_ffn2_w_gate, v_ffn2_w_up, v_ffn2_w_down, v_final_norm):
    weights = dict(ffn1_norm=ffn1_norm, ffn1_w_gate=ffn1_w_gate, ffn1_w_up=ffn1_w_up, ffn1_w_down=ffn1_w_down,
                   mix_norm=mix_norm, w_in=w_in, conv_dw_w=conv_dw_w, conv_dw_b=conv_dw_b, conv_ln_g=conv_ln_g,
                   conv_ln_b=conv_ln_b, conv_w_proj=conv_w_proj, attn_sinks=attn_sinks, attn_w_o=attn_w_o,
                   gate_b=gate_b, w_out=w_out, ffn2_norm=ffn2_norm, ffn2_w_gate=ffn2_w_gate, ffn2_w_up=ffn2_w_up,
                   ffn2_w_down=ffn2_w_down, final_norm=final_norm)
    m_in = dict(ffn1_norm=m_ffn1_norm, ffn1_w_gate=m_ffn1_w_gate, ffn1_w_up=m_ffn1_w_up, ffn1_w_down=m_ffn1_w_down,
                mix_norm=m_mix_norm, w_in=m_w_in, conv_dw_w=m_conv_dw_w, conv_dw_b=m_conv_dw_b,
                conv_ln_g=m_conv_ln_g, conv_ln_b=m_conv_ln_b, conv_w_proj=m_conv_w_proj, attn_sinks=m_attn_sinks,
                attn_w_o=m_attn_w_o, gate_b=m_gate_b, w_out=m_w_out, ffn2_norm=m_ffn2_norm,
                ffn2_w_gate=m_ffn2_w_gate, ffn2_w_up=m_ffn2_w_up, ffn2_w_down=m_ffn2_w_down, final_norm=m_final_norm)
    v_in = dict(ffn1_norm=v_ffn1_norm, ffn1_w_gate=v_ffn1_w_gate, ffn1_w_up=v_ffn1_w_up, ffn1_w_down=v_ffn1_w_down,
                mix_norm=v_mix_norm, w_in=v_w_in, conv_dw_w=v_conv_dw_w, conv_dw_b=v_conv_dw_b,
                conv_ln_g=v_conv_ln_g, conv_ln_b=v_conv_ln_b, conv_w_proj=v_conv_w_proj, attn_sinks=v_attn_sinks,
                attn_w_o=v_attn_w_o, gate_b=v_gate_b, w_out=v_w_out, ffn2_norm=v_ffn2_norm,
                ffn2_w_gate=v_ffn2_w_gate, ffn2_w_up=v_ffn2_w_up, ffn2_w_down=v_ffn2_w_down, final_norm=v_final_norm)
    names = list(weights)
    big = ["ffn1_w_gate", "ffn1_w_up", "ffn1_w_down", "w_in", "conv_w_proj", "attn_w_o", "w_out",
           "ffn2_w_gate", "ffn2_w_up", "ffn2_w_down"]

    xs = x[0]
    T, D = xs.shape
    KV = (w_in.shape[2] * N_CHIPS - 5 * D) // 2
    n_heads = D // HEAD_DIM
    my_chip = 2 * lax.axis_index("x") + lax.axis_index("y")
    place = jnp.stack([my_chip, lax.axis_index("c")]).astype(jnp.int32)

    placed = {k: place_shard(place, weights[k][0], BF16, f"place_{k}") for k in big}
    placed_dw = place_shard(place, conv_dw_w[0], F32, "place_conv_dw_w")
    first, later = big[:3], big[3:]
    wg1, wu1, wd1 = run_side(gather_side([placed[k] for k in first], []), "gather_ffn1")
    x1, h1, g1, u1, *gathered = ffn_fwd(x[0], ffn1_norm, wg1, wu1, wd1, "ffn1_fwd",
                                        side=gather_side([placed[k] for k in later], [placed_dw]))
    full = dict(zip(later + ["conv_dw_w"], gathered))
    wg2, wu2, wd2 = full["ffn2_w_gate"], full["ffn2_w_up"], full["ffn2_w_down"]
    w_glu = _cols_from_pieces(full["w_in"], 0, 2 * D)
    w_qkv = _cols_from_pieces(full["w_in"], 2 * D, 3 * D + 2 * KV)
    w_gate = _cols_from_pieces(full["w_in"], 3 * D + 2 * KV, 5 * D + 2 * KV)
    w_proj = full["conv_w_proj"].reshape(D, D)
    w_o = full["attn_w_o"].reshape(D, D)
    w_out_f = full["w_out"].reshape(D, D)
    dw_w = full["conv_dw_w"].transpose(1, 0, 2).reshape(CONV_WIDTH, D)
    dw_w = jnp.concatenate([dw_w, jnp.zeros((CONV_HALO - CONV_WIDTH, D), F32)], axis=0)
    cs, sn = _rope_tables(positions[0])
    fn_row = final_norm.reshape(1, D)

    h2, p_glu, p_gate, qr, kr, vb = mix_in_fwd(x1, mix_norm, w_glu, w_qkv, w_gate, cs, sn, "mix_in_fwd")
    c1, c3 = conv_fwd(p_glu, dw_w, conv_dw_b, conv_ln_g, conv_ln_b, "conv_fwd")
    o = attn_fwd(qr, kr, vb, attn_sinks, "attn_fwd")
    x2, conv_out, attn_out, merged = merge_fwd(x1, c3, o, p_gate, gate_b, w_proj, w_o, w_out_f, "merge_fwd")
    x3, h3, g2, u2 = ffn_fwd(x2, ffn2_norm, wg2, wu2, wd2, "ffn2_fwd")

    dx3, head_sums = loss_head(x3, fn_row, loss_target[0], "loss_head")
    dh3, dwg2, dwu2, dwd2 = ffn_bwd(h3, g2, u2, wg2, wu2, wd2, dx3, "ffn2_bwd")
    dx2, d_ffn2_norm = rmsnorm_bwd(x2, ffn2_norm, dh3, dx3, "ffn2_norm_bwd")
    ffn2_grads = [dwg2, dwu2, dwd2]
    d_gates, d_conv_out, d_attn_out, d_o, dc1, merge_sums, *sibs_f2 = merge_bwd(
        dx2, p_gate, gate_b, conv_out, attn_out, c1, conv_ln_g, conv_ln_b, w_proj, w_o, w_out_f, "merge_bwd",
        side=exchange_siblings_side(ffn2_grads))
    d_w_out = matmul_tn(merged, dx2, "d_w_out")
    d_w_proj = matmul_tn(c3, d_conv_out, "d_conv_w_proj")
    d_w_o = matmul_tn(o, d_attn_out, "d_attn_w_o")
    wires_f2, owns_f2 = rs_to_wires(place, ffn2_grads, "ffn2", sibs=sibs_f2)
    d_glu, d_dw_w, *gots_f2 = conv_bwd(p_glu, dc1, dw_w, "conv_bwd", side=exchange_chips_side(wires_f2))
    dq, dk, dv, d_sinks = attn_bwd(qr, kr, vb, o, d_o, attn_sinks, "attn_bwd")
    d_qkv = rope_bwd(dq, dk, dv, cs, sn, "rope_bwd")
    dx1, d_mix_norm = mix_in_bwd([d_glu, d_qkv, d_gates], [w_glu, w_qkv, w_gate], x1, mix_norm, dx2, "mix_in_bwd")
    d_w_in = _pieces_from_groups([matmul_tn(h2, d_glu, "d_w_in_glu"), matmul_tn(h2, d_qkv, "d_w_in_qkv"),
                                  matmul_tn(h2, d_gates, "d_w_in_gate")])
    dwc = D // N_CHIPS
    mixer_grads = [d_w_in, d_w_proj.reshape(N_CHIPS, dwc, D), d_w_o.reshape(N_CHIPS, dwc, D),
                   d_w_out.reshape(N_CHIPS, dwc, D)]
    wires_m, owns_m = rs_to_wires(place, mixer_grads, "mixer")
    dh1, dwg1, dwu1, dwd1, *gots_m = ffn_bwd(h1, g1, u1, wg1, wu1, wd1, dx1, "ffn1_bwd",
                                             side=exchange_chips_side(wires_m))
    dx0, d_ffn1_norm = rmsnorm_bwd(xs, ffn1_norm, dh1, dx1, "ffn1_norm_bwd")
    wires_l, owns_l = rs_to_wires(place, [dwg1, dwu1, dwd1], "ffn1")
    gots_l = run_side(exchange_chips_side(wires_l), "rs_exchange_chips_ffn1")
    reduced = rs_finish(place, owns_f2 + owns_m + owns_l, list(gots_f2) + list(gots_m) + list(gots_l))

    pad_row = lambda v: jnp.pad(v, ((0, 0), (0, D - v.shape[1])))
    small_rows = jnp.concatenate([
        d_ffn1_norm, d_mix_norm, merge_sums[2:3, :D], merge_sums[1:2, :D], merge_sums[1:2, D:],
        pad_row(d_sinks[0:1, :n_heads]), merge_sums[0:1, :D], merge_sums[0:1, D:], d_ffn2_norm,
        head_sums[0:1], head_sums[1:2], jnp.zeros((5, D), F32), d_dw_w], axis=0)
    small = allreduce_small(small_rows)
    loss = small[10, 0]
    grads = {"ffn1_norm": small[0:1], "mix_norm": small[1:2], "conv_dw_b": small[2:3], "conv_ln_g": small[3:4],
             "conv_ln_b": small[4:5], "attn_sinks": small[5:6, :n_heads],
             "gate_b": jnp.concatenate([small[6:7], small[7:8]], axis=1), "ffn2_norm": small[8:9],
             "final_norm": small[9:10]}
    grads["conv_dw_w"] = lax.dynamic_slice(small[16:16 + CONV_WIDTH], (0, my_chip * dwc), (CONV_WIDTH, dwc))
    grads.update(zip(["ffn2_w_gate", "ffn2_w_up", "ffn2_w_down", "w_in", "conv_w_proj", "attn_w_o", "w_out",
                      "ffn1_w_gate", "ffn1_w_up", "ffn1_w_down"], reduced))

    deltas, new_m, new_v = {}, {}, {}
    for k in names:
        shape = weights[k].shape
        g2d = grads[k].reshape(-1, shape[-1])
        grads[k] = g2d.reshape(shape)
        d, mn, vn = adamw(weights[k].reshape(g2d.shape), g2d, m_in[k].reshape(g2d.shape),
                          v_in[k].reshape(g2d.shape), f"adamw_{k}")
        deltas[k], new_m[k], new_v[k] = d.reshape(shape), mn.reshape(shape), vn.reshape(shape)

    return (loss, dx0[None], *[grads[k] for k in names], *[deltas[k] for k in names],
            *[new_m[k] for k in names], *[new_v[k] for k in names])
```

```python
import functools

import jax
import jax.numpy as jnp
from jax import lax
from jax.experimental import pallas as pl
from jax.experimental.pallas import tpu as pltpu

F32 = jnp.float32
BF16 = jnp.bfloat16
MESH = pl.DeviceIdType.MESH

HEAD_DIM = 64
WINDOW = 128
CONV_WIDTH = 31
CONV_HALO = 32
ROPE_THETA = 10000.0
EPS = 1e-6
LN_EPS = 1e-5
NEG_INF = -1e30
N_CHIPS = 4
N_DEV = 8

ADAM_LR = 0.001
ADAM_B1 = 0.9
ADAM_B2 = 0.999
ADAM_EPS = 1e-08
ADAM_WD = 0.01
ADAM_STEP = 10

TM_FFN = 512
TM_FFN_FWD = 1024
TM_ROW = 256
TK_TN = 1024
TR_ELT = 256
VMEM_LIMIT = 56 * 1024 * 1024

NT_DIMS = (((1,), (1,)), ((), ()))
TN_DIMS = (((0,), (0,)), ((), ()))


def _row_tile(rows, cap):
    for t in range(min(cap, rows), 15, -1):
        if rows % t == 0 and t % 16 == 0:
            return t
    return rows


def _params(sem):
    return pltpu.CompilerParams(dimension_semantics=sem, vmem_limit_bytes=VMEM_LIMIT)


def _dot(a, b):
    return jnp.dot(a, b, preferred_element_type=F32)


def _dot_nt(a, b):
    return lax.dot_general(a, b, NT_DIMS, preferred_element_type=F32)


def _dot_tn(a, b):
    return lax.dot_general(a, b, TN_DIMS, preferred_element_type=F32)


def _split_rows(dot, a, b):
    m = a.shape[0] // 2
    return jnp.concatenate([dot(a[:m], b), dot(a[m:], b)], axis=0)


def _sigmoid(x):
    return jax.nn.sigmoid(x)


def _rms_scale(xv):
    return lax.rsqrt(jnp.mean(xv * xv, axis=-1, keepdims=True) + EPS)


def _rms_bwd(xv, nw, dh):
    r = _rms_scale(xv)
    dn = dh * nw
    dx = r * dn - xv * (r * r * r) * jnp.mean(dn * xv, axis=-1, keepdims=True)
    dnw = jnp.sum(dh * (xv * r), axis=0, keepdims=True)
    return dx, dnw


def _silu_grad(z, s):
    return s * (1.0 + z * (1.0 - s))


HBM_SPEC = pl.BlockSpec(memory_space=pl.ANY)


def _call_hosting(body, side, *, grid, in_specs, out_specs, out_shape, scratch_shapes, operands, name, aliases=None):
    params = _params(("arbitrary",) * len(grid))
    aliases = dict(aliases or {})
    if side is None:
        return pl.pallas_call(body, name=name, grid=grid, in_specs=in_specs, out_specs=out_specs, out_shape=out_shape,
                              scratch_shapes=scratch_shapes, input_output_aliases=aliases,
                              compiler_params=params)(*operands)
    n_in, n_out, n_scr = len(in_specs), len(out_shape), len(scratch_shapes)
    s_in, s_out = len(side["inputs"]), len(side["out_shapes"])

    def at_step(end):
        hit = pl.program_id(0) == (grid[0] - 1 if end else 0)
        for a in range(1, len(grid)):
            hit &= pl.program_id(a) == (grid[a] - 1 if end else 0)
        return hit

    def hosted(*refs):
        b = n_in + s_in
        c = b + n_out
        d = c + s_out
        e = d + n_scr
        src, dst, sems = refs[n_in:b], refs[c:d], refs[e:]

        @pl.when(at_step(False))
        def _():
            side["start"](src, dst, sems)

        body(*refs[:n_in], *refs[b:c], *refs[d:e])

        @pl.when(at_step(True))
        def _():
            side["finish"](src, dst, sems)

    return pl.pallas_call(
        hosted, name=name, grid=grid, in_specs=list(in_specs) + [HBM_SPEC] * s_in,
        out_specs=list(out_specs) + [HBM_SPEC] * s_out, out_shape=list(out_shape) + list(side["out_shapes"]),
        scratch_shapes=list(scratch_shapes) + list(side["sems"]),
        input_output_aliases={**aliases, **{n_in + a: n_out + b for a, b in side["aliases"].items()}},
        compiler_params=params)(*operands, *side["inputs"])


def ffn_fwd(x, nw, wg, wu, wd, name, side=None):
    T, D = x.shape
    NP, _, Fs = wg.shape
    tm = min(TM_FFN_FWD, T)

    def body(x_ref, nw_ref, wg_ref, wu_ref, wd_ref, xo_ref, h_ref, g_ref, u_ref, acc_ref):
        j = pl.program_id(1)

        @pl.when(j == 0)
        def _():
            xv = x_ref[...]
            h_ref[...] = (xv * _rms_scale(xv) * nw_ref[...]).astype(BF16)
            acc_ref[...] = jnp.zeros_like(acc_ref)

        h = h_ref[...]
        g = _dot(h, wg_ref[...])
        u = _dot(h, wu_ref[...])
        a = (g * _sigmoid(g)) * u
        g_ref[...] = g.astype(BF16)
        u_ref[...] = u.astype(BF16)
        acc_ref[...] += _dot(a.astype(BF16), wd_ref[...])

        @pl.when(j == NP - 1)
        def _():
            xo_ref[...] = x_ref[...] + 0.5 * acc_ref[...]

    return _call_hosting(
        body, side, name=name, grid=(T // tm, NP),
        in_specs=[pl.BlockSpec((tm, D), lambda i, j: (i, 0)),
                  pl.BlockSpec((1, D), lambda i, j: (0, 0)),
                  pl.BlockSpec((None, D, Fs), lambda i, j: (j, 0, 0)),
                  pl.BlockSpec((None, D, Fs), lambda i, j: (j, 0, 0)),
                  pl.BlockSpec((None, Fs, D), lambda i, j: (j, 0, 0))],
        out_specs=[pl.BlockSpec((tm, D), lambda i, j: (i, 0)),
                   pl.BlockSpec((tm, D), lambda i, j: (i, 0)),
                   pl.BlockSpec((None, tm, Fs), lambda i, j: (j, i, 0)),
                   pl.BlockSpec((None, tm, Fs), lambda i, j: (j, i, 0))],
        out_shape=[jax.ShapeDtypeStruct((T, D), F32), jax.ShapeDtypeStruct((T, D), BF16),
                   jax.ShapeDtypeStruct((NP, T, Fs), BF16), jax.ShapeDtypeStruct((NP, T, Fs), BF16)],
        scratch_shapes=[pltpu.VMEM((tm, D), F32)],
        operands=(x, nw, wg, wu, wd))


def _ffn_bwd_piece(j, h, g, u, wg, wu, wd, dout, dh_in, dws_in, name, side, norm):
    T, D = h.shape
    NP, _, Fs = wg.shape
    tm = min(TM_FFN, T)
    n_in = 7 + (dh_in is not None) + (2 if norm else 0) + (3 if dws_in else 0)

    def body(*refs):
        h_ref, g_ref, u_ref, wg_ref, wu_ref, wd_ref, do_ref = refs[:7]
        dhin_ref = refs[7] if dh_in is not None else None
        dh_ref, dwg_ref, dwu_ref, dwd_ref = refs[n_in:n_in + 4]

        @pl.when(pl.program_id(0) == 0)
        def _():
            dwg_ref[...] = jnp.zeros_like(dwg_ref)
            dwu_ref[...] = jnp.zeros_like(dwu_ref)
            dwd_ref[...] = jnp.zeros_like(dwd_ref)
            if norm:
                refs[n_in + 4][...] = jnp.zeros_like(refs[n_in + 4])

        dob = (0.5 * do_ref[...]).astype(BF16)
        da = _split_rows(_dot_nt, dob, wd_ref[...])
        gf = g_ref[...].astype(F32)
        uf = u_ref[...].astype(F32)
        s = _sigmoid(gf)
        act = gf * s
        dg = (da * uf * _silu_grad(gf, s)).astype(BF16)
        du = (da * act).astype(BF16)
        a = (act * uf).astype(BF16)
        dh = _dot_nt(dg, wg_ref[...]) + _dot_nt(du, wu_ref[...])
        dh = dh if dhin_ref is None else dhin_ref[...] + dh
        if norm:
            x_ref, nw_ref = refs[7 + (dh_in is not None):9 + (dh_in is not None)]
            dxn, dnw = _rms_bwd(x_ref[...], nw_ref[...], dh)
            dh_ref[...] = do_ref[...] + dxn
            refs[n_in + 4][...] += dnw
        else:
            dh_ref[...] = dh
        hb = h_ref[...]
        dwg_ref[...] += _dot_tn(hb, dg)
        dwu_ref[...] += _dot_tn(hb, du)
        dwd_ref[...] += _dot_tn(a, dob)

    rows = pl.BlockSpec((tm, D), lambda i: (i, 0))
    piece = pl.BlockSpec((None, tm, Fs), lambda i: (j, i, 0))
    once = pl.Buffered(1)
    slot = lambda r, c: pl.BlockSpec((None, r, c), lambda i: (j, 0, 0), pipeline_mode=once)
    in_specs = [rows, piece, piece, slot(D, Fs), slot(D, Fs), slot(Fs, D), rows]
    operands = [h, g, u, wg, wu, wd, dout]
    aliases = {}
    if dh_in is not None:
        in_specs.append(rows)
        operands.append(dh_in)
    if norm:
        in_specs += [rows, pl.BlockSpec((1, D), lambda i: (0, 0))]
        operands += list(norm)
    if dws_in:
        aliases = {len(operands) + k: 1 + k for k in range(3)}
        in_specs += [HBM_SPEC] * 3
        operands += list(dws_in)
    out_specs = [rows, slot(D, Fs), slot(D, Fs), slot(Fs, D)]
    out_shape = [jax.ShapeDtypeStruct((T, D), F32), jax.ShapeDtypeStruct((NP, D, Fs), F32),
                 jax.ShapeDtypeStruct((NP, D, Fs), F32), jax.ShapeDtypeStruct((NP, Fs, D), F32)]
    if norm:
        out_specs.append(pl.BlockSpec((1, D), lambda i: (0, 0)))
        out_shape.append(jax.ShapeDtypeStruct((1, D), F32))
    return _call_hosting(body, side, name=name, grid=(T // tm,), in_specs=in_specs, out_specs=out_specs,
                         out_shape=out_shape, scratch_shapes=[], aliases=aliases, operands=tuple(operands))


def ffn_bwd(x, nw, h, g, u, wg, wu, wd, dout, name, side=None):
    NP = wg.shape[0]
    dh, dws, extra = None, None, []
    for j in range(NP):
        dh, *rest = _ffn_bwd_piece(j, h, g, u, wg, wu, wd, dout, dh, dws, f"{name}_{j}",
                                   side if j == 0 else None, (x, nw) if j == NP - 1 else None)
        dws, rest = rest[:3], rest[3:]
        if j == 0:
            extra = rest[1:] if NP == 1 else rest
    return (dh, *dws, rest[0], *extra)


def mix_in_fwd(x, nw, w_glu, w_qkv, w_gate, cs, sn, name):
    T, D = x.shape
    KV = (w_qkv.shape[1] - D) // 2
    tm = min(TM_ROW, T)

    def body(x_ref, nw_ref, wa_ref, wq_ref, wg_ref, cs_ref, sn_ref, h_ref, pa_ref, pg_ref, q_ref, k_ref, v_ref):
        xv = x_ref[...]
        h = (xv * _rms_scale(xv) * nw_ref[...]).astype(BF16)
        h_ref[...] = h
        pa_ref[...] = _dot(h, wa_ref[...])
        pg_ref[...] = _dot(h, wg_ref[...])
        qkv = _dot(h, wq_ref[...])
        cs_v, sn_v = cs_ref[...], sn_ref[...]
        q_ref[...] = _rope_chunks(qkv[:, :D], cs_v, sn_v, 1.0).astype(BF16)
        k_ref[...] = _rope_chunks(qkv[:, D:D + KV], cs_v, sn_v, 1.0).astype(BF16)
        v_ref[...] = qkv[:, D + KV:].astype(BF16)

    rows = lambda w: pl.BlockSpec((tm, w), lambda i: (i, 0))
    whole = lambda a: pl.BlockSpec(a.shape, lambda i: (0, 0))
    return pl.pallas_call(
        body, name=name, grid=(T // tm,),
        in_specs=[rows(D), whole(nw), whole(w_glu), whole(w_qkv), whole(w_gate), rows(128), rows(128)],
        out_specs=[rows(D), rows(2 * D), rows(2 * D), rows(D), rows(KV), rows(KV)],
        out_shape=[jax.ShapeDtypeStruct((T, D), BF16), jax.ShapeDtypeStruct((T, 2 * D), F32),
                   jax.ShapeDtypeStruct((T, 2 * D), F32), jax.ShapeDtypeStruct((T, D), BF16),
                   jax.ShapeDtypeStruct((T, KV), BF16), jax.ShapeDtypeStruct((T, KV), BF16)],
        compiler_params=_params(("parallel",)),
    )(x, nw, w_glu, w_qkv, w_gate, cs, sn)


def matmul_tn(lhs, rhs, name):
    T, K = lhs.shape
    N = rhs.shape[1]
    tk = min(TK_TN, T)

    def body(l_ref, r_ref, o_ref):
        @pl.when(pl.program_id(0) == 0)
        def _():
            o_ref[...] = jnp.zeros_like(o_ref)

        o_ref[...] += _dot_tn(l_ref[...].astype(BF16), r_ref[...].astype(BF16))

    return pl.pallas_call(
        body, name=name, grid=(T // tk,),
        in_specs=[pl.BlockSpec((tk, K), lambda t: (t, 0)), pl.BlockSpec((tk, N), lambda t: (t, 0))],
        out_specs=pl.BlockSpec((K, N), lambda t: (0, 0)),
        out_shape=jax.ShapeDtypeStruct((K, N), F32),
        compiler_params=_params(("arbitrary",)),
    )(lhs, rhs)


def mix_in_bwd(dps, ws, x, nw, dres, name):
    T, D = x.shape
    tm = min(TM_ROW, T)
    n = len(dps)

    def body(*refs):
        dp_refs, w_refs = refs[:n], refs[n:2 * n]
        x_ref, nw_ref, dr_ref, dx_ref, dnw_ref = refs[2 * n:]

        @pl.when(pl.program_id(0) == 0)
        def _():
            dnw_ref[...] = jnp.zeros_like(dnw_ref)

        dh = _dot_nt(dp_refs[0][...], w_refs[0][...])
        for k in range(1, n):
            dh += _dot_nt(dp_refs[k][...], w_refs[k][...])
        dxn, dnw = _rms_bwd(x_ref[...], nw_ref[...], dh)
        dx_ref[...] = dr_ref[...] + dxn
        dnw_ref[...] += dnw

    in_specs = [pl.BlockSpec((tm, dp.shape[1]), lambda i: (i, 0)) for dp in dps]
    in_specs += [pl.BlockSpec(w.shape, lambda i: (0, 0)) for w in ws]
    in_specs += [pl.BlockSpec((tm, D), lambda i: (i, 0)), pl.BlockSpec((1, D), lambda i: (0, 0)),
                 pl.BlockSpec((tm, D), lambda i: (i, 0))]
    return pl.pallas_call(
        body, name=name, grid=(T // tm,), in_specs=in_specs,
        out_specs=[pl.BlockSpec((tm, D), lambda i: (i, 0)), pl.BlockSpec((1, D), lambda i: (0, 0))],
        out_shape=[jax.ShapeDtypeStruct((T, D), F32), jax.ShapeDtypeStruct((1, D), F32)],
        compiler_params=_params(("arbitrary",)),
    )(*dps, *ws, x, nw, dres)


def _layernorm_stats(c1):
    mu = jnp.mean(c1, axis=-1, keepdims=True)
    xc = c1 - mu
    rstd = lax.rsqrt(jnp.mean(xc * xc, axis=-1, keepdims=True) + LN_EPS)
    return xc * rstd, rstd


def _shifted_copies(src_ref, dst_ref):
    rows = dst_ref.shape[1]
    for b in range(1, 8):
        dst_ref[b - 1] = src_ref[pl.ds(b, rows), :]


def _shifted_rows(src_ref, shifted_ref, start, rows, cols):
    a8, b = divmod(start, 8)
    if b == 0:
        return src_ref[pl.ds(8 * a8, rows), cols]
    return shifted_ref[b - 1, pl.ds(8 * a8, rows), cols]


def conv_fwd(p_glu, dw_w, dw_b, ln_g, ln_b, name):
    T, D2 = p_glu.shape
    D = D2 // 2
    tm = min(TM_ROW, T)
    hb = tm // CONV_HALO

    def body(a_ref, b_ref, ah_ref, bh_ref, w_ref, wb_ref, g_ref, be_ref, c1_ref, c3_ref, e_ref, es_ref):
        i = pl.program_id(0)
        halo = ah_ref[...] * _sigmoid(bh_ref[...])
        e_ref[pl.ds(0, CONV_HALO), :] = jnp.where(i > 0, halo, 0.0)
        e_ref[pl.ds(CONV_HALO, tm), :] = a_ref[...] * _sigmoid(b_ref[...])
        _shifted_copies(e_ref, es_ref)
        off = CONV_HALO - (CONV_WIDTH - 1)

        def strip(s, carry):
            cols = pl.ds(pl.multiple_of(s * 128, 128), 128)
            acc = jnp.zeros((tm, 128), F32) + wb_ref[:, cols]
            for k in range(CONV_WIDTH):
                acc += w_ref[pl.ds(k, 1), cols] * _shifted_rows(e_ref, es_ref, off + k, tm, cols)
            c1_ref[:, cols] = acc
            return carry

        lax.fori_loop(0, D // 128, strip, 0)
        xhat, _ = _layernorm_stats(c1_ref[...])
        c2 = xhat * g_ref[...] + be_ref[...]
        c3_ref[...] = (c2 * _sigmoid(c2)).astype(BF16)

    row = pl.BlockSpec((1, D), lambda i: (0, 0))
    return pl.pallas_call(
        body, name=name, grid=(T // tm,),
        in_specs=[pl.BlockSpec((tm, D), lambda i: (i, 0)), pl.BlockSpec((tm, D), lambda i: (i, 1)),
                  pl.BlockSpec((CONV_HALO, D), lambda i: (jnp.maximum(i * hb - 1, 0), 0)),
                  pl.BlockSpec((CONV_HALO, D), lambda i: (jnp.maximum(i * hb - 1, 0), 1)),
                  pl.BlockSpec((CONV_HALO, D), lambda i: (0, 0)), row, row, row],
        out_specs=[pl.BlockSpec((tm, D), lambda i: (i, 0)), pl.BlockSpec((tm, D), lambda i: (i, 0))],
        out_shape=[jax.ShapeDtypeStruct((T, D), F32), jax.ShapeDtypeStruct((T, D), BF16)],
        scratch_shapes=[pltpu.VMEM((tm + CONV_HALO, D), F32), pltpu.VMEM((7, tm + CONV_HALO - 8, D), F32)],
        compiler_params=_params(("parallel",)),
    )(p_glu, p_glu, p_glu, p_glu, dw_w, dw_b, ln_g, ln_b)


def conv_bwd(p_glu, dc1, dw_w, name, side=None):
    T, D2 = p_glu.shape
    D = D2 // 2
    tm = min(TM_ROW, T)
    hb = tm // CONV_HALO
    last = T // CONV_HALO - 1
    nblk = T // tm

    def body(a_ref, b_ref, ah_ref, bh_ref, d_ref, dn_ref, w_ref, dp_ref, dw_ref, e_ref, f_ref, es_ref, fs_ref):
        i = pl.program_id(0)

        @pl.when(i == 0)
        def _():
            dw_ref[...] = jnp.zeros_like(dw_ref)

        halo = ah_ref[...] * _sigmoid(bh_ref[...])
        e_ref[pl.ds(0, CONV_HALO), :] = jnp.where(i > 0, halo, 0.0)
        e_ref[pl.ds(CONV_HALO, tm), :] = a_ref[...] * _sigmoid(b_ref[...])
        f_ref[pl.ds(0, tm), :] = d_ref[...]
        f_ref[pl.ds(tm, CONV_HALO), :] = jnp.where(i < nblk - 1, dn_ref[...], 0.0)
        _shifted_copies(e_ref, es_ref)
        _shifted_copies(f_ref, fs_ref)
        off = CONV_HALO - (CONV_WIDTH - 1)

        def strip(s, carry):
            cols = pl.ds(pl.multiple_of(s * 128, 128), 128)
            d = d_ref[:, cols]
            dc0 = jnp.zeros((tm, 128), F32)
            for k in range(CONV_WIDTH):
                dw_ref[pl.ds(k, 1), cols] += jnp.sum(d * _shifted_rows(e_ref, es_ref, off + k, tm, cols),
                                                     axis=0, keepdims=True)
                dc0 += w_ref[pl.ds(k, 1), cols] * _shifted_rows(f_ref, fs_ref, CONV_WIDTH - 1 - k, tm, cols)
            a = a_ref[:, cols]
            sb = _sigmoid(b_ref[:, cols])
            dp_ref[:, cols] = (dc0 * sb).astype(BF16)
            dp_ref[:, pl.ds(pl.multiple_of(D + s * 128, 128), 128)] = (dc0 * a * sb * (1.0 - sb)).astype(BF16)
            return carry

        lax.fori_loop(0, D // 128, strip, 0)

    return _call_hosting(
        body, side, name=name, grid=(nblk,),
        in_specs=[pl.BlockSpec((tm, D), lambda i: (i, 0)), pl.BlockSpec((tm, D), lambda i: (i, 1)),
                  pl.BlockSpec((CONV_HALO, D), lambda i: (jnp.maximum(i * hb - 1, 0), 0)),
                  pl.BlockSpec((CONV_HALO, D), lambda i: (jnp.maximum(i * hb - 1, 0), 1)),
                  pl.BlockSpec((tm, D), lambda i: (i, 0)),
                  pl.BlockSpec((CONV_HALO, D), lambda i: (jnp.minimum((i + 1) * hb, last), 0)),
                  pl.BlockSpec((CONV_HALO, D), lambda i: (0, 0))],
        out_specs=[pl.BlockSpec((tm, D2), lambda i: (i, 0)), pl.BlockSpec((CONV_HALO, D), lambda i: (0, 0))],
        out_shape=[jax.ShapeDtypeStruct((T, D2), BF16), jax.ShapeDtypeStruct((CONV_HALO, D), F32)],
        scratch_shapes=[pltpu.VMEM((tm + CONV_HALO, D), F32), pltpu.VMEM((tm + CONV_HALO, D), F32),
                        pltpu.VMEM((7, tm + CONV_HALO - 8, D), F32), pltpu.VMEM((7, tm + CONV_HALO - 8, D), F32)],
        operands=(p_glu, p_glu, p_glu, p_glu, dc1, dc1, dw_w))


def _rot_half(x):
    lane = lax.broadcasted_iota(jnp.int32, x.shape, 1)
    first = (lane % HEAD_DIM) < HEAD_DIM // 2
    return jnp.where(first, pltpu.roll(x, 128 - HEAD_DIM // 2, 1), pltpu.roll(x, HEAD_DIM // 2, 1))


def _rope_chunks(x, cs, sn, sign):
    outs = []
    for c in range(x.shape[1] // 128):
        xc = x[:, c * 128:(c + 1) * 128]
        outs.append(xc * cs + sign * (_rot_half(xc) * sn))
    return outs[0] if len(outs) == 1 else jnp.concatenate(outs, axis=1)


def rope_bwd(dq, dk, dv, cs, sn, name):
    T, D = dq.shape
    KV = dk.shape[1]
    tm = min(TM_ROW, T)

    def body(dq_ref, dk_ref, dv_ref, cs_ref, sn_ref, o_ref):
        cs_v, sn_v = cs_ref[...], sn_ref[...]
        o_ref[:, pl.ds(0, D)] = _rope_chunks(dq_ref[...], cs_v, sn_v, -1.0).astype(BF16)
        o_ref[:, pl.ds(D, KV)] = _rope_chunks(dk_ref[...], cs_v, sn_v, -1.0).astype(BF16)
        o_ref[:, pl.ds(D + KV, KV)] = dv_ref[...].astype(BF16)

    tab = pl.BlockSpec((tm, 128), lambda i: (i, 0))
    return pl.pallas_call(
        body, name=name, grid=(T // tm,),
        in_specs=[pl.BlockSpec((tm, D), lambda i: (i, 0)), pl.BlockSpec((tm, KV), lambda i: (i, 0)),
                  pl.BlockSpec((tm, KV), lambda i: (i, 0)), tab, tab],
        out_specs=pl.BlockSpec((tm, D + 2 * KV), lambda i: (i, 0)),
        out_shape=jax.ShapeDtypeStruct((T, D + 2 * KV), BF16),
        compiler_params=_params(("parallel",)),
    )(dq, dk, dv, cs, sn)


def _lane_lo():
    return lax.broadcasted_iota(jnp.int32, (1, 128), 1) < HEAD_DIM


def _band_mask(i, reps):
    shape = (reps * WINDOW, 2 * WINDOW)
    qi = lax.broadcasted_iota(jnp.int32, shape, 0) % WINDOW
    cj = lax.broadcasted_iota(jnp.int32, shape, 1)
    rel = qi - cj + WINDOW
    return (rel >= 0) & (rel < WINDOW) & ((i > 0) | (cj >= WINDOW))


def _stack_pairs(ref, first, n):
    parts = [ref[:, pl.ds((first + p) * 128, 128)] for p in range(n)]
    return parts[0] if n == 1 else jnp.concatenate(parts, axis=0)


def _pair_rows(n):
    return lax.broadcasted_iota(jnp.int32, (n * WINDOW, 1), 0) // WINDOW


def _per_pair_column(values, n):
    rows = _pair_rows(n)
    col = jnp.zeros((n * WINDOW, 1), F32) + values[0]
    for p in range(1, n):
        col = jnp.where(rows == p, values[p], col)
    return col


def _kv_lo_hi(x2, g):
    pair, half = divmod(g, 2)
    lo = _lane_lo()
    xg = x2[:, pair * 128:(pair + 1) * 128].astype(F32)
    xg = jnp.where(lo if half == 0 else ~lo, xg, 0.0)
    sw = pltpu.roll(xg, HEAD_DIM, 1)
    x_lo, x_hi = (xg, sw) if half == 0 else (sw, xg)
    return x_lo.astype(BF16), x_hi.astype(BF16)


def _softmax_sink(s, allowed, sink):
    s = jnp.where(allowed, s * (HEAD_DIM ** -0.5), NEG_INF)
    m = jnp.maximum(jnp.max(s, axis=-1, keepdims=True), sink)
    p = jnp.exp(s - m)
    es = jnp.exp(sink - m)
    inv = 1.0 / (jnp.sum(p, axis=-1, keepdims=True) + es)
    return p * inv, es * inv


def attn_fwd(qr, kr, vb, sinks, name):
    T, D = qr.shape
    KV = kr.shape[1]
    n_kv = KV // HEAD_DIM
    group = (D // HEAD_DIM) // n_kv
    nb = T // WINDOW

    npair = group // 2

    def body(sink_ref, q_ref, kp_ref, kc_ref, vp_ref, vc_ref, o_ref):
        i = pl.program_id(0)
        allowed = _band_mask(i, npair)
        k2 = jnp.concatenate([kp_ref[...], kc_ref[...]], axis=0)
        v2 = jnp.concatenate([vp_ref[...], vc_ref[...]], axis=0)
        outs = [None] * (D // 128)
        for g in range(n_kv):
            k_lo, k_hi = _kv_lo_hi(k2, g)
            v_lo, v_hi = _kv_lo_hi(v2, g)
            first = (g * group) // 2
            q = _stack_pairs(q_ref, first, npair)
            sink_e = _per_pair_column([sink_ref[0, g * group + 2 * p] for p in range(npair)], npair)
            sink_o = _per_pair_column([sink_ref[0, g * group + 2 * p + 1] for p in range(npair)], npair)
            pe, _ = _softmax_sink(_dot_nt(q, k_lo), allowed, sink_e)
            po, _ = _softmax_sink(_dot_nt(q, k_hi), allowed, sink_o)
            o = _dot(pe.astype(BF16), v_lo) + _dot(po.astype(BF16), v_hi)
            for p in range(npair):
                outs[first + p] = o[p * WINDOW:(p + 1) * WINDOW]
        o_ref[...] = jnp.concatenate(outs, axis=1).astype(BF16)

    prev = lambda i: (jnp.maximum(i - 1, 0), 0)
    cur = lambda i: (i, 0)
    return pl.pallas_call(
        body, name=name, grid=(nb,),
        in_specs=[pl.BlockSpec(memory_space=pltpu.SMEM),
                  pl.BlockSpec((WINDOW, D), cur),
                  pl.BlockSpec((WINDOW, KV), prev), pl.BlockSpec((WINDOW, KV), cur),
                  pl.BlockSpec((WINDOW, KV), prev), pl.BlockSpec((WINDOW, KV), cur)],
        out_specs=pl.BlockSpec((WINDOW, D), cur),
        out_shape=jax.ShapeDtypeStruct((T, D), BF16),
        compiler_params=_params(("parallel",)),
    )(sinks, qr, kr, kr, vb, vb)


def attn_bwd(qr, kr, vb, o, do, sinks, name):
    T, D = qr.shape
    KV = kr.shape[1]
    n_heads = D // HEAD_DIM
    n_kv = KV // HEAD_DIM
    group = n_heads // n_kv
    nb = T // WINDOW
    npair = group // 2
    scale = HEAD_DIM ** -0.5

    def body(sink_ref, q_ref, kp_ref, kc_ref, vp_ref, vc_ref, o_ref, do_ref,
             dq_ref, dk_ref, dv_ref, ds_ref, ck_ref, cv_ref):
        i = pl.program_id(0)
        lo = _lane_lo()

        @pl.when(i == 0)
        def _():
            ck_ref[...] = jnp.zeros_like(ck_ref)
            cv_ref[...] = jnp.zeros_like(cv_ref)
            ds_ref[...] = jnp.zeros_like(ds_ref)

        @pl.when(i < nb)
        def _():
            allowed = _band_mask(i, npair)
            rows = _pair_rows(npair)
            k2 = jnp.concatenate([kp_ref[...], kc_ref[...]], axis=0)
            v2 = jnp.concatenate([vp_ref[...], vc_ref[...]], axis=0)
            lane = lax.broadcasted_iota(jnp.int32, (1, 128), 1)
            dsink = jnp.zeros((1, 128), F32)
            dq_out = [None] * (D // 128)
            dk_pairs = [jnp.zeros((2 * WINDOW, 128), F32) for _ in range(KV // 128)]
            dv_pairs = [jnp.zeros((2 * WINDOW, 128), F32) for _ in range(KV // 128)]
            for g in range(n_kv):
                k_lo, k_hi = _kv_lo_hi(k2, g)
                v_lo, v_hi = _kv_lo_hi(v2, g)
                first = (g * group) // 2
                q = _stack_pairs(q_ref, first, npair)
                dop = _stack_pairs(do_ref, first, npair)
                dd = dop.astype(F32) * _stack_pairs(o_ref, first, npair).astype(F32)
                dq = jnp.zeros((npair * WINDOW, 128), F32)
                dkg = jnp.zeros((2 * WINDOW, 128), F32)
                dvg = jnp.zeros((2 * WINDOW, 128), F32)
                for parity, k_h, v_h, sel in ((0, k_lo, v_lo, lo), (1, k_hi, v_hi, ~lo)):
                    heads = [g * group + 2 * p + parity for p in range(npair)]
                    sink = _per_pair_column([sink_ref[0, h] for h in heads], npair)
                    p_, ps = _softmax_sink(_dot_nt(q, k_h), allowed, sink)
                    delta = jnp.sum(jnp.where(sel, dd, 0.0), axis=-1, keepdims=True)
                    dsc = (p_ * (_dot_nt(dop, v_h) - delta)).astype(BF16)
                    sd = -ps * delta
                    for p, h in enumerate(heads):
                        dsink += jnp.where(lane == h, jnp.sum(jnp.where(rows == p, sd, 0.0)), 0.0)
                    dq += _dot(dsc, k_h)
                    dkg += jnp.where(sel, _dot_tn(dsc, q), 0.0)
                    dvg += jnp.where(sel, _dot_tn(p_.astype(BF16), dop), 0.0)
                for p in range(npair):
                    dq_out[first + p] = dq[p * WINDOW:(p + 1) * WINDOW]
                pair, half = divmod(g, 2)
                keep = lo if half == 0 else ~lo
                dk_pairs[pair] += jnp.where(keep, dkg + pltpu.roll(dkg, HEAD_DIM, 1), 0.0) * scale
                dv_pairs[pair] += jnp.where(keep, dvg + pltpu.roll(dvg, HEAD_DIM, 1), 0.0)
            dq_ref[...] = jnp.concatenate(dq_out, axis=1) * scale
            dk2 = dk_pairs[0] if len(dk_pairs) == 1 else jnp.concatenate(dk_pairs, axis=1)
            dv2 = dv_pairs[0] if len(dv_pairs) == 1 else jnp.concatenate(dv_pairs, axis=1)
            dk_ref[...] = ck_ref[...] + dk2[:WINDOW]
            dv_ref[...] = cv_ref[...] + dv2[:WINDOW]
            ck_ref[...] = dk2[WINDOW:]
            cv_ref[...] = dv2[WINDOW:]
            ds_ref[pl.ds(0, 1), :] += dsink

        @pl.when(i == nb)
        def _():
            dk_ref[...] = ck_ref[...]
            dv_ref[...] = cv_ref[...]

    prev = lambda i: (jnp.maximum(i - 1, 0), 0)
    cur = lambda i: (jnp.minimum(i, nb - 1), 0)
    prevc = lambda i: (jnp.maximum(jnp.minimum(i, nb - 1) - 1, 0), 0)
    return pl.pallas_call(
        body, name=name, grid=(nb + 1,),
        in_specs=[pl.BlockSpec(memory_space=pltpu.SMEM),
                  pl.BlockSpec((WINDOW, D), cur),
                  pl.BlockSpec((WINDOW, KV), prevc), pl.BlockSpec((WINDOW, KV), cur),
                  pl.BlockSpec((WINDOW, KV), prevc), pl.BlockSpec((WINDOW, KV), cur),
                  pl.BlockSpec((WINDOW, D), cur), pl.BlockSpec((WINDOW, D), cur)],
        out_specs=[pl.BlockSpec((WINDOW, D), cur), pl.BlockSpec((WINDOW, KV), prev),
                   pl.BlockSpec((WINDOW, KV), prev), pl.BlockSpec((8, 128), lambda i: (0, 0))],
        out_shape=[jax.ShapeDtypeStruct((T, D), F32), jax.ShapeDtypeStruct((T, KV), F32),
                   jax.ShapeDtypeStruct((T, KV), F32), jax.ShapeDtypeStruct((8, 128), F32)],
        scratch_shapes=[pltpu.VMEM((WINDOW, KV), F32), pltpu.VMEM((WINDOW, KV), F32)],
        compiler_params=_params(("arbitrary",)),
    )(sinks, qr, kr, kr, vb, vb, o, do)


def merge_fwd(x, c3, o, p_gate, gate_b, w_proj, w_o, w_out, name):
    T, D = x.shape
    tm = min(TM_ROW, T)

    def body(x_ref, c3_ref, o_ref, gc_ref, ga_ref, bc_ref, ba_ref, wp_ref, wo_ref, wout_ref,
             xo_ref, co_ref, ao_ref, mg_ref):
        conv_out = _dot(c3_ref[...], wp_ref[...])
        attn_out = _dot(o_ref[...], wo_ref[...])
        merged = (_sigmoid(gc_ref[...] + bc_ref[...]) * conv_out
                  + _sigmoid(ga_ref[...] + ba_ref[...]) * attn_out).astype(BF16)
        co_ref[...] = conv_out.astype(BF16)
        ao_ref[...] = attn_out.astype(BF16)
        mg_ref[...] = merged
        xo_ref[...] = x_ref[...] + _dot(merged, wout_ref[...])

    blk = lambda j: pl.BlockSpec((tm, D), lambda i: (i, j))
    row = lambda j: pl.BlockSpec((1, D), lambda i: (0, j))
    mat = pl.BlockSpec((D, D), lambda i: (0, 0))
    return pl.pallas_call(
        body, name=name, grid=(T // tm,),
        in_specs=[blk(0), blk(0), blk(0), blk(0), blk(1), row(0), row(1), mat, mat, mat],
        out_specs=[blk(0), blk(0), blk(0), blk(0)],
        out_shape=[jax.ShapeDtypeStruct((T, D), F32)] + [jax.ShapeDtypeStruct((T, D), BF16)] * 3,
        compiler_params=_params(("parallel",)),
    )(x, c3, o, p_gate, p_gate, gate_b, gate_b, w_proj, w_o, w_out)


def merge_bwd(dx, p_gate, gate_b, conv_out, attn_out, c1, ln_g, ln_b, w_proj, w_o, w_out, name, side=None):
    T, D = dx.shape
    tm = min(TM_ROW, T)

    def body(dx_ref, gc_ref, ga_ref, bc_ref, ba_ref, co_ref, ao_ref, c1_ref, g_ref, be_ref,
             wp_ref, wo_ref, wout_ref, dgt_ref, dco_ref, dao_ref, do_ref, dc1_ref, sm_ref):
        @pl.when(pl.program_id(0) == 0)
        def _():
            sm_ref[...] = jnp.zeros_like(sm_ref)

        dm = _dot_nt(dx_ref[...].astype(BF16), wout_ref[...])
        sc = _sigmoid(gc_ref[...] + bc_ref[...])
        sa = _sigmoid(ga_ref[...] + ba_ref[...])
        dco = (dm * sc).astype(BF16)
        dao = (dm * sa).astype(BF16)
        dgc = dm * co_ref[...].astype(F32) * sc * (1.0 - sc)
        dga = dm * ao_ref[...].astype(F32) * sa * (1.0 - sa)
        dgt_ref[:, pl.ds(0, D)] = dgc.astype(BF16)
        dgt_ref[:, pl.ds(D, D)] = dga.astype(BF16)
        dco_ref[...] = dco
        dao_ref[...] = dao
        do_ref[...] = _dot_nt(dao, wo_ref[...]).astype(BF16)
        dc3 = _dot_nt(dco, wp_ref[...])
        xhat, rstd = _layernorm_stats(c1_ref[...])
        c2 = xhat * g_ref[...] + be_ref[...]
        dc2 = dc3 * _silu_grad(c2, _sigmoid(c2))
        dxh = dc2 * g_ref[...]
        dc1 = rstd * (dxh - jnp.mean(dxh, axis=-1, keepdims=True)
                      - xhat * jnp.mean(dxh * xhat, axis=-1, keepdims=True))
        dc1_ref[...] = dc1
        colsum = lambda v: jnp.sum(v, axis=0, keepdims=True)
        for r, (left, right) in enumerate(((dgc, dga), (dc2 * xhat, dc2), (dc1, None))):
            sm_ref[pl.ds(r, 1), pl.ds(0, D)] += colsum(left)
            if right is not None:
                sm_ref[pl.ds(r, 1), pl.ds(D, D)] += colsum(right)

    blk = lambda j: pl.BlockSpec((tm, D), lambda i: (i, j))
    row = lambda j: pl.BlockSpec((1, D), lambda i: (0, j))
    mat = pl.BlockSpec((D, D), lambda i: (0, 0))
    return _call_hosting(
        body, side, name=name, grid=(T // tm,),
        in_specs=[blk(0), blk(0), blk(1), row(0), row(1), blk(0), blk(0), blk(0), row(0), row(0), mat, mat, mat],
        out_specs=[pl.BlockSpec((tm, 2 * D), lambda i: (i, 0)), blk(0), blk(0), blk(0), blk(0),
                   pl.BlockSpec((8, 2 * D), lambda i: (0, 0))],
        out_shape=[jax.ShapeDtypeStruct((T, 2 * D), BF16)] + [jax.ShapeDtypeStruct((T, D), BF16)] * 3
                  + [jax.ShapeDtypeStruct((T, D), F32), jax.ShapeDtypeStruct((8, 2 * D), F32)],
        scratch_shapes=[],
        operands=(dx, p_gate, p_gate, gate_b, gate_b, conv_out, attn_out, c1, ln_g, ln_b, w_proj, w_o, w_out))


def loss_head(x, nw, target, name):
    T, D = x.shape
    tm = min(TM_ROW, T)

    def body(x_ref, nw_ref, t_ref, dx_ref, sm_ref):
        @pl.when(pl.program_id(0) == 0)
        def _():
            sm_ref[...] = jnp.zeros_like(sm_ref)

        xv = x_ref[...]
        err = xv * _rms_scale(xv) * nw_ref[...] - t_ref[...]
        loss = 0.5 * jnp.sum(jnp.mean(err * err, axis=-1, keepdims=True))
        dxn, dnw = _rms_bwd(xv, nw_ref[...], err * (1.0 / D))
        dx_ref[...] = dxn
        sm_ref[pl.ds(0, 1), :] += dnw
        sm_ref[pl.ds(1, 1), :] += jnp.zeros((1, D), F32) + loss

    return pl.pallas_call(
        body, name=name, grid=(T // tm,),
        in_specs=[pl.BlockSpec((tm, D), lambda i: (i, 0)), pl.BlockSpec((1, D), lambda i: (0, 0)),
                  pl.BlockSpec((tm, D), lambda i: (i, 0))],
        out_specs=[pl.BlockSpec((tm, D), lambda i: (i, 0)), pl.BlockSpec((8, D), lambda i: (0, 0))],
        out_shape=[jax.ShapeDtypeStruct((T, D), F32), jax.ShapeDtypeStruct((8, D), F32)],
        compiler_params=_params(("arbitrary",)),
    )(x, nw, target)


def adamw(w, g, m, v, name):
    R, C = w.shape
    tr = _row_tile(R, TR_ELT)

    def body(w_ref, g_ref, m_ref, v_ref, d_ref, mo_ref, vo_ref):
        gv = g_ref[...]
        mn = ADAM_B1 * m_ref[...] + (1.0 - ADAM_B1) * gv
        vn = ADAM_B2 * v_ref[...] + (1.0 - ADAM_B2) * (gv * gv)
        m_hat = mn / (1.0 - ADAM_B1 ** ADAM_STEP)
        v_hat = vn / (1.0 - ADAM_B2 ** ADAM_STEP)
        d_ref[...] = -ADAM_LR * (m_hat / (jnp.sqrt(v_hat) + ADAM_EPS) + ADAM_WD * w_ref[...])
        mo_ref[...] = mn
        vo_ref[...] = vn

    spec = pl.BlockSpec((tr, C), lambda i: (i, 0))
    return pl.pallas_call(
        body, name=name, grid=(R // tr,), in_specs=[spec] * 4, out_specs=[spec] * 3,
        out_shape=[jax.ShapeDtypeStruct((R, C), F32)] * 3,
        compiler_params=_params(("parallel",)),
    )(w, g, m, v)


def _place():
    return lax.axis_index("x"), lax.axis_index("y"), lax.axis_index("c")


def place_shard(place, w, dtype, name):
    R, C = w.shape
    tr = _row_tile(R, TR_ELT)

    def body(pc_ref, w_ref, o_ref):
        o_ref[...] = w_ref[...].astype(dtype)

    return pl.pallas_call(
        body, name=name,
        grid_spec=pltpu.PrefetchScalarGridSpec(
            num_scalar_prefetch=1, grid=(R // tr,),
            in_specs=[pl.BlockSpec((tr, C), lambda r, pc: (r, 0))],
            out_specs=pl.BlockSpec((None, tr, C), lambda r, pc: (pc[0], r, 0))),
        out_shape=jax.ShapeDtypeStruct((N_CHIPS, R, C), dtype),
        compiler_params=_params(("arbitrary",)),
    )(place, w)


def gather_side(shards, small):
    n, ns = len(shards), len(small)

    def ici_copy(dst, sems, k, j, x, y, c, sending):
        px, py = x ^ (j >> 1), y ^ (j & 1)
        slot = 2 * x + y if sending else 2 * px + py
        half = dst[k].shape[1] // 2
        part = dst[k].at[slot, pl.ds(c * half, half)] if k < n else dst[k].at[slot]
        return pltpu.make_async_remote_copy(part, part, sems[0].at[3 * k + j - 1], sems[1].at[3 * k + j - 1],
                                            device_id=(px, py, c), device_id_type=MESH)

    def d2d_copy(dst, sems, k, j, x, y, c, sending):
        half = dst[k].shape[1] // 2
        part = dst[k].at[2 * (x ^ (j >> 1)) + (y ^ (j & 1)), pl.ds((c if sending else 1 - c) * half, half)]
        return pltpu.make_async_remote_copy(part, part, sems[2].at[3 * k + j - 1], sems[3].at[3 * k + j - 1],
                                            device_id=(x, y, 1 - c), device_id_type=MESH)

    def start(src, dst, sems):
        x, y, c = _place()
        for k in range(n + ns):
            for j in (1, 2, 3):
                ici_copy(dst, sems, k, j, x, y, c, True).start()

    def finish(src, dst, sems):
        x, y, c = _place()
        for k in range(n + ns):
            for j in (1, 2, 3):
                ici_copy(dst, sems, k, j, x, y, c, False).wait_recv()
                if k < n:
                    d2d_copy(dst, sems, k, j, x, y, c, True).start()
        for k in range(n):
            for j in (1, 2, 3):
                d2d_copy(dst, sems, k, j, x, y, c, False).wait_recv()
        for k in range(n + ns):
            for j in (1, 2, 3):
                ici_copy(dst, sems, k, j, x, y, c, True).wait_send()
                if k < n:
                    d2d_copy(dst, sems, k, j, x, y, c, True).wait_send()

    arrays = list(shards) + list(small)
    return dict(inputs=arrays, out_shapes=[jax.ShapeDtypeStruct(a.shape, a.dtype) for a in arrays],
                aliases={k: k for k in range(n + ns)},
                sems=[pltpu.SemaphoreType.DMA((3 * (n + ns),)), pltpu.SemaphoreType.DMA((3 * (n + ns),)),
                      pltpu.SemaphoreType.DMA((3 * n,)), pltpu.SemaphoreType.DMA((3 * n,))],
                start=start, finish=finish)


def run_side(side, name):
    n_in, n_out = len(side["inputs"]), len(side["out_shapes"])

    def body(*refs):
        src, dst, sems = refs[:n_in], refs[n_in:n_in + n_out], refs[n_in + n_out:]
        side["start"](src, dst, sems)
        side["finish"](src, dst, sems)

    return pl.pallas_call(
        body, name=name, in_specs=[HBM_SPEC] * n_in, out_specs=[HBM_SPEC] * n_out,
        out_shape=side["out_shapes"], input_output_aliases=side["aliases"], scratch_shapes=side["sems"],
    )(*side["inputs"])


def allreduce_small(block):
    R, C = block.shape

    def body(x_ref, out_ref, all_ref, send_sems, recv_sems, local_sem):
        x, y, c = _place()
        me, sibling = (x, y, c), (x, y, 1 - c)
        chips = [(1 - x, y), (x, 1 - y), (1 - x, 1 - y)]

        def slot(px, py, pc):
            return all_ref.at[4 * px + 2 * py + pc]

        def copy(k, block_of, to, src=None):
            return pltpu.make_async_remote_copy(
                src_ref=slot(*block_of) if src is None else src, dst_ref=slot(*block_of),
                send_sem=send_sems.at[k], recv_sem=recv_sems.at[k], device_id=to, device_id_type=MESH)

        mine = pltpu.make_async_copy(x_ref, slot(*me), local_sem)
        mine.start()
        first = [copy(0, me, sibling, src=x_ref)]
        first += [copy(1 + j, me, (*chip, c), src=x_ref) for j, chip in enumerate(chips)]
        for cp in first:
            cp.start()
        passed = [copy(4 + j, (*chip, c), sibling) for j, chip in enumerate(chips)]
        for j, chip in enumerate(chips):
            copy(1 + j, (*chip, c), me).wait_recv()
            passed[j].start()
        copy(0, sibling, me).wait_recv()
        for j, chip in enumerate(chips):
            copy(4 + j, (*chip, 1 - c), me).wait_recv()
        for cp in first + passed:
            cp.wait_send()
        mine.wait()
        total = all_ref[0]
        for d in range(1, N_DEV):
            total = total + all_ref[d]
        out_ref[...] = total

    return pl.pallas_call(
        body, name="allreduce_small",
        in_specs=[pl.BlockSpec(memory_space=pltpu.VMEM)], out_specs=pl.BlockSpec(memory_space=pltpu.VMEM),
        out_shape=jax.ShapeDtypeStruct((R, C), F32),
        scratch_shapes=[pltpu.VMEM((N_DEV, R, C), F32), pltpu.SemaphoreType.DMA((7,)),
                        pltpu.SemaphoreType.DMA((7,)), pltpu.SemaphoreType.DMA],
        compiler_params=pltpu.CompilerParams(vmem_limit_bytes=VMEM_LIMIT),
    )(block)


def exchange_siblings_side(grads):
    n = len(grads)

    def copies(src, dst, sems):
        x, y, c = _place()
        for k in range(n):
            half = src[k].shape[1] // 2
            yield pltpu.make_async_remote_copy(src[k].at[:, pl.ds((1 - c) * half, half)], dst[k],
                                               sems[0].at[k], sems[1].at[k],
                                               device_id=(x, y, 1 - c), device_id_type=MESH)

    def start(src, dst, sems):
        for cp in copies(src, dst, sems):
            cp.start()

    def finish(src, dst, sems):
        for cp in copies(src, dst, sems):
            cp.wait()

    return dict(inputs=list(grads), aliases={},
                out_shapes=[jax.ShapeDtypeStruct((N_CHIPS, g.shape[1] // 2, g.shape[2]), F32) for g in grads],
                sems=[pltpu.SemaphoreType.DMA((n,)), pltpu.SemaphoreType.DMA((n,))], start=start, finish=finish)


def rs_chip_sum(place, grad, sib, name):
    NP, R, C = grad.shape
    half = R // 2
    tr = _row_tile(half, TR_ELT)
    nr = half // tr

    def body(pc_ref, g_ref, s_ref, wire_ref, own_ref):
        q = pl.program_id(1)
        total = g_ref[...] + s_ref[...]
        wire_ref[...] = total.astype(BF16)

        @pl.when(q == pc_ref[0])
        def _():
            own_ref[...] = total

    return pl.pallas_call(
        body, name=name,
        grid_spec=pltpu.PrefetchScalarGridSpec(
            num_scalar_prefetch=1, grid=(nr, NP),
            in_specs=[pl.BlockSpec((None, tr, C), lambda r, q, pc: (q, pc[1] * nr + r, 0)),
                      pl.BlockSpec((None, tr, C), lambda r, q, pc: (q, r, 0))],
            out_specs=[pl.BlockSpec((None, tr, C), lambda r, q, pc: (q, r, 0)),
                       pl.BlockSpec((tr, C), lambda r, q, pc: (r, 0))]),
        out_shape=[jax.ShapeDtypeStruct((NP, half, C), BF16), jax.ShapeDtypeStruct((half, C), F32)],
        compiler_params=_params(("arbitrary", "arbitrary")),
    )(place, grad, sib)


def exchange_chips_side(wires):
    n = len(wires)

    def copies(src, dst, sems):
        x, y, c = _place()
        for k in range(n):
            for j in (1, 2, 3):
                qx, qy = x ^ (j >> 1), y ^ (j & 1)
                yield pltpu.make_async_remote_copy(src[k].at[2 * qx + qy], dst[k].at[2 * x + y],
                                                   sems[0].at[3 * k + j - 1], sems[1].at[3 * k + j - 1],
                                                   device_id=(qx, qy, c), device_id_type=MESH)

    def start(src, dst, sems):
        for cp in copies(src, dst, sems):
            cp.start()

    def finish(src, dst, sems):
        for cp in copies(src, dst, sems):
            cp.wait()

    return dict(inputs=list(wires), out_shapes=[jax.ShapeDtypeStruct(w.shape, BF16) for w in wires], aliases={},
                sems=[pltpu.SemaphoreType.DMA((3 * n,)), pltpu.SemaphoreType.DMA((3 * n,))],
                start=start, finish=finish)


SEM_SPEC = pl.BlockSpec(memory_space=pltpu.SEMAPHORE)


def exchange_chips_start(wires, name):
    n = len(wires)
    side = exchange_chips_side(wires)

    def body(*refs):
        src, land, sems = refs[:n], refs[n:2 * n], refs[2 * n:2 * n + 2]
        side["start"](src, land, sems)
        refs[-1][...] = jnp.zeros_like(refs[-1])

    hbm = [pltpu.HBM(w.shape, w.dtype) for w in wires]
    outs = pl.pallas_call(
        body, name=name, in_specs=[HBM_SPEC] * (2 * n),
        out_specs=[SEM_SPEC, SEM_SPEC] + [HBM_SPEC] * (2 * n) + [pl.BlockSpec(memory_space=pltpu.VMEM)],
        out_shape=list(side["sems"]) + hbm + hbm + [jax.ShapeDtypeStruct((8, 128), F32)],
        input_output_aliases={k: 2 + k for k in range(2 * n)},
        compiler_params=pltpu.CompilerParams(has_side_effects=pltpu.SideEffectType.DATAFLOW_SIDE_EFFECTING),
    )(*[pltpu.with_memory_space_constraint(w, pltpu.HBM) for w in wires],
      *[pltpu.with_memory_space_constraint(lax.empty(w.shape, w.dtype), pltpu.HBM) for w in wires])
    return outs[0], outs[1], outs[2:2 + n], outs[2 + n:2 + 2 * n], outs[-1]


def exchange_chips_wait(send_sems, recv_sems, wires, lands, after, name):
    n = len(wires)
    side = exchange_chips_side(wires)

    def body(*refs):
        side["finish"](refs[:n], refs[n:2 * n], refs[2 * n:2 * n + 2])

    hbm = [pltpu.HBM(w.shape, w.dtype) for w in wires]
    outs = pl.pallas_call(
        body, name=name, in_specs=[HBM_SPEC] * (2 * n) + [SEM_SPEC, SEM_SPEC, HBM_SPEC],
        out_specs=[HBM_SPEC] * (2 * n), out_shape=hbm + hbm,
        input_output_aliases={k: k for k in range(2 * n)},
        compiler_params=pltpu.CompilerParams(has_side_effects=pltpu.SideEffectType.DATAFLOW_SIDE_EFFECTING),
    )(*wires, *lands, send_sems, recv_sems, after)
    return outs[n:]


def rs_final_sum(place, own, got, name):
    NP, half, C = got.shape
    tr = _row_tile(half, TR_ELT)
    nr = half // tr

    def body(pc_ref, own_ref, g1_ref, g2_ref, g3_ref, out_ref):
        out_ref[...] = ((own_ref[...] + g1_ref[...].astype(F32)) + g2_ref[...].astype(F32)) + g3_ref[...].astype(F32)

    slot = lambda j: pl.BlockSpec((None, tr, C), lambda r, pc: (pc[0] ^ j, r, 0))
    return pl.pallas_call(
        body, name=name,
        grid_spec=pltpu.PrefetchScalarGridSpec(
            num_scalar_prefetch=1, grid=(nr,),
            in_specs=[pl.BlockSpec((tr, C), lambda r, pc: (r, 0)), slot(1), slot(2), slot(3)],
            out_specs=pl.BlockSpec((tr, C), lambda r, pc: (pc[1] * nr + r, 0))),
        out_shape=jax.ShapeDtypeStruct((2 * half, C), F32),
        compiler_params=_params(("arbitrary",)),
    )(place, own, got, got, got)


def rs_share_siblings(totals, name):
    n = len(totals)

    def body(*refs):
        dst = refs[n:2 * n]
        send_sems, recv_sems = refs[2 * n:]
        x, y, c = _place()
        copies = []
        for k in range(n):
            half = dst[k].shape[0] // 2
            rows = dst[k].at[pl.ds(c * half, half)]
            cp = pltpu.make_async_remote_copy(rows, rows, send_sems.at[k], recv_sems.at[k],
                                              device_id=(x, y, 1 - c), device_id_type=MESH)
            cp.start()
            copies.append(cp)
        for k, cp in enumerate(copies):
            cp.wait_send()
            half = dst[k].shape[0] // 2
            got = dst[k].at[pl.ds((1 - c) * half, half)]
            pltpu.make_async_remote_copy(got, got, send_sems.at[k], recv_sems.at[k],
                                         device_id=(x, y, c), device_id_type=MESH).wait_recv()

    return pl.pallas_call(
        body, name=name,
        in_specs=[HBM_SPEC] * n, out_specs=[HBM_SPEC] * n,
        out_shape=[jax.ShapeDtypeStruct(t.shape, F32) for t in totals],
        input_output_aliases={k: k for k in range(n)},
        scratch_shapes=[pltpu.SemaphoreType.DMA((n,)), pltpu.SemaphoreType.DMA((n,))],
    )(*totals)


def rs_to_wires(place, grads, tag, sibs=None):
    if sibs is None:
        sibs = run_side(exchange_siblings_side(grads), f"rs_exchange_siblings_{tag}")
    wires, owns = [], []
    for k, (g, s) in enumerate(zip(grads, sibs)):
        w, o = rs_chip_sum(place, g, s, f"rs_chip_sum_{tag}{k}")
        wires.append(w)
        owns.append(o)
    return wires, owns


def rs_finish(place, owns, gots, tag):
    totals = [rs_final_sum(place, o, g, f"rs_final_sum_{tag}{k}") for k, (o, g) in enumerate(zip(owns, gots))]
    return rs_share_siblings(totals, f"rs_share_siblings_{tag}")


def _rope_tables(positions):
    half = HEAD_DIM // 2
    inv_freq = ROPE_THETA ** (-jnp.arange(half, dtype=F32) / half)
    ang = positions.astype(F32)[:, None] * inv_freq
    cos, sin = jnp.cos(ang), jnp.sin(ang)
    return jnp.tile(cos, (1, 4)), jnp.concatenate([-sin, sin, -sin, sin], axis=1)


def _cols_from_pieces(pieces, start, stop):
    C = pieces.shape[2]
    parts = []
    for q in range(N_CHIPS):
        lo, hi = max(start, q * C), min(stop, (q + 1) * C)
        if lo < hi:
            parts.append(pieces[q][:, lo - q * C:hi - q * C])
    return parts[0] if len(parts) == 1 else jnp.concatenate(parts, axis=1)


def _pieces_from_groups(groups):
    C = sum(g.shape[1] for g in groups) // N_CHIPS
    pieces = []
    for q in range(N_CHIPS):
        parts, off = [], 0
        for g in groups:
            lo, hi = max(q * C, off), min((q + 1) * C, off + g.shape[1])
            if lo < hi:
                parts.append(g[:, lo - off:hi - off])
            off += g.shape[1]
        pieces.append(parts[0] if len(parts) == 1 else jnp.concatenate(parts, axis=1))
    return jnp.stack(pieces)


def kernel(x, positions, ffn1_norm, ffn1_w_gate, ffn1_w_up, ffn1_w_down, mix_norm, w_in, conv_dw_w, conv_dw_b, conv_ln_g, conv_ln_b, conv_w_proj, attn_sinks, attn_w_o, gate_b, w_out, ffn2_norm, ffn2_w_gate, ffn2_w_up, ffn2_w_down, final_norm, loss_target, m_ffn1_norm, m_ffn1_w_gate, m_ffn1_w_up, m_ffn1_w_down, m_mix_norm, m_w_in, m_conv_dw_w, m_conv_dw_b, m_conv_ln_g, m_conv_ln_b, m_conv_w_proj, m_attn_sinks, m_attn_w_o, m_gate_b, m_w_out, m_ffn2_norm, m_ffn2_w_gate, m_ffn2_w_up, m_ffn2_w_down, m_final_norm, v_ffn1_norm, v_ffn1_w_gate, v_ffn1_w_up, v_ffn1_w_down, v_mix_norm, v_w_in, v_conv_dw_w, v_conv_dw_b, v_conv_ln_g, v_conv_ln_b, v_conv_w_proj, v_attn_sinks, v_attn_w_o, v_gate_b, v_w_out, v_ffn2_norm, v_ffn2_w_gate, v_ffn2_w_up, v_ffn2_w_down, v_final_norm):
    weights = dict(ffn1_norm=ffn1_norm, ffn1_w_gate=ffn1_w_gate, ffn1_w_up=ffn1_w_up, ffn1_w_down=ffn1_w_down,
                   mix_norm=mix_norm, w_in=w_in, conv_dw_w=conv_dw_w, conv_dw_b=conv_dw_b, conv_ln_g=conv_ln_g,
                   conv_ln_b=conv_ln_b, conv_w_proj=conv_w_proj, attn_sinks=attn_sinks, attn_w_o=attn_w_o,
                   gate_b=gate_b, w_out=w_out, ffn2_norm=ffn2_norm, ffn2_w_gate=ffn2_w_gate, ffn2_w_up=ffn2_w_up,
                   ffn2_w_down=ffn2_w_down, final_norm=final_norm)
    m_in = dict(ffn1_norm=m_ffn1_norm, ffn1_w_gate=m_ffn1_w_gate, ffn1_w_up=m_ffn1_w_up, ffn1_w_down=m_ffn1_w_down,
                mix_norm=m_mix_norm, w_in=m_w_in, conv_dw_w=m_conv_dw_w, conv_dw_b=m_conv_dw_b,
                conv_ln_g=m_conv_ln_g, conv_ln_b=m_conv_ln_b, conv_w_proj=m_conv_w_proj, attn_sinks=m_attn_sinks,
                attn_w_o=m_attn_w_o, gate_b=m_gate_b, w_out=m_w_out, ffn2_norm=m_ffn2_norm,
                ffn2_w_gate=m_ffn2_w_gate, ffn2_w_up=m_ffn2_w_up, ffn2_w_down=m_ffn2_w_down, final_norm=m_final_norm)
    v_in = dict(ffn1_norm=v_ffn1_norm, ffn1_w_gate=v_ffn1_w_gate, ffn1_w_up=v_ffn1_w_up, ffn1_w_down=v_ffn1_w_down,
                mix_norm=v_mix_norm, w_in=v_w_in, conv_dw_w=v_conv_dw_w, conv_dw_b=v_conv_dw_b,
                conv_ln_g=v_conv_ln_g, conv_ln_b=v_conv_ln_b, conv_w_proj=v_conv_w_proj, attn_sinks=v_attn_sinks,
                attn_w_o=v_attn_w_o, gate_b=v_gate_b, w_out=v_w_out, ffn2_norm=v_ffn2_norm,
                ffn2_w_gate=v_ffn2_w_gate, ffn2_w_up=v_ffn2_w_up, ffn2_w_down=v_ffn2_w_down, final_norm=v_final_norm)
    names = list(weights)
    big = ["ffn1_w_gate", "ffn1_w_up", "ffn1_w_down", "w_in", "conv_w_proj", "attn_w_o", "w_out",
           "ffn2_w_gate", "ffn2_w_up", "ffn2_w_down"]

    xs = x[0]
    T, D = xs.shape
    KV = (w_in.shape[2] * N_CHIPS - 5 * D) // 2
    n_heads = D // HEAD_DIM
    my_chip = 2 * lax.axis_index("x") + lax.axis_index("y")
    place = jnp.stack([my_chip, lax.axis_index("c")]).astype(jnp.int32)

    placed = {k: place_shard(place, weights[k][0], BF16, f"place_{k}") for k in big}
    placed_dw = place_shard(place, conv_dw_w[0], F32, "place_conv_dw_w")
    first, later = big[:3], big[3:]
    wg1, wu1, wd1 = run_side(gather_side([placed[k] for k in first], []), "gather_ffn1")
    x1, h1, g1, u1, *gathered = ffn_fwd(x[0], ffn1_norm, wg1, wu1, wd1, "ffn1_fwd",
                                        side=gather_side([placed[k] for k in later], [placed_dw]))
    full = dict(zip(later + ["conv_dw_w"], gathered))
    wg2, wu2, wd2 = full["ffn2_w_gate"], full["ffn2_w_up"], full["ffn2_w_down"]
    w_glu = _cols_from_pieces(full["w_in"], 0, 2 * D)
    w_qkv = _cols_from_pieces(full["w_in"], 2 * D, 3 * D + 2 * KV)
    w_gate = _cols_from_pieces(full["w_in"], 3 * D + 2 * KV, 5 * D + 2 * KV)
    w_proj = full["conv_w_proj"].reshape(D, D)
    w_o = full["attn_w_o"].reshape(D, D)
    w_out_f = full["w_out"].reshape(D, D)
    dw_w = full["conv_dw_w"].transpose(1, 0, 2).reshape(CONV_WIDTH, D)
    dw_w = jnp.concatenate([dw_w, jnp.zeros((CONV_HALO - CONV_WIDTH, D), F32)], axis=0)
    cs, sn = _rope_tables(positions[0])
    fn_row = final_norm.reshape(1, D)

    h2, p_glu, p_gate, qr, kr, vb = mix_in_fwd(x1, mix_norm, w_glu, w_qkv, w_gate, cs, sn, "mix_in_fwd")
    c1, c3 = conv_fwd(p_glu, dw_w, conv_dw_b, conv_ln_g, conv_ln_b, "conv_fwd")
    o = attn_fwd(qr, kr, vb, attn_sinks, "attn_fwd")
    x2, conv_out, attn_out, merged = merge_fwd(x1, c3, o, p_gate, gate_b, w_proj, w_o, w_out_f, "merge_fwd")
    x3, h3, g2, u2 = ffn_fwd(x2, ffn2_norm, wg2, wu2, wd2, "ffn2_fwd")

    dx3, head_sums = loss_head(x3, fn_row, loss_target[0], "loss_head")
    dx2, dwg2, dwu2, dwd2, d_ffn2_norm = ffn_bwd(x2, ffn2_norm, h3, g2, u2, wg2, wu2, wd2, dx3, "ffn2_bwd")
    ffn2_grads = [dwg2, dwu2, dwd2]
    d_gates, d_conv_out, d_attn_out, d_o, dc1, merge_sums, *sibs_f2 = merge_bwd(
        dx2, p_gate, gate_b, conv_out, attn_out, c1, conv_ln_g, conv_ln_b, w_proj, w_o, w_out_f, "merge_bwd",
        side=exchange_siblings_side(ffn2_grads))
    d_w_out = matmul_tn(merged, dx2, "d_w_out")
    d_w_proj = matmul_tn(c3, d_conv_out, "d_conv_w_proj")
    d_w_o = matmul_tn(o, d_attn_out, "d_attn_w_o")
    wires_f2, owns_f2 = rs_to_wires(place, ffn2_grads, "ffn2", sibs=sibs_f2)
    d_glu, d_dw_w, *gots_f2 = conv_bwd(p_glu, dc1, dw_w, "conv_bwd", side=exchange_chips_side(wires_f2))
    dq, dk, dv, d_sinks = attn_bwd(qr, kr, vb, o, d_o, attn_sinks, "attn_bwd")
    d_qkv = rope_bwd(dq, dk, dv, cs, sn, "rope_bwd")
    dx1, d_mix_norm = mix_in_bwd([d_glu, d_qkv, d_gates], [w_glu, w_qkv, w_gate], x1, mix_norm, dx2, "mix_in_bwd")
    d_w_in = _pieces_from_groups([matmul_tn(h2, d_glu, "d_w_in_glu"), matmul_tn(h2, d_qkv, "d_w_in_qkv"),
                                  matmul_tn(h2, d_gates, "d_w_in_gate")])
    dwc = D // N_CHIPS
    mixer_grads = [d_w_in, d_w_proj.reshape(N_CHIPS, dwc, D), d_w_o.reshape(N_CHIPS, dwc, D),
                   d_w_out.reshape(N_CHIPS, dwc, D)]
    wires_m, owns_m = rs_to_wires(place, mixer_grads, "mixer")
    dx0, dwg1, dwu1, dwd1, d_ffn1_norm, *gots_m = ffn_bwd(xs, ffn1_norm, h1, g1, u1, wg1, wu1, wd1, dx1, "ffn1_bwd",
                                                          side=exchange_chips_side(wires_m))
    wires_l, owns_l = rs_to_wires(place, [dwg1, dwu1, dwd1], "ffn1")
    send_sems, recv_sems, wires_l, lands_l, _ = exchange_chips_start(wires_l, "rs_exchange_chips_ffn1_start")
    early_names = ["ffn2_w_gate", "ffn2_w_up", "ffn2_w_down", "w_in", "conv_w_proj", "attn_w_o", "w_out"]
    late_names = ["ffn1_w_gate", "ffn1_w_up", "ffn1_w_down"]
    reduced_early = rs_finish(place, owns_f2 + owns_m, list(gots_f2) + list(gots_m), "early")

    pad_row = lambda v: jnp.pad(v, ((0, 0), (0, D - v.shape[1])))
    small_rows = jnp.concatenate([
        d_ffn1_norm, d_mix_norm, merge_sums[2:3, :D], merge_sums[1:2, :D], merge_sums[1:2, D:],
        pad_row(d_sinks[0:1, :n_heads]), merge_sums[0:1, :D], merge_sums[0:1, D:], d_ffn2_norm,
        head_sums[0:1], head_sums[1:2], jnp.zeros((5, D), F32), d_dw_w], axis=0)
    small = allreduce_small(small_rows)
    loss = small[10, 0]
    grads = {"ffn1_norm": small[0:1], "mix_norm": small[1:2], "conv_dw_b": small[2:3], "conv_ln_g": small[3:4],
             "conv_ln_b": small[4:5], "attn_sinks": small[5:6, :n_heads],
             "gate_b": jnp.concatenate([small[6:7], small[7:8]], axis=1), "ffn2_norm": small[8:9],
             "final_norm": small[9:10]}
    grads["conv_dw_w"] = lax.dynamic_slice(small[16:16 + CONV_WIDTH], (0, my_chip * dwc), (CONV_WIDTH, dwc))
    grads.update(zip(early_names, reduced_early))

    deltas, new_m, new_v = {}, {}, {}

    def apply_adamw(k):
        shape = weights[k].shape
        g2d = grads[k].reshape(-1, shape[-1])
        grads[k] = g2d.reshape(shape)
        d, mn, vn = adamw(weights[k].reshape(g2d.shape), g2d, m_in[k].reshape(g2d.shape),
                          v_in[k].reshape(g2d.shape), f"adamw_{k}")
        deltas[k], new_m[k], new_v[k] = d.reshape(shape), mn.reshape(shape), vn.reshape(shape)
        return d

    last = None
    for k in names:
        if k not in late_names:
            last = apply_adamw(k)
    gots_l = exchange_chips_wait(send_sems, recv_sems, wires_l, lands_l, last, "rs_exchange_chips_ffn1_wait")
    grads.update(zip(late_names, rs_finish(place, owns_l, gots_l, "late")))
    for k in late_names:
        apply_adamw(k)

    return (loss, dx0[None], *[grads[k] for k in names], *[deltas[k] for k in names],
            *[new_m[k] for k in names], *[new_v[k] for k in names])
```

```python
import functools

import jax
import jax.numpy as jnp
from jax import lax
from jax.experimental import pallas as pl
from jax.experimental.pallas import tpu as pltpu

F32 = jnp.float32
BF16 = jnp.bfloat16
MESH = pl.DeviceIdType.MESH

HEAD_DIM = 64
WINDOW = 128
CONV_WIDTH = 31
CONV_HALO = 32
ROPE_THETA = 10000.0
EPS = 1e-6
LN_EPS = 1e-5
NEG_INF = -1e30
N_CHIPS = 4
N_DEV = 8

ADAM_LR = 0.001
ADAM_B1 = 0.9
ADAM_B2 = 0.999
ADAM_EPS = 1e-08
ADAM_WD = 0.01
ADAM_STEP = 10

TM_FFN = 512
TM_FFN_FWD = 1024
TM_ROW = 256
TK_TN = 1024
TR_ELT = 256
VMEM_LIMIT = 56 * 1024 * 1024

NT_DIMS = (((1,), (1,)), ((), ()))
TN_DIMS = (((0,), (0,)), ((), ()))


def _row_tile(rows, cap):
    for t in range(min(cap, rows), 15, -1):
        if rows % t == 0 and t % 16 == 0:
            return t
    return rows


def _params(sem):
    return pltpu.CompilerParams(dimension_semantics=sem, vmem_limit_bytes=VMEM_LIMIT)


def _dot(a, b):
    return jnp.dot(a, b, preferred_element_type=F32)


def _dot_nt(a, b):
    return lax.dot_general(a, b, NT_DIMS, preferred_element_type=F32)


def _dot_tn(a, b):
    return lax.dot_general(a, b, TN_DIMS, preferred_element_type=F32)


def _split_rows(dot, a, b):
    m = a.shape[0] // 2
    return jnp.concatenate([dot(a[:m], b), dot(a[m:], b)], axis=0)


def _sigmoid(x):
    return jax.nn.sigmoid(x)


def _rms_scale(xv):
    return lax.rsqrt(jnp.mean(xv * xv, axis=-1, keepdims=True) + EPS)


def _rms_bwd(xv, nw, dh):
    r = _rms_scale(xv)
    dn = dh * nw
    dx = r * dn - xv * (r * r * r) * jnp.mean(dn * xv, axis=-1, keepdims=True)
    dnw = jnp.sum(dh * (xv * r), axis=0, keepdims=True)
    return dx, dnw


def _silu_grad(z, s):
    return s * (1.0 + z * (1.0 - s))


HBM_SPEC = pl.BlockSpec(memory_space=pl.ANY)


def _call_hosting(body, side, *, grid, in_specs, out_specs, out_shape, scratch_shapes, operands, name, aliases=None):
    params = _params(("arbitrary",) * len(grid))
    aliases = dict(aliases or {})
    if side is None:
        return pl.pallas_call(body, name=name, grid=grid, in_specs=in_specs, out_specs=out_specs, out_shape=out_shape,
                              scratch_shapes=scratch_shapes, input_output_aliases=aliases,
                              compiler_params=params)(*operands)
    n_in, n_out, n_scr = len(in_specs), len(out_shape), len(scratch_shapes)
    s_in, s_out = len(side["inputs"]), len(side["out_shapes"])

    def at_step(end):
        hit = pl.program_id(0) == (grid[0] - 1 if end else 0)
        for a in range(1, len(grid)):
            hit &= pl.program_id(a) == (grid[a] - 1 if end else 0)
        return hit

    def hosted(*refs):
        b = n_in + s_in
        c = b + n_out
        d = c + s_out
        e = d + n_scr
        src, dst, sems = refs[n_in:b], refs[c:d], refs[e:]

        @pl.when(at_step(False))
        def _():
            side["start"](src, dst, sems)

        body(*refs[:n_in], *refs[b:c], *refs[d:e])

        @pl.when(at_step(True))
        def _():
            side["finish"](src, dst, sems)

    return pl.pallas_call(
        hosted, name=name, grid=grid, in_specs=list(in_specs) + [HBM_SPEC] * s_in,
        out_specs=list(out_specs) + [HBM_SPEC] * s_out, out_shape=list(out_shape) + list(side["out_shapes"]),
        scratch_shapes=list(scratch_shapes) + list(side["sems"]),
        input_output_aliases={**aliases, **{n_in + a: n_out + b for a, b in side["aliases"].items()}},
        compiler_params=params)(*operands, *side["inputs"])


def ffn_fwd(x, nw, wg, wu, wd, name, side=None):
    T, D = x.shape
    NP, _, Fs = wg.shape
    tm = min(TM_FFN_FWD, T)

    def body(x_ref, nw_ref, wg_ref, wu_ref, wd_ref, xo_ref, h_ref, g_ref, u_ref, acc_ref):
        j = pl.program_id(1)

        @pl.when(j == 0)
        def _():
            xv = x_ref[...]
            h_ref[...] = (xv * _rms_scale(xv) * nw_ref[...]).astype(BF16)
            acc_ref[...] = jnp.zeros_like(acc_ref)

        h = h_ref[...]
        g = _dot(h, wg_ref[...])
        u = _dot(h, wu_ref[...])
        a = (g * _sigmoid(g)) * u
        g_ref[...] = g.astype(BF16)
        u_ref[...] = u.astype(BF16)
        acc_ref[...] += _dot(a.astype(BF16), wd_ref[...])

        @pl.when(j == NP - 1)
        def _():
            xo_ref[...] = x_ref[...] + 0.5 * acc_ref[...]

    return _call_hosting(
        body, side, name=name, grid=(T // tm, NP),
        in_specs=[pl.BlockSpec((tm, D), lambda i, j: (i, 0)),
                  pl.BlockSpec((1, D), lambda i, j: (0, 0)),
                  pl.BlockSpec((None, D, Fs), lambda i, j: (j, 0, 0)),
                  pl.BlockSpec((None, D, Fs), lambda i, j: (j, 0, 0)),
                  pl.BlockSpec((None, Fs, D), lambda i, j: (j, 0, 0))],
        out_specs=[pl.BlockSpec((tm, D), lambda i, j: (i, 0)),
                   pl.BlockSpec((tm, D), lambda i, j: (i, 0)),
                   pl.BlockSpec((None, tm, Fs), lambda i, j: (j, i, 0)),
                   pl.BlockSpec((None, tm, Fs), lambda i, j: (j, i, 0))],
        out_shape=[jax.ShapeDtypeStruct((T, D), F32), jax.ShapeDtypeStruct((T, D), BF16),
                   jax.ShapeDtypeStruct((NP, T, Fs), BF16), jax.ShapeDtypeStruct((NP, T, Fs), BF16)],
        scratch_shapes=[pltpu.VMEM((tm, D), F32)],
        operands=(x, nw, wg, wu, wd))


def _ffn_bwd_piece(j, h, g, u, wg, wu, wd, dout, dh_in, dws_in, name, side, norm):
    T, D = h.shape
    NP, _, Fs = wg.shape
    tm = min(TM_FFN, T)
    n_in = 7 + (dh_in is not None) + (2 if norm else 0) + (3 if dws_in else 0)

    def body(*refs):
        h_ref, g_ref, u_ref, wg_ref, wu_ref, wd_ref, do_ref = refs[:7]
        dhin_ref = refs[7] if dh_in is not None else None
        dh_ref, dwg_ref, dwu_ref, dwd_ref = refs[n_in:n_in + 4]

        @pl.when(pl.program_id(0) == 0)
        def _():
            dwg_ref[...] = jnp.zeros_like(dwg_ref)
            dwu_ref[...] = jnp.zeros_like(dwu_ref)
            dwd_ref[...] = jnp.zeros_like(dwd_ref)
            if norm:
                refs[n_in + 4][...] = jnp.zeros_like(refs[n_in + 4])

        dob = (0.5 * do_ref[...]).astype(BF16)
        da = _split_rows(_dot_nt, dob, wd_ref[...])
        gf = g_ref[...].astype(F32)
        uf = u_ref[...].astype(F32)
        s = _sigmoid(gf)
        act = gf * s
        dg = (da * uf * _silu_grad(gf, s)).astype(BF16)
        du = (da * act).astype(BF16)
        a = (act * uf).astype(BF16)
        dh = _dot_nt(dg, wg_ref[...]) + _dot_nt(du, wu_ref[...])
        dh = dh if dhin_ref is None else dhin_ref[...] + dh
        if norm:
            x_ref, nw_ref = refs[7 + (dh_in is not None):9 + (dh_in is not None)]
            dxn, dnw = _rms_bwd(x_ref[...], nw_ref[...], dh)
            dh_ref[...] = do_ref[...] + dxn
            refs[n_in + 4][...] += dnw
        else:
            dh_ref[...] = dh
        hb = h_ref[...]
        dwg_ref[...] += _dot_tn(hb, dg)
        dwu_ref[...] += _dot_tn(hb, du)
        dwd_ref[...] += _dot_tn(a, dob)

    rows = pl.BlockSpec((tm, D), lambda i: (i, 0))
    piece = pl.BlockSpec((None, tm, Fs), lambda i: (j, i, 0))
    once = pl.Buffered(1)
    slot = lambda r, c: pl.BlockSpec((None, r, c), lambda i: (j, 0, 0), pipeline_mode=once)
    in_specs = [rows, piece, piece, slot(D, Fs), slot(D, Fs), slot(Fs, D), rows]
    operands = [h, g, u, wg, wu, wd, dout]
    aliases = {}
    if dh_in is not None:
        in_specs.append(rows)
        operands.append(dh_in)
    if norm:
        in_specs += [rows, pl.BlockSpec((1, D), lambda i: (0, 0))]
        operands += list(norm)
    if dws_in:
        aliases = {len(operands) + k: 1 + k for k in range(3)}
        in_specs += [HBM_SPEC] * 3
        operands += list(dws_in)
    out_specs = [rows, slot(D, Fs), slot(D, Fs), slot(Fs, D)]
    out_shape = [jax.ShapeDtypeStruct((T, D), F32), jax.ShapeDtypeStruct((NP, D, Fs), F32),
                 jax.ShapeDtypeStruct((NP, D, Fs), F32), jax.ShapeDtypeStruct((NP, Fs, D), F32)]
    if norm:
        out_specs.append(pl.BlockSpec((1, D), lambda i: (0, 0)))
        out_shape.append(jax.ShapeDtypeStruct((1, D), F32))
    return _call_hosting(body, side, name=name, grid=(T // tm,), in_specs=in_specs, out_specs=out_specs,
                         out_shape=out_shape, scratch_shapes=[], aliases=aliases, operands=tuple(operands))


def ffn_bwd(x, nw, h, g, u, wg, wu, wd, dout, name, side=None):
    NP = wg.shape[0]
    dh, dws, extra = None, None, []
    for j in range(NP):
        dh, *rest = _ffn_bwd_piece(j, h, g, u, wg, wu, wd, dout, dh, dws, f"{name}_{j}",
                                   side if j == 0 else None, (x, nw) if j == NP - 1 else None)
        dws, rest = rest[:3], rest[3:]
        if j == 0:
            extra = rest[1:] if NP == 1 else rest
    return (dh, *dws, rest[0], *extra)


def mix_in_fwd(x, nw, w_glu, w_qkv, w_gate, cs, sn, name):
    T, D = x.shape
    KV = (w_qkv.shape[1] - D) // 2
    tm = min(TM_ROW, T)

    def body(x_ref, nw_ref, wa_ref, wq_ref, wg_ref, cs_ref, sn_ref, h_ref, pa_ref, pg_ref, q_ref, k_ref, v_ref):
        xv = x_ref[...]
        h = (xv * _rms_scale(xv) * nw_ref[...]).astype(BF16)
        h_ref[...] = h
        pa_ref[...] = _dot(h, wa_ref[...])
        pg_ref[...] = _dot(h, wg_ref[...])
        qkv = _dot(h, wq_ref[...])
        cs_v, sn_v = cs_ref[...], sn_ref[...]
        q_ref[...] = _rope_chunks(qkv[:, :D], cs_v, sn_v, 1.0).astype(BF16)
        k_ref[...] = _rope_chunks(qkv[:, D:D + KV], cs_v, sn_v, 1.0).astype(BF16)
        v_ref[...] = qkv[:, D + KV:].astype(BF16)

    rows = lambda w: pl.BlockSpec((tm, w), lambda i: (i, 0))
    whole = lambda a: pl.BlockSpec(a.shape, lambda i: (0, 0))
    return pl.pallas_call(
        body, name=name, grid=(T // tm,),
        in_specs=[rows(D), whole(nw), whole(w_glu), whole(w_qkv), whole(w_gate), rows(128), rows(128)],
        out_specs=[rows(D), rows(2 * D), rows(2 * D), rows(D), rows(KV), rows(KV)],
        out_shape=[jax.ShapeDtypeStruct((T, D), BF16), jax.ShapeDtypeStruct((T, 2 * D), F32),
                   jax.ShapeDtypeStruct((T, 2 * D), F32), jax.ShapeDtypeStruct((T, D), BF16),
                   jax.ShapeDtypeStruct((T, KV), BF16), jax.ShapeDtypeStruct((T, KV), BF16)],
        compiler_params=_params(("parallel",)),
    )(x, nw, w_glu, w_qkv, w_gate, cs, sn)


def matmul_tn(lhs, rhs, name):
    T, K = lhs.shape
    N = rhs.shape[1]
    tk = min(TK_TN, T)

    def body(l_ref, r_ref, o_ref):
        @pl.when(pl.program_id(0) == 0)
        def _():
            o_ref[...] = jnp.zeros_like(o_ref)

        o_ref[...] += _dot_tn(l_ref[...].astype(BF16), r_ref[...].astype(BF16))

    return pl.pallas_call(
        body, name=name, grid=(T // tk,),
        in_specs=[pl.BlockSpec((tk, K), lambda t: (t, 0)), pl.BlockSpec((tk, N), lambda t: (t, 0))],
        out_specs=pl.BlockSpec((K, N), lambda t: (0, 0)),
        out_shape=jax.ShapeDtypeStruct((K, N), F32),
        compiler_params=_params(("arbitrary",)),
    )(lhs, rhs)


def mix_in_bwd(dps, ws, x, nw, dres, name):
    T, D = x.shape
    tm = min(TM_ROW, T)
    n = len(dps)

    def body(*refs):
        dp_refs, w_refs = refs[:n], refs[n:2 * n]
        x_ref, nw_ref, dr_ref, dx_ref, dnw_ref = refs[2 * n:]

        @pl.when(pl.program_id(0) == 0)
        def _():
            dnw_ref[...] = jnp.zeros_like(dnw_ref)

        dh = _dot_nt(dp_refs[0][...], w_refs[0][...])
        for k in range(1, n):
            dh += _dot_nt(dp_refs[k][...], w_refs[k][...])
        dxn, dnw = _rms_bwd(x_ref[...], nw_ref[...], dh)
        dx_ref[...] = dr_ref[...] + dxn
        dnw_ref[...] += dnw

    in_specs = [pl.BlockSpec((tm, dp.shape[1]), lambda i: (i, 0)) for dp in dps]
    in_specs += [pl.BlockSpec(w.shape, lambda i: (0, 0)) for w in ws]
    in_specs += [pl.BlockSpec((tm, D), lambda i: (i, 0)), pl.BlockSpec((1, D), lambda i: (0, 0)),
                 pl.BlockSpec((tm, D), lambda i: (i, 0))]
    return pl.pallas_call(
        body, name=name, grid=(T // tm,), in_specs=in_specs,
        out_specs=[pl.BlockSpec((tm, D), lambda i: (i, 0)), pl.BlockSpec((1, D), lambda i: (0, 0))],
        out_shape=[jax.ShapeDtypeStruct((T, D), F32), jax.ShapeDtypeStruct((1, D), F32)],
        compiler_params=_params(("arbitrary",)),
    )(*dps, *ws, x, nw, dres)


def _layernorm_stats(c1):
    mu = jnp.mean(c1, axis=-1, keepdims=True)
    xc = c1 - mu
    rstd = lax.rsqrt(jnp.mean(xc * xc, axis=-1, keepdims=True) + LN_EPS)
    return xc * rstd, rstd


def _shifted_copies(src_ref, dst_ref):
    rows = dst_ref.shape[1]
    for b in range(1, 8):
        dst_ref[b - 1] = src_ref[pl.ds(b, rows), :]


def _shifted_rows(src_ref, shifted_ref, start, rows, cols):
    a8, b = divmod(start, 8)
    if b == 0:
        return src_ref[pl.ds(8 * a8, rows), cols]
    return shifted_ref[b - 1, pl.ds(8 * a8, rows), cols]


def conv_fwd(p_glu, dw_w, dw_b, ln_g, ln_b, name):
    T, D2 = p_glu.shape
    D = D2 // 2
    tm = min(TM_ROW, T)
    hb = tm // CONV_HALO

    def body(a_ref, b_ref, ah_ref, bh_ref, w_ref, wb_ref, g_ref, be_ref, c1_ref, c3_ref, e_ref, es_ref):
        i = pl.program_id(0)
        halo = ah_ref[...] * _sigmoid(bh_ref[...])
        e_ref[pl.ds(0, CONV_HALO), :] = jnp.where(i > 0, halo, 0.0)
        e_ref[pl.ds(CONV_HALO, tm), :] = a_ref[...] * _sigmoid(b_ref[...])
        _shifted_copies(e_ref, es_ref)
        off = CONV_HALO - (CONV_WIDTH - 1)

        def strip(s, carry):
            cols = pl.ds(pl.multiple_of(s * 128, 128), 128)
            acc = jnp.zeros((tm, 128), F32) + wb_ref[:, cols]
            for k in range(CONV_WIDTH):
                acc += w_ref[pl.ds(k, 1), cols] * _shifted_rows(e_ref, es_ref, off + k, tm, cols)
            c1_ref[:, cols] = acc
            return carry

        lax.fori_loop(0, D // 128, strip, 0)
        xhat, _ = _layernorm_stats(c1_ref[...])
        c2 = xhat * g_ref[...] + be_ref[...]
        c3_ref[...] = (c2 * _sigmoid(c2)).astype(BF16)

    row = pl.BlockSpec((1, D), lambda i: (0, 0))
    return pl.pallas_call(
        body, name=name, grid=(T // tm,),
        in_specs=[pl.BlockSpec((tm, D), lambda i: (i, 0)), pl.BlockSpec((tm, D), lambda i: (i, 1)),
                  pl.BlockSpec((CONV_HALO, D), lambda i: (jnp.maximum(i * hb - 1, 0), 0)),
                  pl.BlockSpec((CONV_HALO, D), lambda i: (jnp.maximum(i * hb - 1, 0), 1)),
                  pl.BlockSpec((CONV_HALO, D), lambda i: (0, 0)), row, row, row],
        out_specs=[pl.BlockSpec((tm, D), lambda i: (i, 0)), pl.BlockSpec((tm, D), lambda i: (i, 0))],
        out_shape=[jax.ShapeDtypeStruct((T, D), F32), jax.ShapeDtypeStruct((T, D), BF16)],
        scratch_shapes=[pltpu.VMEM((tm + CONV_HALO, D), F32), pltpu.VMEM((7, tm + CONV_HALO - 8, D), F32)],
        compiler_params=_params(("parallel",)),
    )(p_glu, p_glu, p_glu, p_glu, dw_w, dw_b, ln_g, ln_b)


def conv_bwd(p_glu, dc1, dw_w, name, side=None):
    T, D2 = p_glu.shape
    D = D2 // 2
    tm = min(TM_ROW, T)
    hb = tm // CONV_HALO
    last = T // CONV_HALO - 1
    nblk = T // tm

    def body(a_ref, b_ref, ah_ref, bh_ref, d_ref, dn_ref, w_ref, dp_ref, dw_ref, e_ref, f_ref, es_ref, fs_ref):
        i = pl.program_id(0)

        @pl.when(i == 0)
        def _():
            dw_ref[...] = jnp.zeros_like(dw_ref)

        halo = ah_ref[...] * _sigmoid(bh_ref[...])
        e_ref[pl.ds(0, CONV_HALO), :] = jnp.where(i > 0, halo, 0.0)
        e_ref[pl.ds(CONV_HALO, tm), :] = a_ref[...] * _sigmoid(b_ref[...])
        f_ref[pl.ds(0, tm), :] = d_ref[...]
        f_ref[pl.ds(tm, CONV_HALO), :] = jnp.where(i < nblk - 1, dn_ref[...], 0.0)
        _shifted_copies(e_ref, es_ref)
        _shifted_copies(f_ref, fs_ref)
        off = CONV_HALO - (CONV_WIDTH - 1)

        def strip(s, carry):
            cols = pl.ds(pl.multiple_of(s * 128, 128), 128)
            d = d_ref[:, cols]
            dc0 = jnp.zeros((tm, 128), F32)
            for k in range(CONV_WIDTH):
                dw_ref[pl.ds(k, 1), cols] += jnp.sum(d * _shifted_rows(e_ref, es_ref, off + k, tm, cols),
                                                     axis=0, keepdims=True)
                dc0 += w_ref[pl.ds(k, 1), cols] * _shifted_rows(f_ref, fs_ref, CONV_WIDTH - 1 - k, tm, cols)
            a = a_ref[:, cols]
            sb = _sigmoid(b_ref[:, cols])
            dp_ref[:, cols] = (dc0 * sb).astype(BF16)
            dp_ref[:, pl.ds(pl.multiple_of(D + s * 128, 128), 128)] = (dc0 * a * sb * (1.0 - sb)).astype(BF16)
            return carry

        lax.fori_loop(0, D // 128, strip, 0)

    return _call_hosting(
        body, side, name=name, grid=(nblk,),
        in_specs=[pl.BlockSpec((tm, D), lambda i: (i, 0)), pl.BlockSpec((tm, D), lambda i: (i, 1)),
                  pl.BlockSpec((CONV_HALO, D), lambda i: (jnp.maximum(i * hb - 1, 0), 0)),
                  pl.BlockSpec((CONV_HALO, D), lambda i: (jnp.maximum(i * hb - 1, 0), 1)),
                  pl.BlockSpec((tm, D), lambda i: (i, 0)),
                  pl.BlockSpec((CONV_HALO, D), lambda i: (jnp.minimum((i + 1) * hb, last), 0)),
                  pl.BlockSpec((CONV_HALO, D), lambda i: (0, 0))],
        out_specs=[pl.BlockSpec((tm, D2), lambda i: (i, 0)), pl.BlockSpec((CONV_HALO, D), lambda i: (0, 0))],
        out_shape=[jax.ShapeDtypeStruct((T, D2), BF16), jax.ShapeDtypeStruct((CONV_HALO, D), F32)],
        scratch_shapes=[pltpu.VMEM((tm + CONV_HALO, D), F32), pltpu.VMEM((tm + CONV_HALO, D), F32),
                        pltpu.VMEM((7, tm + CONV_HALO - 8, D), F32), pltpu.VMEM((7, tm + CONV_HALO - 8, D), F32)],
        operands=(p_glu, p_glu, p_glu, p_glu, dc1, dc1, dw_w))


def _rot_half(x):
    lane = lax.broadcasted_iota(jnp.int32, x.shape, 1)
    first = (lane % HEAD_DIM) < HEAD_DIM // 2
    return jnp.where(first, pltpu.roll(x, 128 - HEAD_DIM // 2, 1), pltpu.roll(x, HEAD_DIM // 2, 1))


def _rope_chunks(x, cs, sn, sign):
    outs = []
    for c in range(x.shape[1] // 128):
        xc = x[:, c * 128:(c + 1) * 128]
        outs.append(xc * cs + sign * (_rot_half(xc) * sn))
    return outs[0] if len(outs) == 1 else jnp.concatenate(outs, axis=1)


def rope_bwd(dq, dk, dv, cs, sn, name):
    T, D = dq.shape
    KV = dk.shape[1]
    tm = min(TM_ROW, T)

    def body(dq_ref, dk_ref, dv_ref, cs_ref, sn_ref, o_ref):
        cs_v, sn_v = cs_ref[...], sn_ref[...]
        o_ref[:, pl.ds(0, D)] = _rope_chunks(dq_ref[...], cs_v, sn_v, -1.0).astype(BF16)
        o_ref[:, pl.ds(D, KV)] = _rope_chunks(dk_ref[...], cs_v, sn_v, -1.0).astype(BF16)
        o_ref[:, pl.ds(D + KV, KV)] = dv_ref[...].astype(BF16)

    tab = pl.BlockSpec((tm, 128), lambda i: (i, 0))
    return pl.pallas_call(
        body, name=name, grid=(T // tm,),
        in_specs=[pl.BlockSpec((tm, D), lambda i: (i, 0)), pl.BlockSpec((tm, KV), lambda i: (i, 0)),
                  pl.BlockSpec((tm, KV), lambda i: (i, 0)), tab, tab],
        out_specs=pl.BlockSpec((tm, D + 2 * KV), lambda i: (i, 0)),
        out_shape=jax.ShapeDtypeStruct((T, D + 2 * KV), BF16),
        compiler_params=_params(("parallel",)),
    )(dq, dk, dv, cs, sn)


def _lane_lo():
    return lax.broadcasted_iota(jnp.int32, (1, 128), 1) < HEAD_DIM


def _band_mask(i, reps):
    shape = (reps * WINDOW, 2 * WINDOW)
    qi = lax.broadcasted_iota(jnp.int32, shape, 0) % WINDOW
    cj = lax.broadcasted_iota(jnp.int32, shape, 1)
    rel = qi - cj + WINDOW
    return (rel >= 0) & (rel < WINDOW) & ((i > 0) | (cj >= WINDOW))


def _stack_pairs(ref, first, n):
    parts = [ref[:, pl.ds((first + p) * 128, 128)] for p in range(n)]
    return parts[0] if n == 1 else jnp.concatenate(parts, axis=0)


def _pair_rows(n):
    return lax.broadcasted_iota(jnp.int32, (n * WINDOW, 1), 0) // WINDOW


def _per_pair_column(values, n):
    rows = _pair_rows(n)
    col = jnp.zeros((n * WINDOW, 1), F32) + values[0]
    for p in range(1, n):
        col = jnp.where(rows == p, values[p], col)
    return col


def _kv_lo_hi(x2, g):
    pair, half = divmod(g, 2)
    lo = _lane_lo()
    xg = x2[:, pair * 128:(pair + 1) * 128].astype(F32)
    xg = jnp.where(lo if half == 0 else ~lo, xg, 0.0)
    sw = pltpu.roll(xg, HEAD_DIM, 1)
    x_lo, x_hi = (xg, sw) if half == 0 else (sw, xg)
    return x_lo.astype(BF16), x_hi.astype(BF16)


def _softmax_sink(s, allowed, sink):
    s = jnp.where(allowed, s * (HEAD_DIM ** -0.5), NEG_INF)
    m = jnp.maximum(jnp.max(s, axis=-1, keepdims=True), sink)
    p = jnp.exp(s - m)
    es = jnp.exp(sink - m)
    inv = 1.0 / (jnp.sum(p, axis=-1, keepdims=True) + es)
    return p * inv, es * inv


def attn_fwd(qr, kr, vb, sinks, name):
    T, D = qr.shape
    KV = kr.shape[1]
    n_kv = KV // HEAD_DIM
    group = (D // HEAD_DIM) // n_kv
    nb = T // WINDOW

    npair = group // 2

    def body(sink_ref, q_ref, kp_ref, kc_ref, vp_ref, vc_ref, o_ref):
        i = pl.program_id(0)
        allowed = _band_mask(i, npair)
        k2 = jnp.concatenate([kp_ref[...], kc_ref[...]], axis=0)
        v2 = jnp.concatenate([vp_ref[...], vc_ref[...]], axis=0)
        outs = [None] * (D // 128)
        for g in range(n_kv):
            k_lo, k_hi = _kv_lo_hi(k2, g)
            v_lo, v_hi = _kv_lo_hi(v2, g)
            first = (g * group) // 2
            q = _stack_pairs(q_ref, first, npair)
            sink_e = _per_pair_column([sink_ref[0, g * group + 2 * p] for p in range(npair)], npair)
            sink_o = _per_pair_column([sink_ref[0, g * group + 2 * p + 1] for p in range(npair)], npair)
            pe, _ = _softmax_sink(_dot_nt(q, k_lo), allowed, sink_e)
            po, _ = _softmax_sink(_dot_nt(q, k_hi), allowed, sink_o)
            o = _dot(pe.astype(BF16), v_lo) + _dot(po.astype(BF16), v_hi)
            for p in range(npair):
                outs[first + p] = o[p * WINDOW:(p + 1) * WINDOW]
        o_ref[...] = jnp.concatenate(outs, axis=1).astype(BF16)

    prev = lambda i: (jnp.maximum(i - 1, 0), 0)
    cur = lambda i: (i, 0)
    return pl.pallas_call(
        body, name=name, grid=(nb,),
        in_specs=[pl.BlockSpec(memory_space=pltpu.SMEM),
                  pl.BlockSpec((WINDOW, D), cur),
                  pl.BlockSpec((WINDOW, KV), prev), pl.BlockSpec((WINDOW, KV), cur),
                  pl.BlockSpec((WINDOW, KV), prev), pl.BlockSpec((WINDOW, KV), cur)],
        out_specs=pl.BlockSpec((WINDOW, D), cur),
        out_shape=jax.ShapeDtypeStruct((T, D), BF16),
        compiler_params=_params(("parallel",)),
    )(sinks, qr, kr, kr, vb, vb)


def attn_bwd(qr, kr, vb, o, do, sinks, name):
    T, D = qr.shape
    KV = kr.shape[1]
    n_heads = D // HEAD_DIM
    n_kv = KV // HEAD_DIM
    group = n_heads // n_kv
    nb = T // WINDOW
    npair = group // 2
    scale = HEAD_DIM ** -0.5

    def body(sink_ref, q_ref, kp_ref, kc_ref, vp_ref, vc_ref, o_ref, do_ref,
             dq_ref, dk_ref, dv_ref, ds_ref, ck_ref, cv_ref):
        i = pl.program_id(0)
        lo = _lane_lo()

        @pl.when(i == 0)
        def _():
            ck_ref[...] = jnp.zeros_like(ck_ref)
            cv_ref[...] = jnp.zeros_like(cv_ref)
            ds_ref[...] = jnp.zeros_like(ds_ref)

        @pl.when(i < nb)
        def _():
            allowed = _band_mask(i, npair)
            rows = _pair_rows(npair)
            k2 = jnp.concatenate([kp_ref[...], kc_ref[...]], axis=0)
            v2 = jnp.concatenate([vp_ref[...], vc_ref[...]], axis=0)
            lane = lax.broadcasted_iota(jnp.int32, (1, 128), 1)
            dsink = jnp.zeros((1, 128), F32)
            dq_out = [None] * (D // 128)
            dk_pairs = [jnp.zeros((2 * WINDOW, 128), F32) for _ in range(KV // 128)]
            dv_pairs = [jnp.zeros((2 * WINDOW, 128), F32) for _ in range(KV // 128)]
            for g in range(n_kv):
                k_lo, k_hi = _kv_lo_hi(k2, g)
                v_lo, v_hi = _kv_lo_hi(v2, g)
                first = (g * group) // 2
                q = _stack_pairs(q_ref, first, npair)
                dop = _stack_pairs(do_ref, first, npair)
                dd = dop.astype(F32) * _stack_pairs(o_ref, first, npair).astype(F32)
                dq = jnp.zeros((npair * WINDOW, 128), F32)
                dkg = jnp.zeros((2 * WINDOW, 128), F32)
                dvg = jnp.zeros((2 * WINDOW, 128), F32)
                for parity, k_h, v_h, sel in ((0, k_lo, v_lo, lo), (1, k_hi, v_hi, ~lo)):
                    heads = [g * group + 2 * p + parity for p in range(npair)]
                    sink = _per_pair_column([sink_ref[0, h] for h in heads], npair)
                    p_, ps = _softmax_sink(_dot_nt(q, k_h), allowed, sink)
                    delta = jnp.sum(jnp.where(sel, dd, 0.0), axis=-1, keepdims=True)
                    dsc = (p_ * (_dot_nt(dop, v_h) - delta)).astype(BF16)
                    sd = -ps * delta
                    for p, h in enumerate(heads):
                        dsink += jnp.where(lane == h, jnp.sum(jnp.where(rows == p, sd, 0.0)), 0.0)
                    dq += _dot(dsc, k_h)
                    dkg += jnp.where(sel, _dot_tn(dsc, q), 0.0)
                    dvg += jnp.where(sel, _dot_tn(p_.astype(BF16), dop), 0.0)
                for p in range(npair):
                    dq_out[first + p] = dq[p * WINDOW:(p + 1) * WINDOW]
                pair, half = divmod(g, 2)
                keep = lo if half == 0 else ~lo
                dk_pairs[pair] += jnp.where(keep, dkg + pltpu.roll(dkg, HEAD_DIM, 1), 0.0) * scale
                dv_pairs[pair] += jnp.where(keep, dvg + pltpu.roll(dvg, HEAD_DIM, 1), 0.0)
            dq_ref[...] = jnp.concatenate(dq_out, axis=1) * scale
            dk2 = dk_pairs[0] if len(dk_pairs) == 1 else jnp.concatenate(dk_pairs, axis=1)
            dv2 = dv_pairs[0] if len(dv_pairs) == 1 else jnp.concatenate(dv_pairs, axis=1)
            dk_ref[...] = ck_ref[...] + dk2[:WINDOW]
            dv_ref[...] = cv_ref[...] + dv2[:WINDOW]
            ck_ref[...] = dk2[WINDOW:]
            cv_ref[...] = dv2[WINDOW:]
            ds_ref[pl.ds(0, 1), :] += dsink

        @pl.when(i == nb)
        def _():
            dk_ref[...] = ck_ref[...]
            dv_ref[...] = cv_ref[...]

    prev = lambda i: (jnp.maximum(i - 1, 0), 0)
    cur = lambda i: (jnp.minimum(i, nb - 1), 0)
    prevc = lambda i: (jnp.maximum(jnp.minimum(i, nb - 1) - 1, 0), 0)
    return pl.pallas_call(
        body, name=name, grid=(nb + 1,),
        in_specs=[pl.BlockSpec(memory_space=pltpu.SMEM),
                  pl.BlockSpec((WINDOW, D), cur),
                  pl.BlockSpec((WINDOW, KV), prevc), pl.BlockSpec((WINDOW, KV), cur),
                  pl.BlockSpec((WINDOW, KV), prevc), pl.BlockSpec((WINDOW, KV), cur),
                  pl.BlockSpec((WINDOW, D), cur), pl.BlockSpec((WINDOW, D), cur)],
        out_specs=[pl.BlockSpec((WINDOW, D), cur), pl.BlockSpec((WINDOW, KV), prev),
                   pl.BlockSpec((WINDOW, KV), prev), pl.BlockSpec((8, 128), lambda i: (0, 0))],
        out_shape=[jax.ShapeDtypeStruct((T, D), F32), jax.ShapeDtypeStruct((T, KV), F32),
                   jax.ShapeDtypeStruct((T, KV), F32), jax.ShapeDtypeStruct((8, 128), F32)],
        scratch_shapes=[pltpu.VMEM((WINDOW, KV), F32), pltpu.VMEM((WINDOW, KV), F32)],
        compiler_params=_params(("arbitrary",)),
    )(sinks, qr, kr, kr, vb, vb, o, do)


def merge_fwd(x, c3, o, p_gate, gate_b, w_proj, w_o, w_out, name):
    T, D = x.shape
    tm = min(TM_ROW, T)

    def body(x_ref, c3_ref, o_ref, gc_ref, ga_ref, bc_ref, ba_ref, wp_ref, wo_ref, wout_ref,
             xo_ref, co_ref, ao_ref, mg_ref):
        conv_out = _dot(c3_ref[...], wp_ref[...])
        attn_out = _dot(o_ref[...], wo_ref[...])
        merged = (_sigmoid(gc_ref[...] + bc_ref[...]) * conv_out
                  + _sigmoid(ga_ref[...] + ba_ref[...]) * attn_out).astype(BF16)
        co_ref[...] = conv_out.astype(BF16)
        ao_ref[...] = attn_out.astype(BF16)
        mg_ref[...] = merged
        xo_ref[...] = x_ref[...] + _dot(merged, wout_ref[...])

    blk = lambda j: pl.BlockSpec((tm, D), lambda i: (i, j))
    row = lambda j: pl.BlockSpec((1, D), lambda i: (0, j))
    mat = pl.BlockSpec((D, D), lambda i: (0, 0))
    return pl.pallas_call(
        body, name=name, grid=(T // tm,),
        in_specs=[blk(0), blk(0), blk(0), blk(0), blk(1), row(0), row(1), mat, mat, mat],
        out_specs=[blk(0), blk(0), blk(0), blk(0)],
        out_shape=[jax.ShapeDtypeStruct((T, D), F32)] + [jax.ShapeDtypeStruct((T, D), BF16)] * 3,
        compiler_params=_params(("parallel",)),
    )(x, c3, o, p_gate, p_gate, gate_b, gate_b, w_proj, w_o, w_out)


def merge_bwd(dx, p_gate, gate_b, conv_out, attn_out, c1, ln_g, ln_b, w_proj, w_o, w_out, name, side=None):
    T, D = dx.shape
    tm = min(TM_ROW, T)

    def body(dx_ref, gc_ref, ga_ref, bc_ref, ba_ref, co_ref, ao_ref, c1_ref, g_ref, be_ref,
             wp_ref, wo_ref, wout_ref, dgt_ref, dco_ref, dao_ref, do_ref, dc1_ref, sm_ref):
        @pl.when(pl.program_id(0) == 0)
        def _():
            sm_ref[...] = jnp.zeros_like(sm_ref)

        dm = _dot_nt(dx_ref[...].astype(BF16), wout_ref[...])
        sc = _sigmoid(gc_ref[...] + bc_ref[...])
        sa = _sigmoid(ga_ref[...] + ba_ref[...])
        dco = (dm * sc).astype(BF16)
        dao = (dm * sa).astype(BF16)
        dgc = dm * co_ref[...].astype(F32) * sc * (1.0 - sc)
        dga = dm * ao_ref[...].astype(F32) * sa * (1.0 - sa)
        dgt_ref[:, pl.ds(0, D)] = dgc.astype(BF16)
        dgt_ref[:, pl.ds(D, D)] = dga.astype(BF16)
        dco_ref[...] = dco
        dao_ref[...] = dao
        do_ref[...] = _dot_nt(dao, wo_ref[...]).astype(BF16)
        dc3 = _dot_nt(dco, wp_ref[...])
        xhat, rstd = _layernorm_stats(c1_ref[...])
        c2 = xhat * g_ref[...] + be_ref[...]
        dc2 = dc3 * _silu_grad(c2, _sigmoid(c2))
        dxh = dc2 * g_ref[...]
        dc1 = rstd * (dxh - jnp.mean(dxh, axis=-1, keepdims=True)
                      - xhat * jnp.mean(dxh * xhat, axis=-1, keepdims=True))
        dc1_ref[...] = dc1
        colsum = lambda v: jnp.sum(v, axis=0, keepdims=True)
        for r, (left, right) in enumerate(((dgc, dga), (dc2 * xhat, dc2), (dc1, None))):
            sm_ref[pl.ds(r, 1), pl.ds(0, D)] += colsum(left)
            if right is not None:
                sm_ref[pl.ds(r, 1), pl.ds(D, D)] += colsum(right)

    blk = lambda j: pl.BlockSpec((tm, D), lambda i: (i, j))
    row = lambda j: pl.BlockSpec((1, D), lambda i: (0, j))
    mat = pl.BlockSpec((D, D), lambda i: (0, 0))
    return _call_hosting(
        body, side, name=name, grid=(T // tm,),
        in_specs=[blk(0), blk(0), blk(1), row(0), row(1), blk(0), blk(0), blk(0), row(0), row(0), mat, mat, mat],
        out_specs=[pl.BlockSpec((tm, 2 * D), lambda i: (i, 0)), blk(0), blk(0), blk(0), blk(0),
                   pl.BlockSpec((8, 2 * D), lambda i: (0, 0))],
        out_shape=[jax.ShapeDtypeStruct((T, 2 * D), BF16)] + [jax.ShapeDtypeStruct((T, D), BF16)] * 3
                  + [jax.ShapeDtypeStruct((T, D), F32), jax.ShapeDtypeStruct((8, 2 * D), F32)],
        scratch_shapes=[],
        operands=(dx, p_gate, p_gate, gate_b, gate_b, conv_out, attn_out, c1, ln_g, ln_b, w_proj, w_o, w_out))


def loss_head(x, nw, target, name):
    T, D = x.shape
    tm = min(TM_ROW, T)

    def body(x_ref, nw_ref, t_ref, dx_ref, sm_ref):
        @pl.when(pl.program_id(0) == 0)
        def _():
            sm_ref[...] = jnp.zeros_like(sm_ref)

        xv = x_ref[...]
        err = xv * _rms_scale(xv) * nw_ref[...] - t_ref[...]
        loss = 0.5 * jnp.sum(jnp.mean(err * err, axis=-1, keepdims=True))
        dxn, dnw = _rms_bwd(xv, nw_ref[...], err * (1.0 / D))
        dx_ref[...] = dxn
        sm_ref[pl.ds(0, 1), :] += dnw
        sm_ref[pl.ds(1, 1), :] += jnp.zeros((1, D), F32) + loss

    return pl.pallas_call(
        body, name=name, grid=(T // tm,),
        in_specs=[pl.BlockSpec((tm, D), lambda i: (i, 0)), pl.BlockSpec((1, D), lambda i: (0, 0)),
                  pl.BlockSpec((tm, D), lambda i: (i, 0))],
        out_specs=[pl.BlockSpec((tm, D), lambda i: (i, 0)), pl.BlockSpec((8, D), lambda i: (0, 0))],
        out_shape=[jax.ShapeDtypeStruct((T, D), F32), jax.ShapeDtypeStruct((8, D), F32)],
        compiler_params=_params(("arbitrary",)),
    )(x, nw, target)


def adamw(w, g, m, v, name):
    R, C = w.shape
    tr = _row_tile(R, TR_ELT)

    def body(w_ref, g_ref, m_ref, v_ref, d_ref, mo_ref, vo_ref):
        gv = g_ref[...]
        mn = ADAM_B1 * m_ref[...] + (1.0 - ADAM_B1) * gv
        vn = ADAM_B2 * v_ref[...] + (1.0 - ADAM_B2) * (gv * gv)
        m_hat = mn / (1.0 - ADAM_B1 ** ADAM_STEP)
        v_hat = vn / (1.0 - ADAM_B2 ** ADAM_STEP)
        d_ref[...] = -ADAM_LR * (m_hat / (jnp.sqrt(v_hat) + ADAM_EPS) + ADAM_WD * w_ref[...])
        mo_ref[...] = mn
        vo_ref[...] = vn

    spec = pl.BlockSpec((tr, C), lambda i: (i, 0))
    return pl.pallas_call(
        body, name=name, grid=(R // tr,), in_specs=[spec] * 4, out_specs=[spec] * 3,
        out_shape=[jax.ShapeDtypeStruct((R, C), F32)] * 3,
        compiler_params=_params(("parallel",)),
    )(w, g, m, v)


def _place():
    return lax.axis_index("x"), lax.axis_index("y"), lax.axis_index("c")


def place_shard(place, w, dtype, name):
    R, C = w.shape
    tr = _row_tile(R, TR_ELT)

    def body(pc_ref, w_ref, o_ref):
        o_ref[...] = w_ref[...].astype(dtype)

    return pl.pallas_call(
        body, name=name,
        grid_spec=pltpu.PrefetchScalarGridSpec(
            num_scalar_prefetch=1, grid=(R // tr,),
            in_specs=[pl.BlockSpec((tr, C), lambda r, pc: (r, 0))],
            out_specs=pl.BlockSpec((None, tr, C), lambda r, pc: (pc[0], r, 0))),
        out_shape=jax.ShapeDtypeStruct((N_CHIPS, R, C), dtype),
        compiler_params=_params(("arbitrary",)),
    )(place, w)


def gather_side(shards, small):
    n, ns = len(shards), len(small)

    def ici_copy(dst, sems, k, j, x, y, c, sending):
        px, py = x ^ (j >> 1), y ^ (j & 1)
        slot = 2 * x + y if sending else 2 * px + py
        half = dst[k].shape[1] // 2
        part = dst[k].at[slot, pl.ds(c * half, half)] if k < n else dst[k].at[slot]
        return pltpu.make_async_remote_copy(part, part, sems[0].at[3 * k + j - 1], sems[1].at[3 * k + j - 1],
                                            device_id=(px, py, c), device_id_type=MESH)

    def d2d_copy(dst, sems, k, j, x, y, c, sending):
        half = dst[k].shape[1] // 2
        part = dst[k].at[2 * (x ^ (j >> 1)) + (y ^ (j & 1)), pl.ds((c if sending else 1 - c) * half, half)]
        return pltpu.make_async_remote_copy(part, part, sems[2].at[3 * k + j - 1], sems[3].at[3 * k + j - 1],
                                            device_id=(x, y, 1 - c), device_id_type=MESH)

    def start(src, dst, sems):
        x, y, c = _place()
        for k in range(n + ns):
            for j in (1, 2, 3):
                ici_copy(dst, sems, k, j, x, y, c, True).start()

    def finish(src, dst, sems):
        x, y, c = _place()
        for k in range(n + ns):
            for j in (1, 2, 3):
                ici_copy(dst, sems, k, j, x, y, c, False).wait_recv()
                if k < n:
                    d2d_copy(dst, sems, k, j, x, y, c, True).start()
        for k in range(n):
            for j in (1, 2, 3):
                d2d_copy(dst, sems, k, j, x, y, c, False).wait_recv()
        for k in range(n + ns):
            for j in (1, 2, 3):
                ici_copy(dst, sems, k, j, x, y, c, True).wait_send()
                if k < n:
                    d2d_copy(dst, sems, k, j, x, y, c, True).wait_send()

    arrays = list(shards) + list(small)
    return dict(inputs=arrays, out_shapes=[jax.ShapeDtypeStruct(a.shape, a.dtype) for a in arrays],
                aliases={k: k for k in range(n + ns)},
                sems=[pltpu.SemaphoreType.DMA((3 * (n + ns),)), pltpu.SemaphoreType.DMA((3 * (n + ns),)),
                      pltpu.SemaphoreType.DMA((3 * n,)), pltpu.SemaphoreType.DMA((3 * n,))],
                start=start, finish=finish)


def run_side(side, name):
    n_in, n_out = len(side["inputs"]), len(side["out_shapes"])

    def body(*refs):
        src, dst, sems = refs[:n_in], refs[n_in:n_in + n_out], refs[n_in + n_out:]
        side["start"](src, dst, sems)
        side["finish"](src, dst, sems)

    return pl.pallas_call(
        body, name=name, in_specs=[HBM_SPEC] * n_in, out_specs=[HBM_SPEC] * n_out,
        out_shape=side["out_shapes"], input_output_aliases=side["aliases"], scratch_shapes=side["sems"],
    )(*side["inputs"])


def allreduce_small(block):
    R, C = block.shape

    def body(x_ref, out_ref, all_ref, send_sems, recv_sems, local_sem):
        x, y, c = _place()
        me, sibling = (x, y, c), (x, y, 1 - c)
        chips = [(1 - x, y), (x, 1 - y), (1 - x, 1 - y)]

        def slot(px, py, pc):
            return all_ref.at[4 * px + 2 * py + pc]

        def copy(k, block_of, to, src=None):
            return pltpu.make_async_remote_copy(
                src_ref=slot(*block_of) if src is None else src, dst_ref=slot(*block_of),
                send_sem=send_sems.at[k], recv_sem=recv_sems.at[k], device_id=to, device_id_type=MESH)

        mine = pltpu.make_async_copy(x_ref, slot(*me), local_sem)
        mine.start()
        first = [copy(0, me, sibling, src=x_ref)]
        first += [copy(1 + j, me, (*chip, c), src=x_ref) for j, chip in enumerate(chips)]
        for cp in first:
            cp.start()
        passed = [copy(4 + j, (*chip, c), sibling) for j, chip in enumerate(chips)]
        for j, chip in enumerate(chips):
            copy(1 + j, (*chip, c), me).wait_recv()
            passed[j].start()
        copy(0, sibling, me).wait_recv()
        for j, chip in enumerate(chips):
            copy(4 + j, (*chip, 1 - c), me).wait_recv()
        for cp in first + passed:
            cp.wait_send()
        mine.wait()
        total = all_ref[0]
        for d in range(1, N_DEV):
            total = total + all_ref[d]
        out_ref[...] = total

    return pl.pallas_call(
        body, name="allreduce_small",
        in_specs=[pl.BlockSpec(memory_space=pltpu.VMEM)], out_specs=pl.BlockSpec(memory_space=pltpu.VMEM),
        out_shape=jax.ShapeDtypeStruct((R, C), F32),
        scratch_shapes=[pltpu.VMEM((N_DEV, R, C), F32), pltpu.SemaphoreType.DMA((7,)),
                        pltpu.SemaphoreType.DMA((7,)), pltpu.SemaphoreType.DMA],
        compiler_params=pltpu.CompilerParams(vmem_limit_bytes=VMEM_LIMIT),
    )(block)


def exchange_siblings_side(grads):
    n = len(grads)

    def copies(src, dst, sems):
        x, y, c = _place()
        for k in range(n):
            half = src[k].shape[1] // 2
            yield pltpu.make_async_remote_copy(src[k].at[:, pl.ds((1 - c) * half, half)], dst[k],
                                               sems[0].at[k], sems[1].at[k],
                                               device_id=(x, y, 1 - c), device_id_type=MESH)

    def start(src, dst, sems):
        for cp in copies(src, dst, sems):
            cp.start()

    def finish(src, dst, sems):
        for cp in copies(src, dst, sems):
            cp.wait()

    return dict(inputs=list(grads), aliases={},
                out_shapes=[jax.ShapeDtypeStruct((N_CHIPS, g.shape[1] // 2, g.shape[2]), F32) for g in grads],
                sems=[pltpu.SemaphoreType.DMA((n,)), pltpu.SemaphoreType.DMA((n,))], start=start, finish=finish)


def rs_chip_sum(place, grad, sib, name):
    NP, R, C = grad.shape
    half = R // 2
    tr = _row_tile(half, TR_ELT)
    nr = half // tr

    def body(pc_ref, g_ref, s_ref, wire_ref, own_ref):
        q = pl.program_id(1)
        total = g_ref[...] + s_ref[...]
        wire_ref[...] = total.astype(BF16)

        @pl.when(q == pc_ref[0])
        def _():
            own_ref[...] = total

    return pl.pallas_call(
        body, name=name,
        grid_spec=pltpu.PrefetchScalarGridSpec(
            num_scalar_prefetch=1, grid=(nr, NP),
            in_specs=[pl.BlockSpec((None, tr, C), lambda r, q, pc: (q, pc[1] * nr + r, 0)),
                      pl.BlockSpec((None, tr, C), lambda r, q, pc: (q, r, 0))],
            out_specs=[pl.BlockSpec((None, tr, C), lambda r, q, pc: (q, r, 0)),
                       pl.BlockSpec((tr, C), lambda r, q, pc: (r, 0))]),
        out_shape=[jax.ShapeDtypeStruct((NP, half, C), BF16), jax.ShapeDtypeStruct((half, C), F32)],
        compiler_params=_params(("arbitrary", "arbitrary")),
    )(place, grad, sib)


def exchange_chips_side(wires):
    n = len(wires)

    def copies(src, dst, sems):
        x, y, c = _place()
        for k in range(n):
            for j in (1, 2, 3):
                qx, qy = x ^ (j >> 1), y ^ (j & 1)
                yield pltpu.make_async_remote_copy(src[k].at[2 * qx + qy], dst[k].at[2 * x + y],
                                                   sems[0].at[3 * k + j - 1], sems[1].at[3 * k + j - 1],
                                                   device_id=(qx, qy, c), device_id_type=MESH)

    def start(src, dst, sems):
        for cp in copies(src, dst, sems):
            cp.start()

    def finish(src, dst, sems):
        for cp in copies(src, dst, sems):
            cp.wait()

    return dict(inputs=list(wires), out_shapes=[jax.ShapeDtypeStruct(w.shape, BF16) for w in wires], aliases={},
                sems=[pltpu.SemaphoreType.DMA((3 * n,)), pltpu.SemaphoreType.DMA((3 * n,))],
                start=start, finish=finish)


SEM_SPEC = pl.BlockSpec(memory_space=pltpu.SEMAPHORE)


def exchange_chips_start(wires, name):
    n = len(wires)
    side = exchange_chips_side(wires)

    def body(*refs):
        src, land, sems = refs[:n], refs[n:2 * n], refs[2 * n:2 * n + 2]
        side["start"](src, land, sems)
        refs[-1][...] = jnp.zeros_like(refs[-1])

    hbm = [pltpu.HBM(w.shape, w.dtype) for w in wires]
    outs = pl.pallas_call(
        body, name=name, in_specs=[HBM_SPEC] * (2 * n),
        out_specs=[SEM_SPEC, SEM_SPEC] + [HBM_SPEC] * (2 * n) + [pl.BlockSpec(memory_space=pltpu.VMEM)],
        out_shape=list(side["sems"]) + hbm + hbm + [jax.ShapeDtypeStruct((8, 128), F32)],
        input_output_aliases={k: 2 + k for k in range(2 * n)},
        compiler_params=pltpu.CompilerParams(has_side_effects=pltpu.SideEffectType.DATAFLOW_SIDE_EFFECTING),
    )(*[pltpu.with_memory_space_constraint(w, pltpu.HBM) for w in wires],
      *[pltpu.with_memory_space_constraint(lax.empty(w.shape, w.dtype), pltpu.HBM) for w in wires])
    return outs[0], outs[1], outs[2:2 + n], outs[2 + n:2 + 2 * n], outs[-1]


def exchange_chips_wait(send_sems, recv_sems, wires, lands, after, name):
    n = len(wires)
    side = exchange_chips_side(wires)

    def body(*refs):
        side["finish"](refs[:n], refs[n:2 * n], refs[2 * n:2 * n + 2])

    hbm = [pltpu.HBM(w.shape, w.dtype) for w in wires]
    outs = pl.pallas_call(
        body, name=name, in_specs=[HBM_SPEC] * (2 * n) + [SEM_SPEC, SEM_SPEC, HBM_SPEC],
        out_specs=[HBM_SPEC] * (2 * n), out_shape=hbm + hbm,
        input_output_aliases={k: k for k in range(2 * n)},
        compiler_params=pltpu.CompilerParams(has_side_effects=pltpu.SideEffectType.DATAFLOW_SIDE_EFFECTING),
    )(*wires, *lands, send_sems, recv_sems, after)
    return outs[n:]


def rs_final_sum(place, own, got, after, name):
    NP, half, C = got.shape
    tr = _row_tile(half, TR_ELT)
    nr = half // tr

    def body(pc_ref, own_ref, g1_ref, g2_ref, g3_ref, after_ref, out_ref):
        out_ref[...] = ((own_ref[...] + g1_ref[...].astype(F32)) + g2_ref[...].astype(F32)) + g3_ref[...].astype(F32)

    slot = lambda j: pl.BlockSpec((None, tr, C), lambda r, pc: (pc[0] ^ j, r, 0))
    return pl.pallas_call(
        body, name=name,
        grid_spec=pltpu.PrefetchScalarGridSpec(
            num_scalar_prefetch=1, grid=(nr,),
            in_specs=[pl.BlockSpec((tr, C), lambda r, pc: (r, 0)), slot(1), slot(2), slot(3),
                      pl.BlockSpec((8, 128), lambda r, pc: (0, 0))],
            out_specs=pl.BlockSpec((tr, C), lambda r, pc: (pc[1] * nr + r, 0))),
        out_shape=jax.ShapeDtypeStruct((2 * half, C), F32),
        compiler_params=_params(("arbitrary",)),
    )(place, own, got, got, got, after)


def rs_share_siblings(totals, name):
    n = len(totals)

    def body(*refs):
        dst = refs[n:2 * n]
        send_sems, recv_sems = refs[2 * n:]
        x, y, c = _place()
        copies = []
        for k in range(n):
            half = dst[k].shape[0] // 2
            rows = dst[k].at[pl.ds(c * half, half)]
            cp = pltpu.make_async_remote_copy(rows, rows, send_sems.at[k], recv_sems.at[k],
                                              device_id=(x, y, 1 - c), device_id_type=MESH)
            cp.start()
            copies.append(cp)
        for k, cp in enumerate(copies):
            cp.wait_send()
            half = dst[k].shape[0] // 2
            got = dst[k].at[pl.ds((1 - c) * half, half)]
            pltpu.make_async_remote_copy(got, got, send_sems.at[k], recv_sems.at[k],
                                         device_id=(x, y, c), device_id_type=MESH).wait_recv()

    return pl.pallas_call(
        body, name=name,
        in_specs=[HBM_SPEC] * n, out_specs=[HBM_SPEC] * n,
        out_shape=[jax.ShapeDtypeStruct(t.shape, F32) for t in totals],
        input_output_aliases={k: k for k in range(n)},
        scratch_shapes=[pltpu.SemaphoreType.DMA((n,)), pltpu.SemaphoreType.DMA((n,))],
    )(*totals)


def rs_to_wires(place, grads, tag, sibs=None):
    if sibs is None:
        sibs = run_side(exchange_siblings_side(grads), f"rs_exchange_siblings_{tag}")
    wires, owns = [], []
    for k, (g, s) in enumerate(zip(grads, sibs)):
        w, o = rs_chip_sum(place, g, s, f"rs_chip_sum_{tag}{k}")
        wires.append(w)
        owns.append(o)
    return wires, owns


def rs_finish(place, owns, gots, after, tag):
    totals = [rs_final_sum(place, o, g, after, f"rs_final_sum_{tag}{k}") for k, (o, g) in enumerate(zip(owns, gots))]
    return rs_share_siblings(totals, f"rs_share_siblings_{tag}")


def _rope_tables(positions):
    half = HEAD_DIM // 2
    inv_freq = ROPE_THETA ** (-jnp.arange(half, dtype=F32) / half)
    ang = positions.astype(F32)[:, None] * inv_freq
    cos, sin = jnp.cos(ang), jnp.sin(ang)
    return jnp.tile(cos, (1, 4)), jnp.concatenate([-sin, sin, -sin, sin], axis=1)


def _cols_from_pieces(pieces, start, stop):
    C = pieces.shape[2]
    parts = []
    for q in range(N_CHIPS):
        lo, hi = max(start, q * C), min(stop, (q + 1) * C)
        if lo < hi:
            parts.append(pieces[q][:, lo - q * C:hi - q * C])
    return parts[0] if len(parts) == 1 else jnp.concatenate(parts, axis=1)


def _pieces_from_groups(groups):
    C = sum(g.shape[1] for g in groups) // N_CHIPS
    pieces = []
    for q in range(N_CHIPS):
        parts, off = [], 0
        for g in groups:
            lo, hi = max(q * C, off), min((q + 1) * C, off + g.shape[1])
            if lo < hi:
                parts.append(g[:, lo - off:hi - off])
            off += g.shape[1]
        pieces.append(parts[0] if len(parts) == 1 else jnp.concatenate(parts, axis=1))
    return jnp.stack(pieces)


def kernel(x, positions, ffn1_norm, ffn1_w_gate, ffn1_w_up, ffn1_w_down, mix_norm, w_in, conv_dw_w, conv_dw_b, conv_ln_g, conv_ln_b, conv_w_proj, attn_sinks, attn_w_o, gate_b, w_out, ffn2_norm, ffn2_w_gate, ffn2_w_up, ffn2_w_down, final_norm, loss_target, m_ffn1_norm, m_ffn1_w_gate, m_ffn1_w_up, m_ffn1_w_down, m_mix_norm, m_w_in, m_conv_dw_w, m_conv_dw_b, m_conv_ln_g, m_conv_ln_b, m_conv_w_proj, m_attn_sinks, m_attn_w_o, m_gate_b, m_w_out, m_ffn2_norm, m_ffn2_w_gate, m_ffn2_w_up, m_ffn2_w_down, m_final_norm, v_ffn1_norm, v_ffn1_w_gate, v_ffn1_w_up, v_ffn1_w_down, v_mix_norm, v_w_in, v_conv_dw_w, v_conv_dw_b, v_conv_ln_g, v_conv_ln_b, v_conv_w_proj, v_attn_sinks, v_attn_w_o, v_gate_b, v_w_out, v_ffn2_norm, v_ffn2_w_gate, v_ffn2_w_up, v_ffn2_w_down, v_final_norm):
    weights = dict(ffn1_norm=ffn1_norm, ffn1_w_gate=ffn1_w_gate, ffn1_w_up=ffn1_w_up, ffn1_w_down=ffn1_w_down,
                   mix_norm=mix_norm, w_in=w_in, conv_dw_w=conv_dw_w, conv_dw_b=conv_dw_b, conv_ln_g=conv_ln_g,
                   conv_ln_b=conv_ln_b, conv_w_proj=conv_w_proj, attn_sinks=attn_sinks, attn_w_o=attn_w_o,
                   gate_b=gate_b, w_out=w_out, ffn2_norm=ffn2_norm, ffn2_w_gate=ffn2_w_gate, ffn2_w_up=ffn2_w_up,
                   ffn2_w_down=ffn2_w_down, final_norm=final_norm)
    m_in = dict(ffn1_norm=m_ffn1_norm, ffn1_w_gate=m_ffn1_w_gate, ffn1_w_up=m_ffn1_w_up, ffn1_w_down=m_ffn1_w_down,
                mix_norm=m_mix_norm, w_in=m_w_in, conv_dw_w=m_conv_dw_w, conv_dw_b=m_conv_dw_b,
                conv_ln_g=m_conv_ln_g, conv_ln_b=m_conv_ln_b, conv_w_proj=m_conv_w_proj, attn_sinks=m_attn_sinks,
                attn_w_o=m_attn_w_o, gate_b=m_gate_b, w_out=m_w_out, ffn2_norm=m_ffn2_norm,
                ffn2_w_gate=m_ffn2_w_gate, ffn2_w_up=m_ffn2_w_up, ffn2_w_down=m_ffn2_w_down, final_norm=m_final_norm)
    v_in = dict(ffn1_norm=v_ffn1_norm, ffn1_w_gate=v_ffn1_w_gate, ffn1_w_up=v_ffn1_w_up, ffn1_w_down=v_ffn1_w_down,
                mix_norm=v_mix_norm, w_in=v_w_in, conv_dw_w=v_conv_dw_w, conv_dw_b=v_conv_dw_b,
                conv_ln_g=v_conv_ln_g, conv_ln_b=v_conv_ln_b, conv_w_proj=v_conv_w_proj, attn_sinks=v_attn_sinks,
                attn_w_o=v_attn_w_o, gate_b=v_gate_b, w_out=v_w_out, ffn2_norm=v_ffn2_norm,
                ffn2_w_gate=v_ffn2_w_gate, ffn2_w_up=v_ffn2_w_up, ffn2_w_down=v_ffn2_w_down, final_norm=v_final_norm)
    names = list(weights)
    big = ["ffn1_w_gate", "ffn1_w_up", "ffn1_w_down", "w_in", "conv_w_proj", "attn_w_o", "w_out",
           "ffn2_w_gate", "ffn2_w_up", "ffn2_w_down"]

    xs = x[0]
    T, D = xs.shape
    KV = (w_in.shape[2] * N_CHIPS - 5 * D) // 2
    n_heads = D // HEAD_DIM
    my_chip = 2 * lax.axis_index("x") + lax.axis_index("y")
    place = jnp.stack([my_chip, lax.axis_index("c")]).astype(jnp.int32)

    placed = {k: place_shard(place, weights[k][0], BF16, f"place_{k}") for k in big}
    placed_dw = place_shard(place, conv_dw_w[0], F32, "place_conv_dw_w")
    first, later = big[:3], big[3:]
    wg1, wu1, wd1 = run_side(gather_side([placed[k] for k in first], []), "gather_ffn1")
    x1, h1, g1, u1, *gathered = ffn_fwd(x[0], ffn1_norm, wg1, wu1, wd1, "ffn1_fwd",
                                        side=gather_side([placed[k] for k in later], [placed_dw]))
    full = dict(zip(later + ["conv_dw_w"], gathered))
    wg2, wu2, wd2 = full["ffn2_w_gate"], full["ffn2_w_up"], full["ffn2_w_down"]
    w_glu = _cols_from_pieces(full["w_in"], 0, 2 * D)
    w_qkv = _cols_from_pieces(full["w_in"], 2 * D, 3 * D + 2 * KV)
    w_gate = _cols_from_pieces(full["w_in"], 3 * D + 2 * KV, 5 * D + 2 * KV)
    w_proj = full["conv_w_proj"].reshape(D, D)
    w_o = full["attn_w_o"].reshape(D, D)
    w_out_f = full["w_out"].reshape(D, D)
    dw_w = full["conv_dw_w"].transpose(1, 0, 2).reshape(CONV_WIDTH, D)
    dw_w = jnp.concatenate([dw_w, jnp.zeros((CONV_HALO - CONV_WIDTH, D), F32)], axis=0)
    cs, sn = _rope_tables(positions[0])
    fn_row = final_norm.reshape(1, D)

    h2, p_glu, p_gate, qr, kr, vb = mix_in_fwd(x1, mix_norm, w_glu, w_qkv, w_gate, cs, sn, "mix_in_fwd")
    c1, c3 = conv_fwd(p_glu, dw_w, conv_dw_b, conv_ln_g, conv_ln_b, "conv_fwd")
    o = attn_fwd(qr, kr, vb, attn_sinks, "attn_fwd")
    x2, conv_out, attn_out, merged = merge_fwd(x1, c3, o, p_gate, gate_b, w_proj, w_o, w_out_f, "merge_fwd")
    x3, h3, g2, u2 = ffn_fwd(x2, ffn2_norm, wg2, wu2, wd2, "ffn2_fwd")

    dx3, head_sums = loss_head(x3, fn_row, loss_target[0], "loss_head")
    dx2, dwg2, dwu2, dwd2, d_ffn2_norm = ffn_bwd(x2, ffn2_norm, h3, g2, u2, wg2, wu2, wd2, dx3, "ffn2_bwd")
    ffn2_grads = [dwg2, dwu2, dwd2]
    d_gates, d_conv_out, d_attn_out, d_o, dc1, merge_sums, *sibs_f2 = merge_bwd(
        dx2, p_gate, gate_b, conv_out, attn_out, c1, conv_ln_g, conv_ln_b, w_proj, w_o, w_out_f, "merge_bwd",
        side=exchange_siblings_side(ffn2_grads))
    d_w_out = matmul_tn(merged, dx2, "d_w_out")
    d_w_proj = matmul_tn(c3, d_conv_out, "d_conv_w_proj")
    d_w_o = matmul_tn(o, d_attn_out, "d_attn_w_o")
    wires_f2, owns_f2 = rs_to_wires(place, ffn2_grads, "ffn2", sibs=sibs_f2)
    d_glu, d_dw_w, *gots_f2 = conv_bwd(p_glu, dc1, dw_w, "conv_bwd", side=exchange_chips_side(wires_f2))
    dq, dk, dv, d_sinks = attn_bwd(qr, kr, vb, o, d_o, attn_sinks, "attn_bwd")
    d_qkv = rope_bwd(dq, dk, dv, cs, sn, "rope_bwd")
    dx1, d_mix_norm = mix_in_bwd([d_glu, d_qkv, d_gates], [w_glu, w_qkv, w_gate], x1, mix_norm, dx2, "mix_in_bwd")
    d_w_in = _pieces_from_groups([matmul_tn(h2, d_glu, "d_w_in_glu"), matmul_tn(h2, d_qkv, "d_w_in_qkv"),
                                  matmul_tn(h2, d_gates, "d_w_in_gate")])
    dwc = D // N_CHIPS
    mixer_grads = [d_w_in, d_w_proj.reshape(N_CHIPS, dwc, D), d_w_o.reshape(N_CHIPS, dwc, D),
                   d_w_out.reshape(N_CHIPS, dwc, D)]
    wires_m, owns_m = rs_to_wires(place, mixer_grads, "mixer")
    dx0, dwg1, dwu1, dwd1, d_ffn1_norm, *gots_m = ffn_bwd(xs, ffn1_norm, h1, g1, u1, wg1, wu1, wd1, dx1, "ffn1_bwd",
                                                          side=exchange_chips_side(wires_m))
    wires_l, owns_l = rs_to_wires(place, [dwg1, dwu1, dwd1], "ffn1")
    send_sems, recv_sems, wires_l, lands_l, token = exchange_chips_start(wires_l, "rs_exchange_chips_ffn1_start")
    early_names = ["ffn2_w_gate", "ffn2_w_up", "ffn2_w_down", "w_in", "conv_w_proj", "attn_w_o", "w_out"]
    late_names = ["ffn1_w_gate", "ffn1_w_up", "ffn1_w_down"]
    reduced_early = rs_finish(place, owns_f2 + owns_m, list(gots_f2) + list(gots_m), token, "early")

    pad_row = lambda v: jnp.pad(v, ((0, 0), (0, D - v.shape[1])))
    small_rows = jnp.concatenate([
        d_ffn1_norm, d_mix_norm, merge_sums[2:3, :D], merge_sums[1:2, :D], merge_sums[1:2, D:],
        pad_row(d_sinks[0:1, :n_heads]), merge_sums[0:1, :D], merge_sums[0:1, D:], d_ffn2_norm,
        head_sums[0:1], head_sums[1:2], jnp.zeros((5, D), F32), d_dw_w], axis=0)
    small = allreduce_small(small_rows)
    loss = small[10, 0]
    grads = {"ffn1_norm": small[0:1], "mix_norm": small[1:2], "conv_dw_b": small[2:3], "conv_ln_g": small[3:4],
             "conv_ln_b": small[4:5], "attn_sinks": small[5:6, :n_heads],
             "gate_b": jnp.concatenate([small[6:7], small[7:8]], axis=1), "ffn2_norm": small[8:9],
             "final_norm": small[9:10]}
    grads["conv_dw_w"] = lax.dynamic_slice(small[16:16 + CONV_WIDTH], (0, my_chip * dwc), (CONV_WIDTH, dwc))
    grads.update(zip(early_names, reduced_early))

    deltas, new_m, new_v = {}, {}, {}

    def apply_adamw(k):
        shape = weights[k].shape
        g2d = grads[k].reshape(-1, shape[-1])
        grads[k] = g2d.reshape(shape)
        d, mn, vn = adamw(weights[k].reshape(g2d.shape), g2d, m_in[k].reshape(g2d.shape),
                          v_in[k].reshape(g2d.shape), f"adamw_{k}")
        deltas[k], new_m[k], new_v[k] = d.reshape(shape), mn.reshape(shape), vn.reshape(shape)
        return d

    done = [apply_adamw(k)[:1, :1] for k in names if k not in late_names]
    after = jnp.broadcast_to(sum(done[1:], done[0]), (8, 128))
    gots_l = exchange_chips_wait(send_sems, recv_sems, wires_l, lands_l, after, "rs_exchange_chips_ffn1_wait")
    grads.update(zip(late_names, rs_finish(place, owns_l, gots_l, after, "late")))
    for k in late_names:
        apply_adamw(k)

    return (loss, dx0[None], *[grads[k] for k in names], *[deltas[k] for k in names],
            *[new_m[k] for k in names], *[new_v[k] for k in names])
```

```python
import functools

import jax
import jax.numpy as jnp
from jax import lax
from jax.experimental import pallas as pl
from jax.experimental.pallas import tpu as pltpu

F32 = jnp.float32
BF16 = jnp.bfloat16
MESH = pl.DeviceIdType.MESH

HEAD_DIM = 64
WINDOW = 128
CONV_WIDTH = 31
CONV_HALO = 32
ROPE_THETA = 10000.0
EPS = 1e-6
LN_EPS = 1e-5
NEG_INF = -1e30
N_CHIPS = 4
N_DEV = 8

ADAM_LR = 0.001
ADAM_B1 = 0.9
ADAM_B2 = 0.999
ADAM_EPS = 1e-08
ADAM_WD = 0.01
ADAM_STEP = 10

TM_FFN = 512
TM_FFN_FWD = 1024
TM_ROW = 256
TK_TN = 1024
TR_ELT = 256
VMEM_LIMIT = 56 * 1024 * 1024

NT_DIMS = (((1,), (1,)), ((), ()))
TN_DIMS = (((0,), (0,)), ((), ()))


def _row_tile(rows, cap):
    for t in range(min(cap, rows), 15, -1):
        if rows % t == 0 and t % 16 == 0:
            return t
    return rows


def _params(sem):
    return pltpu.CompilerParams(dimension_semantics=sem, vmem_limit_bytes=VMEM_LIMIT)


def _dot(a, b):
    return jnp.dot(a, b, preferred_element_type=F32)


def _dot_nt(a, b):
    return lax.dot_general(a, b, NT_DIMS, preferred_element_type=F32)


def _dot_tn(a, b):
    return lax.dot_general(a, b, TN_DIMS, preferred_element_type=F32)


def _split_rows(dot, a, b):
    m = a.shape[0] // 2
    return jnp.concatenate([dot(a[:m], b), dot(a[m:], b)], axis=0)


def _sigmoid(x):
    return jax.nn.sigmoid(x)


def _rms_scale(xv):
    return lax.rsqrt(jnp.mean(xv * xv, axis=-1, keepdims=True) + EPS)


def _rms_bwd(xv, nw, dh):
    r = _rms_scale(xv)
    dn = dh * nw
    dx = r * dn - xv * (r * r * r) * jnp.mean(dn * xv, axis=-1, keepdims=True)
    dnw = jnp.sum(dh * (xv * r), axis=0, keepdims=True)
    return dx, dnw


def _silu_grad(z, s):
    return s * (1.0 + z * (1.0 - s))


HBM_SPEC = pl.BlockSpec(memory_space=pl.ANY)


def _call_hosting(body, side, *, grid, in_specs, out_specs, out_shape, scratch_shapes, operands, name, aliases=None):
    params = _params(("arbitrary",) * len(grid))
    aliases = dict(aliases or {})
    if side is None:
        return pl.pallas_call(body, name=name, grid=grid, in_specs=in_specs, out_specs=out_specs, out_shape=out_shape,
                              scratch_shapes=scratch_shapes, input_output_aliases=aliases,
                              compiler_params=params)(*operands)
    n_in, n_out, n_scr = len(in_specs), len(out_shape), len(scratch_shapes)
    s_in, s_out = len(side["inputs"]), len(side["out_shapes"])

    def at_step(end):
        hit = pl.program_id(0) == (grid[0] - 1 if end else 0)
        for a in range(1, len(grid)):
            hit &= pl.program_id(a) == (grid[a] - 1 if end else 0)
        return hit

    def hosted(*refs):
        b = n_in + s_in
        c = b + n_out
        d = c + s_out
        e = d + n_scr
        src, dst, sems = refs[n_in:b], refs[c:d], refs[e:]

        @pl.when(at_step(False))
        def _():
            side["start"](src, dst, sems)

        body(*refs[:n_in], *refs[b:c], *refs[d:e])

        @pl.when(at_step(True))
        def _():
            side["finish"](src, dst, sems)

    return pl.pallas_call(
        hosted, name=name, grid=grid, in_specs=list(in_specs) + [HBM_SPEC] * s_in,
        out_specs=list(out_specs) + [HBM_SPEC] * s_out, out_shape=list(out_shape) + list(side["out_shapes"]),
        scratch_shapes=list(scratch_shapes) + list(side["sems"]),
        input_output_aliases={**aliases, **{n_in + a: n_out + b for a, b in side["aliases"].items()}},
        compiler_params=params)(*operands, *side["inputs"])


def ffn_fwd(x, nw, wg, wu, wd, name, side=None):
    T, D = x.shape
    NP, Fs, _ = wg.shape
    tm = min(TM_FFN_FWD, T)

    def body(x_ref, nw_ref, wg_ref, wu_ref, wd_ref, xo_ref, h_ref, g_ref, u_ref, acc_ref):
        j = pl.program_id(1)

        @pl.when(j == 0)
        def _():
            xv = x_ref[...]
            h_ref[...] = (xv * _rms_scale(xv) * nw_ref[...]).astype(BF16)
            acc_ref[...] = jnp.zeros_like(acc_ref)

        h = h_ref[...]
        g = _dot_nt(h, wg_ref[...])
        u = _dot_nt(h, wu_ref[...])
        a = (g * _sigmoid(g)) * u
        g_ref[...] = g.astype(BF16)
        u_ref[...] = u.astype(BF16)
        acc_ref[...] += _dot(a.astype(BF16), wd_ref[...])

        @pl.when(j == NP - 1)
        def _():
            xo_ref[...] = x_ref[...] + 0.5 * acc_ref[...]

    return _call_hosting(
        body, side, name=name, grid=(T // tm, NP),
        in_specs=[pl.BlockSpec((tm, D), lambda i, j: (i, 0)),
                  pl.BlockSpec((1, D), lambda i, j: (0, 0)),
                  pl.BlockSpec((None, Fs, D), lambda i, j: (j, 0, 0)),
                  pl.BlockSpec((None, Fs, D), lambda i, j: (j, 0, 0)),
                  pl.BlockSpec((None, Fs, D), lambda i, j: (j, 0, 0))],
        out_specs=[pl.BlockSpec((tm, D), lambda i, j: (i, 0)),
                   pl.BlockSpec((tm, D), lambda i, j: (i, 0)),
                   pl.BlockSpec((None, tm, Fs), lambda i, j: (j, i, 0)),
                   pl.BlockSpec((None, tm, Fs), lambda i, j: (j, i, 0))],
        out_shape=[jax.ShapeDtypeStruct((T, D), F32), jax.ShapeDtypeStruct((T, D), BF16),
                   jax.ShapeDtypeStruct((NP, T, Fs), BF16), jax.ShapeDtypeStruct((NP, T, Fs), BF16)],
        scratch_shapes=[pltpu.VMEM((tm, D), F32)],
        operands=(x, nw, wg, wu, wd))


def _ffn_bwd_piece(j, h, g, u, wg, wu, wd, dout, dh_in, dws_in, name, side, norm):
    T, D = h.shape
    NP, Fs, _ = wg.shape
    tm = min(TM_FFN, T)
    n_in = 7 + (dh_in is not None) + (2 if norm else 0) + (3 if dws_in else 0)

    def body(*refs):
        h_ref, g_ref, u_ref, wg_ref, wu_ref, wd_ref, do_ref = refs[:7]
        dhin_ref = refs[7] if dh_in is not None else None
        dh_ref, dwg_ref, dwu_ref, dwd_ref = refs[n_in:n_in + 4]

        @pl.when(pl.program_id(0) == 0)
        def _():
            dwg_ref[...] = jnp.zeros_like(dwg_ref)
            dwu_ref[...] = jnp.zeros_like(dwu_ref)
            dwd_ref[...] = jnp.zeros_like(dwd_ref)
            if norm:
                refs[n_in + 4][...] = jnp.zeros_like(refs[n_in + 4])

        dob = (0.5 * do_ref[...]).astype(BF16)
        da = _split_rows(_dot_nt, dob, wd_ref[...])
        gf = g_ref[...].astype(F32)
        uf = u_ref[...].astype(F32)
        s = _sigmoid(gf)
        act = gf * s
        dg = (da * uf * _silu_grad(gf, s)).astype(BF16)
        du = (da * act).astype(BF16)
        a = (act * uf).astype(BF16)
        dh = _dot(dg, wg_ref[...]) + _dot(du, wu_ref[...])
        dh = dh if dhin_ref is None else dhin_ref[...] + dh
        if norm:
            x_ref, nw_ref = refs[7 + (dh_in is not None):9 + (dh_in is not None)]
            dxn, dnw = _rms_bwd(x_ref[...], nw_ref[...], dh)
            dh_ref[...] = do_ref[...] + dxn
            refs[n_in + 4][...] += dnw
        else:
            dh_ref[...] = dh
        hb = h_ref[...]
        dwg_ref[...] += _dot_tn(dg, hb)
        dwu_ref[...] += _dot_tn(du, hb)
        dwd_ref[...] += _dot_tn(a, dob)

    rows = pl.BlockSpec((tm, D), lambda i: (i, 0))
    piece = pl.BlockSpec((None, tm, Fs), lambda i: (j, i, 0))
    slot = pl.BlockSpec((None, Fs, D), lambda i: (j, 0, 0), pipeline_mode=pl.Buffered(1))
    in_specs = [rows, piece, piece, slot, slot, slot, rows]
    operands = [h, g, u, wg, wu, wd, dout]
    aliases = {}
    if dh_in is not None:
        in_specs.append(rows)
        operands.append(dh_in)
    if norm:
        in_specs += [rows, pl.BlockSpec((1, D), lambda i: (0, 0))]
        operands += list(norm)
    if dws_in:
        aliases = {len(operands) + k: 1 + k for k in range(3)}
        in_specs += [HBM_SPEC] * 3
        operands += list(dws_in)
    out_specs = [rows, slot, slot, slot]
    out_shape = [jax.ShapeDtypeStruct((T, D), F32)] + [jax.ShapeDtypeStruct((NP, Fs, D), F32)] * 3
    if norm:
        out_specs.append(pl.BlockSpec((1, D), lambda i: (0, 0)))
        out_shape.append(jax.ShapeDtypeStruct((1, D), F32))
    return _call_hosting(body, side, name=name, grid=(T // tm,), in_specs=in_specs, out_specs=out_specs,
                         out_shape=out_shape, scratch_shapes=[], aliases=aliases, operands=tuple(operands))


def ffn_bwd(x, nw, h, g, u, wg, wu, wd, dout, name, side=None):
    NP = wg.shape[0]
    dh, dws, extra = None, None, []
    for j in range(NP):
        dh, *rest = _ffn_bwd_piece(j, h, g, u, wg, wu, wd, dout, dh, dws, f"{name}_{j}",
                                   side if j == 0 else None, (x, nw) if j == NP - 1 else None)
        dws, rest = rest[:3], rest[3:]
        if j == 0:
            extra = rest[1:] if NP == 1 else rest
    return (dh, *dws, rest[0], *extra)


def mix_in_fwd(x, nw, w_glu, w_qkv, w_gate, cs, sn, name):
    T, D = x.shape
    KV = (w_qkv.shape[1] - D) // 2
    tm = min(TM_ROW, T)

    def body(x_ref, nw_ref, wa_ref, wq_ref, wg_ref, cs_ref, sn_ref, h_ref, pa_ref, pg_ref, q_ref, k_ref, v_ref):
        xv = x_ref[...]
        h = (xv * _rms_scale(xv) * nw_ref[...]).astype(BF16)
        h_ref[...] = h
        pa_ref[...] = _dot(h, wa_ref[...])
        pg_ref[...] = _dot(h, wg_ref[...])
        qkv = _dot(h, wq_ref[...])
        cs_v, sn_v = cs_ref[...], sn_ref[...]
        q_ref[...] = _rope_chunks(qkv[:, :D], cs_v, sn_v, 1.0).astype(BF16)
        k_ref[...] = _rope_chunks(qkv[:, D:D + KV], cs_v, sn_v, 1.0).astype(BF16)
        v_ref[...] = qkv[:, D + KV:].astype(BF16)

    rows = lambda w: pl.BlockSpec((tm, w), lambda i: (i, 0))
    whole = lambda a: pl.BlockSpec(a.shape, lambda i: (0, 0))
    return pl.pallas_call(
        body, name=name, grid=(T // tm,),
        in_specs=[rows(D), whole(nw), whole(w_glu), whole(w_qkv), whole(w_gate), rows(128), rows(128)],
        out_specs=[rows(D), rows(2 * D), rows(2 * D), rows(D), rows(KV), rows(KV)],
        out_shape=[jax.ShapeDtypeStruct((T, D), BF16), jax.ShapeDtypeStruct((T, 2 * D), F32),
                   jax.ShapeDtypeStruct((T, 2 * D), F32), jax.ShapeDtypeStruct((T, D), BF16),
                   jax.ShapeDtypeStruct((T, KV), BF16), jax.ShapeDtypeStruct((T, KV), BF16)],
        compiler_params=_params(("parallel",)),
    )(x, nw, w_glu, w_qkv, w_gate, cs, sn)


def matmul_tn(lhs, rhs, name):
    T, K = lhs.shape
    N = rhs.shape[1]
    tk = min(TK_TN, T)

    def body(l_ref, r_ref, o_ref):
        @pl.when(pl.program_id(0) == 0)
        def _():
            o_ref[...] = jnp.zeros_like(o_ref)

        o_ref[...] += _dot_tn(l_ref[...].astype(BF16), r_ref[...].astype(BF16))

    return pl.pallas_call(
        body, name=name, grid=(T // tk,),
        in_specs=[pl.BlockSpec((tk, K), lambda t: (t, 0)), pl.BlockSpec((tk, N), lambda t: (t, 0))],
        out_specs=pl.BlockSpec((K, N), lambda t: (0, 0)),
        out_shape=jax.ShapeDtypeStruct((K, N), F32),
        compiler_params=_params(("arbitrary",)),
    )(lhs, rhs)


def mix_in_bwd(dps, ws, x, nw, dres, name):
    T, D = x.shape
    tm = min(TM_ROW, T)
    n = len(dps)

    def body(*refs):
        dp_refs, w_refs = refs[:n], refs[n:2 * n]
        x_ref, nw_ref, dr_ref, dx_ref, dnw_ref = refs[2 * n:]

        @pl.when(pl.program_id(0) == 0)
        def _():
            dnw_ref[...] = jnp.zeros_like(dnw_ref)

        dh = _dot_nt(dp_refs[0][...], w_refs[0][...])
        for k in range(1, n):
            dh += _dot_nt(dp_refs[k][...], w_refs[k][...])
        dxn, dnw = _rms_bwd(x_ref[...], nw_ref[...], dh)
        dx_ref[...] = dr_ref[...] + dxn
        dnw_ref[...] += dnw

    in_specs = [pl.BlockSpec((tm, dp.shape[1]), lambda i: (i, 0)) for dp in dps]
    in_specs += [pl.BlockSpec(w.shape, lambda i: (0, 0)) for w in ws]
    in_specs += [pl.BlockSpec((tm, D), lambda i: (i, 0)), pl.BlockSpec((1, D), lambda i: (0, 0)),
                 pl.BlockSpec((tm, D), lambda i: (i, 0))]
    return pl.pallas_call(
        body, name=name, grid=(T // tm,), in_specs=in_specs,
        out_specs=[pl.BlockSpec((tm, D), lambda i: (i, 0)), pl.BlockSpec((1, D), lambda i: (0, 0))],
        out_shape=[jax.ShapeDtypeStruct((T, D), F32), jax.ShapeDtypeStruct((1, D), F32)],
        compiler_params=_params(("arbitrary",)),
    )(*dps, *ws, x, nw, dres)


def _layernorm_stats(c1):
    mu = jnp.mean(c1, axis=-1, keepdims=True)
    xc = c1 - mu
    rstd = lax.rsqrt(jnp.mean(xc * xc, axis=-1, keepdims=True) + LN_EPS)
    return xc * rstd, rstd


def _shifted_copies(src_ref, dst_ref):
    rows = dst_ref.shape[1]
    for b in range(1, 8):
        dst_ref[b - 1] = src_ref[pl.ds(b, rows), :]


def _shifted_rows(src_ref, shifted_ref, start, rows, cols):
    a8, b = divmod(start, 8)
    if b == 0:
        return src_ref[pl.ds(8 * a8, rows), cols]
    return shifted_ref[b - 1, pl.ds(8 * a8, rows), cols]


def conv_fwd(p_glu, dw_w, dw_b, ln_g, ln_b, name):
    T, D2 = p_glu.shape
    D = D2 // 2
    tm = min(TM_ROW, T)
    hb = tm // CONV_HALO

    def body(a_ref, b_ref, ah_ref, bh_ref, w_ref, wb_ref, g_ref, be_ref, c1_ref, c3_ref, e_ref, es_ref):
        i = pl.program_id(0)
        halo = ah_ref[...] * _sigmoid(bh_ref[...])
        e_ref[pl.ds(0, CONV_HALO), :] = jnp.where(i > 0, halo, 0.0)
        e_ref[pl.ds(CONV_HALO, tm), :] = a_ref[...] * _sigmoid(b_ref[...])
        _shifted_copies(e_ref, es_ref)
        off = CONV_HALO - (CONV_WIDTH - 1)

        def strip(s, carry):
            cols = pl.ds(pl.multiple_of(s * 128, 128), 128)
            acc = jnp.zeros((tm, 128), F32) + wb_ref[:, cols]
            for k in range(CONV_WIDTH):
                acc += w_ref[pl.ds(k, 1), cols] * _shifted_rows(e_ref, es_ref, off + k, tm, cols)
            c1_ref[:, cols] = acc
            return carry

        lax.fori_loop(0, D // 128, strip, 0)
        xhat, _ = _layernorm_stats(c1_ref[...])
        c2 = xhat * g_ref[...] + be_ref[...]
        c3_ref[...] = (c2 * _sigmoid(c2)).astype(BF16)

    row = pl.BlockSpec((1, D), lambda i: (0, 0))
    return pl.pallas_call(
        body, name=name, grid=(T // tm,),
        in_specs=[pl.BlockSpec((tm, D), lambda i: (i, 0)), pl.BlockSpec((tm, D), lambda i: (i, 1)),
                  pl.BlockSpec((CONV_HALO, D), lambda i: (jnp.maximum(i * hb - 1, 0), 0)),
                  pl.BlockSpec((CONV_HALO, D), lambda i: (jnp.maximum(i * hb - 1, 0), 1)),
                  pl.BlockSpec((CONV_HALO, D), lambda i: (0, 0)), row, row, row],
        out_specs=[pl.BlockSpec((tm, D), lambda i: (i, 0)), pl.BlockSpec((tm, D), lambda i: (i, 0))],
        out_shape=[jax.ShapeDtypeStruct((T, D), F32), jax.ShapeDtypeStruct((T, D), BF16)],
        scratch_shapes=[pltpu.VMEM((tm + CONV_HALO, D), F32), pltpu.VMEM((7, tm + CONV_HALO - 8, D), F32)],
        compiler_params=_params(("parallel",)),
    )(p_glu, p_glu, p_glu, p_glu, dw_w, dw_b, ln_g, ln_b)


def conv_bwd(p_glu, dc1, dw_w, name, side=None):
    T, D2 = p_glu.shape
    D = D2 // 2
    tm = min(TM_ROW, T)
    hb = tm // CONV_HALO
    last = T // CONV_HALO - 1
    nblk = T // tm

    def body(a_ref, b_ref, ah_ref, bh_ref, d_ref, dn_ref, w_ref, dp_ref, dw_ref, e_ref, f_ref, es_ref, fs_ref):
        i = pl.program_id(0)

        @pl.when(i == 0)
        def _():
            dw_ref[...] = jnp.zeros_like(dw_ref)

        halo = ah_ref[...] * _sigmoid(bh_ref[...])
        e_ref[pl.ds(0, CONV_HALO), :] = jnp.where(i > 0, halo, 0.0)
        e_ref[pl.ds(CONV_HALO, tm), :] = a_ref[...] * _sigmoid(b_ref[...])
        f_ref[pl.ds(0, tm), :] = d_ref[...]
        f_ref[pl.ds(tm, CONV_HALO), :] = jnp.where(i < nblk - 1, dn_ref[...], 0.0)
        _shifted_copies(e_ref, es_ref)
        _shifted_copies(f_ref, fs_ref)
        off = CONV_HALO - (CONV_WIDTH - 1)

        def strip(s, carry):
            cols = pl.ds(pl.multiple_of(s * 128, 128), 128)
            d = d_ref[:, cols]
            dc0 = jnp.zeros((tm, 128), F32)
            for k in range(CONV_WIDTH):
                dw_ref[pl.ds(k, 1), cols] += jnp.sum(d * _shifted_rows(e_ref, es_ref, off + k, tm, cols),
                                                     axis=0, keepdims=True)
                dc0 += w_ref[pl.ds(k, 1), cols] * _shifted_rows(f_ref, fs_ref, CONV_WIDTH - 1 - k, tm, cols)
            a = a_ref[:, cols]
            sb = _sigmoid(b_ref[:, cols])
            dp_ref[:, cols] = (dc0 * sb).astype(BF16)
            dp_ref[:, pl.ds(pl.multiple_of(D + s * 128, 128), 128)] = (dc0 * a * sb * (1.0 - sb)).astype(BF16)
            return carry

        lax.fori_loop(0, D // 128, strip, 0)

    return _call_hosting(
        body, side, name=name, grid=(nblk,),
        in_specs=[pl.BlockSpec((tm, D), lambda i: (i, 0)), pl.BlockSpec((tm, D), lambda i: (i, 1)),
                  pl.BlockSpec((CONV_HALO, D), lambda i: (jnp.maximum(i * hb - 1, 0), 0)),
                  pl.BlockSpec((CONV_HALO, D), lambda i: (jnp.maximum(i * hb - 1, 0), 1)),
                  pl.BlockSpec((tm, D), lambda i: (i, 0)),
                  pl.BlockSpec((CONV_HALO, D), lambda i: (jnp.minimum((i + 1) * hb, last), 0)),
                  pl.BlockSpec((CONV_HALO, D), lambda i: (0, 0))],
        out_specs=[pl.BlockSpec((tm, D2), lambda i: (i, 0)), pl.BlockSpec((CONV_HALO, D), lambda i: (0, 0))],
        out_shape=[jax.ShapeDtypeStruct((T, D2), BF16), jax.ShapeDtypeStruct((CONV_HALO, D), F32)],
        scratch_shapes=[pltpu.VMEM((tm + CONV_HALO, D), F32), pltpu.VMEM((tm + CONV_HALO, D), F32),
                        pltpu.VMEM((7, tm + CONV_HALO - 8, D), F32), pltpu.VMEM((7, tm + CONV_HALO - 8, D), F32)],
        operands=(p_glu, p_glu, p_glu, p_glu, dc1, dc1, dw_w))


def _rot_half(x):
    lane = lax.broadcasted_iota(jnp.int32, x.shape, 1)
    first = (lane % HEAD_DIM) < HEAD_DIM // 2
    return jnp.where(first, pltpu.roll(x, 128 - HEAD_DIM // 2, 1), pltpu.roll(x, HEAD_DIM // 2, 1))


def _rope_chunks(x, cs, sn, sign):
    outs = []
    for c in range(x.shape[1] // 128):
        xc = x[:, c * 128:(c + 1) * 128]
        outs.append(xc * cs + sign * (_rot_half(xc) * sn))
    return outs[0] if len(outs) == 1 else jnp.concatenate(outs, axis=1)


def rope_bwd(dq, dk, dv, cs, sn, name):
    T, D = dq.shape
    KV = dk.shape[1]
    tm = min(TM_ROW, T)

    def body(dq_ref, dk_ref, dv_ref, cs_ref, sn_ref, o_ref):
        cs_v, sn_v = cs_ref[...], sn_ref[...]
        o_ref[:, pl.ds(0, D)] = _rope_chunks(dq_ref[...], cs_v, sn_v, -1.0).astype(BF16)
        o_ref[:, pl.ds(D, KV)] = _rope_chunks(dk_ref[...], cs_v, sn_v, -1.0).astype(BF16)
        o_ref[:, pl.ds(D + KV, KV)] = dv_ref[...].astype(BF16)

    tab = pl.BlockSpec((tm, 128), lambda i: (i, 0))
    return pl.pallas_call(
        body, name=name, grid=(T // tm,),
        in_specs=[pl.BlockSpec((tm, D), lambda i: (i, 0)), pl.BlockSpec((tm, KV), lambda i: (i, 0)),
                  pl.BlockSpec((tm, KV), lambda i: (i, 0)), tab, tab],
        out_specs=pl.BlockSpec((tm, D + 2 * KV), lambda i: (i, 0)),
        out_shape=jax.ShapeDtypeStruct((T, D + 2 * KV), BF16),
        compiler_params=_params(("parallel",)),
    )(dq, dk, dv, cs, sn)


def _lane_lo():
    return lax.broadcasted_iota(jnp.int32, (1, 128), 1) < HEAD_DIM


def _band_mask(i, reps):
    shape = (reps * WINDOW, 2 * WINDOW)
    qi = lax.broadcasted_iota(jnp.int32, shape, 0) % WINDOW
    cj = lax.broadcasted_iota(jnp.int32, shape, 1)
    rel = qi - cj + WINDOW
    return (rel >= 0) & (rel < WINDOW) & ((i > 0) | (cj >= WINDOW))


def _stack_pairs(ref, first, n):
    parts = [ref[:, pl.ds((first + p) * 128, 128)] for p in range(n)]
    return parts[0] if n == 1 else jnp.concatenate(parts, axis=0)


def _pair_rows(n):
    return lax.broadcasted_iota(jnp.int32, (n * WINDOW, 1), 0) // WINDOW


def _per_pair_column(values, n):
    rows = _pair_rows(n)
    col = jnp.zeros((n * WINDOW, 1), F32) + values[0]
    for p in range(1, n):
        col = jnp.where(rows == p, values[p], col)
    return col


def _kv_lo_hi(x2, g):
    pair, half = divmod(g, 2)
    lo = _lane_lo()
    xg = x2[:, pair * 128:(pair + 1) * 128].astype(F32)
    xg = jnp.where(lo if half == 0 else ~lo, xg, 0.0)
    sw = pltpu.roll(xg, HEAD_DIM, 1)
    x_lo, x_hi = (xg, sw) if half == 0 else (sw, xg)
    return x_lo.astype(BF16), x_hi.astype(BF16)


def _softmax_sink(s, allowed, sink):
    s = jnp.where(allowed, s * (HEAD_DIM ** -0.5), NEG_INF)
    m = jnp.maximum(jnp.max(s, axis=-1, keepdims=True), sink)
    p = jnp.exp(s - m)
    es = jnp.exp(sink - m)
    inv = 1.0 / (jnp.sum(p, axis=-1, keepdims=True) + es)
    return p * inv, es * inv


def attn_fwd(qr, kr, vb, sinks, name):
    T, D = qr.shape
    KV = kr.shape[1]
    n_kv = KV // HEAD_DIM
    group = (D // HEAD_DIM) // n_kv
    nb = T // WINDOW

    npair = group // 2

    def body(sink_ref, q_ref, kp_ref, kc_ref, vp_ref, vc_ref, o_ref):
        i = pl.program_id(0)
        allowed = _band_mask(i, npair)
        k2 = jnp.concatenate([kp_ref[...], kc_ref[...]], axis=0)
        v2 = jnp.concatenate([vp_ref[...], vc_ref[...]], axis=0)
        outs = [None] * (D // 128)
        for g in range(n_kv):
            k_lo, k_hi = _kv_lo_hi(k2, g)
            v_lo, v_hi = _kv_lo_hi(v2, g)
            first = (g * group) // 2
            q = _stack_pairs(q_ref, first, npair)
            sink_e = _per_pair_column([sink_ref[0, g * group + 2 * p] for p in range(npair)], npair)
            sink_o = _per_pair_column([sink_ref[0, g * group + 2 * p + 1] for p in range(npair)], npair)
            pe, _ = _softmax_sink(_dot_nt(q, k_lo), allowed, sink_e)
            po, _ = _softmax_sink(_dot_nt(q, k_hi), allowed, sink_o)
            o = _dot(pe.astype(BF16), v_lo) + _dot(po.astype(BF16), v_hi)
            for p in range(npair):
                outs[first + p] = o[p * WINDOW:(p + 1) * WINDOW]
        o_ref[...] = jnp.concatenate(outs, axis=1).astype(BF16)

    prev = lambda i: (jnp.maximum(i - 1, 0), 0)
    cur = lambda i: (i, 0)
    return pl.pallas_call(
        body, name=name, grid=(nb,),
        in_specs=[pl.BlockSpec(memory_space=pltpu.SMEM),
                  pl.BlockSpec((WINDOW, D), cur),
                  pl.BlockSpec((WINDOW, KV), prev), pl.BlockSpec((WINDOW, KV), cur),
                  pl.BlockSpec((WINDOW, KV), prev), pl.BlockSpec((WINDOW, KV), cur)],
        out_specs=pl.BlockSpec((WINDOW, D), cur),
        out_shape=jax.ShapeDtypeStruct((T, D), BF16),
        compiler_params=_params(("parallel",)),
    )(sinks, qr, kr, kr, vb, vb)


def attn_bwd(qr, kr, vb, o, do, sinks, name):
    T, D = qr.shape
    KV = kr.shape[1]
    n_heads = D // HEAD_DIM
    n_kv = KV // HEAD_DIM
    group = n_heads // n_kv
    nb = T // WINDOW
    npair = group // 2
    scale = HEAD_DIM ** -0.5

    def body(sink_ref, q_ref, kp_ref, kc_ref, vp_ref, vc_ref, o_ref, do_ref,
             dq_ref, dk_ref, dv_ref, ds_ref, ck_ref, cv_ref):
        i = pl.program_id(0)
        lo = _lane_lo()

        @pl.when(i == 0)
        def _():
            ck_ref[...] = jnp.zeros_like(ck_ref)
            cv_ref[...] = jnp.zeros_like(cv_ref)
            ds_ref[...] = jnp.zeros_like(ds_ref)

        @pl.when(i < nb)
        def _():
            allowed = _band_mask(i, npair)
            rows = _pair_rows(npair)
            k2 = jnp.concatenate([kp_ref[...], kc_ref[...]], axis=0)
            v2 = jnp.concatenate([vp_ref[...], vc_ref[...]], axis=0)
            lane = lax.broadcasted_iota(jnp.int32, (1, 128), 1)
            dsink = jnp.zeros((1, 128), F32)
            dq_out = [None] * (D // 128)
            dk_pairs = [jnp.zeros((2 * WINDOW, 128), F32) for _ in range(KV // 128)]
            dv_pairs = [jnp.zeros((2 * WINDOW, 128), F32) for _ in range(KV // 128)]
            for g in range(n_kv):
                k_lo, k_hi = _kv_lo_hi(k2, g)
                v_lo, v_hi = _kv_lo_hi(v2, g)
                first = (g * group) // 2
                q = _stack_pairs(q_ref, first, npair)
                dop = _stack_pairs(do_ref, first, npair)
                dd = dop.astype(F32) * _stack_pairs(o_ref, first, npair).astype(F32)
                dq = jnp.zeros((npair * WINDOW, 128), F32)
                dkg = jnp.zeros((2 * WINDOW, 128), F32)
                dvg = jnp.zeros((2 * WINDOW, 128), F32)
                for parity, k_h, v_h, sel in ((0, k_lo, v_lo, lo), (1, k_hi, v_hi, ~lo)):
                    heads = [g * group + 2 * p + parity for p in range(npair)]
                    sink = _per_pair_column([sink_ref[0, h] for h in heads], npair)
                    p_, ps = _softmax_sink(_dot_nt(q, k_h), allowed, sink)
                    delta = jnp.sum(jnp.where(sel, dd, 0.0), axis=-1, keepdims=True)
                    dsc = (p_ * (_dot_nt(dop, v_h) - delta)).astype(BF16)
                    sd = -ps * delta
                    for p, h in enumerate(heads):
                        dsink += jnp.where(lane == h, jnp.sum(jnp.where(rows == p, sd, 0.0)), 0.0)
                    dq += _dot(dsc, k_h)
                    dkg += jnp.where(sel, _dot_tn(dsc, q), 0.0)
                    dvg += jnp.where(sel, _dot_tn(p_.astype(BF16), dop), 0.0)
                for p in range(npair):
                    dq_out[first + p] = dq[p * WINDOW:(p + 1) * WINDOW]
                pair, half = divmod(g, 2)
                keep = lo if half == 0 else ~lo
                dk_pairs[pair] += jnp.where(keep, dkg + pltpu.roll(dkg, HEAD_DIM, 1), 0.0) * scale
                dv_pairs[pair] += jnp.where(keep, dvg + pltpu.roll(dvg, HEAD_DIM, 1), 0.0)
            dq_ref[...] = jnp.concatenate(dq_out, axis=1) * scale
            dk2 = dk_pairs[0] if len(dk_pairs) == 1 else jnp.concatenate(dk_pairs, axis=1)
            dv2 = dv_pairs[0] if len(dv_pairs) == 1 else jnp.concatenate(dv_pairs, axis=1)
            dk_ref[...] = ck_ref[...] + dk2[:WINDOW]
            dv_ref[...] = cv_ref[...] + dv2[:WINDOW]
            ck_ref[...] = dk2[WINDOW:]
            cv_ref[...] = dv2[WINDOW:]
            ds_ref[pl.ds(0, 1), :] += dsink

        @pl.when(i == nb)
        def _():
            dk_ref[...] = ck_ref[...]
            dv_ref[...] = cv_ref[...]

    prev = lambda i: (jnp.maximum(i - 1, 0), 0)
    cur = lambda i: (jnp.minimum(i, nb - 1), 0)
    prevc = lambda i: (jnp.maximum(jnp.minimum(i, nb - 1) - 1, 0), 0)
    return pl.pallas_call(
        body, name=name, grid=(nb + 1,),
        in_specs=[pl.BlockSpec(memory_space=pltpu.SMEM),
                  pl.BlockSpec((WINDOW, D), cur),
                  pl.BlockSpec((WINDOW, KV), prevc), pl.BlockSpec((WINDOW, KV), cur),
                  pl.BlockSpec((WINDOW, KV), prevc), pl.BlockSpec((WINDOW, KV), cur),
                  pl.BlockSpec((WINDOW, D), cur), pl.BlockSpec((WINDOW, D), cur)],
        out_specs=[pl.BlockSpec((WINDOW, D), cur), pl.BlockSpec((WINDOW, KV), prev),
                   pl.BlockSpec((WINDOW, KV), prev), pl.BlockSpec((8, 128), lambda i: (0, 0))],
        out_shape=[jax.ShapeDtypeStruct((T, D), F32), jax.ShapeDtypeStruct((T, KV), F32),
                   jax.ShapeDtypeStruct((T, KV), F32), jax.ShapeDtypeStruct((8, 128), F32)],
        scratch_shapes=[pltpu.VMEM((WINDOW, KV), F32), pltpu.VMEM((WINDOW, KV), F32)],
        compiler_params=_params(("arbitrary",)),
    )(sinks, qr, kr, kr, vb, vb, o, do)


def merge_fwd(x, c3, o, p_gate, gate_b, w_proj, w_o, w_out, name):
    T, D = x.shape
    tm = min(TM_ROW, T)

    def body(x_ref, c3_ref, o_ref, gc_ref, ga_ref, bc_ref, ba_ref, wp_ref, wo_ref, wout_ref,
             xo_ref, co_ref, ao_ref, mg_ref):
        conv_out = _dot(c3_ref[...], wp_ref[...])
        attn_out = _dot(o_ref[...], wo_ref[...])
        merged = (_sigmoid(gc_ref[...] + bc_ref[...]) * conv_out
                  + _sigmoid(ga_ref[...] + ba_ref[...]) * attn_out).astype(BF16)
        co_ref[...] = conv_out.astype(BF16)
        ao_ref[...] = attn_out.astype(BF16)
        mg_ref[...] = merged
        xo_ref[...] = x_ref[...] + _dot(merged, wout_ref[...])

    blk = lambda j: pl.BlockSpec((tm, D), lambda i: (i, j))
    row = lambda j: pl.BlockSpec((1, D), lambda i: (0, j))
    mat = pl.BlockSpec((D, D), lambda i: (0, 0))
    return pl.pallas_call(
        body, name=name, grid=(T // tm,),
        in_specs=[blk(0), blk(0), blk(0), blk(0), blk(1), row(0), row(1), mat, mat, mat],
        out_specs=[blk(0), blk(0), blk(0), blk(0)],
        out_shape=[jax.ShapeDtypeStruct((T, D), F32)] + [jax.ShapeDtypeStruct((T, D), BF16)] * 3,
        compiler_params=_params(("parallel",)),
    )(x, c3, o, p_gate, p_gate, gate_b, gate_b, w_proj, w_o, w_out)


def merge_bwd(dx, p_gate, gate_b, conv_out, attn_out, c1, ln_g, ln_b, w_proj, w_o, w_out, name, side=None):
    T, D = dx.shape
    tm = min(TM_ROW, T)

    def body(dx_ref, gc_ref, ga_ref, bc_ref, ba_ref, co_ref, ao_ref, c1_ref, g_ref, be_ref,
             wp_ref, wo_ref, wout_ref, dgt_ref, dco_ref, dao_ref, do_ref, dc1_ref, sm_ref):
        @pl.when(pl.program_id(0) == 0)
        def _():
            sm_ref[...] = jnp.zeros_like(sm_ref)

        dm = _dot_nt(dx_ref[...].astype(BF16), wout_ref[...])
        sc = _sigmoid(gc_ref[...] + bc_ref[...])
        sa = _sigmoid(ga_ref[...] + ba_ref[...])
        dco = (dm * sc).astype(BF16)
        dao = (dm * sa).astype(BF16)
        dgc = dm * co_ref[...].astype(F32) * sc * (1.0 - sc)
        dga = dm * ao_ref[...].astype(F32) * sa * (1.0 - sa)
        dgt_ref[:, pl.ds(0, D)] = dgc.astype(BF16)
        dgt_ref[:, pl.ds(D, D)] = dga.astype(BF16)
        dco_ref[...] = dco
        dao_ref[...] = dao
        do_ref[...] = _dot_nt(dao, wo_ref[...]).astype(BF16)
        dc3 = _dot_nt(dco, wp_ref[...])
        xhat, rstd = _layernorm_stats(c1_ref[...])
        c2 = xhat * g_ref[...] + be_ref[...]
        dc2 = dc3 * _silu_grad(c2, _sigmoid(c2))
        dxh = dc2 * g_ref[...]
        dc1 = rstd * (dxh - jnp.mean(dxh, axis=-1, keepdims=True)
                      - xhat * jnp.mean(dxh * xhat, axis=-1, keepdims=True))
        dc1_ref[...] = dc1
        colsum = lambda v: jnp.sum(v, axis=0, keepdims=True)
        for r, (left, right) in enumerate(((dgc, dga), (dc2 * xhat, dc2), (dc1, None))):
            sm_ref[pl.ds(r, 1), pl.ds(0, D)] += colsum(left)
            if right is not None:
                sm_ref[pl.ds(r, 1), pl.ds(D, D)] += colsum(right)

    blk = lambda j: pl.BlockSpec((tm, D), lambda i: (i, j))
    row = lambda j: pl.BlockSpec((1, D), lambda i: (0, j))
    mat = pl.BlockSpec((D, D), lambda i: (0, 0))
    return _call_hosting(
        body, side, name=name, grid=(T // tm,),
        in_specs=[blk(0), blk(0), blk(1), row(0), row(1), blk(0), blk(0), blk(0), row(0), row(0), mat, mat, mat],
        out_specs=[pl.BlockSpec((tm, 2 * D), lambda i: (i, 0)), blk(0), blk(0), blk(0), blk(0),
                   pl.BlockSpec((8, 2 * D), lambda i: (0, 0))],
        out_shape=[jax.ShapeDtypeStruct((T, 2 * D), BF16)] + [jax.ShapeDtypeStruct((T, D), BF16)] * 3
                  + [jax.ShapeDtypeStruct((T, D), F32), jax.ShapeDtypeStruct((8, 2 * D), F32)],
        scratch_shapes=[],
        operands=(dx, p_gate, p_gate, gate_b, gate_b, conv_out, attn_out, c1, ln_g, ln_b, w_proj, w_o, w_out))


def loss_head(x, nw, target, name):
    T, D = x.shape
    tm = min(TM_ROW, T)

    def body(x_ref, nw_ref, t_ref, dx_ref, sm_ref):
        @pl.when(pl.program_id(0) == 0)
        def _():
            sm_ref[...] = jnp.zeros_like(sm_ref)

        xv = x_ref[...]
        err = xv * _rms_scale(xv) * nw_ref[...] - t_ref[...]
        loss = 0.5 * jnp.sum(jnp.mean(err * err, axis=-1, keepdims=True))
        dxn, dnw = _rms_bwd(xv, nw_ref[...], err * (1.0 / D))
        dx_ref[...] = dxn
        sm_ref[pl.ds(0, 1), :] += dnw
        sm_ref[pl.ds(1, 1), :] += jnp.zeros((1, D), F32) + loss

    return pl.pallas_call(
        body, name=name, grid=(T // tm,),
        in_specs=[pl.BlockSpec((tm, D), lambda i: (i, 0)), pl.BlockSpec((1, D), lambda i: (0, 0)),
                  pl.BlockSpec((tm, D), lambda i: (i, 0))],
        out_specs=[pl.BlockSpec((tm, D), lambda i: (i, 0)), pl.BlockSpec((8, D), lambda i: (0, 0))],
        out_shape=[jax.ShapeDtypeStruct((T, D), F32), jax.ShapeDtypeStruct((8, D), F32)],
        compiler_params=_params(("arbitrary",)),
    )(x, nw, target)


def adamw(w, g, m, v, name):
    R, C = w.shape
    tr = _row_tile(R, TR_ELT)

    def body(w_ref, g_ref, m_ref, v_ref, d_ref, mo_ref, vo_ref):
        gv = g_ref[...]
        mn = ADAM_B1 * m_ref[...] + (1.0 - ADAM_B1) * gv
        vn = ADAM_B2 * v_ref[...] + (1.0 - ADAM_B2) * (gv * gv)
        m_hat = mn / (1.0 - ADAM_B1 ** ADAM_STEP)
        v_hat = vn / (1.0 - ADAM_B2 ** ADAM_STEP)
        d_ref[...] = -ADAM_LR * (m_hat / (jnp.sqrt(v_hat) + ADAM_EPS) + ADAM_WD * w_ref[...])
        mo_ref[...] = mn
        vo_ref[...] = vn

    spec = pl.BlockSpec((tr, C), lambda i: (i, 0))
    return pl.pallas_call(
        body, name=name, grid=(R // tr,), in_specs=[spec] * 4, out_specs=[spec] * 3,
        out_shape=[jax.ShapeDtypeStruct((R, C), F32)] * 3,
        compiler_params=_params(("parallel",)),
    )(w, g, m, v)


def _place():
    return lax.axis_index("x"), lax.axis_index("y"), lax.axis_index("c")


def place_shard(place, w, dtype, name):
    R, C = w.shape
    tr = _row_tile(R, TR_ELT)

    def body(pc_ref, w_ref, o_ref):
        o_ref[...] = w_ref[...].astype(dtype)

    return pl.pallas_call(
        body, name=name,
        grid_spec=pltpu.PrefetchScalarGridSpec(
            num_scalar_prefetch=1, grid=(R // tr,),
            in_specs=[pl.BlockSpec((tr, C), lambda r, pc: (r, 0))],
            out_specs=pl.BlockSpec((None, tr, C), lambda r, pc: (pc[0], r, 0))),
        out_shape=jax.ShapeDtypeStruct((N_CHIPS, R, C), dtype),
        compiler_params=_params(("arbitrary",)),
    )(place, w)


def gather_side(shards, small):
    n, ns = len(shards), len(small)

    def ici_copy(dst, sems, k, j, x, y, c, sending):
        px, py = x ^ (j >> 1), y ^ (j & 1)
        slot = 2 * x + y if sending else 2 * px + py
        half = dst[k].shape[1] // 2
        part = dst[k].at[slot, pl.ds(c * half, half)] if k < n else dst[k].at[slot]
        return pltpu.make_async_remote_copy(part, part, sems[0].at[3 * k + j - 1], sems[1].at[3 * k + j - 1],
                                            device_id=(px, py, c), device_id_type=MESH)

    def d2d_copy(dst, sems, k, j, x, y, c, sending):
        half = dst[k].shape[1] // 2
        part = dst[k].at[2 * (x ^ (j >> 1)) + (y ^ (j & 1)), pl.ds((c if sending else 1 - c) * half, half)]
        return pltpu.make_async_remote_copy(part, part, sems[2].at[3 * k + j - 1], sems[3].at[3 * k + j - 1],
                                            device_id=(x, y, 1 - c), device_id_type=MESH)

    def start(src, dst, sems):
        x, y, c = _place()
        for k in range(n + ns):
            for j in (1, 2, 3):
                ici_copy(dst, sems, k, j, x, y, c, True).start()

    def finish(src, dst, sems):
        x, y, c = _place()
        for k in range(n + ns):
            for j in (1, 2, 3):
                ici_copy(dst, sems, k, j, x, y, c, False).wait_recv()
                if k < n:
                    d2d_copy(dst, sems, k, j, x, y, c, True).start()
        for k in range(n):
            for j in (1, 2, 3):
                d2d_copy(dst, sems, k, j, x, y, c, False).wait_recv()
        for k in range(n + ns):
            for j in (1, 2, 3):
                ici_copy(dst, sems, k, j, x, y, c, True).wait_send()
                if k < n:
                    d2d_copy(dst, sems, k, j, x, y, c, True).wait_send()

    arrays = list(shards) + list(small)
    return dict(inputs=arrays, out_shapes=[jax.ShapeDtypeStruct(a.shape, a.dtype) for a in arrays],
                aliases={k: k for k in range(n + ns)},
                sems=[pltpu.SemaphoreType.DMA((3 * (n + ns),)), pltpu.SemaphoreType.DMA((3 * (n + ns),)),
                      pltpu.SemaphoreType.DMA((3 * n,)), pltpu.SemaphoreType.DMA((3 * n,))],
                start=start, finish=finish)


def run_side(side, name):
    n_in, n_out = len(side["inputs"]), len(side["out_shapes"])

    def body(*refs):
        src, dst, sems = refs[:n_in], refs[n_in:n_in + n_out], refs[n_in + n_out:]
        side["start"](src, dst, sems)
        side["finish"](src, dst, sems)

    return pl.pallas_call(
        body, name=name, in_specs=[HBM_SPEC] * n_in, out_specs=[HBM_SPEC] * n_out,
        out_shape=side["out_shapes"], input_output_aliases=side["aliases"], scratch_shapes=side["sems"],
    )(*side["inputs"])


def allreduce_small(block):
    R, C = block.shape

    def body(x_ref, out_ref, all_ref, send_sems, recv_sems, local_sem):
        x, y, c = _place()
        me, sibling = (x, y, c), (x, y, 1 - c)
        chips = [(1 - x, y), (x, 1 - y), (1 - x, 1 - y)]

        def slot(px, py, pc):
            return all_ref.at[4 * px + 2 * py + pc]

        def copy(k, block_of, to, src=None):
            return pltpu.make_async_remote_copy(
                src_ref=slot(*block_of) if src is None else src, dst_ref=slot(*block_of),
                send_sem=send_sems.at[k], recv_sem=recv_sems.at[k], device_id=to, device_id_type=MESH)

        mine = pltpu.make_async_copy(x_ref, slot(*me), local_sem)
        mine.start()
        first = [copy(0, me, sibling, src=x_ref)]
        first += [copy(1 + j, me, (*chip, c), src=x_ref) for j, chip in enumerate(chips)]
        for cp in first:
            cp.start()
        passed = [copy(4 + j, (*chip, c), sibling) for j, chip in enumerate(chips)]
        for j, chip in enumerate(chips):
            copy(1 + j, (*chip, c), me).wait_recv()
            passed[j].start()
        copy(0, sibling, me).wait_recv()
        for j, chip in enumerate(chips):
            copy(4 + j, (*chip, 1 - c), me).wait_recv()
        for cp in first + passed:
            cp.wait_send()
        mine.wait()
        total = all_ref[0]
        for d in range(1, N_DEV):
            total = total + all_ref[d]
        out_ref[...] = total

    return pl.pallas_call(
        body, name="allreduce_small",
        in_specs=[pl.BlockSpec(memory_space=pltpu.VMEM)], out_specs=pl.BlockSpec(memory_space=pltpu.VMEM),
        out_shape=jax.ShapeDtypeStruct((R, C), F32),
        scratch_shapes=[pltpu.VMEM((N_DEV, R, C), F32), pltpu.SemaphoreType.DMA((7,)),
                        pltpu.SemaphoreType.DMA((7,)), pltpu.SemaphoreType.DMA],
        compiler_params=pltpu.CompilerParams(vmem_limit_bytes=VMEM_LIMIT),
    )(block)


def exchange_siblings_side(grads):
    n = len(grads)

    def copies(src, dst, sems):
        x, y, c = _place()
        for k in range(n):
            half = src[k].shape[1] // 2
            yield pltpu.make_async_remote_copy(src[k].at[:, pl.ds((1 - c) * half, half)], dst[k],
                                               sems[0].at[k], sems[1].at[k],
                                               device_id=(x, y, 1 - c), device_id_type=MESH)

    def start(src, dst, sems):
        for cp in copies(src, dst, sems):
            cp.start()

    def finish(src, dst, sems):
        for cp in copies(src, dst, sems):
            cp.wait()

    return dict(inputs=list(grads), aliases={},
                out_shapes=[jax.ShapeDtypeStruct((N_CHIPS, g.shape[1] // 2, g.shape[2]), F32) for g in grads],
                sems=[pltpu.SemaphoreType.DMA((n,)), pltpu.SemaphoreType.DMA((n,))], start=start, finish=finish)


def rs_chip_sum(place, grad, sib, name):
    NP, R, C = grad.shape
    half = R // 2
    tr = _row_tile(half, TR_ELT)
    nr = half // tr

    def body(pc_ref, g_ref, s_ref, wire_ref, own_ref):
        q = pl.program_id(1)
        total = g_ref[...] + s_ref[...]
        wire_ref[...] = total.astype(BF16)

        @pl.when(q == pc_ref[0])
        def _():
            own_ref[...] = total

    return pl.pallas_call(
        body, name=name,
        grid_spec=pltpu.PrefetchScalarGridSpec(
            num_scalar_prefetch=1, grid=(nr, NP),
            in_specs=[pl.BlockSpec((None, tr, C), lambda r, q, pc: (q, pc[1] * nr + r, 0)),
                      pl.BlockSpec((None, tr, C), lambda r, q, pc: (q, r, 0))],
            out_specs=[pl.BlockSpec((None, tr, C), lambda r, q, pc: (q, r, 0)),
                       pl.BlockSpec((tr, C), lambda r, q, pc: (r, 0))]),
        out_shape=[jax.ShapeDtypeStruct((NP, half, C), BF16), jax.ShapeDtypeStruct((half, C), F32)],
        compiler_params=_params(("arbitrary", "arbitrary")),
    )(place, grad, sib)


def exchange_chips_side(wires):
    n = len(wires)

    def copies(src, dst, sems):
        x, y, c = _place()
        for k in range(n):
            for j in (1, 2, 3):
                qx, qy = x ^ (j >> 1), y ^ (j & 1)
                yield pltpu.make_async_remote_copy(src[k].at[2 * qx + qy], dst[k].at[2 * x + y],
                                                   sems[0].at[3 * k + j - 1], sems[1].at[3 * k + j - 1],
                                                   device_id=(qx, qy, c), device_id_type=MESH)

    def start(src, dst, sems):
        for cp in copies(src, dst, sems):
            cp.start()

    def finish(src, dst, sems):
        for cp in copies(src, dst, sems):
            cp.wait()

    return dict(inputs=list(wires), out_shapes=[jax.ShapeDtypeStruct(w.shape, BF16) for w in wires], aliases={},
                sems=[pltpu.SemaphoreType.DMA((3 * n,)), pltpu.SemaphoreType.DMA((3 * n,))],
                start=start, finish=finish)


SEM_SPEC = pl.BlockSpec(memory_space=pltpu.SEMAPHORE)


def exchange_chips_start(wires, name):
    n = len(wires)
    side = exchange_chips_side(wires)

    def body(*refs):
        src, land, sems = refs[:n], refs[n:2 * n], refs[2 * n:2 * n + 2]
        side["start"](src, land, sems)
        refs[-1][...] = jnp.zeros_like(refs[-1])

    hbm = [pltpu.HBM(w.shape, w.dtype) for w in wires]
    outs = pl.pallas_call(
        body, name=name, in_specs=[HBM_SPEC] * (2 * n),
        out_specs=[SEM_SPEC, SEM_SPEC] + [HBM_SPEC] * (2 * n) + [pl.BlockSpec(memory_space=pltpu.VMEM)],
        out_shape=list(side["sems"]) + hbm + hbm + [jax.ShapeDtypeStruct((8, 128), F32)],
        input_output_aliases={k: 2 + k for k in range(2 * n)},
        compiler_params=pltpu.CompilerParams(has_side_effects=pltpu.SideEffectType.DATAFLOW_SIDE_EFFECTING),
    )(*[pltpu.with_memory_space_constraint(w, pltpu.HBM) for w in wires],
      *[pltpu.with_memory_space_constraint(lax.empty(w.shape, w.dtype), pltpu.HBM) for w in wires])
    return outs[0], outs[1], outs[2:2 + n], outs[2 + n:2 + 2 * n], outs[-1]


def exchange_chips_wait(send_sems, recv_sems, wires, lands, after, name):
    n = len(wires)
    side = exchange_chips_side(wires)

    def body(*refs):
        side["finish"](refs[:n], refs[n:2 * n], refs[2 * n:2 * n + 2])

    hbm = [pltpu.HBM(w.shape, w.dtype) for w in wires]
    outs = pl.pallas_call(
        body, name=name, in_specs=[HBM_SPEC] * (2 * n) + [SEM_SPEC, SEM_SPEC] + [HBM_SPEC] * len(after),
        out_specs=[HBM_SPEC] * (2 * n), out_shape=hbm + hbm,
        input_output_aliases={k: k for k in range(2 * n)},
        compiler_params=pltpu.CompilerParams(has_side_effects=pltpu.SideEffectType.DATAFLOW_SIDE_EFFECTING),
    )(*wires, *lands, send_sems, recv_sems, *after)
    return outs[n:]


def rs_final_sum(place, own, got, after, name):
    NP, half, C = got.shape
    tr = _row_tile(half, TR_ELT)
    nr = half // tr

    def body(pc_ref, own_ref, g1_ref, g2_ref, g3_ref, after_ref, out_ref):
        out_ref[...] = ((own_ref[...] + g1_ref[...].astype(F32)) + g2_ref[...].astype(F32)) + g3_ref[...].astype(F32)

    slot = lambda j: pl.BlockSpec((None, tr, C), lambda r, pc: (pc[0] ^ j, r, 0))
    return pl.pallas_call(
        body, name=name,
        grid_spec=pltpu.PrefetchScalarGridSpec(
            num_scalar_prefetch=1, grid=(nr,),
            in_specs=[pl.BlockSpec((tr, C), lambda r, pc: (r, 0)), slot(1), slot(2), slot(3),
                      pl.BlockSpec((8, 128), lambda r, pc: (0, 0))],
            out_specs=pl.BlockSpec((tr, C), lambda r, pc: (pc[1] * nr + r, 0))),
        out_shape=jax.ShapeDtypeStruct((2 * half, C), F32),
        compiler_params=_params(("arbitrary",)),
    )(place, own, got, got, got, after)


def rs_share_siblings(totals, name):
    n = len(totals)

    def body(*refs):
        dst = refs[n:2 * n]
        send_sems, recv_sems = refs[2 * n:]
        x, y, c = _place()
        copies = []
        for k in range(n):
            half = dst[k].shape[0] // 2
            rows = dst[k].at[pl.ds(c * half, half)]
            cp = pltpu.make_async_remote_copy(rows, rows, send_sems.at[k], recv_sems.at[k],
                                              device_id=(x, y, 1 - c), device_id_type=MESH)
            cp.start()
            copies.append(cp)
        for k, cp in enumerate(copies):
            cp.wait_send()
            half = dst[k].shape[0] // 2
            got = dst[k].at[pl.ds((1 - c) * half, half)]
            pltpu.make_async_remote_copy(got, got, send_sems.at[k], recv_sems.at[k],
                                         device_id=(x, y, c), device_id_type=MESH).wait_recv()

    return pl.pallas_call(
        body, name=name,
        in_specs=[HBM_SPEC] * n, out_specs=[HBM_SPEC] * n,
        out_shape=[jax.ShapeDtypeStruct(t.shape, F32) for t in totals],
        input_output_aliases={k: k for k in range(n)},
        scratch_shapes=[pltpu.SemaphoreType.DMA((n,)), pltpu.SemaphoreType.DMA((n,))],
    )(*totals)


def rs_to_wires(place, grads, tag, sibs=None):
    if sibs is None:
        sibs = run_side(exchange_siblings_side(grads), f"rs_exchange_siblings_{tag}")
    wires, owns = [], []
    for k, (g, s) in enumerate(zip(grads, sibs)):
        w, o = rs_chip_sum(place, g, s, f"rs_chip_sum_{tag}{k}")
        wires.append(w)
        owns.append(o)
    return wires, owns


def rs_finish(place, owns, gots, after, tag):
    totals = [rs_final_sum(place, o, g, after, f"rs_final_sum_{tag}{k}") for k, (o, g) in enumerate(zip(owns, gots))]
    return rs_share_siblings(totals, f"rs_share_siblings_{tag}")


def _rope_tables(positions):
    half = HEAD_DIM // 2
    inv_freq = ROPE_THETA ** (-jnp.arange(half, dtype=F32) / half)
    ang = positions.astype(F32)[:, None] * inv_freq
    cos, sin = jnp.cos(ang), jnp.sin(ang)
    return jnp.tile(cos, (1, 4)), jnp.concatenate([-sin, sin, -sin, sin], axis=1)


def _cols_from_pieces(pieces, start, stop):
    C = pieces.shape[2]
    parts = []
    for q in range(N_CHIPS):
        lo, hi = max(start, q * C), min(stop, (q + 1) * C)
        if lo < hi:
            parts.append(pieces[q][:, lo - q * C:hi - q * C])
    return parts[0] if len(parts) == 1 else jnp.concatenate(parts, axis=1)


def _pieces_from_groups(groups):
    C = sum(g.shape[1] for g in groups) // N_CHIPS
    pieces = []
    for q in range(N_CHIPS):
        parts, off = [], 0
        for g in groups:
            lo, hi = max(q * C, off), min((q + 1) * C, off + g.shape[1])
            if lo < hi:
                parts.append(g[:, lo - off:hi - off])
            off += g.shape[1]
        pieces.append(parts[0] if len(parts) == 1 else jnp.concatenate(parts, axis=1))
    return jnp.stack(pieces)


def kernel(x, positions, ffn1_norm, ffn1_w_gate, ffn1_w_up, ffn1_w_down, mix_norm, w_in, conv_dw_w, conv_dw_b, conv_ln_g, conv_ln_b, conv_w_proj, attn_sinks, attn_w_o, gate_b, w_out, ffn2_norm, ffn2_w_gate, ffn2_w_up, ffn2_w_down, final_norm, loss_target, m_ffn1_norm, m_ffn1_w_gate, m_ffn1_w_up, m_ffn1_w_down, m_mix_norm, m_w_in, m_conv_dw_w, m_conv_dw_b, m_conv_ln_g, m_conv_ln_b, m_conv_w_proj, m_attn_sinks, m_attn_w_o, m_gate_b, m_w_out, m_ffn2_norm, m_ffn2_w_gate, m_ffn2_w_up, m_ffn2_w_down, m_final_norm, v_ffn1_norm, v_ffn1_w_gate, v_ffn1_w_up, v_ffn1_w_down, v_mix_norm, v_w_in, v_conv_dw_w, v_conv_dw_b, v_conv_ln_g, v_conv_ln_b, v_conv_w_proj, v_attn_sinks, v_attn_w_o, v_gate_b, v_w_out, v_ffn2_norm, v_ffn2_w_gate, v_ffn2_w_up, v_ffn2_w_down, v_final_norm):
    weights = dict(ffn1_norm=ffn1_norm, ffn1_w_gate=ffn1_w_gate, ffn1_w_up=ffn1_w_up, ffn1_w_down=ffn1_w_down,
                   mix_norm=mix_norm, w_in=w_in, conv_dw_w=conv_dw_w, conv_dw_b=conv_dw_b, conv_ln_g=conv_ln_g,
                   conv_ln_b=conv_ln_b, conv_w_proj=conv_w_proj, attn_sinks=attn_sinks, attn_w_o=attn_w_o,
                   gate_b=gate_b, w_out=w_out, ffn2_norm=ffn2_norm, ffn2_w_gate=ffn2_w_gate, ffn2_w_up=ffn2_w_up,
                   ffn2_w_down=ffn2_w_down, final_norm=final_norm)
    m_in = dict(ffn1_norm=m_ffn1_norm, ffn1_w_gate=m_ffn1_w_gate, ffn1_w_up=m_ffn1_w_up, ffn1_w_down=m_ffn1_w_down,
                mix_norm=m_mix_norm, w_in=m_w_in, conv_dw_w=m_conv_dw_w, conv_dw_b=m_conv_dw_b,
                conv_ln_g=m_conv_ln_g, conv_ln_b=m_conv_ln_b, conv_w_proj=m_conv_w_proj, attn_sinks=m_attn_sinks,
                attn_w_o=m_attn_w_o, gate_b=m_gate_b, w_out=m_w_out, ffn2_norm=m_ffn2_norm,
                ffn2_w_gate=m_ffn2_w_gate, ffn2_w_up=m_ffn2_w_up, ffn2_w_down=m_ffn2_w_down, final_norm=m_final_norm)
    v_in = dict(ffn1_norm=v_ffn1_norm, ffn1_w_gate=v_ffn1_w_gate, ffn1_w_up=v_ffn1_w_up, ffn1_w_down=v_ffn1_w_down,
                mix_norm=v_mix_norm, w_in=v_w_in, conv_dw_w=v_conv_dw_w, conv_dw_b=v_conv_dw_b,
                conv_ln_g=v_conv_ln_g, conv_ln_b=v_conv_ln_b, conv_w_proj=v_conv_w_proj, attn_sinks=v_attn_sinks,
                attn_w_o=v_attn_w_o, gate_b=v_gate_b, w_out=v_w_out, ffn2_norm=v_ffn2_norm,
                ffn2_w_gate=v_ffn2_w_gate, ffn2_w_up=v_ffn2_w_up, ffn2_w_down=v_ffn2_w_down, final_norm=v_final_norm)
    names = list(weights)
    big = ["ffn1_w_gate", "ffn1_w_up", "ffn1_w_down", "w_in", "conv_w_proj", "attn_w_o", "w_out",
           "ffn2_w_gate", "ffn2_w_up", "ffn2_w_down"]
    transposed = [k for k in big if k.endswith(("w_gate", "w_up"))]
    for k in transposed:
        weights[k], m_in[k], v_in[k] = (jnp.swapaxes(a, 1, 2) for a in (weights[k], m_in[k], v_in[k]))

    xs = x[0]
    T, D = xs.shape
    KV = (w_in.shape[2] * N_CHIPS - 5 * D) // 2
    n_heads = D // HEAD_DIM
    my_chip = 2 * lax.axis_index("x") + lax.axis_index("y")
    place = jnp.stack([my_chip, lax.axis_index("c")]).astype(jnp.int32)

    placed = {k: place_shard(place, weights[k][0], BF16, f"place_{k}") for k in big}
    placed_dw = place_shard(place, conv_dw_w[0], F32, "place_conv_dw_w")
    first, later = big[:3], big[3:]
    wg1, wu1, wd1 = run_side(gather_side([placed[k] for k in first], []), "gather_ffn1")
    x1, h1, g1, u1, *gathered = ffn_fwd(x[0], ffn1_norm, wg1, wu1, wd1, "ffn1_fwd",
                                        side=gather_side([placed[k] for k in later], [placed_dw]))
    full = dict(zip(later + ["conv_dw_w"], gathered))
    wg2, wu2, wd2 = full["ffn2_w_gate"], full["ffn2_w_up"], full["ffn2_w_down"]
    w_glu = _cols_from_pieces(full["w_in"], 0, 2 * D)
    w_qkv = _cols_from_pieces(full["w_in"], 2 * D, 3 * D + 2 * KV)
    w_gate = _cols_from_pieces(full["w_in"], 3 * D + 2 * KV, 5 * D + 2 * KV)
    w_proj = full["conv_w_proj"].reshape(D, D)
    w_o = full["attn_w_o"].reshape(D, D)
    w_out_f = full["w_out"].reshape(D, D)
    dw_w = full["conv_dw_w"].transpose(1, 0, 2).reshape(CONV_WIDTH, D)
    dw_w = jnp.concatenate([dw_w, jnp.zeros((CONV_HALO - CONV_WIDTH, D), F32)], axis=0)
    cs, sn = _rope_tables(positions[0])
    fn_row = final_norm.reshape(1, D)

    h2, p_glu, p_gate, qr, kr, vb = mix_in_fwd(x1, mix_norm, w_glu, w_qkv, w_gate, cs, sn, "mix_in_fwd")
    c1, c3 = conv_fwd(p_glu, dw_w, conv_dw_b, conv_ln_g, conv_ln_b, "conv_fwd")
    o = attn_fwd(qr, kr, vb, attn_sinks, "attn_fwd")
    x2, conv_out, attn_out, merged = merge_fwd(x1, c3, o, p_gate, gate_b, w_proj, w_o, w_out_f, "merge_fwd")
    x3, h3, g2, u2 = ffn_fwd(x2, ffn2_norm, wg2, wu2, wd2, "ffn2_fwd")

    dx3, head_sums = loss_head(x3, fn_row, loss_target[0], "loss_head")
    dx2, dwg2, dwu2, dwd2, d_ffn2_norm = ffn_bwd(x2, ffn2_norm, h3, g2, u2, wg2, wu2, wd2, dx3, "ffn2_bwd")
    ffn2_grads = [dwg2, dwu2, dwd2]
    d_gates, d_conv_out, d_attn_out, d_o, dc1, merge_sums, *sibs_f2 = merge_bwd(
        dx2, p_gate, gate_b, conv_out, attn_out, c1, conv_ln_g, conv_ln_b, w_proj, w_o, w_out_f, "merge_bwd",
        side=exchange_siblings_side(ffn2_grads))
    d_w_out = matmul_tn(merged, dx2, "d_w_out")
    d_w_proj = matmul_tn(c3, d_conv_out, "d_conv_w_proj")
    d_w_o = matmul_tn(o, d_attn_out, "d_attn_w_o")
    wires_f2, owns_f2 = rs_to_wires(place, ffn2_grads, "ffn2", sibs=sibs_f2)
    d_glu, d_dw_w, *gots_f2 = conv_bwd(p_glu, dc1, dw_w, "conv_bwd", side=exchange_chips_side(wires_f2))
    dq, dk, dv, d_sinks = attn_bwd(qr, kr, vb, o, d_o, attn_sinks, "attn_bwd")
    d_qkv = rope_bwd(dq, dk, dv, cs, sn, "rope_bwd")
    dx1, d_mix_norm = mix_in_bwd([d_glu, d_qkv, d_gates], [w_glu, w_qkv, w_gate], x1, mix_norm, dx2, "mix_in_bwd")
    d_w_in = _pieces_from_groups([matmul_tn(h2, d_glu, "d_w_in_glu"), matmul_tn(h2, d_qkv, "d_w_in_qkv"),
                                  matmul_tn(h2, d_gates, "d_w_in_gate")])
    dwc = D // N_CHIPS
    mixer_grads = [d_w_in, d_w_proj.reshape(N_CHIPS, dwc, D), d_w_o.reshape(N_CHIPS, dwc, D),
                   d_w_out.reshape(N_CHIPS, dwc, D)]
    wires_m, owns_m = rs_to_wires(place, mixer_grads, "mixer")
    dx0, dwg1, dwu1, dwd1, d_ffn1_norm, *gots_m = ffn_bwd(xs, ffn1_norm, h1, g1, u1, wg1, wu1, wd1, dx1, "ffn1_bwd",
                                                          side=exchange_chips_side(wires_m))
    wires_l, owns_l = rs_to_wires(place, [dwg1, dwu1, dwd1], "ffn1")
    send_sems, recv_sems, wires_l, lands_l, token = exchange_chips_start(wires_l, "rs_exchange_chips_ffn1_start")
    early_names = ["ffn2_w_gate", "ffn2_w_up", "ffn2_w_down", "w_in", "conv_w_proj", "attn_w_o", "w_out"]
    late_names = ["ffn1_w_gate", "ffn1_w_up", "ffn1_w_down"]
    reduced_early = rs_finish(place, owns_f2 + owns_m, list(gots_f2) + list(gots_m), token, "early")

    pad_row = lambda v: jnp.pad(v, ((0, 0), (0, D - v.shape[1])))
    small_rows = jnp.concatenate([
        d_ffn1_norm, d_mix_norm, merge_sums[2:3, :D], merge_sums[1:2, :D], merge_sums[1:2, D:],
        pad_row(d_sinks[0:1, :n_heads]), merge_sums[0:1, :D], merge_sums[0:1, D:], d_ffn2_norm,
        head_sums[0:1], head_sums[1:2], jnp.zeros((5, D), F32), d_dw_w], axis=0)
    small = allreduce_small(small_rows)
    loss = small[10, 0]
    grads = {"ffn1_norm": small[0:1], "mix_norm": small[1:2], "conv_dw_b": small[2:3], "conv_ln_g": small[3:4],
             "conv_ln_b": small[4:5], "attn_sinks": small[5:6, :n_heads],
             "gate_b": jnp.concatenate([small[6:7], small[7:8]], axis=1), "ffn2_norm": small[8:9],
             "final_norm": small[9:10]}
    grads["conv_dw_w"] = lax.dynamic_slice(small[16:16 + CONV_WIDTH], (0, my_chip * dwc), (CONV_WIDTH, dwc))
    grads.update(zip(early_names, reduced_early))

    deltas, new_m, new_v = {}, {}, {}

    def apply_adamw(k):
        shape = weights[k].shape
        g2d = grads[k].reshape(-1, shape[-1])
        grads[k] = g2d.reshape(shape)
        d, mn, vn = adamw(weights[k].reshape(g2d.shape), g2d, m_in[k].reshape(g2d.shape),
                          v_in[k].reshape(g2d.shape), f"adamw_{k}")
        deltas[k], new_m[k], new_v[k] = d.reshape(shape), mn.reshape(shape), vn.reshape(shape)
        return d

    done = {k: apply_adamw(k) for k in names if k not in late_names}
    gots_l = exchange_chips_wait(send_sems, recv_sems, wires_l, lands_l, [done[k] for k in early_names],
                                 "rs_exchange_chips_ffn1_wait")
    grads.update(zip(late_names, rs_finish(place, owns_l, gots_l, token, "late")))
    for k in late_names:
        apply_adamw(k)
    for k in transposed:
        for group in (grads, deltas, new_m, new_v):
            group[k] = jnp.swapaxes(group[k], 1, 2)

    return (loss, dx0[None], *[grads[k] for k in names], *[deltas[k] for k in names],
            *[new_m[k] for k in names], *[new_v[k] for k in names])
```

```python
import functools

import jax
import jax.numpy as jnp
from jax import lax
from jax.experimental import pallas as pl
from jax.experimental.pallas import tpu as pltpu

F32 = jnp.float32
BF16 = jnp.bfloat16
MESH = pl.DeviceIdType.MESH

HEAD_DIM = 64
WINDOW = 128
CONV_WIDTH = 31
CONV_HALO = 32
ROPE_THETA = 10000.0
EPS = 1e-6
LN_EPS = 1e-5
NEG_INF = -1e30
N_CHIPS = 4
N_DEV = 8

ADAM_LR = 0.001
ADAM_B1 = 0.9
ADAM_B2 = 0.999
ADAM_EPS = 1e-08
ADAM_WD = 0.01
ADAM_STEP = 10

TM_FFN = 512
TM_FFN_FWD = 1024
TM_ROW = 256
TK_TN = 1024
TR_ELT = 256
VMEM_LIMIT = 56 * 1024 * 1024

NT_DIMS = (((1,), (1,)), ((), ()))
TN_DIMS = (((0,), (0,)), ((), ()))


def _row_tile(rows, cap):
    for t in range(min(cap, rows), 15, -1):
        if rows % t == 0 and t % 16 == 0:
            return t
    return rows


def _params(sem):
    return pltpu.CompilerParams(dimension_semantics=sem, vmem_limit_bytes=VMEM_LIMIT)


def _dot(a, b):
    return jnp.dot(a, b, preferred_element_type=F32)


def _dot_nt(a, b):
    return lax.dot_general(a, b, NT_DIMS, preferred_element_type=F32)


def _dot_tn(a, b):
    return lax.dot_general(a, b, TN_DIMS, preferred_element_type=F32)


def _split_rows(dot, a, b):
    m = a.shape[0] // 2
    return jnp.concatenate([dot(a[:m], b), dot(a[m:], b)], axis=0)


def _sigmoid(x):
    return jax.nn.sigmoid(x)


def _rms_scale(xv):
    return lax.rsqrt(jnp.mean(xv * xv, axis=-1, keepdims=True) + EPS)


def _rms_bwd(xv, nw, dh):
    r = _rms_scale(xv)
    dn = dh * nw
    dx = r * dn - xv * (r * r * r) * jnp.mean(dn * xv, axis=-1, keepdims=True)
    dnw = jnp.sum(dh * (xv * r), axis=0, keepdims=True)
    return dx, dnw


def _silu_grad(z, s):
    return s * (1.0 + z * (1.0 - s))


HBM_SPEC = pl.BlockSpec(memory_space=pl.ANY)


def _call_hosting(body, side, *, grid, in_specs, out_specs, out_shape, scratch_shapes, operands, name, aliases=None):
    params = _params(("arbitrary",) * len(grid))
    aliases = dict(aliases or {})
    if side is None:
        return pl.pallas_call(body, name=name, grid=grid, in_specs=in_specs, out_specs=out_specs, out_shape=out_shape,
                              scratch_shapes=scratch_shapes, input_output_aliases=aliases,
                              compiler_params=params)(*operands)
    n_in, n_out, n_scr = len(in_specs), len(out_shape), len(scratch_shapes)
    s_in, s_out = len(side["inputs"]), len(side["out_shapes"])

    def at_step(end):
        hit = pl.program_id(0) == (grid[0] - 1 if end else 0)
        for a in range(1, len(grid)):
            hit &= pl.program_id(a) == (grid[a] - 1 if end else 0)
        return hit

    def hosted(*refs):
        b = n_in + s_in
        c = b + n_out
        d = c + s_out
        e = d + n_scr
        src, dst, sems = refs[n_in:b], refs[c:d], refs[e:]

        @pl.when(at_step(False))
        def _():
            side["start"](src, dst, sems)

        body(*refs[:n_in], *refs[b:c], *refs[d:e])

        @pl.when(at_step(True))
        def _():
            side["finish"](src, dst, sems)

    return pl.pallas_call(
        hosted, name=name, grid=grid, in_specs=list(in_specs) + [HBM_SPEC] * s_in,
        out_specs=list(out_specs) + [HBM_SPEC] * s_out, out_shape=list(out_shape) + list(side["out_shapes"]),
        scratch_shapes=list(scratch_shapes) + list(side["sems"]),
        input_output_aliases={**aliases, **{n_in + a: n_out + b for a, b in side["aliases"].items()}},
        compiler_params=params)(*operands, *side["inputs"])


def ffn_fwd(x, nw, wg, wu, wd, name, side=None):
    T, D = x.shape
    NP, Fs, _ = wg.shape
    tm = min(TM_FFN_FWD, T)

    def body(x_ref, nw_ref, wg_ref, wu_ref, wd_ref, xo_ref, h_ref, g_ref, u_ref, acc_ref):
        j = pl.program_id(1)

        @pl.when(j == 0)
        def _():
            xv = x_ref[...]
            h_ref[...] = (xv * _rms_scale(xv) * nw_ref[...]).astype(BF16)
            acc_ref[...] = jnp.zeros_like(acc_ref)

        h = h_ref[...]
        g = _dot_nt(h, wg_ref[...])
        u = _dot_nt(h, wu_ref[...])
        a = (g * _sigmoid(g)) * u
        g_ref[...] = g.astype(BF16)
        u_ref[...] = u.astype(BF16)
        acc_ref[...] += _dot(a.astype(BF16), wd_ref[...])

        @pl.when(j == NP - 1)
        def _():
            xo_ref[...] = x_ref[...] + 0.5 * acc_ref[...]

    return _call_hosting(
        body, side, name=name, grid=(T // tm, NP),
        in_specs=[pl.BlockSpec((tm, D), lambda i, j: (i, 0)),
                  pl.BlockSpec((1, D), lambda i, j: (0, 0)),
                  pl.BlockSpec((None, Fs, D), lambda i, j: (j, 0, 0)),
                  pl.BlockSpec((None, Fs, D), lambda i, j: (j, 0, 0)),
                  pl.BlockSpec((None, Fs, D), lambda i, j: (j, 0, 0))],
        out_specs=[pl.BlockSpec((tm, D), lambda i, j: (i, 0)),
                   pl.BlockSpec((tm, D), lambda i, j: (i, 0)),
                   pl.BlockSpec((None, tm, Fs), lambda i, j: (j, i, 0)),
                   pl.BlockSpec((None, tm, Fs), lambda i, j: (j, i, 0))],
        out_shape=[jax.ShapeDtypeStruct((T, D), F32), jax.ShapeDtypeStruct((T, D), BF16),
                   jax.ShapeDtypeStruct((NP, T, Fs), BF16), jax.ShapeDtypeStruct((NP, T, Fs), BF16)],
        scratch_shapes=[pltpu.VMEM((tm, D), F32)],
        operands=(x, nw, wg, wu, wd))


def _ffn_bwd_piece(j, h, g, u, wg, wu, wd, dout, dh_in, dws_in, name, side, norm):
    T, D = h.shape
    NP, Fs, _ = wg.shape
    tm = min(TM_FFN, T)
    n_in = 7 + (dh_in is not None) + (2 if norm else 0) + (3 if dws_in else 0)

    def body(*refs):
        h_ref, g_ref, u_ref, wg_ref, wu_ref, wd_ref, do_ref = refs[:7]
        dhin_ref = refs[7] if dh_in is not None else None
        dh_ref, dwg_ref, dwu_ref, dwd_ref = refs[n_in:n_in + 4]

        @pl.when(pl.program_id(0) == 0)
        def _():
            dwg_ref[...] = jnp.zeros_like(dwg_ref)
            dwu_ref[...] = jnp.zeros_like(dwu_ref)
            dwd_ref[...] = jnp.zeros_like(dwd_ref)
            if norm:
                refs[n_in + 4][...] = jnp.zeros_like(refs[n_in + 4])

        dob = (0.5 * do_ref[...]).astype(BF16)
        da = _split_rows(_dot_nt, dob, wd_ref[...])
        gf = g_ref[...].astype(F32)
        uf = u_ref[...].astype(F32)
        s = _sigmoid(gf)
        act = gf * s
        dg = (da * uf * _silu_grad(gf, s)).astype(BF16)
        du = (da * act).astype(BF16)
        a = (act * uf).astype(BF16)
        dh = _dot(dg, wg_ref[...]) + _dot(du, wu_ref[...])
        dh = dh if dhin_ref is None else dhin_ref[...] + dh
        if norm:
            x_ref, nw_ref = refs[7 + (dh_in is not None):9 + (dh_in is not None)]
            dxn, dnw = _rms_bwd(x_ref[...], nw_ref[...], dh)
            dh_ref[...] = do_ref[...] + dxn
            refs[n_in + 4][...] += dnw
        else:
            dh_ref[...] = dh
        hb = h_ref[...]
        dwg_ref[...] += _dot_tn(dg, hb)
        dwu_ref[...] += _dot_tn(du, hb)
        dwd_ref[...] += _dot_tn(a, dob)

    rows = pl.BlockSpec((tm, D), lambda i: (i, 0))
    piece = pl.BlockSpec((None, tm, Fs), lambda i: (j, i, 0))
    slot = pl.BlockSpec((None, Fs, D), lambda i: (j, 0, 0), pipeline_mode=pl.Buffered(1))
    in_specs = [rows, piece, piece, slot, slot, slot, rows]
    operands = [h, g, u, wg, wu, wd, dout]
    aliases = {}
    if dh_in is not None:
        in_specs.append(rows)
        operands.append(dh_in)
    if norm:
        in_specs += [rows, pl.BlockSpec((1, D), lambda i: (0, 0))]
        operands += list(norm)
    if dws_in:
        aliases = {len(operands) + k: 1 + k for k in range(3)}
        in_specs += [HBM_SPEC] * 3
        operands += list(dws_in)
    out_specs = [rows, slot, slot, slot]
    out_shape = [jax.ShapeDtypeStruct((T, D), F32)] + [jax.ShapeDtypeStruct((NP, Fs, D), F32)] * 3
    if norm:
        out_specs.append(pl.BlockSpec((1, D), lambda i: (0, 0)))
        out_shape.append(jax.ShapeDtypeStruct((1, D), F32))
    return _call_hosting(body, side, name=name, grid=(T // tm,), in_specs=in_specs, out_specs=out_specs,
                         out_shape=out_shape, scratch_shapes=[], aliases=aliases, operands=tuple(operands))


def ffn_bwd(x, nw, h, g, u, wg, wu, wd, dout, name, side=None):
    NP = wg.shape[0]
    dh, dws, extra = None, None, []
    for j in range(NP):
        dh, *rest = _ffn_bwd_piece(j, h, g, u, wg, wu, wd, dout, dh, dws, f"{name}_{j}",
                                   side if j == 0 else None, (x, nw) if j == NP - 1 else None)
        dws, rest = rest[:3], rest[3:]
        if j == 0:
            extra = rest[1:] if NP == 1 else rest
    return (dh, *dws, rest[0], *extra)


def mix_in_fwd(x, nw, w_glu, w_qkv, w_gate, cs, sn, name):
    T, D = x.shape
    KV = (w_qkv.shape[1] - D) // 2
    tm = min(TM_ROW, T)

    def body(x_ref, nw_ref, wa_ref, wq_ref, wg_ref, cs_ref, sn_ref, h_ref, pa_ref, pg_ref, q_ref, k_ref, v_ref):
        xv = x_ref[...]
        h = (xv * _rms_scale(xv) * nw_ref[...]).astype(BF16)
        h_ref[...] = h
        pa_ref[...] = _dot(h, wa_ref[...])
        pg_ref[...] = _dot(h, wg_ref[...])
        qkv = _dot(h, wq_ref[...])
        cs_v, sn_v = cs_ref[...], sn_ref[...]
        q_ref[...] = _rope_chunks(qkv[:, :D], cs_v, sn_v, 1.0).astype(BF16)
        k_ref[...] = _rope_chunks(qkv[:, D:D + KV], cs_v, sn_v, 1.0).astype(BF16)
        v_ref[...] = qkv[:, D + KV:].astype(BF16)

    rows = lambda w: pl.BlockSpec((tm, w), lambda i: (i, 0))
    whole = lambda a: pl.BlockSpec(a.shape, lambda i: (0, 0))
    return pl.pallas_call(
        body, name=name, grid=(T // tm,),
        in_specs=[rows(D), whole(nw), whole(w_glu), whole(w_qkv), whole(w_gate), rows(128), rows(128)],
        out_specs=[rows(D), rows(2 * D), rows(2 * D), rows(D), rows(KV), rows(KV)],
        out_shape=[jax.ShapeDtypeStruct((T, D), BF16), jax.ShapeDtypeStruct((T, 2 * D), F32),
                   jax.ShapeDtypeStruct((T, 2 * D), F32), jax.ShapeDtypeStruct((T, D), BF16),
                   jax.ShapeDtypeStruct((T, KV), BF16), jax.ShapeDtypeStruct((T, KV), BF16)],
        compiler_params=_params(("parallel",)),
    )(x, nw, w_glu, w_qkv, w_gate, cs, sn)


def matmul_tn(lhs, rhs, name):
    T, K = lhs.shape
    N = rhs.shape[1]
    tk = min(TK_TN, T)

    def body(l_ref, r_ref, o_ref):
        @pl.when(pl.program_id(0) == 0)
        def _():
            o_ref[...] = jnp.zeros_like(o_ref)

        o_ref[...] += _dot_tn(l_ref[...].astype(BF16), r_ref[...].astype(BF16))

    return pl.pallas_call(
        body, name=name, grid=(T // tk,),
        in_specs=[pl.BlockSpec((tk, K), lambda t: (t, 0)), pl.BlockSpec((tk, N), lambda t: (t, 0))],
        out_specs=pl.BlockSpec((K, N), lambda t: (0, 0)),
        out_shape=jax.ShapeDtypeStruct((K, N), F32),
        compiler_params=_params(("arbitrary",)),
    )(lhs, rhs)


def mix_in_bwd(dps, ws, x, nw, dres, name):
    T, D = x.shape
    tm = min(TM_ROW, T)
    n = len(dps)

    def body(*refs):
        dp_refs, w_refs = refs[:n], refs[n:2 * n]
        x_ref, nw_ref, dr_ref, dx_ref, dnw_ref = refs[2 * n:]

        @pl.when(pl.program_id(0) == 0)
        def _():
            dnw_ref[...] = jnp.zeros_like(dnw_ref)

        dh = _dot_nt(dp_refs[0][...], w_refs[0][...])
        for k in range(1, n):
            dh += _dot_nt(dp_refs[k][...], w_refs[k][...])
        dxn, dnw = _rms_bwd(x_ref[...], nw_ref[...], dh)
        dx_ref[...] = dr_ref[...] + dxn
        dnw_ref[...] += dnw

    in_specs = [pl.BlockSpec((tm, dp.shape[1]), lambda i: (i, 0)) for dp in dps]
    in_specs += [pl.BlockSpec(w.shape, lambda i: (0, 0)) for w in ws]
    in_specs += [pl.BlockSpec((tm, D), lambda i: (i, 0)), pl.BlockSpec((1, D), lambda i: (0, 0)),
                 pl.BlockSpec((tm, D), lambda i: (i, 0))]
    return pl.pallas_call(
        body, name=name, grid=(T // tm,), in_specs=in_specs,
        out_specs=[pl.BlockSpec((tm, D), lambda i: (i, 0)), pl.BlockSpec((1, D), lambda i: (0, 0))],
        out_shape=[jax.ShapeDtypeStruct((T, D), F32), jax.ShapeDtypeStruct((1, D), F32)],
        compiler_params=_params(("arbitrary",)),
    )(*dps, *ws, x, nw, dres)


def _layernorm_stats(c1):
    mu = jnp.mean(c1, axis=-1, keepdims=True)
    xc = c1 - mu
    rstd = lax.rsqrt(jnp.mean(xc * xc, axis=-1, keepdims=True) + LN_EPS)
    return xc * rstd, rstd


def _shifted_copies(src_ref, dst_ref):
    rows = dst_ref.shape[1]
    for b in range(1, 8):
        dst_ref[b - 1] = src_ref[pl.ds(b, rows), :]


def _shifted_rows(src_ref, shifted_ref, start, rows, cols):
    a8, b = divmod(start, 8)
    if b == 0:
        return src_ref[pl.ds(8 * a8, rows), cols]
    return shifted_ref[b - 1, pl.ds(8 * a8, rows), cols]


def conv_fwd(p_glu, dw_w, dw_b, ln_g, ln_b, name):
    T, D2 = p_glu.shape
    D = D2 // 2
    tm = min(TM_ROW, T)
    hb = tm // CONV_HALO

    def body(a_ref, b_ref, ah_ref, bh_ref, w_ref, wb_ref, g_ref, be_ref, c1_ref, c3_ref, e_ref, es_ref):
        i = pl.program_id(0)
        halo = ah_ref[...] * _sigmoid(bh_ref[...])
        e_ref[pl.ds(0, CONV_HALO), :] = jnp.where(i > 0, halo, 0.0)
        e_ref[pl.ds(CONV_HALO, tm), :] = a_ref[...] * _sigmoid(b_ref[...])
        _shifted_copies(e_ref, es_ref)
        off = CONV_HALO - (CONV_WIDTH - 1)

        def strip(s, carry):
            cols = pl.ds(pl.multiple_of(s * 128, 128), 128)
            acc = jnp.zeros((tm, 128), F32) + wb_ref[:, cols]
            for k in range(CONV_WIDTH):
                acc += w_ref[pl.ds(k, 1), cols] * _shifted_rows(e_ref, es_ref, off + k, tm, cols)
            c1_ref[:, cols] = acc
            return carry

        lax.fori_loop(0, D // 128, strip, 0)
        xhat, _ = _layernorm_stats(c1_ref[...])
        c2 = xhat * g_ref[...] + be_ref[...]
        c3_ref[...] = (c2 * _sigmoid(c2)).astype(BF16)

    row = pl.BlockSpec((1, D), lambda i: (0, 0))
    return pl.pallas_call(
        body, name=name, grid=(T // tm,),
        in_specs=[pl.BlockSpec((tm, D), lambda i: (i, 0)), pl.BlockSpec((tm, D), lambda i: (i, 1)),
                  pl.BlockSpec((CONV_HALO, D), lambda i: (jnp.maximum(i * hb - 1, 0), 0)),
                  pl.BlockSpec((CONV_HALO, D), lambda i: (jnp.maximum(i * hb - 1, 0), 1)),
                  pl.BlockSpec((CONV_HALO, D), lambda i: (0, 0)), row, row, row],
        out_specs=[pl.BlockSpec((tm, D), lambda i: (i, 0)), pl.BlockSpec((tm, D), lambda i: (i, 0))],
        out_shape=[jax.ShapeDtypeStruct((T, D), F32), jax.ShapeDtypeStruct((T, D), BF16)],
        scratch_shapes=[pltpu.VMEM((tm + CONV_HALO, D), F32), pltpu.VMEM((7, tm + CONV_HALO - 8, D), F32)],
        compiler_params=_params(("parallel",)),
    )(p_glu, p_glu, p_glu, p_glu, dw_w, dw_b, ln_g, ln_b)


def conv_bwd(p_glu, dc1, dw_w, name, side=None):
    T, D2 = p_glu.shape
    D = D2 // 2
    tm = min(TM_ROW, T)
    hb = tm // CONV_HALO
    last = T // CONV_HALO - 1
    nblk = T // tm

    def body(a_ref, b_ref, ah_ref, bh_ref, d_ref, dn_ref, w_ref, dp_ref, dw_ref, e_ref, f_ref, es_ref, fs_ref):
        i = pl.program_id(0)

        @pl.when(i == 0)
        def _():
            dw_ref[...] = jnp.zeros_like(dw_ref)

        halo = ah_ref[...] * _sigmoid(bh_ref[...])
        e_ref[pl.ds(0, CONV_HALO), :] = jnp.where(i > 0, halo, 0.0)
        e_ref[pl.ds(CONV_HALO, tm), :] = a_ref[...] * _sigmoid(b_ref[...])
        f_ref[pl.ds(0, tm), :] = d_ref[...]
        f_ref[pl.ds(tm, CONV_HALO), :] = jnp.where(i < nblk - 1, dn_ref[...], 0.0)
        _shifted_copies(e_ref, es_ref)
        _shifted_copies(f_ref, fs_ref)
        off = CONV_HALO - (CONV_WIDTH - 1)

        def strip(s, carry):
            cols = pl.ds(pl.multiple_of(s * 128, 128), 128)
            d = d_ref[:, cols]
            dc0 = jnp.zeros((tm, 128), F32)
            for k in range(CONV_WIDTH):
                dw_ref[pl.ds(k, 1), cols] += jnp.sum(d * _shifted_rows(e_ref, es_ref, off + k, tm, cols),
                                                     axis=0, keepdims=True)
                dc0 += w_ref[pl.ds(k, 1), cols] * _shifted_rows(f_ref, fs_ref, CONV_WIDTH - 1 - k, tm, cols)
            a = a_ref[:, cols]
            sb = _sigmoid(b_ref[:, cols])
            dp_ref[:, cols] = (dc0 * sb).astype(BF16)
            dp_ref[:, pl.ds(pl.multiple_of(D + s * 128, 128), 128)] = (dc0 * a * sb * (1.0 - sb)).astype(BF16)
            return carry

        lax.fori_loop(0, D // 128, strip, 0)

    return _call_hosting(
        body, side, name=name, grid=(nblk,),
        in_specs=[pl.BlockSpec((tm, D), lambda i: (i, 0)), pl.BlockSpec((tm, D), lambda i: (i, 1)),
                  pl.BlockSpec((CONV_HALO, D), lambda i: (jnp.maximum(i * hb - 1, 0), 0)),
                  pl.BlockSpec((CONV_HALO, D), lambda i: (jnp.maximum(i * hb - 1, 0), 1)),
                  pl.BlockSpec((tm, D), lambda i: (i, 0)),
                  pl.BlockSpec((CONV_HALO, D), lambda i: (jnp.minimum((i + 1) * hb, last), 0)),
                  pl.BlockSpec((CONV_HALO, D), lambda i: (0, 0))],
        out_specs=[pl.BlockSpec((tm, D2), lambda i: (i, 0)), pl.BlockSpec((CONV_HALO, D), lambda i: (0, 0))],
        out_shape=[jax.ShapeDtypeStruct((T, D2), BF16), jax.ShapeDtypeStruct((CONV_HALO, D), F32)],
        scratch_shapes=[pltpu.VMEM((tm + CONV_HALO, D), F32), pltpu.VMEM((tm + CONV_HALO, D), F32),
                        pltpu.VMEM((7, tm + CONV_HALO - 8, D), F32), pltpu.VMEM((7, tm + CONV_HALO - 8, D), F32)],
        operands=(p_glu, p_glu, p_glu, p_glu, dc1, dc1, dw_w))


def _rot_half(x):
    lane = lax.broadcasted_iota(jnp.int32, x.shape, 1)
    first = (lane % HEAD_DIM) < HEAD_DIM // 2
    return jnp.where(first, pltpu.roll(x, 128 - HEAD_DIM // 2, 1), pltpu.roll(x, HEAD_DIM // 2, 1))


def _rope_chunks(x, cs, sn, sign):
    outs = []
    for c in range(x.shape[1] // 128):
        xc = x[:, c * 128:(c + 1) * 128]
        outs.append(xc * cs + sign * (_rot_half(xc) * sn))
    return outs[0] if len(outs) == 1 else jnp.concatenate(outs, axis=1)


def rope_bwd(dq, dk, dv, cs, sn, name):
    T, D = dq.shape
    KV = dk.shape[1]
    tm = min(TM_ROW, T)

    def body(dq_ref, dk_ref, dv_ref, cs_ref, sn_ref, o_ref):
        cs_v, sn_v = cs_ref[...], sn_ref[...]
        o_ref[:, pl.ds(0, D)] = _rope_chunks(dq_ref[...], cs_v, sn_v, -1.0).astype(BF16)
        o_ref[:, pl.ds(D, KV)] = _rope_chunks(dk_ref[...], cs_v, sn_v, -1.0).astype(BF16)
        o_ref[:, pl.ds(D + KV, KV)] = dv_ref[...].astype(BF16)

    tab = pl.BlockSpec((tm, 128), lambda i: (i, 0))
    return pl.pallas_call(
        body, name=name, grid=(T // tm,),
        in_specs=[pl.BlockSpec((tm, D), lambda i: (i, 0)), pl.BlockSpec((tm, KV), lambda i: (i, 0)),
                  pl.BlockSpec((tm, KV), lambda i: (i, 0)), tab, tab],
        out_specs=pl.BlockSpec((tm, D + 2 * KV), lambda i: (i, 0)),
        out_shape=jax.ShapeDtypeStruct((T, D + 2 * KV), BF16),
        compiler_params=_params(("parallel",)),
    )(dq, dk, dv, cs, sn)


def _lane_lo():
    return lax.broadcasted_iota(jnp.int32, (1, 128), 1) < HEAD_DIM


def _band_mask(i, reps):
    shape = (reps * WINDOW, 2 * WINDOW)
    qi = lax.broadcasted_iota(jnp.int32, shape, 0) % WINDOW
    cj = lax.broadcasted_iota(jnp.int32, shape, 1)
    rel = qi - cj + WINDOW
    return (rel >= 0) & (rel < WINDOW) & ((i > 0) | (cj >= WINDOW))


def _stack_pairs(ref, first, n):
    parts = [ref[:, pl.ds((first + p) * 128, 128)] for p in range(n)]
    return parts[0] if n == 1 else jnp.concatenate(parts, axis=0)


def _pair_rows(n):
    return lax.broadcasted_iota(jnp.int32, (n * WINDOW, 1), 0) // WINDOW


def _per_pair_column(values, n):
    rows = _pair_rows(n)
    col = jnp.zeros((n * WINDOW, 1), F32) + values[0]
    for p in range(1, n):
        col = jnp.where(rows == p, values[p], col)
    return col


def _kv_lo_hi(x2, g):
    pair, half = divmod(g, 2)
    lo = _lane_lo()
    xg = x2[:, pair * 128:(pair + 1) * 128].astype(F32)
    xg = jnp.where(lo if half == 0 else ~lo, xg, 0.0)
    sw = pltpu.roll(xg, HEAD_DIM, 1)
    x_lo, x_hi = (xg, sw) if half == 0 else (sw, xg)
    return x_lo.astype(BF16), x_hi.astype(BF16)


def _softmax_sink(s, allowed, sink):
    s = jnp.where(allowed, s * (HEAD_DIM ** -0.5), NEG_INF)
    m = jnp.maximum(jnp.max(s, axis=-1, keepdims=True), sink)
    p = jnp.exp(s - m)
    es = jnp.exp(sink - m)
    inv = 1.0 / (jnp.sum(p, axis=-1, keepdims=True) + es)
    return p * inv, es * inv


def attn_fwd(qr, kr, vb, sinks, name, side=None):
    T, D = qr.shape
    KV = kr.shape[1]
    n_kv = KV // HEAD_DIM
    group = (D // HEAD_DIM) // n_kv
    nb = T // WINDOW

    npair = group // 2

    def body(sink_ref, q_ref, kp_ref, kc_ref, vp_ref, vc_ref, o_ref):
        i = pl.program_id(0)
        allowed = _band_mask(i, npair)
        k2 = jnp.concatenate([kp_ref[...], kc_ref[...]], axis=0)
        v2 = jnp.concatenate([vp_ref[...], vc_ref[...]], axis=0)
        outs = [None] * (D // 128)
        for g in range(n_kv):
            k_lo, k_hi = _kv_lo_hi(k2, g)
            v_lo, v_hi = _kv_lo_hi(v2, g)
            first = (g * group) // 2
            q = _stack_pairs(q_ref, first, npair)
            sink_e = _per_pair_column([sink_ref[0, g * group + 2 * p] for p in range(npair)], npair)
            sink_o = _per_pair_column([sink_ref[0, g * group + 2 * p + 1] for p in range(npair)], npair)
            pe, _ = _softmax_sink(_dot_nt(q, k_lo), allowed, sink_e)
            po, _ = _softmax_sink(_dot_nt(q, k_hi), allowed, sink_o)
            o = _dot(pe.astype(BF16), v_lo) + _dot(po.astype(BF16), v_hi)
            for p in range(npair):
                outs[first + p] = o[p * WINDOW:(p + 1) * WINDOW]
        o_ref[...] = jnp.concatenate(outs, axis=1).astype(BF16)

    prev = lambda i: (jnp.maximum(i - 1, 0), 0)
    cur = lambda i: (i, 0)
    return _call_hosting(
        body, side, name=name, grid=(nb,),
        in_specs=[pl.BlockSpec(memory_space=pltpu.SMEM),
                  pl.BlockSpec((WINDOW, D), cur),
                  pl.BlockSpec((WINDOW, KV), prev), pl.BlockSpec((WINDOW, KV), cur),
                  pl.BlockSpec((WINDOW, KV), prev), pl.BlockSpec((WINDOW, KV), cur)],
        out_specs=[pl.BlockSpec((WINDOW, D), cur)],
        out_shape=[jax.ShapeDtypeStruct((T, D), BF16)],
        scratch_shapes=[], operands=(sinks, qr, kr, kr, vb, vb))


def attn_bwd(qr, kr, vb, o, do, sinks, name):
    T, D = qr.shape
    KV = kr.shape[1]
    n_heads = D // HEAD_DIM
    n_kv = KV // HEAD_DIM
    group = n_heads // n_kv
    nb = T // WINDOW
    npair = group // 2
    scale = HEAD_DIM ** -0.5

    def body(sink_ref, q_ref, kp_ref, kc_ref, vp_ref, vc_ref, o_ref, do_ref,
             dq_ref, dk_ref, dv_ref, ds_ref, ck_ref, cv_ref):
        i = pl.program_id(0)
        lo = _lane_lo()

        @pl.when(i == 0)
        def _():
            ck_ref[...] = jnp.zeros_like(ck_ref)
            cv_ref[...] = jnp.zeros_like(cv_ref)
            ds_ref[...] = jnp.zeros_like(ds_ref)

        @pl.when(i < nb)
        def _():
            allowed = _band_mask(i, npair)
            rows = _pair_rows(npair)
            k2 = jnp.concatenate([kp_ref[...], kc_ref[...]], axis=0)
            v2 = jnp.concatenate([vp_ref[...], vc_ref[...]], axis=0)
            lane = lax.broadcasted_iota(jnp.int32, (1, 128), 1)
            dsink = jnp.zeros((1, 128), F32)
            dq_out = [None] * (D // 128)
            dk_pairs = [jnp.zeros((2 * WINDOW, 128), F32) for _ in range(KV // 128)]
            dv_pairs = [jnp.zeros((2 * WINDOW, 128), F32) for _ in range(KV // 128)]
            for g in range(n_kv):
                k_lo, k_hi = _kv_lo_hi(k2, g)
                v_lo, v_hi = _kv_lo_hi(v2, g)
                first = (g * group) // 2
                q = _stack_pairs(q_ref, first, npair)
                dop = _stack_pairs(do_ref, first, npair)
                dd = dop.astype(F32) * _stack_pairs(o_ref, first, npair).astype(F32)
                dq = jnp.zeros((npair * WINDOW, 128), F32)
                dkg = jnp.zeros((2 * WINDOW, 128), F32)
                dvg = jnp.zeros((2 * WINDOW, 128), F32)
                for parity, k_h, v_h, sel in ((0, k_lo, v_lo, lo), (1, k_hi, v_hi, ~lo)):
                    heads = [g * group + 2 * p + parity for p in range(npair)]
                    sink = _per_pair_column([sink_ref[0, h] for h in heads], npair)
                    p_, ps = _softmax_sink(_dot_nt(q, k_h), allowed, sink)
                    delta = jnp.sum(jnp.where(sel, dd, 0.0), axis=-1, keepdims=True)
                    dsc = (p_ * (_dot_nt(dop, v_h) - delta)).astype(BF16)
                    sd = -ps * delta
                    for p, h in enumerate(heads):
                        dsink += jnp.where(lane == h, jnp.sum(jnp.where(rows == p, sd, 0.0)), 0.0)
                    dq += _dot(dsc, k_h)
                    dkg += jnp.where(sel, _dot_tn(dsc, q), 0.0)
                    dvg += jnp.where(sel, _dot_tn(p_.astype(BF16), dop), 0.0)
                for p in range(npair):
                    dq_out[first + p] = dq[p * WINDOW:(p + 1) * WINDOW]
                pair, half = divmod(g, 2)
                keep = lo if half == 0 else ~lo
                dk_pairs[pair] += jnp.where(keep, dkg + pltpu.roll(dkg, HEAD_DIM, 1), 0.0) * scale
                dv_pairs[pair] += jnp.where(keep, dvg + pltpu.roll(dvg, HEAD_DIM, 1), 0.0)
            dq_ref[...] = jnp.concatenate(dq_out, axis=1) * scale
            dk2 = dk_pairs[0] if len(dk_pairs) == 1 else jnp.concatenate(dk_pairs, axis=1)
            dv2 = dv_pairs[0] if len(dv_pairs) == 1 else jnp.concatenate(dv_pairs, axis=1)
            dk_ref[...] = ck_ref[...] + dk2[:WINDOW]
            dv_ref[...] = cv_ref[...] + dv2[:WINDOW]
            ck_ref[...] = dk2[WINDOW:]
            cv_ref[...] = dv2[WINDOW:]
            ds_ref[pl.ds(0, 1), :] += dsink

        @pl.when(i == nb)
        def _():
            dk_ref[...] = ck_ref[...]
            dv_ref[...] = cv_ref[...]

    prev = lambda i: (jnp.maximum(i - 1, 0), 0)
    cur = lambda i: (jnp.minimum(i, nb - 1), 0)
    prevc = lambda i: (jnp.maximum(jnp.minimum(i, nb - 1) - 1, 0), 0)
    return pl.pallas_call(
        body, name=name, grid=(nb + 1,),
        in_specs=[pl.BlockSpec(memory_space=pltpu.SMEM),
                  pl.BlockSpec((WINDOW, D), cur),
                  pl.BlockSpec((WINDOW, KV), prevc), pl.BlockSpec((WINDOW, KV), cur),
                  pl.BlockSpec((WINDOW, KV), prevc), pl.BlockSpec((WINDOW, KV), cur),
                  pl.BlockSpec((WINDOW, D), cur), pl.BlockSpec((WINDOW, D), cur)],
        out_specs=[pl.BlockSpec((WINDOW, D), cur), pl.BlockSpec((WINDOW, KV), prev),
                   pl.BlockSpec((WINDOW, KV), prev), pl.BlockSpec((8, 128), lambda i: (0, 0))],
        out_shape=[jax.ShapeDtypeStruct((T, D), F32), jax.ShapeDtypeStruct((T, KV), F32),
                   jax.ShapeDtypeStruct((T, KV), F32), jax.ShapeDtypeStruct((8, 128), F32)],
        scratch_shapes=[pltpu.VMEM((WINDOW, KV), F32), pltpu.VMEM((WINDOW, KV), F32)],
        compiler_params=_params(("arbitrary",)),
    )(sinks, qr, kr, kr, vb, vb, o, do)


def merge_fwd(x, c3, o, p_gate, gate_b, w_proj, w_o, w_out, name):
    T, D = x.shape
    tm = min(TM_ROW, T)

    def body(x_ref, c3_ref, o_ref, gc_ref, ga_ref, bc_ref, ba_ref, wp_ref, wo_ref, wout_ref,
             xo_ref, co_ref, ao_ref, mg_ref):
        conv_out = _dot(c3_ref[...], wp_ref[...])
        attn_out = _dot(o_ref[...], wo_ref[...])
        merged = (_sigmoid(gc_ref[...] + bc_ref[...]) * conv_out
                  + _sigmoid(ga_ref[...] + ba_ref[...]) * attn_out).astype(BF16)
        co_ref[...] = conv_out.astype(BF16)
        ao_ref[...] = attn_out.astype(BF16)
        mg_ref[...] = merged
        xo_ref[...] = x_ref[...] + _dot(merged, wout_ref[...])

    blk = lambda j: pl.BlockSpec((tm, D), lambda i: (i, j))
    row = lambda j: pl.BlockSpec((1, D), lambda i: (0, j))
    mat = pl.BlockSpec((D, D), lambda i: (0, 0))
    return pl.pallas_call(
        body, name=name, grid=(T // tm,),
        in_specs=[blk(0), blk(0), blk(0), blk(0), blk(1), row(0), row(1), mat, mat, mat],
        out_specs=[blk(0), blk(0), blk(0), blk(0)],
        out_shape=[jax.ShapeDtypeStruct((T, D), F32)] + [jax.ShapeDtypeStruct((T, D), BF16)] * 3,
        compiler_params=_params(("parallel",)),
    )(x, c3, o, p_gate, p_gate, gate_b, gate_b, w_proj, w_o, w_out)


def merge_bwd(dx, p_gate, gate_b, conv_out, attn_out, c1, ln_g, ln_b, w_proj, w_o, w_out, name, side=None):
    T, D = dx.shape
    tm = min(TM_ROW, T)

    def body(dx_ref, gc_ref, ga_ref, bc_ref, ba_ref, co_ref, ao_ref, c1_ref, g_ref, be_ref,
             wp_ref, wo_ref, wout_ref, dgt_ref, dco_ref, dao_ref, do_ref, dc1_ref, sm_ref):
        @pl.when(pl.program_id(0) == 0)
        def _():
            sm_ref[...] = jnp.zeros_like(sm_ref)

        dm = _dot_nt(dx_ref[...].astype(BF16), wout_ref[...])
        sc = _sigmoid(gc_ref[...] + bc_ref[...])
        sa = _sigmoid(ga_ref[...] + ba_ref[...])
        dco = (dm * sc).astype(BF16)
        dao = (dm * sa).astype(BF16)
        dgc = dm * co_ref[...].astype(F32) * sc * (1.0 - sc)
        dga = dm * ao_ref[...].astype(F32) * sa * (1.0 - sa)
        dgt_ref[:, pl.ds(0, D)] = dgc.astype(BF16)
        dgt_ref[:, pl.ds(D, D)] = dga.astype(BF16)
        dco_ref[...] = dco
        dao_ref[...] = dao
        do_ref[...] = _dot_nt(dao, wo_ref[...]).astype(BF16)
        dc3 = _dot_nt(dco, wp_ref[...])
        xhat, rstd = _layernorm_stats(c1_ref[...])
        c2 = xhat * g_ref[...] + be_ref[...]
        dc2 = dc3 * _silu_grad(c2, _sigmoid(c2))
        dxh = dc2 * g_ref[...]
        dc1 = rstd * (dxh - jnp.mean(dxh, axis=-1, keepdims=True)
                      - xhat * jnp.mean(dxh * xhat, axis=-1, keepdims=True))
        dc1_ref[...] = dc1
        colsum = lambda v: jnp.sum(v, axis=0, keepdims=True)
        for r, (left, right) in enumerate(((dgc, dga), (dc2 * xhat, dc2), (dc1, None))):
            sm_ref[pl.ds(r, 1), pl.ds(0, D)] += colsum(left)
            if right is not None:
                sm_ref[pl.ds(r, 1), pl.ds(D, D)] += colsum(right)

    blk = lambda j: pl.BlockSpec((tm, D), lambda i: (i, j))
    row = lambda j: pl.BlockSpec((1, D), lambda i: (0, j))
    mat = pl.BlockSpec((D, D), lambda i: (0, 0))
    return _call_hosting(
        body, side, name=name, grid=(T // tm,),
        in_specs=[blk(0), blk(0), blk(1), row(0), row(1), blk(0), blk(0), blk(0), row(0), row(0), mat, mat, mat],
        out_specs=[pl.BlockSpec((tm, 2 * D), lambda i: (i, 0)), blk(0), blk(0), blk(0), blk(0),
                   pl.BlockSpec((8, 2 * D), lambda i: (0, 0))],
        out_shape=[jax.ShapeDtypeStruct((T, 2 * D), BF16)] + [jax.ShapeDtypeStruct((T, D), BF16)] * 3
                  + [jax.ShapeDtypeStruct((T, D), F32), jax.ShapeDtypeStruct((8, 2 * D), F32)],
        scratch_shapes=[],
        operands=(dx, p_gate, p_gate, gate_b, gate_b, conv_out, attn_out, c1, ln_g, ln_b, w_proj, w_o, w_out))


def loss_head(x, nw, target, name):
    T, D = x.shape
    tm = min(TM_ROW, T)

    def body(x_ref, nw_ref, t_ref, dx_ref, sm_ref):
        @pl.when(pl.program_id(0) == 0)
        def _():
            sm_ref[...] = jnp.zeros_like(sm_ref)

        xv = x_ref[...]
        err = xv * _rms_scale(xv) * nw_ref[...] - t_ref[...]
        loss = 0.5 * jnp.sum(jnp.mean(err * err, axis=-1, keepdims=True))
        dxn, dnw = _rms_bwd(xv, nw_ref[...], err * (1.0 / D))
        dx_ref[...] = dxn
        sm_ref[pl.ds(0, 1), :] += dnw
        sm_ref[pl.ds(1, 1), :] += jnp.zeros((1, D), F32) + loss

    return pl.pallas_call(
        body, name=name, grid=(T // tm,),
        in_specs=[pl.BlockSpec((tm, D), lambda i: (i, 0)), pl.BlockSpec((1, D), lambda i: (0, 0)),
                  pl.BlockSpec((tm, D), lambda i: (i, 0))],
        out_specs=[pl.BlockSpec((tm, D), lambda i: (i, 0)), pl.BlockSpec((8, D), lambda i: (0, 0))],
        out_shape=[jax.ShapeDtypeStruct((T, D), F32), jax.ShapeDtypeStruct((8, D), F32)],
        compiler_params=_params(("arbitrary",)),
    )(x, nw, target)


def adamw(w, g, m, v, name):
    R, C = w.shape
    tr = _row_tile(R, TR_ELT)

    def body(w_ref, g_ref, m_ref, v_ref, d_ref, mo_ref, vo_ref):
        gv = g_ref[...]
        mn = ADAM_B1 * m_ref[...] + (1.0 - ADAM_B1) * gv
        vn = ADAM_B2 * v_ref[...] + (1.0 - ADAM_B2) * (gv * gv)
        m_hat = mn / (1.0 - ADAM_B1 ** ADAM_STEP)
        v_hat = vn / (1.0 - ADAM_B2 ** ADAM_STEP)
        d_ref[...] = -ADAM_LR * (m_hat / (jnp.sqrt(v_hat) + ADAM_EPS) + ADAM_WD * w_ref[...])
        mo_ref[...] = mn
        vo_ref[...] = vn

    spec = pl.BlockSpec((tr, C), lambda i: (i, 0))
    return pl.pallas_call(
        body, name=name, grid=(R // tr,), in_specs=[spec] * 4, out_specs=[spec] * 3,
        out_shape=[jax.ShapeDtypeStruct((R, C), F32)] * 3,
        compiler_params=_params(("parallel",)),
    )(w, g, m, v)


def _place():
    return lax.axis_index("x"), lax.axis_index("y"), lax.axis_index("c")


def place_shard(place, w, dtype, name):
    R, C = w.shape
    tr = _row_tile(R, TR_ELT)

    def body(pc_ref, w_ref, o_ref):
        o_ref[...] = w_ref[...].astype(dtype)

    return pl.pallas_call(
        body, name=name,
        grid_spec=pltpu.PrefetchScalarGridSpec(
            num_scalar_prefetch=1, grid=(R // tr,),
            in_specs=[pl.BlockSpec((tr, C), lambda r, pc: (r, 0))],
            out_specs=pl.BlockSpec((None, tr, C), lambda r, pc: (pc[0], r, 0))),
        out_shape=jax.ShapeDtypeStruct((N_CHIPS, R, C), dtype),
        compiler_params=_params(("arbitrary",)),
    )(place, w)


def gather_side(shards, small):
    n, ns = len(shards), len(small)

    def ici_copy(dst, sems, k, j, x, y, c, sending):
        px, py = x ^ (j >> 1), y ^ (j & 1)
        slot = 2 * x + y if sending else 2 * px + py
        half = dst[k].shape[1] // 2
        part = dst[k].at[slot, pl.ds(c * half, half)] if k < n else dst[k].at[slot]
        return pltpu.make_async_remote_copy(part, part, sems[0].at[3 * k + j - 1], sems[1].at[3 * k + j - 1],
                                            device_id=(px, py, c), device_id_type=MESH)

    def d2d_copy(dst, sems, k, j, x, y, c, sending):
        half = dst[k].shape[1] // 2
        part = dst[k].at[2 * (x ^ (j >> 1)) + (y ^ (j & 1)), pl.ds((c if sending else 1 - c) * half, half)]
        return pltpu.make_async_remote_copy(part, part, sems[2].at[3 * k + j - 1], sems[3].at[3 * k + j - 1],
                                            device_id=(x, y, 1 - c), device_id_type=MESH)

    def start(src, dst, sems):
        x, y, c = _place()
        for k in range(n + ns):
            for j in (1, 2, 3):
                ici_copy(dst, sems, k, j, x, y, c, True).start()

    def finish(src, dst, sems):
        x, y, c = _place()
        for k in range(n + ns):
            for j in (1, 2, 3):
                ici_copy(dst, sems, k, j, x, y, c, False).wait_recv()
                if k < n:
                    d2d_copy(dst, sems, k, j, x, y, c, True).start()
        for k in range(n):
            for j in (1, 2, 3):
                d2d_copy(dst, sems, k, j, x, y, c, False).wait_recv()
        for k in range(n + ns):
            for j in (1, 2, 3):
                ici_copy(dst, sems, k, j, x, y, c, True).wait_send()
                if k < n:
                    d2d_copy(dst, sems, k, j, x, y, c, True).wait_send()

    arrays = list(shards) + list(small)
    return dict(inputs=arrays, out_shapes=[jax.ShapeDtypeStruct(a.shape, a.dtype) for a in arrays],
                aliases={k: k for k in range(n + ns)},
                sems=[pltpu.SemaphoreType.DMA((3 * (n + ns),)), pltpu.SemaphoreType.DMA((3 * (n + ns),)),
                      pltpu.SemaphoreType.DMA((3 * n,)), pltpu.SemaphoreType.DMA((3 * n,))],
                start=start, finish=finish)


def run_side(side, name):
    n_in, n_out = len(side["inputs"]), len(side["out_shapes"])

    def body(*refs):
        src, dst, sems = refs[:n_in], refs[n_in:n_in + n_out], refs[n_in + n_out:]
        side["start"](src, dst, sems)
        side["finish"](src, dst, sems)

    return pl.pallas_call(
        body, name=name, in_specs=[HBM_SPEC] * n_in, out_specs=[HBM_SPEC] * n_out,
        out_shape=side["out_shapes"], input_output_aliases=side["aliases"], scratch_shapes=side["sems"],
    )(*side["inputs"])


def allreduce_small(block):
    R, C = block.shape

    def body(x_ref, out_ref, all_ref, send_sems, recv_sems, local_sem):
        x, y, c = _place()
        me, sibling = (x, y, c), (x, y, 1 - c)
        chips = [(1 - x, y), (x, 1 - y), (1 - x, 1 - y)]

        def slot(px, py, pc):
            return all_ref.at[4 * px + 2 * py + pc]

        def copy(k, block_of, to, src=None):
            return pltpu.make_async_remote_copy(
                src_ref=slot(*block_of) if src is None else src, dst_ref=slot(*block_of),
                send_sem=send_sems.at[k], recv_sem=recv_sems.at[k], device_id=to, device_id_type=MESH)

        mine = pltpu.make_async_copy(x_ref, slot(*me), local_sem)
        mine.start()
        first = [copy(0, me, sibling, src=x_ref)]
        first += [copy(1 + j, me, (*chip, c), src=x_ref) for j, chip in enumerate(chips)]
        for cp in first:
            cp.start()
        passed = [copy(4 + j, (*chip, c), sibling) for j, chip in enumerate(chips)]
        for j, chip in enumerate(chips):
            copy(1 + j, (*chip, c), me).wait_recv()
            passed[j].start()
        copy(0, sibling, me).wait_recv()
        for j, chip in enumerate(chips):
            copy(4 + j, (*chip, 1 - c), me).wait_recv()
        for cp in first + passed:
            cp.wait_send()
        mine.wait()
        total = all_ref[0]
        for d in range(1, N_DEV):
            total = total + all_ref[d]
        out_ref[...] = total

    return pl.pallas_call(
        body, name="allreduce_small",
        in_specs=[pl.BlockSpec(memory_space=pltpu.VMEM)], out_specs=pl.BlockSpec(memory_space=pltpu.VMEM),
        out_shape=jax.ShapeDtypeStruct((R, C), F32),
        scratch_shapes=[pltpu.VMEM((N_DEV, R, C), F32), pltpu.SemaphoreType.DMA((7,)),
                        pltpu.SemaphoreType.DMA((7,)), pltpu.SemaphoreType.DMA],
        compiler_params=pltpu.CompilerParams(vmem_limit_bytes=VMEM_LIMIT),
    )(block)


def exchange_siblings_side(grads):
    n = len(grads)

    def copies(src, dst, sems):
        x, y, c = _place()
        for k in range(n):
            half = src[k].shape[1] // 2
            yield pltpu.make_async_remote_copy(src[k].at[:, pl.ds((1 - c) * half, half)], dst[k],
                                               sems[0].at[k], sems[1].at[k],
                                               device_id=(x, y, 1 - c), device_id_type=MESH)

    def start(src, dst, sems):
        for cp in copies(src, dst, sems):
            cp.start()

    def finish(src, dst, sems):
        for cp in copies(src, dst, sems):
            cp.wait()

    return dict(inputs=list(grads), aliases={},
                out_shapes=[jax.ShapeDtypeStruct((N_CHIPS, g.shape[1] // 2, g.shape[2]), F32) for g in grads],
                sems=[pltpu.SemaphoreType.DMA((n,)), pltpu.SemaphoreType.DMA((n,))], start=start, finish=finish)


def rs_chip_sum(place, grad, sib, name):
    NP, R, C = grad.shape
    half = R // 2
    tr = _row_tile(half, TR_ELT)
    nr = half // tr

    def body(pc_ref, g_ref, s_ref, wire_ref, own_ref):
        q = pl.program_id(1)
        total = g_ref[...] + s_ref[...]
        wire_ref[...] = total.astype(BF16)

        @pl.when(q == pc_ref[0])
        def _():
            own_ref[...] = total

    return pl.pallas_call(
        body, name=name,
        grid_spec=pltpu.PrefetchScalarGridSpec(
            num_scalar_prefetch=1, grid=(nr, NP),
            in_specs=[pl.BlockSpec((None, tr, C), lambda r, q, pc: (q, pc[1] * nr + r, 0)),
                      pl.BlockSpec((None, tr, C), lambda r, q, pc: (q, r, 0))],
            out_specs=[pl.BlockSpec((None, tr, C), lambda r, q, pc: (q, r, 0)),
                       pl.BlockSpec((tr, C), lambda r, q, pc: (r, 0))]),
        out_shape=[jax.ShapeDtypeStruct((NP, half, C), BF16), jax.ShapeDtypeStruct((half, C), F32)],
        compiler_params=_params(("arbitrary", "arbitrary")),
    )(place, grad, sib)


def exchange_chips_side(wires):
    n = len(wires)

    def copies(src, dst, sems):
        x, y, c = _place()
        for k in range(n):
            for j in (1, 2, 3):
                qx, qy = x ^ (j >> 1), y ^ (j & 1)
                yield pltpu.make_async_remote_copy(src[k].at[2 * qx + qy], dst[k].at[2 * x + y],
                                                   sems[0].at[3 * k + j - 1], sems[1].at[3 * k + j - 1],
                                                   device_id=(qx, qy, c), device_id_type=MESH)

    def start(src, dst, sems):
        for cp in copies(src, dst, sems):
            cp.start()

    def finish(src, dst, sems):
        for cp in copies(src, dst, sems):
            cp.wait()

    return dict(inputs=list(wires), out_shapes=[jax.ShapeDtypeStruct(w.shape, BF16) for w in wires], aliases={},
                sems=[pltpu.SemaphoreType.DMA((3 * n,)), pltpu.SemaphoreType.DMA((3 * n,))],
                start=start, finish=finish)


SEM_SPEC = pl.BlockSpec(memory_space=pltpu.SEMAPHORE)


def exchange_chips_start(wires, name):
    n = len(wires)
    side = exchange_chips_side(wires)

    def body(*refs):
        src, land, sems = refs[:n], refs[n:2 * n], refs[2 * n:2 * n + 2]
        side["start"](src, land, sems)
        refs[-1][...] = jnp.zeros_like(refs[-1])

    hbm = [pltpu.HBM(w.shape, w.dtype) for w in wires]
    outs = pl.pallas_call(
        body, name=name, in_specs=[HBM_SPEC] * (2 * n),
        out_specs=[SEM_SPEC, SEM_SPEC] + [HBM_SPEC] * (2 * n) + [pl.BlockSpec(memory_space=pltpu.VMEM)],
        out_shape=list(side["sems"]) + hbm + hbm + [jax.ShapeDtypeStruct((8, 128), F32)],
        input_output_aliases={k: 2 + k for k in range(2 * n)},
        compiler_params=pltpu.CompilerParams(has_side_effects=pltpu.SideEffectType.DATAFLOW_SIDE_EFFECTING),
    )(*[pltpu.with_memory_space_constraint(w, pltpu.HBM) for w in wires],
      *[pltpu.with_memory_space_constraint(lax.empty(w.shape, w.dtype), pltpu.HBM) for w in wires])
    return outs[0], outs[1], outs[2:2 + n], outs[2 + n:2 + 2 * n], outs[-1]


def exchange_chips_wait(send_sems, recv_sems, wires, lands, after, name):
    n = len(wires)
    side = exchange_chips_side(wires)

    def body(*refs):
        side["finish"](refs[:n], refs[n:2 * n], refs[2 * n:2 * n + 2])

    hbm = [pltpu.HBM(w.shape, w.dtype) for w in wires]
    outs = pl.pallas_call(
        body, name=name, in_specs=[HBM_SPEC] * (2 * n) + [SEM_SPEC, SEM_SPEC] + [HBM_SPEC] * len(after),
        out_specs=[HBM_SPEC] * (2 * n), out_shape=hbm + hbm,
        input_output_aliases={k: k for k in range(2 * n)},
        compiler_params=pltpu.CompilerParams(has_side_effects=pltpu.SideEffectType.DATAFLOW_SIDE_EFFECTING),
    )(*wires, *lands, send_sems, recv_sems, *after)
    return outs[n:]


def rs_final_sum(place, own, got, after, name):
    NP, half, C = got.shape
    tr = _row_tile(half, TR_ELT)
    nr = half // tr

    def body(pc_ref, own_ref, g1_ref, g2_ref, g3_ref, after_ref, out_ref):
        out_ref[...] = ((own_ref[...] + g1_ref[...].astype(F32)) + g2_ref[...].astype(F32)) + g3_ref[...].astype(F32)

    slot = lambda j: pl.BlockSpec((None, tr, C), lambda r, pc: (pc[0] ^ j, r, 0))
    return pl.pallas_call(
        body, name=name,
        grid_spec=pltpu.PrefetchScalarGridSpec(
            num_scalar_prefetch=1, grid=(nr,),
            in_specs=[pl.BlockSpec((tr, C), lambda r, pc: (r, 0)), slot(1), slot(2), slot(3),
                      pl.BlockSpec((8, 128), lambda r, pc: (0, 0))],
            out_specs=pl.BlockSpec((tr, C), lambda r, pc: (pc[1] * nr + r, 0))),
        out_shape=jax.ShapeDtypeStruct((2 * half, C), F32),
        compiler_params=_params(("arbitrary",)),
    )(place, own, got, got, got, after)


def rs_share_siblings(totals, name):
    n = len(totals)

    def body(*refs):
        dst = refs[n:2 * n]
        send_sems, recv_sems = refs[2 * n:]
        x, y, c = _place()
        copies = []
        for k in range(n):
            half = dst[k].shape[0] // 2
            rows = dst[k].at[pl.ds(c * half, half)]
            cp = pltpu.make_async_remote_copy(rows, rows, send_sems.at[k], recv_sems.at[k],
                                              device_id=(x, y, 1 - c), device_id_type=MESH)
            cp.start()
            copies.append(cp)
        for k, cp in enumerate(copies):
            cp.wait_send()
            half = dst[k].shape[0] // 2
            got = dst[k].at[pl.ds((1 - c) * half, half)]
            pltpu.make_async_remote_copy(got, got, send_sems.at[k], recv_sems.at[k],
                                         device_id=(x, y, c), device_id_type=MESH).wait_recv()

    return pl.pallas_call(
        body, name=name,
        in_specs=[HBM_SPEC] * n, out_specs=[HBM_SPEC] * n,
        out_shape=[jax.ShapeDtypeStruct(t.shape, F32) for t in totals],
        input_output_aliases={k: k for k in range(n)},
        scratch_shapes=[pltpu.SemaphoreType.DMA((n,)), pltpu.SemaphoreType.DMA((n,))],
    )(*totals)


def rs_to_wires(place, grads, tag, sibs=None):
    if sibs is None:
        sibs = run_side(exchange_siblings_side(grads), f"rs_exchange_siblings_{tag}")
    wires, owns = [], []
    for k, (g, s) in enumerate(zip(grads, sibs)):
        w, o = rs_chip_sum(place, g, s, f"rs_chip_sum_{tag}{k}")
        wires.append(w)
        owns.append(o)
    return wires, owns


def rs_finish(place, owns, gots, after, tag):
    totals = [rs_final_sum(place, o, g, after, f"rs_final_sum_{tag}{k}") for k, (o, g) in enumerate(zip(owns, gots))]
    return rs_share_siblings(totals, f"rs_share_siblings_{tag}")


def _rope_tables(positions):
    half = HEAD_DIM // 2
    inv_freq = ROPE_THETA ** (-jnp.arange(half, dtype=F32) / half)
    ang = positions.astype(F32)[:, None] * inv_freq
    cos, sin = jnp.cos(ang), jnp.sin(ang)
    return jnp.tile(cos, (1, 4)), jnp.concatenate([-sin, sin, -sin, sin], axis=1)


def _cols_from_pieces(pieces, start, stop):
    C = pieces.shape[2]
    parts = []
    for q in range(N_CHIPS):
        lo, hi = max(start, q * C), min(stop, (q + 1) * C)
        if lo < hi:
            parts.append(pieces[q][:, lo - q * C:hi - q * C])
    return parts[0] if len(parts) == 1 else jnp.concatenate(parts, axis=1)


def _pieces_from_groups(groups):
    C = sum(g.shape[1] for g in groups) // N_CHIPS
    pieces = []
    for q in range(N_CHIPS):
        parts, off = [], 0
        for g in groups:
            lo, hi = max(q * C, off), min((q + 1) * C, off + g.shape[1])
            if lo < hi:
                parts.append(g[:, lo - off:hi - off])
            off += g.shape[1]
        pieces.append(parts[0] if len(parts) == 1 else jnp.concatenate(parts, axis=1))
    return jnp.stack(pieces)


def kernel(x, positions, ffn1_norm, ffn1_w_gate, ffn1_w_up, ffn1_w_down, mix_norm, w_in, conv_dw_w, conv_dw_b, conv_ln_g, conv_ln_b, conv_w_proj, attn_sinks, attn_w_o, gate_b, w_out, ffn2_norm, ffn2_w_gate, ffn2_w_up, ffn2_w_down, final_norm, loss_target, m_ffn1_norm, m_ffn1_w_gate, m_ffn1_w_up, m_ffn1_w_down, m_mix_norm, m_w_in, m_conv_dw_w, m_conv_dw_b, m_conv_ln_g, m_conv_ln_b, m_conv_w_proj, m_attn_sinks, m_attn_w_o, m_gate_b, m_w_out, m_ffn2_norm, m_ffn2_w_gate, m_ffn2_w_up, m_ffn2_w_down, m_final_norm, v_ffn1_norm, v_ffn1_w_gate, v_ffn1_w_up, v_ffn1_w_down, v_mix_norm, v_w_in, v_conv_dw_w, v_conv_dw_b, v_conv_ln_g, v_conv_ln_b, v_conv_w_proj, v_attn_sinks, v_attn_w_o, v_gate_b, v_w_out, v_ffn2_norm, v_ffn2_w_gate, v_ffn2_w_up, v_ffn2_w_down, v_final_norm):
    weights = dict(ffn1_norm=ffn1_norm, ffn1_w_gate=ffn1_w_gate, ffn1_w_up=ffn1_w_up, ffn1_w_down=ffn1_w_down,
                   mix_norm=mix_norm, w_in=w_in, conv_dw_w=conv_dw_w, conv_dw_b=conv_dw_b, conv_ln_g=conv_ln_g,
                   conv_ln_b=conv_ln_b, conv_w_proj=conv_w_proj, attn_sinks=attn_sinks, attn_w_o=attn_w_o,
                   gate_b=gate_b, w_out=w_out, ffn2_norm=ffn2_norm, ffn2_w_gate=ffn2_w_gate, ffn2_w_up=ffn2_w_up,
                   ffn2_w_down=ffn2_w_down, final_norm=final_norm)
    m_in = dict(ffn1_norm=m_ffn1_norm, ffn1_w_gate=m_ffn1_w_gate, ffn1_w_up=m_ffn1_w_up, ffn1_w_down=m_ffn1_w_down,
                mix_norm=m_mix_norm, w_in=m_w_in, conv_dw_w=m_conv_dw_w, conv_dw_b=m_conv_dw_b,
                conv_ln_g=m_conv_ln_g, conv_ln_b=m_conv_ln_b, conv_w_proj=m_conv_w_proj, attn_sinks=m_attn_sinks,
                attn_w_o=m_attn_w_o, gate_b=m_gate_b, w_out=m_w_out, ffn2_norm=m_ffn2_norm,
                ffn2_w_gate=m_ffn2_w_gate, ffn2_w_up=m_ffn2_w_up, ffn2_w_down=m_ffn2_w_down, final_norm=m_final_norm)
    v_in = dict(ffn1_norm=v_ffn1_norm, ffn1_w_gate=v_ffn1_w_gate, ffn1_w_up=v_ffn1_w_up, ffn1_w_down=v_ffn1_w_down,
                mix_norm=v_mix_norm, w_in=v_w_in, conv_dw_w=v_conv_dw_w, conv_dw_b=v_conv_dw_b,
                conv_ln_g=v_conv_ln_g, conv_ln_b=v_conv_ln_b, conv_w_proj=v_conv_w_proj, attn_sinks=v_attn_sinks,
                attn_w_o=v_attn_w_o, gate_b=v_gate_b, w_out=v_w_out, ffn2_norm=v_ffn2_norm,
                ffn2_w_gate=v_ffn2_w_gate, ffn2_w_up=v_ffn2_w_up, ffn2_w_down=v_ffn2_w_down, final_norm=v_final_norm)
    names = list(weights)
    big = ["ffn1_w_gate", "ffn1_w_up", "ffn1_w_down", "w_in", "conv_w_proj", "attn_w_o", "w_out",
           "ffn2_w_gate", "ffn2_w_up", "ffn2_w_down"]
    transposed = [k for k in big if k.endswith(("w_gate", "w_up"))]
    for k in transposed:
        weights[k], m_in[k], v_in[k] = (jnp.swapaxes(a, 1, 2) for a in (weights[k], m_in[k], v_in[k]))

    xs = x[0]
    T, D = xs.shape
    KV = (w_in.shape[2] * N_CHIPS - 5 * D) // 2
    n_heads = D // HEAD_DIM
    my_chip = 2 * lax.axis_index("x") + lax.axis_index("y")
    place = jnp.stack([my_chip, lax.axis_index("c")]).astype(jnp.int32)

    placed = {k: place_shard(place, weights[k][0], BF16, f"place_{k}") for k in big}
    placed_dw = place_shard(place, conv_dw_w[0], F32, "place_conv_dw_w")
    first, mixer_w, second = big[:3], big[3:7], big[7:]
    wg1, wu1, wd1 = run_side(gather_side([placed[k] for k in first], []), "gather_ffn1")
    x1, h1, g1, u1, *gathered = ffn_fwd(x[0], ffn1_norm, wg1, wu1, wd1, "ffn1_fwd",
                                        side=gather_side([placed[k] for k in mixer_w], [placed_dw]))
    full = dict(zip(mixer_w + ["conv_dw_w"], gathered))
    w_glu = _cols_from_pieces(full["w_in"], 0, 2 * D)
    w_qkv = _cols_from_pieces(full["w_in"], 2 * D, 3 * D + 2 * KV)
    w_gate = _cols_from_pieces(full["w_in"], 3 * D + 2 * KV, 5 * D + 2 * KV)
    w_proj = full["conv_w_proj"].reshape(D, D)
    w_o = full["attn_w_o"].reshape(D, D)
    w_out_f = full["w_out"].reshape(D, D)
    dw_w = full["conv_dw_w"].transpose(1, 0, 2).reshape(CONV_WIDTH, D)
    dw_w = jnp.concatenate([dw_w, jnp.zeros((CONV_HALO - CONV_WIDTH, D), F32)], axis=0)
    cs, sn = _rope_tables(positions[0])
    fn_row = final_norm.reshape(1, D)

    h2, p_glu, p_gate, qr, kr, vb = mix_in_fwd(x1, mix_norm, w_glu, w_qkv, w_gate, cs, sn, "mix_in_fwd")
    c1, c3 = conv_fwd(p_glu, dw_w, conv_dw_b, conv_ln_g, conv_ln_b, "conv_fwd")
    o, wg2, wu2, wd2 = attn_fwd(qr, kr, vb, attn_sinks, "attn_fwd",
                                side=gather_side([placed[k] for k in second], []))
    x2, conv_out, attn_out, merged = merge_fwd(x1, c3, o, p_gate, gate_b, w_proj, w_o, w_out_f, "merge_fwd")
    x3, h3, g2, u2 = ffn_fwd(x2, ffn2_norm, wg2, wu2, wd2, "ffn2_fwd")

    dx3, head_sums = loss_head(x3, fn_row, loss_target[0], "loss_head")
    dx2, dwg2, dwu2, dwd2, d_ffn2_norm = ffn_bwd(x2, ffn2_norm, h3, g2, u2, wg2, wu2, wd2, dx3, "ffn2_bwd")
    ffn2_grads = [dwg2, dwu2, dwd2]
    d_gates, d_conv_out, d_attn_out, d_o, dc1, merge_sums, *sibs_f2 = merge_bwd(
        dx2, p_gate, gate_b, conv_out, attn_out, c1, conv_ln_g, conv_ln_b, w_proj, w_o, w_out_f, "merge_bwd",
        side=exchange_siblings_side(ffn2_grads))
    d_w_out = matmul_tn(merged, dx2, "d_w_out")
    d_w_proj = matmul_tn(c3, d_conv_out, "d_conv_w_proj")
    d_w_o = matmul_tn(o, d_attn_out, "d_attn_w_o")
    wires_f2, owns_f2 = rs_to_wires(place, ffn2_grads, "ffn2", sibs=sibs_f2)
    d_glu, d_dw_w, *gots_f2 = conv_bwd(p_glu, dc1, dw_w, "conv_bwd", side=exchange_chips_side(wires_f2))
    dq, dk, dv, d_sinks = attn_bwd(qr, kr, vb, o, d_o, attn_sinks, "attn_bwd")
    d_qkv = rope_bwd(dq, dk, dv, cs, sn, "rope_bwd")
    dx1, d_mix_norm = mix_in_bwd([d_glu, d_qkv, d_gates], [w_glu, w_qkv, w_gate], x1, mix_norm, dx2, "mix_in_bwd")
    d_w_in = _pieces_from_groups([matmul_tn(h2, d_glu, "d_w_in_glu"), matmul_tn(h2, d_qkv, "d_w_in_qkv"),
                                  matmul_tn(h2, d_gates, "d_w_in_gate")])
    dwc = D // N_CHIPS
    mixer_grads = [d_w_in, d_w_proj.reshape(N_CHIPS, dwc, D), d_w_o.reshape(N_CHIPS, dwc, D),
                   d_w_out.reshape(N_CHIPS, dwc, D)]
    wires_m, owns_m = rs_to_wires(place, mixer_grads, "mixer")
    dx0, dwg1, dwu1, dwd1, d_ffn1_norm, *gots_m = ffn_bwd(xs, ffn1_norm, h1, g1, u1, wg1, wu1, wd1, dx1, "ffn1_bwd",
                                                          side=exchange_chips_side(wires_m))
    wires_l, owns_l = rs_to_wires(place, [dwg1, dwu1, dwd1], "ffn1")
    send_sems, recv_sems, wires_l, lands_l, token = exchange_chips_start(wires_l, "rs_exchange_chips_ffn1_start")
    early_names = ["ffn2_w_gate", "ffn2_w_up", "ffn2_w_down", "w_in", "conv_w_proj", "attn_w_o", "w_out"]
    late_names = ["ffn1_w_gate", "ffn1_w_up", "ffn1_w_down"]
    reduced_early = rs_finish(place, owns_f2 + owns_m, list(gots_f2) + list(gots_m), token, "early")

    pad_row = lambda v: jnp.pad(v, ((0, 0), (0, D - v.shape[1])))
    small_rows = jnp.concatenate([
        d_ffn1_norm, d_mix_norm, merge_sums[2:3, :D], merge_sums[1:2, :D], merge_sums[1:2, D:],
        pad_row(d_sinks[0:1, :n_heads]), merge_sums[0:1, :D], merge_sums[0:1, D:], d_ffn2_norm,
        head_sums[0:1], head_sums[1:2], jnp.zeros((5, D), F32), d_dw_w], axis=0)
    small = allreduce_small(small_rows)
    loss = small[10, 0]
    grads = {"ffn1_norm": small[0:1], "mix_norm": small[1:2], "conv_dw_b": small[2:3], "conv_ln_g": small[3:4],
             "conv_ln_b": small[4:5], "attn_sinks": small[5:6, :n_heads],
             "gate_b": jnp.concatenate([small[6:7], small[7:8]], axis=1), "ffn2_norm": small[8:9],
             "final_norm": small[9:10]}
    grads["conv_dw_w"] = lax.dynamic_slice(small[16:16 + CONV_WIDTH], (0, my_chip * dwc), (CONV_WIDTH, dwc))
    grads.update(zip(early_names, reduced_early))

    deltas, new_m, new_v = {}, {}, {}

    def apply_adamw(k):
        shape = weights[k].shape
        g2d = grads[k].reshape(-1, shape[-1])
        grads[k] = g2d.reshape(shape)
        d, mn, vn = adamw(weights[k].reshape(g2d.shape), g2d, m_in[k].reshape(g2d.shape),
                          v_in[k].reshape(g2d.shape), f"adamw_{k}")
        deltas[k], new_m[k], new_v[k] = d.reshape(shape), mn.reshape(shape), vn.reshape(shape)
        return d

    done = {k: apply_adamw(k) for k in names if k not in late_names}
    gots_l = exchange_chips_wait(send_sems, recv_sems, wires_l, lands_l, [done[k] for k in early_names],
                                 "rs_exchange_chips_ffn1_wait")
    grads.update(zip(late_names, rs_finish(place, owns_l, gots_l, token, "late")))
    for k in late_names:
        apply_adamw(k)
    for k in transposed:
        for group in (grads, deltas, new_m, new_v):
            group[k] = jnp.swapaxes(group[k], 1, 2)

    return (loss, dx0[None], *[grads[k] for k in names], *[deltas[k] for k in names],
            *[new_m[k] for k in names], *[new_v[k] for k in names])
```

```python
import functools

import jax
import jax.numpy as jnp
from jax import lax
from jax.experimental import pallas as pl
from jax.experimental.pallas import tpu as pltpu

F32 = jnp.float32
BF16 = jnp.bfloat16
MESH = pl.DeviceIdType.MESH

HEAD_DIM = 64
WINDOW = 128
CONV_WIDTH = 31
CONV_HALO = 32
ROPE_THETA = 10000.0
EPS = 1e-6
LN_EPS = 1e-5
NEG_INF = -1e30
N_CHIPS = 4
N_DEV = 8

ADAM_LR = 0.001
ADAM_B1 = 0.9
ADAM_B2 = 0.999
ADAM_EPS = 1e-08
ADAM_WD = 0.01
ADAM_STEP = 10

TM_FFN = 512
TM_FFN_FWD = 1024
TM_ROW = 256
TK_TN = 1024
TR_ELT = 256
VMEM_LIMIT = 56 * 1024 * 1024

NT_DIMS = (((1,), (1,)), ((), ()))
TN_DIMS = (((0,), (0,)), ((), ()))


def _row_tile(rows, cap):
    for t in range(min(cap, rows), 15, -1):
        if rows % t == 0 and t % 16 == 0:
            return t
    return rows


def _params(sem):
    return pltpu.CompilerParams(dimension_semantics=sem, vmem_limit_bytes=VMEM_LIMIT)


def _dot(a, b):
    return jnp.dot(a, b, preferred_element_type=F32)


def _dot_nt(a, b):
    return lax.dot_general(a, b, NT_DIMS, preferred_element_type=F32)


def _dot_tn(a, b):
    return lax.dot_general(a, b, TN_DIMS, preferred_element_type=F32)


def _split_rows(dot, a, b):
    m = a.shape[0] // 2
    return jnp.concatenate([dot(a[:m], b), dot(a[m:], b)], axis=0)


def _sigmoid(x):
    return jax.nn.sigmoid(x)


def _rms_scale(xv):
    return lax.rsqrt(jnp.mean(xv * xv, axis=-1, keepdims=True) + EPS)


def _rms_bwd(xv, nw, dh):
    r = _rms_scale(xv)
    dn = dh * nw
    dx = r * dn - xv * (r * r * r) * jnp.mean(dn * xv, axis=-1, keepdims=True)
    dnw = jnp.sum(dh * (xv * r), axis=0, keepdims=True)
    return dx, dnw


def _silu_grad(z, s):
    return s * (1.0 + z * (1.0 - s))


HBM_SPEC = pl.BlockSpec(memory_space=pl.ANY)


def _call_hosting(body, side, *, grid, in_specs, out_specs, out_shape, scratch_shapes, operands, name, aliases=None):
    params = _params(("arbitrary",) * len(grid))
    aliases = dict(aliases or {})
    if side is None:
        return pl.pallas_call(body, name=name, grid=grid, in_specs=in_specs, out_specs=out_specs, out_shape=out_shape,
                              scratch_shapes=scratch_shapes, input_output_aliases=aliases,
                              compiler_params=params)(*operands)
    n_in, n_out, n_scr = len(in_specs), len(out_shape), len(scratch_shapes)
    s_in, s_out = len(side["inputs"]), len(side["out_shapes"])

    steps = 1
    for extent in grid:
        steps *= extent

    def at_step(index):
        linear = pl.program_id(0)
        for a in range(1, len(grid)):
            linear = linear * grid[a] + pl.program_id(a)
        return linear == index

    def hosted(*refs):
        b = n_in + s_in
        c = b + n_out
        d = c + s_out
        e = d + n_scr
        src, dst, sems = refs[n_in:b], refs[c:d], refs[e:]

        @pl.when(at_step(0))
        def _():
            side["start"](src, dst, sems)

        if "relay" in side:
            @pl.when(at_step(min((3 * steps) // 4, steps - 1)))
            def _():
                side["relay"](src, dst, sems)

        body(*refs[:n_in], *refs[b:c], *refs[d:e])

        @pl.when(at_step(steps - 1))
        def _():
            side["finish"](src, dst, sems)

    return pl.pallas_call(
        hosted, name=name, grid=grid, in_specs=list(in_specs) + [HBM_SPEC] * s_in,
        out_specs=list(out_specs) + [HBM_SPEC] * s_out, out_shape=list(out_shape) + list(side["out_shapes"]),
        scratch_shapes=list(scratch_shapes) + list(side["sems"]),
        input_output_aliases={**aliases, **{n_in + a: n_out + b for a, b in side["aliases"].items()}},
        compiler_params=params)(*operands, *side["inputs"])


def ffn_fwd(x, nw, wg, wu, wd, name, side=None):
    T, D = x.shape
    NP, Fs, _ = wg.shape
    tm = min(TM_FFN_FWD, T)

    def body(x_ref, nw_ref, wg_ref, wu_ref, wd_ref, xo_ref, h_ref, g_ref, u_ref, acc_ref):
        j = pl.program_id(1)

        @pl.when(j == 0)
        def _():
            xv = x_ref[...]
            h_ref[...] = (xv * _rms_scale(xv) * nw_ref[...]).astype(BF16)
            acc_ref[...] = jnp.zeros_like(acc_ref)

        h = h_ref[...]
        g = _dot_nt(h, wg_ref[...])
        u = _dot_nt(h, wu_ref[...])
        a = (g * _sigmoid(g)) * u
        g_ref[...] = g.astype(BF16)
        u_ref[...] = u.astype(BF16)
        acc_ref[...] += _dot(a.astype(BF16), wd_ref[...])

        @pl.when(j == NP - 1)
        def _():
            xo_ref[...] = x_ref[...] + 0.5 * acc_ref[...]

    return _call_hosting(
        body, side, name=name, grid=(T // tm, NP),
        in_specs=[pl.BlockSpec((tm, D), lambda i, j: (i, 0)),
                  pl.BlockSpec((1, D), lambda i, j: (0, 0)),
                  pl.BlockSpec((None, Fs, D), lambda i, j: (j, 0, 0)),
                  pl.BlockSpec((None, Fs, D), lambda i, j: (j, 0, 0)),
                  pl.BlockSpec((None, Fs, D), lambda i, j: (j, 0, 0))],
        out_specs=[pl.BlockSpec((tm, D), lambda i, j: (i, 0)),
                   pl.BlockSpec((tm, D), lambda i, j: (i, 0)),
                   pl.BlockSpec((None, tm, Fs), lambda i, j: (j, i, 0)),
                   pl.BlockSpec((None, tm, Fs), lambda i, j: (j, i, 0))],
        out_shape=[jax.ShapeDtypeStruct((T, D), F32), jax.ShapeDtypeStruct((T, D), BF16),
                   jax.ShapeDtypeStruct((NP, T, Fs), BF16), jax.ShapeDtypeStruct((NP, T, Fs), BF16)],
        scratch_shapes=[pltpu.VMEM((tm, D), F32)],
        operands=(x, nw, wg, wu, wd))


def _ffn_bwd_piece(j, h, g, u, wg, wu, wd, dout, dh_in, dws_in, name, side, norm):
    T, D = h.shape
    NP, Fs, _ = wg.shape
    tm = min(TM_FFN, T)
    n_in = 7 + (dh_in is not None) + (2 if norm else 0) + (3 if dws_in else 0)

    def body(*refs):
        h_ref, g_ref, u_ref, wg_ref, wu_ref, wd_ref, do_ref = refs[:7]
        dhin_ref = refs[7] if dh_in is not None else None
        dh_ref, dwg_ref, dwu_ref, dwd_ref = refs[n_in:n_in + 4]

        @pl.when(pl.program_id(0) == 0)
        def _():
            dwg_ref[...] = jnp.zeros_like(dwg_ref)
            dwu_ref[...] = jnp.zeros_like(dwu_ref)
            dwd_ref[...] = jnp.zeros_like(dwd_ref)
            if norm:
                refs[n_in + 4][...] = jnp.zeros_like(refs[n_in + 4])

        dob = (0.5 * do_ref[...]).astype(BF16)
        da = _split_rows(_dot_nt, dob, wd_ref[...])
        gf = g_ref[...].astype(F32)
        uf = u_ref[...].astype(F32)
        s = _sigmoid(gf)
        act = gf * s
        dg = (da * uf * _silu_grad(gf, s)).astype(BF16)
        du = (da * act).astype(BF16)
        a = (act * uf).astype(BF16)
        dh = _dot(dg, wg_ref[...]) + _dot(du, wu_ref[...])
        dh = dh if dhin_ref is None else dhin_ref[...] + dh
        if norm:
            x_ref, nw_ref = refs[7 + (dh_in is not None):9 + (dh_in is not None)]
            dxn, dnw = _rms_bwd(x_ref[...], nw_ref[...], dh)
            dh_ref[...] = do_ref[...] + dxn
            refs[n_in + 4][...] += dnw
        else:
            dh_ref[...] = dh
        hb = h_ref[...]
        dwg_ref[...] += _dot_tn(dg, hb)
        dwu_ref[...] += _dot_tn(du, hb)
        dwd_ref[...] += _dot_tn(a, dob)

    rows = pl.BlockSpec((tm, D), lambda i: (i, 0))
    piece = pl.BlockSpec((None, tm, Fs), lambda i: (j, i, 0))
    slot = pl.BlockSpec((None, Fs, D), lambda i: (j, 0, 0), pipeline_mode=pl.Buffered(1))
    in_specs = [rows, piece, piece, slot, slot, slot, rows]
    operands = [h, g, u, wg, wu, wd, dout]
    aliases = {}
    if dh_in is not None:
        in_specs.append(rows)
        operands.append(dh_in)
    if norm:
        in_specs += [rows, pl.BlockSpec((1, D), lambda i: (0, 0))]
        operands += list(norm)
    if dws_in:
        aliases = {len(operands) + k: 1 + k for k in range(3)}
        in_specs += [HBM_SPEC] * 3
        operands += list(dws_in)
    out_specs = [rows, slot, slot, slot]
    out_shape = [jax.ShapeDtypeStruct((T, D), F32)] + [jax.ShapeDtypeStruct((NP, Fs, D), F32)] * 3
    if norm:
        out_specs.append(pl.BlockSpec((1, D), lambda i: (0, 0)))
        out_shape.append(jax.ShapeDtypeStruct((1, D), F32))
    return _call_hosting(body, side, name=name, grid=(T // tm,), in_specs=in_specs, out_specs=out_specs,
                         out_shape=out_shape, scratch_shapes=[], aliases=aliases, operands=tuple(operands))


def ffn_bwd(x, nw, h, g, u, wg, wu, wd, dout, name, side=None):
    NP = wg.shape[0]
    dh, dws, extra = None, None, []
    for j in range(NP):
        dh, *rest = _ffn_bwd_piece(j, h, g, u, wg, wu, wd, dout, dh, dws, f"{name}_{j}",
                                   side if j == 0 else None, (x, nw) if j == NP - 1 else None)
        dws, rest = rest[:3], rest[3:]
        if j == 0:
            extra = rest[1:] if NP == 1 else rest
    return (dh, *dws, rest[0], *extra)


def mix_in_fwd(x, nw, w_glu, w_qkv, w_gate, cs, sn, name):
    T, D = x.shape
    KV = (w_qkv.shape[1] - D) // 2
    tm = min(TM_ROW, T)

    def body(x_ref, nw_ref, wa_ref, wq_ref, wg_ref, cs_ref, sn_ref, h_ref, pa_ref, pg_ref, q_ref, k_ref, v_ref):
        xv = x_ref[...]
        h = (xv * _rms_scale(xv) * nw_ref[...]).astype(BF16)
        h_ref[...] = h
        pa_ref[...] = _dot(h, wa_ref[...])
        pg_ref[...] = _dot(h, wg_ref[...])
        qkv = _dot(h, wq_ref[...])
        cs_v, sn_v = cs_ref[...], sn_ref[...]
        q_ref[...] = _rope_chunks(qkv[:, :D], cs_v, sn_v, 1.0).astype(BF16)
        k_ref[...] = _rope_chunks(qkv[:, D:D + KV], cs_v, sn_v, 1.0).astype(BF16)
        v_ref[...] = qkv[:, D + KV:].astype(BF16)

    rows = lambda w: pl.BlockSpec((tm, w), lambda i: (i, 0))
    whole = lambda a: pl.BlockSpec(a.shape, lambda i: (0, 0))
    return pl.pallas_call(
        body, name=name, grid=(T // tm,),
        in_specs=[rows(D), whole(nw), whole(w_glu), whole(w_qkv), whole(w_gate), rows(128), rows(128)],
        out_specs=[rows(D), rows(2 * D), rows(2 * D), rows(D), rows(KV), rows(KV)],
        out_shape=[jax.ShapeDtypeStruct((T, D), BF16), jax.ShapeDtypeStruct((T, 2 * D), F32),
                   jax.ShapeDtypeStruct((T, 2 * D), F32), jax.ShapeDtypeStruct((T, D), BF16),
                   jax.ShapeDtypeStruct((T, KV), BF16), jax.ShapeDtypeStruct((T, KV), BF16)],
        compiler_params=_params(("parallel",)),
    )(x, nw, w_glu, w_qkv, w_gate, cs, sn)


def matmul_tn(lhs, rhs, name):
    T, K = lhs.shape
    N = rhs.shape[1]
    tk = min(TK_TN, T)

    def body(l_ref, r_ref, o_ref):
        @pl.when(pl.program_id(0) == 0)
        def _():
            o_ref[...] = jnp.zeros_like(o_ref)

        o_ref[...] += _dot_tn(l_ref[...].astype(BF16), r_ref[...].astype(BF16))

    return pl.pallas_call(
        body, name=name, grid=(T // tk,),
        in_specs=[pl.BlockSpec((tk, K), lambda t: (t, 0)), pl.BlockSpec((tk, N), lambda t: (t, 0))],
        out_specs=pl.BlockSpec((K, N), lambda t: (0, 0)),
        out_shape=jax.ShapeDtypeStruct((K, N), F32),
        compiler_params=_params(("arbitrary",)),
    )(lhs, rhs)


def mix_in_bwd(dps, ws, x, nw, dres, name):
    T, D = x.shape
    tm = min(TM_ROW, T)
    n = len(dps)

    def body(*refs):
        dp_refs, w_refs = refs[:n], refs[n:2 * n]
        x_ref, nw_ref, dr_ref, dx_ref, dnw_ref = refs[2 * n:]

        @pl.when(pl.program_id(0) == 0)
        def _():
            dnw_ref[...] = jnp.zeros_like(dnw_ref)

        dh = _dot_nt(dp_refs[0][...], w_refs[0][...])
        for k in range(1, n):
            dh += _dot_nt(dp_refs[k][...], w_refs[k][...])
        dxn, dnw = _rms_bwd(x_ref[...], nw_ref[...], dh)
        dx_ref[...] = dr_ref[...] + dxn
        dnw_ref[...] += dnw

    in_specs = [pl.BlockSpec((tm, dp.shape[1]), lambda i: (i, 0)) for dp in dps]
    in_specs += [pl.BlockSpec(w.shape, lambda i: (0, 0)) for w in ws]
    in_specs += [pl.BlockSpec((tm, D), lambda i: (i, 0)), pl.BlockSpec((1, D), lambda i: (0, 0)),
                 pl.BlockSpec((tm, D), lambda i: (i, 0))]
    return pl.pallas_call(
        body, name=name, grid=(T // tm,), in_specs=in_specs,
        out_specs=[pl.BlockSpec((tm, D), lambda i: (i, 0)), pl.BlockSpec((1, D), lambda i: (0, 0))],
        out_shape=[jax.ShapeDtypeStruct((T, D), F32), jax.ShapeDtypeStruct((1, D), F32)],
        compiler_params=_params(("arbitrary",)),
    )(*dps, *ws, x, nw, dres)


def _layernorm_stats(c1):
    mu = jnp.mean(c1, axis=-1, keepdims=True)
    xc = c1 - mu
    rstd = lax.rsqrt(jnp.mean(xc * xc, axis=-1, keepdims=True) + LN_EPS)
    return xc * rstd, rstd


def _shifted_copies(src_ref, dst_ref):
    rows = dst_ref.shape[1]
    for b in range(1, 8):
        dst_ref[b - 1] = src_ref[pl.ds(b, rows), :]


def _shifted_rows(src_ref, shifted_ref, start, rows, cols):
    a8, b = divmod(start, 8)
    if b == 0:
        return src_ref[pl.ds(8 * a8, rows), cols]
    return shifted_ref[b - 1, pl.ds(8 * a8, rows), cols]


def conv_fwd(p_glu, dw_w, dw_b, ln_g, ln_b, name):
    T, D2 = p_glu.shape
    D = D2 // 2
    tm = min(TM_ROW, T)
    hb = tm // CONV_HALO

    def body(a_ref, b_ref, ah_ref, bh_ref, w_ref, wb_ref, g_ref, be_ref, c1_ref, c3_ref, e_ref, es_ref):
        i = pl.program_id(0)
        halo = ah_ref[...] * _sigmoid(bh_ref[...])
        e_ref[pl.ds(0, CONV_HALO), :] = jnp.where(i > 0, halo, 0.0)
        e_ref[pl.ds(CONV_HALO, tm), :] = a_ref[...] * _sigmoid(b_ref[...])
        _shifted_copies(e_ref, es_ref)
        off = CONV_HALO - (CONV_WIDTH - 1)

        def strip(s, carry):
            cols = pl.ds(pl.multiple_of(s * 128, 128), 128)
            acc = jnp.zeros((tm, 128), F32) + wb_ref[:, cols]
            for k in range(CONV_WIDTH):
                acc += w_ref[pl.ds(k, 1), cols] * _shifted_rows(e_ref, es_ref, off + k, tm, cols)
            c1_ref[:, cols] = acc
            return carry

        lax.fori_loop(0, D // 128, strip, 0)
        xhat, _ = _layernorm_stats(c1_ref[...])
        c2 = xhat * g_ref[...] + be_ref[...]
        c3_ref[...] = (c2 * _sigmoid(c2)).astype(BF16)

    row = pl.BlockSpec((1, D), lambda i: (0, 0))
    return pl.pallas_call(
        body, name=name, grid=(T // tm,),
        in_specs=[pl.BlockSpec((tm, D), lambda i: (i, 0)), pl.BlockSpec((tm, D), lambda i: (i, 1)),
                  pl.BlockSpec((CONV_HALO, D), lambda i: (jnp.maximum(i * hb - 1, 0), 0)),
                  pl.BlockSpec((CONV_HALO, D), lambda i: (jnp.maximum(i * hb - 1, 0), 1)),
                  pl.BlockSpec((CONV_HALO, D), lambda i: (0, 0)), row, row, row],
        out_specs=[pl.BlockSpec((tm, D), lambda i: (i, 0)), pl.BlockSpec((tm, D), lambda i: (i, 0))],
        out_shape=[jax.ShapeDtypeStruct((T, D), F32), jax.ShapeDtypeStruct((T, D), BF16)],
        scratch_shapes=[pltpu.VMEM((tm + CONV_HALO, D), F32), pltpu.VMEM((7, tm + CONV_HALO - 8, D), F32)],
        compiler_params=_params(("parallel",)),
    )(p_glu, p_glu, p_glu, p_glu, dw_w, dw_b, ln_g, ln_b)


def conv_bwd(p_glu, dc1, dw_w, name, side=None):
    T, D2 = p_glu.shape
    D = D2 // 2
    tm = min(TM_ROW, T)
    hb = tm // CONV_HALO
    last = T // CONV_HALO - 1
    nblk = T // tm

    def body(a_ref, b_ref, ah_ref, bh_ref, d_ref, dn_ref, w_ref, dp_ref, dw_ref, e_ref, f_ref, es_ref, fs_ref):
        i = pl.program_id(0)

        @pl.when(i == 0)
        def _():
            dw_ref[...] = jnp.zeros_like(dw_ref)

        halo = ah_ref[...] * _sigmoid(bh_ref[...])
        e_ref[pl.ds(0, CONV_HALO), :] = jnp.where(i > 0, halo, 0.0)
        e_ref[pl.ds(CONV_HALO, tm), :] = a_ref[...] * _sigmoid(b_ref[...])
        f_ref[pl.ds(0, tm), :] = d_ref[...]
        f_ref[pl.ds(tm, CONV_HALO), :] = jnp.where(i < nblk - 1, dn_ref[...], 0.0)
        _shifted_copies(e_ref, es_ref)
        _shifted_copies(f_ref, fs_ref)
        off = CONV_HALO - (CONV_WIDTH - 1)

        def strip(s, carry):
            cols = pl.ds(pl.multiple_of(s * 128, 128), 128)
            d = d_ref[:, cols]
            dc0 = jnp.zeros((tm, 128), F32)
            for k in range(CONV_WIDTH):
                dw_ref[pl.ds(k, 1), cols] += jnp.sum(d * _shifted_rows(e_ref, es_ref, off + k, tm, cols),
                                                     axis=0, keepdims=True)
                dc0 += w_ref[pl.ds(k, 1), cols] * _shifted_rows(f_ref, fs_ref, CONV_WIDTH - 1 - k, tm, cols)
            a = a_ref[:, cols]
            sb = _sigmoid(b_ref[:, cols])
            dp_ref[:, cols] = (dc0 * sb).astype(BF16)
            dp_ref[:, pl.ds(pl.multiple_of(D + s * 128, 128), 128)] = (dc0 * a * sb * (1.0 - sb)).astype(BF16)
            return carry

        lax.fori_loop(0, D // 128, strip, 0)

    return _call_hosting(
        body, side, name=name, grid=(nblk,),
        in_specs=[pl.BlockSpec((tm, D), lambda i: (i, 0)), pl.BlockSpec((tm, D), lambda i: (i, 1)),
                  pl.BlockSpec((CONV_HALO, D), lambda i: (jnp.maximum(i * hb - 1, 0), 0)),
                  pl.BlockSpec((CONV_HALO, D), lambda i: (jnp.maximum(i * hb - 1, 0), 1)),
                  pl.BlockSpec((tm, D), lambda i: (i, 0)),
                  pl.BlockSpec((CONV_HALO, D), lambda i: (jnp.minimum((i + 1) * hb, last), 0)),
                  pl.BlockSpec((CONV_HALO, D), lambda i: (0, 0))],
        out_specs=[pl.BlockSpec((tm, D2), lambda i: (i, 0)), pl.BlockSpec((CONV_HALO, D), lambda i: (0, 0))],
        out_shape=[jax.ShapeDtypeStruct((T, D2), BF16), jax.ShapeDtypeStruct((CONV_HALO, D), F32)],
        scratch_shapes=[pltpu.VMEM((tm + CONV_HALO, D), F32), pltpu.VMEM((tm + CONV_HALO, D), F32),
                        pltpu.VMEM((7, tm + CONV_HALO - 8, D), F32), pltpu.VMEM((7, tm + CONV_HALO - 8, D), F32)],
        operands=(p_glu, p_glu, p_glu, p_glu, dc1, dc1, dw_w))


def _rot_half(x):
    lane = lax.broadcasted_iota(jnp.int32, x.shape, 1)
    first = (lane % HEAD_DIM) < HEAD_DIM // 2
    return jnp.where(first, pltpu.roll(x, 128 - HEAD_DIM // 2, 1), pltpu.roll(x, HEAD_DIM // 2, 1))


def _rope_chunks(x, cs, sn, sign):
    outs = []
    for c in range(x.shape[1] // 128):
        xc = x[:, c * 128:(c + 1) * 128]
        outs.append(xc * cs + sign * (_rot_half(xc) * sn))
    return outs[0] if len(outs) == 1 else jnp.concatenate(outs, axis=1)


def rope_bwd(dq, dk, dv, cs, sn, name):
    T, D = dq.shape
    KV = dk.shape[1]
    tm = min(TM_ROW, T)

    def body(dq_ref, dk_ref, dv_ref, cs_ref, sn_ref, o_ref):
        cs_v, sn_v = cs_ref[...], sn_ref[...]
        o_ref[:, pl.ds(0, D)] = _rope_chunks(dq_ref[...], cs_v, sn_v, -1.0).astype(BF16)
        o_ref[:, pl.ds(D, KV)] = _rope_chunks(dk_ref[...], cs_v, sn_v, -1.0).astype(BF16)
        o_ref[:, pl.ds(D + KV, KV)] = dv_ref[...].astype(BF16)

    tab = pl.BlockSpec((tm, 128), lambda i: (i, 0))
    return pl.pallas_call(
        body, name=name, grid=(T // tm,),
        in_specs=[pl.BlockSpec((tm, D), lambda i: (i, 0)), pl.BlockSpec((tm, KV), lambda i: (i, 0)),
                  pl.BlockSpec((tm, KV), lambda i: (i, 0)), tab, tab],
        out_specs=pl.BlockSpec((tm, D + 2 * KV), lambda i: (i, 0)),
        out_shape=jax.ShapeDtypeStruct((T, D + 2 * KV), BF16),
        compiler_params=_params(("parallel",)),
    )(dq, dk, dv, cs, sn)


def _lane_lo():
    return lax.broadcasted_iota(jnp.int32, (1, 128), 1) < HEAD_DIM


def _band_mask(i, reps):
    shape = (reps * WINDOW, 2 * WINDOW)
    qi = lax.broadcasted_iota(jnp.int32, shape, 0) % WINDOW
    cj = lax.broadcasted_iota(jnp.int32, shape, 1)
    rel = qi - cj + WINDOW
    return (rel >= 0) & (rel < WINDOW) & ((i > 0) | (cj >= WINDOW))


def _stack_pairs(ref, first, n):
    parts = [ref[:, pl.ds((first + p) * 128, 128)] for p in range(n)]
    return parts[0] if n == 1 else jnp.concatenate(parts, axis=0)


def _pair_rows(n):
    return lax.broadcasted_iota(jnp.int32, (n * WINDOW, 1), 0) // WINDOW


def _per_pair_column(values, n):
    rows = _pair_rows(n)
    col = jnp.zeros((n * WINDOW, 1), F32) + values[0]
    for p in range(1, n):
        col = jnp.where(rows == p, values[p], col)
    return col


def _kv_lo_hi(x2, g):
    pair, half = divmod(g, 2)
    lo = _lane_lo()
    xg = x2[:, pair * 128:(pair + 1) * 128].astype(F32)
    xg = jnp.where(lo if half == 0 else ~lo, xg, 0.0)
    sw = pltpu.roll(xg, HEAD_DIM, 1)
    x_lo, x_hi = (xg, sw) if half == 0 else (sw, xg)
    return x_lo.astype(BF16), x_hi.astype(BF16)


def _softmax_sink(s, allowed, sink):
    s = jnp.where(allowed, s * (HEAD_DIM ** -0.5), NEG_INF)
    m = jnp.maximum(jnp.max(s, axis=-1, keepdims=True), sink)
    p = jnp.exp(s - m)
    es = jnp.exp(sink - m)
    inv = 1.0 / (jnp.sum(p, axis=-1, keepdims=True) + es)
    return p * inv, es * inv


def attn_fwd(qr, kr, vb, sinks, name, side=None):
    T, D = qr.shape
    KV = kr.shape[1]
    n_kv = KV // HEAD_DIM
    group = (D // HEAD_DIM) // n_kv
    nb = T // WINDOW

    npair = group // 2

    def body(sink_ref, q_ref, kp_ref, kc_ref, vp_ref, vc_ref, o_ref):
        i = pl.program_id(0)
        allowed = _band_mask(i, npair)
        k2 = jnp.concatenate([kp_ref[...], kc_ref[...]], axis=0)
        v2 = jnp.concatenate([vp_ref[...], vc_ref[...]], axis=0)
        outs = [None] * (D // 128)
        for g in range(n_kv):
            k_lo, k_hi = _kv_lo_hi(k2, g)
            v_lo, v_hi = _kv_lo_hi(v2, g)
            first = (g * group) // 2
            q = _stack_pairs(q_ref, first, npair)
            sink_e = _per_pair_column([sink_ref[0, g * group + 2 * p] for p in range(npair)], npair)
            sink_o = _per_pair_column([sink_ref[0, g * group + 2 * p + 1] for p in range(npair)], npair)
            pe, _ = _softmax_sink(_dot_nt(q, k_lo), allowed, sink_e)
            po, _ = _softmax_sink(_dot_nt(q, k_hi), allowed, sink_o)
            o = _dot(pe.astype(BF16), v_lo) + _dot(po.astype(BF16), v_hi)
            for p in range(npair):
                outs[first + p] = o[p * WINDOW:(p + 1) * WINDOW]
        o_ref[...] = jnp.concatenate(outs, axis=1).astype(BF16)

    prev = lambda i: (jnp.maximum(i - 1, 0), 0)
    cur = lambda i: (i, 0)
    return _call_hosting(
        body, side, name=name, grid=(nb,),
        in_specs=[pl.BlockSpec(memory_space=pltpu.SMEM),
                  pl.BlockSpec((WINDOW, D), cur),
                  pl.BlockSpec((WINDOW, KV), prev), pl.BlockSpec((WINDOW, KV), cur),
                  pl.BlockSpec((WINDOW, KV), prev), pl.BlockSpec((WINDOW, KV), cur)],
        out_specs=[pl.BlockSpec((WINDOW, D), cur)],
        out_shape=[jax.ShapeDtypeStruct((T, D), BF16)],
        scratch_shapes=[], operands=(sinks, qr, kr, kr, vb, vb))


def attn_bwd(qr, kr, vb, o, do, sinks, name, side=None):
    T, D = qr.shape
    KV = kr.shape[1]
    n_heads = D // HEAD_DIM
    n_kv = KV // HEAD_DIM
    group = n_heads // n_kv
    nb = T // WINDOW
    npair = group // 2
    scale = HEAD_DIM ** -0.5

    def body(sink_ref, q_ref, kp_ref, kc_ref, vp_ref, vc_ref, o_ref, do_ref,
             dq_ref, dk_ref, dv_ref, ds_ref, ck_ref, cv_ref):
        i = pl.program_id(0)
        lo = _lane_lo()

        @pl.when(i == 0)
        def _():
            ck_ref[...] = jnp.zeros_like(ck_ref)
            cv_ref[...] = jnp.zeros_like(cv_ref)
            ds_ref[...] = jnp.zeros_like(ds_ref)

        @pl.when(i < nb)
        def _():
            allowed = _band_mask(i, npair)
            rows = _pair_rows(npair)
            k2 = jnp.concatenate([kp_ref[...], kc_ref[...]], axis=0)
            v2 = jnp.concatenate([vp_ref[...], vc_ref[...]], axis=0)
            lane = lax.broadcasted_iota(jnp.int32, (1, 128), 1)
            dsink = jnp.zeros((1, 128), F32)
            dq_out = [None] * (D // 128)
            dk_pairs = [jnp.zeros((2 * WINDOW, 128), F32) for _ in range(KV // 128)]
            dv_pairs = [jnp.zeros((2 * WINDOW, 128), F32) for _ in range(KV // 128)]
            for g in range(n_kv):
                k_lo, k_hi = _kv_lo_hi(k2, g)
                v_lo, v_hi = _kv_lo_hi(v2, g)
                first = (g * group) // 2
                q = _stack_pairs(q_ref, first, npair)
                dop = _stack_pairs(do_ref, first, npair)
                dd = dop.astype(F32) * _stack_pairs(o_ref, first, npair).astype(F32)
                dq = jnp.zeros((npair * WINDOW, 128), F32)
                dkg = jnp.zeros((2 * WINDOW, 128), F32)
                dvg = jnp.zeros((2 * WINDOW, 128), F32)
                for parity, k_h, v_h, sel in ((0, k_lo, v_lo, lo), (1, k_hi, v_hi, ~lo)):
                    heads = [g * group + 2 * p + parity for p in range(npair)]
                    sink = _per_pair_column([sink_ref[0, h] for h in heads], npair)
                    p_, ps = _softmax_sink(_dot_nt(q, k_h), allowed, sink)
                    delta = jnp.sum(jnp.where(sel, dd, 0.0), axis=-1, keepdims=True)
                    dsc = (p_ * (_dot_nt(dop, v_h) - delta)).astype(BF16)
                    sd = -ps * delta
                    for p, h in enumerate(heads):
                        dsink += jnp.where(lane == h, jnp.sum(jnp.where(rows == p, sd, 0.0)), 0.0)
                    dq += _dot(dsc, k_h)
                    dkg += jnp.where(sel, _dot_tn(dsc, q), 0.0)
                    dvg += jnp.where(sel, _dot_tn(p_.astype(BF16), dop), 0.0)
                for p in range(npair):
                    dq_out[first + p] = dq[p * WINDOW:(p + 1) * WINDOW]
                pair, half = divmod(g, 2)
                keep = lo if half == 0 else ~lo
                dk_pairs[pair] += jnp.where(keep, dkg + pltpu.roll(dkg, HEAD_DIM, 1), 0.0) * scale
                dv_pairs[pair] += jnp.where(keep, dvg + pltpu.roll(dvg, HEAD_DIM, 1), 0.0)
            dq_ref[...] = jnp.concatenate(dq_out, axis=1) * scale
            dk2 = dk_pairs[0] if len(dk_pairs) == 1 else jnp.concatenate(dk_pairs, axis=1)
            dv2 = dv_pairs[0] if len(dv_pairs) == 1 else jnp.concatenate(dv_pairs, axis=1)
            dk_ref[...] = ck_ref[...] + dk2[:WINDOW]
            dv_ref[...] = cv_ref[...] + dv2[:WINDOW]
            ck_ref[...] = dk2[WINDOW:]
            cv_ref[...] = dv2[WINDOW:]
            ds_ref[pl.ds(0, 1), :] += dsink

        @pl.when(i == nb)
        def _():
            dk_ref[...] = ck_ref[...]
            dv_ref[...] = cv_ref[...]

    prev = lambda i: (jnp.maximum(i - 1, 0), 0)
    cur = lambda i: (jnp.minimum(i, nb - 1), 0)
    prevc = lambda i: (jnp.maximum(jnp.minimum(i, nb - 1) - 1, 0), 0)
    return _call_hosting(
        body, side, name=name, grid=(nb + 1,),
        in_specs=[pl.BlockSpec(memory_space=pltpu.SMEM),
                  pl.BlockSpec((WINDOW, D), cur),
                  pl.BlockSpec((WINDOW, KV), prevc), pl.BlockSpec((WINDOW, KV), cur),
                  pl.BlockSpec((WINDOW, KV), prevc), pl.BlockSpec((WINDOW, KV), cur),
                  pl.BlockSpec((WINDOW, D), cur), pl.BlockSpec((WINDOW, D), cur)],
        out_specs=[pl.BlockSpec((WINDOW, D), cur), pl.BlockSpec((WINDOW, KV), prev),
                   pl.BlockSpec((WINDOW, KV), prev), pl.BlockSpec((8, 128), lambda i: (0, 0))],
        out_shape=[jax.ShapeDtypeStruct((T, D), F32), jax.ShapeDtypeStruct((T, KV), F32),
                   jax.ShapeDtypeStruct((T, KV), F32), jax.ShapeDtypeStruct((8, 128), F32)],
        scratch_shapes=[pltpu.VMEM((WINDOW, KV), F32), pltpu.VMEM((WINDOW, KV), F32)],
        operands=(sinks, qr, kr, kr, vb, vb, o, do))


def merge_fwd(x, c3, o, p_gate, gate_b, w_proj, w_o, w_out, name):
    T, D = x.shape
    tm = min(TM_ROW, T)

    def body(x_ref, c3_ref, o_ref, gc_ref, ga_ref, bc_ref, ba_ref, wp_ref, wo_ref, wout_ref,
             xo_ref, co_ref, ao_ref, mg_ref):
        conv_out = _dot(c3_ref[...], wp_ref[...])
        attn_out = _dot(o_ref[...], wo_ref[...])
        merged = (_sigmoid(gc_ref[...] + bc_ref[...]) * conv_out
                  + _sigmoid(ga_ref[...] + ba_ref[...]) * attn_out).astype(BF16)
        co_ref[...] = conv_out.astype(BF16)
        ao_ref[...] = attn_out.astype(BF16)
        mg_ref[...] = merged
        xo_ref[...] = x_ref[...] + _dot(merged, wout_ref[...])

    blk = lambda j: pl.BlockSpec((tm, D), lambda i: (i, j))
    row = lambda j: pl.BlockSpec((1, D), lambda i: (0, j))
    mat = pl.BlockSpec((D, D), lambda i: (0, 0))
    return pl.pallas_call(
        body, name=name, grid=(T // tm,),
        in_specs=[blk(0), blk(0), blk(0), blk(0), blk(1), row(0), row(1), mat, mat, mat],
        out_specs=[blk(0), blk(0), blk(0), blk(0)],
        out_shape=[jax.ShapeDtypeStruct((T, D), F32)] + [jax.ShapeDtypeStruct((T, D), BF16)] * 3,
        compiler_params=_params(("parallel",)),
    )(x, c3, o, p_gate, p_gate, gate_b, gate_b, w_proj, w_o, w_out)


def merge_bwd(dx, p_gate, gate_b, conv_out, attn_out, c1, ln_g, ln_b, w_proj, w_o, w_out, name, side=None):
    T, D = dx.shape
    tm = min(TM_ROW, T)

    def body(dx_ref, gc_ref, ga_ref, bc_ref, ba_ref, co_ref, ao_ref, c1_ref, g_ref, be_ref,
             wp_ref, wo_ref, wout_ref, dgt_ref, dco_ref, dao_ref, do_ref, dc1_ref, sm_ref):
        @pl.when(pl.program_id(0) == 0)
        def _():
            sm_ref[...] = jnp.zeros_like(sm_ref)

        dm = _dot_nt(dx_ref[...].astype(BF16), wout_ref[...])
        sc = _sigmoid(gc_ref[...] + bc_ref[...])
        sa = _sigmoid(ga_ref[...] + ba_ref[...])
        dco = (dm * sc).astype(BF16)
        dao = (dm * sa).astype(BF16)
        dgc = dm * co_ref[...].astype(F32) * sc * (1.0 - sc)
        dga = dm * ao_ref[...].astype(F32) * sa * (1.0 - sa)
        dgt_ref[:, pl.ds(0, D)] = dgc.astype(BF16)
        dgt_ref[:, pl.ds(D, D)] = dga.astype(BF16)
        dco_ref[...] = dco
        dao_ref[...] = dao
        do_ref[...] = _dot_nt(dao, wo_ref[...]).astype(BF16)
        dc3 = _dot_nt(dco, wp_ref[...])
        xhat, rstd = _layernorm_stats(c1_ref[...])
        c2 = xhat * g_ref[...] + be_ref[...]
        dc2 = dc3 * _silu_grad(c2, _sigmoid(c2))
        dxh = dc2 * g_ref[...]
        dc1 = rstd * (dxh - jnp.mean(dxh, axis=-1, keepdims=True)
                      - xhat * jnp.mean(dxh * xhat, axis=-1, keepdims=True))
        dc1_ref[...] = dc1
        colsum = lambda v: jnp.sum(v, axis=0, keepdims=True)
        for r, (left, right) in enumerate(((dgc, dga), (dc2 * xhat, dc2), (dc1, None))):
            sm_ref[pl.ds(r, 1), pl.ds(0, D)] += colsum(left)
            if right is not None:
                sm_ref[pl.ds(r, 1), pl.ds(D, D)] += colsum(right)

    blk = lambda j: pl.BlockSpec((tm, D), lambda i: (i, j))
    row = lambda j: pl.BlockSpec((1, D), lambda i: (0, j))
    mat = pl.BlockSpec((D, D), lambda i: (0, 0))
    return _call_hosting(
        body, side, name=name, grid=(T // tm,),
        in_specs=[blk(0), blk(0), blk(1), row(0), row(1), blk(0), blk(0), blk(0), row(0), row(0), mat, mat, mat],
        out_specs=[pl.BlockSpec((tm, 2 * D), lambda i: (i, 0)), blk(0), blk(0), blk(0), blk(0),
                   pl.BlockSpec((8, 2 * D), lambda i: (0, 0))],
        out_shape=[jax.ShapeDtypeStruct((T, 2 * D), BF16)] + [jax.ShapeDtypeStruct((T, D), BF16)] * 3
                  + [jax.ShapeDtypeStruct((T, D), F32), jax.ShapeDtypeStruct((8, 2 * D), F32)],
        scratch_shapes=[],
        operands=(dx, p_gate, p_gate, gate_b, gate_b, conv_out, attn_out, c1, ln_g, ln_b, w_proj, w_o, w_out))


def loss_head(x, nw, target, name):
    T, D = x.shape
    tm = min(TM_ROW, T)

    def body(x_ref, nw_ref, t_ref, dx_ref, sm_ref):
        @pl.when(pl.program_id(0) == 0)
        def _():
            sm_ref[...] = jnp.zeros_like(sm_ref)

        xv = x_ref[...]
        err = xv * _rms_scale(xv) * nw_ref[...] - t_ref[...]
        loss = 0.5 * jnp.sum(jnp.mean(err * err, axis=-1, keepdims=True))
        dxn, dnw = _rms_bwd(xv, nw_ref[...], err * (1.0 / D))
        dx_ref[...] = dxn
        sm_ref[pl.ds(0, 1), :] += dnw
        sm_ref[pl.ds(1, 1), :] += jnp.zeros((1, D), F32) + loss

    return pl.pallas_call(
        body, name=name, grid=(T // tm,),
        in_specs=[pl.BlockSpec((tm, D), lambda i: (i, 0)), pl.BlockSpec((1, D), lambda i: (0, 0)),
                  pl.BlockSpec((tm, D), lambda i: (i, 0))],
        out_specs=[pl.BlockSpec((tm, D), lambda i: (i, 0)), pl.BlockSpec((8, D), lambda i: (0, 0))],
        out_shape=[jax.ShapeDtypeStruct((T, D), F32), jax.ShapeDtypeStruct((8, D), F32)],
        compiler_params=_params(("arbitrary",)),
    )(x, nw, target)


def adamw(w, g, m, v, name):
    R, C = w.shape
    tr = _row_tile(R, TR_ELT)

    def body(w_ref, g_ref, m_ref, v_ref, d_ref, mo_ref, vo_ref):
        gv = g_ref[...]
        mn = ADAM_B1 * m_ref[...] + (1.0 - ADAM_B1) * gv
        vn = ADAM_B2 * v_ref[...] + (1.0 - ADAM_B2) * (gv * gv)
        m_hat = mn / (1.0 - ADAM_B1 ** ADAM_STEP)
        v_hat = vn / (1.0 - ADAM_B2 ** ADAM_STEP)
        d_ref[...] = -ADAM_LR * (m_hat / (jnp.sqrt(v_hat) + ADAM_EPS) + ADAM_WD * w_ref[...])
        mo_ref[...] = mn
        vo_ref[...] = vn

    spec = pl.BlockSpec((tr, C), lambda i: (i, 0))
    return pl.pallas_call(
        body, name=name, grid=(R // tr,), in_specs=[spec] * 4, out_specs=[spec] * 3,
        out_shape=[jax.ShapeDtypeStruct((R, C), F32)] * 3,
        compiler_params=_params(("parallel",)),
    )(w, g, m, v)


def _place():
    return lax.axis_index("x"), lax.axis_index("y"), lax.axis_index("c")


def place_shard(place, w, dtype, name):
    R, C = w.shape
    tr = _row_tile(R, TR_ELT)

    def body(pc_ref, w_ref, o_ref):
        o_ref[...] = w_ref[...].astype(dtype)

    return pl.pallas_call(
        body, name=name,
        grid_spec=pltpu.PrefetchScalarGridSpec(
            num_scalar_prefetch=1, grid=(R // tr,),
            in_specs=[pl.BlockSpec((tr, C), lambda r, pc: (r, 0))],
            out_specs=pl.BlockSpec((None, tr, C), lambda r, pc: (pc[0], r, 0))),
        out_shape=jax.ShapeDtypeStruct((N_CHIPS, R, C), dtype),
        compiler_params=_params(("arbitrary",)),
    )(place, w)


def gather_side(shards, small):
    n, ns = len(shards), len(small)

    def ici_copy(dst, sems, k, j, x, y, c, sending):
        px, py = x ^ (j >> 1), y ^ (j & 1)
        slot = 2 * x + y if sending else 2 * px + py
        half = dst[k].shape[1] // 2
        part = dst[k].at[slot, pl.ds(c * half, half)] if k < n else dst[k].at[slot]
        return pltpu.make_async_remote_copy(part, part, sems[0].at[3 * k + j - 1], sems[1].at[3 * k + j - 1],
                                            device_id=(px, py, c), device_id_type=MESH)

    def d2d_copy(dst, sems, k, j, x, y, c, sending):
        half = dst[k].shape[1] // 2
        part = dst[k].at[2 * (x ^ (j >> 1)) + (y ^ (j & 1)), pl.ds((c if sending else 1 - c) * half, half)]
        return pltpu.make_async_remote_copy(part, part, sems[2].at[3 * k + j - 1], sems[3].at[3 * k + j - 1],
                                            device_id=(x, y, 1 - c), device_id_type=MESH)

    def start(src, dst, sems):
        x, y, c = _place()
        for k in range(n + ns):
            for j in (1, 2, 3):
                ici_copy(dst, sems, k, j, x, y, c, True).start()

    def relay(src, dst, sems):
        x, y, c = _place()
        for k in range(n + ns):
            for j in (1, 2, 3):
                ici_copy(dst, sems, k, j, x, y, c, False).wait_recv()
                if k < n:
                    d2d_copy(dst, sems, k, j, x, y, c, True).start()

    def finish(src, dst, sems):
        x, y, c = _place()
        for k in range(n):
            for j in (1, 2, 3):
                d2d_copy(dst, sems, k, j, x, y, c, False).wait_recv()
        for k in range(n + ns):
            for j in (1, 2, 3):
                ici_copy(dst, sems, k, j, x, y, c, True).wait_send()
                if k < n:
                    d2d_copy(dst, sems, k, j, x, y, c, True).wait_send()

    arrays = list(shards) + list(small)
    return dict(inputs=arrays, out_shapes=[jax.ShapeDtypeStruct(a.shape, a.dtype) for a in arrays],
                aliases={k: k for k in range(n + ns)},
                sems=[pltpu.SemaphoreType.DMA((3 * (n + ns),)), pltpu.SemaphoreType.DMA((3 * (n + ns),)),
                      pltpu.SemaphoreType.DMA((3 * n,)), pltpu.SemaphoreType.DMA((3 * n,))],
                start=start, relay=relay, finish=finish)


def run_side(side, name):
    n_in, n_out = len(side["inputs"]), len(side["out_shapes"])

    def body(*refs):
        src, dst, sems = refs[:n_in], refs[n_in:n_in + n_out], refs[n_in + n_out:]
        side["start"](src, dst, sems)
        if "relay" in side:
            side["relay"](src, dst, sems)
        side["finish"](src, dst, sems)

    return pl.pallas_call(
        body, name=name, in_specs=[HBM_SPEC] * n_in, out_specs=[HBM_SPEC] * n_out,
        out_shape=side["out_shapes"], input_output_aliases=side["aliases"], scratch_shapes=side["sems"],
    )(*side["inputs"])


def allreduce_small(block):
    R, C = block.shape

    def body(x_ref, out_ref, all_ref, send_sems, recv_sems, local_sem):
        x, y, c = _place()
        me, sibling = (x, y, c), (x, y, 1 - c)
        chips = [(1 - x, y), (x, 1 - y), (1 - x, 1 - y)]

        def slot(px, py, pc):
            return all_ref.at[4 * px + 2 * py + pc]

        def copy(k, block_of, to, src=None):
            return pltpu.make_async_remote_copy(
                src_ref=slot(*block_of) if src is None else src, dst_ref=slot(*block_of),
                send_sem=send_sems.at[k], recv_sem=recv_sems.at[k], device_id=to, device_id_type=MESH)

        mine = pltpu.make_async_copy(x_ref, slot(*me), local_sem)
        mine.start()
        first = [copy(0, me, sibling, src=x_ref)]
        first += [copy(1 + j, me, (*chip, c), src=x_ref) for j, chip in enumerate(chips)]
        for cp in first:
            cp.start()
        passed = [copy(4 + j, (*chip, c), sibling) for j, chip in enumerate(chips)]
        for j, chip in enumerate(chips):
            copy(1 + j, (*chip, c), me).wait_recv()
            passed[j].start()
        copy(0, sibling, me).wait_recv()
        for j, chip in enumerate(chips):
            copy(4 + j, (*chip, 1 - c), me).wait_recv()
        for cp in first + passed:
            cp.wait_send()
        mine.wait()
        total = all_ref[0]
        for d in range(1, N_DEV):
            total = total + all_ref[d]
        out_ref[...] = total

    return pl.pallas_call(
        body, name="allreduce_small",
        in_specs=[pl.BlockSpec(memory_space=pltpu.VMEM)], out_specs=pl.BlockSpec(memory_space=pltpu.VMEM),
        out_shape=jax.ShapeDtypeStruct((R, C), F32),
        scratch_shapes=[pltpu.VMEM((N_DEV, R, C), F32), pltpu.SemaphoreType.DMA((7,)),
                        pltpu.SemaphoreType.DMA((7,)), pltpu.SemaphoreType.DMA],
        compiler_params=pltpu.CompilerParams(vmem_limit_bytes=VMEM_LIMIT),
    )(block)


def exchange_siblings_side(grads):
    n = len(grads)

    def copies(src, dst, sems):
        x, y, c = _place()
        for k in range(n):
            half = src[k].shape[1] // 2
            yield pltpu.make_async_remote_copy(src[k].at[:, pl.ds((1 - c) * half, half)], dst[k],
                                               sems[0].at[k], sems[1].at[k],
                                               device_id=(x, y, 1 - c), device_id_type=MESH)

    def start(src, dst, sems):
        for cp in copies(src, dst, sems):
            cp.start()

    def finish(src, dst, sems):
        for cp in copies(src, dst, sems):
            cp.wait()

    return dict(inputs=list(grads), aliases={},
                out_shapes=[jax.ShapeDtypeStruct((N_CHIPS, g.shape[1] // 2, g.shape[2]), F32) for g in grads],
                sems=[pltpu.SemaphoreType.DMA((n,)), pltpu.SemaphoreType.DMA((n,))], start=start, finish=finish)


def rs_chip_sum(place, grad, sib, name):
    NP, R, C = grad.shape
    half = R // 2
    tr = _row_tile(half, TR_ELT)
    nr = half // tr

    def body(pc_ref, g_ref, s_ref, wire_ref, own_ref):
        q = pl.program_id(1)
        total = g_ref[...] + s_ref[...]
        wire_ref[...] = total.astype(BF16)

        @pl.when(q == pc_ref[0])
        def _():
            own_ref[...] = total

    return pl.pallas_call(
        body, name=name,
        grid_spec=pltpu.PrefetchScalarGridSpec(
            num_scalar_prefetch=1, grid=(nr, NP),
            in_specs=[pl.BlockSpec((None, tr, C), lambda r, q, pc: (q, pc[1] * nr + r, 0)),
                      pl.BlockSpec((None, tr, C), lambda r, q, pc: (q, r, 0))],
            out_specs=[pl.BlockSpec((None, tr, C), lambda r, q, pc: (q, r, 0)),
                       pl.BlockSpec((tr, C), lambda r, q, pc: (r, 0))]),
        out_shape=[jax.ShapeDtypeStruct((NP, half, C), BF16), jax.ShapeDtypeStruct((half, C), F32)],
        compiler_params=_params(("arbitrary", "arbitrary")),
    )(place, grad, sib)


def exchange_chips_side(wires):
    n = len(wires)

    def copies(src, dst, sems):
        x, y, c = _place()
        for k in range(n):
            for j in (1, 2, 3):
                qx, qy = x ^ (j >> 1), y ^ (j & 1)
                yield pltpu.make_async_remote_copy(src[k].at[2 * qx + qy], dst[k].at[2 * x + y],
                                                   sems[0].at[3 * k + j - 1], sems[1].at[3 * k + j - 1],
                                                   device_id=(qx, qy, c), device_id_type=MESH)

    def start(src, dst, sems):
        for cp in copies(src, dst, sems):
            cp.start()

    def finish(src, dst, sems):
        for cp in copies(src, dst, sems):
            cp.wait()

    return dict(inputs=list(wires), out_shapes=[jax.ShapeDtypeStruct(w.shape, BF16) for w in wires], aliases={},
                sems=[pltpu.SemaphoreType.DMA((3 * n,)), pltpu.SemaphoreType.DMA((3 * n,))],
                start=start, finish=finish)


SEM_SPEC = pl.BlockSpec(memory_space=pltpu.SEMAPHORE)


def exchange_chips_start(wires, name):
    n = len(wires)
    side = exchange_chips_side(wires)

    def body(*refs):
        src, land, sems = refs[:n], refs[n:2 * n], refs[2 * n:2 * n + 2]
        side["start"](src, land, sems)
        refs[-1][...] = jnp.zeros_like(refs[-1])

    hbm = [pltpu.HBM(w.shape, w.dtype) for w in wires]
    outs = pl.pallas_call(
        body, name=name, in_specs=[HBM_SPEC] * (2 * n),
        out_specs=[SEM_SPEC, SEM_SPEC] + [HBM_SPEC] * (2 * n) + [pl.BlockSpec(memory_space=pltpu.VMEM)],
        out_shape=list(side["sems"]) + hbm + hbm + [jax.ShapeDtypeStruct((8, 128), F32)],
        input_output_aliases={k: 2 + k for k in range(2 * n)},
        compiler_params=pltpu.CompilerParams(has_side_effects=pltpu.SideEffectType.DATAFLOW_SIDE_EFFECTING),
    )(*[pltpu.with_memory_space_constraint(w, pltpu.HBM) for w in wires],
      *[pltpu.with_memory_space_constraint(lax.empty(w.shape, w.dtype), pltpu.HBM) for w in wires])
    return outs[0], outs[1], outs[2:2 + n], outs[2 + n:2 + 2 * n], outs[-1]


def exchange_chips_wait(send_sems, recv_sems, wires, lands, after, name):
    n = len(wires)
    side = exchange_chips_side(wires)

    def body(*refs):
        side["finish"](refs[:n], refs[n:2 * n], refs[2 * n:2 * n + 2])

    hbm = [pltpu.HBM(w.shape, w.dtype) for w in wires]
    outs = pl.pallas_call(
        body, name=name, in_specs=[HBM_SPEC] * (2 * n) + [SEM_SPEC, SEM_SPEC] + [HBM_SPEC] * len(after),
        out_specs=[HBM_SPEC] * (2 * n), out_shape=hbm + hbm,
        input_output_aliases={k: k for k in range(2 * n)},
        compiler_params=pltpu.CompilerParams(has_side_effects=pltpu.SideEffectType.DATAFLOW_SIDE_EFFECTING),
    )(*wires, *lands, send_sems, recv_sems, *after)
    return outs[n:]


def rs_final_sum(place, own, got, after, name):
    NP, half, C = got.shape
    tr = _row_tile(half, TR_ELT)
    nr = half // tr

    def body(pc_ref, own_ref, g1_ref, g2_ref, g3_ref, after_ref, out_ref):
        out_ref[...] = ((own_ref[...] + g1_ref[...].astype(F32)) + g2_ref[...].astype(F32)) + g3_ref[...].astype(F32)

    slot = lambda j: pl.BlockSpec((None, tr, C), lambda r, pc: (pc[0] ^ j, r, 0))
    return pl.pallas_call(
        body, name=name,
        grid_spec=pltpu.PrefetchScalarGridSpec(
            num_scalar_prefetch=1, grid=(nr,),
            in_specs=[pl.BlockSpec((tr, C), lambda r, pc: (r, 0)), slot(1), slot(2), slot(3),
                      pl.BlockSpec((8, 128), lambda r, pc: (0, 0))],
            out_specs=pl.BlockSpec((tr, C), lambda r, pc: (pc[1] * nr + r, 0))),
        out_shape=jax.ShapeDtypeStruct((2 * half, C), F32),
        compiler_params=_params(("arbitrary",)),
    )(place, own, got, got, got, after)


def rs_share_siblings(totals, name):
    n = len(totals)

    def body(*refs):
        dst = refs[n:2 * n]
        send_sems, recv_sems = refs[2 * n:]
        x, y, c = _place()
        copies = []
        for k in range(n):
            half = dst[k].shape[0] // 2
            rows = dst[k].at[pl.ds(c * half, half)]
            cp = pltpu.make_async_remote_copy(rows, rows, send_sems.at[k], recv_sems.at[k],
                                              device_id=(x, y, 1 - c), device_id_type=MESH)
            cp.start()
            copies.append(cp)
        for k, cp in enumerate(copies):
            cp.wait_send()
            half = dst[k].shape[0] // 2
            got = dst[k].at[pl.ds((1 - c) * half, half)]
            pltpu.make_async_remote_copy(got, got, send_sems.at[k], recv_sems.at[k],
                                         device_id=(x, y, c), device_id_type=MESH).wait_recv()

    return pl.pallas_call(
        body, name=name,
        in_specs=[HBM_SPEC] * n, out_specs=[HBM_SPEC] * n,
        out_shape=[jax.ShapeDtypeStruct(t.shape, F32) for t in totals],
        input_output_aliases={k: k for k in range(n)},
        scratch_shapes=[pltpu.SemaphoreType.DMA((n,)), pltpu.SemaphoreType.DMA((n,))],
    )(*totals)


def rs_to_wires(place, grads, tag, sibs=None):
    if sibs is None:
        sibs = run_side(exchange_siblings_side(grads), f"rs_exchange_siblings_{tag}")
    wires, owns = [], []
    for k, (g, s) in enumerate(zip(grads, sibs)):
        w, o = rs_chip_sum(place, g, s, f"rs_chip_sum_{tag}{k}")
        wires.append(w)
        owns.append(o)
    return wires, owns


def rs_finish(place, owns, gots, after, tag):
    totals = [rs_final_sum(place, o, g, after, f"rs_final_sum_{tag}{k}") for k, (o, g) in enumerate(zip(owns, gots))]
    return rs_share_siblings(totals, f"rs_share_siblings_{tag}")


def _rope_tables(positions):
    half = HEAD_DIM // 2
    inv_freq = ROPE_THETA ** (-jnp.arange(half, dtype=F32) / half)
    ang = positions.astype(F32)[:, None] * inv_freq
    lanes = jnp.arange(128)
    spread = (lanes[None, :] % half == jnp.arange(half)[:, None]).astype(F32)
    signed = spread * jnp.where(lanes % HEAD_DIM < half, -1.0, 1.0).astype(F32)
    exact = lax.Precision.HIGHEST
    return jnp.dot(jnp.cos(ang), spread, precision=exact), jnp.dot(jnp.sin(ang), signed, precision=exact)


def _cols_from_pieces(pieces, start, stop):
    C = pieces.shape[2]
    parts = []
    for q in range(N_CHIPS):
        lo, hi = max(start, q * C), min(stop, (q + 1) * C)
        if lo < hi:
            parts.append(pieces[q][:, lo - q * C:hi - q * C])
    return parts[0] if len(parts) == 1 else jnp.concatenate(parts, axis=1)


def _pieces_from_groups(groups):
    C = sum(g.shape[1] for g in groups) // N_CHIPS
    pieces = []
    for q in range(N_CHIPS):
        parts, off = [], 0
        for g in groups:
            lo, hi = max(q * C, off), min((q + 1) * C, off + g.shape[1])
            if lo < hi:
                parts.append(g[:, lo - off:hi - off])
            off += g.shape[1]
        pieces.append(parts[0] if len(parts) == 1 else jnp.concatenate(parts, axis=1))
    return jnp.stack(pieces)


def kernel(x, positions, ffn1_norm, ffn1_w_gate, ffn1_w_up, ffn1_w_down, mix_norm, w_in, conv_dw_w, conv_dw_b, conv_ln_g, conv_ln_b, conv_w_proj, attn_sinks, attn_w_o, gate_b, w_out, ffn2_norm, ffn2_w_gate, ffn2_w_up, ffn2_w_down, final_norm, loss_target, m_ffn1_norm, m_ffn1_w_gate, m_ffn1_w_up, m_ffn1_w_down, m_mix_norm, m_w_in, m_conv_dw_w, m_conv_dw_b, m_conv_ln_g, m_conv_ln_b, m_conv_w_proj, m_attn_sinks, m_attn_w_o, m_gate_b, m_w_out, m_ffn2_norm, m_ffn2_w_gate, m_ffn2_w_up, m_ffn2_w_down, m_final_norm, v_ffn1_norm, v_ffn1_w_gate, v_ffn1_w_up, v_ffn1_w_down, v_mix_norm, v_w_in, v_conv_dw_w, v_conv_dw_b, v_conv_ln_g, v_conv_ln_b, v_conv_w_proj, v_attn_sinks, v_attn_w_o, v_gate_b, v_w_out, v_ffn2_norm, v_ffn2_w_gate, v_ffn2_w_up, v_ffn2_w_down, v_final_norm):
    weights = dict(ffn1_norm=ffn1_norm, ffn1_w_gate=ffn1_w_gate, ffn1_w_up=ffn1_w_up, ffn1_w_down=ffn1_w_down,
                   mix_norm=mix_norm, w_in=w_in, conv_dw_w=conv_dw_w, conv_dw_b=conv_dw_b, conv_ln_g=conv_ln_g,
                   conv_ln_b=conv_ln_b, conv_w_proj=conv_w_proj, attn_sinks=attn_sinks, attn_w_o=attn_w_o,
                   gate_b=gate_b, w_out=w_out, ffn2_norm=ffn2_norm, ffn2_w_gate=ffn2_w_gate, ffn2_w_up=ffn2_w_up,
                   ffn2_w_down=ffn2_w_down, final_norm=final_norm)
    m_in = dict(ffn1_norm=m_ffn1_norm, ffn1_w_gate=m_ffn1_w_gate, ffn1_w_up=m_ffn1_w_up, ffn1_w_down=m_ffn1_w_down,
                mix_norm=m_mix_norm, w_in=m_w_in, conv_dw_w=m_conv_dw_w, conv_dw_b=m_conv_dw_b,
                conv_ln_g=m_conv_ln_g, conv_ln_b=m_conv_ln_b, conv_w_proj=m_conv_w_proj, attn_sinks=m_attn_sinks,
                attn_w_o=m_attn_w_o, gate_b=m_gate_b, w_out=m_w_out, ffn2_norm=m_ffn2_norm,
                ffn2_w_gate=m_ffn2_w_gate, ffn2_w_up=m_ffn2_w_up, ffn2_w_down=m_ffn2_w_down, final_norm=m_final_norm)
    v_in = dict(ffn1_norm=v_ffn1_norm, ffn1_w_gate=v_ffn1_w_gate, ffn1_w_up=v_ffn1_w_up, ffn1_w_down=v_ffn1_w_down,
                mix_norm=v_mix_norm, w_in=v_w_in, conv_dw_w=v_conv_dw_w, conv_dw_b=v_conv_dw_b,
                conv_ln_g=v_conv_ln_g, conv_ln_b=v_conv_ln_b, conv_w_proj=v_conv_w_proj, attn_sinks=v_attn_sinks,
                attn_w_o=v_attn_w_o, gate_b=v_gate_b, w_out=v_w_out, ffn2_norm=v_ffn2_norm,
                ffn2_w_gate=v_ffn2_w_gate, ffn2_w_up=v_ffn2_w_up, ffn2_w_down=v_ffn2_w_down, final_norm=v_final_norm)
    names = list(weights)
    big = ["ffn1_w_gate", "ffn1_w_up", "ffn1_w_down", "w_in", "conv_w_proj", "attn_w_o", "w_out",
           "ffn2_w_gate", "ffn2_w_up", "ffn2_w_down"]
    transposed = [k for k in big if k.endswith(("w_gate", "w_up"))]
    for k in transposed:
        weights[k], m_in[k], v_in[k] = (jnp.swapaxes(a, 1, 2) for a in (weights[k], m_in[k], v_in[k]))

    xs = x[0]
    T, D = xs.shape
    KV = (w_in.shape[2] * N_CHIPS - 5 * D) // 2
    n_heads = D // HEAD_DIM
    my_chip = 2 * lax.axis_index("x") + lax.axis_index("y")
    place = jnp.stack([my_chip, lax.axis_index("c")]).astype(jnp.int32)

    placed = {k: place_shard(place, weights[k][0], BF16, f"place_{k}") for k in big}
    placed_dw = place_shard(place, conv_dw_w[0], F32, "place_conv_dw_w")
    first, mixer_w, second = big[:3], big[3:7], big[7:]
    wg1, wu1, wd1 = run_side(gather_side([placed[k] for k in first], []), "gather_ffn1")
    x1, h1, g1, u1, *gathered = ffn_fwd(x[0], ffn1_norm, wg1, wu1, wd1, "ffn1_fwd",
                                        side=gather_side([placed[k] for k in mixer_w], [placed_dw]))
    full = dict(zip(mixer_w + ["conv_dw_w"], gathered))
    w_glu = _cols_from_pieces(full["w_in"], 0, 2 * D)
    w_qkv = _cols_from_pieces(full["w_in"], 2 * D, 3 * D + 2 * KV)
    w_gate = _cols_from_pieces(full["w_in"], 3 * D + 2 * KV, 5 * D + 2 * KV)
    w_proj = full["conv_w_proj"].reshape(D, D)
    w_o = full["attn_w_o"].reshape(D, D)
    w_out_f = full["w_out"].reshape(D, D)
    dw_w = full["conv_dw_w"].transpose(1, 0, 2).reshape(CONV_WIDTH, D)
    dw_w = jnp.concatenate([dw_w, jnp.zeros((CONV_HALO - CONV_WIDTH, D), F32)], axis=0)
    cs, sn = _rope_tables(positions[0])
    fn_row = final_norm.reshape(1, D)

    h2, p_glu, p_gate, qr, kr, vb = mix_in_fwd(x1, mix_norm, w_glu, w_qkv, w_gate, cs, sn, "mix_in_fwd")
    c1, c3 = conv_fwd(p_glu, dw_w, conv_dw_b, conv_ln_g, conv_ln_b, "conv_fwd")
    o, wg2, wu2, wd2 = attn_fwd(qr, kr, vb, attn_sinks, "attn_fwd",
                                side=gather_side([placed[k] for k in second], []))
    x2, conv_out, attn_out, merged = merge_fwd(x1, c3, o, p_gate, gate_b, w_proj, w_o, w_out_f, "merge_fwd")
    x3, h3, g2, u2 = ffn_fwd(x2, ffn2_norm, wg2, wu2, wd2, "ffn2_fwd")

    dx3, head_sums = loss_head(x3, fn_row, loss_target[0], "loss_head")
    dx2, dwg2, dwu2, dwd2, d_ffn2_norm = ffn_bwd(x2, ffn2_norm, h3, g2, u2, wg2, wu2, wd2, dx3, "ffn2_bwd")
    ffn2_grads = [dwg2, dwu2, dwd2]
    d_gates, d_conv_out, d_attn_out, d_o, dc1, merge_sums, *sibs_f2 = merge_bwd(
        dx2, p_gate, gate_b, conv_out, attn_out, c1, conv_ln_g, conv_ln_b, w_proj, w_o, w_out_f, "merge_bwd",
        side=exchange_siblings_side(ffn2_grads))
    d_w_out = matmul_tn(merged, dx2, "d_w_out")
    d_w_proj = matmul_tn(c3, d_conv_out, "d_conv_w_proj")
    d_w_o = matmul_tn(o, d_attn_out, "d_attn_w_o")
    wires_f2, owns_f2 = rs_to_wires(place, ffn2_grads, "ffn2", sibs=sibs_f2)
    d_glu, d_dw_w, *gots_f2 = conv_bwd(p_glu, dc1, dw_w, "conv_bwd", side=exchange_chips_side(wires_f2))
    dwc = D // N_CHIPS
    square_grads = [d_w_proj.reshape(N_CHIPS, dwc, D), d_w_o.reshape(N_CHIPS, dwc, D),
                    d_w_out.reshape(N_CHIPS, dwc, D)]
    dq, dk, dv, d_sinks, *sibs_sq = attn_bwd(qr, kr, vb, o, d_o, attn_sinks, "attn_bwd",
                                             side=exchange_siblings_side(square_grads))
    d_qkv = rope_bwd(dq, dk, dv, cs, sn, "rope_bwd")
    dx1, d_mix_norm = mix_in_bwd([d_glu, d_qkv, d_gates], [w_glu, w_qkv, w_gate], x1, mix_norm, dx2, "mix_in_bwd")
    d_w_in = _pieces_from_groups([matmul_tn(h2, d_glu, "d_w_in_glu"), matmul_tn(h2, d_qkv, "d_w_in_qkv"),
                                  matmul_tn(h2, d_gates, "d_w_in_gate")])
    sib_w_in = run_side(exchange_siblings_side([d_w_in]), "rs_exchange_siblings_w_in")
    wires_m, owns_m = rs_to_wires(place, [d_w_in] + square_grads, "mixer", sibs=list(sib_w_in) + list(sibs_sq))
    dx0, dwg1, dwu1, dwd1, d_ffn1_norm, *gots_m = ffn_bwd(xs, ffn1_norm, h1, g1, u1, wg1, wu1, wd1, dx1, "ffn1_bwd",
                                                          side=exchange_chips_side(wires_m))
    wires_l, owns_l = rs_to_wires(place, [dwg1, dwu1, dwd1], "ffn1")
    send_sems, recv_sems, wires_l, lands_l, token = exchange_chips_start(wires_l, "rs_exchange_chips_ffn1_start")
    early_names = ["ffn2_w_gate", "ffn2_w_up", "ffn2_w_down", "w_in", "conv_w_proj", "attn_w_o", "w_out"]
    late_names = ["ffn1_w_gate", "ffn1_w_up", "ffn1_w_down"]
    reduced_early = rs_finish(place, owns_f2 + owns_m, list(gots_f2) + list(gots_m), token, "early")

    pad_row = lambda v: jnp.pad(v, ((0, 0), (0, D - v.shape[1])))
    small_rows = jnp.concatenate([
        d_ffn1_norm, d_mix_norm, merge_sums[2:3, :D], merge_sums[1:2, :D], merge_sums[1:2, D:],
        pad_row(d_sinks[0:1, :n_heads]), merge_sums[0:1, :D], merge_sums[0:1, D:], d_ffn2_norm,
        head_sums[0:1], head_sums[1:2], jnp.zeros((5, D), F32), d_dw_w], axis=0)
    small = allreduce_small(small_rows)
    loss = small[10, 0]
    grads = {"ffn1_norm": small[0:1], "mix_norm": small[1:2], "conv_dw_b": small[2:3], "conv_ln_g": small[3:4],
             "conv_ln_b": small[4:5], "attn_sinks": small[5:6, :n_heads],
             "gate_b": jnp.concatenate([small[6:7], small[7:8]], axis=1), "ffn2_norm": small[8:9],
             "final_norm": small[9:10]}
    grads["conv_dw_w"] = lax.dynamic_slice(small[16:16 + CONV_WIDTH], (0, my_chip * dwc), (CONV_WIDTH, dwc))
    grads.update(zip(early_names, reduced_early))

    deltas, new_m, new_v = {}, {}, {}

    def apply_adamw(k):
        shape = weights[k].shape
        g2d = grads[k].reshape(-1, shape[-1])
        grads[k] = g2d.reshape(shape)
        d, mn, vn = adamw(weights[k].reshape(g2d.shape), g2d, m_in[k].reshape(g2d.shape),
                          v_in[k].reshape(g2d.shape), f"adamw_{k}")
        deltas[k], new_m[k], new_v[k] = d.reshape(shape), mn.reshape(shape), vn.reshape(shape)
        return d

    done = {k: apply_adamw(k) for k in names if k not in late_names}
    gots_l = exchange_chips_wait(send_sems, recv_sems, wires_l, lands_l, [done[k] for k in early_names],
                                 "rs_exchange_chips_ffn1_wait")
    grads.update(zip(late_names, rs_finish(place, owns_l, gots_l, token, "late")))
    for k in late_names:
        apply_adamw(k)
    for k in transposed:
        for group in (grads, deltas, new_m, new_v):
            group[k] = jnp.swapaxes(group[k], 1, 2)

    return (loss, dx0[None], *[grads[k] for k in names], *[deltas[k] for k in names],
            *[new_m[k] for k in names], *[new_v[k] for k in names])
```

```python
import functools

import jax
import jax.numpy as jnp
from jax import lax
from jax.experimental import pallas as pl
from jax.experimental.pallas import tpu as pltpu

F32 = jnp.float32
BF16 = jnp.bfloat16
MESH = pl.DeviceIdType.MESH

HEAD_DIM = 64
WINDOW = 128
CONV_WIDTH = 31
CONV_HALO = 32
ROPE_THETA = 10000.0
EPS = 1e-6
LN_EPS = 1e-5
NEG_INF = -1e30
N_CHIPS = 4
N_DEV = 8

ADAM_LR = 0.001
ADAM_B1 = 0.9
ADAM_B2 = 0.999
ADAM_EPS = 1e-08
ADAM_WD = 0.01
ADAM_STEP = 10

TM_FFN = 512
TM_FFN_FWD = 1024
TM_ROW = 256
TK_TN = 1024
TR_ELT = 256
VMEM_LIMIT = 56 * 1024 * 1024

NT_DIMS = (((1,), (1,)), ((), ()))
TN_DIMS = (((0,), (0,)), ((), ()))


def _row_tile(rows, cap):
    for t in range(min(cap, rows), 15, -1):
        if rows % t == 0 and t % 16 == 0:
            return t
    return rows


def _params(sem):
    return pltpu.CompilerParams(dimension_semantics=sem, vmem_limit_bytes=VMEM_LIMIT)


def _dot(a, b):
    return jnp.dot(a, b, preferred_element_type=F32)


def _dot_nt(a, b):
    return lax.dot_general(a, b, NT_DIMS, preferred_element_type=F32)


def _dot_tn(a, b):
    return lax.dot_general(a, b, TN_DIMS, preferred_element_type=F32)


def _split_rows(dot, a, b):
    m = a.shape[0] // 2
    return jnp.concatenate([dot(a[:m], b), dot(a[m:], b)], axis=0)


def _sigmoid(x):
    return jax.nn.sigmoid(x)


def _rms_scale(xv):
    return lax.rsqrt(jnp.mean(xv * xv, axis=-1, keepdims=True) + EPS)


def _rms_bwd(xv, nw, dh):
    r = _rms_scale(xv)
    dn = dh * nw
    dx = r * dn - xv * (r * r * r) * jnp.mean(dn * xv, axis=-1, keepdims=True)
    dnw = jnp.sum(dh * (xv * r), axis=0, keepdims=True)
    return dx, dnw


def _silu_grad(z, s):
    return s * (1.0 + z * (1.0 - s))


HBM_SPEC = pl.BlockSpec(memory_space=pl.ANY)


def _call_hosting(body, side, *, grid, in_specs, out_specs, out_shape, scratch_shapes, operands, name, aliases=None):
    params = _params(("arbitrary",) * len(grid))
    aliases = dict(aliases or {})
    if side is None:
        return pl.pallas_call(body, name=name, grid=grid, in_specs=in_specs, out_specs=out_specs, out_shape=out_shape,
                              scratch_shapes=scratch_shapes, input_output_aliases=aliases,
                              compiler_params=params)(*operands)
    n_in, n_out, n_scr = len(in_specs), len(out_shape), len(scratch_shapes)
    s_in, s_out = len(side["inputs"]), len(side["out_shapes"])

    steps = 1
    for extent in grid:
        steps *= extent

    def at_step(index):
        linear = pl.program_id(0)
        for a in range(1, len(grid)):
            linear = linear * grid[a] + pl.program_id(a)
        return linear == index

    def hosted(*refs):
        b = n_in + s_in
        c = b + n_out
        d = c + s_out
        e = d + n_scr
        src, dst, sems = refs[n_in:b], refs[c:d], refs[e:]

        @pl.when(at_step(0))
        def _():
            side["start"](src, dst, sems)

        if "relay" in side:
            @pl.when(at_step(min((3 * steps) // 4, steps - 1)))
            def _():
                side["relay"](src, dst, sems)

        body(*refs[:n_in], *refs[b:c], *refs[d:e])

        @pl.when(at_step(steps - 1))
        def _():
            side["finish"](src, dst, sems)

    return pl.pallas_call(
        hosted, name=name, grid=grid, in_specs=list(in_specs) + [HBM_SPEC] * s_in,
        out_specs=list(out_specs) + [HBM_SPEC] * s_out, out_shape=list(out_shape) + list(side["out_shapes"]),
        scratch_shapes=list(scratch_shapes) + list(side["sems"]),
        input_output_aliases={**aliases, **{n_in + a: n_out + b for a, b in side["aliases"].items()}},
        compiler_params=params)(*operands, *side["inputs"])


def ffn_fwd(x, nw, wg, wu, wd, name, side=None):
    T, D = x.shape
    NP, Fs, _ = wg.shape
    tm = min(TM_FFN_FWD, T)

    def body(x_ref, nw_ref, wg_ref, wu_ref, wd_ref, xo_ref, h_ref, g_ref, u_ref, acc_ref):
        j = pl.program_id(1)

        @pl.when(j == 0)
        def _():
            xv = x_ref[...]
            h_ref[...] = (xv * _rms_scale(xv) * nw_ref[...]).astype(BF16)
            acc_ref[...] = jnp.zeros_like(acc_ref)

        h = h_ref[...]
        g = _dot_nt(h, wg_ref[...])
        u = _dot_nt(h, wu_ref[...])
        a = (g * _sigmoid(g)) * u
        g_ref[...] = g.astype(BF16)
        u_ref[...] = u.astype(BF16)
        acc_ref[...] += _dot(a.astype(BF16), wd_ref[...])

        @pl.when(j == NP - 1)
        def _():
            xo_ref[...] = x_ref[...] + 0.5 * acc_ref[...]

    return _call_hosting(
        body, side, name=name, grid=(T // tm, NP),
        in_specs=[pl.BlockSpec((tm, D), lambda i, j: (i, 0)),
                  pl.BlockSpec((1, D), lambda i, j: (0, 0)),
                  pl.BlockSpec((None, Fs, D), lambda i, j: (j, 0, 0)),
                  pl.BlockSpec((None, Fs, D), lambda i, j: (j, 0, 0)),
                  pl.BlockSpec((None, Fs, D), lambda i, j: (j, 0, 0))],
        out_specs=[pl.BlockSpec((tm, D), lambda i, j: (i, 0)),
                   pl.BlockSpec((tm, D), lambda i, j: (i, 0)),
                   pl.BlockSpec((None, tm, Fs), lambda i, j: (j, i, 0)),
                   pl.BlockSpec((None, tm, Fs), lambda i, j: (j, i, 0))],
        out_shape=[jax.ShapeDtypeStruct((T, D), F32), jax.ShapeDtypeStruct((T, D), BF16),
                   jax.ShapeDtypeStruct((NP, T, Fs), BF16), jax.ShapeDtypeStruct((NP, T, Fs), BF16)],
        scratch_shapes=[pltpu.VMEM((tm, D), F32)],
        operands=(x, nw, wg, wu, wd))


def _ffn_bwd_piece(j, h, g, u, wg, wu, wd, dout, dh_in, dws_in, name, side, norm):
    T, D = h.shape
    NP, Fs, _ = wg.shape
    tm = min(TM_FFN, T)
    n_in = 7 + (dh_in is not None) + (2 if norm else 0) + (3 if dws_in else 0)

    def body(*refs):
        h_ref, g_ref, u_ref, wg_ref, wu_ref, wd_ref, do_ref = refs[:7]
        dhin_ref = refs[7] if dh_in is not None else None
        dh_ref, dwg_ref, dwu_ref, dwd_ref = refs[n_in:n_in + 4]

        @pl.when(pl.program_id(0) == 0)
        def _():
            dwg_ref[...] = jnp.zeros_like(dwg_ref)
            dwu_ref[...] = jnp.zeros_like(dwu_ref)
            dwd_ref[...] = jnp.zeros_like(dwd_ref)
            if norm:
                refs[n_in + 4][...] = jnp.zeros_like(refs[n_in + 4])

        dob = (0.5 * do_ref[...]).astype(BF16)
        da = _split_rows(_dot_nt, dob, wd_ref[...])
        gf = g_ref[...].astype(F32)
        uf = u_ref[...].astype(F32)
        s = _sigmoid(gf)
        act = gf * s
        dg = (da * uf * _silu_grad(gf, s)).astype(BF16)
        du = (da * act).astype(BF16)
        a = (act * uf).astype(BF16)
        dh = _dot(dg, wg_ref[...]) + _dot(du, wu_ref[...])
        dh = dh if dhin_ref is None else dhin_ref[...] + dh
        if norm:
            x_ref, nw_ref = refs[7 + (dh_in is not None):9 + (dh_in is not None)]
            dxn, dnw = _rms_bwd(x_ref[...], nw_ref[...], dh)
            dh_ref[...] = do_ref[...] + dxn
            refs[n_in + 4][...] += dnw
        else:
            dh_ref[...] = dh
        hb = h_ref[...]
        dwg_ref[...] += _dot_tn(dg, hb)
        dwu_ref[...] += _dot_tn(du, hb)
        dwd_ref[...] += _dot_tn(a, dob)

    rows = pl.BlockSpec((tm, D), lambda i: (i, 0))
    piece = pl.BlockSpec((None, tm, Fs), lambda i: (j, i, 0))
    slot = pl.BlockSpec((None, Fs, D), lambda i: (j, 0, 0), pipeline_mode=pl.Buffered(1))
    in_specs = [rows, piece, piece, slot, slot, slot, rows]
    operands = [h, g, u, wg, wu, wd, dout]
    aliases = {}
    if dh_in is not None:
        in_specs.append(rows)
        operands.append(dh_in)
    if norm:
        in_specs += [rows, pl.BlockSpec((1, D), lambda i: (0, 0))]
        operands += list(norm)
    if dws_in:
        aliases = {len(operands) + k: 1 + k for k in range(3)}
        in_specs += [HBM_SPEC] * 3
        operands += list(dws_in)
    out_specs = [rows, slot, slot, slot]
    out_shape = [jax.ShapeDtypeStruct((T, D), F32)] + [jax.ShapeDtypeStruct((NP, Fs, D), F32)] * 3
    if norm:
        out_specs.append(pl.BlockSpec((1, D), lambda i: (0, 0)))
        out_shape.append(jax.ShapeDtypeStruct((1, D), F32))
    return _call_hosting(body, side, name=name, grid=(T // tm,), in_specs=in_specs, out_specs=out_specs,
                         out_shape=out_shape, scratch_shapes=[], aliases=aliases, operands=tuple(operands))


def ffn_bwd(x, nw, h, g, u, wg, wu, wd, dout, name, side=None):
    NP = wg.shape[0]
    dh, dws, extra = None, None, []
    for j in range(NP):
        dh, *rest = _ffn_bwd_piece(j, h, g, u, wg, wu, wd, dout, dh, dws, f"{name}_{j}",
                                   side if j == 0 else None, (x, nw) if j == NP - 1 else None)
        dws, rest = rest[:3], rest[3:]
        if j == 0:
            extra = rest[1:] if NP == 1 else rest
    return (dh, *dws, rest[0], *extra)


def mix_in_fwd(x, nw, w_glu, w_qkv, w_gate, cs, sn, name):
    T, D = x.shape
    KV = (w_qkv.shape[1] - D) // 2
    tm = min(TM_ROW, T)

    def body(x_ref, nw_ref, wa_ref, wq_ref, wg_ref, cs_ref, sn_ref, h_ref, pa_ref, pg_ref, q_ref, k_ref, v_ref):
        xv = x_ref[...]
        h = (xv * _rms_scale(xv) * nw_ref[...]).astype(BF16)
        h_ref[...] = h
        pa_ref[...] = _dot(h, wa_ref[...])
        pg_ref[...] = _dot(h, wg_ref[...])
        qkv = _dot(h, wq_ref[...])
        cs_v, sn_v = cs_ref[...], sn_ref[...]
        q_ref[...] = _rope_chunks(qkv[:, :D], cs_v, sn_v, 1.0).astype(BF16)
        k_ref[...] = _rope_chunks(qkv[:, D:D + KV], cs_v, sn_v, 1.0).astype(BF16)
        v_ref[...] = qkv[:, D + KV:].astype(BF16)

    rows = lambda w: pl.BlockSpec((tm, w), lambda i: (i, 0))
    whole = lambda a: pl.BlockSpec(a.shape, lambda i: (0, 0))
    return pl.pallas_call(
        body, name=name, grid=(T // tm,),
        in_specs=[rows(D), whole(nw), whole(w_glu), whole(w_qkv), whole(w_gate), rows(128), rows(128)],
        out_specs=[rows(D), rows(2 * D), rows(2 * D), rows(D), rows(KV), rows(KV)],
        out_shape=[jax.ShapeDtypeStruct((T, D), BF16), jax.ShapeDtypeStruct((T, 2 * D), F32),
                   jax.ShapeDtypeStruct((T, 2 * D), F32), jax.ShapeDtypeStruct((T, D), BF16),
                   jax.ShapeDtypeStruct((T, KV), BF16), jax.ShapeDtypeStruct((T, KV), BF16)],
        compiler_params=_params(("parallel",)),
    )(x, nw, w_glu, w_qkv, w_gate, cs, sn)


def matmul_tn(lhs, rhs, name):
    T, K = lhs.shape
    N = rhs.shape[1]
    tk = min(TK_TN, T)

    def body(l_ref, r_ref, o_ref):
        @pl.when(pl.program_id(0) == 0)
        def _():
            o_ref[...] = jnp.zeros_like(o_ref)

        o_ref[...] += _dot_tn(l_ref[...].astype(BF16), r_ref[...].astype(BF16))

    return pl.pallas_call(
        body, name=name, grid=(T // tk,),
        in_specs=[pl.BlockSpec((tk, K), lambda t: (t, 0)), pl.BlockSpec((tk, N), lambda t: (t, 0))],
        out_specs=pl.BlockSpec((K, N), lambda t: (0, 0)),
        out_shape=jax.ShapeDtypeStruct((K, N), F32),
        compiler_params=_params(("arbitrary",)),
    )(lhs, rhs)


def mix_in_bwd(dps, ws, x, nw, dres, name):
    T, D = x.shape
    tm = min(TM_ROW, T)
    n = len(dps)

    def body(*refs):
        dp_refs, w_refs = refs[:n], refs[n:2 * n]
        x_ref, nw_ref, dr_ref, dx_ref, dnw_ref = refs[2 * n:]

        @pl.when(pl.program_id(0) == 0)
        def _():
            dnw_ref[...] = jnp.zeros_like(dnw_ref)

        dh = _dot_nt(dp_refs[0][...], w_refs[0][...])
        for k in range(1, n):
            dh += _dot_nt(dp_refs[k][...], w_refs[k][...])
        dxn, dnw = _rms_bwd(x_ref[...], nw_ref[...], dh)
        dx_ref[...] = dr_ref[...] + dxn
        dnw_ref[...] += dnw

    in_specs = [pl.BlockSpec((tm, dp.shape[1]), lambda i: (i, 0)) for dp in dps]
    in_specs += [pl.BlockSpec(w.shape, lambda i: (0, 0)) for w in ws]
    in_specs += [pl.BlockSpec((tm, D), lambda i: (i, 0)), pl.BlockSpec((1, D), lambda i: (0, 0)),
                 pl.BlockSpec((tm, D), lambda i: (i, 0))]
    return pl.pallas_call(
        body, name=name, grid=(T // tm,), in_specs=in_specs,
        out_specs=[pl.BlockSpec((tm, D), lambda i: (i, 0)), pl.BlockSpec((1, D), lambda i: (0, 0))],
        out_shape=[jax.ShapeDtypeStruct((T, D), F32), jax.ShapeDtypeStruct((1, D), F32)],
        compiler_params=_params(("arbitrary",)),
    )(*dps, *ws, x, nw, dres)


def _layernorm_stats(c1):
    mu = jnp.mean(c1, axis=-1, keepdims=True)
    xc = c1 - mu
    rstd = lax.rsqrt(jnp.mean(xc * xc, axis=-1, keepdims=True) + LN_EPS)
    return xc * rstd, rstd


def _shifted_copies(src_ref, dst_ref):
    rows = dst_ref.shape[1]
    for b in range(1, 8):
        dst_ref[b - 1] = src_ref[pl.ds(b, rows), :]


def _shifted_rows(src_ref, shifted_ref, start, rows, cols):
    a8, b = divmod(start, 8)
    if b == 0:
        return src_ref[pl.ds(8 * a8, rows), cols]
    return shifted_ref[b - 1, pl.ds(8 * a8, rows), cols]


def conv_fwd(p_glu, dw_w, dw_b, ln_g, ln_b, name):
    T, D2 = p_glu.shape
    D = D2 // 2
    tm = min(TM_ROW, T)
    hb = tm // CONV_HALO

    def body(a_ref, b_ref, ah_ref, bh_ref, w_ref, wb_ref, g_ref, be_ref, c1_ref, c3_ref, e_ref, es_ref):
        i = pl.program_id(0)
        halo = ah_ref[...] * _sigmoid(bh_ref[...])
        e_ref[pl.ds(0, CONV_HALO), :] = jnp.where(i > 0, halo, 0.0)
        e_ref[pl.ds(CONV_HALO, tm), :] = a_ref[...] * _sigmoid(b_ref[...])
        _shifted_copies(e_ref, es_ref)
        off = CONV_HALO - (CONV_WIDTH - 1)

        def strip(s, carry):
            cols = pl.ds(pl.multiple_of(s * 128, 128), 128)
            acc = jnp.zeros((tm, 128), F32) + wb_ref[:, cols]
            for k in range(CONV_WIDTH):
                acc += w_ref[pl.ds(k, 1), cols] * _shifted_rows(e_ref, es_ref, off + k, tm, cols)
            c1_ref[:, cols] = acc
            return carry

        lax.fori_loop(0, D // 128, strip, 0)
        xhat, _ = _layernorm_stats(c1_ref[...])
        c2 = xhat * g_ref[...] + be_ref[...]
        c3_ref[...] = (c2 * _sigmoid(c2)).astype(BF16)

    row = pl.BlockSpec((1, D), lambda i: (0, 0))
    return pl.pallas_call(
        body, name=name, grid=(T // tm,),
        in_specs=[pl.BlockSpec((tm, D), lambda i: (i, 0)), pl.BlockSpec((tm, D), lambda i: (i, 1)),
                  pl.BlockSpec((CONV_HALO, D), lambda i: (jnp.maximum(i * hb - 1, 0), 0)),
                  pl.BlockSpec((CONV_HALO, D), lambda i: (jnp.maximum(i * hb - 1, 0), 1)),
                  pl.BlockSpec((CONV_HALO, D), lambda i: (0, 0)), row, row, row],
        out_specs=[pl.BlockSpec((tm, D), lambda i: (i, 0)), pl.BlockSpec((tm, D), lambda i: (i, 0))],
        out_shape=[jax.ShapeDtypeStruct((T, D), F32), jax.ShapeDtypeStruct((T, D), BF16)],
        scratch_shapes=[pltpu.VMEM((tm + CONV_HALO, D), F32), pltpu.VMEM((7, tm + CONV_HALO - 8, D), F32)],
        compiler_params=_params(("parallel",)),
    )(p_glu, p_glu, p_glu, p_glu, dw_w, dw_b, ln_g, ln_b)


def conv_bwd(p_glu, dc1, dw_w, name, side=None):
    T, D2 = p_glu.shape
    D = D2 // 2
    tm = min(TM_ROW, T)
    hb = tm // CONV_HALO
    last = T // CONV_HALO - 1
    nblk = T // tm

    def body(a_ref, b_ref, ah_ref, bh_ref, d_ref, dn_ref, w_ref, dp_ref, dw_ref, e_ref, f_ref, es_ref, fs_ref):
        i = pl.program_id(0)

        @pl.when(i == 0)
        def _():
            dw_ref[...] = jnp.zeros_like(dw_ref)

        halo = ah_ref[...] * _sigmoid(bh_ref[...])
        e_ref[pl.ds(0, CONV_HALO), :] = jnp.where(i > 0, halo, 0.0)
        e_ref[pl.ds(CONV_HALO, tm), :] = a_ref[...] * _sigmoid(b_ref[...])
        f_ref[pl.ds(0, tm), :] = d_ref[...]
        f_ref[pl.ds(tm, CONV_HALO), :] = jnp.where(i < nblk - 1, dn_ref[...], 0.0)
        _shifted_copies(e_ref, es_ref)
        _shifted_copies(f_ref, fs_ref)
        off = CONV_HALO - (CONV_WIDTH - 1)

        def strip(s, carry):
            cols = pl.ds(pl.multiple_of(s * 128, 128), 128)
            d = d_ref[:, cols]
            dc0 = jnp.zeros((tm, 128), F32)
            for k in range(CONV_WIDTH):
                dw_ref[pl.ds(k, 1), cols] += jnp.sum(d * _shifted_rows(e_ref, es_ref, off + k, tm, cols),
                                                     axis=0, keepdims=True)
                dc0 += w_ref[pl.ds(k, 1), cols] * _shifted_rows(f_ref, fs_ref, CONV_WIDTH - 1 - k, tm, cols)
            a = a_ref[:, cols]
            sb = _sigmoid(b_ref[:, cols])
            dp_ref[:, cols] = (dc0 * sb).astype(BF16)
            dp_ref[:, pl.ds(pl.multiple_of(D + s * 128, 128), 128)] = (dc0 * a * sb * (1.0 - sb)).astype(BF16)
            return carry

        lax.fori_loop(0, D // 128, strip, 0)

    return _call_hosting(
        body, side, name=name, grid=(nblk,),
        in_specs=[pl.BlockSpec((tm, D), lambda i: (i, 0)), pl.BlockSpec((tm, D), lambda i: (i, 1)),
                  pl.BlockSpec((CONV_HALO, D), lambda i: (jnp.maximum(i * hb - 1, 0), 0)),
                  pl.BlockSpec((CONV_HALO, D), lambda i: (jnp.maximum(i * hb - 1, 0), 1)),
                  pl.BlockSpec((tm, D), lambda i: (i, 0)),
                  pl.BlockSpec((CONV_HALO, D), lambda i: (jnp.minimum((i + 1) * hb, last), 0)),
                  pl.BlockSpec((CONV_HALO, D), lambda i: (0, 0))],
        out_specs=[pl.BlockSpec((tm, D2), lambda i: (i, 0)), pl.BlockSpec((CONV_HALO, D), lambda i: (0, 0))],
        out_shape=[jax.ShapeDtypeStruct((T, D2), BF16), jax.ShapeDtypeStruct((CONV_HALO, D), F32)],
        scratch_shapes=[pltpu.VMEM((tm + CONV_HALO, D), F32), pltpu.VMEM((tm + CONV_HALO, D), F32),
                        pltpu.VMEM((7, tm + CONV_HALO - 8, D), F32), pltpu.VMEM((7, tm + CONV_HALO - 8, D), F32)],
        operands=(p_glu, p_glu, p_glu, p_glu, dc1, dc1, dw_w))


def _rot_half(x):
    lane = lax.broadcasted_iota(jnp.int32, x.shape, 1)
    first = (lane % HEAD_DIM) < HEAD_DIM // 2
    return jnp.where(first, pltpu.roll(x, 128 - HEAD_DIM // 2, 1), pltpu.roll(x, HEAD_DIM // 2, 1))


def _rope_chunks(x, cs, sn, sign):
    outs = []
    for c in range(x.shape[1] // 128):
        xc = x[:, c * 128:(c + 1) * 128]
        outs.append(xc * cs + sign * (_rot_half(xc) * sn))
    return outs[0] if len(outs) == 1 else jnp.concatenate(outs, axis=1)


def rope_bwd(dq, dk, dv, cs, sn, name):
    T, D = dq.shape
    KV = dk.shape[1]
    tm = min(TM_ROW, T)

    def body(dq_ref, dk_ref, dv_ref, cs_ref, sn_ref, o_ref):
        cs_v, sn_v = cs_ref[...], sn_ref[...]
        o_ref[:, pl.ds(0, D)] = _rope_chunks(dq_ref[...], cs_v, sn_v, -1.0).astype(BF16)
        o_ref[:, pl.ds(D, KV)] = _rope_chunks(dk_ref[...], cs_v, sn_v, -1.0).astype(BF16)
        o_ref[:, pl.ds(D + KV, KV)] = dv_ref[...].astype(BF16)

    tab = pl.BlockSpec((tm, 128), lambda i: (i, 0))
    return pl.pallas_call(
        body, name=name, grid=(T // tm,),
        in_specs=[pl.BlockSpec((tm, D), lambda i: (i, 0)), pl.BlockSpec((tm, KV), lambda i: (i, 0)),
                  pl.BlockSpec((tm, KV), lambda i: (i, 0)), tab, tab],
        out_specs=pl.BlockSpec((tm, D + 2 * KV), lambda i: (i, 0)),
        out_shape=jax.ShapeDtypeStruct((T, D + 2 * KV), BF16),
        compiler_params=_params(("parallel",)),
    )(dq, dk, dv, cs, sn)


def _lane_lo():
    return lax.broadcasted_iota(jnp.int32, (1, 128), 1) < HEAD_DIM


def _band_mask(i, reps):
    shape = (reps * WINDOW, 2 * WINDOW)
    qi = lax.broadcasted_iota(jnp.int32, shape, 0) % WINDOW
    cj = lax.broadcasted_iota(jnp.int32, shape, 1)
    rel = qi - cj + WINDOW
    return (rel >= 0) & (rel < WINDOW) & ((i > 0) | (cj >= WINDOW))


def _stack_pairs(ref, first, n):
    parts = [ref[:, pl.ds((first + p) * 128, 128)] for p in range(n)]
    return parts[0] if n == 1 else jnp.concatenate(parts, axis=0)


def _pair_rows(n):
    return lax.broadcasted_iota(jnp.int32, (n * WINDOW, 1), 0) // WINDOW


def _per_pair_column(values, n):
    rows = _pair_rows(n)
    col = jnp.zeros((n * WINDOW, 1), F32) + values[0]
    for p in range(1, n):
        col = jnp.where(rows == p, values[p], col)
    return col


def _kv_lo_hi(x2, g):
    pair, half = divmod(g, 2)
    lo = _lane_lo()
    xg = x2[:, pair * 128:(pair + 1) * 128].astype(F32)
    xg = jnp.where(lo if half == 0 else ~lo, xg, 0.0)
    sw = pltpu.roll(xg, HEAD_DIM, 1)
    x_lo, x_hi = (xg, sw) if half == 0 else (sw, xg)
    return x_lo.astype(BF16), x_hi.astype(BF16)


def _softmax_sink(s, allowed, sink):
    s = jnp.where(allowed, s * (HEAD_DIM ** -0.5), NEG_INF)
    m = jnp.maximum(jnp.max(s, axis=-1, keepdims=True), sink)
    p = jnp.exp(s - m)
    es = jnp.exp(sink - m)
    inv = 1.0 / (jnp.sum(p, axis=-1, keepdims=True) + es)
    return p * inv, es * inv


def attn_fwd(qr, kr, vb, sinks, name, side=None):
    T, D = qr.shape
    KV = kr.shape[1]
    n_kv = KV // HEAD_DIM
    group = (D // HEAD_DIM) // n_kv
    nb = T // WINDOW

    npair = group // 2

    def body(sink_ref, q_ref, kp_ref, kc_ref, vp_ref, vc_ref, o_ref):
        i = pl.program_id(0)
        allowed = _band_mask(i, npair)
        k2 = jnp.concatenate([kp_ref[...], kc_ref[...]], axis=0)
        v2 = jnp.concatenate([vp_ref[...], vc_ref[...]], axis=0)
        outs = [None] * (D // 128)
        for g in range(n_kv):
            k_lo, k_hi = _kv_lo_hi(k2, g)
            v_lo, v_hi = _kv_lo_hi(v2, g)
            first = (g * group) // 2
            q = _stack_pairs(q_ref, first, npair)
            sink_e = _per_pair_column([sink_ref[0, g * group + 2 * p] for p in range(npair)], npair)
            sink_o = _per_pair_column([sink_ref[0, g * group + 2 * p + 1] for p in range(npair)], npair)
            pe, _ = _softmax_sink(_dot_nt(q, k_lo), allowed, sink_e)
            po, _ = _softmax_sink(_dot_nt(q, k_hi), allowed, sink_o)
            o = _dot(pe.astype(BF16), v_lo) + _dot(po.astype(BF16), v_hi)
            for p in range(npair):
                outs[first + p] = o[p * WINDOW:(p + 1) * WINDOW]
        o_ref[...] = jnp.concatenate(outs, axis=1).astype(BF16)

    prev = lambda i: (jnp.maximum(i - 1, 0), 0)
    cur = lambda i: (i, 0)
    return _call_hosting(
        body, side, name=name, grid=(nb,),
        in_specs=[pl.BlockSpec(memory_space=pltpu.SMEM),
                  pl.BlockSpec((WINDOW, D), cur),
                  pl.BlockSpec((WINDOW, KV), prev), pl.BlockSpec((WINDOW, KV), cur),
                  pl.BlockSpec((WINDOW, KV), prev), pl.BlockSpec((WINDOW, KV), cur)],
        out_specs=[pl.BlockSpec((WINDOW, D), cur)],
        out_shape=[jax.ShapeDtypeStruct((T, D), BF16)],
        scratch_shapes=[], operands=(sinks, qr, kr, kr, vb, vb))


def attn_bwd(qr, kr, vb, o, do, sinks, name, side=None):
    T, D = qr.shape
    KV = kr.shape[1]
    n_heads = D // HEAD_DIM
    n_kv = KV // HEAD_DIM
    group = n_heads // n_kv
    nb = T // WINDOW
    npair = group // 2
    scale = HEAD_DIM ** -0.5

    def body(sink_ref, q_ref, kp_ref, kc_ref, vp_ref, vc_ref, o_ref, do_ref,
             dq_ref, dk_ref, dv_ref, ds_ref, ck_ref, cv_ref):
        i = pl.program_id(0)
        lo = _lane_lo()

        @pl.when(i == 0)
        def _():
            ck_ref[...] = jnp.zeros_like(ck_ref)
            cv_ref[...] = jnp.zeros_like(cv_ref)
            ds_ref[...] = jnp.zeros_like(ds_ref)

        @pl.when(i < nb)
        def _():
            allowed = _band_mask(i, npair)
            rows = _pair_rows(npair)
            k2 = jnp.concatenate([kp_ref[...], kc_ref[...]], axis=0)
            v2 = jnp.concatenate([vp_ref[...], vc_ref[...]], axis=0)
            lane = lax.broadcasted_iota(jnp.int32, (1, 128), 1)
            dsink = jnp.zeros((1, 128), F32)
            dq_out = [None] * (D // 128)
            dk_pairs = [jnp.zeros((2 * WINDOW, 128), F32) for _ in range(KV // 128)]
            dv_pairs = [jnp.zeros((2 * WINDOW, 128), F32) for _ in range(KV // 128)]
            for g in range(n_kv):
                k_lo, k_hi = _kv_lo_hi(k2, g)
                v_lo, v_hi = _kv_lo_hi(v2, g)
                first = (g * group) // 2
                q = _stack_pairs(q_ref, first, npair)
                dop = _stack_pairs(do_ref, first, npair)
                dd = dop.astype(F32) * _stack_pairs(o_ref, first, npair).astype(F32)
                dq = jnp.zeros((npair * WINDOW, 128), F32)
                dkg = jnp.zeros((2 * WINDOW, 128), F32)
                dvg = jnp.zeros((2 * WINDOW, 128), F32)
                for parity, k_h, v_h, sel in ((0, k_lo, v_lo, lo), (1, k_hi, v_hi, ~lo)):
                    heads = [g * group + 2 * p + parity for p in range(npair)]
                    sink = _per_pair_column([sink_ref[0, h] for h in heads], npair)
                    p_, ps = _softmax_sink(_dot_nt(q, k_h), allowed, sink)
                    delta = jnp.sum(jnp.where(sel, dd, 0.0), axis=-1, keepdims=True)
                    dsc = (p_ * (_dot_nt(dop, v_h) - delta)).astype(BF16)
                    sd = -ps * delta
                    for p, h in enumerate(heads):
                        dsink += jnp.where(lane == h, jnp.sum(jnp.where(rows == p, sd, 0.0)), 0.0)
                    dq += _dot(dsc, k_h)
                    dkg += jnp.where(sel, _dot_tn(dsc, q), 0.0)
                    dvg += jnp.where(sel, _dot_tn(p_.astype(BF16), dop), 0.0)
                for p in range(npair):
                    dq_out[first + p] = dq[p * WINDOW:(p + 1) * WINDOW]
                pair, half = divmod(g, 2)
                keep = lo if half == 0 else ~lo
                dk_pairs[pair] += jnp.where(keep, dkg + pltpu.roll(dkg, HEAD_DIM, 1), 0.0) * scale
                dv_pairs[pair] += jnp.where(keep, dvg + pltpu.roll(dvg, HEAD_DIM, 1), 0.0)
            dq_ref[...] = jnp.concatenate(dq_out, axis=1) * scale
            dk2 = dk_pairs[0] if len(dk_pairs) == 1 else jnp.concatenate(dk_pairs, axis=1)
            dv2 = dv_pairs[0] if len(dv_pairs) == 1 else jnp.concatenate(dv_pairs, axis=1)
            dk_ref[...] = ck_ref[...] + dk2[:WINDOW]
            dv_ref[...] = cv_ref[...] + dv2[:WINDOW]
            ck_ref[...] = dk2[WINDOW:]
            cv_ref[...] = dv2[WINDOW:]
            ds_ref[pl.ds(0, 1), :] += dsink

        @pl.when(i == nb)
        def _():
            dk_ref[...] = ck_ref[...]
            dv_ref[...] = cv_ref[...]

    prev = lambda i: (jnp.maximum(i - 1, 0), 0)
    cur = lambda i: (jnp.minimum(i, nb - 1), 0)
    prevc = lambda i: (jnp.maximum(jnp.minimum(i, nb - 1) - 1, 0), 0)
    return _call_hosting(
        body, side, name=name, grid=(nb + 1,),
        in_specs=[pl.BlockSpec(memory_space=pltpu.SMEM),
                  pl.BlockSpec((WINDOW, D), cur),
                  pl.BlockSpec((WINDOW, KV), prevc), pl.BlockSpec((WINDOW, KV), cur),
                  pl.BlockSpec((WINDOW, KV), prevc), pl.BlockSpec((WINDOW, KV), cur),
                  pl.BlockSpec((WINDOW, D), cur), pl.BlockSpec((WINDOW, D), cur)],
        out_specs=[pl.BlockSpec((WINDOW, D), cur), pl.BlockSpec((WINDOW, KV), prev),
                   pl.BlockSpec((WINDOW, KV), prev), pl.BlockSpec((8, 128), lambda i: (0, 0))],
        out_shape=[jax.ShapeDtypeStruct((T, D), F32), jax.ShapeDtypeStruct((T, KV), F32),
                   jax.ShapeDtypeStruct((T, KV), F32), jax.ShapeDtypeStruct((8, 128), F32)],
        scratch_shapes=[pltpu.VMEM((WINDOW, KV), F32), pltpu.VMEM((WINDOW, KV), F32)],
        operands=(sinks, qr, kr, kr, vb, vb, o, do))


def merge_fwd(x, c3, o, p_gate, gate_b, w_proj, w_o, w_out, name):
    T, D = x.shape
    tm = min(TM_ROW, T)

    def body(x_ref, c3_ref, o_ref, gc_ref, ga_ref, bc_ref, ba_ref, wp_ref, wo_ref, wout_ref,
             xo_ref, co_ref, ao_ref, mg_ref):
        conv_out = _dot(c3_ref[...], wp_ref[...])
        attn_out = _dot(o_ref[...], wo_ref[...])
        merged = (_sigmoid(gc_ref[...] + bc_ref[...]) * conv_out
                  + _sigmoid(ga_ref[...] + ba_ref[...]) * attn_out).astype(BF16)
        co_ref[...] = conv_out.astype(BF16)
        ao_ref[...] = attn_out.astype(BF16)
        mg_ref[...] = merged
        xo_ref[...] = x_ref[...] + _dot(merged, wout_ref[...])

    blk = lambda j: pl.BlockSpec((tm, D), lambda i: (i, j))
    row = lambda j: pl.BlockSpec((1, D), lambda i: (0, j))
    mat = pl.BlockSpec((D, D), lambda i: (0, 0))
    return pl.pallas_call(
        body, name=name, grid=(T // tm,),
        in_specs=[blk(0), blk(0), blk(0), blk(0), blk(1), row(0), row(1), mat, mat, mat],
        out_specs=[blk(0), blk(0), blk(0), blk(0)],
        out_shape=[jax.ShapeDtypeStruct((T, D), F32)] + [jax.ShapeDtypeStruct((T, D), BF16)] * 3,
        compiler_params=_params(("parallel",)),
    )(x, c3, o, p_gate, p_gate, gate_b, gate_b, w_proj, w_o, w_out)


def merge_bwd(dx, p_gate, gate_b, conv_out, attn_out, c1, ln_g, ln_b, w_proj, w_o, w_out, name, side=None):
    T, D = dx.shape
    tm = min(TM_ROW, T)

    def body(dx_ref, gc_ref, ga_ref, bc_ref, ba_ref, co_ref, ao_ref, c1_ref, g_ref, be_ref,
             wp_ref, wo_ref, wout_ref, dgt_ref, dco_ref, dao_ref, do_ref, dc1_ref, sm_ref):
        @pl.when(pl.program_id(0) == 0)
        def _():
            sm_ref[...] = jnp.zeros_like(sm_ref)

        dm = _dot_nt(dx_ref[...].astype(BF16), wout_ref[...])
        sc = _sigmoid(gc_ref[...] + bc_ref[...])
        sa = _sigmoid(ga_ref[...] + ba_ref[...])
        dco = (dm * sc).astype(BF16)
        dao = (dm * sa).astype(BF16)
        dgc = dm * co_ref[...].astype(F32) * sc * (1.0 - sc)
        dga = dm * ao_ref[...].astype(F32) * sa * (1.0 - sa)
        dgt_ref[:, pl.ds(0, D)] = dgc.astype(BF16)
        dgt_ref[:, pl.ds(D, D)] = dga.astype(BF16)
        dco_ref[...] = dco
        dao_ref[...] = dao
        do_ref[...] = _dot_nt(dao, wo_ref[...]).astype(BF16)
        dc3 = _dot_nt(dco, wp_ref[...])
        xhat, rstd = _layernorm_stats(c1_ref[...])
        c2 = xhat * g_ref[...] + be_ref[...]
        dc2 = dc3 * _silu_grad(c2, _sigmoid(c2))
        dxh = dc2 * g_ref[...]
        dc1 = rstd * (dxh - jnp.mean(dxh, axis=-1, keepdims=True)
                      - xhat * jnp.mean(dxh * xhat, axis=-1, keepdims=True))
        dc1_ref[...] = dc1
        colsum = lambda v: jnp.sum(v, axis=0, keepdims=True)
        for r, (left, right) in enumerate(((dgc, dga), (dc2 * xhat, dc2), (dc1, None))):
            sm_ref[pl.ds(r, 1), pl.ds(0, D)] += colsum(left)
            if right is not None:
                sm_ref[pl.ds(r, 1), pl.ds(D, D)] += colsum(right)

    blk = lambda j: pl.BlockSpec((tm, D), lambda i: (i, j))
    row = lambda j: pl.BlockSpec((1, D), lambda i: (0, j))
    mat = pl.BlockSpec((D, D), lambda i: (0, 0))
    return _call_hosting(
        body, side, name=name, grid=(T // tm,),
        in_specs=[blk(0), blk(0), blk(1), row(0), row(1), blk(0), blk(0), blk(0), row(0), row(0), mat, mat, mat],
        out_specs=[pl.BlockSpec((tm, 2 * D), lambda i: (i, 0)), blk(0), blk(0), blk(0), blk(0),
                   pl.BlockSpec((8, 2 * D), lambda i: (0, 0))],
        out_shape=[jax.ShapeDtypeStruct((T, 2 * D), BF16)] + [jax.ShapeDtypeStruct((T, D), BF16)] * 3
                  + [jax.ShapeDtypeStruct((T, D), F32), jax.ShapeDtypeStruct((8, 2 * D), F32)],
        scratch_shapes=[],
        operands=(dx, p_gate, p_gate, gate_b, gate_b, conv_out, attn_out, c1, ln_g, ln_b, w_proj, w_o, w_out))


def loss_head(x, nw, target, name):
    T, D = x.shape
    tm = min(TM_ROW, T)

    def body(x_ref, nw_ref, t_ref, dx_ref, sm_ref):
        @pl.when(pl.program_id(0) == 0)
        def _():
            sm_ref[...] = jnp.zeros_like(sm_ref)

        xv = x_ref[...]
        err = xv * _rms_scale(xv) * nw_ref[...] - t_ref[...]
        loss = 0.5 * jnp.sum(jnp.mean(err * err, axis=-1, keepdims=True))
        dxn, dnw = _rms_bwd(xv, nw_ref[...], err * (1.0 / D))
        dx_ref[...] = dxn
        sm_ref[pl.ds(0, 1), :] += dnw
        sm_ref[pl.ds(1, 1), :] += jnp.zeros((1, D), F32) + loss

    return pl.pallas_call(
        body, name=name, grid=(T // tm,),
        in_specs=[pl.BlockSpec((tm, D), lambda i: (i, 0)), pl.BlockSpec((1, D), lambda i: (0, 0)),
                  pl.BlockSpec((tm, D), lambda i: (i, 0))],
        out_specs=[pl.BlockSpec((tm, D), lambda i: (i, 0)), pl.BlockSpec((8, D), lambda i: (0, 0))],
        out_shape=[jax.ShapeDtypeStruct((T, D), F32), jax.ShapeDtypeStruct((8, D), F32)],
        compiler_params=_params(("arbitrary",)),
    )(x, nw, target)


def _by_shape(arrays):
    groups = {}
    for k, a in enumerate(arrays):
        groups.setdefault(a.shape, []).append(k)
    return list(groups.values())


def adamw(ws, gs, ms, vs, name):
    n = len(ws)
    R, C = ws[0].shape
    tr = _row_tile(R, TR_ELT)

    def body(*refs):
        for a in range(n):
            w_ref, g_ref, m_ref, v_ref, d_ref, mo_ref, vo_ref = (refs[k * n + a] for k in range(7))
            gv = g_ref[...]
            mn = ADAM_B1 * m_ref[...] + (1.0 - ADAM_B1) * gv
            vn = ADAM_B2 * v_ref[...] + (1.0 - ADAM_B2) * (gv * gv)
            m_hat = mn / (1.0 - ADAM_B1 ** ADAM_STEP)
            v_hat = vn / (1.0 - ADAM_B2 ** ADAM_STEP)
            d_ref[...] = -ADAM_LR * (m_hat / (jnp.sqrt(v_hat) + ADAM_EPS) + ADAM_WD * w_ref[...])
            mo_ref[...] = mn
            vo_ref[...] = vn

    spec = pl.BlockSpec((tr, C), lambda i: (i, 0))
    outs = pl.pallas_call(
        body, name=name, grid=(R // tr,), in_specs=[spec] * (4 * n), out_specs=[spec] * (3 * n),
        out_shape=[jax.ShapeDtypeStruct((R, C), F32)] * (3 * n),
        compiler_params=_params(("parallel",)),
    )(*ws, *gs, *ms, *vs)
    return [(outs[a], outs[n + a], outs[2 * n + a]) for a in range(n)]


def _place():
    return lax.axis_index("x"), lax.axis_index("y"), lax.axis_index("c")


def place_shards(place, ws, dtype, name):
    n = len(ws)
    R, C = ws[0].shape
    tr = _row_tile(R, TR_ELT)

    def body(pc_ref, *refs):
        for a in range(n):
            refs[n + a][...] = refs[a][...].astype(dtype)

    return pl.pallas_call(
        body, name=name,
        grid_spec=pltpu.PrefetchScalarGridSpec(
            num_scalar_prefetch=1, grid=(R // tr,),
            in_specs=[pl.BlockSpec((tr, C), lambda r, pc: (r, 0))] * n,
            out_specs=[pl.BlockSpec((None, tr, C), lambda r, pc: (pc[0], r, 0))] * n),
        out_shape=[jax.ShapeDtypeStruct((N_CHIPS, R, C), dtype)] * n,
        compiler_params=_params(("arbitrary",)),
    )(place, *ws)


def gather_side(shards, small):
    n, ns = len(shards), len(small)

    def ici_copy(dst, sems, k, j, x, y, c, sending):
        px, py = x ^ (j >> 1), y ^ (j & 1)
        slot = 2 * x + y if sending else 2 * px + py
        half = dst[k].shape[1] // 2
        part = dst[k].at[slot, pl.ds(c * half, half)] if k < n else dst[k].at[slot]
        return pltpu.make_async_remote_copy(part, part, sems[0].at[3 * k + j - 1], sems[1].at[3 * k + j - 1],
                                            device_id=(px, py, c), device_id_type=MESH)

    def d2d_copy(dst, sems, k, j, x, y, c, sending):
        half = dst[k].shape[1] // 2
        part = dst[k].at[2 * (x ^ (j >> 1)) + (y ^ (j & 1)), pl.ds((c if sending else 1 - c) * half, half)]
        return pltpu.make_async_remote_copy(part, part, sems[2].at[3 * k + j - 1], sems[3].at[3 * k + j - 1],
                                            device_id=(x, y, 1 - c), device_id_type=MESH)

    def start(src, dst, sems):
        x, y, c = _place()
        for k in range(n + ns):
            for j in (1, 2, 3):
                ici_copy(dst, sems, k, j, x, y, c, True).start()

    def relay(src, dst, sems):
        x, y, c = _place()
        for k in range(n + ns):
            for j in (1, 2, 3):
                ici_copy(dst, sems, k, j, x, y, c, False).wait_recv()
                if k < n:
                    d2d_copy(dst, sems, k, j, x, y, c, True).start()

    def finish(src, dst, sems):
        x, y, c = _place()
        for k in range(n):
            for j in (1, 2, 3):
                d2d_copy(dst, sems, k, j, x, y, c, False).wait_recv()
        for k in range(n + ns):
            for j in (1, 2, 3):
                ici_copy(dst, sems, k, j, x, y, c, True).wait_send()
                if k < n:
                    d2d_copy(dst, sems, k, j, x, y, c, True).wait_send()

    arrays = list(shards) + list(small)
    return dict(inputs=arrays, out_shapes=[jax.ShapeDtypeStruct(a.shape, a.dtype) for a in arrays],
                aliases={k: k for k in range(n + ns)},
                sems=[pltpu.SemaphoreType.DMA((3 * (n + ns),)), pltpu.SemaphoreType.DMA((3 * (n + ns),)),
                      pltpu.SemaphoreType.DMA((3 * n,)), pltpu.SemaphoreType.DMA((3 * n,))],
                start=start, relay=relay, finish=finish)


def run_side(side, name):
    n_in, n_out = len(side["inputs"]), len(side["out_shapes"])

    def body(*refs):
        src, dst, sems = refs[:n_in], refs[n_in:n_in + n_out], refs[n_in + n_out:]
        side["start"](src, dst, sems)
        if "relay" in side:
            side["relay"](src, dst, sems)
        side["finish"](src, dst, sems)

    return pl.pallas_call(
        body, name=name, in_specs=[HBM_SPEC] * n_in, out_specs=[HBM_SPEC] * n_out,
        out_shape=side["out_shapes"], input_output_aliases=side["aliases"], scratch_shapes=side["sems"],
    )(*side["inputs"])


def allreduce_small(block):
    R, C = block.shape

    def body(x_ref, out_ref, all_ref, send_sems, recv_sems, local_sem):
        x, y, c = _place()
        me, sibling = (x, y, c), (x, y, 1 - c)
        chips = [(1 - x, y), (x, 1 - y), (1 - x, 1 - y)]

        def slot(px, py, pc):
            return all_ref.at[4 * px + 2 * py + pc]

        def copy(k, block_of, to, src=None):
            return pltpu.make_async_remote_copy(
                src_ref=slot(*block_of) if src is None else src, dst_ref=slot(*block_of),
                send_sem=send_sems.at[k], recv_sem=recv_sems.at[k], device_id=to, device_id_type=MESH)

        mine = pltpu.make_async_copy(x_ref, slot(*me), local_sem)
        mine.start()
        first = [copy(0, me, sibling, src=x_ref)]
        first += [copy(1 + j, me, (*chip, c), src=x_ref) for j, chip in enumerate(chips)]
        for cp in first:
            cp.start()
        passed = [copy(4 + j, (*chip, c), sibling) for j, chip in enumerate(chips)]
        for j, chip in enumerate(chips):
            copy(1 + j, (*chip, c), me).wait_recv()
            passed[j].start()
        copy(0, sibling, me).wait_recv()
        for j, chip in enumerate(chips):
            copy(4 + j, (*chip, 1 - c), me).wait_recv()
        for cp in first + passed:
            cp.wait_send()
        mine.wait()
        total = all_ref[0]
        for d in range(1, N_DEV):
            total = total + all_ref[d]
        out_ref[...] = total

    return pl.pallas_call(
        body, name="allreduce_small",
        in_specs=[pl.BlockSpec(memory_space=pltpu.VMEM)], out_specs=pl.BlockSpec(memory_space=pltpu.VMEM),
        out_shape=jax.ShapeDtypeStruct((R, C), F32),
        scratch_shapes=[pltpu.VMEM((N_DEV, R, C), F32), pltpu.SemaphoreType.DMA((7,)),
                        pltpu.SemaphoreType.DMA((7,)), pltpu.SemaphoreType.DMA],
        compiler_params=pltpu.CompilerParams(vmem_limit_bytes=VMEM_LIMIT),
    )(block)


def exchange_siblings_side(grads):
    n = len(grads)

    def copies(src, dst, sems):
        x, y, c = _place()
        for k in range(n):
            half = src[k].shape[1] // 2
            yield pltpu.make_async_remote_copy(src[k].at[:, pl.ds((1 - c) * half, half)], dst[k],
                                               sems[0].at[k], sems[1].at[k],
                                               device_id=(x, y, 1 - c), device_id_type=MESH)

    def start(src, dst, sems):
        for cp in copies(src, dst, sems):
            cp.start()

    def finish(src, dst, sems):
        for cp in copies(src, dst, sems):
            cp.wait()

    return dict(inputs=list(grads), aliases={},
                out_shapes=[jax.ShapeDtypeStruct((N_CHIPS, g.shape[1] // 2, g.shape[2]), F32) for g in grads],
                sems=[pltpu.SemaphoreType.DMA((n,)), pltpu.SemaphoreType.DMA((n,))], start=start, finish=finish)


def rs_chip_sum(place, grads, sibs, name):
    n = len(grads)
    NP, R, C = grads[0].shape
    half = R // 2
    tr = _row_tile(half, TR_ELT)
    nr = half // tr

    def body(pc_ref, *refs):
        q = pl.program_id(1)
        for a in range(n):
            g_ref, s_ref, wire_ref, own_ref = (refs[k * n + a] for k in range(4))
            total = g_ref[...] + s_ref[...]
            wire_ref[...] = total.astype(BF16)

            @pl.when(q == pc_ref[0])
            def _():
                own_ref[...] = total

    outs = pl.pallas_call(
        body, name=name,
        grid_spec=pltpu.PrefetchScalarGridSpec(
            num_scalar_prefetch=1, grid=(nr, NP),
            in_specs=[pl.BlockSpec((None, tr, C), lambda r, q, pc: (q, pc[1] * nr + r, 0))] * n
                     + [pl.BlockSpec((None, tr, C), lambda r, q, pc: (q, r, 0))] * n,
            out_specs=[pl.BlockSpec((None, tr, C), lambda r, q, pc: (q, r, 0))] * n
                      + [pl.BlockSpec((tr, C), lambda r, q, pc: (r, 0))] * n),
        out_shape=[jax.ShapeDtypeStruct((NP, half, C), BF16)] * n + [jax.ShapeDtypeStruct((half, C), F32)] * n,
        compiler_params=_params(("arbitrary", "arbitrary")),
    )(place, *grads, *sibs)
    return outs[:n], outs[n:]


def exchange_chips_side(wires):
    n = len(wires)

    def copies(src, dst, sems):
        x, y, c = _place()
        for k in range(n):
            for j in (1, 2, 3):
                qx, qy = x ^ (j >> 1), y ^ (j & 1)
                yield pltpu.make_async_remote_copy(src[k].at[2 * qx + qy], dst[k].at[2 * x + y],
                                                   sems[0].at[3 * k + j - 1], sems[1].at[3 * k + j - 1],
                                                   device_id=(qx, qy, c), device_id_type=MESH)

    def start(src, dst, sems):
        for cp in copies(src, dst, sems):
            cp.start()

    def finish(src, dst, sems):
        for cp in copies(src, dst, sems):
            cp.wait()

    return dict(inputs=list(wires), out_shapes=[jax.ShapeDtypeStruct(w.shape, BF16) for w in wires], aliases={},
                sems=[pltpu.SemaphoreType.DMA((3 * n,)), pltpu.SemaphoreType.DMA((3 * n,))],
                start=start, finish=finish)


SEM_SPEC = pl.BlockSpec(memory_space=pltpu.SEMAPHORE)


def exchange_chips_start(wires, name):
    n = len(wires)
    side = exchange_chips_side(wires)

    def body(*refs):
        src, land, sems = refs[:n], refs[n:2 * n], refs[2 * n:2 * n + 2]
        side["start"](src, land, sems)
        refs[-1][...] = jnp.zeros_like(refs[-1])

    hbm = [pltpu.HBM(w.shape, w.dtype) for w in wires]
    outs = pl.pallas_call(
        body, name=name, in_specs=[HBM_SPEC] * (2 * n),
        out_specs=[SEM_SPEC, SEM_SPEC] + [HBM_SPEC] * (2 * n) + [pl.BlockSpec(memory_space=pltpu.VMEM)],
        out_shape=list(side["sems"]) + hbm + hbm + [jax.ShapeDtypeStruct((8, 128), F32)],
        input_output_aliases={k: 2 + k for k in range(2 * n)},
        compiler_params=pltpu.CompilerParams(has_side_effects=pltpu.SideEffectType.DATAFLOW_SIDE_EFFECTING),
    )(*[pltpu.with_memory_space_constraint(w, pltpu.HBM) for w in wires],
      *[pltpu.with_memory_space_constraint(lax.empty(w.shape, w.dtype), pltpu.HBM) for w in wires])
    return outs[0], outs[1], outs[2:2 + n], outs[2 + n:2 + 2 * n], outs[-1]


def exchange_chips_wait(send_sems, recv_sems, wires, lands, after, name):
    n = len(wires)
    side = exchange_chips_side(wires)

    def body(*refs):
        side["finish"](refs[:n], refs[n:2 * n], refs[2 * n:2 * n + 2])

    hbm = [pltpu.HBM(w.shape, w.dtype) for w in wires]
    outs = pl.pallas_call(
        body, name=name, in_specs=[HBM_SPEC] * (2 * n) + [SEM_SPEC, SEM_SPEC] + [HBM_SPEC] * len(after),
        out_specs=[HBM_SPEC] * (2 * n), out_shape=hbm + hbm,
        input_output_aliases={k: k for k in range(2 * n)},
        compiler_params=pltpu.CompilerParams(has_side_effects=pltpu.SideEffectType.DATAFLOW_SIDE_EFFECTING),
    )(*wires, *lands, send_sems, recv_sems, *after)
    return outs[n:]


def rs_final_sum(place, owns, gots, after, name):
    n = len(owns)
    NP, half, C = gots[0].shape
    tr = _row_tile(half, TR_ELT)
    nr = half // tr

    def body(pc_ref, *refs):
        for a in range(n):
            own_ref, g1_ref, g2_ref, g3_ref = (refs[k * n + a] for k in range(4))
            refs[4 * n + 1 + a][...] = (((own_ref[...] + g1_ref[...].astype(F32)) + g2_ref[...].astype(F32))
                                        + g3_ref[...].astype(F32))

    slot = lambda j: pl.BlockSpec((None, tr, C), lambda r, pc: (pc[0] ^ j, r, 0))
    return pl.pallas_call(
        body, name=name,
        grid_spec=pltpu.PrefetchScalarGridSpec(
            num_scalar_prefetch=1, grid=(nr,),
            in_specs=[pl.BlockSpec((tr, C), lambda r, pc: (r, 0))] * n + [slot(1)] * n + [slot(2)] * n + [slot(3)] * n
                     + [pl.BlockSpec((8, 128), lambda r, pc: (0, 0))],
            out_specs=[pl.BlockSpec((tr, C), lambda r, pc: (pc[1] * nr + r, 0))] * n),
        out_shape=[jax.ShapeDtypeStruct((2 * half, C), F32)] * n,
        compiler_params=_params(("arbitrary",)),
    )(place, *owns, *gots, *gots, *gots, after)


def rs_share_siblings(totals, name):
    n = len(totals)

    def body(*refs):
        dst = refs[n:2 * n]
        send_sems, recv_sems = refs[2 * n:]
        x, y, c = _place()
        copies = []
        for k in range(n):
            half = dst[k].shape[0] // 2
            rows = dst[k].at[pl.ds(c * half, half)]
            cp = pltpu.make_async_remote_copy(rows, rows, send_sems.at[k], recv_sems.at[k],
                                              device_id=(x, y, 1 - c), device_id_type=MESH)
            cp.start()
            copies.append(cp)
        for k, cp in enumerate(copies):
            cp.wait_send()
            half = dst[k].shape[0] // 2
            got = dst[k].at[pl.ds((1 - c) * half, half)]
            pltpu.make_async_remote_copy(got, got, send_sems.at[k], recv_sems.at[k],
                                         device_id=(x, y, c), device_id_type=MESH).wait_recv()

    return pl.pallas_call(
        body, name=name,
        in_specs=[HBM_SPEC] * n, out_specs=[HBM_SPEC] * n,
        out_shape=[jax.ShapeDtypeStruct(t.shape, F32) for t in totals],
        input_output_aliases={k: k for k in range(n)},
        scratch_shapes=[pltpu.SemaphoreType.DMA((n,)), pltpu.SemaphoreType.DMA((n,))],
    )(*totals)


def rs_to_wires(place, grads, tag, sibs=None):
    if sibs is None:
        sibs = run_side(exchange_siblings_side(grads), f"rs_exchange_siblings_{tag}")
    wires, owns = [None] * len(grads), [None] * len(grads)
    for ks in _by_shape(grads):
        ws, os_ = rs_chip_sum(place, [grads[k] for k in ks], [sibs[k] for k in ks], f"rs_chip_sum_{tag}{ks[0]}")
        for k, w, o in zip(ks, ws, os_):
            wires[k], owns[k] = w, o
    return wires, owns


def rs_finish(place, owns, gots, after, tag):
    totals = [None] * len(owns)
    for ks in _by_shape(owns):
        sums = rs_final_sum(place, [owns[k] for k in ks], [gots[k] for k in ks], after, f"rs_final_sum_{tag}{ks[0]}")
        for k, t in zip(ks, sums):
            totals[k] = t
    return rs_share_siblings(totals, f"rs_share_siblings_{tag}")


def _rope_tables(positions):
    half = HEAD_DIM // 2
    inv_freq = ROPE_THETA ** (-jnp.arange(half, dtype=F32) / half)
    ang = positions.astype(F32)[:, None] * inv_freq
    lanes = jnp.arange(128)
    spread = (lanes[None, :] % half == jnp.arange(half)[:, None]).astype(F32)
    signed = spread * jnp.where(lanes % HEAD_DIM < half, -1.0, 1.0).astype(F32)
    exact = lax.Precision.HIGHEST
    return jnp.dot(jnp.cos(ang), spread, precision=exact), jnp.dot(jnp.sin(ang), signed, precision=exact)


def _cols_from_pieces(pieces, start, stop):
    C = pieces.shape[2]
    parts = []
    for q in range(N_CHIPS):
        lo, hi = max(start, q * C), min(stop, (q + 1) * C)
        if lo < hi:
            parts.append(pieces[q][:, lo - q * C:hi - q * C])
    return parts[0] if len(parts) == 1 else jnp.concatenate(parts, axis=1)


def _pieces_from_groups(groups):
    C = sum(g.shape[1] for g in groups) // N_CHIPS
    pieces = []
    for q in range(N_CHIPS):
        parts, off = [], 0
        for g in groups:
            lo, hi = max(q * C, off), min((q + 1) * C, off + g.shape[1])
            if lo < hi:
                parts.append(g[:, lo - off:hi - off])
            off += g.shape[1]
        pieces.append(parts[0] if len(parts) == 1 else jnp.concatenate(parts, axis=1))
    return jnp.stack(pieces)


def kernel(x, positions, ffn1_norm, ffn1_w_gate, ffn1_w_up, ffn1_w_down, mix_norm, w_in, conv_dw_w, conv_dw_b, conv_ln_g, conv_ln_b, conv_w_proj, attn_sinks, attn_w_o, gate_b, w_out, ffn2_norm, ffn2_w_gate, ffn2_w_up, ffn2_w_down, final_norm, loss_target, m_ffn1_norm, m_ffn1_w_gate, m_ffn1_w_up, m_ffn1_w_down, m_mix_norm, m_w_in, m_conv_dw_w, m_conv_dw_b, m_conv_ln_g, m_conv_ln_b, m_conv_w_proj, m_attn_sinks, m_attn_w_o, m_gate_b, m_w_out, m_ffn2_norm, m_ffn2_w_gate, m_ffn2_w_up, m_ffn2_w_down, m_final_norm, v_ffn1_norm, v_ffn1_w_gate, v_ffn1_w_up, v_ffn1_w_down, v_mix_norm, v_w_in, v_conv_dw_w, v_conv_dw_b, v_conv_ln_g, v_conv_ln_b, v_conv_w_proj, v_attn_sinks, v_attn_w_o, v_gate_b, v_w_out, v_ffn2_norm, v_ffn2_w_gate, v_ffn2_w_up, v_ffn2_w_down, v_final_norm):
    weights = dict(ffn1_norm=ffn1_norm, ffn1_w_gate=ffn1_w_gate, ffn1_w_up=ffn1_w_up, ffn1_w_down=ffn1_w_down,
                   mix_norm=mix_norm, w_in=w_in, conv_dw_w=conv_dw_w, conv_dw_b=conv_dw_b, conv_ln_g=conv_ln_g,
                   conv_ln_b=conv_ln_b, conv_w_proj=conv_w_proj, attn_sinks=attn_sinks, attn_w_o=attn_w_o,
                   gate_b=gate_b, w_out=w_out, ffn2_norm=ffn2_norm, ffn2_w_gate=ffn2_w_gate, ffn2_w_up=ffn2_w_up,
                   ffn2_w_down=ffn2_w_down, final_norm=final_norm)
    m_in = dict(ffn1_norm=m_ffn1_norm, ffn1_w_gate=m_ffn1_w_gate, ffn1_w_up=m_ffn1_w_up, ffn1_w_down=m_ffn1_w_down,
                mix_norm=m_mix_norm, w_in=m_w_in, conv_dw_w=m_conv_dw_w, conv_dw_b=m_conv_dw_b,
                conv_ln_g=m_conv_ln_g, conv_ln_b=m_conv_ln_b, conv_w_proj=m_conv_w_proj, attn_sinks=m_attn_sinks,
                attn_w_o=m_attn_w_o, gate_b=m_gate_b, w_out=m_w_out, ffn2_norm=m_ffn2_norm,
                ffn2_w_gate=m_ffn2_w_gate, ffn2_w_up=m_ffn2_w_up, ffn2_w_down=m_ffn2_w_down, final_norm=m_final_norm)
    v_in = dict(ffn1_norm=v_ffn1_norm, ffn1_w_gate=v_ffn1_w_gate, ffn1_w_up=v_ffn1_w_up, ffn1_w_down=v_ffn1_w_down,
                mix_norm=v_mix_norm, w_in=v_w_in, conv_dw_w=v_conv_dw_w, conv_dw_b=v_conv_dw_b,
                conv_ln_g=v_conv_ln_g, conv_ln_b=v_conv_ln_b, conv_w_proj=v_conv_w_proj, attn_sinks=v_attn_sinks,
                attn_w_o=v_attn_w_o, gate_b=v_gate_b, w_out=v_w_out, ffn2_norm=v_ffn2_norm,
                ffn2_w_gate=v_ffn2_w_gate, ffn2_w_up=v_ffn2_w_up, ffn2_w_down=v_ffn2_w_down, final_norm=v_final_norm)
    names = list(weights)
    big = ["ffn1_w_gate", "ffn1_w_up", "ffn1_w_down", "w_in", "conv_w_proj", "attn_w_o", "w_out",
           "ffn2_w_gate", "ffn2_w_up", "ffn2_w_down"]
    transposed = [k for k in big if k.endswith(("w_gate", "w_up"))]
    for k in transposed:
        weights[k], m_in[k], v_in[k] = (jnp.swapaxes(a, 1, 2) for a in (weights[k], m_in[k], v_in[k]))

    xs = x[0]
    T, D = xs.shape
    KV = (w_in.shape[2] * N_CHIPS - 5 * D) // 2
    n_heads = D // HEAD_DIM
    my_chip = 2 * lax.axis_index("x") + lax.axis_index("y")
    place = jnp.stack([my_chip, lax.axis_index("c")]).astype(jnp.int32)

    first, mixer_w, second = big[:3], big[3:7], big[7:]
    placed = {}
    for group in (first, mixer_w, second):
        for ks in _by_shape([weights[k][0] for k in group]):
            same = [group[k] for k in ks]
            placed.update(zip(same, place_shards(place, [weights[k][0] for k in same], BF16, f"place_{same[0]}")))
    placed_dw, = place_shards(place, [conv_dw_w[0]], F32, "place_conv_dw_w")
    wg1, wu1, wd1 = run_side(gather_side([placed[k] for k in first], []), "gather_ffn1")
    x1, h1, g1, u1, *gathered = ffn_fwd(x[0], ffn1_norm, wg1, wu1, wd1, "ffn1_fwd",
                                        side=gather_side([placed[k] for k in mixer_w], [placed_dw]))
    full = dict(zip(mixer_w + ["conv_dw_w"], gathered))
    w_glu = _cols_from_pieces(full["w_in"], 0, 2 * D)
    w_qkv = _cols_from_pieces(full["w_in"], 2 * D, 3 * D + 2 * KV)
    w_gate = _cols_from_pieces(full["w_in"], 3 * D + 2 * KV, 5 * D + 2 * KV)
    w_proj = full["conv_w_proj"].reshape(D, D)
    w_o = full["attn_w_o"].reshape(D, D)
    w_out_f = full["w_out"].reshape(D, D)
    dw_w = full["conv_dw_w"].transpose(1, 0, 2).reshape(CONV_WIDTH, D)
    dw_w = jnp.concatenate([dw_w, jnp.zeros((CONV_HALO - CONV_WIDTH, D), F32)], axis=0)
    cs, sn = _rope_tables(positions[0])
    fn_row = final_norm.reshape(1, D)

    h2, p_glu, p_gate, qr, kr, vb = mix_in_fwd(x1, mix_norm, w_glu, w_qkv, w_gate, cs, sn, "mix_in_fwd")
    c1, c3 = conv_fwd(p_glu, dw_w, conv_dw_b, conv_ln_g, conv_ln_b, "conv_fwd")
    o, wg2, wu2, wd2 = attn_fwd(qr, kr, vb, attn_sinks, "attn_fwd",
                                side=gather_side([placed[k] for k in second], []))
    x2, conv_out, attn_out, merged = merge_fwd(x1, c3, o, p_gate, gate_b, w_proj, w_o, w_out_f, "merge_fwd")
    x3, h3, g2, u2 = ffn_fwd(x2, ffn2_norm, wg2, wu2, wd2, "ffn2_fwd")

    dx3, head_sums = loss_head(x3, fn_row, loss_target[0], "loss_head")
    dx2, dwg2, dwu2, dwd2, d_ffn2_norm = ffn_bwd(x2, ffn2_norm, h3, g2, u2, wg2, wu2, wd2, dx3, "ffn2_bwd")
    ffn2_grads = [dwg2, dwu2, dwd2]
    d_gates, d_conv_out, d_attn_out, d_o, dc1, merge_sums, *sibs_f2 = merge_bwd(
        dx2, p_gate, gate_b, conv_out, attn_out, c1, conv_ln_g, conv_ln_b, w_proj, w_o, w_out_f, "merge_bwd",
        side=exchange_siblings_side(ffn2_grads))
    d_w_out = matmul_tn(merged, dx2, "d_w_out")
    d_w_proj = matmul_tn(c3, d_conv_out, "d_conv_w_proj")
    d_w_o = matmul_tn(o, d_attn_out, "d_attn_w_o")
    wires_f2, owns_f2 = rs_to_wires(place, ffn2_grads, "ffn2", sibs=sibs_f2)
    d_glu, d_dw_w, *gots_f2 = conv_bwd(p_glu, dc1, dw_w, "conv_bwd", side=exchange_chips_side(wires_f2))
    dwc = D // N_CHIPS
    square_grads = [d_w_proj.reshape(N_CHIPS, dwc, D), d_w_o.reshape(N_CHIPS, dwc, D),
                    d_w_out.reshape(N_CHIPS, dwc, D)]
    dq, dk, dv, d_sinks, *sibs_sq = attn_bwd(qr, kr, vb, o, d_o, attn_sinks, "attn_bwd",
                                             side=exchange_siblings_side(square_grads))
    d_qkv = rope_bwd(dq, dk, dv, cs, sn, "rope_bwd")
    dx1, d_mix_norm = mix_in_bwd([d_glu, d_qkv, d_gates], [w_glu, w_qkv, w_gate], x1, mix_norm, dx2, "mix_in_bwd")
    d_w_in = _pieces_from_groups([matmul_tn(h2, d_glu, "d_w_in_glu"), matmul_tn(h2, d_qkv, "d_w_in_qkv"),
                                  matmul_tn(h2, d_gates, "d_w_in_gate")])
    sib_w_in = run_side(exchange_siblings_side([d_w_in]), "rs_exchange_siblings_w_in")
    wires_m, owns_m = rs_to_wires(place, [d_w_in] + square_grads, "mixer", sibs=list(sib_w_in) + list(sibs_sq))
    dx0, dwg1, dwu1, dwd1, d_ffn1_norm, *gots_m = ffn_bwd(xs, ffn1_norm, h1, g1, u1, wg1, wu1, wd1, dx1, "ffn1_bwd",
                                                          side=exchange_chips_side(wires_m))
    wires_l, owns_l = rs_to_wires(place, [dwg1, dwu1, dwd1], "ffn1")
    send_sems, recv_sems, wires_l, lands_l, token = exchange_chips_start(wires_l, "rs_exchange_chips_ffn1_start")
    early_names = ["ffn2_w_gate", "ffn2_w_up", "ffn2_w_down", "w_in", "conv_w_proj", "attn_w_o", "w_out"]
    late_names = ["ffn1_w_gate", "ffn1_w_up", "ffn1_w_down"]
    reduced_early = rs_finish(place, owns_f2 + owns_m, list(gots_f2) + list(gots_m), token, "early")

    pad_row = lambda v: jnp.pad(v, ((0, 0), (0, D - v.shape[1])))
    small_rows = jnp.concatenate([
        d_ffn1_norm, d_mix_norm, merge_sums[2:3, :D], merge_sums[1:2, :D], merge_sums[1:2, D:],
        pad_row(d_sinks[0:1, :n_heads]), merge_sums[0:1, :D], merge_sums[0:1, D:], d_ffn2_norm,
        head_sums[0:1], head_sums[1:2], jnp.zeros((5, D), F32), d_dw_w], axis=0)
    small = allreduce_small(small_rows)
    loss = small[10, 0]
    grads = {"ffn1_norm": small[0:1], "mix_norm": small[1:2], "conv_dw_b": small[2:3], "conv_ln_g": small[3:4],
             "conv_ln_b": small[4:5], "attn_sinks": small[5:6, :n_heads],
             "gate_b": jnp.concatenate([small[6:7], small[7:8]], axis=1), "ffn2_norm": small[8:9],
             "final_norm": small[9:10]}
    grads["conv_dw_w"] = lax.dynamic_slice(small[16:16 + CONV_WIDTH], (0, my_chip * dwc), (CONV_WIDTH, dwc))
    grads.update(zip(early_names, reduced_early))

    deltas, new_m, new_v = {}, {}, {}

    def apply_adamw(ks):
        flat = lambda a, k: a.reshape(-1, weights[k].shape[-1])
        done = {}
        for idx in _by_shape([flat(grads[k], k) for k in ks]):
            same = [ks[i] for i in idx]
            results = adamw([flat(weights[k], k) for k in same], [flat(grads[k], k) for k in same],
                            [flat(m_in[k], k) for k in same], [flat(v_in[k], k) for k in same], f"adamw_{same[0]}")
            for k, (d, mn, vn) in zip(same, results):
                shape = weights[k].shape
                grads[k] = grads[k].reshape(shape)
                deltas[k], new_m[k], new_v[k] = d.reshape(shape), mn.reshape(shape), vn.reshape(shape)
                done[k] = d
        return done

    done = apply_adamw([k for k in names if k not in late_names])
    gots_l = exchange_chips_wait(send_sems, recv_sems, wires_l, lands_l, [done[k] for k in early_names],
                                 "rs_exchange_chips_ffn1_wait")
    grads.update(zip(late_names, rs_finish(place, owns_l, gots_l, token, "late")))
    apply_adamw(late_names)
    for k in transposed:
        for group in (grads, deltas, new_m, new_v):
            group[k] = jnp.swapaxes(group[k], 1, 2)

    return (loss, dx0[None], *[grads[k] for k in names], *[deltas[k] for k in names],
            *[new_m[k] for k in names], *[new_v[k] for k in names])
```

```python
import functools

import jax
import jax.numpy as jnp
from jax import lax
from jax.experimental import pallas as pl
from jax.experimental.pallas import tpu as pltpu

F32 = jnp.float32
BF16 = jnp.bfloat16
MESH = pl.DeviceIdType.MESH

HEAD_DIM = 64
WINDOW = 128
CONV_WIDTH = 31
CONV_HALO = 32
ROPE_THETA = 10000.0
EPS = 1e-6
LN_EPS = 1e-5
NEG_INF = -1e30
N_CHIPS = 4
N_DEV = 8

ADAM_LR = 0.001
ADAM_B1 = 0.9
ADAM_B2 = 0.999
ADAM_EPS = 1e-08
ADAM_WD = 0.01
ADAM_STEP = 10

TM_FFN = 512
TM_FFN_FWD = 1024
TM_ROW = 256
TM_MIX = 512
TK_TN = 1024
TR_ELT = 256
VMEM_LIMIT = 56 * 1024 * 1024

NT_DIMS = (((1,), (1,)), ((), ()))
TN_DIMS = (((0,), (0,)), ((), ()))


def _row_tile(rows, cap):
    for t in range(min(cap, rows), 15, -1):
        if rows % t == 0 and t % 16 == 0:
            return t
    return rows


def _params(sem):
    return pltpu.CompilerParams(dimension_semantics=sem, vmem_limit_bytes=VMEM_LIMIT)


def _dot(a, b):
    return jnp.dot(a, b, preferred_element_type=F32)


def _dot_nt(a, b):
    return lax.dot_general(a, b, NT_DIMS, preferred_element_type=F32)


def _dot_tn(a, b):
    return lax.dot_general(a, b, TN_DIMS, preferred_element_type=F32)


def _split_rows(dot, a, b):
    m = a.shape[0] // 2
    return jnp.concatenate([dot(a[:m], b), dot(a[m:], b)], axis=0)


def _sigmoid(x):
    return jax.nn.sigmoid(x)


def _rms_scale(xv):
    return lax.rsqrt(jnp.mean(xv * xv, axis=-1, keepdims=True) + EPS)


def _rms_bwd(xv, nw, dh):
    r = _rms_scale(xv)
    dn = dh * nw
    dx = r * dn - xv * (r * r * r) * jnp.mean(dn * xv, axis=-1, keepdims=True)
    dnw = jnp.sum(dh * (xv * r), axis=0, keepdims=True)
    return dx, dnw


def _silu_grad(z, s):
    return s * (1.0 + z * (1.0 - s))


HBM_SPEC = pl.BlockSpec(memory_space=pl.ANY)


def _call_hosting(body, side, *, grid, in_specs, out_specs, out_shape, scratch_shapes, operands, name, aliases=None):
    params = _params(("arbitrary",) * len(grid))
    aliases = dict(aliases or {})
    if side is None:
        return pl.pallas_call(body, name=name, grid=grid, in_specs=in_specs, out_specs=out_specs, out_shape=out_shape,
                              scratch_shapes=scratch_shapes, input_output_aliases=aliases,
                              compiler_params=params)(*operands)
    n_in, n_out, n_scr = len(in_specs), len(out_shape), len(scratch_shapes)
    s_in, s_out = len(side["inputs"]), len(side["out_shapes"])

    steps = 1
    for extent in grid:
        steps *= extent

    def at_step(index):
        linear = pl.program_id(0)
        for a in range(1, len(grid)):
            linear = linear * grid[a] + pl.program_id(a)
        return linear == index

    def hosted(*refs):
        b = n_in + s_in
        c = b + n_out
        d = c + s_out
        e = d + n_scr
        src, dst, sems = refs[n_in:b], refs[c:d], refs[e:]

        @pl.when(at_step(0))
        def _():
            side["start"](src, dst, sems)

        if "relay" in side:
            @pl.when(at_step(min((3 * steps) // 4, steps - 1)))
            def _():
                side["relay"](src, dst, sems)

        body(*refs[:n_in], *refs[b:c], *refs[d:e])

        @pl.when(at_step(steps - 1))
        def _():
            side["finish"](src, dst, sems)

    return pl.pallas_call(
        hosted, name=name, grid=grid, in_specs=list(in_specs) + [HBM_SPEC] * s_in,
        out_specs=list(out_specs) + [HBM_SPEC] * s_out, out_shape=list(out_shape) + list(side["out_shapes"]),
        scratch_shapes=list(scratch_shapes) + list(side["sems"]),
        input_output_aliases={**aliases, **{n_in + a: n_out + b for a, b in side["aliases"].items()}},
        compiler_params=params)(*operands, *side["inputs"])


def ffn_fwd(x, nw, wg, wu, wd, name, side=None):
    T, D = x.shape
    NP, Fs, _ = wg.shape
    tm = min(TM_FFN_FWD, T)

    def body(x_ref, nw_ref, wg_ref, wu_ref, wd_ref, xo_ref, h_ref, g_ref, u_ref, acc_ref):
        j = pl.program_id(1)

        @pl.when(j == 0)
        def _():
            xv = x_ref[...]
            h_ref[...] = (xv * _rms_scale(xv) * nw_ref[...]).astype(BF16)
            acc_ref[...] = jnp.zeros_like(acc_ref)

        h = h_ref[...]
        g = _dot_nt(h, wg_ref[...])
        u = _dot_nt(h, wu_ref[...])
        a = (g * _sigmoid(g)) * u
        g_ref[...] = g.astype(BF16)
        u_ref[...] = u.astype(BF16)
        acc_ref[...] += _dot(a.astype(BF16), wd_ref[...])

        @pl.when(j == NP - 1)
        def _():
            xo_ref[...] = x_ref[...] + 0.5 * acc_ref[...]

    return _call_hosting(
        body, side, name=name, grid=(T // tm, NP),
        in_specs=[pl.BlockSpec((tm, D), lambda i, j: (i, 0)),
                  pl.BlockSpec((1, D), lambda i, j: (0, 0)),
                  pl.BlockSpec((None, Fs, D), lambda i, j: (j, 0, 0)),
                  pl.BlockSpec((None, Fs, D), lambda i, j: (j, 0, 0)),
                  pl.BlockSpec((None, Fs, D), lambda i, j: (j, 0, 0))],
        out_specs=[pl.BlockSpec((tm, D), lambda i, j: (i, 0)),
                   pl.BlockSpec((tm, D), lambda i, j: (i, 0)),
                   pl.BlockSpec((None, tm, Fs), lambda i, j: (j, i, 0)),
                   pl.BlockSpec((None, tm, Fs), lambda i, j: (j, i, 0))],
        out_shape=[jax.ShapeDtypeStruct((T, D), F32), jax.ShapeDtypeStruct((T, D), BF16),
                   jax.ShapeDtypeStruct((NP, T, Fs), BF16), jax.ShapeDtypeStruct((NP, T, Fs), BF16)],
        scratch_shapes=[pltpu.VMEM((tm, D), F32)],
        operands=(x, nw, wg, wu, wd))


def _ffn_bwd_piece(j, h, g, u, wg, wu, wd, dout, dh_in, dws_in, name, side, norm):
    T, D = h.shape
    NP, Fs, _ = wg.shape
    tm = min(TM_FFN, T)
    n_in = 7 + (dh_in is not None) + (2 if norm else 0) + (3 if dws_in else 0)

    def body(*refs):
        h_ref, g_ref, u_ref, wg_ref, wu_ref, wd_ref, do_ref = refs[:7]
        dhin_ref = refs[7] if dh_in is not None else None
        dh_ref, dwg_ref, dwu_ref, dwd_ref = refs[n_in:n_in + 4]

        @pl.when(pl.program_id(0) == 0)
        def _():
            dwg_ref[...] = jnp.zeros_like(dwg_ref)
            dwu_ref[...] = jnp.zeros_like(dwu_ref)
            dwd_ref[...] = jnp.zeros_like(dwd_ref)
            if norm:
                refs[n_in + 4][...] = jnp.zeros_like(refs[n_in + 4])

        dob = (0.5 * do_ref[...]).astype(BF16)
        da = _split_rows(_dot_nt, dob, wd_ref[...])
        gf = g_ref[...].astype(F32)
        uf = u_ref[...].astype(F32)
        s = _sigmoid(gf)
        act = gf * s
        dg = (da * uf * _silu_grad(gf, s)).astype(BF16)
        du = (da * act).astype(BF16)
        a = (act * uf).astype(BF16)
        dh = _dot(dg, wg_ref[...]) + _dot(du, wu_ref[...])
        dh = dh if dhin_ref is None else dhin_ref[...] + dh
        if norm:
            x_ref, nw_ref = refs[7 + (dh_in is not None):9 + (dh_in is not None)]
            dxn, dnw = _rms_bwd(x_ref[...], nw_ref[...], dh)
            dh_ref[...] = do_ref[...] + dxn
            refs[n_in + 4][...] += dnw
        else:
            dh_ref[...] = dh
        hb = h_ref[...]
        dwg_ref[...] += _dot_tn(dg, hb)
        dwu_ref[...] += _dot_tn(du, hb)
        dwd_ref[...] += _dot_tn(a, dob)

    rows = pl.BlockSpec((tm, D), lambda i: (i, 0))
    piece = pl.BlockSpec((None, tm, Fs), lambda i: (j, i, 0))
    slot = pl.BlockSpec((None, Fs, D), lambda i: (j, 0, 0), pipeline_mode=pl.Buffered(1))
    in_specs = [rows, piece, piece, slot, slot, slot, rows]
    operands = [h, g, u, wg, wu, wd, dout]
    aliases = {}
    if dh_in is not None:
        in_specs.append(rows)
        operands.append(dh_in)
    if norm:
        in_specs += [rows, pl.BlockSpec((1, D), lambda i: (0, 0))]
        operands += list(norm)
    if dws_in:
        aliases = {len(operands) + k: 1 + k for k in range(3)}
        in_specs += [HBM_SPEC] * 3
        operands += list(dws_in)
    out_specs = [rows, slot, slot, slot]
    out_shape = [jax.ShapeDtypeStruct((T, D), F32)] + [jax.ShapeDtypeStruct((NP, Fs, D), F32)] * 3
    if norm:
        out_specs.append(pl.BlockSpec((1, D), lambda i: (0, 0)))
        out_shape.append(jax.ShapeDtypeStruct((1, D), F32))
    return _call_hosting(body, side, name=name, grid=(T // tm,), in_specs=in_specs, out_specs=out_specs,
                         out_shape=out_shape, scratch_shapes=[], aliases=aliases, operands=tuple(operands))


def ffn_bwd(x, nw, h, g, u, wg, wu, wd, dout, name, side=None):
    NP = wg.shape[0]
    dh, dws, extra = None, None, []
    for j in range(NP):
        dh, *rest = _ffn_bwd_piece(j, h, g, u, wg, wu, wd, dout, dh, dws, f"{name}_{j}",
                                   side if j == 0 else None, (x, nw) if j == NP - 1 else None)
        dws, rest = rest[:3], rest[3:]
        if j == 0:
            extra = rest[1:] if NP == 1 else rest
    return (dh, *dws, rest[0], *extra)


def mix_in_fwd(x, nw, w_glu, w_qkv, w_gate, cs, sn, name):
    T, D = x.shape
    KV = (w_qkv.shape[1] - D) // 2
    tm = min(TM_MIX, T)

    def body(x_ref, nw_ref, wa_ref, wq_ref, wg_ref, cs_ref, sn_ref, h_ref, pa_ref, pg_ref, q_ref, k_ref, v_ref):
        xv = x_ref[...]
        h = (xv * _rms_scale(xv) * nw_ref[...]).astype(BF16)
        h_ref[...] = h
        pa_ref[...] = _dot(h, wa_ref[...])
        pg_ref[...] = _dot(h, wg_ref[...])
        qkv = _dot(h, wq_ref[...])
        cs_v, sn_v = cs_ref[...], sn_ref[...]
        q_ref[...] = _rope_chunks(qkv[:, :D], cs_v, sn_v, 1.0).astype(BF16)
        k_ref[...] = _rope_chunks(qkv[:, D:D + KV], cs_v, sn_v, 1.0).astype(BF16)
        v_ref[...] = qkv[:, D + KV:].astype(BF16)

    rows = lambda w: pl.BlockSpec((tm, w), lambda i: (i, 0))
    whole = lambda a: pl.BlockSpec(a.shape, lambda i: (0, 0), pipeline_mode=pl.Buffered(1))
    return pl.pallas_call(
        body, name=name, grid=(T // tm,),
        in_specs=[rows(D), whole(nw), whole(w_glu), whole(w_qkv), whole(w_gate), rows(128), rows(128)],
        out_specs=[rows(D), rows(2 * D), rows(2 * D), rows(D), rows(KV), rows(KV)],
        out_shape=[jax.ShapeDtypeStruct((T, D), BF16), jax.ShapeDtypeStruct((T, 2 * D), F32),
                   jax.ShapeDtypeStruct((T, 2 * D), F32), jax.ShapeDtypeStruct((T, D), BF16),
                   jax.ShapeDtypeStruct((T, KV), BF16), jax.ShapeDtypeStruct((T, KV), BF16)],
        compiler_params=_params(("parallel",)),
    )(x, nw, w_glu, w_qkv, w_gate, cs, sn)


def matmul_tn(lhs, rhs, name):
    T, K = lhs.shape
    N = rhs.shape[1]
    tk = min(TK_TN, T)

    def body(l_ref, r_ref, o_ref):
        @pl.when(pl.program_id(0) == 0)
        def _():
            o_ref[...] = jnp.zeros_like(o_ref)

        o_ref[...] += _dot_tn(l_ref[...].astype(BF16), r_ref[...].astype(BF16))

    return pl.pallas_call(
        body, name=name, grid=(T // tk,),
        in_specs=[pl.BlockSpec((tk, K), lambda t: (t, 0)), pl.BlockSpec((tk, N), lambda t: (t, 0))],
        out_specs=pl.BlockSpec((K, N), lambda t: (0, 0)),
        out_shape=jax.ShapeDtypeStruct((K, N), F32),
        compiler_params=_params(("arbitrary",)),
    )(lhs, rhs)


def mix_in_bwd(dps, ws, x, nw, dres, name):
    T, D = x.shape
    tm = min(TM_MIX, T)
    n = len(dps)

    def body(*refs):
        dp_refs, w_refs = refs[:n], refs[n:2 * n]
        x_ref, nw_ref, dr_ref, dx_ref, dnw_ref = refs[2 * n:]

        @pl.when(pl.program_id(0) == 0)
        def _():
            dnw_ref[...] = jnp.zeros_like(dnw_ref)

        dh = _dot_nt(dp_refs[0][...], w_refs[0][...])
        for k in range(1, n):
            dh += _dot_nt(dp_refs[k][...], w_refs[k][...])
        dxn, dnw = _rms_bwd(x_ref[...], nw_ref[...], dh)
        dx_ref[...] = dr_ref[...] + dxn
        dnw_ref[...] += dnw

    in_specs = [pl.BlockSpec((tm, dp.shape[1]), lambda i: (i, 0)) for dp in dps]
    in_specs += [pl.BlockSpec(w.shape, lambda i: (0, 0), pipeline_mode=pl.Buffered(1)) for w in ws]
    in_specs += [pl.BlockSpec((tm, D), lambda i: (i, 0)), pl.BlockSpec((1, D), lambda i: (0, 0)),
                 pl.BlockSpec((tm, D), lambda i: (i, 0))]
    return pl.pallas_call(
        body, name=name, grid=(T // tm,), in_specs=in_specs,
        out_specs=[pl.BlockSpec((tm, D), lambda i: (i, 0)), pl.BlockSpec((1, D), lambda i: (0, 0))],
        out_shape=[jax.ShapeDtypeStruct((T, D), F32), jax.ShapeDtypeStruct((1, D), F32)],
        compiler_params=_params(("arbitrary",)),
    )(*dps, *ws, x, nw, dres)


def _layernorm_stats(c1):
    mu = jnp.mean(c1, axis=-1, keepdims=True)
    xc = c1 - mu
    rstd = lax.rsqrt(jnp.mean(xc * xc, axis=-1, keepdims=True) + LN_EPS)
    return xc * rstd, rstd


def _shifted_copies(src_ref, dst_ref):
    rows = dst_ref.shape[1]
    for b in range(1, 8):
        dst_ref[b - 1] = src_ref[pl.ds(b, rows), :]


def _shifted_rows(src_ref, shifted_ref, start, rows, cols):
    a8, b = divmod(start, 8)
    if b == 0:
        return src_ref[pl.ds(8 * a8, rows), cols]
    return shifted_ref[b - 1, pl.ds(8 * a8, rows), cols]


def conv_fwd(p_glu, dw_w, dw_b, ln_g, ln_b, name):
    T, D2 = p_glu.shape
    D = D2 // 2
    tm = min(TM_ROW, T)
    hb = tm // CONV_HALO

    def body(a_ref, b_ref, ah_ref, bh_ref, w_ref, wb_ref, g_ref, be_ref, c1_ref, c3_ref, e_ref, es_ref):
        i = pl.program_id(0)
        halo = ah_ref[...] * _sigmoid(bh_ref[...])
        e_ref[pl.ds(0, CONV_HALO), :] = jnp.where(i > 0, halo, 0.0)
        e_ref[pl.ds(CONV_HALO, tm), :] = a_ref[...] * _sigmoid(b_ref[...])
        _shifted_copies(e_ref, es_ref)
        off = CONV_HALO - (CONV_WIDTH - 1)

        def strip(s, carry):
            cols = pl.ds(pl.multiple_of(s * 128, 128), 128)
            acc = jnp.zeros((tm, 128), F32) + wb_ref[:, cols]
            for k in range(CONV_WIDTH):
                acc += w_ref[pl.ds(k, 1), cols] * _shifted_rows(e_ref, es_ref, off + k, tm, cols)
            c1_ref[:, cols] = acc
            return carry

        lax.fori_loop(0, D // 128, strip, 0)
        xhat, _ = _layernorm_stats(c1_ref[...])
        c2 = xhat * g_ref[...] + be_ref[...]
        c3_ref[...] = (c2 * _sigmoid(c2)).astype(BF16)

    row = pl.BlockSpec((1, D), lambda i: (0, 0))
    return pl.pallas_call(
        body, name=name, grid=(T // tm,),
        in_specs=[pl.BlockSpec((tm, D), lambda i: (i, 0)), pl.BlockSpec((tm, D), lambda i: (i, 1)),
                  pl.BlockSpec((CONV_HALO, D), lambda i: (jnp.maximum(i * hb - 1, 0), 0)),
                  pl.BlockSpec((CONV_HALO, D), lambda i: (jnp.maximum(i * hb - 1, 0), 1)),
                  pl.BlockSpec((CONV_HALO, D), lambda i: (0, 0)), row, row, row],
        out_specs=[pl.BlockSpec((tm, D), lambda i: (i, 0)), pl.BlockSpec((tm, D), lambda i: (i, 0))],
        out_shape=[jax.ShapeDtypeStruct((T, D), F32), jax.ShapeDtypeStruct((T, D), BF16)],
        scratch_shapes=[pltpu.VMEM((tm + CONV_HALO, D), F32), pltpu.VMEM((7, tm + CONV_HALO - 8, D), F32)],
        compiler_params=_params(("parallel",)),
    )(p_glu, p_glu, p_glu, p_glu, dw_w, dw_b, ln_g, ln_b)


def conv_bwd(p_glu, dc1, dw_w, name, side=None):
    T, D2 = p_glu.shape
    D = D2 // 2
    tm = min(TM_ROW, T)
    hb = tm // CONV_HALO
    last = T // CONV_HALO - 1
    nblk = T // tm

    def body(a_ref, b_ref, ah_ref, bh_ref, d_ref, dn_ref, w_ref, dp_ref, dw_ref, e_ref, f_ref, es_ref, fs_ref):
        i = pl.program_id(0)

        @pl.when(i == 0)
        def _():
            dw_ref[...] = jnp.zeros_like(dw_ref)

        halo = ah_ref[...] * _sigmoid(bh_ref[...])
        e_ref[pl.ds(0, CONV_HALO), :] = jnp.where(i > 0, halo, 0.0)
        e_ref[pl.ds(CONV_HALO, tm), :] = a_ref[...] * _sigmoid(b_ref[...])
        f_ref[pl.ds(0, tm), :] = d_ref[...]
        f_ref[pl.ds(tm, CONV_HALO), :] = jnp.where(i < nblk - 1, dn_ref[...], 0.0)
        _shifted_copies(e_ref, es_ref)
        _shifted_copies(f_ref, fs_ref)
        off = CONV_HALO - (CONV_WIDTH - 1)

        def strip(s, carry):
            cols = pl.ds(pl.multiple_of(s * 128, 128), 128)
            d = d_ref[:, cols]
            dc0 = jnp.zeros((tm, 128), F32)
            for k in range(CONV_WIDTH):
                dw_ref[pl.ds(k, 1), cols] += jnp.sum(d * _shifted_rows(e_ref, es_ref, off + k, tm, cols),
                                                     axis=0, keepdims=True)
                dc0 += w_ref[pl.ds(k, 1), cols] * _shifted_rows(f_ref, fs_ref, CONV_WIDTH - 1 - k, tm, cols)
            a = a_ref[:, cols]
            sb = _sigmoid(b_ref[:, cols])
            dp_ref[:, cols] = (dc0 * sb).astype(BF16)
            dp_ref[:, pl.ds(pl.multiple_of(D + s * 128, 128), 128)] = (dc0 * a * sb * (1.0 - sb)).astype(BF16)
            return carry

        lax.fori_loop(0, D // 128, strip, 0)

    return _call_hosting(
        body, side, name=name, grid=(nblk,),
        in_specs=[pl.BlockSpec((tm, D), lambda i: (i, 0)), pl.BlockSpec((tm, D), lambda i: (i, 1)),
                  pl.BlockSpec((CONV_HALO, D), lambda i: (jnp.maximum(i * hb - 1, 0), 0)),
                  pl.BlockSpec((CONV_HALO, D), lambda i: (jnp.maximum(i * hb - 1, 0), 1)),
                  pl.BlockSpec((tm, D), lambda i: (i, 0)),
                  pl.BlockSpec((CONV_HALO, D), lambda i: (jnp.minimum((i + 1) * hb, last), 0)),
                  pl.BlockSpec((CONV_HALO, D), lambda i: (0, 0))],
        out_specs=[pl.BlockSpec((tm, D2), lambda i: (i, 0)), pl.BlockSpec((CONV_HALO, D), lambda i: (0, 0))],
        out_shape=[jax.ShapeDtypeStruct((T, D2), BF16), jax.ShapeDtypeStruct((CONV_HALO, D), F32)],
        scratch_shapes=[pltpu.VMEM((tm + CONV_HALO, D), F32), pltpu.VMEM((tm + CONV_HALO, D), F32),
                        pltpu.VMEM((7, tm + CONV_HALO - 8, D), F32), pltpu.VMEM((7, tm + CONV_HALO - 8, D), F32)],
        operands=(p_glu, p_glu, p_glu, p_glu, dc1, dc1, dw_w))


def _rot_half(x):
    lane = lax.broadcasted_iota(jnp.int32, x.shape, 1)
    first = (lane % HEAD_DIM) < HEAD_DIM // 2
    return jnp.where(first, pltpu.roll(x, 128 - HEAD_DIM // 2, 1), pltpu.roll(x, HEAD_DIM // 2, 1))


def _rope_chunks(x, cs, sn, sign):
    outs = []
    for c in range(x.shape[1] // 128):
        xc = x[:, c * 128:(c + 1) * 128]
        outs.append(xc * cs + sign * (_rot_half(xc) * sn))
    return outs[0] if len(outs) == 1 else jnp.concatenate(outs, axis=1)


def rope_bwd(dq, dk, dv, cs, sn, name):
    T, D = dq.shape
    KV = dk.shape[1]
    tm = min(TM_ROW, T)

    def body(dq_ref, dk_ref, dv_ref, cs_ref, sn_ref, o_ref):
        cs_v, sn_v = cs_ref[...], sn_ref[...]
        o_ref[:, pl.ds(0, D)] = _rope_chunks(dq_ref[...], cs_v, sn_v, -1.0).astype(BF16)
        o_ref[:, pl.ds(D, KV)] = _rope_chunks(dk_ref[...], cs_v, sn_v, -1.0).astype(BF16)
        o_ref[:, pl.ds(D + KV, KV)] = dv_ref[...].astype(BF16)

    tab = pl.BlockSpec((tm, 128), lambda i: (i, 0))
    return pl.pallas_call(
        body, name=name, grid=(T // tm,),
        in_specs=[pl.BlockSpec((tm, D), lambda i: (i, 0)), pl.BlockSpec((tm, KV), lambda i: (i, 0)),
                  pl.BlockSpec((tm, KV), lambda i: (i, 0)), tab, tab],
        out_specs=pl.BlockSpec((tm, D + 2 * KV), lambda i: (i, 0)),
        out_shape=jax.ShapeDtypeStruct((T, D + 2 * KV), BF16),
        compiler_params=_params(("parallel",)),
    )(dq, dk, dv, cs, sn)


def _lane_lo():
    return lax.broadcasted_iota(jnp.int32, (1, 128), 1) < HEAD_DIM


def _band_mask(i, reps):
    shape = (reps * WINDOW, 2 * WINDOW)
    qi = lax.broadcasted_iota(jnp.int32, shape, 0) % WINDOW
    cj = lax.broadcasted_iota(jnp.int32, shape, 1)
    rel = qi - cj + WINDOW
    return (rel >= 0) & (rel < WINDOW) & ((i > 0) | (cj >= WINDOW))


def _stack_pairs(ref, first, n):
    parts = [ref[:, pl.ds((first + p) * 128, 128)] for p in range(n)]
    return parts[0] if n == 1 else jnp.concatenate(parts, axis=0)


def _pair_rows(n):
    return lax.broadcasted_iota(jnp.int32, (n * WINDOW, 1), 0) // WINDOW


def _per_pair_column(values, n):
    rows = _pair_rows(n)
    col = jnp.zeros((n * WINDOW, 1), F32) + values[0]
    for p in range(1, n):
        col = jnp.where(rows == p, values[p], col)
    return col


def _kv_lo_hi(x2, g):
    pair, half = divmod(g, 2)
    lo = _lane_lo()
    xg = x2[:, pair * 128:(pair + 1) * 128].astype(F32)
    xg = jnp.where(lo if half == 0 else ~lo, xg, 0.0)
    sw = pltpu.roll(xg, HEAD_DIM, 1)
    x_lo, x_hi = (xg, sw) if half == 0 else (sw, xg)
    return x_lo.astype(BF16), x_hi.astype(BF16)


def _softmax_sink(s, allowed, sink):
    s = jnp.where(allowed, s * (HEAD_DIM ** -0.5), NEG_INF)
    m = jnp.maximum(jnp.max(s, axis=-1, keepdims=True), sink)
    p = jnp.exp(s - m)
    es = jnp.exp(sink - m)
    inv = 1.0 / (jnp.sum(p, axis=-1, keepdims=True) + es)
    return p * inv, es * inv


def attn_fwd(qr, kr, vb, sinks, name, side=None):
    T, D = qr.shape
    KV = kr.shape[1]
    n_kv = KV // HEAD_DIM
    group = (D // HEAD_DIM) // n_kv
    nb = T // WINDOW

    npair = group // 2

    def body(sink_ref, q_ref, kp_ref, kc_ref, vp_ref, vc_ref, o_ref):
        i = pl.program_id(0)
        allowed = _band_mask(i, npair)
        k2 = jnp.concatenate([kp_ref[...], kc_ref[...]], axis=0)
        v2 = jnp.concatenate([vp_ref[...], vc_ref[...]], axis=0)
        outs = [None] * (D // 128)
        for g in range(n_kv):
            k_lo, k_hi = _kv_lo_hi(k2, g)
            v_lo, v_hi = _kv_lo_hi(v2, g)
            first = (g * group) // 2
            q = _stack_pairs(q_ref, first, npair)
            sink_e = _per_pair_column([sink_ref[0, g * group + 2 * p] for p in range(npair)], npair)
            sink_o = _per_pair_column([sink_ref[0, g * group + 2 * p + 1] for p in range(npair)], npair)
            pe, _ = _softmax_sink(_dot_nt(q, k_lo), allowed, sink_e)
            po, _ = _softmax_sink(_dot_nt(q, k_hi), allowed, sink_o)
            o = _dot(pe.astype(BF16), v_lo) + _dot(po.astype(BF16), v_hi)
            for p in range(npair):
                outs[first + p] = o[p * WINDOW:(p + 1) * WINDOW]
        o_ref[...] = jnp.concatenate(outs, axis=1).astype(BF16)

    prev = lambda i: (jnp.maximum(i - 1, 0), 0)
    cur = lambda i: (i, 0)
    return _call_hosting(
        body, side, name=name, grid=(nb,),
        in_specs=[pl.BlockSpec(memory_space=pltpu.SMEM),
                  pl.BlockSpec((WINDOW, D), cur),
                  pl.BlockSpec((WINDOW, KV), prev), pl.BlockSpec((WINDOW, KV), cur),
                  pl.BlockSpec((WINDOW, KV), prev), pl.BlockSpec((WINDOW, KV), cur)],
        out_specs=[pl.BlockSpec((WINDOW, D), cur)],
        out_shape=[jax.ShapeDtypeStruct((T, D), BF16)],
        scratch_shapes=[], operands=(sinks, qr, kr, kr, vb, vb))


def attn_bwd(qr, kr, vb, o, do, sinks, name, side=None):
    T, D = qr.shape
    KV = kr.shape[1]
    n_heads = D // HEAD_DIM
    n_kv = KV // HEAD_DIM
    group = n_heads // n_kv
    nb = T // WINDOW
    npair = group // 2
    scale = HEAD_DIM ** -0.5

    def body(sink_ref, q_ref, kp_ref, kc_ref, vp_ref, vc_ref, o_ref, do_ref,
             dq_ref, dk_ref, dv_ref, ds_ref, ck_ref, cv_ref):
        i = pl.program_id(0)
        lo = _lane_lo()

        @pl.when(i == 0)
        def _():
            ck_ref[...] = jnp.zeros_like(ck_ref)
            cv_ref[...] = jnp.zeros_like(cv_ref)
            ds_ref[...] = jnp.zeros_like(ds_ref)

        @pl.when(i < nb)
        def _():
            allowed = _band_mask(i, npair)
            rows = _pair_rows(npair)
            k2 = jnp.concatenate([kp_ref[...], kc_ref[...]], axis=0)
            v2 = jnp.concatenate([vp_ref[...], vc_ref[...]], axis=0)
            lane = lax.broadcasted_iota(jnp.int32, (1, 128), 1)
            dsink = jnp.zeros((1, 128), F32)
            dq_out = [None] * (D // 128)
            dk_pairs = [jnp.zeros((2 * WINDOW, 128), F32) for _ in range(KV // 128)]
            dv_pairs = [jnp.zeros((2 * WINDOW, 128), F32) for _ in range(KV // 128)]
            for g in range(n_kv):
                k_lo, k_hi = _kv_lo_hi(k2, g)
                v_lo, v_hi = _kv_lo_hi(v2, g)
                first = (g * group) // 2
                q = _stack_pairs(q_ref, first, npair)
                dop = _stack_pairs(do_ref, first, npair)
                dd = dop.astype(F32) * _stack_pairs(o_ref, first, npair).astype(F32)
                dq = jnp.zeros((npair * WINDOW, 128), F32)
                dkg = jnp.zeros((2 * WINDOW, 128), F32)
                dvg = jnp.zeros((2 * WINDOW, 128), F32)
                for parity, k_h, v_h, sel in ((0, k_lo, v_lo, lo), (1, k_hi, v_hi, ~lo)):
                    heads = [g * group + 2 * p + parity for p in range(npair)]
                    sink = _per_pair_column([sink_ref[0, h] for h in heads], npair)
                    p_, ps = _softmax_sink(_dot_nt(q, k_h), allowed, sink)
                    delta = jnp.sum(jnp.where(sel, dd, 0.0), axis=-1, keepdims=True)
                    dsc = (p_ * (_dot_nt(dop, v_h) - delta)).astype(BF16)
                    sd = -ps * delta
                    for p, h in enumerate(heads):
                        dsink += jnp.where(lane == h, jnp.sum(jnp.where(rows == p, sd, 0.0)), 0.0)
                    dq += _dot(dsc, k_h)
                    dkg += jnp.where(sel, _dot_tn(dsc, q), 0.0)
                    dvg += jnp.where(sel, _dot_tn(p_.astype(BF16), dop), 0.0)
                for p in range(npair):
                    dq_out[first + p] = dq[p * WINDOW:(p + 1) * WINDOW]
                pair, half = divmod(g, 2)
                keep = lo if half == 0 else ~lo
                dk_pairs[pair] += jnp.where(keep, dkg + pltpu.roll(dkg, HEAD_DIM, 1), 0.0) * scale
                dv_pairs[pair] += jnp.where(keep, dvg + pltpu.roll(dvg, HEAD_DIM, 1), 0.0)
            dq_ref[...] = jnp.concatenate(dq_out, axis=1) * scale
            dk2 = dk_pairs[0] if len(dk_pairs) == 1 else jnp.concatenate(dk_pairs, axis=1)
            dv2 = dv_pairs[0] if len(dv_pairs) == 1 else jnp.concatenate(dv_pairs, axis=1)
            dk_ref[...] = ck_ref[...] + dk2[:WINDOW]
            dv_ref[...] = cv_ref[...] + dv2[:WINDOW]
            ck_ref[...] = dk2[WINDOW:]
            cv_ref[...] = dv2[WINDOW:]
            ds_ref[pl.ds(0, 1), :] += dsink

        @pl.when(i == nb)
        def _():
            dk_ref[...] = ck_ref[...]
            dv_ref[...] = cv_ref[...]

    prev = lambda i: (jnp.maximum(i - 1, 0), 0)
    cur = lambda i: (jnp.minimum(i, nb - 1), 0)
    prevc = lambda i: (jnp.maximum(jnp.minimum(i, nb - 1) - 1, 0), 0)
    return _call_hosting(
        body, side, name=name, grid=(nb + 1,),
        in_specs=[pl.BlockSpec(memory_space=pltpu.SMEM),
                  pl.BlockSpec((WINDOW, D), cur),
                  pl.BlockSpec((WINDOW, KV), prevc), pl.BlockSpec((WINDOW, KV), cur),
                  pl.BlockSpec((WINDOW, KV), prevc), pl.BlockSpec((WINDOW, KV), cur),
                  pl.BlockSpec((WINDOW, D), cur), pl.BlockSpec((WINDOW, D), cur)],
        out_specs=[pl.BlockSpec((WINDOW, D), cur), pl.BlockSpec((WINDOW, KV), prev),
                   pl.BlockSpec((WINDOW, KV), prev), pl.BlockSpec((8, 128), lambda i: (0, 0))],
        out_shape=[jax.ShapeDtypeStruct((T, D), F32), jax.ShapeDtypeStruct((T, KV), F32),
                   jax.ShapeDtypeStruct((T, KV), F32), jax.ShapeDtypeStruct((8, 128), F32)],
        scratch_shapes=[pltpu.VMEM((WINDOW, KV), F32), pltpu.VMEM((WINDOW, KV), F32)],
        operands=(sinks, qr, kr, kr, vb, vb, o, do))


def merge_fwd(x, c3, o, p_gate, gate_b, w_proj, w_o, w_out, name):
    T, D = x.shape
    tm = min(TM_MIX, T)

    def body(x_ref, c3_ref, o_ref, gc_ref, ga_ref, bc_ref, ba_ref, wp_ref, wo_ref, wout_ref,
             xo_ref, co_ref, ao_ref, mg_ref):
        conv_out = _dot(c3_ref[...], wp_ref[...])
        attn_out = _dot(o_ref[...], wo_ref[...])
        merged = (_sigmoid(gc_ref[...] + bc_ref[...]) * conv_out
                  + _sigmoid(ga_ref[...] + ba_ref[...]) * attn_out).astype(BF16)
        co_ref[...] = conv_out.astype(BF16)
        ao_ref[...] = attn_out.astype(BF16)
        mg_ref[...] = merged
        xo_ref[...] = x_ref[...] + _dot(merged, wout_ref[...])

    blk = lambda j: pl.BlockSpec((tm, D), lambda i: (i, j))
    row = lambda j: pl.BlockSpec((1, D), lambda i: (0, j))
    mat = pl.BlockSpec((D, D), lambda i: (0, 0), pipeline_mode=pl.Buffered(1))
    return pl.pallas_call(
        body, name=name, grid=(T // tm,),
        in_specs=[blk(0), blk(0), blk(0), blk(0), blk(1), row(0), row(1), mat, mat, mat],
        out_specs=[blk(0), blk(0), blk(0), blk(0)],
        out_shape=[jax.ShapeDtypeStruct((T, D), F32)] + [jax.ShapeDtypeStruct((T, D), BF16)] * 3,
        compiler_params=_params(("parallel",)),
    )(x, c3, o, p_gate, p_gate, gate_b, gate_b, w_proj, w_o, w_out)


def merge_bwd(dx, p_gate, gate_b, conv_out, attn_out, c1, ln_g, ln_b, w_proj, w_o, w_out, name, side=None):
    T, D = dx.shape
    tm = min(TM_ROW, T)

    def body(dx_ref, gc_ref, ga_ref, bc_ref, ba_ref, co_ref, ao_ref, c1_ref, g_ref, be_ref,
             wp_ref, wo_ref, wout_ref, dgt_ref, dco_ref, dao_ref, do_ref, dc1_ref, sm_ref):
        @pl.when(pl.program_id(0) == 0)
        def _():
            sm_ref[...] = jnp.zeros_like(sm_ref)

        dm = _dot_nt(dx_ref[...].astype(BF16), wout_ref[...])
        sc = _sigmoid(gc_ref[...] + bc_ref[...])
        sa = _sigmoid(ga_ref[...] + ba_ref[...])
        dco = (dm * sc).astype(BF16)
        dao = (dm * sa).astype(BF16)
        dgc = dm * co_ref[...].astype(F32) * sc * (1.0 - sc)
        dga = dm * ao_ref[...].astype(F32) * sa * (1.0 - sa)
        dgt_ref[:, pl.ds(0, D)] = dgc.astype(BF16)
        dgt_ref[:, pl.ds(D, D)] = dga.astype(BF16)
        dco_ref[...] = dco
        dao_ref[...] = dao
        do_ref[...] = _dot_nt(dao, wo_ref[...]).astype(BF16)
        dc3 = _dot_nt(dco, wp_ref[...])
        xhat, rstd = _layernorm_stats(c1_ref[...])
        c2 = xhat * g_ref[...] + be_ref[...]
        dc2 = dc3 * _silu_grad(c2, _sigmoid(c2))
        dxh = dc2 * g_ref[...]
        dc1 = rstd * (dxh - jnp.mean(dxh, axis=-1, keepdims=True)
                      - xhat * jnp.mean(dxh * xhat, axis=-1, keepdims=True))
        dc1_ref[...] = dc1
        colsum = lambda v: jnp.sum(v, axis=0, keepdims=True)
        for r, (left, right) in enumerate(((dgc, dga), (dc2 * xhat, dc2), (dc1, None))):
            sm_ref[pl.ds(r, 1), pl.ds(0, D)] += colsum(left)
            if right is not None:
                sm_ref[pl.ds(r, 1), pl.ds(D, D)] += colsum(right)

    blk = lambda j: pl.BlockSpec((tm, D), lambda i: (i, j))
    row = lambda j: pl.BlockSpec((1, D), lambda i: (0, j))
    mat = pl.BlockSpec((D, D), lambda i: (0, 0))
    return _call_hosting(
        body, side, name=name, grid=(T // tm,),
        in_specs=[blk(0), blk(0), blk(1), row(0), row(1), blk(0), blk(0), blk(0), row(0), row(0), mat, mat, mat],
        out_specs=[pl.BlockSpec((tm, 2 * D), lambda i: (i, 0)), blk(0), blk(0), blk(0), blk(0),
                   pl.BlockSpec((8, 2 * D), lambda i: (0, 0))],
        out_shape=[jax.ShapeDtypeStruct((T, 2 * D), BF16)] + [jax.ShapeDtypeStruct((T, D), BF16)] * 3
                  + [jax.ShapeDtypeStruct((T, D), F32), jax.ShapeDtypeStruct((8, 2 * D), F32)],
        scratch_shapes=[],
        operands=(dx, p_gate, p_gate, gate_b, gate_b, conv_out, attn_out, c1, ln_g, ln_b, w_proj, w_o, w_out))


def loss_head(x, nw, target, name):
    T, D = x.shape
    tm = min(TM_ROW, T)

    def body(x_ref, nw_ref, t_ref, dx_ref, sm_ref):
        @pl.when(pl.program_id(0) == 0)
        def _():
            sm_ref[...] = jnp.zeros_like(sm_ref)

        xv = x_ref[...]
        err = xv * _rms_scale(xv) * nw_ref[...] - t_ref[...]
        loss = 0.5 * jnp.sum(jnp.mean(err * err, axis=-1, keepdims=True))
        dxn, dnw = _rms_bwd(xv, nw_ref[...], err * (1.0 / D))
        dx_ref[...] = dxn
        sm_ref[pl.ds(0, 1), :] += dnw
        sm_ref[pl.ds(1, 1), :] += jnp.zeros((1, D), F32) + loss

    return pl.pallas_call(
        body, name=name, grid=(T // tm,),
        in_specs=[pl.BlockSpec((tm, D), lambda i: (i, 0)), pl.BlockSpec((1, D), lambda i: (0, 0)),
                  pl.BlockSpec((tm, D), lambda i: (i, 0))],
        out_specs=[pl.BlockSpec((tm, D), lambda i: (i, 0)), pl.BlockSpec((8, D), lambda i: (0, 0))],
        out_shape=[jax.ShapeDtypeStruct((T, D), F32), jax.ShapeDtypeStruct((8, D), F32)],
        compiler_params=_params(("arbitrary",)),
    )(x, nw, target)


def _by_shape(arrays):
    groups = {}
    for k, a in enumerate(arrays):
        groups.setdefault(a.shape, []).append(k)
    return list(groups.values())


def adamw(ws, gs, ms, vs, name):
    n = len(ws)
    R, C = ws[0].shape
    tr = _row_tile(R, TR_ELT)

    def body(*refs):
        for a in range(n):
            w_ref, g_ref, m_ref, v_ref, d_ref, mo_ref, vo_ref = (refs[k * n + a] for k in range(7))
            gv = g_ref[...]
            mn = ADAM_B1 * m_ref[...] + (1.0 - ADAM_B1) * gv
            vn = ADAM_B2 * v_ref[...] + (1.0 - ADAM_B2) * (gv * gv)
            m_hat = mn / (1.0 - ADAM_B1 ** ADAM_STEP)
            v_hat = vn / (1.0 - ADAM_B2 ** ADAM_STEP)
            d_ref[...] = -ADAM_LR * (m_hat / (jnp.sqrt(v_hat) + ADAM_EPS) + ADAM_WD * w_ref[...])
            mo_ref[...] = mn
            vo_ref[...] = vn

    spec = pl.BlockSpec((tr, C), lambda i: (i, 0))
    outs = pl.pallas_call(
        body, name=name, grid=(R // tr,), in_specs=[spec] * (4 * n), out_specs=[spec] * (3 * n),
        out_shape=[jax.ShapeDtypeStruct((R, C), F32)] * (3 * n),
        compiler_params=_params(("parallel",)),
    )(*ws, *gs, *ms, *vs)
    return [(outs[a], outs[n + a], outs[2 * n + a]) for a in range(n)]


def _place():
    return lax.axis_index("x"), lax.axis_index("y"), lax.axis_index("c")


def place_shards(place, ws, dtype, name):
    n = len(ws)
    R, C = ws[0].shape
    tr = _row_tile(R, TR_ELT)

    def body(pc_ref, *refs):
        for a in range(n):
            refs[n + a][...] = refs[a][...].astype(dtype)

    return pl.pallas_call(
        body, name=name,
        grid_spec=pltpu.PrefetchScalarGridSpec(
            num_scalar_prefetch=1, grid=(R // tr,),
            in_specs=[pl.BlockSpec((tr, C), lambda r, pc: (r, 0))] * n,
            out_specs=[pl.BlockSpec((None, tr, C), lambda r, pc: (pc[0], r, 0))] * n),
        out_shape=[jax.ShapeDtypeStruct((N_CHIPS, R, C), dtype)] * n,
        compiler_params=_params(("arbitrary",)),
    )(place, *ws)


def gather_side(shards, small):
    n, ns = len(shards), len(small)

    def ici_copy(dst, sems, k, j, x, y, c, sending):
        px, py = x ^ (j >> 1), y ^ (j & 1)
        slot = 2 * x + y if sending else 2 * px + py
        half = dst[k].shape[1] // 2
        part = dst[k].at[slot, pl.ds(c * half, half)] if k < n else dst[k].at[slot]
        return pltpu.make_async_remote_copy(part, part, sems[0].at[3 * k + j - 1], sems[1].at[3 * k + j - 1],
                                            device_id=(px, py, c), device_id_type=MESH)

    def d2d_copy(dst, sems, k, j, x, y, c, sending):
        half = dst[k].shape[1] // 2
        part = dst[k].at[2 * (x ^ (j >> 1)) + (y ^ (j & 1)), pl.ds((c if sending else 1 - c) * half, half)]
        return pltpu.make_async_remote_copy(part, part, sems[2].at[3 * k + j - 1], sems[3].at[3 * k + j - 1],
                                            device_id=(x, y, 1 - c), device_id_type=MESH)

    def start(src, dst, sems):
        x, y, c = _place()
        for k in range(n + ns):
            for j in (1, 2, 3):
                ici_copy(dst, sems, k, j, x, y, c, True).start()

    def relay(src, dst, sems):
        x, y, c = _place()
        for k in range(n + ns):
            for j in (1, 2, 3):
                ici_copy(dst, sems, k, j, x, y, c, False).wait_recv()
                if k < n:
                    d2d_copy(dst, sems, k, j, x, y, c, True).start()

    def finish(src, dst, sems):
        x, y, c = _place()
        for k in range(n):
            for j in (1, 2, 3):
                d2d_copy(dst, sems, k, j, x, y, c, False).wait_recv()
        for k in range(n + ns):
            for j in (1, 2, 3):
                ici_copy(dst, sems, k, j, x, y, c, True).wait_send()
                if k < n:
                    d2d_copy(dst, sems, k, j, x, y, c, True).wait_send()

    arrays = list(shards) + list(small)
    return dict(inputs=arrays, out_shapes=[jax.ShapeDtypeStruct(a.shape, a.dtype) for a in arrays],
                aliases={k: k for k in range(n + ns)},
                sems=[pltpu.SemaphoreType.DMA((3 * (n + ns),)), pltpu.SemaphoreType.DMA((3 * (n + ns),)),
                      pltpu.SemaphoreType.DMA((3 * n,)), pltpu.SemaphoreType.DMA((3 * n,))],
                start=start, relay=relay, finish=finish)


def run_side(side, name):
    n_in, n_out = len(side["inputs"]), len(side["out_shapes"])

    def body(*refs):
        src, dst, sems = refs[:n_in], refs[n_in:n_in + n_out], refs[n_in + n_out:]
        side["start"](src, dst, sems)
        if "relay" in side:
            side["relay"](src, dst, sems)
        side["finish"](src, dst, sems)

    return pl.pallas_call(
        body, name=name, in_specs=[HBM_SPEC] * n_in, out_specs=[HBM_SPEC] * n_out,
        out_shape=side["out_shapes"], input_output_aliases=side["aliases"], scratch_shapes=side["sems"],
    )(*side["inputs"])


def allreduce_small(block):
    R, C = block.shape

    def body(x_ref, out_ref, all_ref, send_sems, recv_sems, local_sem):
        x, y, c = _place()
        me, sibling = (x, y, c), (x, y, 1 - c)
        chips = [(1 - x, y), (x, 1 - y), (1 - x, 1 - y)]

        def slot(px, py, pc):
            return all_ref.at[4 * px + 2 * py + pc]

        def copy(k, block_of, to, src=None):
            return pltpu.make_async_remote_copy(
                src_ref=slot(*block_of) if src is None else src, dst_ref=slot(*block_of),
                send_sem=send_sems.at[k], recv_sem=recv_sems.at[k], device_id=to, device_id_type=MESH)

        mine = pltpu.make_async_copy(x_ref, slot(*me), local_sem)
        mine.start()
        first = [copy(0, me, sibling, src=x_ref)]
        first += [copy(1 + j, me, (*chip, c), src=x_ref) for j, chip in enumerate(chips)]
        for cp in first:
            cp.start()
        passed = [copy(4 + j, (*chip, c), sibling) for j, chip in enumerate(chips)]
        for j, chip in enumerate(chips):
            copy(1 + j, (*chip, c), me).wait_recv()
            passed[j].start()
        copy(0, sibling, me).wait_recv()
        for j, chip in enumerate(chips):
            copy(4 + j, (*chip, 1 - c), me).wait_recv()
        for cp in first + passed:
            cp.wait_send()
        mine.wait()
        total = all_ref[0]
        for d in range(1, N_DEV):
            total = total + all_ref[d]
        out_ref[...] = total

    return pl.pallas_call(
        body, name="allreduce_small",
        in_specs=[pl.BlockSpec(memory_space=pltpu.VMEM)], out_specs=pl.BlockSpec(memory_space=pltpu.VMEM),
        out_shape=jax.ShapeDtypeStruct((R, C), F32),
        scratch_shapes=[pltpu.VMEM((N_DEV, R, C), F32), pltpu.SemaphoreType.DMA((7,)),
                        pltpu.SemaphoreType.DMA((7,)), pltpu.SemaphoreType.DMA],
        compiler_params=pltpu.CompilerParams(vmem_limit_bytes=VMEM_LIMIT),
    )(block)


def exchange_siblings_side(grads):
    n = len(grads)

    def copies(src, dst, sems):
        x, y, c = _place()
        for k in range(n):
            half = src[k].shape[1] // 2
            yield pltpu.make_async_remote_copy(src[k].at[:, pl.ds((1 - c) * half, half)], dst[k],
                                               sems[0].at[k], sems[1].at[k],
                                               device_id=(x, y, 1 - c), device_id_type=MESH)

    def start(src, dst, sems):
        for cp in copies(src, dst, sems):
            cp.start()

    def finish(src, dst, sems):
        for cp in copies(src, dst, sems):
            cp.wait()

    return dict(inputs=list(grads), aliases={},
                out_shapes=[jax.ShapeDtypeStruct((N_CHIPS, g.shape[1] // 2, g.shape[2]), F32) for g in grads],
                sems=[pltpu.SemaphoreType.DMA((n,)), pltpu.SemaphoreType.DMA((n,))], start=start, finish=finish)


def rs_chip_sum(place, grads, sibs, name):
    n = len(grads)
    NP, R, C = grads[0].shape
    half = R // 2
    tr = _row_tile(half, TR_ELT)
    nr = half // tr

    def body(pc_ref, *refs):
        q = pl.program_id(1)
        for a in range(n):
            g_ref, s_ref, wire_ref, own_ref = (refs[k * n + a] for k in range(4))
            total = g_ref[...] + s_ref[...]
            wire_ref[...] = total.astype(BF16)

            @pl.when(q == pc_ref[0])
            def _():
                own_ref[...] = total

    outs = pl.pallas_call(
        body, name=name,
        grid_spec=pltpu.PrefetchScalarGridSpec(
            num_scalar_prefetch=1, grid=(nr, NP),
            in_specs=[pl.BlockSpec((None, tr, C), lambda r, q, pc: (q, pc[1] * nr + r, 0))] * n
                     + [pl.BlockSpec((None, tr, C), lambda r, q, pc: (q, r, 0))] * n,
            out_specs=[pl.BlockSpec((None, tr, C), lambda r, q, pc: (q, r, 0))] * n
                      + [pl.BlockSpec((tr, C), lambda r, q, pc: (r, 0))] * n),
        out_shape=[jax.ShapeDtypeStruct((NP, half, C), BF16)] * n + [jax.ShapeDtypeStruct((half, C), F32)] * n,
        compiler_params=_params(("arbitrary", "arbitrary")),
    )(place, *grads, *sibs)
    return outs[:n], outs[n:]


def exchange_chips_side(wires):
    n = len(wires)

    def copies(src, dst, sems):
        x, y, c = _place()
        for k in range(n):
            for j in (1, 2, 3):
                qx, qy = x ^ (j >> 1), y ^ (j & 1)
                yield pltpu.make_async_remote_copy(src[k].at[2 * qx + qy], dst[k].at[2 * x + y],
                                                   sems[0].at[3 * k + j - 1], sems[1].at[3 * k + j - 1],
                                                   device_id=(qx, qy, c), device_id_type=MESH)

    def start(src, dst, sems):
        for cp in copies(src, dst, sems):
            cp.start()

    def finish(src, dst, sems):
        for cp in copies(src, dst, sems):
            cp.wait()

    return dict(inputs=list(wires), out_shapes=[jax.ShapeDtypeStruct(w.shape, BF16) for w in wires], aliases={},
                sems=[pltpu.SemaphoreType.DMA((3 * n,)), pltpu.SemaphoreType.DMA((3 * n,))],
                start=start, finish=finish)


SEM_SPEC = pl.BlockSpec(memory_space=pltpu.SEMAPHORE)


def exchange_chips_start(wires, name):
    n = len(wires)
    side = exchange_chips_side(wires)

    def body(*refs):
        src, land, sems = refs[:n], refs[n:2 * n], refs[2 * n:2 * n + 2]
        side["start"](src, land, sems)
        refs[-1][...] = jnp.zeros_like(refs[-1])

    hbm = [pltpu.HBM(w.shape, w.dtype) for w in wires]
    outs = pl.pallas_call(
        body, name=name, in_specs=[HBM_SPEC] * (2 * n),
        out_specs=[SEM_SPEC, SEM_SPEC] + [HBM_SPEC] * (2 * n) + [pl.BlockSpec(memory_space=pltpu.VMEM)],
        out_shape=list(side["sems"]) + hbm + hbm + [jax.ShapeDtypeStruct((8, 128), F32)],
        input_output_aliases={k: 2 + k for k in range(2 * n)},
        compiler_params=pltpu.CompilerParams(has_side_effects=pltpu.SideEffectType.DATAFLOW_SIDE_EFFECTING),
    )(*[pltpu.with_memory_space_constraint(w, pltpu.HBM) for w in wires],
      *[pltpu.with_memory_space_constraint(lax.empty(w.shape, w.dtype), pltpu.HBM) for w in wires])
    return outs[0], outs[1], outs[2:2 + n], outs[2 + n:2 + 2 * n], outs[-1]


def exchange_chips_wait(send_sems, recv_sems, wires, lands, after, name):
    n = len(wires)
    side = exchange_chips_side(wires)

    def body(*refs):
        side["finish"](refs[:n], refs[n:2 * n], refs[2 * n:2 * n + 2])

    hbm = [pltpu.HBM(w.shape, w.dtype) for w in wires]
    outs = pl.pallas_call(
        body, name=name, in_specs=[HBM_SPEC] * (2 * n) + [SEM_SPEC, SEM_SPEC] + [HBM_SPEC] * len(after),
        out_specs=[HBM_SPEC] * (2 * n), out_shape=hbm + hbm,
        input_output_aliases={k: k for k in range(2 * n)},
        compiler_params=pltpu.CompilerParams(has_side_effects=pltpu.SideEffectType.DATAFLOW_SIDE_EFFECTING),
    )(*wires, *lands, send_sems, recv_sems, *after)
    return outs[n:]


def rs_final_sum(place, owns, gots, after, name):
    n = len(owns)
    NP, half, C = gots[0].shape
    tr = _row_tile(half, TR_ELT)
    nr = half // tr

    def body(pc_ref, *refs):
        for a in range(n):
            own_ref, g1_ref, g2_ref, g3_ref = (refs[k * n + a] for k in range(4))
            refs[4 * n + 1 + a][...] = (((own_ref[...] + g1_ref[...].astype(F32)) + g2_ref[...].astype(F32))
                                        + g3_ref[...].astype(F32))

    slot = lambda j: pl.BlockSpec((None, tr, C), lambda r, pc: (pc[0] ^ j, r, 0))
    return pl.pallas_call(
        body, name=name,
        grid_spec=pltpu.PrefetchScalarGridSpec(
            num_scalar_prefetch=1, grid=(nr,),
            in_specs=[pl.BlockSpec((tr, C), lambda r, pc: (r, 0))] * n + [slot(1)] * n + [slot(2)] * n + [slot(3)] * n
                     + [pl.BlockSpec((8, 128), lambda r, pc: (0, 0))],
            out_specs=[pl.BlockSpec((tr, C), lambda r, pc: (pc[1] * nr + r, 0))] * n),
        out_shape=[jax.ShapeDtypeStruct((2 * half, C), F32)] * n,
        compiler_params=_params(("arbitrary",)),
    )(place, *owns, *gots, *gots, *gots, after)


def rs_share_siblings(totals, name):
    n = len(totals)

    def body(*refs):
        dst = refs[n:2 * n]
        send_sems, recv_sems = refs[2 * n:]
        x, y, c = _place()
        copies = []
        for k in range(n):
            half = dst[k].shape[0] // 2
            rows = dst[k].at[pl.ds(c * half, half)]
            cp = pltpu.make_async_remote_copy(rows, rows, send_sems.at[k], recv_sems.at[k],
                                              device_id=(x, y, 1 - c), device_id_type=MESH)
            cp.start()
            copies.append(cp)
        for k, cp in enumerate(copies):
            cp.wait_send()
            half = dst[k].shape[0] // 2
            got = dst[k].at[pl.ds((1 - c) * half, half)]
            pltpu.make_async_remote_copy(got, got, send_sems.at[k], recv_sems.at[k],
                                         device_id=(x, y, c), device_id_type=MESH).wait_recv()

    return pl.pallas_call(
        body, name=name,
        in_specs=[HBM_SPEC] * n, out_specs=[HBM_SPEC] * n,
        out_shape=[jax.ShapeDtypeStruct(t.shape, F32) for t in totals],
        input_output_aliases={k: k for k in range(n)},
        scratch_shapes=[pltpu.SemaphoreType.DMA((n,)), pltpu.SemaphoreType.DMA((n,))],
    )(*totals)


def rs_to_wires(place, grads, tag, sibs=None):
    if sibs is None:
        sibs = run_side(exchange_siblings_side(grads), f"rs_exchange_siblings_{tag}")
    wires, owns = [None] * len(grads), [None] * len(grads)
    for ks in _by_shape(grads):
        ws, os_ = rs_chip_sum(place, [grads[k] for k in ks], [sibs[k] for k in ks], f"rs_chip_sum_{tag}{ks[0]}")
        for k, w, o in zip(ks, ws, os_):
            wires[k], owns[k] = w, o
    return wires, owns


def rs_finish(place, owns, gots, after, tag):
    totals = [None] * len(owns)
    for ks in _by_shape(owns):
        sums = rs_final_sum(place, [owns[k] for k in ks], [gots[k] for k in ks], after, f"rs_final_sum_{tag}{ks[0]}")
        for k, t in zip(ks, sums):
            totals[k] = t
    return rs_share_siblings(totals, f"rs_share_siblings_{tag}")


def _rope_tables(positions):
    half = HEAD_DIM // 2
    inv_freq = ROPE_THETA ** (-jnp.arange(half, dtype=F32) / half)
    ang = positions.astype(F32)[:, None] * inv_freq
    lanes = jnp.arange(128)
    spread = (lanes[None, :] % half == jnp.arange(half)[:, None]).astype(F32)
    signed = spread * jnp.where(lanes % HEAD_DIM < half, -1.0, 1.0).astype(F32)
    exact = lax.Precision.HIGHEST
    return jnp.dot(jnp.cos(ang), spread, precision=exact), jnp.dot(jnp.sin(ang), signed, precision=exact)


def _cols_from_pieces(pieces, start, stop):
    C = pieces.shape[2]
    parts = []
    for q in range(N_CHIPS):
        lo, hi = max(start, q * C), min(stop, (q + 1) * C)
        if lo < hi:
            parts.append(pieces[q][:, lo - q * C:hi - q * C])
    return parts[0] if len(parts) == 1 else jnp.concatenate(parts, axis=1)


def _pieces_from_groups(groups):
    C = sum(g.shape[1] for g in groups) // N_CHIPS
    pieces = []
    for q in range(N_CHIPS):
        parts, off = [], 0
        for g in groups:
            lo, hi = max(q * C, off), min((q + 1) * C, off + g.shape[1])
            if lo < hi:
                parts.append(g[:, lo - off:hi - off])
            off += g.shape[1]
        pieces.append(parts[0] if len(parts) == 1 else jnp.concatenate(parts, axis=1))
    return jnp.stack(pieces)


def kernel(x, positions, ffn1_norm, ffn1_w_gate, ffn1_w_up, ffn1_w_down, mix_norm, w_in, conv_dw_w, conv_dw_b, conv_ln_g, conv_ln_b, conv_w_proj, attn_sinks, attn_w_o, gate_b, w_out, ffn2_norm, ffn2_w_gate, ffn2_w_up, ffn2_w_down, final_norm, loss_target, m_ffn1_norm, m_ffn1_w_gate, m_ffn1_w_up, m_ffn1_w_down, m_mix_norm, m_w_in, m_conv_dw_w, m_conv_dw_b, m_conv_ln_g, m_conv_ln_b, m_conv_w_proj, m_attn_sinks, m_attn_w_o, m_gate_b, m_w_out, m_ffn2_norm, m_ffn2_w_gate, m_ffn2_w_up, m_ffn2_w_down, m_final_norm, v_ffn1_norm, v_ffn1_w_gate, v_ffn1_w_up, v_ffn1_w_down, v_mix_norm, v_w_in, v_conv_dw_w, v_conv_dw_b, v_conv_ln_g, v_conv_ln_b, v_conv_w_proj, v_attn_sinks, v_attn_w_o, v_gate_b, v_w_out, v_ffn2_norm, v_ffn2_w_gate, v_ffn2_w_up, v_ffn2_w_down, v_final_norm):
    weights = dict(ffn1_norm=ffn1_norm, ffn1_w_gate=ffn1_w_gate, ffn1_w_up=ffn1_w_up, ffn1_w_down=ffn1_w_down,
                   mix_norm=mix_norm, w_in=w_in, conv_dw_w=conv_dw_w, conv_dw_b=conv_dw_b, conv_ln_g=conv_ln_g,
                   conv_ln_b=conv_ln_b, conv_w_proj=conv_w_proj, attn_sinks=attn_sinks, attn_w_o=attn_w_o,
                   gate_b=gate_b, w_out=w_out, ffn2_norm=ffn2_norm, ffn2_w_gate=ffn2_w_gate, ffn2_w_up=ffn2_w_up,
                   ffn2_w_down=ffn2_w_down, final_norm=final_norm)
    m_in = dict(ffn1_norm=m_ffn1_norm, ffn1_w_gate=m_ffn1_w_gate, ffn1_w_up=m_ffn1_w_up, ffn1_w_down=m_ffn1_w_down,
                mix_norm=m_mix_norm, w_in=m_w_in, conv_dw_w=m_conv_dw_w, conv_dw_b=m_conv_dw_b,
                conv_ln_g=m_conv_ln_g, conv_ln_b=m_conv_ln_b, conv_w_proj=m_conv_w_proj, attn_sinks=m_attn_sinks,
                attn_w_o=m_attn_w_o, gate_b=m_gate_b, w_out=m_w_out, ffn2_norm=m_ffn2_norm,
                ffn2_w_gate=m_ffn2_w_gate, ffn2_w_up=m_ffn2_w_up, ffn2_w_down=m_ffn2_w_down, final_norm=m_final_norm)
    v_in = dict(ffn1_norm=v_ffn1_norm, ffn1_w_gate=v_ffn1_w_gate, ffn1_w_up=v_ffn1_w_up, ffn1_w_down=v_ffn1_w_down,
                mix_norm=v_mix_norm, w_in=v_w_in, conv_dw_w=v_conv_dw_w, conv_dw_b=v_conv_dw_b,
                conv_ln_g=v_conv_ln_g, conv_ln_b=v_conv_ln_b, conv_w_proj=v_conv_w_proj, attn_sinks=v_attn_sinks,
                attn_w_o=v_attn_w_o, gate_b=v_gate_b, w_out=v_w_out, ffn2_norm=v_ffn2_norm,
                ffn2_w_gate=v_ffn2_w_gate, ffn2_w_up=v_ffn2_w_up, ffn2_w_down=v_ffn2_w_down, final_norm=v_final_norm)
    names = list(weights)
    big = ["ffn1_w_gate", "ffn1_w_up", "ffn1_w_down", "w_in", "conv_w_proj", "attn_w_o", "w_out",
           "ffn2_w_gate", "ffn2_w_up", "ffn2_w_down"]
    transposed = [k for k in big if k.endswith(("w_gate", "w_up"))]
    for k in transposed:
        weights[k], m_in[k], v_in[k] = (jnp.swapaxes(a, 1, 2) for a in (weights[k], m_in[k], v_in[k]))

    xs = x[0]
    T, D = xs.shape
    KV = (w_in.shape[2] * N_CHIPS - 5 * D) // 2
    n_heads = D // HEAD_DIM
    my_chip = 2 * lax.axis_index("x") + lax.axis_index("y")
    place = jnp.stack([my_chip, lax.axis_index("c")]).astype(jnp.int32)

    first, mixer_w, second = big[:3], big[3:7], big[7:]
    placed = {}
    for group in (first, mixer_w, second):
        for ks in _by_shape([weights[k][0] for k in group]):
            same = [group[k] for k in ks]
            placed.update(zip(same, place_shards(place, [weights[k][0] for k in same], BF16, f"place_{same[0]}")))
    placed_dw, = place_shards(place, [conv_dw_w[0]], F32, "place_conv_dw_w")
    wg1, wu1, wd1 = run_side(gather_side([placed[k] for k in first], []), "gather_ffn1")
    x1, h1, g1, u1, *gathered = ffn_fwd(x[0], ffn1_norm, wg1, wu1, wd1, "ffn1_fwd",
                                        side=gather_side([placed[k] for k in mixer_w], [placed_dw]))
    full = dict(zip(mixer_w + ["conv_dw_w"], gathered))
    w_glu = _cols_from_pieces(full["w_in"], 0, 2 * D)
    w_qkv = _cols_from_pieces(full["w_in"], 2 * D, 3 * D + 2 * KV)
    w_gate = _cols_from_pieces(full["w_in"], 3 * D + 2 * KV, 5 * D + 2 * KV)
    w_proj = full["conv_w_proj"].reshape(D, D)
    w_o = full["attn_w_o"].reshape(D, D)
    w_out_f = full["w_out"].reshape(D, D)
    dw_w = full["conv_dw_w"].transpose(1, 0, 2).reshape(CONV_WIDTH, D)
    dw_w = jnp.concatenate([dw_w, jnp.zeros((CONV_HALO - CONV_WIDTH, D), F32)], axis=0)
    cs, sn = _rope_tables(positions[0])
    fn_row = final_norm.reshape(1, D)

    h2, p_glu, p_gate, qr, kr, vb = mix_in_fwd(x1, mix_norm, w_glu, w_qkv, w_gate, cs, sn, "mix_in_fwd")
    c1, c3 = conv_fwd(p_glu, dw_w, conv_dw_b, conv_ln_g, conv_ln_b, "conv_fwd")
    o, wg2, wu2, wd2 = attn_fwd(qr, kr, vb, attn_sinks, "attn_fwd",
                                side=gather_side([placed[k] for k in second], []))
    x2, conv_out, attn_out, merged = merge_fwd(x1, c3, o, p_gate, gate_b, w_proj, w_o, w_out_f, "merge_fwd")
    x3, h3, g2, u2 = ffn_fwd(x2, ffn2_norm, wg2, wu2, wd2, "ffn2_fwd")

    dx3, head_sums = loss_head(x3, fn_row, loss_target[0], "loss_head")
    dx2, dwg2, dwu2, dwd2, d_ffn2_norm = ffn_bwd(x2, ffn2_norm, h3, g2, u2, wg2, wu2, wd2, dx3, "ffn2_bwd")
    ffn2_grads = [dwg2, dwu2, dwd2]
    d_gates, d_conv_out, d_attn_out, d_o, dc1, merge_sums, *sibs_f2 = merge_bwd(
        dx2, p_gate, gate_b, conv_out, attn_out, c1, conv_ln_g, conv_ln_b, w_proj, w_o, w_out_f, "merge_bwd",
        side=exchange_siblings_side(ffn2_grads))
    d_w_out = matmul_tn(merged, dx2, "d_w_out")
    d_w_proj = matmul_tn(c3, d_conv_out, "d_conv_w_proj")
    d_w_o = matmul_tn(o, d_attn_out, "d_attn_w_o")
    wires_f2, owns_f2 = rs_to_wires(place, ffn2_grads, "ffn2", sibs=sibs_f2)
    d_glu, d_dw_w, *gots_f2 = conv_bwd(p_glu, dc1, dw_w, "conv_bwd", side=exchange_chips_side(wires_f2))
    dwc = D // N_CHIPS
    square_grads = [d_w_proj.reshape(N_CHIPS, dwc, D), d_w_o.reshape(N_CHIPS, dwc, D),
                    d_w_out.reshape(N_CHIPS, dwc, D)]
    dq, dk, dv, d_sinks, *sibs_sq = attn_bwd(qr, kr, vb, o, d_o, attn_sinks, "attn_bwd",
                                             side=exchange_siblings_side(square_grads))
    d_qkv = rope_bwd(dq, dk, dv, cs, sn, "rope_bwd")
    dx1, d_mix_norm = mix_in_bwd([d_glu, d_qkv, d_gates], [w_glu, w_qkv, w_gate], x1, mix_norm, dx2, "mix_in_bwd")
    d_w_in = _pieces_from_groups([matmul_tn(h2, d_glu, "d_w_in_glu"), matmul_tn(h2, d_qkv, "d_w_in_qkv"),
                                  matmul_tn(h2, d_gates, "d_w_in_gate")])
    sib_w_in = run_side(exchange_siblings_side([d_w_in]), "rs_exchange_siblings_w_in")
    wires_m, owns_m = rs_to_wires(place, [d_w_in] + square_grads, "mixer", sibs=list(sib_w_in) + list(sibs_sq))
    dx0, dwg1, dwu1, dwd1, d_ffn1_norm, *gots_m = ffn_bwd(xs, ffn1_norm, h1, g1, u1, wg1, wu1, wd1, dx1, "ffn1_bwd",
                                                          side=exchange_chips_side(wires_m))
    wires_l, owns_l = rs_to_wires(place, [dwg1, dwu1, dwd1], "ffn1")
    send_sems, recv_sems, wires_l, lands_l, token = exchange_chips_start(wires_l, "rs_exchange_chips_ffn1_start")
    early_names = ["ffn2_w_gate", "ffn2_w_up", "ffn2_w_down", "w_in", "conv_w_proj", "attn_w_o", "w_out"]
    late_names = ["ffn1_w_gate", "ffn1_w_up", "ffn1_w_down"]
    reduced_early = rs_finish(place, owns_f2 + owns_m, list(gots_f2) + list(gots_m), token, "early")

    pad_row = lambda v: jnp.pad(v, ((0, 0), (0, D - v.shape[1])))
    small_rows = jnp.concatenate([
        d_ffn1_norm, d_mix_norm, merge_sums[2:3, :D], merge_sums[1:2, :D], merge_sums[1:2, D:],
        pad_row(d_sinks[0:1, :n_heads]), merge_sums[0:1, :D], merge_sums[0:1, D:], d_ffn2_norm,
        head_sums[0:1], head_sums[1:2], jnp.zeros((5, D), F32), d_dw_w], axis=0)
    small = allreduce_small(small_rows)
    loss = small[10, 0]
    grads = {"ffn1_norm": small[0:1], "mix_norm": small[1:2], "conv_dw_b": small[2:3], "conv_ln_g": small[3:4],
             "conv_ln_b": small[4:5], "attn_sinks": small[5:6, :n_heads],
             "gate_b": jnp.concatenate([small[6:7], small[7:8]], axis=1), "ffn2_norm": small[8:9],
             "final_norm": small[9:10]}
    grads["conv_dw_w"] = lax.dynamic_slice(small[16:16 + CONV_WIDTH], (0, my_chip * dwc), (CONV_WIDTH, dwc))
    grads.update(zip(early_names, reduced_early))

    deltas, new_m, new_v = {}, {}, {}

    def apply_adamw(ks):
        flat = lambda a, k: a.reshape(-1, weights[k].shape[-1])
        done = {}
        for idx in _by_shape([flat(grads[k], k) for k in ks]):
            same = [ks[i] for i in idx]
            results = adamw([flat(weights[k], k) for k in same], [flat(grads[k], k) for k in same],
                            [flat(m_in[k], k) for k in same], [flat(v_in[k], k) for k in same], f"adamw_{same[0]}")
            for k, (d, mn, vn) in zip(same, results):
                shape = weights[k].shape
                grads[k] = grads[k].reshape(shape)
                deltas[k], new_m[k], new_v[k] = d.reshape(shape), mn.reshape(shape), vn.reshape(shape)
                done[k] = d
        return done

    done = apply_adamw([k for k in names if k not in late_names])
    gots_l = exchange_chips_wait(send_sems, recv_sems, wires_l, lands_l, [done[k] for k in early_names],
                                 "rs_exchange_chips_ffn1_wait")
    grads.update(zip(late_names, rs_finish(place, owns_l, gots_l, token, "late")))
    apply_adamw(late_names)
    for k in transposed:
        for group in (grads, deltas, new_m, new_v):
            group[k] = jnp.swapaxes(group[k], 1, 2)

    return (loss, dx0[None], *[grads[k] for k in names], *[deltas[k] for k in names],
            *[new_m[k] for k in names], *[new_v[k] for k in names])
```

```python
import functools

import jax
import jax.numpy as jnp
from jax import lax
from jax.experimental import pallas as pl
from jax.experimental.pallas import tpu as pltpu

F32 = jnp.float32
BF16 = jnp.bfloat16
MESH = pl.DeviceIdType.MESH

HEAD_DIM = 64
WINDOW = 128
CONV_WIDTH = 31
CONV_HALO = 32
ROPE_THETA = 10000.0
EPS = 1e-6
LN_EPS = 1e-5
NEG_INF = -1e30
N_CHIPS = 4
N_DEV = 8

ADAM_LR = 0.001
ADAM_B1 = 0.9
ADAM_B2 = 0.999
ADAM_EPS = 1e-08
ADAM_WD = 0.01
ADAM_STEP = 10

TM_FFN = 512
TM_FFN_FWD = 1024
TM_ROW = 256
TM_MIX = 512
TK_TN = 1024
TR_ELT = 256
VMEM_LIMIT = 56 * 1024 * 1024

NT_DIMS = (((1,), (1,)), ((), ()))
TN_DIMS = (((0,), (0,)), ((), ()))


def _row_tile(rows, cap):
    for t in range(min(cap, rows), 15, -1):
        if rows % t == 0 and t % 16 == 0:
            return t
    return rows


def _params(sem):
    return pltpu.CompilerParams(dimension_semantics=sem, vmem_limit_bytes=VMEM_LIMIT)


def _dot(a, b):
    return jnp.dot(a, b, preferred_element_type=F32)


def _dot_nt(a, b):
    return lax.dot_general(a, b, NT_DIMS, preferred_element_type=F32)


def _dot_tn(a, b):
    return lax.dot_general(a, b, TN_DIMS, preferred_element_type=F32)


def _split_rows(dot, a, b):
    m = a.shape[0] // 2
    return jnp.concatenate([dot(a[:m], b), dot(a[m:], b)], axis=0)


def _sigmoid(x):
    return jax.nn.sigmoid(x)


def _rms_scale(xv):
    return lax.rsqrt(jnp.mean(xv * xv, axis=-1, keepdims=True) + EPS)


def _rms_bwd(xv, nw, dh):
    r = _rms_scale(xv)
    dn = dh * nw
    dx = r * dn - xv * (r * r * r) * jnp.mean(dn * xv, axis=-1, keepdims=True)
    dnw = jnp.sum(dh * (xv * r), axis=0, keepdims=True)
    return dx, dnw


def _silu_grad(z, s):
    return s * (1.0 + z * (1.0 - s))


HBM_SPEC = pl.BlockSpec(memory_space=pl.ANY)


def _call_hosting(body, side, *, grid, in_specs, out_specs, out_shape, scratch_shapes, operands, name, aliases=None):
    params = _params(("arbitrary",) * len(grid))
    aliases = dict(aliases or {})
    if side is None:
        return pl.pallas_call(body, name=name, grid=grid, in_specs=in_specs, out_specs=out_specs, out_shape=out_shape,
                              scratch_shapes=scratch_shapes, input_output_aliases=aliases,
                              compiler_params=params)(*operands)
    n_in, n_out, n_scr = len(in_specs), len(out_shape), len(scratch_shapes)
    s_in, s_out = len(side["inputs"]), len(side["out_shapes"])

    steps = 1
    for extent in grid:
        steps *= extent

    def at_step(index):
        linear = pl.program_id(0)
        for a in range(1, len(grid)):
            linear = linear * grid[a] + pl.program_id(a)
        return linear == index

    def hosted(*refs):
        b = n_in + s_in
        c = b + n_out
        d = c + s_out
        e = d + n_scr
        src, dst, sems = refs[n_in:b], refs[c:d], refs[e:]

        @pl.when(at_step(0))
        def _():
            side["start"](src, dst, sems)

        if "relay" in side:
            @pl.when(at_step(min((3 * steps) // 4, steps - 1)))
            def _():
                side["relay"](src, dst, sems)

        body(*refs[:n_in], *refs[b:c], *refs[d:e])

        @pl.when(at_step(steps - 1))
        def _():
            side["finish"](src, dst, sems)

    return pl.pallas_call(
        hosted, name=name, grid=grid, in_specs=list(in_specs) + [HBM_SPEC] * s_in,
        out_specs=list(out_specs) + [HBM_SPEC] * s_out, out_shape=list(out_shape) + list(side["out_shapes"]),
        scratch_shapes=list(scratch_shapes) + list(side["sems"]),
        input_output_aliases={**aliases, **{n_in + a: n_out + b for a, b in side["aliases"].items()}},
        compiler_params=params)(*operands, *side["inputs"])


def ffn_fwd(x, nw, wg, wu, wd, name, side=None):
    T, D = x.shape
    NP, Fs, _ = wg.shape
    tm = min(TM_FFN_FWD, T)

    def body(x_ref, nw_ref, wg_ref, wu_ref, wd_ref, xo_ref, h_ref, g_ref, u_ref, acc_ref):
        j = pl.program_id(1)

        @pl.when(j == 0)
        def _():
            xv = x_ref[...]
            h_ref[...] = (xv * _rms_scale(xv) * nw_ref[...]).astype(BF16)
            acc_ref[...] = jnp.zeros_like(acc_ref)

        h = h_ref[...]
        g = _dot_nt(h, wg_ref[...])
        u = _dot_nt(h, wu_ref[...])
        a = (g * _sigmoid(g)) * u
        g_ref[...] = g.astype(BF16)
        u_ref[...] = u.astype(BF16)
        acc_ref[...] += _dot(a.astype(BF16), wd_ref[...])

        @pl.when(j == NP - 1)
        def _():
            xo_ref[...] = x_ref[...] + 0.5 * acc_ref[...]

    return _call_hosting(
        body, side, name=name, grid=(T // tm, NP),
        in_specs=[pl.BlockSpec((tm, D), lambda i, j: (i, 0)),
                  pl.BlockSpec((1, D), lambda i, j: (0, 0)),
                  pl.BlockSpec((None, Fs, D), lambda i, j: (j, 0, 0)),
                  pl.BlockSpec((None, Fs, D), lambda i, j: (j, 0, 0)),
                  pl.BlockSpec((None, Fs, D), lambda i, j: (j, 0, 0))],
        out_specs=[pl.BlockSpec((tm, D), lambda i, j: (i, 0)),
                   pl.BlockSpec((tm, D), lambda i, j: (i, 0)),
                   pl.BlockSpec((None, tm, Fs), lambda i, j: (j, i, 0)),
                   pl.BlockSpec((None, tm, Fs), lambda i, j: (j, i, 0))],
        out_shape=[jax.ShapeDtypeStruct((T, D), F32), jax.ShapeDtypeStruct((T, D), BF16),
                   jax.ShapeDtypeStruct((NP, T, Fs), BF16), jax.ShapeDtypeStruct((NP, T, Fs), BF16)],
        scratch_shapes=[pltpu.VMEM((tm, D), F32)],
        operands=(x, nw, wg, wu, wd))


def _ffn_bwd_piece(j, h, g, u, wg, wu, wd, dout, dh_in, dws_in, name, side, norm):
    T, D = h.shape
    NP, Fs, _ = wg.shape
    tm = min(TM_FFN, T)
    n_in = 7 + (dh_in is not None) + (2 if norm else 0) + (3 if dws_in else 0)

    def body(*refs):
        h_ref, g_ref, u_ref, wg_ref, wu_ref, wd_ref, do_ref = refs[:7]
        dhin_ref = refs[7] if dh_in is not None else None
        dh_ref, dwg_ref, dwu_ref, dwd_ref = refs[n_in:n_in + 4]

        @pl.when(pl.program_id(0) == 0)
        def _():
            dwg_ref[...] = jnp.zeros_like(dwg_ref)
            dwu_ref[...] = jnp.zeros_like(dwu_ref)
            dwd_ref[...] = jnp.zeros_like(dwd_ref)
            if norm:
                refs[n_in + 4][...] = jnp.zeros_like(refs[n_in + 4])

        dob = (0.5 * do_ref[...]).astype(BF16)
        da = _split_rows(_dot_nt, dob, wd_ref[...])
        gf = g_ref[...].astype(F32)
        uf = u_ref[...].astype(F32)
        s = _sigmoid(gf)
        act = gf * s
        dg = (da * uf * _silu_grad(gf, s)).astype(BF16)
        du = (da * act).astype(BF16)
        a = (act * uf).astype(BF16)
        dh = _dot(dg, wg_ref[...]) + _dot(du, wu_ref[...])
        dh = dh if dhin_ref is None else dhin_ref[...] + dh
        if norm:
            x_ref, nw_ref = refs[7 + (dh_in is not None):9 + (dh_in is not None)]
            dxn, dnw = _rms_bwd(x_ref[...], nw_ref[...], dh)
            dh_ref[...] = do_ref[...] + dxn
            refs[n_in + 4][...] += dnw
        else:
            dh_ref[...] = dh
        hb = h_ref[...]
        dwg_ref[...] += _dot_tn(dg, hb)
        dwu_ref[...] += _dot_tn(du, hb)
        dwd_ref[...] += _dot_tn(a, dob)

    rows = pl.BlockSpec((tm, D), lambda i: (i, 0))
    piece = pl.BlockSpec((None, tm, Fs), lambda i: (j, i, 0))
    slot = pl.BlockSpec((None, Fs, D), lambda i: (j, 0, 0), pipeline_mode=pl.Buffered(1))
    in_specs = [rows, piece, piece, slot, slot, slot, rows]
    operands = [h, g, u, wg, wu, wd, dout]
    aliases = {}
    if dh_in is not None:
        in_specs.append(rows)
        operands.append(dh_in)
    if norm:
        in_specs += [rows, pl.BlockSpec((1, D), lambda i: (0, 0))]
        operands += list(norm)
    if dws_in:
        aliases = {len(operands) + k: 1 + k for k in range(3)}
        in_specs += [HBM_SPEC] * 3
        operands += list(dws_in)
    out_specs = [rows, slot, slot, slot]
    out_shape = [jax.ShapeDtypeStruct((T, D), F32)] + [jax.ShapeDtypeStruct((NP, Fs, D), F32)] * 3
    if norm:
        out_specs.append(pl.BlockSpec((1, D), lambda i: (0, 0)))
        out_shape.append(jax.ShapeDtypeStruct((1, D), F32))
    return _call_hosting(body, side, name=name, grid=(T // tm,), in_specs=in_specs, out_specs=out_specs,
                         out_shape=out_shape, scratch_shapes=[], aliases=aliases, operands=tuple(operands))


def ffn_bwd(x, nw, h, g, u, wg, wu, wd, dout, name, side=None):
    NP = wg.shape[0]
    dh, dws, extra = None, None, []
    for j in range(NP):
        dh, *rest = _ffn_bwd_piece(j, h, g, u, wg, wu, wd, dout, dh, dws, f"{name}_{j}",
                                   side if j == 0 else None, (x, nw) if j == NP - 1 else None)
        dws, rest = rest[:3], rest[3:]
        if j == 0:
            extra = rest[1:] if NP == 1 else rest
    return (dh, *dws, rest[0], *extra)


def mix_in_fwd(x, nw, w_glu, w_qkv, w_gate, cs, sn, name):
    T, D = x.shape
    KV = (w_qkv.shape[1] - D) // 2
    tm = min(TM_MIX, T)

    def body(x_ref, nw_ref, wa_ref, wq_ref, wg_ref, cs_ref, sn_ref, h_ref, pa_ref, pg_ref, q_ref, k_ref, v_ref):
        xv = x_ref[...]
        h = (xv * _rms_scale(xv) * nw_ref[...]).astype(BF16)
        h_ref[...] = h
        pa_ref[...] = _dot(h, wa_ref[...])
        pg_ref[...] = _dot(h, wg_ref[...])
        qkv = _dot(h, wq_ref[...])
        cs_v, sn_v = cs_ref[...], sn_ref[...]
        q_ref[...] = _rope_chunks(qkv[:, :D], cs_v, sn_v, 1.0).astype(BF16)
        k_ref[...] = _rope_chunks(qkv[:, D:D + KV], cs_v, sn_v, 1.0).astype(BF16)
        v_ref[...] = qkv[:, D + KV:].astype(BF16)

    rows = lambda w: pl.BlockSpec((tm, w), lambda i: (i, 0))
    whole = lambda a: pl.BlockSpec(a.shape, lambda i: (0, 0), pipeline_mode=pl.Buffered(1))
    return pl.pallas_call(
        body, name=name, grid=(T // tm,),
        in_specs=[rows(D), whole(nw), whole(w_glu), whole(w_qkv), whole(w_gate), rows(128), rows(128)],
        out_specs=[rows(D), rows(2 * D), rows(2 * D), rows(D), rows(KV), rows(KV)],
        out_shape=[jax.ShapeDtypeStruct((T, D), BF16), jax.ShapeDtypeStruct((T, 2 * D), F32),
                   jax.ShapeDtypeStruct((T, 2 * D), F32), jax.ShapeDtypeStruct((T, D), BF16),
                   jax.ShapeDtypeStruct((T, KV), BF16), jax.ShapeDtypeStruct((T, KV), BF16)],
        compiler_params=_params(("parallel",)),
    )(x, nw, w_glu, w_qkv, w_gate, cs, sn)


def matmul_tn(pairs, tk, name):
    n = len(pairs)
    T = pairs[0][0].shape[0]
    tk = min(tk, T)

    def body(*refs):
        @pl.when(pl.program_id(0) == 0)
        def _():
            for a in range(n):
                refs[2 * n + a][...] = jnp.zeros_like(refs[2 * n + a])

        for a in range(n):
            refs[2 * n + a][...] += _dot_tn(refs[2 * a][...].astype(BF16), refs[2 * a + 1][...].astype(BF16))

    rows = lambda a: pl.BlockSpec((tk, a.shape[1]), lambda t: (t, 0))
    shapes = [(lhs.shape[1], rhs.shape[1]) for lhs, rhs in pairs]
    return pl.pallas_call(
        body, name=name, grid=(T // tk,),
        in_specs=[rows(a) for pair in pairs for a in pair],
        out_specs=[pl.BlockSpec(s, lambda t: (0, 0), pipeline_mode=pl.Buffered(1)) for s in shapes],
        out_shape=[jax.ShapeDtypeStruct(s, F32) for s in shapes],
        compiler_params=_params(("arbitrary",)),
    )(*[a for pair in pairs for a in pair])


def mix_in_bwd(dps, ws, x, nw, dres, name):
    T, D = x.shape
    tm = min(TM_MIX, T)
    n = len(dps)

    def body(*refs):
        dp_refs, w_refs = refs[:n], refs[n:2 * n]
        x_ref, nw_ref, dr_ref, dx_ref, dnw_ref = refs[2 * n:]

        @pl.when(pl.program_id(0) == 0)
        def _():
            dnw_ref[...] = jnp.zeros_like(dnw_ref)

        dh = _dot_nt(dp_refs[0][...], w_refs[0][...])
        for k in range(1, n):
            dh += _dot_nt(dp_refs[k][...], w_refs[k][...])
        dxn, dnw = _rms_bwd(x_ref[...], nw_ref[...], dh)
        dx_ref[...] = dr_ref[...] + dxn
        dnw_ref[...] += dnw

    in_specs = [pl.BlockSpec((tm, dp.shape[1]), lambda i: (i, 0)) for dp in dps]
    in_specs += [pl.BlockSpec(w.shape, lambda i: (0, 0), pipeline_mode=pl.Buffered(1)) for w in ws]
    in_specs += [pl.BlockSpec((tm, D), lambda i: (i, 0)), pl.BlockSpec((1, D), lambda i: (0, 0)),
                 pl.BlockSpec((tm, D), lambda i: (i, 0))]
    return pl.pallas_call(
        body, name=name, grid=(T // tm,), in_specs=in_specs,
        out_specs=[pl.BlockSpec((tm, D), lambda i: (i, 0)), pl.BlockSpec((1, D), lambda i: (0, 0))],
        out_shape=[jax.ShapeDtypeStruct((T, D), F32), jax.ShapeDtypeStruct((1, D), F32)],
        compiler_params=_params(("arbitrary",)),
    )(*dps, *ws, x, nw, dres)


def _layernorm_stats(c1):
    mu = jnp.mean(c1, axis=-1, keepdims=True)
    xc = c1 - mu
    rstd = lax.rsqrt(jnp.mean(xc * xc, axis=-1, keepdims=True) + LN_EPS)
    return xc * rstd, rstd


def _shifted_copies(src_ref, dst_ref):
    rows = dst_ref.shape[1]
    for b in range(1, 8):
        dst_ref[b - 1] = src_ref[pl.ds(b, rows), :]


def _shifted_rows(src_ref, shifted_ref, start, rows, cols):
    a8, b = divmod(start, 8)
    if b == 0:
        return src_ref[pl.ds(8 * a8, rows), cols]
    return shifted_ref[b - 1, pl.ds(8 * a8, rows), cols]


def conv_fwd(p_glu, dw_w, dw_b, ln_g, ln_b, name):
    T, D2 = p_glu.shape
    D = D2 // 2
    tm = min(TM_ROW, T)
    hb = tm // CONV_HALO

    def body(a_ref, b_ref, ah_ref, bh_ref, w_ref, wb_ref, g_ref, be_ref, c1_ref, c3_ref, e_ref, es_ref):
        i = pl.program_id(0)
        halo = ah_ref[...] * _sigmoid(bh_ref[...])
        e_ref[pl.ds(0, CONV_HALO), :] = jnp.where(i > 0, halo, 0.0)
        e_ref[pl.ds(CONV_HALO, tm), :] = a_ref[...] * _sigmoid(b_ref[...])
        _shifted_copies(e_ref, es_ref)
        off = CONV_HALO - (CONV_WIDTH - 1)

        def strip(s, carry):
            cols = pl.ds(pl.multiple_of(s * 128, 128), 128)
            acc = jnp.zeros((tm, 128), F32) + wb_ref[:, cols]
            for k in range(CONV_WIDTH):
                acc += w_ref[pl.ds(k, 1), cols] * _shifted_rows(e_ref, es_ref, off + k, tm, cols)
            c1_ref[:, cols] = acc
            return carry

        lax.fori_loop(0, D // 128, strip, 0)
        xhat, _ = _layernorm_stats(c1_ref[...])
        c2 = xhat * g_ref[...] + be_ref[...]
        c3_ref[...] = (c2 * _sigmoid(c2)).astype(BF16)

    row = pl.BlockSpec((1, D), lambda i: (0, 0))
    return pl.pallas_call(
        body, name=name, grid=(T // tm,),
        in_specs=[pl.BlockSpec((tm, D), lambda i: (i, 0)), pl.BlockSpec((tm, D), lambda i: (i, 1)),
                  pl.BlockSpec((CONV_HALO, D), lambda i: (jnp.maximum(i * hb - 1, 0), 0)),
                  pl.BlockSpec((CONV_HALO, D), lambda i: (jnp.maximum(i * hb - 1, 0), 1)),
                  pl.BlockSpec((CONV_HALO, D), lambda i: (0, 0)), row, row, row],
        out_specs=[pl.BlockSpec((tm, D), lambda i: (i, 0)), pl.BlockSpec((tm, D), lambda i: (i, 0))],
        out_shape=[jax.ShapeDtypeStruct((T, D), F32), jax.ShapeDtypeStruct((T, D), BF16)],
        scratch_shapes=[pltpu.VMEM((tm + CONV_HALO, D), F32), pltpu.VMEM((7, tm + CONV_HALO - 8, D), F32)],
        compiler_params=_params(("parallel",)),
    )(p_glu, p_glu, p_glu, p_glu, dw_w, dw_b, ln_g, ln_b)


def conv_bwd(p_glu, dc1, dw_w, name, side=None):
    T, D2 = p_glu.shape
    D = D2 // 2
    tm = min(TM_ROW, T)
    hb = tm // CONV_HALO
    last = T // CONV_HALO - 1
    nblk = T // tm

    def body(a_ref, b_ref, ah_ref, bh_ref, d_ref, dn_ref, w_ref, dp_ref, dw_ref, e_ref, f_ref, es_ref, fs_ref):
        i = pl.program_id(0)

        @pl.when(i == 0)
        def _():
            dw_ref[...] = jnp.zeros_like(dw_ref)

        halo = ah_ref[...] * _sigmoid(bh_ref[...])
        e_ref[pl.ds(0, CONV_HALO), :] = jnp.where(i > 0, halo, 0.0)
        e_ref[pl.ds(CONV_HALO, tm), :] = a_ref[...] * _sigmoid(b_ref[...])
        f_ref[pl.ds(0, tm), :] = d_ref[...]
        f_ref[pl.ds(tm, CONV_HALO), :] = jnp.where(i < nblk - 1, dn_ref[...], 0.0)
        _shifted_copies(e_ref, es_ref)
        _shifted_copies(f_ref, fs_ref)
        off = CONV_HALO - (CONV_WIDTH - 1)

        def strip(s, carry):
            cols = pl.ds(pl.multiple_of(s * 128, 128), 128)
            d = d_ref[:, cols]
            dc0 = jnp.zeros((tm, 128), F32)
            for k in range(CONV_WIDTH):
                dw_ref[pl.ds(k, 1), cols] += jnp.sum(d * _shifted_rows(e_ref, es_ref, off + k, tm, cols),
                                                     axis=0, keepdims=True)
                dc0 += w_ref[pl.ds(k, 1), cols] * _shifted_rows(f_ref, fs_ref, CONV_WIDTH - 1 - k, tm, cols)
            a = a_ref[:, cols]
            sb = _sigmoid(b_ref[:, cols])
            dp_ref[:, cols] = (dc0 * sb).astype(BF16)
            dp_ref[:, pl.ds(pl.multiple_of(D + s * 128, 128), 128)] = (dc0 * a * sb * (1.0 - sb)).astype(BF16)
            return carry

        lax.fori_loop(0, D // 128, strip, 0)

    return _call_hosting(
        body, side, name=name, grid=(nblk,),
        in_specs=[pl.BlockSpec((tm, D), lambda i: (i, 0)), pl.BlockSpec((tm, D), lambda i: (i, 1)),
                  pl.BlockSpec((CONV_HALO, D), lambda i: (jnp.maximum(i * hb - 1, 0), 0)),
                  pl.BlockSpec((CONV_HALO, D), lambda i: (jnp.maximum(i * hb - 1, 0), 1)),
                  pl.BlockSpec((tm, D), lambda i: (i, 0)),
                  pl.BlockSpec((CONV_HALO, D), lambda i: (jnp.minimum((i + 1) * hb, last), 0)),
                  pl.BlockSpec((CONV_HALO, D), lambda i: (0, 0))],
        out_specs=[pl.BlockSpec((tm, D2), lambda i: (i, 0)), pl.BlockSpec((CONV_HALO, D), lambda i: (0, 0))],
        out_shape=[jax.ShapeDtypeStruct((T, D2), BF16), jax.ShapeDtypeStruct((CONV_HALO, D), F32)],
        scratch_shapes=[pltpu.VMEM((tm + CONV_HALO, D), F32), pltpu.VMEM((tm + CONV_HALO, D), F32),
                        pltpu.VMEM((7, tm + CONV_HALO - 8, D), F32), pltpu.VMEM((7, tm + CONV_HALO - 8, D), F32)],
        operands=(p_glu, p_glu, p_glu, p_glu, dc1, dc1, dw_w))


def _rot_half(x):
    lane = lax.broadcasted_iota(jnp.int32, x.shape, 1)
    first = (lane % HEAD_DIM) < HEAD_DIM // 2
    return jnp.where(first, pltpu.roll(x, 128 - HEAD_DIM // 2, 1), pltpu.roll(x, HEAD_DIM // 2, 1))


def _rope_chunks(x, cs, sn, sign):
    outs = []
    for c in range(x.shape[1] // 128):
        xc = x[:, c * 128:(c + 1) * 128]
        outs.append(xc * cs + sign * (_rot_half(xc) * sn))
    return outs[0] if len(outs) == 1 else jnp.concatenate(outs, axis=1)


def rope_bwd(dq, dk, dv, cs, sn, name):
    T, D = dq.shape
    KV = dk.shape[1]
    tm = min(TM_ROW, T)

    def body(dq_ref, dk_ref, dv_ref, cs_ref, sn_ref, o_ref):
        cs_v, sn_v = cs_ref[...], sn_ref[...]
        o_ref[:, pl.ds(0, D)] = _rope_chunks(dq_ref[...], cs_v, sn_v, -1.0).astype(BF16)
        o_ref[:, pl.ds(D, KV)] = _rope_chunks(dk_ref[...], cs_v, sn_v, -1.0).astype(BF16)
        o_ref[:, pl.ds(D + KV, KV)] = dv_ref[...].astype(BF16)

    tab = pl.BlockSpec((tm, 128), lambda i: (i, 0))
    return pl.pallas_call(
        body, name=name, grid=(T // tm,),
        in_specs=[pl.BlockSpec((tm, D), lambda i: (i, 0)), pl.BlockSpec((tm, KV), lambda i: (i, 0)),
                  pl.BlockSpec((tm, KV), lambda i: (i, 0)), tab, tab],
        out_specs=pl.BlockSpec((tm, D + 2 * KV), lambda i: (i, 0)),
        out_shape=jax.ShapeDtypeStruct((T, D + 2 * KV), BF16),
        compiler_params=_params(("parallel",)),
    )(dq, dk, dv, cs, sn)


def _lane_lo():
    return lax.broadcasted_iota(jnp.int32, (1, 128), 1) < HEAD_DIM


def _band_mask(i, reps):
    shape = (reps * WINDOW, 2 * WINDOW)
    qi = lax.broadcasted_iota(jnp.int32, shape, 0) % WINDOW
    cj = lax.broadcasted_iota(jnp.int32, shape, 1)
    rel = qi - cj + WINDOW
    return (rel >= 0) & (rel < WINDOW) & ((i > 0) | (cj >= WINDOW))


def _stack_pairs(ref, first, n):
    parts = [ref[:, pl.ds((first + p) * 128, 128)] for p in range(n)]
    return parts[0] if n == 1 else jnp.concatenate(parts, axis=0)


def _pair_rows(n):
    return lax.broadcasted_iota(jnp.int32, (n * WINDOW, 1), 0) // WINDOW


def _per_pair_column(values, n):
    rows = _pair_rows(n)
    col = jnp.zeros((n * WINDOW, 1), F32) + values[0]
    for p in range(1, n):
        col = jnp.where(rows == p, values[p], col)
    return col


def _kv_lo_hi(x2, g):
    pair, half = divmod(g, 2)
    lo = _lane_lo()
    xg = x2[:, pair * 128:(pair + 1) * 128].astype(F32)
    xg = jnp.where(lo if half == 0 else ~lo, xg, 0.0)
    sw = pltpu.roll(xg, HEAD_DIM, 1)
    x_lo, x_hi = (xg, sw) if half == 0 else (sw, xg)
    return x_lo.astype(BF16), x_hi.astype(BF16)


def _softmax_sink(s, allowed, sink):
    s = jnp.where(allowed, s * (HEAD_DIM ** -0.5), NEG_INF)
    m = jnp.maximum(jnp.max(s, axis=-1, keepdims=True), sink)
    p = jnp.exp(s - m)
    es = jnp.exp(sink - m)
    inv = 1.0 / (jnp.sum(p, axis=-1, keepdims=True) + es)
    return p * inv, es * inv


def attn_fwd(qr, kr, vb, sinks, name, side=None):
    T, D = qr.shape
    KV = kr.shape[1]
    n_kv = KV // HEAD_DIM
    group = (D // HEAD_DIM) // n_kv
    nb = T // WINDOW

    npair = group // 2

    def body(sink_ref, q_ref, kp_ref, kc_ref, vp_ref, vc_ref, o_ref):
        i = pl.program_id(0)
        allowed = _band_mask(i, npair)
        k2 = jnp.concatenate([kp_ref[...], kc_ref[...]], axis=0)
        v2 = jnp.concatenate([vp_ref[...], vc_ref[...]], axis=0)
        outs = [None] * (D // 128)
        for g in range(n_kv):
            k_lo, k_hi = _kv_lo_hi(k2, g)
            v_lo, v_hi = _kv_lo_hi(v2, g)
            first = (g * group) // 2
            q = _stack_pairs(q_ref, first, npair)
            sink_e = _per_pair_column([sink_ref[0, g * group + 2 * p] for p in range(npair)], npair)
            sink_o = _per_pair_column([sink_ref[0, g * group + 2 * p + 1] for p in range(npair)], npair)
            pe, _ = _softmax_sink(_dot_nt(q, k_lo), allowed, sink_e)
            po, _ = _softmax_sink(_dot_nt(q, k_hi), allowed, sink_o)
            o = _dot(pe.astype(BF16), v_lo) + _dot(po.astype(BF16), v_hi)
            for p in range(npair):
                outs[first + p] = o[p * WINDOW:(p + 1) * WINDOW]
        o_ref[...] = jnp.concatenate(outs, axis=1).astype(BF16)

    prev = lambda i: (jnp.maximum(i - 1, 0), 0)
    cur = lambda i: (i, 0)
    return _call_hosting(
        body, side, name=name, grid=(nb,),
        in_specs=[pl.BlockSpec(memory_space=pltpu.SMEM),
                  pl.BlockSpec((WINDOW, D), cur),
                  pl.BlockSpec((WINDOW, KV), prev), pl.BlockSpec((WINDOW, KV), cur),
                  pl.BlockSpec((WINDOW, KV), prev), pl.BlockSpec((WINDOW, KV), cur)],
        out_specs=[pl.BlockSpec((WINDOW, D), cur)],
        out_shape=[jax.ShapeDtypeStruct((T, D), BF16)],
        scratch_shapes=[], operands=(sinks, qr, kr, kr, vb, vb))


def attn_bwd(qr, kr, vb, o, do, sinks, name, side=None):
    T, D = qr.shape
    KV = kr.shape[1]
    n_heads = D // HEAD_DIM
    n_kv = KV // HEAD_DIM
    group = n_heads // n_kv
    nb = T // WINDOW
    npair = group // 2
    scale = HEAD_DIM ** -0.5

    def body(sink_ref, q_ref, kp_ref, kc_ref, vp_ref, vc_ref, o_ref, do_ref,
             dq_ref, dk_ref, dv_ref, ds_ref, ck_ref, cv_ref):
        i = pl.program_id(0)
        lo = _lane_lo()

        @pl.when(i == 0)
        def _():
            ck_ref[...] = jnp.zeros_like(ck_ref)
            cv_ref[...] = jnp.zeros_like(cv_ref)
            ds_ref[...] = jnp.zeros_like(ds_ref)

        @pl.when(i < nb)
        def _():
            allowed = _band_mask(i, npair)
            rows = _pair_rows(npair)
            k2 = jnp.concatenate([kp_ref[...], kc_ref[...]], axis=0)
            v2 = jnp.concatenate([vp_ref[...], vc_ref[...]], axis=0)
            lane = lax.broadcasted_iota(jnp.int32, (1, 128), 1)
            dsink = jnp.zeros((1, 128), F32)
            dq_out = [None] * (D // 128)
            dk_pairs = [jnp.zeros((2 * WINDOW, 128), F32) for _ in range(KV // 128)]
            dv_pairs = [jnp.zeros((2 * WINDOW, 128), F32) for _ in range(KV // 128)]
            for g in range(n_kv):
                k_lo, k_hi = _kv_lo_hi(k2, g)
                v_lo, v_hi = _kv_lo_hi(v2, g)
                first = (g * group) // 2
                q = _stack_pairs(q_ref, first, npair)
                dop = _stack_pairs(do_ref, first, npair)
                dd = dop.astype(F32) * _stack_pairs(o_ref, first, npair).astype(F32)
                dq = jnp.zeros((npair * WINDOW, 128), F32)
                dkg = jnp.zeros((2 * WINDOW, 128), F32)
                dvg = jnp.zeros((2 * WINDOW, 128), F32)
                for parity, k_h, v_h, sel in ((0, k_lo, v_lo, lo), (1, k_hi, v_hi, ~lo)):
                    heads = [g * group + 2 * p + parity for p in range(npair)]
                    sink = _per_pair_column([sink_ref[0, h] for h in heads], npair)
                    p_, ps = _softmax_sink(_dot_nt(q, k_h), allowed, sink)
                    delta = jnp.sum(jnp.where(sel, dd, 0.0), axis=-1, keepdims=True)
                    dsc = (p_ * (_dot_nt(dop, v_h) - delta)).astype(BF16)
                    sd = -ps * delta
                    for p, h in enumerate(heads):
                        dsink += jnp.where(lane == h, jnp.sum(jnp.where(rows == p, sd, 0.0)), 0.0)
                    dq += _dot(dsc, k_h)
                    dkg += jnp.where(sel, _dot_tn(dsc, q), 0.0)
                    dvg += jnp.where(sel, _dot_tn(p_.astype(BF16), dop), 0.0)
                for p in range(npair):
                    dq_out[first + p] = dq[p * WINDOW:(p + 1) * WINDOW]
                pair, half = divmod(g, 2)
                keep = lo if half == 0 else ~lo
                dk_pairs[pair] += jnp.where(keep, dkg + pltpu.roll(dkg, HEAD_DIM, 1), 0.0) * scale
                dv_pairs[pair] += jnp.where(keep, dvg + pltpu.roll(dvg, HEAD_DIM, 1), 0.0)
            dq_ref[...] = jnp.concatenate(dq_out, axis=1) * scale
            dk2 = dk_pairs[0] if len(dk_pairs) == 1 else jnp.concatenate(dk_pairs, axis=1)
            dv2 = dv_pairs[0] if len(dv_pairs) == 1 else jnp.concatenate(dv_pairs, axis=1)
            dk_ref[...] = ck_ref[...] + dk2[:WINDOW]
            dv_ref[...] = cv_ref[...] + dv2[:WINDOW]
            ck_ref[...] = dk2[WINDOW:]
            cv_ref[...] = dv2[WINDOW:]
            ds_ref[pl.ds(0, 1), :] += dsink

        @pl.when(i == nb)
        def _():
            dk_ref[...] = ck_ref[...]
            dv_ref[...] = cv_ref[...]

    prev = lambda i: (jnp.maximum(i - 1, 0), 0)
    cur = lambda i: (jnp.minimum(i, nb - 1), 0)
    prevc = lambda i: (jnp.maximum(jnp.minimum(i, nb - 1) - 1, 0), 0)
    return _call_hosting(
        body, side, name=name, grid=(nb + 1,),
        in_specs=[pl.BlockSpec(memory_space=pltpu.SMEM),
                  pl.BlockSpec((WINDOW, D), cur),
                  pl.BlockSpec((WINDOW, KV), prevc), pl.BlockSpec((WINDOW, KV), cur),
                  pl.BlockSpec((WINDOW, KV), prevc), pl.BlockSpec((WINDOW, KV), cur),
                  pl.BlockSpec((WINDOW, D), cur), pl.BlockSpec((WINDOW, D), cur)],
        out_specs=[pl.BlockSpec((WINDOW, D), cur), pl.BlockSpec((WINDOW, KV), prev),
                   pl.BlockSpec((WINDOW, KV), prev), pl.BlockSpec((8, 128), lambda i: (0, 0))],
        out_shape=[jax.ShapeDtypeStruct((T, D), F32), jax.ShapeDtypeStruct((T, KV), F32),
                   jax.ShapeDtypeStruct((T, KV), F32), jax.ShapeDtypeStruct((8, 128), F32)],
        scratch_shapes=[pltpu.VMEM((WINDOW, KV), F32), pltpu.VMEM((WINDOW, KV), F32)],
        operands=(sinks, qr, kr, kr, vb, vb, o, do))


def merge_fwd(x, c3, o, p_gate, gate_b, w_proj, w_o, w_out, name):
    T, D = x.shape
    tm = min(TM_MIX, T)

    def body(x_ref, c3_ref, o_ref, gc_ref, ga_ref, bc_ref, ba_ref, wp_ref, wo_ref, wout_ref,
             xo_ref, co_ref, ao_ref, mg_ref):
        conv_out = _dot(c3_ref[...], wp_ref[...])
        attn_out = _dot(o_ref[...], wo_ref[...])
        merged = (_sigmoid(gc_ref[...] + bc_ref[...]) * conv_out
                  + _sigmoid(ga_ref[...] + ba_ref[...]) * attn_out).astype(BF16)
        co_ref[...] = conv_out.astype(BF16)
        ao_ref[...] = attn_out.astype(BF16)
        mg_ref[...] = merged
        xo_ref[...] = x_ref[...] + _dot(merged, wout_ref[...])

    blk = lambda j: pl.BlockSpec((tm, D), lambda i: (i, j))
    row = lambda j: pl.BlockSpec((1, D), lambda i: (0, j))
    mat = pl.BlockSpec((D, D), lambda i: (0, 0), pipeline_mode=pl.Buffered(1))
    return pl.pallas_call(
        body, name=name, grid=(T // tm,),
        in_specs=[blk(0), blk(0), blk(0), blk(0), blk(1), row(0), row(1), mat, mat, mat],
        out_specs=[blk(0), blk(0), blk(0), blk(0)],
        out_shape=[jax.ShapeDtypeStruct((T, D), F32)] + [jax.ShapeDtypeStruct((T, D), BF16)] * 3,
        compiler_params=_params(("parallel",)),
    )(x, c3, o, p_gate, p_gate, gate_b, gate_b, w_proj, w_o, w_out)


def merge_bwd(dx, p_gate, gate_b, conv_out, attn_out, c1, ln_g, ln_b, w_proj, w_o, w_out, name, side=None):
    T, D = dx.shape
    tm = min(TM_ROW, T)

    def body(dx_ref, gc_ref, ga_ref, bc_ref, ba_ref, co_ref, ao_ref, c1_ref, g_ref, be_ref,
             wp_ref, wo_ref, wout_ref, dgt_ref, dco_ref, dao_ref, do_ref, dc1_ref, sm_ref):
        @pl.when(pl.program_id(0) == 0)
        def _():
            sm_ref[...] = jnp.zeros_like(sm_ref)

        dm = _dot_nt(dx_ref[...].astype(BF16), wout_ref[...])
        sc = _sigmoid(gc_ref[...] + bc_ref[...])
        sa = _sigmoid(ga_ref[...] + ba_ref[...])
        dco = (dm * sc).astype(BF16)
        dao = (dm * sa).astype(BF16)
        dgc = dm * co_ref[...].astype(F32) * sc * (1.0 - sc)
        dga = dm * ao_ref[...].astype(F32) * sa * (1.0 - sa)
        dgt_ref[:, pl.ds(0, D)] = dgc.astype(BF16)
        dgt_ref[:, pl.ds(D, D)] = dga.astype(BF16)
        dco_ref[...] = dco
        dao_ref[...] = dao
        do_ref[...] = _dot_nt(dao, wo_ref[...]).astype(BF16)
        dc3 = _dot_nt(dco, wp_ref[...])
        xhat, rstd = _layernorm_stats(c1_ref[...])
        c2 = xhat * g_ref[...] + be_ref[...]
        dc2 = dc3 * _silu_grad(c2, _sigmoid(c2))
        dxh = dc2 * g_ref[...]
        dc1 = rstd * (dxh - jnp.mean(dxh, axis=-1, keepdims=True)
                      - xhat * jnp.mean(dxh * xhat, axis=-1, keepdims=True))
        dc1_ref[...] = dc1
        colsum = lambda v: jnp.sum(v, axis=0, keepdims=True)
        for r, (left, right) in enumerate(((dgc, dga), (dc2 * xhat, dc2), (dc1, None))):
            sm_ref[pl.ds(r, 1), pl.ds(0, D)] += colsum(left)
            if right is not None:
                sm_ref[pl.ds(r, 1), pl.ds(D, D)] += colsum(right)

    blk = lambda j: pl.BlockSpec((tm, D), lambda i: (i, j))
    row = lambda j: pl.BlockSpec((1, D), lambda i: (0, j))
    mat = pl.BlockSpec((D, D), lambda i: (0, 0))
    return _call_hosting(
        body, side, name=name, grid=(T // tm,),
        in_specs=[blk(0), blk(0), blk(1), row(0), row(1), blk(0), blk(0), blk(0), row(0), row(0), mat, mat, mat],
        out_specs=[pl.BlockSpec((tm, 2 * D), lambda i: (i, 0)), blk(0), blk(0), blk(0), blk(0),
                   pl.BlockSpec((8, 2 * D), lambda i: (0, 0))],
        out_shape=[jax.ShapeDtypeStruct((T, 2 * D), BF16)] + [jax.ShapeDtypeStruct((T, D), BF16)] * 3
                  + [jax.ShapeDtypeStruct((T, D), F32), jax.ShapeDtypeStruct((8, 2 * D), F32)],
        scratch_shapes=[],
        operands=(dx, p_gate, p_gate, gate_b, gate_b, conv_out, attn_out, c1, ln_g, ln_b, w_proj, w_o, w_out))


def loss_head(x, nw, target, name):
    T, D = x.shape
    tm = min(TM_ROW, T)

    def body(x_ref, nw_ref, t_ref, dx_ref, sm_ref):
        @pl.when(pl.program_id(0) == 0)
        def _():
            sm_ref[...] = jnp.zeros_like(sm_ref)

        xv = x_ref[...]
        err = xv * _rms_scale(xv) * nw_ref[...] - t_ref[...]
        loss = 0.5 * jnp.sum(jnp.mean(err * err, axis=-1, keepdims=True))
        dxn, dnw = _rms_bwd(xv, nw_ref[...], err * (1.0 / D))
        dx_ref[...] = dxn
        sm_ref[pl.ds(0, 1), :] += dnw
        sm_ref[pl.ds(1, 1), :] += jnp.zeros((1, D), F32) + loss

    return pl.pallas_call(
        body, name=name, grid=(T // tm,),
        in_specs=[pl.BlockSpec((tm, D), lambda i: (i, 0)), pl.BlockSpec((1, D), lambda i: (0, 0)),
                  pl.BlockSpec((tm, D), lambda i: (i, 0))],
        out_specs=[pl.BlockSpec((tm, D), lambda i: (i, 0)), pl.BlockSpec((8, D), lambda i: (0, 0))],
        out_shape=[jax.ShapeDtypeStruct((T, D), F32), jax.ShapeDtypeStruct((8, D), F32)],
        compiler_params=_params(("arbitrary",)),
    )(x, nw, target)


def _by_shape(arrays):
    groups = {}
    for k, a in enumerate(arrays):
        groups.setdefault(a.shape, []).append(k)
    return list(groups.values())


def adamw(ws, gs, ms, vs, name):
    n = len(ws)
    R, C = ws[0].shape
    tr = _row_tile(R, TR_ELT)

    def body(*refs):
        for a in range(n):
            w_ref, g_ref, m_ref, v_ref, d_ref, mo_ref, vo_ref = (refs[k * n + a] for k in range(7))
            gv = g_ref[...]
            mn = ADAM_B1 * m_ref[...] + (1.0 - ADAM_B1) * gv
            vn = ADAM_B2 * v_ref[...] + (1.0 - ADAM_B2) * (gv * gv)
            m_hat = mn / (1.0 - ADAM_B1 ** ADAM_STEP)
            v_hat = vn / (1.0 - ADAM_B2 ** ADAM_STEP)
            d_ref[...] = -ADAM_LR * (m_hat / (jnp.sqrt(v_hat) + ADAM_EPS) + ADAM_WD * w_ref[...])
            mo_ref[...] = mn
            vo_ref[...] = vn

    spec = pl.BlockSpec((tr, C), lambda i: (i, 0))
    outs = pl.pallas_call(
        body, name=name, grid=(R // tr,), in_specs=[spec] * (4 * n), out_specs=[spec] * (3 * n),
        out_shape=[jax.ShapeDtypeStruct((R, C), F32)] * (3 * n),
        compiler_params=_params(("parallel",)),
    )(*ws, *gs, *ms, *vs)
    return [(outs[a], outs[n + a], outs[2 * n + a]) for a in range(n)]


def _place():
    return lax.axis_index("x"), lax.axis_index("y"), lax.axis_index("c")


def place_shards(place, ws, dtype, name):
    n = len(ws)
    R, C = ws[0].shape
    tr = _row_tile(R, TR_ELT)

    def body(pc_ref, *refs):
        for a in range(n):
            refs[n + a][...] = refs[a][...].astype(dtype)

    return pl.pallas_call(
        body, name=name,
        grid_spec=pltpu.PrefetchScalarGridSpec(
            num_scalar_prefetch=1, grid=(R // tr,),
            in_specs=[pl.BlockSpec((tr, C), lambda r, pc: (r, 0))] * n,
            out_specs=[pl.BlockSpec((None, tr, C), lambda r, pc: (pc[0], r, 0))] * n),
        out_shape=[jax.ShapeDtypeStruct((N_CHIPS, R, C), dtype)] * n,
        compiler_params=_params(("arbitrary",)),
    )(place, *ws)


def gather_side(shards, small):
    n, ns = len(shards), len(small)

    def ici_copy(dst, sems, k, j, x, y, c, sending):
        px, py = x ^ (j >> 1), y ^ (j & 1)
        slot = 2 * x + y if sending else 2 * px + py
        half = dst[k].shape[1] // 2
        part = dst[k].at[slot, pl.ds(c * half, half)] if k < n else dst[k].at[slot]
        return pltpu.make_async_remote_copy(part, part, sems[0].at[3 * k + j - 1], sems[1].at[3 * k + j - 1],
                                            device_id=(px, py, c), device_id_type=MESH)

    def d2d_copy(dst, sems, k, j, x, y, c, sending):
        half = dst[k].shape[1] // 2
        part = dst[k].at[2 * (x ^ (j >> 1)) + (y ^ (j & 1)), pl.ds((c if sending else 1 - c) * half, half)]
        return pltpu.make_async_remote_copy(part, part, sems[2].at[3 * k + j - 1], sems[3].at[3 * k + j - 1],
                                            device_id=(x, y, 1 - c), device_id_type=MESH)

    def start(src, dst, sems):
        x, y, c = _place()
        for k in range(n + ns):
            for j in (1, 2, 3):
                ici_copy(dst, sems, k, j, x, y, c, True).start()

    def relay(src, dst, sems):
        x, y, c = _place()
        for k in range(n + ns):
            for j in (1, 2, 3):
                ici_copy(dst, sems, k, j, x, y, c, False).wait_recv()
                if k < n:
                    d2d_copy(dst, sems, k, j, x, y, c, True).start()

    def finish(src, dst, sems):
        x, y, c = _place()
        for k in range(n):
            for j in (1, 2, 3):
                d2d_copy(dst, sems, k, j, x, y, c, False).wait_recv()
        for k in range(n + ns):
            for j in (1, 2, 3):
                ici_copy(dst, sems, k, j, x, y, c, True).wait_send()
                if k < n:
                    d2d_copy(dst, sems, k, j, x, y, c, True).wait_send()

    arrays = list(shards) + list(small)
    return dict(inputs=arrays, out_shapes=[jax.ShapeDtypeStruct(a.shape, a.dtype) for a in arrays],
                aliases={k: k for k in range(n + ns)},
                sems=[pltpu.SemaphoreType.DMA((3 * (n + ns),)), pltpu.SemaphoreType.DMA((3 * (n + ns),)),
                      pltpu.SemaphoreType.DMA((3 * n,)), pltpu.SemaphoreType.DMA((3 * n,))],
                start=start, relay=relay, finish=finish)


def run_side(side, name):
    n_in, n_out = len(side["inputs"]), len(side["out_shapes"])

    def body(*refs):
        src, dst, sems = refs[:n_in], refs[n_in:n_in + n_out], refs[n_in + n_out:]
        side["start"](src, dst, sems)
        if "relay" in side:
            side["relay"](src, dst, sems)
        side["finish"](src, dst, sems)

    return pl.pallas_call(
        body, name=name, in_specs=[HBM_SPEC] * n_in, out_specs=[HBM_SPEC] * n_out,
        out_shape=side["out_shapes"], input_output_aliases=side["aliases"], scratch_shapes=side["sems"],
    )(*side["inputs"])


def allreduce_small(block):
    R, C = block.shape

    def body(x_ref, out_ref, all_ref, send_sems, recv_sems, local_sem):
        x, y, c = _place()
        me, sibling = (x, y, c), (x, y, 1 - c)
        chips = [(1 - x, y), (x, 1 - y), (1 - x, 1 - y)]

        def slot(px, py, pc):
            return all_ref.at[4 * px + 2 * py + pc]

        def copy(k, block_of, to, src=None):
            return pltpu.make_async_remote_copy(
                src_ref=slot(*block_of) if src is None else src, dst_ref=slot(*block_of),
                send_sem=send_sems.at[k], recv_sem=recv_sems.at[k], device_id=to, device_id_type=MESH)

        mine = pltpu.make_async_copy(x_ref, slot(*me), local_sem)
        mine.start()
        first = [copy(0, me, sibling, src=x_ref)]
        first += [copy(1 + j, me, (*chip, c), src=x_ref) for j, chip in enumerate(chips)]
        for cp in first:
            cp.start()
        passed = [copy(4 + j, (*chip, c), sibling) for j, chip in enumerate(chips)]
        for j, chip in enumerate(chips):
            copy(1 + j, (*chip, c), me).wait_recv()
            passed[j].start()
        copy(0, sibling, me).wait_recv()
        for j, chip in enumerate(chips):
            copy(4 + j, (*chip, 1 - c), me).wait_recv()
        for cp in first + passed:
            cp.wait_send()
        mine.wait()
        total = all_ref[0]
        for d in range(1, N_DEV):
            total = total + all_ref[d]
        out_ref[...] = total

    return pl.pallas_call(
        body, name="allreduce_small",
        in_specs=[pl.BlockSpec(memory_space=pltpu.VMEM)], out_specs=pl.BlockSpec(memory_space=pltpu.VMEM),
        out_shape=jax.ShapeDtypeStruct((R, C), F32),
        scratch_shapes=[pltpu.VMEM((N_DEV, R, C), F32), pltpu.SemaphoreType.DMA((7,)),
                        pltpu.SemaphoreType.DMA((7,)), pltpu.SemaphoreType.DMA],
        compiler_params=pltpu.CompilerParams(vmem_limit_bytes=VMEM_LIMIT),
    )(block)


def exchange_siblings_side(grads):
    n = len(grads)

    def copies(src, dst, sems):
        x, y, c = _place()
        for k in range(n):
            half = src[k].shape[1] // 2
            yield pltpu.make_async_remote_copy(src[k].at[:, pl.ds((1 - c) * half, half)], dst[k],
                                               sems[0].at[k], sems[1].at[k],
                                               device_id=(x, y, 1 - c), device_id_type=MESH)

    def start(src, dst, sems):
        for cp in copies(src, dst, sems):
            cp.start()

    def finish(src, dst, sems):
        for cp in copies(src, dst, sems):
            cp.wait()

    return dict(inputs=list(grads), aliases={},
                out_shapes=[jax.ShapeDtypeStruct((N_CHIPS, g.shape[1] // 2, g.shape[2]), F32) for g in grads],
                sems=[pltpu.SemaphoreType.DMA((n,)), pltpu.SemaphoreType.DMA((n,))], start=start, finish=finish)


def rs_chip_sum(place, grads, sibs, name):
    n = len(grads)
    NP, R, C = grads[0].shape
    half = R // 2
    tr = _row_tile(half, TR_ELT)
    nr = half // tr

    def body(pc_ref, *refs):
        q = pl.program_id(1)
        for a in range(n):
            g_ref, s_ref, wire_ref, own_ref = (refs[k * n + a] for k in range(4))
            total = g_ref[...] + s_ref[...]
            wire_ref[...] = total.astype(BF16)

            @pl.when(q == pc_ref[0])
            def _():
                own_ref[...] = total

    outs = pl.pallas_call(
        body, name=name,
        grid_spec=pltpu.PrefetchScalarGridSpec(
            num_scalar_prefetch=1, grid=(nr, NP),
            in_specs=[pl.BlockSpec((None, tr, C), lambda r, q, pc: (q, pc[1] * nr + r, 0))] * n
                     + [pl.BlockSpec((None, tr, C), lambda r, q, pc: (q, r, 0))] * n,
            out_specs=[pl.BlockSpec((None, tr, C), lambda r, q, pc: (q, r, 0))] * n
                      + [pl.BlockSpec((tr, C), lambda r, q, pc: (r, 0))] * n),
        out_shape=[jax.ShapeDtypeStruct((NP, half, C), BF16)] * n + [jax.ShapeDtypeStruct((half, C), F32)] * n,
        compiler_params=_params(("arbitrary", "arbitrary")),
    )(place, *grads, *sibs)
    return outs[:n], outs[n:]


def exchange_chips_side(wires):
    n = len(wires)

    def copies(src, dst, sems):
        x, y, c = _place()
        for k in range(n):
            for j in (1, 2, 3):
                qx, qy = x ^ (j >> 1), y ^ (j & 1)
                yield pltpu.make_async_remote_copy(src[k].at[2 * qx + qy], dst[k].at[2 * x + y],
                                                   sems[0].at[3 * k + j - 1], sems[1].at[3 * k + j - 1],
                                                   device_id=(qx, qy, c), device_id_type=MESH)

    def start(src, dst, sems):
        for cp in copies(src, dst, sems):
            cp.start()

    def finish(src, dst, sems):
        for cp in copies(src, dst, sems):
            cp.wait()

    return dict(inputs=list(wires), out_shapes=[jax.ShapeDtypeStruct(w.shape, BF16) for w in wires], aliases={},
                sems=[pltpu.SemaphoreType.DMA((3 * n,)), pltpu.SemaphoreType.DMA((3 * n,))],
                start=start, finish=finish)


SEM_SPEC = pl.BlockSpec(memory_space=pltpu.SEMAPHORE)


def exchange_chips_start(wires, name):
    n = len(wires)
    side = exchange_chips_side(wires)

    def body(*refs):
        src, land, sems = refs[:n], refs[n:2 * n], refs[2 * n:2 * n + 2]
        side["start"](src, land, sems)
        refs[-1][...] = jnp.zeros_like(refs[-1])

    hbm = [pltpu.HBM(w.shape, w.dtype) for w in wires]
    outs = pl.pallas_call(
        body, name=name, in_specs=[HBM_SPEC] * (2 * n),
        out_specs=[SEM_SPEC, SEM_SPEC] + [HBM_SPEC] * (2 * n) + [pl.BlockSpec(memory_space=pltpu.VMEM)],
        out_shape=list(side["sems"]) + hbm + hbm + [jax.ShapeDtypeStruct((8, 128), F32)],
        input_output_aliases={k: 2 + k for k in range(2 * n)},
        compiler_params=pltpu.CompilerParams(has_side_effects=pltpu.SideEffectType.DATAFLOW_SIDE_EFFECTING),
    )(*[pltpu.with_memory_space_constraint(w, pltpu.HBM) for w in wires],
      *[pltpu.with_memory_space_constraint(lax.empty(w.shape, w.dtype), pltpu.HBM) for w in wires])
    return outs[0], outs[1], outs[2:2 + n], outs[2 + n:2 + 2 * n], outs[-1]


def exchange_chips_wait(send_sems, recv_sems, wires, lands, after, name):
    n = len(wires)
    side = exchange_chips_side(wires)

    def body(*refs):
        side["finish"](refs[:n], refs[n:2 * n], refs[2 * n:2 * n + 2])

    hbm = [pltpu.HBM(w.shape, w.dtype) for w in wires]
    outs = pl.pallas_call(
        body, name=name, in_specs=[HBM_SPEC] * (2 * n) + [SEM_SPEC, SEM_SPEC] + [HBM_SPEC] * len(after),
        out_specs=[HBM_SPEC] * (2 * n), out_shape=hbm + hbm,
        input_output_aliases={k: k for k in range(2 * n)},
        compiler_params=pltpu.CompilerParams(has_side_effects=pltpu.SideEffectType.DATAFLOW_SIDE_EFFECTING),
    )(*wires, *lands, send_sems, recv_sems, *after)
    return outs[n:]


def rs_final_sum(place, owns, gots, after, name):
    n = len(owns)
    NP, half, C = gots[0].shape
    tr = _row_tile(half, TR_ELT)
    nr = half // tr

    def body(pc_ref, *refs):
        for a in range(n):
            own_ref, g1_ref, g2_ref, g3_ref = (refs[k * n + a] for k in range(4))
            refs[4 * n + 1 + a][...] = (((own_ref[...] + g1_ref[...].astype(F32)) + g2_ref[...].astype(F32))
                                        + g3_ref[...].astype(F32))

    slot = lambda j: pl.BlockSpec((None, tr, C), lambda r, pc: (pc[0] ^ j, r, 0))
    return pl.pallas_call(
        body, name=name,
        grid_spec=pltpu.PrefetchScalarGridSpec(
            num_scalar_prefetch=1, grid=(nr,),
            in_specs=[pl.BlockSpec((tr, C), lambda r, pc: (r, 0))] * n + [slot(1)] * n + [slot(2)] * n + [slot(3)] * n
                     + [pl.BlockSpec((8, 128), lambda r, pc: (0, 0))],
            out_specs=[pl.BlockSpec((tr, C), lambda r, pc: (pc[1] * nr + r, 0))] * n),
        out_shape=[jax.ShapeDtypeStruct((2 * half, C), F32)] * n,
        compiler_params=_params(("arbitrary",)),
    )(place, *owns, *gots, *gots, *gots, after)


def rs_share_siblings(totals, name):
    n = len(totals)

    def body(*refs):
        dst = refs[n:2 * n]
        send_sems, recv_sems = refs[2 * n:]
        x, y, c = _place()
        copies = []
        for k in range(n):
            half = dst[k].shape[0] // 2
            rows = dst[k].at[pl.ds(c * half, half)]
            cp = pltpu.make_async_remote_copy(rows, rows, send_sems.at[k], recv_sems.at[k],
                                              device_id=(x, y, 1 - c), device_id_type=MESH)
            cp.start()
            copies.append(cp)
        for k, cp in enumerate(copies):
            cp.wait_send()
            half = dst[k].shape[0] // 2
            got = dst[k].at[pl.ds((1 - c) * half, half)]
            pltpu.make_async_remote_copy(got, got, send_sems.at[k], recv_sems.at[k],
                                         device_id=(x, y, c), device_id_type=MESH).wait_recv()

    return pl.pallas_call(
        body, name=name,
        in_specs=[HBM_SPEC] * n, out_specs=[HBM_SPEC] * n,
        out_shape=[jax.ShapeDtypeStruct(t.shape, F32) for t in totals],
        input_output_aliases={k: k for k in range(n)},
        scratch_shapes=[pltpu.SemaphoreType.DMA((n,)), pltpu.SemaphoreType.DMA((n,))],
    )(*totals)


def rs_to_wires(place, grads, tag, sibs=None):
    if sibs is None:
        sibs = run_side(exchange_siblings_side(grads), f"rs_exchange_siblings_{tag}")
    wires, owns = [None] * len(grads), [None] * len(grads)
    for ks in _by_shape(grads):
        ws, os_ = rs_chip_sum(place, [grads[k] for k in ks], [sibs[k] for k in ks], f"rs_chip_sum_{tag}{ks[0]}")
        for k, w, o in zip(ks, ws, os_):
            wires[k], owns[k] = w, o
    return wires, owns


def rs_finish(place, owns, gots, after, tag):
    totals = [None] * len(owns)
    for ks in _by_shape(owns):
        sums = rs_final_sum(place, [owns[k] for k in ks], [gots[k] for k in ks], after, f"rs_final_sum_{tag}{ks[0]}")
        for k, t in zip(ks, sums):
            totals[k] = t
    return rs_share_siblings(totals, f"rs_share_siblings_{tag}")


def _rope_tables(positions):
    half = HEAD_DIM // 2
    inv_freq = ROPE_THETA ** (-jnp.arange(half, dtype=F32) / half)
    ang = positions.astype(F32)[:, None] * inv_freq
    lanes = jnp.arange(128)
    spread = (lanes[None, :] % half == jnp.arange(half)[:, None]).astype(F32)
    signed = spread * jnp.where(lanes % HEAD_DIM < half, -1.0, 1.0).astype(F32)
    exact = lax.Precision.HIGHEST
    return jnp.dot(jnp.cos(ang), spread, precision=exact), jnp.dot(jnp.sin(ang), signed, precision=exact)


def _cols_from_pieces(pieces, start, stop):
    C = pieces.shape[2]
    parts = []
    for q in range(N_CHIPS):
        lo, hi = max(start, q * C), min(stop, (q + 1) * C)
        if lo < hi:
            parts.append(pieces[q][:, lo - q * C:hi - q * C])
    return parts[0] if len(parts) == 1 else jnp.concatenate(parts, axis=1)


def _pieces_from_groups(groups):
    C = sum(g.shape[1] for g in groups) // N_CHIPS
    pieces = []
    for q in range(N_CHIPS):
        parts, off = [], 0
        for g in groups:
            lo, hi = max(q * C, off), min((q + 1) * C, off + g.shape[1])
            if lo < hi:
                parts.append(g[:, lo - off:hi - off])
            off += g.shape[1]
        pieces.append(parts[0] if len(parts) == 1 else jnp.concatenate(parts, axis=1))
    return jnp.stack(pieces)


def kernel(x, positions, ffn1_norm, ffn1_w_gate, ffn1_w_up, ffn1_w_down, mix_norm, w_in, conv_dw_w, conv_dw_b, conv_ln_g, conv_ln_b, conv_w_proj, attn_sinks, attn_w_o, gate_b, w_out, ffn2_norm, ffn2_w_gate, ffn2_w_up, ffn2_w_down, final_norm, loss_target, m_ffn1_norm, m_ffn1_w_gate, m_ffn1_w_up, m_ffn1_w_down, m_mix_norm, m_w_in, m_conv_dw_w, m_conv_dw_b, m_conv_ln_g, m_conv_ln_b, m_conv_w_proj, m_attn_sinks, m_attn_w_o, m_gate_b, m_w_out, m_ffn2_norm, m_ffn2_w_gate, m_ffn2_w_up, m_ffn2_w_down, m_final_norm, v_ffn1_norm, v_ffn1_w_gate, v_ffn1_w_up, v_ffn1_w_down, v_mix_norm, v_w_in, v_conv_dw_w, v_conv_dw_b, v_conv_ln_g, v_conv_ln_b, v_conv_w_proj, v_attn_sinks, v_attn_w_o, v_gate_b, v_w_out, v_ffn2_norm, v_ffn2_w_gate, v_ffn2_w_up, v_ffn2_w_down, v_final_norm):
    weights = dict(ffn1_norm=ffn1_norm, ffn1_w_gate=ffn1_w_gate, ffn1_w_up=ffn1_w_up, ffn1_w_down=ffn1_w_down,
                   mix_norm=mix_norm, w_in=w_in, conv_dw_w=conv_dw_w, conv_dw_b=conv_dw_b, conv_ln_g=conv_ln_g,
                   conv_ln_b=conv_ln_b, conv_w_proj=conv_w_proj, attn_sinks=attn_sinks, attn_w_o=attn_w_o,
                   gate_b=gate_b, w_out=w_out, ffn2_norm=ffn2_norm, ffn2_w_gate=ffn2_w_gate, ffn2_w_up=ffn2_w_up,
                   ffn2_w_down=ffn2_w_down, final_norm=final_norm)
    m_in = dict(ffn1_norm=m_ffn1_norm, ffn1_w_gate=m_ffn1_w_gate, ffn1_w_up=m_ffn1_w_up, ffn1_w_down=m_ffn1_w_down,
                mix_norm=m_mix_norm, w_in=m_w_in, conv_dw_w=m_conv_dw_w, conv_dw_b=m_conv_dw_b,
                conv_ln_g=m_conv_ln_g, conv_ln_b=m_conv_ln_b, conv_w_proj=m_conv_w_proj, attn_sinks=m_attn_sinks,
                attn_w_o=m_attn_w_o, gate_b=m_gate_b, w_out=m_w_out, ffn2_norm=m_ffn2_norm,
                ffn2_w_gate=m_ffn2_w_gate, ffn2_w_up=m_ffn2_w_up, ffn2_w_down=m_ffn2_w_down, final_norm=m_final_norm)
    v_in = dict(ffn1_norm=v_ffn1_norm, ffn1_w_gate=v_ffn1_w_gate, ffn1_w_up=v_ffn1_w_up, ffn1_w_down=v_ffn1_w_down,
                mix_norm=v_mix_norm, w_in=v_w_in, conv_dw_w=v_conv_dw_w, conv_dw_b=v_conv_dw_b,
                conv_ln_g=v_conv_ln_g, conv_ln_b=v_conv_ln_b, conv_w_proj=v_conv_w_proj, attn_sinks=v_attn_sinks,
                attn_w_o=v_attn_w_o, gate_b=v_gate_b, w_out=v_w_out, ffn2_norm=v_ffn2_norm,
                ffn2_w_gate=v_ffn2_w_gate, ffn2_w_up=v_ffn2_w_up, ffn2_w_down=v_ffn2_w_down, final_norm=v_final_norm)
    names = list(weights)
    big = ["ffn1_w_gate", "ffn1_w_up", "ffn1_w_down", "w_in", "conv_w_proj", "attn_w_o", "w_out",
           "ffn2_w_gate", "ffn2_w_up", "ffn2_w_down"]
    transposed = [k for k in big if k.endswith(("w_gate", "w_up"))]
    for k in transposed:
        weights[k], m_in[k], v_in[k] = (jnp.swapaxes(a, 1, 2) for a in (weights[k], m_in[k], v_in[k]))

    xs = x[0]
    T, D = xs.shape
    KV = (w_in.shape[2] * N_CHIPS - 5 * D) // 2
    n_heads = D // HEAD_DIM
    my_chip = 2 * lax.axis_index("x") + lax.axis_index("y")
    place = jnp.stack([my_chip, lax.axis_index("c")]).astype(jnp.int32)

    first, mixer_w, second = big[:3], big[3:7], big[7:]
    placed = {}
    for group in (first, mixer_w, second):
        for ks in _by_shape([weights[k][0] for k in group]):
            same = [group[k] for k in ks]
            placed.update(zip(same, place_shards(place, [weights[k][0] for k in same], BF16, f"place_{same[0]}")))
    placed_dw, = place_shards(place, [conv_dw_w[0]], F32, "place_conv_dw_w")
    wg1, wu1, wd1 = run_side(gather_side([placed[k] for k in first], []), "gather_ffn1")
    x1, h1, g1, u1, *gathered = ffn_fwd(x[0], ffn1_norm, wg1, wu1, wd1, "ffn1_fwd",
                                        side=gather_side([placed[k] for k in mixer_w], [placed_dw]))
    full = dict(zip(mixer_w + ["conv_dw_w"], gathered))
    w_glu = _cols_from_pieces(full["w_in"], 0, 2 * D)
    w_qkv = _cols_from_pieces(full["w_in"], 2 * D, 3 * D + 2 * KV)
    w_gate = _cols_from_pieces(full["w_in"], 3 * D + 2 * KV, 5 * D + 2 * KV)
    w_proj = full["conv_w_proj"].reshape(D, D)
    w_o = full["attn_w_o"].reshape(D, D)
    w_out_f = full["w_out"].reshape(D, D)
    dw_w = full["conv_dw_w"].transpose(1, 0, 2).reshape(CONV_WIDTH, D)
    dw_w = jnp.concatenate([dw_w, jnp.zeros((CONV_HALO - CONV_WIDTH, D), F32)], axis=0)
    cs, sn = _rope_tables(positions[0])
    fn_row = final_norm.reshape(1, D)

    h2, p_glu, p_gate, qr, kr, vb = mix_in_fwd(x1, mix_norm, w_glu, w_qkv, w_gate, cs, sn, "mix_in_fwd")
    c1, c3 = conv_fwd(p_glu, dw_w, conv_dw_b, conv_ln_g, conv_ln_b, "conv_fwd")
    o, wg2, wu2, wd2 = attn_fwd(qr, kr, vb, attn_sinks, "attn_fwd",
                                side=gather_side([placed[k] for k in second], []))
    x2, conv_out, attn_out, merged = merge_fwd(x1, c3, o, p_gate, gate_b, w_proj, w_o, w_out_f, "merge_fwd")
    x3, h3, g2, u2 = ffn_fwd(x2, ffn2_norm, wg2, wu2, wd2, "ffn2_fwd")

    dx3, head_sums = loss_head(x3, fn_row, loss_target[0], "loss_head")
    dx2, dwg2, dwu2, dwd2, d_ffn2_norm = ffn_bwd(x2, ffn2_norm, h3, g2, u2, wg2, wu2, wd2, dx3, "ffn2_bwd")
    ffn2_grads = [dwg2, dwu2, dwd2]
    d_gates, d_conv_out, d_attn_out, d_o, dc1, merge_sums, *sibs_f2 = merge_bwd(
        dx2, p_gate, gate_b, conv_out, attn_out, c1, conv_ln_g, conv_ln_b, w_proj, w_o, w_out_f, "merge_bwd",
        side=exchange_siblings_side(ffn2_grads))
    d_w_out, d_w_proj, d_w_o = matmul_tn([(merged, dx2), (c3, d_conv_out), (o, d_attn_out)], TK_TN, "d_w_square")
    wires_f2, owns_f2 = rs_to_wires(place, ffn2_grads, "ffn2", sibs=sibs_f2)
    d_glu, d_dw_w, *gots_f2 = conv_bwd(p_glu, dc1, dw_w, "conv_bwd", side=exchange_chips_side(wires_f2))
    dwc = D // N_CHIPS
    square_grads = [d_w_proj.reshape(N_CHIPS, dwc, D), d_w_o.reshape(N_CHIPS, dwc, D),
                    d_w_out.reshape(N_CHIPS, dwc, D)]
    dq, dk, dv, d_sinks, *sibs_sq = attn_bwd(qr, kr, vb, o, d_o, attn_sinks, "attn_bwd",
                                             side=exchange_siblings_side(square_grads))
    d_qkv = rope_bwd(dq, dk, dv, cs, sn, "rope_bwd")
    dx1, d_mix_norm = mix_in_bwd([d_glu, d_qkv, d_gates], [w_glu, w_qkv, w_gate], x1, mix_norm, dx2, "mix_in_bwd")
    d_w_in = _pieces_from_groups(matmul_tn([(h2, d_glu), (h2, d_qkv), (h2, d_gates)], TK_TN // 2, "d_w_in"))
    sib_w_in = run_side(exchange_siblings_side([d_w_in]), "rs_exchange_siblings_w_in")
    wires_m, owns_m = rs_to_wires(place, [d_w_in] + square_grads, "mixer", sibs=list(sib_w_in) + list(sibs_sq))
    dx0, dwg1, dwu1, dwd1, d_ffn1_norm, *gots_m = ffn_bwd(xs, ffn1_norm, h1, g1, u1, wg1, wu1, wd1, dx1, "ffn1_bwd",
                                                          side=exchange_chips_side(wires_m))
    wires_l, owns_l = rs_to_wires(place, [dwg1, dwu1, dwd1], "ffn1")
    send_sems, recv_sems, wires_l, lands_l, token = exchange_chips_start(wires_l, "rs_exchange_chips_ffn1_start")
    early_names = ["ffn2_w_gate", "ffn2_w_up", "ffn2_w_down", "w_in", "conv_w_proj", "attn_w_o", "w_out"]
    late_names = ["ffn1_w_gate", "ffn1_w_up", "ffn1_w_down"]
    reduced_early = rs_finish(place, owns_f2 + owns_m, list(gots_f2) + list(gots_m), token, "early")

    pad_row = lambda v: jnp.pad(v, ((0, 0), (0, D - v.shape[1])))
    small_rows = jnp.concatenate([
        d_ffn1_norm, d_mix_norm, merge_sums[2:3, :D], merge_sums[1:2, :D], merge_sums[1:2, D:],
        pad_row(d_sinks[0:1, :n_heads]), merge_sums[0:1, :D], merge_sums[0:1, D:], d_ffn2_norm,
        head_sums[0:1], head_sums[1:2], jnp.zeros((5, D), F32), d_dw_w], axis=0)
    small = allreduce_small(small_rows)
    loss = small[10, 0]
    grads = {"ffn1_norm": small[0:1], "mix_norm": small[1:2], "conv_dw_b": small[2:3], "conv_ln_g": small[3:4],
             "conv_ln_b": small[4:5], "attn_sinks": small[5:6, :n_heads],
             "gate_b": jnp.concatenate([small[6:7], small[7:8]], axis=1), "ffn2_norm": small[8:9],
             "final_norm": small[9:10]}
    grads["conv_dw_w"] = lax.dynamic_slice(small[16:16 + CONV_WIDTH], (0, my_chip * dwc), (CONV_WIDTH, dwc))
    grads.update(zip(early_names, reduced_early))

    deltas, new_m, new_v = {}, {}, {}

    def apply_adamw(ks):
        flat = lambda a, k: a.reshape(-1, weights[k].shape[-1])
        done = {}
        for idx in _by_shape([flat(grads[k], k) for k in ks]):
            same = [ks[i] for i in idx]
            results = adamw([flat(weights[k], k) for k in same], [flat(grads[k], k) for k in same],
                            [flat(m_in[k], k) for k in same], [flat(v_in[k], k) for k in same], f"adamw_{same[0]}")
            for k, (d, mn, vn) in zip(same, results):
                shape = weights[k].shape
                grads[k] = grads[k].reshape(shape)
                deltas[k], new_m[k], new_v[k] = d.reshape(shape), mn.reshape(shape), vn.reshape(shape)
                done[k] = d
        return done

    done = apply_adamw([k for k in names if k not in late_names])
    gots_l = exchange_chips_wait(send_sems, recv_sems, wires_l, lands_l, [done[k] for k in early_names],
                                 "rs_exchange_chips_ffn1_wait")
    grads.update(zip(late_names, rs_finish(place, owns_l, gots_l, token, "late")))
    apply_adamw(late_names)
    for k in transposed:
        for group in (grads, deltas, new_m, new_v):
            group[k] = jnp.swapaxes(group[k], 1, 2)

    return (loss, dx0[None], *[grads[k] for k in names], *[deltas[k] for k in names],
            *[new_m[k] for k in names], *[new_v[k] for k in names])
```

```python
import functools

import jax
import jax.numpy as jnp
from jax import lax
from jax.experimental import pallas as pl
from jax.experimental.pallas import tpu as pltpu

F32 = jnp.float32
BF16 = jnp.bfloat16
MESH = pl.DeviceIdType.MESH

HEAD_DIM = 64
WINDOW = 128
CONV_WIDTH = 31
CONV_HALO = 32
ROPE_THETA = 10000.0
EPS = 1e-6
LN_EPS = 1e-5
NEG_INF = -1e30
N_CHIPS = 4
N_DEV = 8

ADAM_LR = 0.001
ADAM_B1 = 0.9
ADAM_B2 = 0.999
ADAM_EPS = 1e-08
ADAM_WD = 0.01
ADAM_STEP = 10

TM_FFN = 512
TM_FFN_FWD = 1024
TM_ROW = 256
TM_MIX = 512
TK_TN = 1024
TR_ELT = 256
VMEM_LIMIT = 56 * 1024 * 1024

NT_DIMS = (((1,), (1,)), ((), ()))
TN_DIMS = (((0,), (0,)), ((), ()))


def _row_tile(rows, cap):
    for t in range(min(cap, rows), 15, -1):
        if rows % t == 0 and t % 16 == 0:
            return t
    return rows


def _params(sem):
    return pltpu.CompilerParams(dimension_semantics=sem, vmem_limit_bytes=VMEM_LIMIT)


def _dot(a, b):
    return jnp.dot(a, b, preferred_element_type=F32)


def _dot_nt(a, b):
    return lax.dot_general(a, b, NT_DIMS, preferred_element_type=F32)


def _dot_tn(a, b):
    return lax.dot_general(a, b, TN_DIMS, preferred_element_type=F32)


def _split_rows(dot, a, b):
    m = a.shape[0] // 2
    return jnp.concatenate([dot(a[:m], b), dot(a[m:], b)], axis=0)


def _sigmoid(x):
    return jax.nn.sigmoid(x)


def _rms_scale(xv):
    return lax.rsqrt(jnp.mean(xv * xv, axis=-1, keepdims=True) + EPS)


def _rms_bwd(xv, nw, dh):
    r = _rms_scale(xv)
    dn = dh * nw
    dx = r * dn - xv * (r * r * r) * jnp.mean(dn * xv, axis=-1, keepdims=True)
    dnw = jnp.sum(dh * (xv * r), axis=0, keepdims=True)
    return dx, dnw


def _silu_grad(z, s):
    return s * (1.0 + z * (1.0 - s))


HBM_SPEC = pl.BlockSpec(memory_space=pl.ANY)


def _call_hosting(body, side, *, grid, in_specs, out_specs, out_shape, scratch_shapes, operands, name, aliases=None):
    params = _params(("arbitrary",) * len(grid))
    aliases = dict(aliases or {})
    if side is None:
        return pl.pallas_call(body, name=name, grid=grid, in_specs=in_specs, out_specs=out_specs, out_shape=out_shape,
                              scratch_shapes=scratch_shapes, input_output_aliases=aliases,
                              compiler_params=params)(*operands)
    n_in, n_out, n_scr = len(in_specs), len(out_shape), len(scratch_shapes)
    s_in, s_out = len(side["inputs"]), len(side["out_shapes"])

    steps = 1
    for extent in grid:
        steps *= extent

    def at_step(index):
        linear = pl.program_id(0)
        for a in range(1, len(grid)):
            linear = linear * grid[a] + pl.program_id(a)
        return linear == index

    def hosted(*refs):
        b = n_in + s_in
        c = b + n_out
        d = c + s_out
        e = d + n_scr
        src, dst, sems = refs[n_in:b], refs[c:d], refs[e:]

        @pl.when(at_step(0))
        def _():
            side["start"](src, dst, sems)

        if "relay" in side:
            @pl.when(at_step(min((3 * steps) // 4, steps - 1)))
            def _():
                side["relay"](src, dst, sems)

        body(*refs[:n_in], *refs[b:c], *refs[d:e])

        @pl.when(at_step(steps - 1))
        def _():
            side["finish"](src, dst, sems)

    return pl.pallas_call(
        hosted, name=name, grid=grid, in_specs=list(in_specs) + [HBM_SPEC] * s_in,
        out_specs=list(out_specs) + [HBM_SPEC] * s_out, out_shape=list(out_shape) + list(side["out_shapes"]),
        scratch_shapes=list(scratch_shapes) + list(side["sems"]),
        input_output_aliases={**aliases, **{n_in + a: n_out + b for a, b in side["aliases"].items()}},
        compiler_params=params)(*operands, *side["inputs"])


def ffn_fwd(x, nw, wg, wu, wd, name, side=None):
    T, D = x.shape
    NP, Fs, _ = wg.shape
    tm = min(TM_FFN_FWD, T)

    def body(x_ref, nw_ref, wg_ref, wu_ref, wd_ref, xo_ref, h_ref, g_ref, u_ref, acc_ref):
        j = pl.program_id(1)

        @pl.when(j == 0)
        def _():
            xv = x_ref[...]
            h_ref[...] = (xv * _rms_scale(xv) * nw_ref[...]).astype(BF16)
            acc_ref[...] = jnp.zeros_like(acc_ref)

        h = h_ref[...]
        g = _dot_nt(h, wg_ref[...])
        u = _dot_nt(h, wu_ref[...])
        a = (g * _sigmoid(g)) * u
        g_ref[...] = g.astype(BF16)
        u_ref[...] = u.astype(BF16)
        acc_ref[...] += _dot(a.astype(BF16), wd_ref[...])

        @pl.when(j == NP - 1)
        def _():
            xo_ref[...] = x_ref[...] + 0.5 * acc_ref[...]

    return _call_hosting(
        body, side, name=name, grid=(T // tm, NP),
        in_specs=[pl.BlockSpec((tm, D), lambda i, j: (i, 0)),
                  pl.BlockSpec((1, D), lambda i, j: (0, 0)),
                  pl.BlockSpec((None, Fs, D), lambda i, j: (j, 0, 0)),
                  pl.BlockSpec((None, Fs, D), lambda i, j: (j, 0, 0)),
                  pl.BlockSpec((None, Fs, D), lambda i, j: (j, 0, 0))],
        out_specs=[pl.BlockSpec((tm, D), lambda i, j: (i, 0)),
                   pl.BlockSpec((tm, D), lambda i, j: (i, 0)),
                   pl.BlockSpec((None, tm, Fs), lambda i, j: (j, i, 0)),
                   pl.BlockSpec((None, tm, Fs), lambda i, j: (j, i, 0))],
        out_shape=[jax.ShapeDtypeStruct((T, D), F32), jax.ShapeDtypeStruct((T, D), BF16),
                   jax.ShapeDtypeStruct((NP, T, Fs), BF16), jax.ShapeDtypeStruct((NP, T, Fs), BF16)],
        scratch_shapes=[pltpu.VMEM((tm, D), F32)],
        operands=(x, nw, wg, wu, wd))


def _ffn_bwd_piece(j, h, g, u, wg, wu, wd, dout, dh_in, dws_in, name, side, norm):
    T, D = h.shape
    NP, Fs, _ = wg.shape
    tm = min(TM_FFN, T)
    n_in = 7 + (dh_in is not None) + (2 if norm else 0) + (3 if dws_in else 0)

    def body(*refs):
        h_ref, g_ref, u_ref, wg_ref, wu_ref, wd_ref, do_ref = refs[:7]
        dhin_ref = refs[7] if dh_in is not None else None
        dh_ref, dwg_ref, dwu_ref, dwd_ref = refs[n_in:n_in + 4]

        @pl.when(pl.program_id(0) == 0)
        def _():
            dwg_ref[...] = jnp.zeros_like(dwg_ref)
            dwu_ref[...] = jnp.zeros_like(dwu_ref)
            dwd_ref[...] = jnp.zeros_like(dwd_ref)
            if norm:
                refs[n_in + 4][...] = jnp.zeros_like(refs[n_in + 4])

        dob = (0.5 * do_ref[...]).astype(BF16)
        da = _split_rows(_dot_nt, dob, wd_ref[...])
        gf = g_ref[...].astype(F32)
        uf = u_ref[...].astype(F32)
        s = _sigmoid(gf)
        act = gf * s
        dg = (da * uf * _silu_grad(gf, s)).astype(BF16)
        du = (da * act).astype(BF16)
        a = (act * uf).astype(BF16)
        dh = _dot(dg, wg_ref[...]) + _dot(du, wu_ref[...])
        dh = dh if dhin_ref is None else dhin_ref[...] + dh
        if norm:
            x_ref, nw_ref = refs[7 + (dh_in is not None):9 + (dh_in is not None)]
            dxn, dnw = _rms_bwd(x_ref[...], nw_ref[...], dh)
            dh_ref[...] = do_ref[...] + dxn
            refs[n_in + 4][...] += dnw
        else:
            dh_ref[...] = dh
        hb = h_ref[...]
        dwg_ref[...] += _dot_tn(dg, hb)
        dwu_ref[...] += _dot_tn(du, hb)
        dwd_ref[...] += _dot_tn(a, dob)

    rows = pl.BlockSpec((tm, D), lambda i: (i, 0))
    piece = pl.BlockSpec((None, tm, Fs), lambda i: (j, i, 0))
    slot = pl.BlockSpec((None, Fs, D), lambda i: (j, 0, 0), pipeline_mode=pl.Buffered(1))
    in_specs = [rows, piece, piece, slot, slot, slot, rows]
    operands = [h, g, u, wg, wu, wd, dout]
    aliases = {}
    if dh_in is not None:
        in_specs.append(rows)
        operands.append(dh_in)
    if norm:
        in_specs += [rows, pl.BlockSpec((1, D), lambda i: (0, 0))]
        operands += list(norm)
    if dws_in:
        aliases = {len(operands) + k: 1 + k for k in range(3)}
        in_specs += [HBM_SPEC] * 3
        operands += list(dws_in)
    out_specs = [rows, slot, slot, slot]
    out_shape = [jax.ShapeDtypeStruct((T, D), F32)] + [jax.ShapeDtypeStruct((NP, Fs, D), F32)] * 3
    if norm:
        out_specs.append(pl.BlockSpec((1, D), lambda i: (0, 0)))
        out_shape.append(jax.ShapeDtypeStruct((1, D), F32))
    return _call_hosting(body, side, name=name, grid=(T // tm,), in_specs=in_specs, out_specs=out_specs,
                         out_shape=out_shape, scratch_shapes=[], aliases=aliases, operands=tuple(operands))


def ffn_bwd(x, nw, h, g, u, wg, wu, wd, dout, name, side=None):
    NP = wg.shape[0]
    dh, dws, extra = None, None, []
    for j in range(NP):
        dh, *rest = _ffn_bwd_piece(j, h, g, u, wg, wu, wd, dout, dh, dws, f"{name}_{j}",
                                   side if j == 0 else None, (x, nw) if j == NP - 1 else None)
        dws, rest = rest[:3], rest[3:]
        if j == 0:
            extra = rest[1:] if NP == 1 else rest
    return (dh, *dws, rest[0], *extra)


def mix_in_fwd(x, nw, w_glu, w_qkv, w_gate, cs, sn, name):
    T, D = x.shape
    KV = (w_qkv.shape[1] - D) // 2
    tm = min(TM_MIX, T)

    def body(x_ref, nw_ref, wa_ref, wq_ref, wg_ref, cs_ref, sn_ref, h_ref, pa_ref, pg_ref, q_ref, k_ref, v_ref):
        xv = x_ref[...]
        h = (xv * _rms_scale(xv) * nw_ref[...]).astype(BF16)
        h_ref[...] = h
        pa_ref[...] = _dot(h, wa_ref[...])
        pg_ref[...] = _dot(h, wg_ref[...])
        qkv = _dot(h, wq_ref[...])
        cs_v, sn_v = cs_ref[...], sn_ref[...]
        q_ref[...] = _rope_chunks(qkv[:, :D], cs_v, sn_v, 1.0).astype(BF16)
        k_ref[...] = _rope_chunks(qkv[:, D:D + KV], cs_v, sn_v, 1.0).astype(BF16)
        v_ref[...] = qkv[:, D + KV:].astype(BF16)

    rows = lambda w: pl.BlockSpec((tm, w), lambda i: (i, 0))
    whole = lambda a: pl.BlockSpec(a.shape, lambda i: (0, 0), pipeline_mode=pl.Buffered(1))
    return pl.pallas_call(
        body, name=name, grid=(T // tm,),
        in_specs=[rows(D), whole(nw), whole(w_glu), whole(w_qkv), whole(w_gate), rows(128), rows(128)],
        out_specs=[rows(D), rows(2 * D), rows(2 * D), rows(D), rows(KV), rows(KV)],
        out_shape=[jax.ShapeDtypeStruct((T, D), BF16), jax.ShapeDtypeStruct((T, 2 * D), F32),
                   jax.ShapeDtypeStruct((T, 2 * D), F32), jax.ShapeDtypeStruct((T, D), BF16),
                   jax.ShapeDtypeStruct((T, KV), BF16), jax.ShapeDtypeStruct((T, KV), BF16)],
        compiler_params=_params(("parallel",)),
    )(x, nw, w_glu, w_qkv, w_gate, cs, sn)


def matmul_tn(pairs, tk, name):
    n = len(pairs)
    T = pairs[0][0].shape[0]
    tk = min(tk, T)

    def body(*refs):
        @pl.when(pl.program_id(0) == 0)
        def _():
            for a in range(n):
                refs[2 * n + a][...] = jnp.zeros_like(refs[2 * n + a])

        for a in range(n):
            refs[2 * n + a][...] += _dot_tn(refs[2 * a][...].astype(BF16), refs[2 * a + 1][...].astype(BF16))

    rows = lambda a: pl.BlockSpec((tk, a.shape[1]), lambda t: (t, 0))
    shapes = [(lhs.shape[1], rhs.shape[1]) for lhs, rhs in pairs]
    return pl.pallas_call(
        body, name=name, grid=(T // tk,),
        in_specs=[rows(a) for pair in pairs for a in pair],
        out_specs=[pl.BlockSpec(s, lambda t: (0, 0), pipeline_mode=pl.Buffered(1)) for s in shapes],
        out_shape=[jax.ShapeDtypeStruct(s, F32) for s in shapes],
        compiler_params=_params(("arbitrary",)),
    )(*[a for pair in pairs for a in pair])


def mix_in_bwd(dps, ws, x, nw, dres, name, side=None):
    T, D = x.shape
    tm = min(TM_MIX, T)
    n = len(dps)

    def body(*refs):
        dp_refs, w_refs = refs[:n], refs[n:2 * n]
        x_ref, nw_ref, dr_ref, dx_ref, dnw_ref = refs[2 * n:]

        @pl.when(pl.program_id(0) == 0)
        def _():
            dnw_ref[...] = jnp.zeros_like(dnw_ref)

        dh = _dot_nt(dp_refs[0][...], w_refs[0][...])
        for k in range(1, n):
            dh += _dot_nt(dp_refs[k][...], w_refs[k][...])
        dxn, dnw = _rms_bwd(x_ref[...], nw_ref[...], dh)
        dx_ref[...] = dr_ref[...] + dxn
        dnw_ref[...] += dnw

    in_specs = [pl.BlockSpec((tm, dp.shape[1]), lambda i: (i, 0)) for dp in dps]
    in_specs += [pl.BlockSpec(w.shape, lambda i: (0, 0), pipeline_mode=pl.Buffered(1)) for w in ws]
    in_specs += [pl.BlockSpec((tm, D), lambda i: (i, 0)), pl.BlockSpec((1, D), lambda i: (0, 0)),
                 pl.BlockSpec((tm, D), lambda i: (i, 0))]
    return _call_hosting(
        body, side, name=name, grid=(T // tm,), in_specs=in_specs,
        out_specs=[pl.BlockSpec((tm, D), lambda i: (i, 0)), pl.BlockSpec((1, D), lambda i: (0, 0))],
        out_shape=[jax.ShapeDtypeStruct((T, D), F32), jax.ShapeDtypeStruct((1, D), F32)],
        scratch_shapes=[], operands=(*dps, *ws, x, nw, dres))


def _layernorm_stats(c1):
    mu = jnp.mean(c1, axis=-1, keepdims=True)
    xc = c1 - mu
    rstd = lax.rsqrt(jnp.mean(xc * xc, axis=-1, keepdims=True) + LN_EPS)
    return xc * rstd, rstd


def _shifted_copies(src_ref, dst_ref):
    rows = dst_ref.shape[1]
    for b in range(1, 8):
        dst_ref[b - 1] = src_ref[pl.ds(b, rows), :]


def _shifted_rows(src_ref, shifted_ref, start, rows, cols):
    a8, b = divmod(start, 8)
    if b == 0:
        return src_ref[pl.ds(8 * a8, rows), cols]
    return shifted_ref[b - 1, pl.ds(8 * a8, rows), cols]


def conv_fwd(p_glu, dw_w, dw_b, ln_g, ln_b, name):
    T, D2 = p_glu.shape
    D = D2 // 2
    tm = min(TM_ROW, T)
    hb = tm // CONV_HALO

    def body(a_ref, b_ref, ah_ref, bh_ref, w_ref, wb_ref, g_ref, be_ref, c1_ref, c3_ref, e_ref, es_ref):
        i = pl.program_id(0)
        halo = ah_ref[...] * _sigmoid(bh_ref[...])
        e_ref[pl.ds(0, CONV_HALO), :] = jnp.where(i > 0, halo, 0.0)
        e_ref[pl.ds(CONV_HALO, tm), :] = a_ref[...] * _sigmoid(b_ref[...])
        _shifted_copies(e_ref, es_ref)
        off = CONV_HALO - (CONV_WIDTH - 1)

        def strip(s, carry):
            cols = pl.ds(pl.multiple_of(s * 128, 128), 128)
            acc = jnp.zeros((tm, 128), F32) + wb_ref[:, cols]
            for k in range(CONV_WIDTH):
                acc += w_ref[pl.ds(k, 1), cols] * _shifted_rows(e_ref, es_ref, off + k, tm, cols)
            c1_ref[:, cols] = acc
            return carry

        lax.fori_loop(0, D // 128, strip, 0)
        xhat, _ = _layernorm_stats(c1_ref[...])
        c2 = xhat * g_ref[...] + be_ref[...]
        c3_ref[...] = (c2 * _sigmoid(c2)).astype(BF16)

    row = pl.BlockSpec((1, D), lambda i: (0, 0))
    return pl.pallas_call(
        body, name=name, grid=(T // tm,),
        in_specs=[pl.BlockSpec((tm, D), lambda i: (i, 0)), pl.BlockSpec((tm, D), lambda i: (i, 1)),
                  pl.BlockSpec((CONV_HALO, D), lambda i: (jnp.maximum(i * hb - 1, 0), 0)),
                  pl.BlockSpec((CONV_HALO, D), lambda i: (jnp.maximum(i * hb - 1, 0), 1)),
                  pl.BlockSpec((CONV_HALO, D), lambda i: (0, 0)), row, row, row],
        out_specs=[pl.BlockSpec((tm, D), lambda i: (i, 0)), pl.BlockSpec((tm, D), lambda i: (i, 0))],
        out_shape=[jax.ShapeDtypeStruct((T, D), F32), jax.ShapeDtypeStruct((T, D), BF16)],
        scratch_shapes=[pltpu.VMEM((tm + CONV_HALO, D), F32), pltpu.VMEM((7, tm + CONV_HALO - 8, D), F32)],
        compiler_params=_params(("parallel",)),
    )(p_glu, p_glu, p_glu, p_glu, dw_w, dw_b, ln_g, ln_b)


def conv_bwd(p_glu, dc1, dw_w, name, side=None):
    T, D2 = p_glu.shape
    D = D2 // 2
    tm = min(TM_ROW, T)
    hb = tm // CONV_HALO
    last = T // CONV_HALO - 1
    nblk = T // tm

    def body(a_ref, b_ref, ah_ref, bh_ref, d_ref, dn_ref, w_ref, dp_ref, dw_ref, e_ref, f_ref, es_ref, fs_ref):
        i = pl.program_id(0)

        @pl.when(i == 0)
        def _():
            dw_ref[...] = jnp.zeros_like(dw_ref)

        halo = ah_ref[...] * _sigmoid(bh_ref[...])
        e_ref[pl.ds(0, CONV_HALO), :] = jnp.where(i > 0, halo, 0.0)
        e_ref[pl.ds(CONV_HALO, tm), :] = a_ref[...] * _sigmoid(b_ref[...])
        f_ref[pl.ds(0, tm), :] = d_ref[...]
        f_ref[pl.ds(tm, CONV_HALO), :] = jnp.where(i < nblk - 1, dn_ref[...], 0.0)
        _shifted_copies(e_ref, es_ref)
        _shifted_copies(f_ref, fs_ref)
        off = CONV_HALO - (CONV_WIDTH - 1)

        def strip(s, carry):
            cols = pl.ds(pl.multiple_of(s * 128, 128), 128)
            d = d_ref[:, cols]
            dc0 = jnp.zeros((tm, 128), F32)
            for k in range(CONV_WIDTH):
                dw_ref[pl.ds(k, 1), cols] += jnp.sum(d * _shifted_rows(e_ref, es_ref, off + k, tm, cols),
                                                     axis=0, keepdims=True)
                dc0 += w_ref[pl.ds(k, 1), cols] * _shifted_rows(f_ref, fs_ref, CONV_WIDTH - 1 - k, tm, cols)
            a = a_ref[:, cols]
            sb = _sigmoid(b_ref[:, cols])
            dp_ref[:, cols] = (dc0 * sb).astype(BF16)
            dp_ref[:, pl.ds(pl.multiple_of(D + s * 128, 128), 128)] = (dc0 * a * sb * (1.0 - sb)).astype(BF16)
            return carry

        lax.fori_loop(0, D // 128, strip, 0)

    return _call_hosting(
        body, side, name=name, grid=(nblk,),
        in_specs=[pl.BlockSpec((tm, D), lambda i: (i, 0)), pl.BlockSpec((tm, D), lambda i: (i, 1)),
                  pl.BlockSpec((CONV_HALO, D), lambda i: (jnp.maximum(i * hb - 1, 0), 0)),
                  pl.BlockSpec((CONV_HALO, D), lambda i: (jnp.maximum(i * hb - 1, 0), 1)),
                  pl.BlockSpec((tm, D), lambda i: (i, 0)),
                  pl.BlockSpec((CONV_HALO, D), lambda i: (jnp.minimum((i + 1) * hb, last), 0)),
                  pl.BlockSpec((CONV_HALO, D), lambda i: (0, 0))],
        out_specs=[pl.BlockSpec((tm, D2), lambda i: (i, 0)), pl.BlockSpec((CONV_HALO, D), lambda i: (0, 0))],
        out_shape=[jax.ShapeDtypeStruct((T, D2), BF16), jax.ShapeDtypeStruct((CONV_HALO, D), F32)],
        scratch_shapes=[pltpu.VMEM((tm + CONV_HALO, D), F32), pltpu.VMEM((tm + CONV_HALO, D), F32),
                        pltpu.VMEM((7, tm + CONV_HALO - 8, D), F32), pltpu.VMEM((7, tm + CONV_HALO - 8, D), F32)],
        operands=(p_glu, p_glu, p_glu, p_glu, dc1, dc1, dw_w))


def _rot_half(x):
    lane = lax.broadcasted_iota(jnp.int32, x.shape, 1)
    first = (lane % HEAD_DIM) < HEAD_DIM // 2
    return jnp.where(first, pltpu.roll(x, 128 - HEAD_DIM // 2, 1), pltpu.roll(x, HEAD_DIM // 2, 1))


def _rope_chunks(x, cs, sn, sign):
    outs = []
    for c in range(x.shape[1] // 128):
        xc = x[:, c * 128:(c + 1) * 128]
        outs.append(xc * cs + sign * (_rot_half(xc) * sn))
    return outs[0] if len(outs) == 1 else jnp.concatenate(outs, axis=1)


def rope_bwd(dq, dk, dv, cs, sn, name):
    T, D = dq.shape
    KV = dk.shape[1]
    tm = min(TM_ROW, T)

    def body(dq_ref, dk_ref, dv_ref, cs_ref, sn_ref, o_ref):
        cs_v, sn_v = cs_ref[...], sn_ref[...]
        o_ref[:, pl.ds(0, D)] = _rope_chunks(dq_ref[...], cs_v, sn_v, -1.0).astype(BF16)
        o_ref[:, pl.ds(D, KV)] = _rope_chunks(dk_ref[...], cs_v, sn_v, -1.0).astype(BF16)
        o_ref[:, pl.ds(D + KV, KV)] = dv_ref[...].astype(BF16)

    tab = pl.BlockSpec((tm, 128), lambda i: (i, 0))
    return pl.pallas_call(
        body, name=name, grid=(T // tm,),
        in_specs=[pl.BlockSpec((tm, D), lambda i: (i, 0)), pl.BlockSpec((tm, KV), lambda i: (i, 0)),
                  pl.BlockSpec((tm, KV), lambda i: (i, 0)), tab, tab],
        out_specs=pl.BlockSpec((tm, D + 2 * KV), lambda i: (i, 0)),
        out_shape=jax.ShapeDtypeStruct((T, D + 2 * KV), BF16),
        compiler_params=_params(("parallel",)),
    )(dq, dk, dv, cs, sn)


def _lane_lo():
    return lax.broadcasted_iota(jnp.int32, (1, 128), 1) < HEAD_DIM


def _band_mask(i, reps):
    shape = (reps * WINDOW, 2 * WINDOW)
    qi = lax.broadcasted_iota(jnp.int32, shape, 0) % WINDOW
    cj = lax.broadcasted_iota(jnp.int32, shape, 1)
    rel = qi - cj + WINDOW
    return (rel >= 0) & (rel < WINDOW) & ((i > 0) | (cj >= WINDOW))


def _stack_pairs(ref, first, n):
    parts = [ref[:, pl.ds((first + p) * 128, 128)] for p in range(n)]
    return parts[0] if n == 1 else jnp.concatenate(parts, axis=0)


def _pair_rows(n):
    return lax.broadcasted_iota(jnp.int32, (n * WINDOW, 1), 0) // WINDOW


def _per_pair_column(values, n):
    rows = _pair_rows(n)
    col = jnp.zeros((n * WINDOW, 1), F32) + values[0]
    for p in range(1, n):
        col = jnp.where(rows == p, values[p], col)
    return col


def _kv_lo_hi(x2, g):
    pair, half = divmod(g, 2)
    lo = _lane_lo()
    xg = x2[:, pair * 128:(pair + 1) * 128].astype(F32)
    xg = jnp.where(lo if half == 0 else ~lo, xg, 0.0)
    sw = pltpu.roll(xg, HEAD_DIM, 1)
    x_lo, x_hi = (xg, sw) if half == 0 else (sw, xg)
    return x_lo.astype(BF16), x_hi.astype(BF16)


def _softmax_sink(s, allowed, sink):
    s = jnp.where(allowed, s * (HEAD_DIM ** -0.5), NEG_INF)
    m = jnp.maximum(jnp.max(s, axis=-1, keepdims=True), sink)
    p = jnp.exp(s - m)
    es = jnp.exp(sink - m)
    inv = 1.0 / (jnp.sum(p, axis=-1, keepdims=True) + es)
    return p * inv, es * inv


def attn_fwd(qr, kr, vb, sinks, name, side=None):
    T, D = qr.shape
    KV = kr.shape[1]
    n_kv = KV // HEAD_DIM
    group = (D // HEAD_DIM) // n_kv
    nb = T // WINDOW

    npair = group // 2

    def body(sink_ref, q_ref, kp_ref, kc_ref, vp_ref, vc_ref, o_ref):
        i = pl.program_id(0)
        allowed = _band_mask(i, npair)
        k2 = jnp.concatenate([kp_ref[...], kc_ref[...]], axis=0)
        v2 = jnp.concatenate([vp_ref[...], vc_ref[...]], axis=0)
        outs = [None] * (D // 128)
        for g in range(n_kv):
            k_lo, k_hi = _kv_lo_hi(k2, g)
            v_lo, v_hi = _kv_lo_hi(v2, g)
            first = (g * group) // 2
            q = _stack_pairs(q_ref, first, npair)
            sink_e = _per_pair_column([sink_ref[0, g * group + 2 * p] for p in range(npair)], npair)
            sink_o = _per_pair_column([sink_ref[0, g * group + 2 * p + 1] for p in range(npair)], npair)
            pe, _ = _softmax_sink(_dot_nt(q, k_lo), allowed, sink_e)
            po, _ = _softmax_sink(_dot_nt(q, k_hi), allowed, sink_o)
            o = _dot(pe.astype(BF16), v_lo) + _dot(po.astype(BF16), v_hi)
            for p in range(npair):
                outs[first + p] = o[p * WINDOW:(p + 1) * WINDOW]
        o_ref[...] = jnp.concatenate(outs, axis=1).astype(BF16)

    prev = lambda i: (jnp.maximum(i - 1, 0), 0)
    cur = lambda i: (i, 0)
    return _call_hosting(
        body, side, name=name, grid=(nb,),
        in_specs=[pl.BlockSpec(memory_space=pltpu.SMEM),
                  pl.BlockSpec((WINDOW, D), cur),
                  pl.BlockSpec((WINDOW, KV), prev), pl.BlockSpec((WINDOW, KV), cur),
                  pl.BlockSpec((WINDOW, KV), prev), pl.BlockSpec((WINDOW, KV), cur)],
        out_specs=[pl.BlockSpec((WINDOW, D), cur)],
        out_shape=[jax.ShapeDtypeStruct((T, D), BF16)],
        scratch_shapes=[], operands=(sinks, qr, kr, kr, vb, vb))


def attn_bwd(qr, kr, vb, o, do, sinks, name, side=None):
    T, D = qr.shape
    KV = kr.shape[1]
    n_heads = D // HEAD_DIM
    n_kv = KV // HEAD_DIM
    group = n_heads // n_kv
    nb = T // WINDOW
    npair = group // 2
    scale = HEAD_DIM ** -0.5

    def body(sink_ref, q_ref, kp_ref, kc_ref, vp_ref, vc_ref, o_ref, do_ref,
             dq_ref, dk_ref, dv_ref, ds_ref, ck_ref, cv_ref):
        i = pl.program_id(0)
        lo = _lane_lo()

        @pl.when(i == 0)
        def _():
            ck_ref[...] = jnp.zeros_like(ck_ref)
            cv_ref[...] = jnp.zeros_like(cv_ref)
            ds_ref[...] = jnp.zeros_like(ds_ref)

        @pl.when(i < nb)
        def _():
            allowed = _band_mask(i, npair)
            rows = _pair_rows(npair)
            k2 = jnp.concatenate([kp_ref[...], kc_ref[...]], axis=0)
            v2 = jnp.concatenate([vp_ref[...], vc_ref[...]], axis=0)
            lane = lax.broadcasted_iota(jnp.int32, (1, 128), 1)
            dsink = jnp.zeros((1, 128), F32)
            dq_out = [None] * (D // 128)
            dk_pairs = [jnp.zeros((2 * WINDOW, 128), F32) for _ in range(KV // 128)]
            dv_pairs = [jnp.zeros((2 * WINDOW, 128), F32) for _ in range(KV // 128)]
            for g in range(n_kv):
                k_lo, k_hi = _kv_lo_hi(k2, g)
                v_lo, v_hi = _kv_lo_hi(v2, g)
                first = (g * group) // 2
                q = _stack_pairs(q_ref, first, npair)
                dop = _stack_pairs(do_ref, first, npair)
                dd = dop.astype(F32) * _stack_pairs(o_ref, first, npair).astype(F32)
                dq = jnp.zeros((npair * WINDOW, 128), F32)
                dkg = jnp.zeros((2 * WINDOW, 128), F32)
                dvg = jnp.zeros((2 * WINDOW, 128), F32)
                for parity, k_h, v_h, sel in ((0, k_lo, v_lo, lo), (1, k_hi, v_hi, ~lo)):
                    heads = [g * group + 2 * p + parity for p in range(npair)]
                    sink = _per_pair_column([sink_ref[0, h] for h in heads], npair)
                    p_, ps = _softmax_sink(_dot_nt(q, k_h), allowed, sink)
                    delta = jnp.sum(jnp.where(sel, dd, 0.0), axis=-1, keepdims=True)
                    dsc = (p_ * (_dot_nt(dop, v_h) - delta)).astype(BF16)
                    sd = -ps * delta
                    for p, h in enumerate(heads):
                        dsink += jnp.where(lane == h, jnp.sum(jnp.where(rows == p, sd, 0.0)), 0.0)
                    dq += _dot(dsc, k_h)
                    dkg += jnp.where(sel, _dot_tn(dsc, q), 0.0)
                    dvg += jnp.where(sel, _dot_tn(p_.astype(BF16), dop), 0.0)
                for p in range(npair):
                    dq_out[first + p] = dq[p * WINDOW:(p + 1) * WINDOW]
                pair, half = divmod(g, 2)
                keep = lo if half == 0 else ~lo
                dk_pairs[pair] += jnp.where(keep, dkg + pltpu.roll(dkg, HEAD_DIM, 1), 0.0) * scale
                dv_pairs[pair] += jnp.where(keep, dvg + pltpu.roll(dvg, HEAD_DIM, 1), 0.0)
            dq_ref[...] = jnp.concatenate(dq_out, axis=1) * scale
            dk2 = dk_pairs[0] if len(dk_pairs) == 1 else jnp.concatenate(dk_pairs, axis=1)
            dv2 = dv_pairs[0] if len(dv_pairs) == 1 else jnp.concatenate(dv_pairs, axis=1)
            dk_ref[...] = ck_ref[...] + dk2[:WINDOW]
            dv_ref[...] = cv_ref[...] + dv2[:WINDOW]
            ck_ref[...] = dk2[WINDOW:]
            cv_ref[...] = dv2[WINDOW:]
            ds_ref[pl.ds(0, 1), :] += dsink

        @pl.when(i == nb)
        def _():
            dk_ref[...] = ck_ref[...]
            dv_ref[...] = cv_ref[...]

    prev = lambda i: (jnp.maximum(i - 1, 0), 0)
    cur = lambda i: (jnp.minimum(i, nb - 1), 0)
    prevc = lambda i: (jnp.maximum(jnp.minimum(i, nb - 1) - 1, 0), 0)
    return _call_hosting(
        body, side, name=name, grid=(nb + 1,),
        in_specs=[pl.BlockSpec(memory_space=pltpu.SMEM),
                  pl.BlockSpec((WINDOW, D), cur),
                  pl.BlockSpec((WINDOW, KV), prevc), pl.BlockSpec((WINDOW, KV), cur),
                  pl.BlockSpec((WINDOW, KV), prevc), pl.BlockSpec((WINDOW, KV), cur),
                  pl.BlockSpec((WINDOW, D), cur), pl.BlockSpec((WINDOW, D), cur)],
        out_specs=[pl.BlockSpec((WINDOW, D), cur), pl.BlockSpec((WINDOW, KV), prev),
                   pl.BlockSpec((WINDOW, KV), prev), pl.BlockSpec((8, 128), lambda i: (0, 0))],
        out_shape=[jax.ShapeDtypeStruct((T, D), F32), jax.ShapeDtypeStruct((T, KV), F32),
                   jax.ShapeDtypeStruct((T, KV), F32), jax.ShapeDtypeStruct((8, 128), F32)],
        scratch_shapes=[pltpu.VMEM((WINDOW, KV), F32), pltpu.VMEM((WINDOW, KV), F32)],
        operands=(sinks, qr, kr, kr, vb, vb, o, do))


def merge_fwd(x, c3, o, p_gate, gate_b, w_proj, w_o, w_out, name):
    T, D = x.shape
    tm = min(TM_MIX, T)

    def body(x_ref, c3_ref, o_ref, gc_ref, ga_ref, bc_ref, ba_ref, wp_ref, wo_ref, wout_ref,
             xo_ref, co_ref, ao_ref, mg_ref):
        conv_out = _dot(c3_ref[...], wp_ref[...])
        attn_out = _dot(o_ref[...], wo_ref[...])
        merged = (_sigmoid(gc_ref[...] + bc_ref[...]) * conv_out
                  + _sigmoid(ga_ref[...] + ba_ref[...]) * attn_out).astype(BF16)
        co_ref[...] = conv_out.astype(BF16)
        ao_ref[...] = attn_out.astype(BF16)
        mg_ref[...] = merged
        xo_ref[...] = x_ref[...] + _dot(merged, wout_ref[...])

    blk = lambda j: pl.BlockSpec((tm, D), lambda i: (i, j))
    row = lambda j: pl.BlockSpec((1, D), lambda i: (0, j))
    mat = pl.BlockSpec((D, D), lambda i: (0, 0), pipeline_mode=pl.Buffered(1))
    return pl.pallas_call(
        body, name=name, grid=(T // tm,),
        in_specs=[blk(0), blk(0), blk(0), blk(0), blk(1), row(0), row(1), mat, mat, mat],
        out_specs=[blk(0), blk(0), blk(0), blk(0)],
        out_shape=[jax.ShapeDtypeStruct((T, D), F32)] + [jax.ShapeDtypeStruct((T, D), BF16)] * 3,
        compiler_params=_params(("parallel",)),
    )(x, c3, o, p_gate, p_gate, gate_b, gate_b, w_proj, w_o, w_out)


def merge_bwd(dx, p_gate, gate_b, conv_out, attn_out, c1, ln_g, ln_b, w_proj, w_o, w_out, name, side=None):
    T, D = dx.shape
    tm = min(TM_ROW, T)

    def body(dx_ref, gc_ref, ga_ref, bc_ref, ba_ref, co_ref, ao_ref, c1_ref, g_ref, be_ref,
             wp_ref, wo_ref, wout_ref, dgt_ref, dco_ref, dao_ref, do_ref, dc1_ref, sm_ref):
        @pl.when(pl.program_id(0) == 0)
        def _():
            sm_ref[...] = jnp.zeros_like(sm_ref)

        dm = _dot_nt(dx_ref[...].astype(BF16), wout_ref[...])
        sc = _sigmoid(gc_ref[...] + bc_ref[...])
        sa = _sigmoid(ga_ref[...] + ba_ref[...])
        dco = (dm * sc).astype(BF16)
        dao = (dm * sa).astype(BF16)
        dgc = dm * co_ref[...].astype(F32) * sc * (1.0 - sc)
        dga = dm * ao_ref[...].astype(F32) * sa * (1.0 - sa)
        dgt_ref[:, pl.ds(0, D)] = dgc.astype(BF16)
        dgt_ref[:, pl.ds(D, D)] = dga.astype(BF16)
        dco_ref[...] = dco
        dao_ref[...] = dao
        do_ref[...] = _dot_nt(dao, wo_ref[...]).astype(BF16)
        dc3 = _dot_nt(dco, wp_ref[...])
        xhat, rstd = _layernorm_stats(c1_ref[...])
        c2 = xhat * g_ref[...] + be_ref[...]
        dc2 = dc3 * _silu_grad(c2, _sigmoid(c2))
        dxh = dc2 * g_ref[...]
        dc1 = rstd * (dxh - jnp.mean(dxh, axis=-1, keepdims=True)
                      - xhat * jnp.mean(dxh * xhat, axis=-1, keepdims=True))
        dc1_ref[...] = dc1
        colsum = lambda v: jnp.sum(v, axis=0, keepdims=True)
        for r, (left, right) in enumerate(((dgc, dga), (dc2 * xhat, dc2), (dc1, None))):
            sm_ref[pl.ds(r, 1), pl.ds(0, D)] += colsum(left)
            if right is not None:
                sm_ref[pl.ds(r, 1), pl.ds(D, D)] += colsum(right)

    blk = lambda j: pl.BlockSpec((tm, D), lambda i: (i, j))
    row = lambda j: pl.BlockSpec((1, D), lambda i: (0, j))
    mat = pl.BlockSpec((D, D), lambda i: (0, 0))
    return _call_hosting(
        body, side, name=name, grid=(T // tm,),
        in_specs=[blk(0), blk(0), blk(1), row(0), row(1), blk(0), blk(0), blk(0), row(0), row(0), mat, mat, mat],
        out_specs=[pl.BlockSpec((tm, 2 * D), lambda i: (i, 0)), blk(0), blk(0), blk(0), blk(0),
                   pl.BlockSpec((8, 2 * D), lambda i: (0, 0))],
        out_shape=[jax.ShapeDtypeStruct((T, 2 * D), BF16)] + [jax.ShapeDtypeStruct((T, D), BF16)] * 3
                  + [jax.ShapeDtypeStruct((T, D), F32), jax.ShapeDtypeStruct((8, 2 * D), F32)],
        scratch_shapes=[],
        operands=(dx, p_gate, p_gate, gate_b, gate_b, conv_out, attn_out, c1, ln_g, ln_b, w_proj, w_o, w_out))


def loss_head(x, nw, target, name):
    T, D = x.shape
    tm = min(TM_ROW, T)

    def body(x_ref, nw_ref, t_ref, dx_ref, sm_ref):
        @pl.when(pl.program_id(0) == 0)
        def _():
            sm_ref[...] = jnp.zeros_like(sm_ref)

        xv = x_ref[...]
        err = xv * _rms_scale(xv) * nw_ref[...] - t_ref[...]
        loss = 0.5 * jnp.sum(jnp.mean(err * err, axis=-1, keepdims=True))
        dxn, dnw = _rms_bwd(xv, nw_ref[...], err * (1.0 / D))
        dx_ref[...] = dxn
        sm_ref[pl.ds(0, 1), :] += dnw
        sm_ref[pl.ds(1, 1), :] += jnp.zeros((1, D), F32) + loss

    return pl.pallas_call(
        body, name=name, grid=(T // tm,),
        in_specs=[pl.BlockSpec((tm, D), lambda i: (i, 0)), pl.BlockSpec((1, D), lambda i: (0, 0)),
                  pl.BlockSpec((tm, D), lambda i: (i, 0))],
        out_specs=[pl.BlockSpec((tm, D), lambda i: (i, 0)), pl.BlockSpec((8, D), lambda i: (0, 0))],
        out_shape=[jax.ShapeDtypeStruct((T, D), F32), jax.ShapeDtypeStruct((8, D), F32)],
        compiler_params=_params(("arbitrary",)),
    )(x, nw, target)


def _by_shape(arrays):
    groups = {}
    for k, a in enumerate(arrays):
        groups.setdefault(a.shape, []).append(k)
    return list(groups.values())


def adamw(ws, gs, ms, vs, name):
    n = len(ws)
    R, C = ws[0].shape
    tr = _row_tile(R, TR_ELT)

    def body(*refs):
        for a in range(n):
            w_ref, g_ref, m_ref, v_ref, d_ref, mo_ref, vo_ref = (refs[k * n + a] for k in range(7))
            gv = g_ref[...]
            mn = ADAM_B1 * m_ref[...] + (1.0 - ADAM_B1) * gv
            vn = ADAM_B2 * v_ref[...] + (1.0 - ADAM_B2) * (gv * gv)
            m_hat = mn / (1.0 - ADAM_B1 ** ADAM_STEP)
            v_hat = vn / (1.0 - ADAM_B2 ** ADAM_STEP)
            d_ref[...] = -ADAM_LR * (m_hat / (jnp.sqrt(v_hat) + ADAM_EPS) + ADAM_WD * w_ref[...])
            mo_ref[...] = mn
            vo_ref[...] = vn

    spec = pl.BlockSpec((tr, C), lambda i: (i, 0))
    outs = pl.pallas_call(
        body, name=name, grid=(R // tr,), in_specs=[spec] * (4 * n), out_specs=[spec] * (3 * n),
        out_shape=[jax.ShapeDtypeStruct((R, C), F32)] * (3 * n),
        compiler_params=_params(("parallel",)),
    )(*ws, *gs, *ms, *vs)
    return [(outs[a], outs[n + a], outs[2 * n + a]) for a in range(n)]


def _place():
    return lax.axis_index("x"), lax.axis_index("y"), lax.axis_index("c")


def place_shards(place, ws, dtype, name):
    n = len(ws)
    R, C = ws[0].shape
    tr = _row_tile(R, TR_ELT)

    def body(pc_ref, *refs):
        for a in range(n):
            refs[n + a][...] = refs[a][...].astype(dtype)

    return pl.pallas_call(
        body, name=name,
        grid_spec=pltpu.PrefetchScalarGridSpec(
            num_scalar_prefetch=1, grid=(R // tr,),
            in_specs=[pl.BlockSpec((tr, C), lambda r, pc: (r, 0))] * n,
            out_specs=[pl.BlockSpec((None, tr, C), lambda r, pc: (pc[0], r, 0))] * n),
        out_shape=[jax.ShapeDtypeStruct((N_CHIPS, R, C), dtype)] * n,
        compiler_params=_params(("arbitrary",)),
    )(place, *ws)


def gather_side(shards, small):
    n, ns = len(shards), len(small)

    def ici_copy(dst, sems, k, j, x, y, c, sending):
        px, py = x ^ (j >> 1), y ^ (j & 1)
        slot = 2 * x + y if sending else 2 * px + py
        half = dst[k].shape[1] // 2
        part = dst[k].at[slot, pl.ds(c * half, half)] if k < n else dst[k].at[slot]
        return pltpu.make_async_remote_copy(part, part, sems[0].at[3 * k + j - 1], sems[1].at[3 * k + j - 1],
                                            device_id=(px, py, c), device_id_type=MESH)

    def d2d_copy(dst, sems, k, j, x, y, c, sending):
        half = dst[k].shape[1] // 2
        part = dst[k].at[2 * (x ^ (j >> 1)) + (y ^ (j & 1)), pl.ds((c if sending else 1 - c) * half, half)]
        return pltpu.make_async_remote_copy(part, part, sems[2].at[3 * k + j - 1], sems[3].at[3 * k + j - 1],
                                            device_id=(x, y, 1 - c), device_id_type=MESH)

    def start(src, dst, sems):
        x, y, c = _place()
        for k in range(n + ns):
            for j in (1, 2, 3):
                ici_copy(dst, sems, k, j, x, y, c, True).start()

    def relay(src, dst, sems):
        x, y, c = _place()
        for k in range(n + ns):
            for j in (1, 2, 3):
                ici_copy(dst, sems, k, j, x, y, c, False).wait_recv()
                if k < n:
                    d2d_copy(dst, sems, k, j, x, y, c, True).start()

    def finish(src, dst, sems):
        x, y, c = _place()
        for k in range(n):
            for j in (1, 2, 3):
                d2d_copy(dst, sems, k, j, x, y, c, False).wait_recv()
        for k in range(n + ns):
            for j in (1, 2, 3):
                ici_copy(dst, sems, k, j, x, y, c, True).wait_send()
                if k < n:
                    d2d_copy(dst, sems, k, j, x, y, c, True).wait_send()

    arrays = list(shards) + list(small)
    return dict(inputs=arrays, out_shapes=[jax.ShapeDtypeStruct(a.shape, a.dtype) for a in arrays],
                aliases={k: k for k in range(n + ns)},
                sems=[pltpu.SemaphoreType.DMA((3 * (n + ns),)), pltpu.SemaphoreType.DMA((3 * (n + ns),)),
                      pltpu.SemaphoreType.DMA((3 * n,)), pltpu.SemaphoreType.DMA((3 * n,))],
                start=start, relay=relay, finish=finish)


def run_side(side, name):
    n_in, n_out = len(side["inputs"]), len(side["out_shapes"])

    def body(*refs):
        src, dst, sems = refs[:n_in], refs[n_in:n_in + n_out], refs[n_in + n_out:]
        side["start"](src, dst, sems)
        if "relay" in side:
            side["relay"](src, dst, sems)
        side["finish"](src, dst, sems)

    return pl.pallas_call(
        body, name=name, in_specs=[HBM_SPEC] * n_in, out_specs=[HBM_SPEC] * n_out,
        out_shape=side["out_shapes"], input_output_aliases=side["aliases"], scratch_shapes=side["sems"],
    )(*side["inputs"])


def allreduce_small(block):
    R, C = block.shape

    def body(x_ref, out_ref, all_ref, send_sems, recv_sems, local_sem):
        x, y, c = _place()
        me, sibling = (x, y, c), (x, y, 1 - c)
        chips = [(1 - x, y), (x, 1 - y), (1 - x, 1 - y)]

        def slot(px, py, pc):
            return all_ref.at[4 * px + 2 * py + pc]

        def copy(k, block_of, to, src=None):
            return pltpu.make_async_remote_copy(
                src_ref=slot(*block_of) if src is None else src, dst_ref=slot(*block_of),
                send_sem=send_sems.at[k], recv_sem=recv_sems.at[k], device_id=to, device_id_type=MESH)

        mine = pltpu.make_async_copy(x_ref, slot(*me), local_sem)
        mine.start()
        first = [copy(0, me, sibling, src=x_ref)]
        first += [copy(1 + j, me, (*chip, c), src=x_ref) for j, chip in enumerate(chips)]
        for cp in first:
            cp.start()
        passed = [copy(4 + j, (*chip, c), sibling) for j, chip in enumerate(chips)]
        for j, chip in enumerate(chips):
            copy(1 + j, (*chip, c), me).wait_recv()
            passed[j].start()
        copy(0, sibling, me).wait_recv()
        for j, chip in enumerate(chips):
            copy(4 + j, (*chip, 1 - c), me).wait_recv()
        for cp in first + passed:
            cp.wait_send()
        mine.wait()
        total = all_ref[0]
        for d in range(1, N_DEV):
            total = total + all_ref[d]
        out_ref[...] = total

    return pl.pallas_call(
        body, name="allreduce_small",
        in_specs=[pl.BlockSpec(memory_space=pltpu.VMEM)], out_specs=pl.BlockSpec(memory_space=pltpu.VMEM),
        out_shape=jax.ShapeDtypeStruct((R, C), F32),
        scratch_shapes=[pltpu.VMEM((N_DEV, R, C), F32), pltpu.SemaphoreType.DMA((7,)),
                        pltpu.SemaphoreType.DMA((7,)), pltpu.SemaphoreType.DMA],
        compiler_params=pltpu.CompilerParams(vmem_limit_bytes=VMEM_LIMIT),
    )(block)


def exchange_siblings_side(grads):
    n = len(grads)

    def copies(src, dst, sems):
        x, y, c = _place()
        for k in range(n):
            half = src[k].shape[1] // 2
            yield pltpu.make_async_remote_copy(src[k].at[:, pl.ds((1 - c) * half, half)], dst[k],
                                               sems[0].at[k], sems[1].at[k],
                                               device_id=(x, y, 1 - c), device_id_type=MESH)

    def start(src, dst, sems):
        for cp in copies(src, dst, sems):
            cp.start()

    def finish(src, dst, sems):
        for cp in copies(src, dst, sems):
            cp.wait()

    return dict(inputs=list(grads), aliases={},
                out_shapes=[jax.ShapeDtypeStruct((N_CHIPS, g.shape[1] // 2, g.shape[2]), F32) for g in grads],
                sems=[pltpu.SemaphoreType.DMA((n,)), pltpu.SemaphoreType.DMA((n,))], start=start, finish=finish)


def rs_chip_sum(place, grads, sibs, name):
    n = len(grads)
    NP, R, C = grads[0].shape
    half = R // 2
    tr = _row_tile(half, TR_ELT)
    nr = half // tr

    def body(pc_ref, *refs):
        q = pl.program_id(1)
        for a in range(n):
            g_ref, s_ref, wire_ref, own_ref = (refs[k * n + a] for k in range(4))
            total = g_ref[...] + s_ref[...]
            wire_ref[...] = total.astype(BF16)

            @pl.when(q == pc_ref[0])
            def _():
                own_ref[...] = total

    outs = pl.pallas_call(
        body, name=name,
        grid_spec=pltpu.PrefetchScalarGridSpec(
            num_scalar_prefetch=1, grid=(nr, NP),
            in_specs=[pl.BlockSpec((None, tr, C), lambda r, q, pc: (q, pc[1] * nr + r, 0))] * n
                     + [pl.BlockSpec((None, tr, C), lambda r, q, pc: (q, r, 0))] * n,
            out_specs=[pl.BlockSpec((None, tr, C), lambda r, q, pc: (q, r, 0))] * n
                      + [pl.BlockSpec((tr, C), lambda r, q, pc: (r, 0))] * n),
        out_shape=[jax.ShapeDtypeStruct((NP, half, C), BF16)] * n + [jax.ShapeDtypeStruct((half, C), F32)] * n,
        compiler_params=_params(("arbitrary", "arbitrary")),
    )(place, *grads, *sibs)
    return outs[:n], outs[n:]


def exchange_chips_side(wires):
    n = len(wires)

    def copies(src, dst, sems):
        x, y, c = _place()
        for k in range(n):
            for j in (1, 2, 3):
                qx, qy = x ^ (j >> 1), y ^ (j & 1)
                yield pltpu.make_async_remote_copy(src[k].at[2 * qx + qy], dst[k].at[2 * x + y],
                                                   sems[0].at[3 * k + j - 1], sems[1].at[3 * k + j - 1],
                                                   device_id=(qx, qy, c), device_id_type=MESH)

    def start(src, dst, sems):
        for cp in copies(src, dst, sems):
            cp.start()

    def finish(src, dst, sems):
        for cp in copies(src, dst, sems):
            cp.wait()

    return dict(inputs=list(wires), out_shapes=[jax.ShapeDtypeStruct(w.shape, BF16) for w in wires], aliases={},
                sems=[pltpu.SemaphoreType.DMA((3 * n,)), pltpu.SemaphoreType.DMA((3 * n,))],
                start=start, finish=finish)


SEM_SPEC = pl.BlockSpec(memory_space=pltpu.SEMAPHORE)


def exchange_chips_start(wires, name):
    n = len(wires)
    side = exchange_chips_side(wires)

    def body(*refs):
        src, land, sems = refs[:n], refs[n:2 * n], refs[2 * n:2 * n + 2]
        side["start"](src, land, sems)
        refs[-1][...] = jnp.zeros_like(refs[-1])

    hbm = [pltpu.HBM(w.shape, w.dtype) for w in wires]
    outs = pl.pallas_call(
        body, name=name, in_specs=[HBM_SPEC] * (2 * n),
        out_specs=[SEM_SPEC, SEM_SPEC] + [HBM_SPEC] * (2 * n) + [pl.BlockSpec(memory_space=pltpu.VMEM)],
        out_shape=list(side["sems"]) + hbm + hbm + [jax.ShapeDtypeStruct((8, 128), F32)],
        input_output_aliases={k: 2 + k for k in range(2 * n)},
        compiler_params=pltpu.CompilerParams(has_side_effects=pltpu.SideEffectType.DATAFLOW_SIDE_EFFECTING),
    )(*[pltpu.with_memory_space_constraint(w, pltpu.HBM) for w in wires],
      *[pltpu.with_memory_space_constraint(lax.empty(w.shape, w.dtype), pltpu.HBM) for w in wires])
    return outs[0], outs[1], outs[2:2 + n], outs[2 + n:2 + 2 * n], outs[-1]


def exchange_chips_wait(send_sems, recv_sems, wires, lands, after, name):
    n = len(wires)
    side = exchange_chips_side(wires)

    def body(*refs):
        side["finish"](refs[:n], refs[n:2 * n], refs[2 * n:2 * n + 2])

    hbm = [pltpu.HBM(w.shape, w.dtype) for w in wires]
    outs = pl.pallas_call(
        body, name=name, in_specs=[HBM_SPEC] * (2 * n) + [SEM_SPEC, SEM_SPEC] + [HBM_SPEC] * len(after),
        out_specs=[HBM_SPEC] * (2 * n), out_shape=hbm + hbm,
        input_output_aliases={k: k for k in range(2 * n)},
        compiler_params=pltpu.CompilerParams(has_side_effects=pltpu.SideEffectType.DATAFLOW_SIDE_EFFECTING),
    )(*wires, *lands, send_sems, recv_sems, *after)
    return outs[n:]


def rs_final_sum(place, owns, gots, after, name):
    n = len(owns)
    NP, half, C = gots[0].shape
    tr = _row_tile(half, TR_ELT)
    nr = half // tr

    def body(pc_ref, *refs):
        for a in range(n):
            own_ref, g1_ref, g2_ref, g3_ref = (refs[k * n + a] for k in range(4))
            refs[4 * n + 1 + a][...] = (((own_ref[...] + g1_ref[...].astype(F32)) + g2_ref[...].astype(F32))
                                        + g3_ref[...].astype(F32))

    slot = lambda j: pl.BlockSpec((None, tr, C), lambda r, pc: (pc[0] ^ j, r, 0))
    return pl.pallas_call(
        body, name=name,
        grid_spec=pltpu.PrefetchScalarGridSpec(
            num_scalar_prefetch=1, grid=(nr,),
            in_specs=[pl.BlockSpec((tr, C), lambda r, pc: (r, 0))] * n + [slot(1)] * n + [slot(2)] * n + [slot(3)] * n
                     + [pl.BlockSpec((8, 128), lambda r, pc: (0, 0))],
            out_specs=[pl.BlockSpec((tr, C), lambda r, pc: (pc[1] * nr + r, 0))] * n),
        out_shape=[jax.ShapeDtypeStruct((2 * half, C), F32)] * n,
        compiler_params=_params(("arbitrary",)),
    )(place, *owns, *gots, *gots, *gots, after)


def rs_share_siblings(totals, name):
    n = len(totals)

    def body(*refs):
        dst = refs[n:2 * n]
        send_sems, recv_sems = refs[2 * n:]
        x, y, c = _place()
        copies = []
        for k in range(n):
            half = dst[k].shape[0] // 2
            rows = dst[k].at[pl.ds(c * half, half)]
            cp = pltpu.make_async_remote_copy(rows, rows, send_sems.at[k], recv_sems.at[k],
                                              device_id=(x, y, 1 - c), device_id_type=MESH)
            cp.start()
            copies.append(cp)
        for k, cp in enumerate(copies):
            cp.wait_send()
            half = dst[k].shape[0] // 2
            got = dst[k].at[pl.ds((1 - c) * half, half)]
            pltpu.make_async_remote_copy(got, got, send_sems.at[k], recv_sems.at[k],
                                         device_id=(x, y, c), device_id_type=MESH).wait_recv()

    return pl.pallas_call(
        body, name=name,
        in_specs=[HBM_SPEC] * n, out_specs=[HBM_SPEC] * n,
        out_shape=[jax.ShapeDtypeStruct(t.shape, F32) for t in totals],
        input_output_aliases={k: k for k in range(n)},
        scratch_shapes=[pltpu.SemaphoreType.DMA((n,)), pltpu.SemaphoreType.DMA((n,))],
    )(*totals)


def rs_to_wires(place, grads, tag, sibs=None):
    if sibs is None:
        sibs = run_side(exchange_siblings_side(grads), f"rs_exchange_siblings_{tag}")
    wires, owns = [None] * len(grads), [None] * len(grads)
    for ks in _by_shape(grads):
        ws, os_ = rs_chip_sum(place, [grads[k] for k in ks], [sibs[k] for k in ks], f"rs_chip_sum_{tag}{ks[0]}")
        for k, w, o in zip(ks, ws, os_):
            wires[k], owns[k] = w, o
    return wires, owns


def rs_finish(place, owns, gots, after, tag):
    totals = [None] * len(owns)
    for ks in _by_shape(owns):
        sums = rs_final_sum(place, [owns[k] for k in ks], [gots[k] for k in ks], after, f"rs_final_sum_{tag}{ks[0]}")
        for k, t in zip(ks, sums):
            totals[k] = t
    return rs_share_siblings(totals, f"rs_share_siblings_{tag}")


def _rope_tables(positions):
    half = HEAD_DIM // 2
    inv_freq = ROPE_THETA ** (-jnp.arange(half, dtype=F32) / half)
    ang = positions.astype(F32)[:, None] * inv_freq
    lanes = jnp.arange(128)
    spread = (lanes[None, :] % half == jnp.arange(half)[:, None]).astype(F32)
    signed = spread * jnp.where(lanes % HEAD_DIM < half, -1.0, 1.0).astype(F32)
    exact = lax.Precision.HIGHEST
    return jnp.dot(jnp.cos(ang), spread, precision=exact), jnp.dot(jnp.sin(ang), signed, precision=exact)


def _cols_from_pieces(pieces, start, stop):
    C = pieces.shape[2]
    parts = []
    for q in range(N_CHIPS):
        lo, hi = max(start, q * C), min(stop, (q + 1) * C)
        if lo < hi:
            parts.append(pieces[q][:, lo - q * C:hi - q * C])
    return parts[0] if len(parts) == 1 else jnp.concatenate(parts, axis=1)


def _pieces_from_groups(groups):
    C = sum(g.shape[1] for g in groups) // N_CHIPS
    pieces = []
    for q in range(N_CHIPS):
        parts, off = [], 0
        for g in groups:
            lo, hi = max(q * C, off), min((q + 1) * C, off + g.shape[1])
            if lo < hi:
                parts.append(g[:, lo - off:hi - off])
            off += g.shape[1]
        pieces.append(parts[0] if len(parts) == 1 else jnp.concatenate(parts, axis=1))
    return jnp.stack(pieces)


def kernel(x, positions, ffn1_norm, ffn1_w_gate, ffn1_w_up, ffn1_w_down, mix_norm, w_in, conv_dw_w, conv_dw_b, conv_ln_g, conv_ln_b, conv_w_proj, attn_sinks, attn_w_o, gate_b, w_out, ffn2_norm, ffn2_w_gate, ffn2_w_up, ffn2_w_down, final_norm, loss_target, m_ffn1_norm, m_ffn1_w_gate, m_ffn1_w_up, m_ffn1_w_down, m_mix_norm, m_w_in, m_conv_dw_w, m_conv_dw_b, m_conv_ln_g, m_conv_ln_b, m_conv_w_proj, m_attn_sinks, m_attn_w_o, m_gate_b, m_w_out, m_ffn2_norm, m_ffn2_w_gate, m_ffn2_w_up, m_ffn2_w_down, m_final_norm, v_ffn1_norm, v_ffn1_w_gate, v_ffn1_w_up, v_ffn1_w_down, v_mix_norm, v_w_in, v_conv_dw_w, v_conv_dw_b, v_conv_ln_g, v_conv_ln_b, v_conv_w_proj, v_attn_sinks, v_attn_w_o, v_gate_b, v_w_out, v_ffn2_norm, v_ffn2_w_gate, v_ffn2_w_up, v_ffn2_w_down, v_final_norm):
    weights = dict(ffn1_norm=ffn1_norm, ffn1_w_gate=ffn1_w_gate, ffn1_w_up=ffn1_w_up, ffn1_w_down=ffn1_w_down,
                   mix_norm=mix_norm, w_in=w_in, conv_dw_w=conv_dw_w, conv_dw_b=conv_dw_b, conv_ln_g=conv_ln_g,
                   conv_ln_b=conv_ln_b, conv_w_proj=conv_w_proj, attn_sinks=attn_sinks, attn_w_o=attn_w_o,
                   gate_b=gate_b, w_out=w_out, ffn2_norm=ffn2_norm, ffn2_w_gate=ffn2_w_gate, ffn2_w_up=ffn2_w_up,
                   ffn2_w_down=ffn2_w_down, final_norm=final_norm)
    m_in = dict(ffn1_norm=m_ffn1_norm, ffn1_w_gate=m_ffn1_w_gate, ffn1_w_up=m_ffn1_w_up, ffn1_w_down=m_ffn1_w_down,
                mix_norm=m_mix_norm, w_in=m_w_in, conv_dw_w=m_conv_dw_w, conv_dw_b=m_conv_dw_b,
                conv_ln_g=m_conv_ln_g, conv_ln_b=m_conv_ln_b, conv_w_proj=m_conv_w_proj, attn_sinks=m_attn_sinks,
                attn_w_o=m_attn_w_o, gate_b=m_gate_b, w_out=m_w_out, ffn2_norm=m_ffn2_norm,
                ffn2_w_gate=m_ffn2_w_gate, ffn2_w_up=m_ffn2_w_up, ffn2_w_down=m_ffn2_w_down, final_norm=m_final_norm)
    v_in = dict(ffn1_norm=v_ffn1_norm, ffn1_w_gate=v_ffn1_w_gate, ffn1_w_up=v_ffn1_w_up, ffn1_w_down=v_ffn1_w_down,
                mix_norm=v_mix_norm, w_in=v_w_in, conv_dw_w=v_conv_dw_w, conv_dw_b=v_conv_dw_b,
                conv_ln_g=v_conv_ln_g, conv_ln_b=v_conv_ln_b, conv_w_proj=v_conv_w_proj, attn_sinks=v_attn_sinks,
                attn_w_o=v_attn_w_o, gate_b=v_gate_b, w_out=v_w_out, ffn2_norm=v_ffn2_norm,
                ffn2_w_gate=v_ffn2_w_gate, ffn2_w_up=v_ffn2_w_up, ffn2_w_down=v_ffn2_w_down, final_norm=v_final_norm)
    names = list(weights)
    big = ["ffn1_w_gate", "ffn1_w_up", "ffn1_w_down", "w_in", "conv_w_proj", "attn_w_o", "w_out",
           "ffn2_w_gate", "ffn2_w_up", "ffn2_w_down"]
    transposed = [k for k in big if k.endswith(("w_gate", "w_up"))]
    for k in transposed:
        weights[k], m_in[k], v_in[k] = (jnp.swapaxes(a, 1, 2) for a in (weights[k], m_in[k], v_in[k]))

    xs = x[0]
    T, D = xs.shape
    KV = (w_in.shape[2] * N_CHIPS - 5 * D) // 2
    n_heads = D // HEAD_DIM
    my_chip = 2 * lax.axis_index("x") + lax.axis_index("y")
    place = jnp.stack([my_chip, lax.axis_index("c")]).astype(jnp.int32)

    first, mixer_w, second = big[:3], big[3:7], big[7:]
    placed = {}
    for group in (first, mixer_w, second):
        for ks in _by_shape([weights[k][0] for k in group]):
            same = [group[k] for k in ks]
            placed.update(zip(same, place_shards(place, [weights[k][0] for k in same], BF16, f"place_{same[0]}")))
    placed_dw, = place_shards(place, [conv_dw_w[0]], F32, "place_conv_dw_w")
    wg1, wu1, wd1 = run_side(gather_side([placed[k] for k in first], []), "gather_ffn1")
    x1, h1, g1, u1, *gathered = ffn_fwd(x[0], ffn1_norm, wg1, wu1, wd1, "ffn1_fwd",
                                        side=gather_side([placed[k] for k in mixer_w], [placed_dw]))
    full = dict(zip(mixer_w + ["conv_dw_w"], gathered))
    w_glu = _cols_from_pieces(full["w_in"], 0, 2 * D)
    w_qkv = _cols_from_pieces(full["w_in"], 2 * D, 3 * D + 2 * KV)
    w_gate = _cols_from_pieces(full["w_in"], 3 * D + 2 * KV, 5 * D + 2 * KV)
    w_proj = full["conv_w_proj"].reshape(D, D)
    w_o = full["attn_w_o"].reshape(D, D)
    w_out_f = full["w_out"].reshape(D, D)
    dw_w = full["conv_dw_w"].transpose(1, 0, 2).reshape(CONV_WIDTH, D)
    dw_w = jnp.concatenate([dw_w, jnp.zeros((CONV_HALO - CONV_WIDTH, D), F32)], axis=0)
    cs, sn = _rope_tables(positions[0])
    fn_row = final_norm.reshape(1, D)

    h2, p_glu, p_gate, qr, kr, vb = mix_in_fwd(x1, mix_norm, w_glu, w_qkv, w_gate, cs, sn, "mix_in_fwd")
    c1, c3 = conv_fwd(p_glu, dw_w, conv_dw_b, conv_ln_g, conv_ln_b, "conv_fwd")
    o, wg2, wu2, wd2 = attn_fwd(qr, kr, vb, attn_sinks, "attn_fwd",
                                side=gather_side([placed[k] for k in second], []))
    x2, conv_out, attn_out, merged = merge_fwd(x1, c3, o, p_gate, gate_b, w_proj, w_o, w_out_f, "merge_fwd")
    x3, h3, g2, u2 = ffn_fwd(x2, ffn2_norm, wg2, wu2, wd2, "ffn2_fwd")

    dx3, head_sums = loss_head(x3, fn_row, loss_target[0], "loss_head")
    dx2, dwg2, dwu2, dwd2, d_ffn2_norm = ffn_bwd(x2, ffn2_norm, h3, g2, u2, wg2, wu2, wd2, dx3, "ffn2_bwd")
    ffn2_grads = [dwg2, dwu2, dwd2]
    d_gates, d_conv_out, d_attn_out, d_o, dc1, merge_sums, *sibs_f2 = merge_bwd(
        dx2, p_gate, gate_b, conv_out, attn_out, c1, conv_ln_g, conv_ln_b, w_proj, w_o, w_out_f, "merge_bwd",
        side=exchange_siblings_side(ffn2_grads))
    d_w_out, d_w_proj, d_w_o = matmul_tn([(merged, dx2), (c3, d_conv_out), (o, d_attn_out)], TK_TN, "d_w_square")
    wires_f2, owns_f2 = rs_to_wires(place, ffn2_grads, "ffn2", sibs=sibs_f2)
    d_glu, d_dw_w, *gots_f2 = conv_bwd(p_glu, dc1, dw_w, "conv_bwd", side=exchange_chips_side(wires_f2))
    dwc = D // N_CHIPS
    square_grads = [d_w_proj.reshape(N_CHIPS, dwc, D), d_w_o.reshape(N_CHIPS, dwc, D),
                    d_w_out.reshape(N_CHIPS, dwc, D)]
    dq, dk, dv, d_sinks, *sibs_sq = attn_bwd(qr, kr, vb, o, d_o, attn_sinks, "attn_bwd",
                                             side=exchange_siblings_side(square_grads))
    d_qkv = rope_bwd(dq, dk, dv, cs, sn, "rope_bwd")
    d_w_in = _pieces_from_groups(matmul_tn([(h2, d_glu), (h2, d_qkv), (h2, d_gates)], TK_TN // 2, "d_w_in"))
    dx1, d_mix_norm, *sib_w_in = mix_in_bwd([d_glu, d_qkv, d_gates], [w_glu, w_qkv, w_gate], x1, mix_norm, dx2,
                                            "mix_in_bwd", side=exchange_siblings_side([d_w_in]))
    wires_m, owns_m = rs_to_wires(place, [d_w_in] + square_grads, "mixer", sibs=list(sib_w_in) + list(sibs_sq))
    dx0, dwg1, dwu1, dwd1, d_ffn1_norm, *gots_m = ffn_bwd(xs, ffn1_norm, h1, g1, u1, wg1, wu1, wd1, dx1, "ffn1_bwd",
                                                          side=exchange_chips_side(wires_m))
    wires_l, owns_l = rs_to_wires(place, [dwg1, dwu1, dwd1], "ffn1")
    send_sems, recv_sems, wires_l, lands_l, token = exchange_chips_start(wires_l, "rs_exchange_chips_ffn1_start")
    early_names = ["ffn2_w_gate", "ffn2_w_up", "ffn2_w_down", "w_in", "conv_w_proj", "attn_w_o", "w_out"]
    late_names = ["ffn1_w_gate", "ffn1_w_up", "ffn1_w_down"]
    reduced_early = rs_finish(place, owns_f2 + owns_m, list(gots_f2) + list(gots_m), token, "early")

    pad_row = lambda v: jnp.pad(v, ((0, 0), (0, D - v.shape[1])))
    small_rows = jnp.concatenate([
        d_ffn1_norm, d_mix_norm, merge_sums[2:3, :D], merge_sums[1:2, :D], merge_sums[1:2, D:],
        pad_row(d_sinks[0:1, :n_heads]), merge_sums[0:1, :D], merge_sums[0:1, D:], d_ffn2_norm,
        head_sums[0:1], head_sums[1:2], jnp.zeros((5, D), F32), d_dw_w], axis=0)
    small = allreduce_small(small_rows)
    loss = small[10, 0]
    grads = {"ffn1_norm": small[0:1], "mix_norm": small[1:2], "conv_dw_b": small[2:3], "conv_ln_g": small[3:4],
             "conv_ln_b": small[4:5], "attn_sinks": small[5:6, :n_heads],
             "gate_b": jnp.concatenate([small[6:7], small[7:8]], axis=1), "ffn2_norm": small[8:9],
             "final_norm": small[9:10]}
    grads["conv_dw_w"] = lax.dynamic_slice(small[16:16 + CONV_WIDTH], (0, my_chip * dwc), (CONV_WIDTH, dwc))
    grads.update(zip(early_names, reduced_early))

    deltas, new_m, new_v = {}, {}, {}

    def apply_adamw(ks):
        flat = lambda a, k: a.reshape(-1, weights[k].shape[-1])
        done = {}
        for idx in _by_shape([flat(grads[k], k) for k in ks]):
            same = [ks[i] for i in idx]
            results = adamw([flat(weights[k], k) for k in same], [flat(grads[k], k) for k in same],
                            [flat(m_in[k], k) for k in same], [flat(v_in[k], k) for k in same], f"adamw_{same[0]}")
            for k, (d, mn, vn) in zip(same, results):
                shape = weights[k].shape
                grads[k] = grads[k].reshape(shape)
                deltas[k], new_m[k], new_v[k] = d.reshape(shape), mn.reshape(shape), vn.reshape(shape)
                done[k] = d
        return done

    done = apply_adamw([k for k in names if k not in late_names])
    gots_l = exchange_chips_wait(send_sems, recv_sems, wires_l, lands_l, [done[k] for k in early_names],
                                 "rs_exchange_chips_ffn1_wait")
    grads.update(zip(late_names, rs_finish(place, owns_l, gots_l, token, "late")))
    apply_adamw(late_names)
    for k in transposed:
        for group in (grads, deltas, new_m, new_v):
            group[k] = jnp.swapaxes(group[k], 1, 2)

    return (loss, dx0[None], *[grads[k] for k in names], *[deltas[k] for k in names],
            *[new_m[k] for k in names], *[new_v[k] for k in names])
```

```python
import functools

import jax
import jax.numpy as jnp
from jax import lax
from jax.experimental import pallas as pl
from jax.experimental.pallas import tpu as pltpu

F32 = jnp.float32
BF16 = jnp.bfloat16
MESH = pl.DeviceIdType.MESH

HEAD_DIM = 64
WINDOW = 128
CONV_WIDTH = 31
CONV_HALO = 32
ROPE_THETA = 10000.0
EPS = 1e-6
LN_EPS = 1e-5
NEG_INF = -1e30
N_CHIPS = 4
N_DEV = 8

ADAM_LR = 0.001
ADAM_B1 = 0.9
ADAM_B2 = 0.999
ADAM_EPS = 1e-08
ADAM_WD = 0.01
ADAM_STEP = 10

TM_FFN = 512
TM_FFN_FWD = 1024
TM_ROW = 256
TM_MIX = 512
TK_TN = 1024
TR_ELT = 256
VMEM_LIMIT = 56 * 1024 * 1024

NT_DIMS = (((1,), (1,)), ((), ()))
TN_DIMS = (((0,), (0,)), ((), ()))


def _row_tile(rows, cap):
    for t in range(min(cap, rows), 15, -1):
        if rows % t == 0 and t % 16 == 0:
            return t
    return rows


def _params(sem):
    return pltpu.CompilerParams(dimension_semantics=sem, vmem_limit_bytes=VMEM_LIMIT)


def _dot(a, b):
    return jnp.dot(a, b, preferred_element_type=F32)


def _dot_nt(a, b):
    return lax.dot_general(a, b, NT_DIMS, preferred_element_type=F32)


def _dot_tn(a, b):
    return lax.dot_general(a, b, TN_DIMS, preferred_element_type=F32)


def _split_rows(dot, a, b):
    m = a.shape[0] // 2
    return jnp.concatenate([dot(a[:m], b), dot(a[m:], b)], axis=0)


def _sigmoid(x):
    return jax.nn.sigmoid(x)


def _rms_scale(xv):
    return lax.rsqrt(jnp.mean(xv * xv, axis=-1, keepdims=True) + EPS)


def _rms_bwd(xv, nw, dh):
    r = _rms_scale(xv)
    dn = dh * nw
    dx = r * dn - xv * (r * r * r) * jnp.mean(dn * xv, axis=-1, keepdims=True)
    dnw = jnp.sum(dh * (xv * r), axis=0, keepdims=True)
    return dx, dnw


def _silu_grad(z, s):
    return s * (1.0 + z * (1.0 - s))


HBM_SPEC = pl.BlockSpec(memory_space=pl.ANY)


def _call_hosting(body, side, *, grid, in_specs, out_specs, out_shape, scratch_shapes, operands, name, aliases=None):
    params = _params(("arbitrary",) * len(grid))
    aliases = dict(aliases or {})
    if side is None:
        return pl.pallas_call(body, name=name, grid=grid, in_specs=in_specs, out_specs=out_specs, out_shape=out_shape,
                              scratch_shapes=scratch_shapes, input_output_aliases=aliases,
                              compiler_params=params)(*operands)
    n_in, n_out, n_scr = len(in_specs), len(out_shape), len(scratch_shapes)
    s_in, s_out = len(side["inputs"]), len(side["out_shapes"])

    steps = 1
    for extent in grid:
        steps *= extent

    def at_step(index):
        linear = pl.program_id(0)
        for a in range(1, len(grid)):
            linear = linear * grid[a] + pl.program_id(a)
        return linear == index

    def hosted(*refs):
        b = n_in + s_in
        c = b + n_out
        d = c + s_out
        e = d + n_scr
        src, dst, sems = refs[n_in:b], refs[c:d], refs[e:]

        @pl.when(at_step(0))
        def _():
            side["start"](src, dst, sems)

        if "relay" in side:
            @pl.when(at_step(min((3 * steps) // 4, steps - 1)))
            def _():
                side["relay"](src, dst, sems)

        body(*refs[:n_in], *refs[b:c], *refs[d:e])

        @pl.when(at_step(steps - 1))
        def _():
            side["finish"](src, dst, sems)

    return pl.pallas_call(
        hosted, name=name, grid=grid, in_specs=list(in_specs) + [HBM_SPEC] * s_in,
        out_specs=list(out_specs) + [HBM_SPEC] * s_out, out_shape=list(out_shape) + list(side["out_shapes"]),
        scratch_shapes=list(scratch_shapes) + list(side["sems"]),
        input_output_aliases={**aliases, **{n_in + a: n_out + b for a, b in side["aliases"].items()}},
        compiler_params=params)(*operands, *side["inputs"])


def ffn_fwd(x, nw, wg, wu, wd, name, side=None):
    T, D = x.shape
    NP, Fs, _ = wg.shape
    tm = min(TM_FFN_FWD, T)

    def body(x_ref, nw_ref, wg_ref, wu_ref, wd_ref, xo_ref, h_ref, g_ref, u_ref, acc_ref):
        j = pl.program_id(1)

        @pl.when(j == 0)
        def _():
            xv = x_ref[...]
            h_ref[...] = (xv * _rms_scale(xv) * nw_ref[...]).astype(BF16)
            acc_ref[...] = jnp.zeros_like(acc_ref)

        h = h_ref[...]
        g = _dot_nt(h, wg_ref[...])
        u = _dot_nt(h, wu_ref[...])
        a = (g * _sigmoid(g)) * u
        g_ref[...] = g.astype(BF16)
        u_ref[...] = u.astype(BF16)
        acc_ref[...] += _dot(a.astype(BF16), wd_ref[...])

        @pl.when(j == NP - 1)
        def _():
            xo_ref[...] = x_ref[...] + 0.5 * acc_ref[...]

    return _call_hosting(
        body, side, name=name, grid=(T // tm, NP),
        in_specs=[pl.BlockSpec((tm, D), lambda i, j: (i, 0)),
                  pl.BlockSpec((1, D), lambda i, j: (0, 0)),
                  pl.BlockSpec((None, Fs, D), lambda i, j: (j, 0, 0)),
                  pl.BlockSpec((None, Fs, D), lambda i, j: (j, 0, 0)),
                  pl.BlockSpec((None, Fs, D), lambda i, j: (j, 0, 0))],
        out_specs=[pl.BlockSpec((tm, D), lambda i, j: (i, 0)),
                   pl.BlockSpec((tm, D), lambda i, j: (i, 0)),
                   pl.BlockSpec((None, tm, Fs), lambda i, j: (j, i, 0)),
                   pl.BlockSpec((None, tm, Fs), lambda i, j: (j, i, 0))],
        out_shape=[jax.ShapeDtypeStruct((T, D), F32), jax.ShapeDtypeStruct((T, D), BF16),
                   jax.ShapeDtypeStruct((NP, T, Fs), BF16), jax.ShapeDtypeStruct((NP, T, Fs), BF16)],
        scratch_shapes=[pltpu.VMEM((tm, D), F32)],
        operands=(x, nw, wg, wu, wd))


def _ffn_bwd_piece(j, h, g, u, wg, wu, wd, dout, dh_in, dws_in, name, side, norm):
    T, D = h.shape
    NP, Fs, _ = wg.shape
    tm = min(TM_FFN, T)
    n_in = 7 + (dh_in is not None) + (2 if norm else 0) + (3 if dws_in else 0)

    def body(*refs):
        h_ref, g_ref, u_ref, wg_ref, wu_ref, wd_ref, do_ref = refs[:7]
        dhin_ref = refs[7] if dh_in is not None else None
        dh_ref, dwg_ref, dwu_ref, dwd_ref = refs[n_in:n_in + 4]

        @pl.when(pl.program_id(0) == 0)
        def _():
            dwg_ref[...] = jnp.zeros_like(dwg_ref)
            dwu_ref[...] = jnp.zeros_like(dwu_ref)
            dwd_ref[...] = jnp.zeros_like(dwd_ref)
            if norm:
                refs[n_in + 4][...] = jnp.zeros_like(refs[n_in + 4])

        dob = (0.5 * do_ref[...]).astype(BF16)
        da = _split_rows(_dot_nt, dob, wd_ref[...])
        gf = g_ref[...].astype(F32)
        uf = u_ref[...].astype(F32)
        s = _sigmoid(gf)
        act = gf * s
        dg = (da * uf * _silu_grad(gf, s)).astype(BF16)
        du = (da * act).astype(BF16)
        a = (act * uf).astype(BF16)
        dh = _dot(dg, wg_ref[...]) + _dot(du, wu_ref[...])
        dh = dh if dhin_ref is None else dhin_ref[...] + dh
        if norm:
            x_ref, nw_ref = refs[7 + (dh_in is not None):9 + (dh_in is not None)]
            dxn, dnw = _rms_bwd(x_ref[...], nw_ref[...], dh)
            dh_ref[...] = do_ref[...] + dxn
            refs[n_in + 4][...] += dnw
        else:
            dh_ref[...] = dh
        hb = h_ref[...]
        dwg_ref[...] += _dot_tn(dg, hb)
        dwu_ref[...] += _dot_tn(du, hb)
        dwd_ref[...] += _dot_tn(a, dob)

    rows = pl.BlockSpec((tm, D), lambda i: (i, 0))
    piece = pl.BlockSpec((None, tm, Fs), lambda i: (j, i, 0))
    slot = pl.BlockSpec((None, Fs, D), lambda i: (j, 0, 0), pipeline_mode=pl.Buffered(1))
    in_specs = [rows, piece, piece, slot, slot, slot, rows]
    operands = [h, g, u, wg, wu, wd, dout]
    aliases = {}
    if dh_in is not None:
        in_specs.append(rows)
        operands.append(dh_in)
    if norm:
        in_specs += [rows, pl.BlockSpec((1, D), lambda i: (0, 0))]
        operands += list(norm)
    if dws_in:
        aliases = {len(operands) + k: 1 + k for k in range(3)}
        in_specs += [HBM_SPEC] * 3
        operands += list(dws_in)
    out_specs = [rows, slot, slot, slot]
    out_shape = [jax.ShapeDtypeStruct((T, D), F32)] + [jax.ShapeDtypeStruct((NP, Fs, D), F32)] * 3
    if norm:
        out_specs.append(pl.BlockSpec((1, D), lambda i: (0, 0)))
        out_shape.append(jax.ShapeDtypeStruct((1, D), F32))
    return _call_hosting(body, side, name=name, grid=(T // tm,), in_specs=in_specs, out_specs=out_specs,
                         out_shape=out_shape, scratch_shapes=[], aliases=aliases, operands=tuple(operands))


def ffn_bwd(x, nw, h, g, u, wg, wu, wd, dout, name, side=None):
    NP = wg.shape[0]
    dh, dws, extra = None, None, []
    for j in range(NP):
        dh, *rest = _ffn_bwd_piece(j, h, g, u, wg, wu, wd, dout, dh, dws, f"{name}_{j}",
                                   side if j == 0 else None, (x, nw) if j == NP - 1 else None)
        dws, rest = rest[:3], rest[3:]
        if j == 0:
            extra = rest[1:] if NP == 1 else rest
    return (dh, *dws, rest[0], *extra)


def mix_in_fwd(x, nw, w_glu, w_qkv, w_gate, cs, sn, name):
    T, D = x.shape
    KV = (w_qkv.shape[1] - D) // 2
    tm = min(TM_MIX, T)

    def body(x_ref, nw_ref, wa_ref, wq_ref, wg_ref, cs_ref, sn_ref, h_ref, pa_ref, pg_ref, q_ref, k_ref, v_ref):
        xv = x_ref[...]
        h = (xv * _rms_scale(xv) * nw_ref[...]).astype(BF16)
        h_ref[...] = h
        pa_ref[...] = _dot(h, wa_ref[...])
        pg_ref[...] = _dot(h, wg_ref[...])
        qkv = _dot(h, wq_ref[...])
        cs_v, sn_v = cs_ref[...], sn_ref[...]
        q_ref[...] = _rope_chunks(qkv[:, :D], cs_v, sn_v, 1.0).astype(BF16)
        k_ref[...] = _rope_chunks(qkv[:, D:D + KV], cs_v, sn_v, 1.0).astype(BF16)
        v_ref[...] = qkv[:, D + KV:].astype(BF16)

    rows = lambda w: pl.BlockSpec((tm, w), lambda i: (i, 0))
    whole = lambda a: pl.BlockSpec(a.shape, lambda i: (0, 0), pipeline_mode=pl.Buffered(1))
    return pl.pallas_call(
        body, name=name, grid=(T // tm,),
        in_specs=[rows(D), whole(nw), whole(w_glu), whole(w_qkv), whole(w_gate), rows(128), rows(128)],
        out_specs=[rows(D), rows(2 * D), rows(2 * D), rows(D), rows(KV), rows(KV)],
        out_shape=[jax.ShapeDtypeStruct((T, D), BF16), jax.ShapeDtypeStruct((T, 2 * D), F32),
                   jax.ShapeDtypeStruct((T, 2 * D), F32), jax.ShapeDtypeStruct((T, D), BF16),
                   jax.ShapeDtypeStruct((T, KV), BF16), jax.ShapeDtypeStruct((T, KV), BF16)],
        compiler_params=_params(("parallel",)),
    )(x, nw, w_glu, w_qkv, w_gate, cs, sn)


def matmul_tn(pairs, tk, name):
    n = len(pairs)
    T = pairs[0][0].shape[0]
    tk = min(tk, T)

    def body(*refs):
        @pl.when(pl.program_id(0) == 0)
        def _():
            for a in range(n):
                refs[2 * n + a][...] = jnp.zeros_like(refs[2 * n + a])

        for a in range(n):
            refs[2 * n + a][...] += _dot_tn(refs[2 * a][...].astype(BF16), refs[2 * a + 1][...].astype(BF16))

    rows = lambda a: pl.BlockSpec((tk, a.shape[1]), lambda t: (t, 0))
    shapes = [(lhs.shape[1], rhs.shape[1]) for lhs, rhs in pairs]
    return pl.pallas_call(
        body, name=name, grid=(T // tk,),
        in_specs=[rows(a) for pair in pairs for a in pair],
        out_specs=[pl.BlockSpec(s, lambda t: (0, 0), pipeline_mode=pl.Buffered(1)) for s in shapes],
        out_shape=[jax.ShapeDtypeStruct(s, F32) for s in shapes],
        compiler_params=_params(("arbitrary",)),
    )(*[a for pair in pairs for a in pair])


def mix_in_bwd(dps, ws, x, nw, dres, name, side=None):
    T, D = x.shape
    tm = min(TM_MIX, T)
    n = len(dps)

    def body(*refs):
        dp_refs, w_refs = refs[:n], refs[n:2 * n]
        x_ref, nw_ref, dr_ref, dx_ref, dnw_ref = refs[2 * n:]

        @pl.when(pl.program_id(0) == 0)
        def _():
            dnw_ref[...] = jnp.zeros_like(dnw_ref)

        dh = _dot_nt(dp_refs[0][...], w_refs[0][...])
        for k in range(1, n):
            dh += _dot_nt(dp_refs[k][...], w_refs[k][...])
        dxn, dnw = _rms_bwd(x_ref[...], nw_ref[...], dh)
        dx_ref[...] = dr_ref[...] + dxn
        dnw_ref[...] += dnw

    in_specs = [pl.BlockSpec((tm, dp.shape[1]), lambda i: (i, 0)) for dp in dps]
    in_specs += [pl.BlockSpec(w.shape, lambda i: (0, 0), pipeline_mode=pl.Buffered(1)) for w in ws]
    in_specs += [pl.BlockSpec((tm, D), lambda i: (i, 0)), pl.BlockSpec((1, D), lambda i: (0, 0)),
                 pl.BlockSpec((tm, D), lambda i: (i, 0))]
    return _call_hosting(
        body, side, name=name, grid=(T // tm,), in_specs=in_specs,
        out_specs=[pl.BlockSpec((tm, D), lambda i: (i, 0)), pl.BlockSpec((1, D), lambda i: (0, 0))],
        out_shape=[jax.ShapeDtypeStruct((T, D), F32), jax.ShapeDtypeStruct((1, D), F32)],
        scratch_shapes=[], operands=(*dps, *ws, x, nw, dres))


def _layernorm_stats(c1):
    mu = jnp.mean(c1, axis=-1, keepdims=True)
    xc = c1 - mu
    rstd = lax.rsqrt(jnp.mean(xc * xc, axis=-1, keepdims=True) + LN_EPS)
    return xc * rstd, rstd


def _shifted_copies(src_ref, dst_ref):
    rows = dst_ref.shape[1]
    for b in range(1, 8):
        dst_ref[b - 1] = src_ref[pl.ds(b, rows), :]


def _shifted_rows(src_ref, shifted_ref, start, rows, cols):
    a8, b = divmod(start, 8)
    if b == 0:
        return src_ref[pl.ds(8 * a8, rows), cols]
    return shifted_ref[b - 1, pl.ds(8 * a8, rows), cols]


def conv_fwd(p_glu, dw_w, dw_b, ln_g, ln_b, name):
    T, D2 = p_glu.shape
    D = D2 // 2
    tm = min(TM_ROW, T)
    hb = tm // CONV_HALO

    def body(a_ref, b_ref, ah_ref, bh_ref, w_ref, wb_ref, g_ref, be_ref, c1_ref, c3_ref, e_ref, es_ref):
        i = pl.program_id(0)
        halo = ah_ref[...] * _sigmoid(bh_ref[...])
        e_ref[pl.ds(0, CONV_HALO), :] = jnp.where(i > 0, halo, 0.0)
        e_ref[pl.ds(CONV_HALO, tm), :] = a_ref[...] * _sigmoid(b_ref[...])
        _shifted_copies(e_ref, es_ref)
        off = CONV_HALO - (CONV_WIDTH - 1)

        def strip(s, carry):
            cols = pl.ds(pl.multiple_of(s * 128, 128), 128)
            acc = jnp.zeros((tm, 128), F32) + wb_ref[:, cols]
            for k in range(CONV_WIDTH):
                acc += w_ref[pl.ds(k, 1), cols] * _shifted_rows(e_ref, es_ref, off + k, tm, cols)
            c1_ref[:, cols] = acc
            return carry

        lax.fori_loop(0, D // 128, strip, 0)
        xhat, _ = _layernorm_stats(c1_ref[...])
        c2 = xhat * g_ref[...] + be_ref[...]
        c3_ref[...] = (c2 * _sigmoid(c2)).astype(BF16)

    row = pl.BlockSpec((1, D), lambda i: (0, 0))
    return pl.pallas_call(
        body, name=name, grid=(T // tm,),
        in_specs=[pl.BlockSpec((tm, D), lambda i: (i, 0)), pl.BlockSpec((tm, D), lambda i: (i, 1)),
                  pl.BlockSpec((CONV_HALO, D), lambda i: (jnp.maximum(i * hb - 1, 0), 0)),
                  pl.BlockSpec((CONV_HALO, D), lambda i: (jnp.maximum(i * hb - 1, 0), 1)),
                  pl.BlockSpec((CONV_HALO, D), lambda i: (0, 0)), row, row, row],
        out_specs=[pl.BlockSpec((tm, D), lambda i: (i, 0)), pl.BlockSpec((tm, D), lambda i: (i, 0))],
        out_shape=[jax.ShapeDtypeStruct((T, D), F32), jax.ShapeDtypeStruct((T, D), BF16)],
        scratch_shapes=[pltpu.VMEM((tm + CONV_HALO, D), F32), pltpu.VMEM((7, tm + CONV_HALO - 8, D), F32)],
        compiler_params=_params(("parallel",)),
    )(p_glu, p_glu, p_glu, p_glu, dw_w, dw_b, ln_g, ln_b)


def conv_bwd(p_glu, dc1, dw_w, name, side=None):
    T, D2 = p_glu.shape
    D = D2 // 2
    tm = min(TM_ROW, T)
    hb = tm // CONV_HALO
    last = T // CONV_HALO - 1
    nblk = T // tm

    def body(a_ref, b_ref, ah_ref, bh_ref, d_ref, dn_ref, w_ref, dp_ref, dw_ref, e_ref, f_ref, es_ref, fs_ref):
        i = pl.program_id(0)

        @pl.when(i == 0)
        def _():
            dw_ref[...] = jnp.zeros_like(dw_ref)

        halo = ah_ref[...] * _sigmoid(bh_ref[...])
        e_ref[pl.ds(0, CONV_HALO), :] = jnp.where(i > 0, halo, 0.0)
        e_ref[pl.ds(CONV_HALO, tm), :] = a_ref[...] * _sigmoid(b_ref[...])
        f_ref[pl.ds(0, tm), :] = d_ref[...]
        f_ref[pl.ds(tm, CONV_HALO), :] = jnp.where(i < nblk - 1, dn_ref[...], 0.0)
        _shifted_copies(e_ref, es_ref)
        _shifted_copies(f_ref, fs_ref)
        off = CONV_HALO - (CONV_WIDTH - 1)

        def strip(s, carry):
            cols = pl.ds(pl.multiple_of(s * 128, 128), 128)
            d = d_ref[:, cols]
            dc0 = jnp.zeros((tm, 128), F32)
            for k in range(CONV_WIDTH):
                dw_ref[pl.ds(k, 1), cols] += jnp.sum(d * _shifted_rows(e_ref, es_ref, off + k, tm, cols),
                                                     axis=0, keepdims=True)
                dc0 += w_ref[pl.ds(k, 1), cols] * _shifted_rows(f_ref, fs_ref, CONV_WIDTH - 1 - k, tm, cols)
            a = a_ref[:, cols]
            sb = _sigmoid(b_ref[:, cols])
            dp_ref[:, cols] = (dc0 * sb).astype(BF16)
            dp_ref[:, pl.ds(pl.multiple_of(D + s * 128, 128), 128)] = (dc0 * a * sb * (1.0 - sb)).astype(BF16)
            return carry

        lax.fori_loop(0, D // 128, strip, 0)

    return _call_hosting(
        body, side, name=name, grid=(nblk,),
        in_specs=[pl.BlockSpec((tm, D), lambda i: (i, 0)), pl.BlockSpec((tm, D), lambda i: (i, 1)),
                  pl.BlockSpec((CONV_HALO, D), lambda i: (jnp.maximum(i * hb - 1, 0), 0)),
                  pl.BlockSpec((CONV_HALO, D), lambda i: (jnp.maximum(i * hb - 1, 0), 1)),
                  pl.BlockSpec((tm, D), lambda i: (i, 0)),
                  pl.BlockSpec((CONV_HALO, D), lambda i: (jnp.minimum((i + 1) * hb, last), 0)),
                  pl.BlockSpec((CONV_HALO, D), lambda i: (0, 0))],
        out_specs=[pl.BlockSpec((tm, D2), lambda i: (i, 0)), pl.BlockSpec((CONV_HALO, D), lambda i: (0, 0))],
        out_shape=[jax.ShapeDtypeStruct((T, D2), BF16), jax.ShapeDtypeStruct((CONV_HALO, D), F32)],
        scratch_shapes=[pltpu.VMEM((tm + CONV_HALO, D), F32), pltpu.VMEM((tm + CONV_HALO, D), F32),
                        pltpu.VMEM((7, tm + CONV_HALO - 8, D), F32), pltpu.VMEM((7, tm + CONV_HALO - 8, D), F32)],
        operands=(p_glu, p_glu, p_glu, p_glu, dc1, dc1, dw_w))


def _rot_half(x):
    lane = lax.broadcasted_iota(jnp.int32, x.shape, 1)
    first = (lane % HEAD_DIM) < HEAD_DIM // 2
    return jnp.where(first, pltpu.roll(x, 128 - HEAD_DIM // 2, 1), pltpu.roll(x, HEAD_DIM // 2, 1))


def _rope_chunks(x, cs, sn, sign):
    outs = []
    for c in range(x.shape[1] // 128):
        xc = x[:, c * 128:(c + 1) * 128]
        outs.append(xc * cs + sign * (_rot_half(xc) * sn))
    return outs[0] if len(outs) == 1 else jnp.concatenate(outs, axis=1)


def rope_bwd(dq, dk, dv, cs, sn, name, side=None):
    T, D = dq.shape
    KV = dk.shape[1]
    tm = min(TM_ROW, T)

    def body(dq_ref, dk_ref, dv_ref, cs_ref, sn_ref, o_ref):
        cs_v, sn_v = cs_ref[...], sn_ref[...]
        o_ref[:, pl.ds(0, D)] = _rope_chunks(dq_ref[...], cs_v, sn_v, -1.0).astype(BF16)
        o_ref[:, pl.ds(D, KV)] = _rope_chunks(dk_ref[...], cs_v, sn_v, -1.0).astype(BF16)
        o_ref[:, pl.ds(D + KV, KV)] = dv_ref[...].astype(BF16)

    tab = pl.BlockSpec((tm, 128), lambda i: (i, 0))
    return _call_hosting(
        body, side, name=name, grid=(T // tm,),
        in_specs=[pl.BlockSpec((tm, D), lambda i: (i, 0)), pl.BlockSpec((tm, KV), lambda i: (i, 0)),
                  pl.BlockSpec((tm, KV), lambda i: (i, 0)), tab, tab],
        out_specs=[pl.BlockSpec((tm, D + 2 * KV), lambda i: (i, 0))],
        out_shape=[jax.ShapeDtypeStruct((T, D + 2 * KV), BF16)],
        scratch_shapes=[], operands=(dq, dk, dv, cs, sn))


def _lane_lo():
    return lax.broadcasted_iota(jnp.int32, (1, 128), 1) < HEAD_DIM


def _band_mask(i, reps):
    shape = (reps * WINDOW, 2 * WINDOW)
    qi = lax.broadcasted_iota(jnp.int32, shape, 0) % WINDOW
    cj = lax.broadcasted_iota(jnp.int32, shape, 1)
    rel = qi - cj + WINDOW
    return (rel >= 0) & (rel < WINDOW) & ((i > 0) | (cj >= WINDOW))


def _stack_pairs(ref, first, n):
    parts = [ref[:, pl.ds((first + p) * 128, 128)] for p in range(n)]
    return parts[0] if n == 1 else jnp.concatenate(parts, axis=0)


def _pair_rows(n):
    return lax.broadcasted_iota(jnp.int32, (n * WINDOW, 1), 0) // WINDOW


def _per_pair_column(values, n):
    rows = _pair_rows(n)
    col = jnp.zeros((n * WINDOW, 1), F32) + values[0]
    for p in range(1, n):
        col = jnp.where(rows == p, values[p], col)
    return col


def _kv_lo_hi(x2, g):
    pair, half = divmod(g, 2)
    lo = _lane_lo()
    xg = x2[:, pair * 128:(pair + 1) * 128].astype(F32)
    xg = jnp.where(lo if half == 0 else ~lo, xg, 0.0)
    sw = pltpu.roll(xg, HEAD_DIM, 1)
    x_lo, x_hi = (xg, sw) if half == 0 else (sw, xg)
    return x_lo.astype(BF16), x_hi.astype(BF16)


def _softmax_sink(s, allowed, sink):
    s = jnp.where(allowed, s * (HEAD_DIM ** -0.5), NEG_INF)
    m = jnp.maximum(jnp.max(s, axis=-1, keepdims=True), sink)
    p = jnp.exp(s - m)
    es = jnp.exp(sink - m)
    inv = 1.0 / (jnp.sum(p, axis=-1, keepdims=True) + es)
    return p * inv, es * inv


def attn_fwd(qr, kr, vb, sinks, name, side=None):
    T, D = qr.shape
    KV = kr.shape[1]
    n_kv = KV // HEAD_DIM
    group = (D // HEAD_DIM) // n_kv
    nb = T // WINDOW

    npair = group // 2

    def body(sink_ref, q_ref, kp_ref, kc_ref, vp_ref, vc_ref, o_ref):
        i = pl.program_id(0)
        allowed = _band_mask(i, npair)
        k2 = jnp.concatenate([kp_ref[...], kc_ref[...]], axis=0)
        v2 = jnp.concatenate([vp_ref[...], vc_ref[...]], axis=0)
        outs = [None] * (D // 128)
        for g in range(n_kv):
            k_lo, k_hi = _kv_lo_hi(k2, g)
            v_lo, v_hi = _kv_lo_hi(v2, g)
            first = (g * group) // 2
            q = _stack_pairs(q_ref, first, npair)
            sink_e = _per_pair_column([sink_ref[0, g * group + 2 * p] for p in range(npair)], npair)
            sink_o = _per_pair_column([sink_ref[0, g * group + 2 * p + 1] for p in range(npair)], npair)
            pe, _ = _softmax_sink(_dot_nt(q, k_lo), allowed, sink_e)
            po, _ = _softmax_sink(_dot_nt(q, k_hi), allowed, sink_o)
            o = _dot(pe.astype(BF16), v_lo) + _dot(po.astype(BF16), v_hi)
            for p in range(npair):
                outs[first + p] = o[p * WINDOW:(p + 1) * WINDOW]
        o_ref[...] = jnp.concatenate(outs, axis=1).astype(BF16)

    prev = lambda i: (jnp.maximum(i - 1, 0), 0)
    cur = lambda i: (i, 0)
    return _call_hosting(
        body, side, name=name, grid=(nb,),
        in_specs=[pl.BlockSpec(memory_space=pltpu.SMEM),
                  pl.BlockSpec((WINDOW, D), cur),
                  pl.BlockSpec((WINDOW, KV), prev), pl.BlockSpec((WINDOW, KV), cur),
                  pl.BlockSpec((WINDOW, KV), prev), pl.BlockSpec((WINDOW, KV), cur)],
        out_specs=[pl.BlockSpec((WINDOW, D), cur)],
        out_shape=[jax.ShapeDtypeStruct((T, D), BF16)],
        scratch_shapes=[], operands=(sinks, qr, kr, kr, vb, vb))


def attn_bwd(qr, kr, vb, o, do, sinks, name, side=None):
    T, D = qr.shape
    KV = kr.shape[1]
    n_heads = D // HEAD_DIM
    n_kv = KV // HEAD_DIM
    group = n_heads // n_kv
    nb = T // WINDOW
    npair = group // 2
    scale = HEAD_DIM ** -0.5

    def body(sink_ref, q_ref, kp_ref, kc_ref, vp_ref, vc_ref, o_ref, do_ref,
             dq_ref, dk_ref, dv_ref, ds_ref, ck_ref, cv_ref):
        i = pl.program_id(0)
        lo = _lane_lo()

        @pl.when(i == 0)
        def _():
            ck_ref[...] = jnp.zeros_like(ck_ref)
            cv_ref[...] = jnp.zeros_like(cv_ref)
            ds_ref[...] = jnp.zeros_like(ds_ref)

        @pl.when(i < nb)
        def _():
            allowed = _band_mask(i, npair)
            rows = _pair_rows(npair)
            k2 = jnp.concatenate([kp_ref[...], kc_ref[...]], axis=0)
            v2 = jnp.concatenate([vp_ref[...], vc_ref[...]], axis=0)
            lane = lax.broadcasted_iota(jnp.int32, (1, 128), 1)
            dsink = jnp.zeros((1, 128), F32)
            dq_out = [None] * (D // 128)
            dk_pairs = [jnp.zeros((2 * WINDOW, 128), F32) for _ in range(KV // 128)]
            dv_pairs = [jnp.zeros((2 * WINDOW, 128), F32) for _ in range(KV // 128)]
            for g in range(n_kv):
                k_lo, k_hi = _kv_lo_hi(k2, g)
                v_lo, v_hi = _kv_lo_hi(v2, g)
                first = (g * group) // 2
                q = _stack_pairs(q_ref, first, npair)
                dop = _stack_pairs(do_ref, first, npair)
                dd = dop.astype(F32) * _stack_pairs(o_ref, first, npair).astype(F32)
                dq = jnp.zeros((npair * WINDOW, 128), F32)
                dkg = jnp.zeros((2 * WINDOW, 128), F32)
                dvg = jnp.zeros((2 * WINDOW, 128), F32)
                for parity, k_h, v_h, sel in ((0, k_lo, v_lo, lo), (1, k_hi, v_hi, ~lo)):
                    heads = [g * group + 2 * p + parity for p in range(npair)]
                    sink = _per_pair_column([sink_ref[0, h] for h in heads], npair)
                    p_, ps = _softmax_sink(_dot_nt(q, k_h), allowed, sink)
                    delta = jnp.sum(jnp.where(sel, dd, 0.0), axis=-1, keepdims=True)
                    dsc = (p_ * (_dot_nt(dop, v_h) - delta)).astype(BF16)
                    sd = -ps * delta
                    for p, h in enumerate(heads):
                        dsink += jnp.where(lane == h, jnp.sum(jnp.where(rows == p, sd, 0.0)), 0.0)
                    dq += _dot(dsc, k_h)
                    dkg += jnp.where(sel, _dot_tn(dsc, q), 0.0)
                    dvg += jnp.where(sel, _dot_tn(p_.astype(BF16), dop), 0.0)
                for p in range(npair):
                    dq_out[first + p] = dq[p * WINDOW:(p + 1) * WINDOW]
                pair, half = divmod(g, 2)
                keep = lo if half == 0 else ~lo
                dk_pairs[pair] += jnp.where(keep, dkg + pltpu.roll(dkg, HEAD_DIM, 1), 0.0) * scale
                dv_pairs[pair] += jnp.where(keep, dvg + pltpu.roll(dvg, HEAD_DIM, 1), 0.0)
            dq_ref[...] = jnp.concatenate(dq_out, axis=1) * scale
            dk2 = dk_pairs[0] if len(dk_pairs) == 1 else jnp.concatenate(dk_pairs, axis=1)
            dv2 = dv_pairs[0] if len(dv_pairs) == 1 else jnp.concatenate(dv_pairs, axis=1)
            dk_ref[...] = ck_ref[...] + dk2[:WINDOW]
            dv_ref[...] = cv_ref[...] + dv2[:WINDOW]
            ck_ref[...] = dk2[WINDOW:]
            cv_ref[...] = dv2[WINDOW:]
            ds_ref[pl.ds(0, 1), :] += dsink

        @pl.when(i == nb)
        def _():
            dk_ref[...] = ck_ref[...]
            dv_ref[...] = cv_ref[...]

    prev = lambda i: (jnp.maximum(i - 1, 0), 0)
    cur = lambda i: (jnp.minimum(i, nb - 1), 0)
    prevc = lambda i: (jnp.maximum(jnp.minimum(i, nb - 1) - 1, 0), 0)
    return _call_hosting(
        body, side, name=name, grid=(nb + 1,),
        in_specs=[pl.BlockSpec(memory_space=pltpu.SMEM),
                  pl.BlockSpec((WINDOW, D), cur),
                  pl.BlockSpec((WINDOW, KV), prevc), pl.BlockSpec((WINDOW, KV), cur),
                  pl.BlockSpec((WINDOW, KV), prevc), pl.BlockSpec((WINDOW, KV), cur),
                  pl.BlockSpec((WINDOW, D), cur), pl.BlockSpec((WINDOW, D), cur)],
        out_specs=[pl.BlockSpec((WINDOW, D), cur), pl.BlockSpec((WINDOW, KV), prev),
                   pl.BlockSpec((WINDOW, KV), prev), pl.BlockSpec((8, 128), lambda i: (0, 0))],
        out_shape=[jax.ShapeDtypeStruct((T, D), F32), jax.ShapeDtypeStruct((T, KV), F32),
                   jax.ShapeDtypeStruct((T, KV), F32), jax.ShapeDtypeStruct((8, 128), F32)],
        scratch_shapes=[pltpu.VMEM((WINDOW, KV), F32), pltpu.VMEM((WINDOW, KV), F32)],
        operands=(sinks, qr, kr, kr, vb, vb, o, do))


def merge_fwd(x, c3, o, p_gate, gate_b, w_proj, w_o, w_out, name):
    T, D = x.shape
    tm = min(TM_MIX, T)

    def body(x_ref, c3_ref, o_ref, gc_ref, ga_ref, bc_ref, ba_ref, wp_ref, wo_ref, wout_ref,
             xo_ref, co_ref, ao_ref, mg_ref):
        conv_out = _dot(c3_ref[...], wp_ref[...])
        attn_out = _dot(o_ref[...], wo_ref[...])
        merged = (_sigmoid(gc_ref[...] + bc_ref[...]) * conv_out
                  + _sigmoid(ga_ref[...] + ba_ref[...]) * attn_out).astype(BF16)
        co_ref[...] = conv_out.astype(BF16)
        ao_ref[...] = attn_out.astype(BF16)
        mg_ref[...] = merged
        xo_ref[...] = x_ref[...] + _dot(merged, wout_ref[...])

    blk = lambda j: pl.BlockSpec((tm, D), lambda i: (i, j))
    row = lambda j: pl.BlockSpec((1, D), lambda i: (0, j))
    mat = pl.BlockSpec((D, D), lambda i: (0, 0), pipeline_mode=pl.Buffered(1))
    return pl.pallas_call(
        body, name=name, grid=(T // tm,),
        in_specs=[blk(0), blk(0), blk(0), blk(0), blk(1), row(0), row(1), mat, mat, mat],
        out_specs=[blk(0), blk(0), blk(0), blk(0)],
        out_shape=[jax.ShapeDtypeStruct((T, D), F32)] + [jax.ShapeDtypeStruct((T, D), BF16)] * 3,
        compiler_params=_params(("parallel",)),
    )(x, c3, o, p_gate, p_gate, gate_b, gate_b, w_proj, w_o, w_out)


def merge_bwd(dx, p_gate, gate_b, conv_out, attn_out, c1, ln_g, ln_b, w_proj, w_o, w_out, name, side=None):
    T, D = dx.shape
    tm = min(TM_ROW, T)

    def body(dx_ref, gc_ref, ga_ref, bc_ref, ba_ref, co_ref, ao_ref, c1_ref, g_ref, be_ref,
             wp_ref, wo_ref, wout_ref, dgt_ref, dco_ref, dao_ref, do_ref, dc1_ref, sm_ref):
        @pl.when(pl.program_id(0) == 0)
        def _():
            sm_ref[...] = jnp.zeros_like(sm_ref)

        dm = _dot_nt(dx_ref[...].astype(BF16), wout_ref[...])
        sc = _sigmoid(gc_ref[...] + bc_ref[...])
        sa = _sigmoid(ga_ref[...] + ba_ref[...])
        dco = (dm * sc).astype(BF16)
        dao = (dm * sa).astype(BF16)
        dgc = dm * co_ref[...].astype(F32) * sc * (1.0 - sc)
        dga = dm * ao_ref[...].astype(F32) * sa * (1.0 - sa)
        dgt_ref[:, pl.ds(0, D)] = dgc.astype(BF16)
        dgt_ref[:, pl.ds(D, D)] = dga.astype(BF16)
        dco_ref[...] = dco
        dao_ref[...] = dao
        do_ref[...] = _dot_nt(dao, wo_ref[...]).astype(BF16)
        dc3 = _dot_nt(dco, wp_ref[...])
        xhat, rstd = _layernorm_stats(c1_ref[...])
        c2 = xhat * g_ref[...] + be_ref[...]
        dc2 = dc3 * _silu_grad(c2, _sigmoid(c2))
        dxh = dc2 * g_ref[...]
        dc1 = rstd * (dxh - jnp.mean(dxh, axis=-1, keepdims=True)
                      - xhat * jnp.mean(dxh * xhat, axis=-1, keepdims=True))
        dc1_ref[...] = dc1
        colsum = lambda v: jnp.sum(v, axis=0, keepdims=True)
        for r, (left, right) in enumerate(((dgc, dga), (dc2 * xhat, dc2), (dc1, None))):
            sm_ref[pl.ds(r, 1), pl.ds(0, D)] += colsum(left)
            if right is not None:
                sm_ref[pl.ds(r, 1), pl.ds(D, D)] += colsum(right)

    blk = lambda j: pl.BlockSpec((tm, D), lambda i: (i, j))
    row = lambda j: pl.BlockSpec((1, D), lambda i: (0, j))
    mat = pl.BlockSpec((D, D), lambda i: (0, 0))
    return _call_hosting(
        body, side, name=name, grid=(T // tm,),
        in_specs=[blk(0), blk(0), blk(1), row(0), row(1), blk(0), blk(0), blk(0), row(0), row(0), mat, mat, mat],
        out_specs=[pl.BlockSpec((tm, 2 * D), lambda i: (i, 0)), blk(0), blk(0), blk(0), blk(0),
                   pl.BlockSpec((8, 2 * D), lambda i: (0, 0))],
        out_shape=[jax.ShapeDtypeStruct((T, 2 * D), BF16)] + [jax.ShapeDtypeStruct((T, D), BF16)] * 3
                  + [jax.ShapeDtypeStruct((T, D), F32), jax.ShapeDtypeStruct((8, 2 * D), F32)],
        scratch_shapes=[],
        operands=(dx, p_gate, p_gate, gate_b, gate_b, conv_out, attn_out, c1, ln_g, ln_b, w_proj, w_o, w_out))


def loss_head(x, nw, target, name):
    T, D = x.shape
    tm = min(TM_ROW, T)

    def body(x_ref, nw_ref, t_ref, dx_ref, sm_ref):
        @pl.when(pl.program_id(0) == 0)
        def _():
            sm_ref[...] = jnp.zeros_like(sm_ref)

        xv = x_ref[...]
        err = xv * _rms_scale(xv) * nw_ref[...] - t_ref[...]
        loss = 0.5 * jnp.sum(jnp.mean(err * err, axis=-1, keepdims=True))
        dxn, dnw = _rms_bwd(xv, nw_ref[...], err * (1.0 / D))
        dx_ref[...] = dxn
        sm_ref[pl.ds(0, 1), :] += dnw
        sm_ref[pl.ds(1, 1), :] += jnp.zeros((1, D), F32) + loss

    return pl.pallas_call(
        body, name=name, grid=(T // tm,),
        in_specs=[pl.BlockSpec((tm, D), lambda i: (i, 0)), pl.BlockSpec((1, D), lambda i: (0, 0)),
                  pl.BlockSpec((tm, D), lambda i: (i, 0))],
        out_specs=[pl.BlockSpec((tm, D), lambda i: (i, 0)), pl.BlockSpec((8, D), lambda i: (0, 0))],
        out_shape=[jax.ShapeDtypeStruct((T, D), F32), jax.ShapeDtypeStruct((8, D), F32)],
        compiler_params=_params(("arbitrary",)),
    )(x, nw, target)


def _by_shape(arrays):
    groups = {}
    for k, a in enumerate(arrays):
        groups.setdefault(a.shape, []).append(k)
    return list(groups.values())


def adamw(ws, gs, ms, vs, name):
    n = len(ws)
    R, C = ws[0].shape
    tr = _row_tile(R, TR_ELT)

    def body(*refs):
        for a in range(n):
            w_ref, g_ref, m_ref, v_ref, d_ref, mo_ref, vo_ref = (refs[k * n + a] for k in range(7))
            gv = g_ref[...]
            mn = ADAM_B1 * m_ref[...] + (1.0 - ADAM_B1) * gv
            vn = ADAM_B2 * v_ref[...] + (1.0 - ADAM_B2) * (gv * gv)
            m_hat = mn / (1.0 - ADAM_B1 ** ADAM_STEP)
            v_hat = vn / (1.0 - ADAM_B2 ** ADAM_STEP)
            d_ref[...] = -ADAM_LR * (m_hat / (jnp.sqrt(v_hat) + ADAM_EPS) + ADAM_WD * w_ref[...])
            mo_ref[...] = mn
            vo_ref[...] = vn

    spec = pl.BlockSpec((tr, C), lambda i: (i, 0))
    outs = pl.pallas_call(
        body, name=name, grid=(R // tr,), in_specs=[spec] * (4 * n), out_specs=[spec] * (3 * n),
        out_shape=[jax.ShapeDtypeStruct((R, C), F32)] * (3 * n),
        compiler_params=_params(("parallel",)),
    )(*ws, *gs, *ms, *vs)
    return [(outs[a], outs[n + a], outs[2 * n + a]) for a in range(n)]


def _place():
    return lax.axis_index("x"), lax.axis_index("y"), lax.axis_index("c")


def place_shards(place, ws, dtype, name):
    n = len(ws)
    R, C = ws[0].shape
    tr = _row_tile(R, TR_ELT)

    def body(pc_ref, *refs):
        for a in range(n):
            refs[n + a][...] = refs[a][...].astype(dtype)

    return pl.pallas_call(
        body, name=name,
        grid_spec=pltpu.PrefetchScalarGridSpec(
            num_scalar_prefetch=1, grid=(R // tr,),
            in_specs=[pl.BlockSpec((tr, C), lambda r, pc: (r, 0))] * n,
            out_specs=[pl.BlockSpec((None, tr, C), lambda r, pc: (pc[0], r, 0))] * n),
        out_shape=[jax.ShapeDtypeStruct((N_CHIPS, R, C), dtype)] * n,
        compiler_params=_params(("arbitrary",)),
    )(place, *ws)


def gather_side(shards, small):
    n, ns = len(shards), len(small)

    def ici_copy(dst, sems, k, j, x, y, c, sending):
        px, py = x ^ (j >> 1), y ^ (j & 1)
        slot = 2 * x + y if sending else 2 * px + py
        half = dst[k].shape[1] // 2
        part = dst[k].at[slot, pl.ds(c * half, half)] if k < n else dst[k].at[slot]
        return pltpu.make_async_remote_copy(part, part, sems[0].at[3 * k + j - 1], sems[1].at[3 * k + j - 1],
                                            device_id=(px, py, c), device_id_type=MESH)

    def d2d_copy(dst, sems, k, j, x, y, c, sending):
        half = dst[k].shape[1] // 2
        part = dst[k].at[2 * (x ^ (j >> 1)) + (y ^ (j & 1)), pl.ds((c if sending else 1 - c) * half, half)]
        return pltpu.make_async_remote_copy(part, part, sems[2].at[3 * k + j - 1], sems[3].at[3 * k + j - 1],
                                            device_id=(x, y, 1 - c), device_id_type=MESH)

    def start(src, dst, sems):
        x, y, c = _place()
        for k in range(n + ns):
            for j in (1, 2, 3):
                ici_copy(dst, sems, k, j, x, y, c, True).start()

    def relay(src, dst, sems):
        x, y, c = _place()
        for k in range(n + ns):
            for j in (1, 2, 3):
                ici_copy(dst, sems, k, j, x, y, c, False).wait_recv()
                if k < n:
                    d2d_copy(dst, sems, k, j, x, y, c, True).start()

    def finish(src, dst, sems):
        x, y, c = _place()
        for k in range(n):
            for j in (1, 2, 3):
                d2d_copy(dst, sems, k, j, x, y, c, False).wait_recv()
        for k in range(n + ns):
            for j in (1, 2, 3):
                ici_copy(dst, sems, k, j, x, y, c, True).wait_send()
                if k < n:
                    d2d_copy(dst, sems, k, j, x, y, c, True).wait_send()

    arrays = list(shards) + list(small)
    return dict(inputs=arrays, out_shapes=[jax.ShapeDtypeStruct(a.shape, a.dtype) for a in arrays],
                aliases={k: k for k in range(n + ns)},
                sems=[pltpu.SemaphoreType.DMA((3 * (n + ns),)), pltpu.SemaphoreType.DMA((3 * (n + ns),)),
                      pltpu.SemaphoreType.DMA((3 * n,)), pltpu.SemaphoreType.DMA((3 * n,))],
                start=start, relay=relay, finish=finish)


def run_side(side, name):
    n_in, n_out = len(side["inputs"]), len(side["out_shapes"])

    def body(*refs):
        src, dst, sems = refs[:n_in], refs[n_in:n_in + n_out], refs[n_in + n_out:]
        side["start"](src, dst, sems)
        if "relay" in side:
            side["relay"](src, dst, sems)
        side["finish"](src, dst, sems)

    return pl.pallas_call(
        body, name=name, in_specs=[HBM_SPEC] * n_in, out_specs=[HBM_SPEC] * n_out,
        out_shape=side["out_shapes"], input_output_aliases=side["aliases"], scratch_shapes=side["sems"],
    )(*side["inputs"])


def allreduce_small(block):
    R, C = block.shape

    def body(x_ref, out_ref, all_ref, send_sems, recv_sems, local_sem):
        x, y, c = _place()
        me, sibling = (x, y, c), (x, y, 1 - c)
        chips = [(1 - x, y), (x, 1 - y), (1 - x, 1 - y)]

        def slot(px, py, pc):
            return all_ref.at[4 * px + 2 * py + pc]

        def copy(k, block_of, to, src=None):
            return pltpu.make_async_remote_copy(
                src_ref=slot(*block_of) if src is None else src, dst_ref=slot(*block_of),
                send_sem=send_sems.at[k], recv_sem=recv_sems.at[k], device_id=to, device_id_type=MESH)

        mine = pltpu.make_async_copy(x_ref, slot(*me), local_sem)
        mine.start()
        first = [copy(0, me, sibling, src=x_ref)]
        first += [copy(1 + j, me, (*chip, c), src=x_ref) for j, chip in enumerate(chips)]
        for cp in first:
            cp.start()
        passed = [copy(4 + j, (*chip, c), sibling) for j, chip in enumerate(chips)]
        for j, chip in enumerate(chips):
            copy(1 + j, (*chip, c), me).wait_recv()
            passed[j].start()
        copy(0, sibling, me).wait_recv()
        for j, chip in enumerate(chips):
            copy(4 + j, (*chip, 1 - c), me).wait_recv()
        for cp in first + passed:
            cp.wait_send()
        mine.wait()
        total = all_ref[0]
        for d in range(1, N_DEV):
            total = total + all_ref[d]
        out_ref[...] = total

    return pl.pallas_call(
        body, name="allreduce_small",
        in_specs=[pl.BlockSpec(memory_space=pltpu.VMEM)], out_specs=pl.BlockSpec(memory_space=pltpu.VMEM),
        out_shape=jax.ShapeDtypeStruct((R, C), F32),
        scratch_shapes=[pltpu.VMEM((N_DEV, R, C), F32), pltpu.SemaphoreType.DMA((7,)),
                        pltpu.SemaphoreType.DMA((7,)), pltpu.SemaphoreType.DMA],
        compiler_params=pltpu.CompilerParams(vmem_limit_bytes=VMEM_LIMIT),
    )(block)


def exchange_siblings_side(grads):
    n = len(grads)

    def copies(src, dst, sems):
        x, y, c = _place()
        for k in range(n):
            half = src[k].shape[1] // 2
            yield pltpu.make_async_remote_copy(src[k].at[:, pl.ds((1 - c) * half, half)], dst[k],
                                               sems[0].at[k], sems[1].at[k],
                                               device_id=(x, y, 1 - c), device_id_type=MESH)

    def start(src, dst, sems):
        for cp in copies(src, dst, sems):
            cp.start()

    def finish(src, dst, sems):
        for cp in copies(src, dst, sems):
            cp.wait()

    return dict(inputs=list(grads), aliases={},
                out_shapes=[jax.ShapeDtypeStruct((N_CHIPS, g.shape[1] // 2, g.shape[2]), F32) for g in grads],
                sems=[pltpu.SemaphoreType.DMA((n,)), pltpu.SemaphoreType.DMA((n,))], start=start, finish=finish)


def rs_chip_sum(place, grads, sibs, name):
    n = len(grads)
    NP, R, C = grads[0].shape
    half = R // 2
    tr = _row_tile(half, TR_ELT)
    nr = half // tr

    def body(pc_ref, *refs):
        q = pl.program_id(1)
        for a in range(n):
            g_ref, s_ref, wire_ref, own_ref = (refs[k * n + a] for k in range(4))
            total = g_ref[...] + s_ref[...]
            wire_ref[...] = total.astype(BF16)

            @pl.when(q == pc_ref[0])
            def _():
                own_ref[...] = total

    outs = pl.pallas_call(
        body, name=name,
        grid_spec=pltpu.PrefetchScalarGridSpec(
            num_scalar_prefetch=1, grid=(nr, NP),
            in_specs=[pl.BlockSpec((None, tr, C), lambda r, q, pc: (q, pc[1] * nr + r, 0))] * n
                     + [pl.BlockSpec((None, tr, C), lambda r, q, pc: (q, r, 0))] * n,
            out_specs=[pl.BlockSpec((None, tr, C), lambda r, q, pc: (q, r, 0))] * n
                      + [pl.BlockSpec((tr, C), lambda r, q, pc: (r, 0))] * n),
        out_shape=[jax.ShapeDtypeStruct((NP, half, C), BF16)] * n + [jax.ShapeDtypeStruct((half, C), F32)] * n,
        compiler_params=_params(("arbitrary", "arbitrary")),
    )(place, *grads, *sibs)
    return outs[:n], outs[n:]


def exchange_chips_side(wires):
    n = len(wires)

    def copies(src, dst, sems):
        x, y, c = _place()
        for k in range(n):
            for j in (1, 2, 3):
                qx, qy = x ^ (j >> 1), y ^ (j & 1)
                yield pltpu.make_async_remote_copy(src[k].at[2 * qx + qy], dst[k].at[2 * x + y],
                                                   sems[0].at[3 * k + j - 1], sems[1].at[3 * k + j - 1],
                                                   device_id=(qx, qy, c), device_id_type=MESH)

    def start(src, dst, sems):
        for cp in copies(src, dst, sems):
            cp.start()

    def finish(src, dst, sems):
        for cp in copies(src, dst, sems):
            cp.wait()

    return dict(inputs=list(wires), out_shapes=[jax.ShapeDtypeStruct(w.shape, BF16) for w in wires], aliases={},
                sems=[pltpu.SemaphoreType.DMA((3 * n,)), pltpu.SemaphoreType.DMA((3 * n,))],
                start=start, finish=finish)


SEM_SPEC = pl.BlockSpec(memory_space=pltpu.SEMAPHORE)


def exchange_chips_start(wires, name):
    n = len(wires)
    side = exchange_chips_side(wires)

    def body(*refs):
        src, land, sems = refs[:n], refs[n:2 * n], refs[2 * n:2 * n + 2]
        side["start"](src, land, sems)
        refs[-1][...] = jnp.zeros_like(refs[-1])

    hbm = [pltpu.HBM(w.shape, w.dtype) for w in wires]
    outs = pl.pallas_call(
        body, name=name, in_specs=[HBM_SPEC] * (2 * n),
        out_specs=[SEM_SPEC, SEM_SPEC] + [HBM_SPEC] * (2 * n) + [pl.BlockSpec(memory_space=pltpu.VMEM)],
        out_shape=list(side["sems"]) + hbm + hbm + [jax.ShapeDtypeStruct((8, 128), F32)],
        input_output_aliases={k: 2 + k for k in range(2 * n)},
        compiler_params=pltpu.CompilerParams(has_side_effects=pltpu.SideEffectType.DATAFLOW_SIDE_EFFECTING),
    )(*[pltpu.with_memory_space_constraint(w, pltpu.HBM) for w in wires],
      *[pltpu.with_memory_space_constraint(lax.empty(w.shape, w.dtype), pltpu.HBM) for w in wires])
    return outs[0], outs[1], outs[2:2 + n], outs[2 + n:2 + 2 * n], outs[-1]


def exchange_chips_wait(send_sems, recv_sems, wires, lands, after, name):
    n = len(wires)
    side = exchange_chips_side(wires)

    def body(*refs):
        side["finish"](refs[:n], refs[n:2 * n], refs[2 * n:2 * n + 2])

    hbm = [pltpu.HBM(w.shape, w.dtype) for w in wires]
    outs = pl.pallas_call(
        body, name=name, in_specs=[HBM_SPEC] * (2 * n) + [SEM_SPEC, SEM_SPEC] + [HBM_SPEC] * len(after),
        out_specs=[HBM_SPEC] * (2 * n), out_shape=hbm + hbm,
        input_output_aliases={k: k for k in range(2 * n)},
        compiler_params=pltpu.CompilerParams(has_side_effects=pltpu.SideEffectType.DATAFLOW_SIDE_EFFECTING),
    )(*wires, *lands, send_sems, recv_sems, *after)
    return outs[n:]


def rs_final_sum(place, owns, gots, after, name):
    n = len(owns)
    NP, half, C = gots[0].shape
    tr = _row_tile(half, TR_ELT)
    nr = half // tr

    def body(pc_ref, *refs):
        for a in range(n):
            own_ref, g1_ref, g2_ref, g3_ref = (refs[k * n + a] for k in range(4))
            refs[4 * n + 1 + a][...] = (((own_ref[...] + g1_ref[...].astype(F32)) + g2_ref[...].astype(F32))
                                        + g3_ref[...].astype(F32))

    slot = lambda j: pl.BlockSpec((None, tr, C), lambda r, pc: (pc[0] ^ j, r, 0))
    return pl.pallas_call(
        body, name=name,
        grid_spec=pltpu.PrefetchScalarGridSpec(
            num_scalar_prefetch=1, grid=(nr,),
            in_specs=[pl.BlockSpec((tr, C), lambda r, pc: (r, 0))] * n + [slot(1)] * n + [slot(2)] * n + [slot(3)] * n
                     + [pl.BlockSpec((8, 128), lambda r, pc: (0, 0))],
            out_specs=[pl.BlockSpec((tr, C), lambda r, pc: (pc[1] * nr + r, 0))] * n),
        out_shape=[jax.ShapeDtypeStruct((2 * half, C), F32)] * n,
        compiler_params=_params(("arbitrary",)),
    )(place, *owns, *gots, *gots, *gots, after)


def rs_share_siblings(totals, name):
    n = len(totals)

    def body(*refs):
        dst = refs[n:2 * n]
        send_sems, recv_sems = refs[2 * n:]
        x, y, c = _place()
        copies = []
        for k in range(n):
            half = dst[k].shape[0] // 2
            rows = dst[k].at[pl.ds(c * half, half)]
            cp = pltpu.make_async_remote_copy(rows, rows, send_sems.at[k], recv_sems.at[k],
                                              device_id=(x, y, 1 - c), device_id_type=MESH)
            cp.start()
            copies.append(cp)
        for k, cp in enumerate(copies):
            cp.wait_send()
            half = dst[k].shape[0] // 2
            got = dst[k].at[pl.ds((1 - c) * half, half)]
            pltpu.make_async_remote_copy(got, got, send_sems.at[k], recv_sems.at[k],
                                         device_id=(x, y, c), device_id_type=MESH).wait_recv()

    return pl.pallas_call(
        body, name=name,
        in_specs=[HBM_SPEC] * n, out_specs=[HBM_SPEC] * n,
        out_shape=[jax.ShapeDtypeStruct(t.shape, F32) for t in totals],
        input_output_aliases={k: k for k in range(n)},
        scratch_shapes=[pltpu.SemaphoreType.DMA((n,)), pltpu.SemaphoreType.DMA((n,))],
    )(*totals)


def rs_to_wires(place, grads, tag, sibs=None):
    if sibs is None:
        sibs = run_side(exchange_siblings_side(grads), f"rs_exchange_siblings_{tag}")
    wires, owns = [None] * len(grads), [None] * len(grads)
    for ks in _by_shape(grads):
        ws, os_ = rs_chip_sum(place, [grads[k] for k in ks], [sibs[k] for k in ks], f"rs_chip_sum_{tag}{ks[0]}")
        for k, w, o in zip(ks, ws, os_):
            wires[k], owns[k] = w, o
    return wires, owns


def rs_finish(place, owns, gots, after, tag):
    totals = [None] * len(owns)
    for ks in _by_shape(owns):
        sums = rs_final_sum(place, [owns[k] for k in ks], [gots[k] for k in ks], after, f"rs_final_sum_{tag}{ks[0]}")
        for k, t in zip(ks, sums):
            totals[k] = t
    return rs_share_siblings(totals, f"rs_share_siblings_{tag}")


def _rope_tables(positions):
    half = HEAD_DIM // 2
    inv_freq = ROPE_THETA ** (-jnp.arange(half, dtype=F32) / half)
    ang = positions.astype(F32)[:, None] * inv_freq
    lanes = jnp.arange(128)
    spread = (lanes[None, :] % half == jnp.arange(half)[:, None]).astype(F32)
    signed = spread * jnp.where(lanes % HEAD_DIM < half, -1.0, 1.0).astype(F32)
    exact = lax.Precision.HIGHEST
    return jnp.dot(jnp.cos(ang), spread, precision=exact), jnp.dot(jnp.sin(ang), signed, precision=exact)


def _cols_from_pieces(pieces, start, stop):
    C = pieces.shape[2]
    parts = []
    for q in range(N_CHIPS):
        lo, hi = max(start, q * C), min(stop, (q + 1) * C)
        if lo < hi:
            parts.append(pieces[q][:, lo - q * C:hi - q * C])
    return parts[0] if len(parts) == 1 else jnp.concatenate(parts, axis=1)


def _pieces_from_groups(groups):
    C = sum(g.shape[1] for g in groups) // N_CHIPS
    pieces = []
    for q in range(N_CHIPS):
        parts, off = [], 0
        for g in groups:
            lo, hi = max(q * C, off), min((q + 1) * C, off + g.shape[1])
            if lo < hi:
                parts.append(g[:, lo - off:hi - off])
            off += g.shape[1]
        pieces.append(parts[0] if len(parts) == 1 else jnp.concatenate(parts, axis=1))
    return jnp.stack(pieces)


def kernel(x, positions, ffn1_norm, ffn1_w_gate, ffn1_w_up, ffn1_w_down, mix_norm, w_in, conv_dw_w, conv_dw_b, conv_ln_g, conv_ln_b, conv_w_proj, attn_sinks, attn_w_o, gate_b, w_out, ffn2_norm, ffn2_w_gate, ffn2_w_up, ffn2_w_down, final_norm, loss_target, m_ffn1_norm, m_ffn1_w_gate, m_ffn1_w_up, m_ffn1_w_down, m_mix_norm, m_w_in, m_conv_dw_w, m_conv_dw_b, m_conv_ln_g, m_conv_ln_b, m_conv_w_proj, m_attn_sinks, m_attn_w_o, m_gate_b, m_w_out, m_ffn2_norm, m_ffn2_w_gate, m_ffn2_w_up, m_ffn2_w_down, m_final_norm, v_ffn1_norm, v_ffn1_w_gate, v_ffn1_w_up, v_ffn1_w_down, v_mix_norm, v_w_in, v_conv_dw_w, v_conv_dw_b, v_conv_ln_g, v_conv_ln_b, v_conv_w_proj, v_attn_sinks, v_attn_w_o, v_gate_b, v_w_out, v_ffn2_norm, v_ffn2_w_gate, v_ffn2_w_up, v_ffn2_w_down, v_final_norm):
    weights = dict(ffn1_norm=ffn1_norm, ffn1_w_gate=ffn1_w_gate, ffn1_w_up=ffn1_w_up, ffn1_w_down=ffn1_w_down,
                   mix_norm=mix_norm, w_in=w_in, conv_dw_w=conv_dw_w, conv_dw_b=conv_dw_b, conv_ln_g=conv_ln_g,
                   conv_ln_b=conv_ln_b, conv_w_proj=conv_w_proj, attn_sinks=attn_sinks, attn_w_o=attn_w_o,
                   gate_b=gate_b, w_out=w_out, ffn2_norm=ffn2_norm, ffn2_w_gate=ffn2_w_gate, ffn2_w_up=ffn2_w_up,
                   ffn2_w_down=ffn2_w_down, final_norm=final_norm)
    m_in = dict(ffn1_norm=m_ffn1_norm, ffn1_w_gate=m_ffn1_w_gate, ffn1_w_up=m_ffn1_w_up, ffn1_w_down=m_ffn1_w_down,
                mix_norm=m_mix_norm, w_in=m_w_in, conv_dw_w=m_conv_dw_w, conv_dw_b=m_conv_dw_b,
                conv_ln_g=m_conv_ln_g, conv_ln_b=m_conv_ln_b, conv_w_proj=m_conv_w_proj, attn_sinks=m_attn_sinks,
                attn_w_o=m_attn_w_o, gate_b=m_gate_b, w_out=m_w_out, ffn2_norm=m_ffn2_norm,
                ffn2_w_gate=m_ffn2_w_gate, ffn2_w_up=m_ffn2_w_up, ffn2_w_down=m_ffn2_w_down, final_norm=m_final_norm)
    v_in = dict(ffn1_norm=v_ffn1_norm, ffn1_w_gate=v_ffn1_w_gate, ffn1_w_up=v_ffn1_w_up, ffn1_w_down=v_ffn1_w_down,
                mix_norm=v_mix_norm, w_in=v_w_in, conv_dw_w=v_conv_dw_w, conv_dw_b=v_conv_dw_b,
                conv_ln_g=v_conv_ln_g, conv_ln_b=v_conv_ln_b, conv_w_proj=v_conv_w_proj, attn_sinks=v_attn_sinks,
                attn_w_o=v_attn_w_o, gate_b=v_gate_b, w_out=v_w_out, ffn2_norm=v_ffn2_norm,
                ffn2_w_gate=v_ffn2_w_gate, ffn2_w_up=v_ffn2_w_up, ffn2_w_down=v_ffn2_w_down, final_norm=v_final_norm)
    names = list(weights)
    big = ["ffn1_w_gate", "ffn1_w_up", "ffn1_w_down", "w_in", "conv_w_proj", "attn_w_o", "w_out",
           "ffn2_w_gate", "ffn2_w_up", "ffn2_w_down"]
    transposed = [k for k in big if k.endswith(("w_gate", "w_up"))]
    for k in transposed:
        weights[k], m_in[k], v_in[k] = (jnp.swapaxes(a, 1, 2) for a in (weights[k], m_in[k], v_in[k]))

    xs = x[0]
    T, D = xs.shape
    KV = (w_in.shape[2] * N_CHIPS - 5 * D) // 2
    n_heads = D // HEAD_DIM
    my_chip = 2 * lax.axis_index("x") + lax.axis_index("y")
    place = jnp.stack([my_chip, lax.axis_index("c")]).astype(jnp.int32)

    first, mixer_w, second = big[:3], big[3:7], big[7:]
    placed = {}
    for group in (first, mixer_w, second):
        for ks in _by_shape([weights[k][0] for k in group]):
            same = [group[k] for k in ks]
            placed.update(zip(same, place_shards(place, [weights[k][0] for k in same], BF16, f"place_{same[0]}")))
    placed_dw, = place_shards(place, [conv_dw_w[0]], F32, "place_conv_dw_w")
    wg1, wu1, wd1 = run_side(gather_side([placed[k] for k in first], []), "gather_ffn1")
    x1, h1, g1, u1, *gathered = ffn_fwd(x[0], ffn1_norm, wg1, wu1, wd1, "ffn1_fwd",
                                        side=gather_side([placed[k] for k in mixer_w], [placed_dw]))
    full = dict(zip(mixer_w + ["conv_dw_w"], gathered))
    w_glu = _cols_from_pieces(full["w_in"], 0, 2 * D)
    w_qkv = _cols_from_pieces(full["w_in"], 2 * D, 3 * D + 2 * KV)
    w_gate = _cols_from_pieces(full["w_in"], 3 * D + 2 * KV, 5 * D + 2 * KV)
    w_proj = full["conv_w_proj"].reshape(D, D)
    w_o = full["attn_w_o"].reshape(D, D)
    w_out_f = full["w_out"].reshape(D, D)
    dw_w = full["conv_dw_w"].transpose(1, 0, 2).reshape(CONV_WIDTH, D)
    dw_w = jnp.concatenate([dw_w, jnp.zeros((CONV_HALO - CONV_WIDTH, D), F32)], axis=0)
    cs, sn = _rope_tables(positions[0])
    fn_row = final_norm.reshape(1, D)

    h2, p_glu, p_gate, qr, kr, vb = mix_in_fwd(x1, mix_norm, w_glu, w_qkv, w_gate, cs, sn, "mix_in_fwd")
    c1, c3 = conv_fwd(p_glu, dw_w, conv_dw_b, conv_ln_g, conv_ln_b, "conv_fwd")
    o, wg2, wu2, wd2 = attn_fwd(qr, kr, vb, attn_sinks, "attn_fwd",
                                side=gather_side([placed[k] for k in second], []))
    x2, conv_out, attn_out, merged = merge_fwd(x1, c3, o, p_gate, gate_b, w_proj, w_o, w_out_f, "merge_fwd")
    x3, h3, g2, u2 = ffn_fwd(x2, ffn2_norm, wg2, wu2, wd2, "ffn2_fwd")

    dx3, head_sums = loss_head(x3, fn_row, loss_target[0], "loss_head")
    dx2, dwg2, dwu2, dwd2, d_ffn2_norm = ffn_bwd(x2, ffn2_norm, h3, g2, u2, wg2, wu2, wd2, dx3, "ffn2_bwd")
    ffn2_grads = [dwg2, dwu2, dwd2]
    d_gates, d_conv_out, d_attn_out, d_o, dc1, merge_sums, *sibs_f2 = merge_bwd(
        dx2, p_gate, gate_b, conv_out, attn_out, c1, conv_ln_g, conv_ln_b, w_proj, w_o, w_out_f, "merge_bwd",
        side=exchange_siblings_side(ffn2_grads))
    d_w_out, d_w_proj, d_w_o = matmul_tn([(merged, dx2), (c3, d_conv_out), (o, d_attn_out)], TK_TN, "d_w_square")
    wires_f2, owns_f2 = rs_to_wires(place, ffn2_grads, "ffn2", sibs=sibs_f2)
    d_glu, d_dw_w, *gots_f2 = conv_bwd(p_glu, dc1, dw_w, "conv_bwd", side=exchange_chips_side(wires_f2))
    dwc = D // N_CHIPS
    square_grads = [d_w_proj.reshape(N_CHIPS, dwc, D), d_w_o.reshape(N_CHIPS, dwc, D),
                    d_w_out.reshape(N_CHIPS, dwc, D)]
    dq, dk, dv, d_sinks, *sibs_sq = attn_bwd(qr, kr, vb, o, d_o, attn_sinks, "attn_bwd",
                                             side=exchange_siblings_side(square_grads))
    wires_sq, owns_sq = rs_to_wires(place, square_grads, "square", sibs=sibs_sq)
    d_qkv, *gots_sq = rope_bwd(dq, dk, dv, cs, sn, "rope_bwd", side=exchange_chips_side(wires_sq))
    d_w_in = _pieces_from_groups(matmul_tn([(h2, d_glu), (h2, d_qkv), (h2, d_gates)], TK_TN // 2, "d_w_in"))
    dx1, d_mix_norm, *sib_w_in = mix_in_bwd([d_glu, d_qkv, d_gates], [w_glu, w_qkv, w_gate], x1, mix_norm, dx2,
                                            "mix_in_bwd", side=exchange_siblings_side([d_w_in]))
    wires_in, owns_in = rs_to_wires(place, [d_w_in], "w_in", sibs=sib_w_in)
    dx0, dwg1, dwu1, dwd1, d_ffn1_norm, *gots_in = ffn_bwd(xs, ffn1_norm, h1, g1, u1, wg1, wu1, wd1, dx1, "ffn1_bwd",
                                                           side=exchange_chips_side(wires_in))
    owns_m, gots_m = owns_in + owns_sq, list(gots_in) + list(gots_sq)
    wires_l, owns_l = rs_to_wires(place, [dwg1, dwu1, dwd1], "ffn1")
    send_sems, recv_sems, wires_l, lands_l, token = exchange_chips_start(wires_l, "rs_exchange_chips_ffn1_start")
    early_names = ["ffn2_w_gate", "ffn2_w_up", "ffn2_w_down", "w_in", "conv_w_proj", "attn_w_o", "w_out"]
    late_names = ["ffn1_w_gate", "ffn1_w_up", "ffn1_w_down"]
    reduced_early = rs_finish(place, owns_f2 + owns_m, list(gots_f2) + list(gots_m), token, "early")

    pad_row = lambda v: jnp.pad(v, ((0, 0), (0, D - v.shape[1])))
    small_rows = jnp.concatenate([
        d_ffn1_norm, d_mix_norm, merge_sums[2:3, :D], merge_sums[1:2, :D], merge_sums[1:2, D:],
        pad_row(d_sinks[0:1, :n_heads]), merge_sums[0:1, :D], merge_sums[0:1, D:], d_ffn2_norm,
        head_sums[0:1], head_sums[1:2], jnp.zeros((5, D), F32), d_dw_w], axis=0)
    small = allreduce_small(small_rows)
    loss = small[10, 0]
    grads = {"ffn1_norm": small[0:1], "mix_norm": small[1:2], "conv_dw_b": small[2:3], "conv_ln_g": small[3:4],
             "conv_ln_b": small[4:5], "attn_sinks": small[5:6, :n_heads],
             "gate_b": jnp.concatenate([small[6:7], small[7:8]], axis=1), "ffn2_norm": small[8:9],
             "final_norm": small[9:10]}
    grads["conv_dw_w"] = lax.dynamic_slice(small[16:16 + CONV_WIDTH], (0, my_chip * dwc), (CONV_WIDTH, dwc))
    grads.update(zip(early_names, reduced_early))

    deltas, new_m, new_v = {}, {}, {}

    def apply_adamw(ks):
        flat = lambda a, k: a.reshape(-1, weights[k].shape[-1])
        done = {}
        for idx in _by_shape([flat(grads[k], k) for k in ks]):
            same = [ks[i] for i in idx]
            results = adamw([flat(weights[k], k) for k in same], [flat(grads[k], k) for k in same],
                            [flat(m_in[k], k) for k in same], [flat(v_in[k], k) for k in same], f"adamw_{same[0]}")
            for k, (d, mn, vn) in zip(same, results):
                shape = weights[k].shape
                grads[k] = grads[k].reshape(shape)
                deltas[k], new_m[k], new_v[k] = d.reshape(shape), mn.reshape(shape), vn.reshape(shape)
                done[k] = d
        return done

    done = apply_adamw([k for k in names if k not in late_names])
    gots_l = exchange_chips_wait(send_sems, recv_sems, wires_l, lands_l, [done[k] for k in early_names],
                                 "rs_exchange_chips_ffn1_wait")
    grads.update(zip(late_names, rs_finish(place, owns_l, gots_l, token, "late")))
    apply_adamw(late_names)
    for k in transposed:
        for group in (grads, deltas, new_m, new_v):
            group[k] = jnp.swapaxes(group[k], 1, 2)

    return (loss, dx0[None], *[grads[k] for k in names], *[deltas[k] for k in names],
            *[new_m[k] for k in names], *[new_v[k] for k in names])
```

```python
import functools

import jax
import jax.numpy as jnp
from jax import lax
from jax.experimental import pallas as pl
from jax.experimental.pallas import tpu as pltpu

F32 = jnp.float32
BF16 = jnp.bfloat16
MESH = pl.DeviceIdType.MESH

HEAD_DIM = 64
WINDOW = 128
CONV_WIDTH = 31
CONV_HALO = 32
ROPE_THETA = 10000.0
EPS = 1e-6
LN_EPS = 1e-5
NEG_INF = -1e30
N_CHIPS = 4
N_DEV = 8

ADAM_LR = 0.001
ADAM_B1 = 0.9
ADAM_B2 = 0.999
ADAM_EPS = 1e-08
ADAM_WD = 0.01
ADAM_STEP = 10

TM_FFN = 512
TM_FFN_FWD = 1024
TM_ROW = 256
TM_MIX = 512
TK_TN = 1024
TR_ELT = 256
VMEM_LIMIT = 56 * 1024 * 1024

NT_DIMS = (((1,), (1,)), ((), ()))
TN_DIMS = (((0,), (0,)), ((), ()))


def _row_tile(rows, cap):
    for t in range(min(cap, rows), 15, -1):
        if rows % t == 0 and t % 16 == 0:
            return t
    return rows


def _params(sem):
    return pltpu.CompilerParams(dimension_semantics=sem, vmem_limit_bytes=VMEM_LIMIT)


def _dot(a, b):
    return jnp.dot(a, b, preferred_element_type=F32)


def _dot_nt(a, b):
    return lax.dot_general(a, b, NT_DIMS, preferred_element_type=F32)


def _dot_tn(a, b):
    return lax.dot_general(a, b, TN_DIMS, preferred_element_type=F32)


def _split_rows(dot, a, b):
    m = a.shape[0] // 2
    return jnp.concatenate([dot(a[:m], b), dot(a[m:], b)], axis=0)


def _sigmoid(x):
    return jax.nn.sigmoid(x)


def _rms_scale(xv):
    return lax.rsqrt(jnp.mean(xv * xv, axis=-1, keepdims=True) + EPS)


def _rms_bwd(xv, nw, dh):
    r = _rms_scale(xv)
    dn = dh * nw
    dx = r * dn - xv * (r * r * r) * jnp.mean(dn * xv, axis=-1, keepdims=True)
    dnw = jnp.sum(dh * (xv * r), axis=0, keepdims=True)
    return dx, dnw


def _silu_grad(z, s):
    return s * (1.0 + z * (1.0 - s))


HBM_SPEC = pl.BlockSpec(memory_space=pl.ANY)


def _call_hosting(body, side, *, grid, in_specs, out_specs, out_shape, scratch_shapes, operands, name, aliases=None):
    params = _params(("arbitrary",) * len(grid))
    aliases = dict(aliases or {})
    if side is None:
        return pl.pallas_call(body, name=name, grid=grid, in_specs=in_specs, out_specs=out_specs, out_shape=out_shape,
                              scratch_shapes=scratch_shapes, input_output_aliases=aliases,
                              compiler_params=params)(*operands)
    n_in, n_out, n_scr = len(in_specs), len(out_shape), len(scratch_shapes)
    s_in, s_out = len(side["inputs"]), len(side["out_shapes"])

    steps = 1
    for extent in grid:
        steps *= extent

    def at_step(index):
        linear = pl.program_id(0)
        for a in range(1, len(grid)):
            linear = linear * grid[a] + pl.program_id(a)
        return linear == index

    def hosted(*refs):
        b = n_in + s_in
        c = b + n_out
        d = c + s_out
        e = d + n_scr
        src, dst, sems = refs[n_in:b], refs[c:d], refs[e:]

        @pl.when(at_step(0))
        def _():
            side["start"](src, dst, sems)

        if "relay" in side:
            @pl.when(at_step(min((3 * steps) // 4, steps - 1)))
            def _():
                side["relay"](src, dst, sems)

        body(*refs[:n_in], *refs[b:c], *refs[d:e])

        @pl.when(at_step(steps - 1))
        def _():
            side["finish"](src, dst, sems)

    return pl.pallas_call(
        hosted, name=name, grid=grid, in_specs=list(in_specs) + [HBM_SPEC] * s_in,
        out_specs=list(out_specs) + [HBM_SPEC] * s_out, out_shape=list(out_shape) + list(side["out_shapes"]),
        scratch_shapes=list(scratch_shapes) + list(side["sems"]),
        input_output_aliases={**aliases, **{n_in + a: n_out + b for a, b in side["aliases"].items()}},
        compiler_params=params)(*operands, *side["inputs"])


def ffn_fwd(x, nw, wg, wu, wd, name, side=None):
    T, D = x.shape
    NP, Fs, _ = wg.shape
    tm = min(TM_FFN_FWD, T)

    def body(x_ref, nw_ref, wg_ref, wu_ref, wd_ref, xo_ref, h_ref, g_ref, u_ref, acc_ref):
        j = pl.program_id(1)

        @pl.when(j == 0)
        def _():
            xv = x_ref[...]
            h_ref[...] = (xv * _rms_scale(xv) * nw_ref[...]).astype(BF16)
            acc_ref[...] = jnp.zeros_like(acc_ref)

        h = h_ref[...]
        g = _dot_nt(h, wg_ref[...])
        u = _dot_nt(h, wu_ref[...])
        a = (g * _sigmoid(g)) * u
        g_ref[...] = g.astype(BF16)
        u_ref[...] = u.astype(BF16)
        acc_ref[...] += _dot(a.astype(BF16), wd_ref[...])

        @pl.when(j == NP - 1)
        def _():
            xo_ref[...] = x_ref[...] + 0.5 * acc_ref[...]

    return _call_hosting(
        body, side, name=name, grid=(T // tm, NP),
        in_specs=[pl.BlockSpec((tm, D), lambda i, j: (i, 0)),
                  pl.BlockSpec((1, D), lambda i, j: (0, 0)),
                  pl.BlockSpec((None, Fs, D), lambda i, j: (j, 0, 0)),
                  pl.BlockSpec((None, Fs, D), lambda i, j: (j, 0, 0)),
                  pl.BlockSpec((None, Fs, D), lambda i, j: (j, 0, 0))],
        out_specs=[pl.BlockSpec((tm, D), lambda i, j: (i, 0)),
                   pl.BlockSpec((tm, D), lambda i, j: (i, 0)),
                   pl.BlockSpec((None, tm, Fs), lambda i, j: (j, i, 0)),
                   pl.BlockSpec((None, tm, Fs), lambda i, j: (j, i, 0))],
        out_shape=[jax.ShapeDtypeStruct((T, D), F32), jax.ShapeDtypeStruct((T, D), BF16),
                   jax.ShapeDtypeStruct((NP, T, Fs), BF16), jax.ShapeDtypeStruct((NP, T, Fs), BF16)],
        scratch_shapes=[pltpu.VMEM((tm, D), F32)],
        operands=(x, nw, wg, wu, wd))


def _ffn_bwd_piece(j, h, g, u, wg, wu, wd, dout, dh_in, dws_in, name, side, norm, pin=None):
    T, D = h.shape
    NP, Fs, _ = wg.shape
    tm = min(TM_FFN, T)
    n_in = 7 + (dh_in is not None) + (2 if norm else 0) + (pin is not None) + (3 if dws_in else 0)

    def body(*refs):
        h_ref, g_ref, u_ref, wg_ref, wu_ref, wd_ref, do_ref = refs[:7]
        dhin_ref = refs[7] if dh_in is not None else None
        dh_ref, dwg_ref, dwu_ref, dwd_ref = refs[n_in:n_in + 4]

        @pl.when(pl.program_id(0) == 0)
        def _():
            dwg_ref[...] = jnp.zeros_like(dwg_ref)
            dwu_ref[...] = jnp.zeros_like(dwu_ref)
            dwd_ref[...] = jnp.zeros_like(dwd_ref)
            if norm:
                refs[n_in + 4][...] = jnp.zeros_like(refs[n_in + 4])

        dob = (0.5 * do_ref[...]).astype(BF16)
        da = _split_rows(_dot_nt, dob, wd_ref[...])
        gf = g_ref[...].astype(F32)
        uf = u_ref[...].astype(F32)
        s = _sigmoid(gf)
        act = gf * s
        dg = (da * uf * _silu_grad(gf, s)).astype(BF16)
        du = (da * act).astype(BF16)
        a = (act * uf).astype(BF16)
        dh = _dot(dg, wg_ref[...]) + _dot(du, wu_ref[...])
        dh = dh if dhin_ref is None else dhin_ref[...] + dh
        if norm:
            x_ref, nw_ref = refs[7 + (dh_in is not None):9 + (dh_in is not None)]
            dxn, dnw = _rms_bwd(x_ref[...], nw_ref[...], dh)
            dh_ref[...] = do_ref[...] + dxn
            refs[n_in + 4][...] += dnw
        else:
            dh_ref[...] = dh
        hb = h_ref[...]
        dwg_ref[...] += _dot_tn(dg, hb)
        dwu_ref[...] += _dot_tn(du, hb)
        dwd_ref[...] += _dot_tn(a, dob)

    rows = pl.BlockSpec((tm, D), lambda i: (i, 0))
    piece = pl.BlockSpec((None, tm, Fs), lambda i: (j, i, 0))
    slot = pl.BlockSpec((None, Fs, D), lambda i: (j, 0, 0), pipeline_mode=pl.Buffered(1))
    in_specs = [rows, piece, piece, slot, slot, slot, rows]
    operands = [h, g, u, wg, wu, wd, dout]
    aliases = {}
    if dh_in is not None:
        in_specs.append(rows)
        operands.append(dh_in)
    if norm:
        in_specs += [rows, pl.BlockSpec((1, D), lambda i: (0, 0))]
        operands += list(norm)
    if pin is not None:
        in_specs.append(pl.BlockSpec((8, 128), lambda i: (0, 0)))
        operands.append(pin)
    if dws_in:
        aliases = {len(operands) + k: 1 + k for k in range(3)}
        in_specs += [HBM_SPEC] * 3
        operands += list(dws_in)
    out_specs = [rows, slot, slot, slot]
    out_shape = [jax.ShapeDtypeStruct((T, D), F32)] + [jax.ShapeDtypeStruct((NP, Fs, D), F32)] * 3
    if norm:
        out_specs.append(pl.BlockSpec((1, D), lambda i: (0, 0)))
        out_shape.append(jax.ShapeDtypeStruct((1, D), F32))
    return _call_hosting(body, side, name=name, grid=(T // tm,), in_specs=in_specs, out_specs=out_specs,
                         out_shape=out_shape, scratch_shapes=[], aliases=aliases, operands=tuple(operands))


def ffn_bwd(x, nw, h, g, u, wg, wu, wd, dout, name, side=None, pin=None):
    NP = wg.shape[0]
    dh, dws, extra = None, None, []
    for j in range(NP):
        dh, *rest = _ffn_bwd_piece(j, h, g, u, wg, wu, wd, dout, dh, dws, f"{name}_{j}",
                                   side if j == 0 else None, (x, nw) if j == NP - 1 else None,
                                   pin if j == 0 else None)
        dws, rest = rest[:3], rest[3:]
        if j == 0:
            extra = rest[1:] if NP == 1 else rest
    return (dh, *dws, rest[0], *extra)


def mix_in_fwd(x, nw, w_glu, w_qkv, w_gate, cs, sn, name):
    T, D = x.shape
    KV = (w_qkv.shape[1] - D) // 2
    tm = min(TM_MIX, T)

    def body(x_ref, nw_ref, wa_ref, wq_ref, wg_ref, cs_ref, sn_ref, h_ref, pa_ref, pg_ref, q_ref, k_ref, v_ref):
        xv = x_ref[...]
        h = (xv * _rms_scale(xv) * nw_ref[...]).astype(BF16)
        h_ref[...] = h
        pa_ref[...] = _dot(h, wa_ref[...])
        pg_ref[...] = _dot(h, wg_ref[...])
        qkv = _dot(h, wq_ref[...])
        cs_v, sn_v = cs_ref[...], sn_ref[...]
        q_ref[...] = _rope_chunks(qkv[:, :D], cs_v, sn_v, 1.0).astype(BF16)
        k_ref[...] = _rope_chunks(qkv[:, D:D + KV], cs_v, sn_v, 1.0).astype(BF16)
        v_ref[...] = qkv[:, D + KV:].astype(BF16)

    rows = lambda w: pl.BlockSpec((tm, w), lambda i: (i, 0))
    whole = lambda a: pl.BlockSpec(a.shape, lambda i: (0, 0), pipeline_mode=pl.Buffered(1))
    return pl.pallas_call(
        body, name=name, grid=(T // tm,),
        in_specs=[rows(D), whole(nw), whole(w_glu), whole(w_qkv), whole(w_gate), rows(128), rows(128)],
        out_specs=[rows(D), rows(2 * D), rows(2 * D), rows(D), rows(KV), rows(KV)],
        out_shape=[jax.ShapeDtypeStruct((T, D), BF16), jax.ShapeDtypeStruct((T, 2 * D), F32),
                   jax.ShapeDtypeStruct((T, 2 * D), F32), jax.ShapeDtypeStruct((T, D), BF16),
                   jax.ShapeDtypeStruct((T, KV), BF16), jax.ShapeDtypeStruct((T, KV), BF16)],
        compiler_params=_params(("parallel",)),
    )(x, nw, w_glu, w_qkv, w_gate, cs, sn)


def matmul_tn(pairs, tk, name):
    n = len(pairs)
    T = pairs[0][0].shape[0]
    tk = min(tk, T)

    def body(*refs):
        @pl.when(pl.program_id(0) == 0)
        def _():
            for a in range(n):
                refs[2 * n + a][...] = jnp.zeros_like(refs[2 * n + a])

        for a in range(n):
            refs[2 * n + a][...] += _dot_tn(refs[2 * a][...].astype(BF16), refs[2 * a + 1][...].astype(BF16))

    rows = lambda a: pl.BlockSpec((tk, a.shape[1]), lambda t: (t, 0))
    shapes = [(lhs.shape[1], rhs.shape[1]) for lhs, rhs in pairs]
    return pl.pallas_call(
        body, name=name, grid=(T // tk,),
        in_specs=[rows(a) for pair in pairs for a in pair],
        out_specs=[pl.BlockSpec(s, lambda t: (0, 0), pipeline_mode=pl.Buffered(1)) for s in shapes],
        out_shape=[jax.ShapeDtypeStruct(s, F32) for s in shapes],
        compiler_params=_params(("arbitrary",)),
    )(*[a for pair in pairs for a in pair])


def mix_in_bwd(dps, ws, x, nw, dres, name, side=None):
    T, D = x.shape
    tm = min(TM_MIX, T)
    n = len(dps)

    def body(*refs):
        dp_refs, w_refs = refs[:n], refs[n:2 * n]
        x_ref, nw_ref, dr_ref, dx_ref, dnw_ref = refs[2 * n:]

        @pl.when(pl.program_id(0) == 0)
        def _():
            dnw_ref[...] = jnp.zeros_like(dnw_ref)

        dh = _dot_nt(dp_refs[0][...], w_refs[0][...])
        for k in range(1, n):
            dh += _dot_nt(dp_refs[k][...], w_refs[k][...])
        dxn, dnw = _rms_bwd(x_ref[...], nw_ref[...], dh)
        dx_ref[...] = dr_ref[...] + dxn
        dnw_ref[...] += dnw

    in_specs = [pl.BlockSpec((tm, dp.shape[1]), lambda i: (i, 0)) for dp in dps]
    in_specs += [pl.BlockSpec(w.shape, lambda i: (0, 0), pipeline_mode=pl.Buffered(1)) for w in ws]
    in_specs += [pl.BlockSpec((tm, D), lambda i: (i, 0)), pl.BlockSpec((1, D), lambda i: (0, 0)),
                 pl.BlockSpec((tm, D), lambda i: (i, 0))]
    return _call_hosting(
        body, side, name=name, grid=(T // tm,), in_specs=in_specs,
        out_specs=[pl.BlockSpec((tm, D), lambda i: (i, 0)), pl.BlockSpec((1, D), lambda i: (0, 0))],
        out_shape=[jax.ShapeDtypeStruct((T, D), F32), jax.ShapeDtypeStruct((1, D), F32)],
        scratch_shapes=[], operands=(*dps, *ws, x, nw, dres))


def _layernorm_stats(c1):
    mu = jnp.mean(c1, axis=-1, keepdims=True)
    xc = c1 - mu
    rstd = lax.rsqrt(jnp.mean(xc * xc, axis=-1, keepdims=True) + LN_EPS)
    return xc * rstd, rstd


def _shifted_copies(src_ref, dst_ref):
    rows = dst_ref.shape[1]
    for b in range(1, 8):
        dst_ref[b - 1] = src_ref[pl.ds(b, rows), :]


def _shifted_rows(src_ref, shifted_ref, start, rows, cols):
    a8, b = divmod(start, 8)
    if b == 0:
        return src_ref[pl.ds(8 * a8, rows), cols]
    return shifted_ref[b - 1, pl.ds(8 * a8, rows), cols]


def conv_fwd(p_glu, dw_w, dw_b, ln_g, ln_b, name):
    T, D2 = p_glu.shape
    D = D2 // 2
    tm = min(TM_ROW, T)
    hb = tm // CONV_HALO

    def body(a_ref, b_ref, ah_ref, bh_ref, w_ref, wb_ref, g_ref, be_ref, c1_ref, c3_ref, e_ref, es_ref):
        i = pl.program_id(0)
        halo = ah_ref[...] * _sigmoid(bh_ref[...])
        e_ref[pl.ds(0, CONV_HALO), :] = jnp.where(i > 0, halo, 0.0)
        e_ref[pl.ds(CONV_HALO, tm), :] = a_ref[...] * _sigmoid(b_ref[...])
        _shifted_copies(e_ref, es_ref)
        off = CONV_HALO - (CONV_WIDTH - 1)

        def strip(s, carry):
            cols = pl.ds(pl.multiple_of(s * 128, 128), 128)
            acc = jnp.zeros((tm, 128), F32) + wb_ref[:, cols]
            for k in range(CONV_WIDTH):
                acc += w_ref[pl.ds(k, 1), cols] * _shifted_rows(e_ref, es_ref, off + k, tm, cols)
            c1_ref[:, cols] = acc
            return carry

        lax.fori_loop(0, D // 128, strip, 0)
        xhat, _ = _layernorm_stats(c1_ref[...])
        c2 = xhat * g_ref[...] + be_ref[...]
        c3_ref[...] = (c2 * _sigmoid(c2)).astype(BF16)

    row = pl.BlockSpec((1, D), lambda i: (0, 0))
    return pl.pallas_call(
        body, name=name, grid=(T // tm,),
        in_specs=[pl.BlockSpec((tm, D), lambda i: (i, 0)), pl.BlockSpec((tm, D), lambda i: (i, 1)),
                  pl.BlockSpec((CONV_HALO, D), lambda i: (jnp.maximum(i * hb - 1, 0), 0)),
                  pl.BlockSpec((CONV_HALO, D), lambda i: (jnp.maximum(i * hb - 1, 0), 1)),
                  pl.BlockSpec((CONV_HALO, D), lambda i: (0, 0)), row, row, row],
        out_specs=[pl.BlockSpec((tm, D), lambda i: (i, 0)), pl.BlockSpec((tm, D), lambda i: (i, 0))],
        out_shape=[jax.ShapeDtypeStruct((T, D), F32), jax.ShapeDtypeStruct((T, D), BF16)],
        scratch_shapes=[pltpu.VMEM((tm + CONV_HALO, D), F32), pltpu.VMEM((7, tm + CONV_HALO - 8, D), F32)],
        compiler_params=_params(("parallel",)),
    )(p_glu, p_glu, p_glu, p_glu, dw_w, dw_b, ln_g, ln_b)


def conv_bwd(p_glu, dc1, dw_w, name, side=None):
    T, D2 = p_glu.shape
    D = D2 // 2
    tm = min(TM_ROW, T)
    hb = tm // CONV_HALO
    last = T // CONV_HALO - 1
    nblk = T // tm

    def body(a_ref, b_ref, ah_ref, bh_ref, d_ref, dn_ref, w_ref, dp_ref, dw_ref, e_ref, f_ref, es_ref, fs_ref):
        i = pl.program_id(0)

        @pl.when(i == 0)
        def _():
            dw_ref[...] = jnp.zeros_like(dw_ref)

        halo = ah_ref[...] * _sigmoid(bh_ref[...])
        e_ref[pl.ds(0, CONV_HALO), :] = jnp.where(i > 0, halo, 0.0)
        e_ref[pl.ds(CONV_HALO, tm), :] = a_ref[...] * _sigmoid(b_ref[...])
        f_ref[pl.ds(0, tm), :] = d_ref[...]
        f_ref[pl.ds(tm, CONV_HALO), :] = jnp.where(i < nblk - 1, dn_ref[...], 0.0)
        _shifted_copies(e_ref, es_ref)
        _shifted_copies(f_ref, fs_ref)
        off = CONV_HALO - (CONV_WIDTH - 1)

        def strip(s, carry):
            cols = pl.ds(pl.multiple_of(s * 128, 128), 128)
            d = d_ref[:, cols]
            dc0 = jnp.zeros((tm, 128), F32)
            for k in range(CONV_WIDTH):
                dw_ref[pl.ds(k, 1), cols] += jnp.sum(d * _shifted_rows(e_ref, es_ref, off + k, tm, cols),
                                                     axis=0, keepdims=True)
                dc0 += w_ref[pl.ds(k, 1), cols] * _shifted_rows(f_ref, fs_ref, CONV_WIDTH - 1 - k, tm, cols)
            a = a_ref[:, cols]
            sb = _sigmoid(b_ref[:, cols])
            dp_ref[:, cols] = (dc0 * sb).astype(BF16)
            dp_ref[:, pl.ds(pl.multiple_of(D + s * 128, 128), 128)] = (dc0 * a * sb * (1.0 - sb)).astype(BF16)
            return carry

        lax.fori_loop(0, D // 128, strip, 0)

    return _call_hosting(
        body, side, name=name, grid=(nblk,),
        in_specs=[pl.BlockSpec((tm, D), lambda i: (i, 0)), pl.BlockSpec((tm, D), lambda i: (i, 1)),
                  pl.BlockSpec((CONV_HALO, D), lambda i: (jnp.maximum(i * hb - 1, 0), 0)),
                  pl.BlockSpec((CONV_HALO, D), lambda i: (jnp.maximum(i * hb - 1, 0), 1)),
                  pl.BlockSpec((tm, D), lambda i: (i, 0)),
                  pl.BlockSpec((CONV_HALO, D), lambda i: (jnp.minimum((i + 1) * hb, last), 0)),
                  pl.BlockSpec((CONV_HALO, D), lambda i: (0, 0))],
        out_specs=[pl.BlockSpec((tm, D2), lambda i: (i, 0)), pl.BlockSpec((CONV_HALO, D), lambda i: (0, 0))],
        out_shape=[jax.ShapeDtypeStruct((T, D2), BF16), jax.ShapeDtypeStruct((CONV_HALO, D), F32)],
        scratch_shapes=[pltpu.VMEM((tm + CONV_HALO, D), F32), pltpu.VMEM((tm + CONV_HALO, D), F32),
                        pltpu.VMEM((7, tm + CONV_HALO - 8, D), F32), pltpu.VMEM((7, tm + CONV_HALO - 8, D), F32)],
        operands=(p_glu, p_glu, p_glu, p_glu, dc1, dc1, dw_w))


def _rot_half(x):
    lane = lax.broadcasted_iota(jnp.int32, x.shape, 1)
    first = (lane % HEAD_DIM) < HEAD_DIM // 2
    return jnp.where(first, pltpu.roll(x, 128 - HEAD_DIM // 2, 1), pltpu.roll(x, HEAD_DIM // 2, 1))


def _rope_chunks(x, cs, sn, sign):
    outs = []
    for c in range(x.shape[1] // 128):
        xc = x[:, c * 128:(c + 1) * 128]
        outs.append(xc * cs + sign * (_rot_half(xc) * sn))
    return outs[0] if len(outs) == 1 else jnp.concatenate(outs, axis=1)


def rope_bwd(dq, dk, dv, cs, sn, name, side=None):
    T, D = dq.shape
    KV = dk.shape[1]
    tm = min(TM_ROW, T)

    def body(dq_ref, dk_ref, dv_ref, cs_ref, sn_ref, o_ref):
        cs_v, sn_v = cs_ref[...], sn_ref[...]
        o_ref[:, pl.ds(0, D)] = _rope_chunks(dq_ref[...], cs_v, sn_v, -1.0).astype(BF16)
        o_ref[:, pl.ds(D, KV)] = _rope_chunks(dk_ref[...], cs_v, sn_v, -1.0).astype(BF16)
        o_ref[:, pl.ds(D + KV, KV)] = dv_ref[...].astype(BF16)

    tab = pl.BlockSpec((tm, 128), lambda i: (i, 0))
    return _call_hosting(
        body, side, name=name, grid=(T // tm,),
        in_specs=[pl.BlockSpec((tm, D), lambda i: (i, 0)), pl.BlockSpec((tm, KV), lambda i: (i, 0)),
                  pl.BlockSpec((tm, KV), lambda i: (i, 0)), tab, tab],
        out_specs=[pl.BlockSpec((tm, D + 2 * KV), lambda i: (i, 0))],
        out_shape=[jax.ShapeDtypeStruct((T, D + 2 * KV), BF16)],
        scratch_shapes=[], operands=(dq, dk, dv, cs, sn))


def _lane_lo():
    return lax.broadcasted_iota(jnp.int32, (1, 128), 1) < HEAD_DIM


def _band_mask(i, reps):
    shape = (reps * WINDOW, 2 * WINDOW)
    qi = lax.broadcasted_iota(jnp.int32, shape, 0) % WINDOW
    cj = lax.broadcasted_iota(jnp.int32, shape, 1)
    rel = qi - cj + WINDOW
    return (rel >= 0) & (rel < WINDOW) & ((i > 0) | (cj >= WINDOW))


def _stack_pairs(ref, first, n):
    parts = [ref[:, pl.ds((first + p) * 128, 128)] for p in range(n)]
    return parts[0] if n == 1 else jnp.concatenate(parts, axis=0)


def _pair_rows(n):
    return lax.broadcasted_iota(jnp.int32, (n * WINDOW, 1), 0) // WINDOW


def _per_pair_column(values, n):
    rows = _pair_rows(n)
    col = jnp.zeros((n * WINDOW, 1), F32) + values[0]
    for p in range(1, n):
        col = jnp.where(rows == p, values[p], col)
    return col


def _kv_lo_hi(x2, g):
    pair, half = divmod(g, 2)
    lo = _lane_lo()
    xg = x2[:, pair * 128:(pair + 1) * 128].astype(F32)
    xg = jnp.where(lo if half == 0 else ~lo, xg, 0.0)
    sw = pltpu.roll(xg, HEAD_DIM, 1)
    x_lo, x_hi = (xg, sw) if half == 0 else (sw, xg)
    return x_lo.astype(BF16), x_hi.astype(BF16)


def _softmax_sink(s, allowed, sink):
    s = jnp.where(allowed, s * (HEAD_DIM ** -0.5), NEG_INF)
    m = jnp.maximum(jnp.max(s, axis=-1, keepdims=True), sink)
    p = jnp.exp(s - m)
    es = jnp.exp(sink - m)
    inv = 1.0 / (jnp.sum(p, axis=-1, keepdims=True) + es)
    return p * inv, es * inv


def attn_fwd(qr, kr, vb, sinks, name, side=None):
    T, D = qr.shape
    KV = kr.shape[1]
    n_kv = KV // HEAD_DIM
    group = (D // HEAD_DIM) // n_kv
    nb = T // WINDOW

    npair = group // 2

    def body(sink_ref, q_ref, kp_ref, kc_ref, vp_ref, vc_ref, o_ref):
        i = pl.program_id(0)
        allowed = _band_mask(i, npair)
        k2 = jnp.concatenate([kp_ref[...], kc_ref[...]], axis=0)
        v2 = jnp.concatenate([vp_ref[...], vc_ref[...]], axis=0)
        outs = [None] * (D // 128)
        for g in range(n_kv):
            k_lo, k_hi = _kv_lo_hi(k2, g)
            v_lo, v_hi = _kv_lo_hi(v2, g)
            first = (g * group) // 2
            q = _stack_pairs(q_ref, first, npair)
            sink_e = _per_pair_column([sink_ref[0, g * group + 2 * p] for p in range(npair)], npair)
            sink_o = _per_pair_column([sink_ref[0, g * group + 2 * p + 1] for p in range(npair)], npair)
            pe, _ = _softmax_sink(_dot_nt(q, k_lo), allowed, sink_e)
            po, _ = _softmax_sink(_dot_nt(q, k_hi), allowed, sink_o)
            o = _dot(pe.astype(BF16), v_lo) + _dot(po.astype(BF16), v_hi)
            for p in range(npair):
                outs[first + p] = o[p * WINDOW:(p + 1) * WINDOW]
        o_ref[...] = jnp.concatenate(outs, axis=1).astype(BF16)

    prev = lambda i: (jnp.maximum(i - 1, 0), 0)
    cur = lambda i: (i, 0)
    return _call_hosting(
        body, side, name=name, grid=(nb,),
        in_specs=[pl.BlockSpec(memory_space=pltpu.SMEM),
                  pl.BlockSpec((WINDOW, D), cur),
                  pl.BlockSpec((WINDOW, KV), prev), pl.BlockSpec((WINDOW, KV), cur),
                  pl.BlockSpec((WINDOW, KV), prev), pl.BlockSpec((WINDOW, KV), cur)],
        out_specs=[pl.BlockSpec((WINDOW, D), cur)],
        out_shape=[jax.ShapeDtypeStruct((T, D), BF16)],
        scratch_shapes=[], operands=(sinks, qr, kr, kr, vb, vb))


def attn_bwd(qr, kr, vb, o, do, sinks, name, side=None):
    T, D = qr.shape
    KV = kr.shape[1]
    n_heads = D // HEAD_DIM
    n_kv = KV // HEAD_DIM
    group = n_heads // n_kv
    nb = T // WINDOW
    npair = group // 2
    scale = HEAD_DIM ** -0.5

    def body(sink_ref, q_ref, kp_ref, kc_ref, vp_ref, vc_ref, o_ref, do_ref,
             dq_ref, dk_ref, dv_ref, ds_ref, ck_ref, cv_ref):
        i = pl.program_id(0)
        lo = _lane_lo()

        @pl.when(i == 0)
        def _():
            ck_ref[...] = jnp.zeros_like(ck_ref)
            cv_ref[...] = jnp.zeros_like(cv_ref)
            ds_ref[...] = jnp.zeros_like(ds_ref)

        @pl.when(i < nb)
        def _():
            allowed = _band_mask(i, npair)
            rows = _pair_rows(npair)
            k2 = jnp.concatenate([kp_ref[...], kc_ref[...]], axis=0)
            v2 = jnp.concatenate([vp_ref[...], vc_ref[...]], axis=0)
            lane = lax.broadcasted_iota(jnp.int32, (1, 128), 1)
            dsink = jnp.zeros((1, 128), F32)
            dq_out = [None] * (D // 128)
            dk_pairs = [jnp.zeros((2 * WINDOW, 128), F32) for _ in range(KV // 128)]
            dv_pairs = [jnp.zeros((2 * WINDOW, 128), F32) for _ in range(KV // 128)]
            for g in range(n_kv):
                k_lo, k_hi = _kv_lo_hi(k2, g)
                v_lo, v_hi = _kv_lo_hi(v2, g)
                first = (g * group) // 2
                q = _stack_pairs(q_ref, first, npair)
                dop = _stack_pairs(do_ref, first, npair)
                dd = dop.astype(F32) * _stack_pairs(o_ref, first, npair).astype(F32)
                dq = jnp.zeros((npair * WINDOW, 128), F32)
                dkg = jnp.zeros((2 * WINDOW, 128), F32)
                dvg = jnp.zeros((2 * WINDOW, 128), F32)
                for parity, k_h, v_h, sel in ((0, k_lo, v_lo, lo), (1, k_hi, v_hi, ~lo)):
                    heads = [g * group + 2 * p + parity for p in range(npair)]
                    sink = _per_pair_column([sink_ref[0, h] for h in heads], npair)
                    p_, ps = _softmax_sink(_dot_nt(q, k_h), allowed, sink)
                    delta = jnp.sum(jnp.where(sel, dd, 0.0), axis=-1, keepdims=True)
                    dsc = (p_ * (_dot_nt(dop, v_h) - delta)).astype(BF16)
                    sd = -ps * delta
                    for p, h in enumerate(heads):
                        dsink += jnp.where(lane == h, jnp.sum(jnp.where(rows == p, sd, 0.0)), 0.0)
                    dq += _dot(dsc, k_h)
                    dkg += jnp.where(sel, _dot_tn(dsc, q), 0.0)
                    dvg += jnp.where(sel, _dot_tn(p_.astype(BF16), dop), 0.0)
                for p in range(npair):
                    dq_out[first + p] = dq[p * WINDOW:(p + 1) * WINDOW]
                pair, half = divmod(g, 2)
                keep = lo if half == 0 else ~lo
                dk_pairs[pair] += jnp.where(keep, dkg + pltpu.roll(dkg, HEAD_DIM, 1), 0.0) * scale
                dv_pairs[pair] += jnp.where(keep, dvg + pltpu.roll(dvg, HEAD_DIM, 1), 0.0)
            dq_ref[...] = jnp.concatenate(dq_out, axis=1) * scale
            dk2 = dk_pairs[0] if len(dk_pairs) == 1 else jnp.concatenate(dk_pairs, axis=1)
            dv2 = dv_pairs[0] if len(dv_pairs) == 1 else jnp.concatenate(dv_pairs, axis=1)
            dk_ref[...] = ck_ref[...] + dk2[:WINDOW]
            dv_ref[...] = cv_ref[...] + dv2[:WINDOW]
            ck_ref[...] = dk2[WINDOW:]
            cv_ref[...] = dv2[WINDOW:]
            ds_ref[pl.ds(0, 1), :] += dsink

        @pl.when(i == nb)
        def _():
            dk_ref[...] = ck_ref[...]
            dv_ref[...] = cv_ref[...]

    prev = lambda i: (jnp.maximum(i - 1, 0), 0)
    cur = lambda i: (jnp.minimum(i, nb - 1), 0)
    prevc = lambda i: (jnp.maximum(jnp.minimum(i, nb - 1) - 1, 0), 0)
    return _call_hosting(
        body, side, name=name, grid=(nb + 1,),
        in_specs=[pl.BlockSpec(memory_space=pltpu.SMEM),
                  pl.BlockSpec((WINDOW, D), cur),
                  pl.BlockSpec((WINDOW, KV), prevc), pl.BlockSpec((WINDOW, KV), cur),
                  pl.BlockSpec((WINDOW, KV), prevc), pl.BlockSpec((WINDOW, KV), cur),
                  pl.BlockSpec((WINDOW, D), cur), pl.BlockSpec((WINDOW, D), cur)],
        out_specs=[pl.BlockSpec((WINDOW, D), cur), pl.BlockSpec((WINDOW, KV), prev),
                   pl.BlockSpec((WINDOW, KV), prev), pl.BlockSpec((8, 128), lambda i: (0, 0))],
        out_shape=[jax.ShapeDtypeStruct((T, D), F32), jax.ShapeDtypeStruct((T, KV), F32),
                   jax.ShapeDtypeStruct((T, KV), F32), jax.ShapeDtypeStruct((8, 128), F32)],
        scratch_shapes=[pltpu.VMEM((WINDOW, KV), F32), pltpu.VMEM((WINDOW, KV), F32)],
        operands=(sinks, qr, kr, kr, vb, vb, o, do))


def merge_fwd(x, c3, o, p_gate, gate_b, w_proj, w_o, w_out, name):
    T, D = x.shape
    tm = min(TM_MIX, T)

    def body(x_ref, c3_ref, o_ref, gc_ref, ga_ref, bc_ref, ba_ref, wp_ref, wo_ref, wout_ref,
             xo_ref, co_ref, ao_ref, mg_ref):
        conv_out = _dot(c3_ref[...], wp_ref[...])
        attn_out = _dot(o_ref[...], wo_ref[...])
        merged = (_sigmoid(gc_ref[...] + bc_ref[...]) * conv_out
                  + _sigmoid(ga_ref[...] + ba_ref[...]) * attn_out).astype(BF16)
        co_ref[...] = conv_out.astype(BF16)
        ao_ref[...] = attn_out.astype(BF16)
        mg_ref[...] = merged
        xo_ref[...] = x_ref[...] + _dot(merged, wout_ref[...])

    blk = lambda j: pl.BlockSpec((tm, D), lambda i: (i, j))
    row = lambda j: pl.BlockSpec((1, D), lambda i: (0, j))
    mat = pl.BlockSpec((D, D), lambda i: (0, 0), pipeline_mode=pl.Buffered(1))
    return pl.pallas_call(
        body, name=name, grid=(T // tm,),
        in_specs=[blk(0), blk(0), blk(0), blk(0), blk(1), row(0), row(1), mat, mat, mat],
        out_specs=[blk(0), blk(0), blk(0), blk(0)],
        out_shape=[jax.ShapeDtypeStruct((T, D), F32)] + [jax.ShapeDtypeStruct((T, D), BF16)] * 3,
        compiler_params=_params(("parallel",)),
    )(x, c3, o, p_gate, p_gate, gate_b, gate_b, w_proj, w_o, w_out)


def merge_bwd(dx, p_gate, gate_b, conv_out, attn_out, c1, ln_g, ln_b, w_proj, w_o, w_out, name, side=None):
    T, D = dx.shape
    tm = min(TM_ROW, T)

    def body(dx_ref, gc_ref, ga_ref, bc_ref, ba_ref, co_ref, ao_ref, c1_ref, g_ref, be_ref,
             wp_ref, wo_ref, wout_ref, dgt_ref, dco_ref, dao_ref, do_ref, dc1_ref, sm_ref):
        @pl.when(pl.program_id(0) == 0)
        def _():
            sm_ref[...] = jnp.zeros_like(sm_ref)

        dm = _dot_nt(dx_ref[...].astype(BF16), wout_ref[...])
        sc = _sigmoid(gc_ref[...] + bc_ref[...])
        sa = _sigmoid(ga_ref[...] + ba_ref[...])
        dco = (dm * sc).astype(BF16)
        dao = (dm * sa).astype(BF16)
        dgc = dm * co_ref[...].astype(F32) * sc * (1.0 - sc)
        dga = dm * ao_ref[...].astype(F32) * sa * (1.0 - sa)
        dgt_ref[:, pl.ds(0, D)] = dgc.astype(BF16)
        dgt_ref[:, pl.ds(D, D)] = dga.astype(BF16)
        dco_ref[...] = dco
        dao_ref[...] = dao
        do_ref[...] = _dot_nt(dao, wo_ref[...]).astype(BF16)
        dc3 = _dot_nt(dco, wp_ref[...])
        xhat, rstd = _layernorm_stats(c1_ref[...])
        c2 = xhat * g_ref[...] + be_ref[...]
        dc2 = dc3 * _silu_grad(c2, _sigmoid(c2))
        dxh = dc2 * g_ref[...]
        dc1 = rstd * (dxh - jnp.mean(dxh, axis=-1, keepdims=True)
                      - xhat * jnp.mean(dxh * xhat, axis=-1, keepdims=True))
        dc1_ref[...] = dc1
        colsum = lambda v: jnp.sum(v, axis=0, keepdims=True)
        for r, (left, right) in enumerate(((dgc, dga), (dc2 * xhat, dc2), (dc1, None))):
            sm_ref[pl.ds(r, 1), pl.ds(0, D)] += colsum(left)
            if right is not None:
                sm_ref[pl.ds(r, 1), pl.ds(D, D)] += colsum(right)

    blk = lambda j: pl.BlockSpec((tm, D), lambda i: (i, j))
    row = lambda j: pl.BlockSpec((1, D), lambda i: (0, j))
    mat = pl.BlockSpec((D, D), lambda i: (0, 0))
    return _call_hosting(
        body, side, name=name, grid=(T // tm,),
        in_specs=[blk(0), blk(0), blk(1), row(0), row(1), blk(0), blk(0), blk(0), row(0), row(0), mat, mat, mat],
        out_specs=[pl.BlockSpec((tm, 2 * D), lambda i: (i, 0)), blk(0), blk(0), blk(0), blk(0),
                   pl.BlockSpec((8, 2 * D), lambda i: (0, 0))],
        out_shape=[jax.ShapeDtypeStruct((T, 2 * D), BF16)] + [jax.ShapeDtypeStruct((T, D), BF16)] * 3
                  + [jax.ShapeDtypeStruct((T, D), F32), jax.ShapeDtypeStruct((8, 2 * D), F32)],
        scratch_shapes=[],
        operands=(dx, p_gate, p_gate, gate_b, gate_b, conv_out, attn_out, c1, ln_g, ln_b, w_proj, w_o, w_out))


def loss_head(x, nw, target, name):
    T, D = x.shape
    tm = min(TM_ROW, T)

    def body(x_ref, nw_ref, t_ref, dx_ref, sm_ref):
        @pl.when(pl.program_id(0) == 0)
        def _():
            sm_ref[...] = jnp.zeros_like(sm_ref)

        xv = x_ref[...]
        err = xv * _rms_scale(xv) * nw_ref[...] - t_ref[...]
        loss = 0.5 * jnp.sum(jnp.mean(err * err, axis=-1, keepdims=True))
        dxn, dnw = _rms_bwd(xv, nw_ref[...], err * (1.0 / D))
        dx_ref[...] = dxn
        sm_ref[pl.ds(0, 1), :] += dnw
        sm_ref[pl.ds(1, 1), :] += jnp.zeros((1, D), F32) + loss

    return pl.pallas_call(
        body, name=name, grid=(T // tm,),
        in_specs=[pl.BlockSpec((tm, D), lambda i: (i, 0)), pl.BlockSpec((1, D), lambda i: (0, 0)),
                  pl.BlockSpec((tm, D), lambda i: (i, 0))],
        out_specs=[pl.BlockSpec((tm, D), lambda i: (i, 0)), pl.BlockSpec((8, D), lambda i: (0, 0))],
        out_shape=[jax.ShapeDtypeStruct((T, D), F32), jax.ShapeDtypeStruct((8, D), F32)],
        compiler_params=_params(("arbitrary",)),
    )(x, nw, target)


def _by_shape(arrays):
    groups = {}
    for k, a in enumerate(arrays):
        groups.setdefault(a.shape, []).append(k)
    return list(groups.values())


def adamw(ws, gs, ms, vs, name):
    n = len(ws)
    R, C = ws[0].shape
    tr = _row_tile(R, TR_ELT)

    def body(*refs):
        for a in range(n):
            w_ref, g_ref, m_ref, v_ref, d_ref, mo_ref, vo_ref = (refs[k * n + a] for k in range(7))
            gv = g_ref[...]
            mn = ADAM_B1 * m_ref[...] + (1.0 - ADAM_B1) * gv
            vn = ADAM_B2 * v_ref[...] + (1.0 - ADAM_B2) * (gv * gv)
            m_hat = mn / (1.0 - ADAM_B1 ** ADAM_STEP)
            v_hat = vn / (1.0 - ADAM_B2 ** ADAM_STEP)
            d_ref[...] = -ADAM_LR * (m_hat / (jnp.sqrt(v_hat) + ADAM_EPS) + ADAM_WD * w_ref[...])
            mo_ref[...] = mn
            vo_ref[...] = vn

    spec = pl.BlockSpec((tr, C), lambda i: (i, 0))
    outs = pl.pallas_call(
        body, name=name, grid=(R // tr,), in_specs=[spec] * (4 * n), out_specs=[spec] * (3 * n),
        out_shape=[jax.ShapeDtypeStruct((R, C), F32)] * (3 * n),
        compiler_params=_params(("parallel",)),
    )(*ws, *gs, *ms, *vs)
    return [(outs[a], outs[n + a], outs[2 * n + a]) for a in range(n)]


def _place():
    return lax.axis_index("x"), lax.axis_index("y"), lax.axis_index("c")


def place_shards(place, ws, dtype, name):
    n = len(ws)
    R, C = ws[0].shape
    tr = _row_tile(R, TR_ELT)

    def body(pc_ref, *refs):
        for a in range(n):
            refs[n + a][...] = refs[a][...].astype(dtype)

    return pl.pallas_call(
        body, name=name,
        grid_spec=pltpu.PrefetchScalarGridSpec(
            num_scalar_prefetch=1, grid=(R // tr,),
            in_specs=[pl.BlockSpec((tr, C), lambda r, pc: (r, 0))] * n,
            out_specs=[pl.BlockSpec((None, tr, C), lambda r, pc: (pc[0], r, 0))] * n),
        out_shape=[jax.ShapeDtypeStruct((N_CHIPS, R, C), dtype)] * n,
        compiler_params=_params(("arbitrary",)),
    )(place, *ws)


def gather_side(shards, small):
    n, ns = len(shards), len(small)

    def ici_copy(dst, sems, k, j, x, y, c, sending):
        px, py = x ^ (j >> 1), y ^ (j & 1)
        slot = 2 * x + y if sending else 2 * px + py
        half = dst[k].shape[1] // 2
        part = dst[k].at[slot, pl.ds(c * half, half)] if k < n else dst[k].at[slot]
        return pltpu.make_async_remote_copy(part, part, sems[0].at[3 * k + j - 1], sems[1].at[3 * k + j - 1],
                                            device_id=(px, py, c), device_id_type=MESH)

    def d2d_copy(dst, sems, k, j, x, y, c, sending):
        half = dst[k].shape[1] // 2
        part = dst[k].at[2 * (x ^ (j >> 1)) + (y ^ (j & 1)), pl.ds((c if sending else 1 - c) * half, half)]
        return pltpu.make_async_remote_copy(part, part, sems[2].at[3 * k + j - 1], sems[3].at[3 * k + j - 1],
                                            device_id=(x, y, 1 - c), device_id_type=MESH)

    def start(src, dst, sems):
        x, y, c = _place()
        for k in range(n + ns):
            for j in (1, 2, 3):
                ici_copy(dst, sems, k, j, x, y, c, True).start()

    def relay(src, dst, sems):
        x, y, c = _place()
        for k in range(n + ns):
            for j in (1, 2, 3):
                ici_copy(dst, sems, k, j, x, y, c, False).wait_recv()
                if k < n:
                    d2d_copy(dst, sems, k, j, x, y, c, True).start()

    def finish(src, dst, sems):
        x, y, c = _place()
        for k in range(n):
            for j in (1, 2, 3):
                d2d_copy(dst, sems, k, j, x, y, c, False).wait_recv()
        for k in range(n + ns):
            for j in (1, 2, 3):
                ici_copy(dst, sems, k, j, x, y, c, True).wait_send()
                if k < n:
                    d2d_copy(dst, sems, k, j, x, y, c, True).wait_send()

    arrays = list(shards) + list(small)
    return dict(inputs=arrays, out_shapes=[jax.ShapeDtypeStruct(a.shape, a.dtype) for a in arrays],
                aliases={k: k for k in range(n + ns)},
                sems=[pltpu.SemaphoreType.DMA((3 * (n + ns),)), pltpu.SemaphoreType.DMA((3 * (n + ns),)),
                      pltpu.SemaphoreType.DMA((3 * n,)), pltpu.SemaphoreType.DMA((3 * n,))],
                start=start, relay=relay, finish=finish)


def run_side(side, name):
    n_in, n_out = len(side["inputs"]), len(side["out_shapes"])

    def body(*refs):
        src, dst, sems = refs[:n_in], refs[n_in:n_in + n_out], refs[n_in + n_out:]
        side["start"](src, dst, sems)
        if "relay" in side:
            side["relay"](src, dst, sems)
        side["finish"](src, dst, sems)

    return pl.pallas_call(
        body, name=name, in_specs=[HBM_SPEC] * n_in, out_specs=[HBM_SPEC] * n_out,
        out_shape=side["out_shapes"], input_output_aliases=side["aliases"], scratch_shapes=side["sems"],
    )(*side["inputs"])


def allreduce_small(block):
    R, C = block.shape

    def body(x_ref, out_ref, all_ref, send_sems, recv_sems, local_sem):
        x, y, c = _place()
        me, sibling = (x, y, c), (x, y, 1 - c)
        chips = [(1 - x, y), (x, 1 - y), (1 - x, 1 - y)]

        def slot(px, py, pc):
            return all_ref.at[4 * px + 2 * py + pc]

        def copy(k, block_of, to, src=None):
            return pltpu.make_async_remote_copy(
                src_ref=slot(*block_of) if src is None else src, dst_ref=slot(*block_of),
                send_sem=send_sems.at[k], recv_sem=recv_sems.at[k], device_id=to, device_id_type=MESH)

        mine = pltpu.make_async_copy(x_ref, slot(*me), local_sem)
        mine.start()
        first = [copy(0, me, sibling, src=x_ref)]
        first += [copy(1 + j, me, (*chip, c), src=x_ref) for j, chip in enumerate(chips)]
        for cp in first:
            cp.start()
        passed = [copy(4 + j, (*chip, c), sibling) for j, chip in enumerate(chips)]
        for j, chip in enumerate(chips):
            copy(1 + j, (*chip, c), me).wait_recv()
            passed[j].start()
        copy(0, sibling, me).wait_recv()
        for j, chip in enumerate(chips):
            copy(4 + j, (*chip, 1 - c), me).wait_recv()
        for cp in first + passed:
            cp.wait_send()
        mine.wait()
        total = all_ref[0]
        for d in range(1, N_DEV):
            total = total + all_ref[d]
        out_ref[...] = total

    return pl.pallas_call(
        body, name="allreduce_small",
        in_specs=[pl.BlockSpec(memory_space=pltpu.VMEM)], out_specs=pl.BlockSpec(memory_space=pltpu.VMEM),
        out_shape=jax.ShapeDtypeStruct((R, C), F32),
        scratch_shapes=[pltpu.VMEM((N_DEV, R, C), F32), pltpu.SemaphoreType.DMA((7,)),
                        pltpu.SemaphoreType.DMA((7,)), pltpu.SemaphoreType.DMA],
        compiler_params=pltpu.CompilerParams(vmem_limit_bytes=VMEM_LIMIT),
    )(block)


def exchange_siblings_side(grads):
    n = len(grads)

    def copies(src, dst, sems):
        x, y, c = _place()
        for k in range(n):
            half = src[k].shape[1] // 2
            yield pltpu.make_async_remote_copy(src[k].at[:, pl.ds((1 - c) * half, half)], dst[k],
                                               sems[0].at[k], sems[1].at[k],
                                               device_id=(x, y, 1 - c), device_id_type=MESH)

    def start(src, dst, sems):
        for cp in copies(src, dst, sems):
            cp.start()

    def finish(src, dst, sems):
        for cp in copies(src, dst, sems):
            cp.wait()

    return dict(inputs=list(grads), aliases={},
                out_shapes=[jax.ShapeDtypeStruct((N_CHIPS, g.shape[1] // 2, g.shape[2]), F32) for g in grads],
                sems=[pltpu.SemaphoreType.DMA((n,)), pltpu.SemaphoreType.DMA((n,))], start=start, finish=finish)


def rs_chip_sum(place, grads, sibs, name):
    n = len(grads)
    NP, R, C = grads[0].shape
    half = R // 2
    tr = _row_tile(half, TR_ELT)
    nr = half // tr

    def body(pc_ref, *refs):
        q = pl.program_id(1)
        for a in range(n):
            g_ref, s_ref, wire_ref, own_ref = (refs[k * n + a] for k in range(4))
            total = g_ref[...] + s_ref[...]
            wire_ref[...] = total.astype(BF16)

            @pl.when(q == pc_ref[0])
            def _():
                own_ref[...] = total

    outs = pl.pallas_call(
        body, name=name,
        grid_spec=pltpu.PrefetchScalarGridSpec(
            num_scalar_prefetch=1, grid=(nr, NP),
            in_specs=[pl.BlockSpec((None, tr, C), lambda r, q, pc: (q, pc[1] * nr + r, 0))] * n
                     + [pl.BlockSpec((None, tr, C), lambda r, q, pc: (q, r, 0))] * n,
            out_specs=[pl.BlockSpec((None, tr, C), lambda r, q, pc: (q, r, 0))] * n
                      + [pl.BlockSpec((tr, C), lambda r, q, pc: (r, 0))] * n),
        out_shape=[jax.ShapeDtypeStruct((NP, half, C), BF16)] * n + [jax.ShapeDtypeStruct((half, C), F32)] * n,
        compiler_params=_params(("arbitrary", "arbitrary")),
    )(place, *grads, *sibs)
    return outs[:n], outs[n:]


def exchange_chips_side(wires):
    n = len(wires)

    def copies(src, dst, sems):
        x, y, c = _place()
        for k in range(n):
            for j in (1, 2, 3):
                qx, qy = x ^ (j >> 1), y ^ (j & 1)
                yield pltpu.make_async_remote_copy(src[k].at[2 * qx + qy], dst[k].at[2 * x + y],
                                                   sems[0].at[3 * k + j - 1], sems[1].at[3 * k + j - 1],
                                                   device_id=(qx, qy, c), device_id_type=MESH)

    def start(src, dst, sems):
        for cp in copies(src, dst, sems):
            cp.start()

    def finish(src, dst, sems):
        for cp in copies(src, dst, sems):
            cp.wait()

    return dict(inputs=list(wires), out_shapes=[jax.ShapeDtypeStruct(w.shape, BF16) for w in wires], aliases={},
                sems=[pltpu.SemaphoreType.DMA((3 * n,)), pltpu.SemaphoreType.DMA((3 * n,))],
                start=start, finish=finish)


SEM_SPEC = pl.BlockSpec(memory_space=pltpu.SEMAPHORE)


def exchange_chips_start(wires, name):
    n = len(wires)
    side = exchange_chips_side(wires)

    def body(*refs):
        src, land, sems = refs[:n], refs[n:2 * n], refs[2 * n:2 * n + 2]
        side["start"](src, land, sems)
        refs[-1][...] = jnp.zeros_like(refs[-1])

    hbm = [pltpu.HBM(w.shape, w.dtype) for w in wires]
    outs = pl.pallas_call(
        body, name=name, in_specs=[HBM_SPEC] * (2 * n),
        out_specs=[SEM_SPEC, SEM_SPEC] + [HBM_SPEC] * (2 * n) + [pl.BlockSpec(memory_space=pltpu.VMEM)],
        out_shape=list(side["sems"]) + hbm + hbm + [jax.ShapeDtypeStruct((8, 128), F32)],
        input_output_aliases={k: 2 + k for k in range(2 * n)},
        compiler_params=pltpu.CompilerParams(has_side_effects=pltpu.SideEffectType.DATAFLOW_SIDE_EFFECTING),
    )(*[pltpu.with_memory_space_constraint(w, pltpu.HBM) for w in wires],
      *[pltpu.with_memory_space_constraint(lax.empty(w.shape, w.dtype), pltpu.HBM) for w in wires])
    return outs[0], outs[1], outs[2:2 + n], outs[2 + n:2 + 2 * n], outs[-1]


def exchange_chips_wait(send_sems, recv_sems, wires, lands, after, name):
    n = len(wires)
    side = exchange_chips_side(wires)

    def body(*refs):
        side["finish"](refs[:n], refs[n:2 * n], refs[2 * n:2 * n + 2])

    hbm = [pltpu.HBM(w.shape, w.dtype) for w in wires]
    outs = pl.pallas_call(
        body, name=name, in_specs=[HBM_SPEC] * (2 * n) + [SEM_SPEC, SEM_SPEC] + [HBM_SPEC] * len(after),
        out_specs=[HBM_SPEC] * (2 * n), out_shape=hbm + hbm,
        input_output_aliases={k: k for k in range(2 * n)},
        compiler_params=pltpu.CompilerParams(has_side_effects=pltpu.SideEffectType.DATAFLOW_SIDE_EFFECTING),
    )(*wires, *lands, send_sems, recv_sems, *after)
    return outs[n:]


def rs_final_sum(place, owns, gots, after, name):
    n = len(owns)
    NP, half, C = gots[0].shape
    tr = _row_tile(half, TR_ELT)
    nr = half // tr

    def body(pc_ref, *refs):
        for a in range(n):
            own_ref, g1_ref, g2_ref, g3_ref = (refs[k * n + a] for k in range(4))
            refs[4 * n + 1 + a][...] = (((own_ref[...] + g1_ref[...].astype(F32)) + g2_ref[...].astype(F32))
                                        + g3_ref[...].astype(F32))

    slot = lambda j: pl.BlockSpec((None, tr, C), lambda r, pc: (pc[0] ^ j, r, 0))
    return pl.pallas_call(
        body, name=name,
        grid_spec=pltpu.PrefetchScalarGridSpec(
            num_scalar_prefetch=1, grid=(nr,),
            in_specs=[pl.BlockSpec((tr, C), lambda r, pc: (r, 0))] * n + [slot(1)] * n + [slot(2)] * n + [slot(3)] * n
                     + [pl.BlockSpec((8, 128), lambda r, pc: (0, 0))],
            out_specs=[pl.BlockSpec((tr, C), lambda r, pc: (pc[1] * nr + r, 0))] * n),
        out_shape=[jax.ShapeDtypeStruct((2 * half, C), F32)] * n,
        compiler_params=_params(("arbitrary",)),
    )(place, *owns, *gots, *gots, *gots, after)


def rs_share_siblings(totals, name):
    n = len(totals)

    def body(*refs):
        dst = refs[n:2 * n]
        send_sems, recv_sems = refs[2 * n:]
        x, y, c = _place()
        copies = []
        for k in range(n):
            half = dst[k].shape[0] // 2
            rows = dst[k].at[pl.ds(c * half, half)]
            cp = pltpu.make_async_remote_copy(rows, rows, send_sems.at[k], recv_sems.at[k],
                                              device_id=(x, y, 1 - c), device_id_type=MESH)
            cp.start()
            copies.append(cp)
        for k, cp in enumerate(copies):
            cp.wait_send()
            half = dst[k].shape[0] // 2
            got = dst[k].at[pl.ds((1 - c) * half, half)]
            pltpu.make_async_remote_copy(got, got, send_sems.at[k], recv_sems.at[k],
                                         device_id=(x, y, c), device_id_type=MESH).wait_recv()

    return pl.pallas_call(
        body, name=name,
        in_specs=[HBM_SPEC] * n, out_specs=[HBM_SPEC] * n,
        out_shape=[jax.ShapeDtypeStruct(t.shape, F32) for t in totals],
        input_output_aliases={k: k for k in range(n)},
        scratch_shapes=[pltpu.SemaphoreType.DMA((n,)), pltpu.SemaphoreType.DMA((n,))],
    )(*totals)


def rs_to_wires(place, grads, tag, sibs=None):
    if sibs is None:
        sibs = run_side(exchange_siblings_side(grads), f"rs_exchange_siblings_{tag}")
    wires, owns = [None] * len(grads), [None] * len(grads)
    for ks in _by_shape(grads):
        ws, os_ = rs_chip_sum(place, [grads[k] for k in ks], [sibs[k] for k in ks], f"rs_chip_sum_{tag}{ks[0]}")
        for k, w, o in zip(ks, ws, os_):
            wires[k], owns[k] = w, o
    return wires, owns


def rs_finish(place, owns, gots, after, tag):
    totals = [None] * len(owns)
    for ks in _by_shape(owns):
        sums = rs_final_sum(place, [owns[k] for k in ks], [gots[k] for k in ks], after, f"rs_final_sum_{tag}{ks[0]}")
        for k, t in zip(ks, sums):
            totals[k] = t
    return rs_share_siblings(totals, f"rs_share_siblings_{tag}")


def _rope_tables(positions):
    half = HEAD_DIM // 2
    inv_freq = ROPE_THETA ** (-jnp.arange(half, dtype=F32) / half)
    ang = positions.astype(F32)[:, None] * inv_freq
    lanes = jnp.arange(128)
    spread = (lanes[None, :] % half == jnp.arange(half)[:, None]).astype(F32)
    signed = spread * jnp.where(lanes % HEAD_DIM < half, -1.0, 1.0).astype(F32)
    exact = lax.Precision.HIGHEST
    return jnp.dot(jnp.cos(ang), spread, precision=exact), jnp.dot(jnp.sin(ang), signed, precision=exact)


def _cols_from_pieces(pieces, start, stop):
    C = pieces.shape[2]
    parts = []
    for q in range(N_CHIPS):
        lo, hi = max(start, q * C), min(stop, (q + 1) * C)
        if lo < hi:
            parts.append(pieces[q][:, lo - q * C:hi - q * C])
    return parts[0] if len(parts) == 1 else jnp.concatenate(parts, axis=1)


def _pieces_from_groups(groups):
    C = sum(g.shape[1] for g in groups) // N_CHIPS
    pieces = []
    for q in range(N_CHIPS):
        parts, off = [], 0
        for g in groups:
            lo, hi = max(q * C, off), min((q + 1) * C, off + g.shape[1])
            if lo < hi:
                parts.append(g[:, lo - off:hi - off])
            off += g.shape[1]
        pieces.append(parts[0] if len(parts) == 1 else jnp.concatenate(parts, axis=1))
    return jnp.stack(pieces)


def kernel(x, positions, ffn1_norm, ffn1_w_gate, ffn1_w_up, ffn1_w_down, mix_norm, w_in, conv_dw_w, conv_dw_b, conv_ln_g, conv_ln_b, conv_w_proj, attn_sinks, attn_w_o, gate_b, w_out, ffn2_norm, ffn2_w_gate, ffn2_w_up, ffn2_w_down, final_norm, loss_target, m_ffn1_norm, m_ffn1_w_gate, m_ffn1_w_up, m_ffn1_w_down, m_mix_norm, m_w_in, m_conv_dw_w, m_conv_dw_b, m_conv_ln_g, m_conv_ln_b, m_conv_w_proj, m_attn_sinks, m_attn_w_o, m_gate_b, m_w_out, m_ffn2_norm, m_ffn2_w_gate, m_ffn2_w_up, m_ffn2_w_down, m_final_norm, v_ffn1_norm, v_ffn1_w_gate, v_ffn1_w_up, v_ffn1_w_down, v_mix_norm, v_w_in, v_conv_dw_w, v_conv_dw_b, v_conv_ln_g, v_conv_ln_b, v_conv_w_proj, v_attn_sinks, v_attn_w_o, v_gate_b, v_w_out, v_ffn2_norm, v_ffn2_w_gate, v_ffn2_w_up, v_ffn2_w_down, v_final_norm):
    weights = dict(ffn1_norm=ffn1_norm, ffn1_w_gate=ffn1_w_gate, ffn1_w_up=ffn1_w_up, ffn1_w_down=ffn1_w_down,
                   mix_norm=mix_norm, w_in=w_in, conv_dw_w=conv_dw_w, conv_dw_b=conv_dw_b, conv_ln_g=conv_ln_g,
                   conv_ln_b=conv_ln_b, conv_w_proj=conv_w_proj, attn_sinks=attn_sinks, attn_w_o=attn_w_o,
                   gate_b=gate_b, w_out=w_out, ffn2_norm=ffn2_norm, ffn2_w_gate=ffn2_w_gate, ffn2_w_up=ffn2_w_up,
                   ffn2_w_down=ffn2_w_down, final_norm=final_norm)
    m_in = dict(ffn1_norm=m_ffn1_norm, ffn1_w_gate=m_ffn1_w_gate, ffn1_w_up=m_ffn1_w_up, ffn1_w_down=m_ffn1_w_down,
                mix_norm=m_mix_norm, w_in=m_w_in, conv_dw_w=m_conv_dw_w, conv_dw_b=m_conv_dw_b,
                conv_ln_g=m_conv_ln_g, conv_ln_b=m_conv_ln_b, conv_w_proj=m_conv_w_proj, attn_sinks=m_attn_sinks,
                attn_w_o=m_attn_w_o, gate_b=m_gate_b, w_out=m_w_out, ffn2_norm=m_ffn2_norm,
                ffn2_w_gate=m_ffn2_w_gate, ffn2_w_up=m_ffn2_w_up, ffn2_w_down=m_ffn2_w_down, final_norm=m_final_norm)
    v_in = dict(ffn1_norm=v_ffn1_norm, ffn1_w_gate=v_ffn1_w_gate, ffn1_w_up=v_ffn1_w_up, ffn1_w_down=v_ffn1_w_down,
                mix_norm=v_mix_norm, w_in=v_w_in, conv_dw_w=v_conv_dw_w, conv_dw_b=v_conv_dw_b,
                conv_ln_g=v_conv_ln_g, conv_ln_b=v_conv_ln_b, conv_w_proj=v_conv_w_proj, attn_sinks=v_attn_sinks,
                attn_w_o=v_attn_w_o, gate_b=v_gate_b, w_out=v_w_out, ffn2_norm=v_ffn2_norm,
                ffn2_w_gate=v_ffn2_w_gate, ffn2_w_up=v_ffn2_w_up, ffn2_w_down=v_ffn2_w_down, final_norm=v_final_norm)
    names = list(weights)
    big = ["ffn1_w_gate", "ffn1_w_up", "ffn1_w_down", "w_in", "conv_w_proj", "attn_w_o", "w_out",
           "ffn2_w_gate", "ffn2_w_up", "ffn2_w_down"]
    transposed = [k for k in big if k.endswith(("w_gate", "w_up"))]
    for k in transposed:
        weights[k], m_in[k], v_in[k] = (jnp.swapaxes(a, 1, 2) for a in (weights[k], m_in[k], v_in[k]))

    xs = x[0]
    T, D = xs.shape
    KV = (w_in.shape[2] * N_CHIPS - 5 * D) // 2
    n_heads = D // HEAD_DIM
    my_chip = 2 * lax.axis_index("x") + lax.axis_index("y")
    place = jnp.stack([my_chip, lax.axis_index("c")]).astype(jnp.int32)

    first, mixer_w, second = big[:3], big[3:7], big[7:]
    placed = {}
    for group in (first, mixer_w, second):
        for ks in _by_shape([weights[k][0] for k in group]):
            same = [group[k] for k in ks]
            placed.update(zip(same, place_shards(place, [weights[k][0] for k in same], BF16, f"place_{same[0]}")))
    placed_dw, = place_shards(place, [conv_dw_w[0]], F32, "place_conv_dw_w")
    wg1, wu1, wd1 = run_side(gather_side([placed[k] for k in first], []), "gather_ffn1")
    x1, h1, g1, u1, *gathered = ffn_fwd(x[0], ffn1_norm, wg1, wu1, wd1, "ffn1_fwd",
                                        side=gather_side([placed[k] for k in mixer_w], [placed_dw]))
    full = dict(zip(mixer_w + ["conv_dw_w"], gathered))
    w_glu = _cols_from_pieces(full["w_in"], 0, 2 * D)
    w_qkv = _cols_from_pieces(full["w_in"], 2 * D, 3 * D + 2 * KV)
    w_gate = _cols_from_pieces(full["w_in"], 3 * D + 2 * KV, 5 * D + 2 * KV)
    w_proj = full["conv_w_proj"].reshape(D, D)
    w_o = full["attn_w_o"].reshape(D, D)
    w_out_f = full["w_out"].reshape(D, D)
    dw_w = full["conv_dw_w"].transpose(1, 0, 2).reshape(CONV_WIDTH, D)
    dw_w = jnp.concatenate([dw_w, jnp.zeros((CONV_HALO - CONV_WIDTH, D), F32)], axis=0)
    cs, sn = _rope_tables(positions[0])
    fn_row = final_norm.reshape(1, D)

    h2, p_glu, p_gate, qr, kr, vb = mix_in_fwd(x1, mix_norm, w_glu, w_qkv, w_gate, cs, sn, "mix_in_fwd")
    c1, c3 = conv_fwd(p_glu, dw_w, conv_dw_b, conv_ln_g, conv_ln_b, "conv_fwd")
    o, wg2, wu2, wd2 = attn_fwd(qr, kr, vb, attn_sinks, "attn_fwd",
                                side=gather_side([placed[k] for k in second], []))
    x2, conv_out, attn_out, merged = merge_fwd(x1, c3, o, p_gate, gate_b, w_proj, w_o, w_out_f, "merge_fwd")
    x3, h3, g2, u2 = ffn_fwd(x2, ffn2_norm, wg2, wu2, wd2, "ffn2_fwd")

    dx3, head_sums = loss_head(x3, fn_row, loss_target[0], "loss_head")
    dx2, dwg2, dwu2, dwd2, d_ffn2_norm = ffn_bwd(x2, ffn2_norm, h3, g2, u2, wg2, wu2, wd2, dx3, "ffn2_bwd")
    ffn2_grads = [dwg2, dwu2, dwd2]
    d_gates, d_conv_out, d_attn_out, d_o, dc1, merge_sums, *sibs_f2 = merge_bwd(
        dx2, p_gate, gate_b, conv_out, attn_out, c1, conv_ln_g, conv_ln_b, w_proj, w_o, w_out_f, "merge_bwd",
        side=exchange_siblings_side(ffn2_grads))
    d_w_out, d_w_proj, d_w_o = matmul_tn([(merged, dx2), (c3, d_conv_out), (o, d_attn_out)], TK_TN, "d_w_square")
    wires_f2, owns_f2 = rs_to_wires(place, ffn2_grads, "ffn2", sibs=sibs_f2)
    d_glu, d_dw_w, *gots_f2 = conv_bwd(p_glu, dc1, dw_w, "conv_bwd", side=exchange_chips_side(wires_f2))
    dwc = D // N_CHIPS
    square_grads = [d_w_proj.reshape(N_CHIPS, dwc, D), d_w_o.reshape(N_CHIPS, dwc, D),
                    d_w_out.reshape(N_CHIPS, dwc, D)]
    dq, dk, dv, d_sinks, *sibs_sq = attn_bwd(qr, kr, vb, o, d_o, attn_sinks, "attn_bwd",
                                             side=exchange_siblings_side(square_grads))
    wires_sq, owns_sq = rs_to_wires(place, square_grads, "square", sibs=sibs_sq)
    d_qkv, *gots_sq = rope_bwd(dq, dk, dv, cs, sn, "rope_bwd", side=exchange_chips_side(wires_sq))
    d_w_in = _pieces_from_groups(matmul_tn([(h2, d_glu), (h2, d_qkv), (h2, d_gates)], TK_TN // 2, "d_w_in"))
    dx1, d_mix_norm, *sib_w_in = mix_in_bwd([d_glu, d_qkv, d_gates], [w_glu, w_qkv, w_gate], x1, mix_norm, dx2,
                                            "mix_in_bwd", side=exchange_siblings_side([d_w_in]))
    wires_in, owns_in = rs_to_wires(place, [d_w_in], "w_in", sibs=sib_w_in)
    in_send, in_recv, wires_in, lands_in, in_token = exchange_chips_start(wires_in, "rs_exchange_chips_w_in_start")
    dx0, dwg1, dwu1, dwd1, d_ffn1_norm = ffn_bwd(xs, ffn1_norm, h1, g1, u1, wg1, wu1, wd1, dx1, "ffn1_bwd",
                                                 pin=in_token)
    gots_in = exchange_chips_wait(in_send, in_recv, wires_in, lands_in, [dwd1], "rs_exchange_chips_w_in_wait")
    owns_m, gots_m = owns_in + owns_sq, list(gots_in) + list(gots_sq)
    wires_l, owns_l = rs_to_wires(place, [dwg1, dwu1, dwd1], "ffn1")
    send_sems, recv_sems, wires_l, lands_l, token = exchange_chips_start(wires_l, "rs_exchange_chips_ffn1_start")
    early_names = ["ffn2_w_gate", "ffn2_w_up", "ffn2_w_down", "w_in", "conv_w_proj", "attn_w_o", "w_out"]
    late_names = ["ffn1_w_gate", "ffn1_w_up", "ffn1_w_down"]
    reduced_early = rs_finish(place, owns_f2 + owns_m, list(gots_f2) + list(gots_m), token, "early")

    pad_row = lambda v: jnp.pad(v, ((0, 0), (0, D - v.shape[1])))
    small_rows = jnp.concatenate([
        d_ffn1_norm, d_mix_norm, merge_sums[2:3, :D], merge_sums[1:2, :D], merge_sums[1:2, D:],
        pad_row(d_sinks[0:1, :n_heads]), merge_sums[0:1, :D], merge_sums[0:1, D:], d_ffn2_norm,
        head_sums[0:1], head_sums[1:2], jnp.zeros((5, D), F32), d_dw_w], axis=0)
    small = allreduce_small(small_rows)
    loss = small[10, 0]
    grads = {"ffn1_norm": small[0:1], "mix_norm": small[1:2], "conv_dw_b": small[2:3], "conv_ln_g": small[3:4],
             "conv_ln_b": small[4:5], "attn_sinks": small[5:6, :n_heads],
             "gate_b": jnp.concatenate([small[6:7], small[7:8]], axis=1), "ffn2_norm": small[8:9],
             "final_norm": small[9:10]}
    grads["conv_dw_w"] = lax.dynamic_slice(small[16:16 + CONV_WIDTH], (0, my_chip * dwc), (CONV_WIDTH, dwc))
    grads.update(zip(early_names, reduced_early))

    deltas, new_m, new_v = {}, {}, {}

    def apply_adamw(ks):
        flat = lambda a, k: a.reshape(-1, weights[k].shape[-1])
        done = {}
        for idx in _by_shape([flat(grads[k], k) for k in ks]):
            same = [ks[i] for i in idx]
            results = adamw([flat(weights[k], k) for k in same], [flat(grads[k], k) for k in same],
                            [flat(m_in[k], k) for k in same], [flat(v_in[k], k) for k in same], f"adamw_{same[0]}")
            for k, (d, mn, vn) in zip(same, results):
                shape = weights[k].shape
                grads[k] = grads[k].reshape(shape)
                deltas[k], new_m[k], new_v[k] = d.reshape(shape), mn.reshape(shape), vn.reshape(shape)
                done[k] = d
        return done

    done = apply_adamw([k for k in names if k not in late_names])
    gots_l = exchange_chips_wait(send_sems, recv_sems, wires_l, lands_l, [done[k] for k in early_names],
                                 "rs_exchange_chips_ffn1_wait")
    grads.update(zip(late_names, rs_finish(place, owns_l, gots_l, token, "late")))
    apply_adamw(late_names)
    for k in transposed:
        for group in (grads, deltas, new_m, new_v):
            group[k] = jnp.swapaxes(group[k], 1, 2)

    return (loss, dx0[None], *[grads[k] for k in names], *[deltas[k] for k in names],
            *[new_m[k] for k in names], *[new_v[k] for k in names])
```
